```python
import jax, jax.numpy as jnp
from jax import lax
import numpy as np

D_MODEL = 1024
BATCH = 8
SEQ = 4096
DEPTH = 4

N_META = 16
EPS = 1e-6
N_BRANCH = 4
POOL_WINDOWS = (2, 4, 8, 16)
POOL_GROUP = 64
POOL_W = POOL_GROUP * 4
MLA_HEADS = 8
QK_NOPE = 64
QK_ROPE = 32
V_DIM = 64
Q_RANK = 256
KV_RANK = 128
ROPE_THETA = 10000.0
MLA_W = MLA_HEADS * V_DIM
Q_BLOCK = 128
CONF_W = 256
CONF_K = 31
SC_W = 256
SC_K = 3

IN_SPLITS = (POOL_W, POOL_W,
             Q_RANK, KV_RANK, QK_ROPE, MLA_W,
             2 * CONF_W, CONF_W,
             3 * SC_W, SC_W,
             N_BRANCH * D_MODEL)
IN_W = sum(IN_SPLITS)

kernel_name = "hybrid_parallel_gated_mixers"


def rms_norm(x, g):
    xf = x.astype(jnp.float32)
    y = xf * lax.rsqrt(jnp.mean(xf * xf, axis=-1, keepdims=True) + EPS)
    return (y * g.astype(jnp.float32)).astype(x.dtype)


def layer_norm(x, g, b):
    xf = x.astype(jnp.float32)
    mu = jnp.mean(xf, axis=-1, keepdims=True)
    var = jnp.mean(jnp.square(xf - mu), axis=-1, keepdims=True)
    y = (xf - mu) * lax.rsqrt(var + EPS)
    return (y * g.astype(jnp.float32) + b.astype(jnp.float32)).astype(x.dtype)


def split_cols(z):
    idx = [int(i) for i in np.cumsum(IN_SPLITS)[:-1]]
    return jnp.split(z, idx, axis=-1)


def causal_dwconv(u, w):
    width, c = w.shape
    up = jnp.pad(u, ((0, 0), (width - 1, 0), (0, 0)))
    return lax.conv_general_dilated(up, w[:, None, :].astype(u.dtype), window_strides=(1,),
                                    padding='VALID', dimension_numbers=('NWC', 'WIO', 'NWC'),
                                    feature_group_count=c)


def rope_tables(n_pos, dim, dtype):
    inv = 1.0 / (ROPE_THETA ** (jnp.arange(0, dim, 2, dtype=jnp.float32) / dim))
    ang = jnp.arange(n_pos, dtype=jnp.float32)[:, None] * inv[None, :]
    return jnp.cos(ang).astype(dtype), jnp.sin(ang).astype(dtype)


def apply_rope(t, cos, sin):
    t1, t2 = jnp.split(t, 2, axis=-1)
    return jnp.concatenate([t1 * cos - t2 * sin, t1 * sin + t2 * cos], axis=-1)


def pool_mixer(v, w_group, scale):
    b_, l_, _ = v.shape
    vf = v.astype(jnp.float32)
    groups = jnp.split(vf, len(POOL_WINDOWS), axis=-1)
    pos = jnp.arange(l_)
    outs = []
    for g, w in zip(groups, POOL_WINDOWS):
        cs = jnp.cumsum(g, axis=1)
        lag = jnp.pad(cs, ((0, 0), (w, 0), (0, 0)))[:, :l_]
        cnt = jnp.minimum(pos + 1, w).astype(jnp.float32)[None, :, None]
        outs.append((cs - lag) / cnt - g)
    p = jnp.stack(outs, axis=2).astype(v.dtype)
    y = jnp.einsum('blgc,gcd->blgd', p, w_group).reshape(b_, l_, POOL_W)
    return y * scale


def mla_attention(c_q, c_kv, k_rope, q_norm_g, w_uq, kv_norm_g, w_ukv, cos, sin):
    b_, l_, _ = c_q.shape
    q = (rms_norm(c_q, q_norm_g) @ w_uq).reshape(b_, l_, MLA_HEADS, QK_NOPE + QK_ROPE)
    q_nope, q_rope = jnp.split(q, [QK_NOPE], axis=-1)
    q_rope = apply_rope(q_rope, cos[:, None, :], sin[:, None, :])
    kv = (rms_norm(c_kv, kv_norm_g) @ w_ukv).reshape(b_, l_, MLA_HEADS, QK_NOPE + V_DIM)
    k_nope, v = jnp.split(kv, [QK_NOPE], axis=-1)
    k_rope = apply_rope(k_rope, cos, sin)
    k = jnp.concatenate([k_nope, jnp.broadcast_to(k_rope[:, :, None, :], (b_, l_, MLA_HEADS, QK_ROPE))], axis=-1)
    qf = jnp.concatenate([q_nope, q_rope], axis=-1) * ((QK_NOPE + QK_ROPE) ** -0.5)
    n_blk = -(-l_ // Q_BLOCK)
    lp = n_blk * Q_BLOCK
    pad = ((0, 0), (0, lp - l_), (0, 0), (0, 0))
    qf, k, v = jnp.pad(qf, pad), jnp.pad(k, pad), jnp.pad(v, pad)
    k_pos = jnp.arange(lp)
    q_blocks = qf.reshape(b_, n_blk, Q_BLOCK, MLA_HEADS, QK_NOPE + QK_ROPE).transpose(1, 0, 2, 3, 4)

    def attend(args):
        qb, i = args
        s = jnp.einsum('bqhd,bkhd->bhqk', qb, k).astype(jnp.float32)
        q_pos = i * Q_BLOCK + jnp.arange(Q_BLOCK)
        s = jnp.where(k_pos[None, :] <= q_pos[:, None], s, -jnp.inf)
        p = jax.nn.softmax(s, axis=-1).astype(v.dtype)
        return jnp.einsum('bhqk,bkhd->bqhd', p, v)

    o = lax.map(attend, (q_blocks, jnp.arange(n_blk)))
    return o.transpose(1, 0, 2, 3, 4).reshape(b_, lp, MLA_W)[:, :l_]


def conformer_conv(u, w_dw, b_dw, ln_g, ln_b):
    a, gate = jnp.split(u, 2, axis=-1)
    z = a * jax.nn.sigmoid(gate)
    z = causal_dwconv(z, w_dw) + b_dw
    z = layer_norm(z, ln_g, ln_b)
    return jax.nn.silu(z)


def short_conv(bcx, w_dw):
    bg, cg, xv = jnp.split(bcx, 3, axis=-1)
    return bg * causal_dwconv(cg * xv, w_dw)


def _fwd_setup_inputs(seed: int = 0) -> dict:
    key = jax.random.key(seed)
    ks = jax.random.split(key, 24)
    f32 = jnp.float32

    def nrm(k, shape, fan_in):
        return jax.random.normal(k, shape, f32) * (fan_in ** -0.5)

    def gain(k, shape):
        return 1.0 + 0.05 * jax.random.normal(k, shape, f32)

    def bias(k, shape):
        return 0.02 * jax.random.normal(k, shape, f32)

    return {
        "x": jax.random.normal(ks[0], (BATCH, SEQ, D_MODEL), f32),
        "meta_tokens": jax.random.normal(ks[1], (N_META, D_MODEL), f32),
        "pre_norm_g": gain(ks[2], (DEPTH, D_MODEL)),
        "w_in": nrm(ks[3], (DEPTH, D_MODEL, IN_W), D_MODEL),
        "gate_bias": bias(ks[4], (DEPTH, N_BRANCH * D_MODEL)),
        "pool_w": nrm(ks[5], (DEPTH, 4, POOL_GROUP, POOL_GROUP), POOL_GROUP),
        "pool_scale": gain(ks[6], (DEPTH, POOL_W)),
        "w_out_pool": nrm(ks[7], (DEPTH, POOL_W, D_MODEL), POOL_W),
        "q_norm_g": gain(ks[8], (DEPTH, Q_RANK)),
        "w_uq": nrm(ks[9], (DEPTH, Q_RANK, MLA_HEADS * (QK_NOPE + QK_ROPE)), Q_RANK),
        "kv_norm_g": gain(ks[10], (DEPTH, KV_RANK)),
        "w_ukv": nrm(ks[11], (DEPTH, KV_RANK, MLA_HEADS * (QK_NOPE + V_DIM)), KV_RANK),
        "w_out_mla": nrm(ks[12], (DEPTH, MLA_W, D_MODEL), MLA_W),
        "conf_dw_w": nrm(ks[13], (DEPTH, CONF_K, CONF_W), CONF_K),
        "conf_dw_b": bias(ks[14], (DEPTH, CONF_W)),
        "conf_ln_g": gain(ks[15], (DEPTH, CONF_W)),
        "conf_ln_b": bias(ks[16], (DEPTH, CONF_W)),
        "w_out_conf": nrm(ks[17], (DEPTH, CONF_W, D_MODEL), CONF_W),
        "sc_dw_w": nrm(ks[18], (DEPTH, SC_K, SC_W), SC_K),
        "w_out_sc": nrm(ks[19], (DEPTH, SC_W, D_MODEL), SC_W),
        "w_o": nrm(ks[20], (DEPTH, D_MODEL, D_MODEL), D_MODEL),
        "post_norm_g": gain(ks[21], (DEPTH, D_MODEL)),
    }


def _fwd_reference(x, meta_tokens, pre_norm_g, w_in, gate_bias, pool_w, pool_scale, w_out_pool,
              q_norm_g, w_uq, kv_norm_g, w_ukv, w_out_mla, conf_dw_w, conf_dw_b, conf_ln_g,
              conf_ln_b, w_out_conf, sc_dw_w, w_out_sc, w_o, post_norm_g):
    b_ = x.shape[0]
    meta = jnp.broadcast_to(meta_tokens[None].astype(x.dtype), (b_, N_META, D_MODEL))
    h_res = jnp.concatenate([meta, x], axis=1)
    l_ = h_res.shape[1]
    cos, sin = rope_tables(l_, QK_ROPE, x.dtype)

    for i in range(DEPTH):
        h = rms_norm(h_res, pre_norm_g[i])
        z = h @ w_in[i]
        (pv, pg, cq, ckv, kr, mg, cu, cg, sbcx, sg, gl) = split_cols(z)

        y_a = (pool_mixer(pv, pool_w[i], pool_scale[i]) * jax.nn.silu(pg)) @ w_out_pool[i]
        y_b = (mla_attention(cq, ckv, kr, q_norm_g[i], w_uq[i], kv_norm_g[i], w_ukv[i], cos, sin)
               * jax.nn.silu(mg)) @ w_out_mla[i]
        y_c = (conformer_conv(cu, conf_dw_w[i], conf_dw_b[i], conf_ln_g[i], conf_ln_b[i])
               * jax.nn.silu(cg)) @ w_out_conf[i]
        y_d = (short_conv(sbcx, sc_dw_w[i]) * jax.nn.silu(sg)) @ w_out_sc[i]

        gts = jax.nn.sigmoid(gl + gate_bias[i]).reshape(b_, l_, N_BRANCH, D_MODEL)
        m = (gts[:, :, 0] * y_a + gts[:, :, 1] * y_b + gts[:, :, 2] * y_c + gts[:, :, 3] * y_d)
        h_res = h_res + rms_norm(m @ w_o[i], post_norm_g[i])

    return h_res[:, N_META:]


import jax as _jax
import jax.numpy as _jnp

TWIN_FORMAT = 'train_step'
FWD_PARAMS = ['x', 'meta_tokens', 'pre_norm_g', 'w_in', 'gate_bias', 'pool_w', 'pool_scale', 'w_out_pool', 'q_norm_g', 'w_uq', 'kv_norm_g', 'w_ukv', 'w_out_mla', 'conf_dw_w', 'conf_dw_b', 'conf_ln_g', 'conf_ln_b', 'w_out_conf', 'sc_dw_w', 'w_out_sc', 'w_o', 'post_norm_g']
TWIN_WEIGHTS = ['meta_tokens', 'pre_norm_g', 'w_in', 'gate_bias', 'pool_w', 'pool_scale', 'w_out_pool', 'q_norm_g', 'w_uq', 'kv_norm_g', 'w_ukv', 'w_out_mla', 'conf_dw_w', 'conf_dw_b', 'conf_ln_g', 'conf_ln_b', 'w_out_conf', 'sc_dw_w', 'w_out_sc', 'w_o', 'post_norm_g']
TWIN_DIFF_INPUT = 'x'
TWIN_INPUTS = ['x', 'meta_tokens', 'pre_norm_g', 'w_in', 'gate_bias', 'pool_w', 'pool_scale', 'w_out_pool', 'q_norm_g', 'w_uq', 'kv_norm_g', 'w_ukv', 'w_out_mla', 'conf_dw_w', 'conf_dw_b', 'conf_ln_g', 'conf_ln_b', 'w_out_conf', 'sc_dw_w', 'w_out_sc', 'w_o', 'post_norm_g', 'loss_target', 'm_meta_tokens', 'm_pre_norm_g', 'm_w_in', 'm_gate_bias', 'm_pool_w', 'm_pool_scale', 'm_w_out_pool', 'm_q_norm_g', 'm_w_uq', 'm_kv_norm_g', 'm_w_ukv', 'm_w_out_mla', 'm_conf_dw_w', 'm_conf_dw_b', 'm_conf_ln_g', 'm_conf_ln_b', 'm_w_out_conf', 'm_sc_dw_w', 'm_w_out_sc', 'm_w_o', 'm_post_norm_g', 'v_meta_tokens', 'v_pre_norm_g', 'v_w_in', 'v_gate_bias', 'v_pool_w', 'v_pool_scale', 'v_w_out_pool', 'v_q_norm_g', 'v_w_uq', 'v_kv_norm_g', 'v_w_ukv', 'v_w_out_mla', 'v_conf_dw_w', 'v_conf_dw_b', 'v_conf_ln_g', 'v_conf_ln_b', 'v_w_out_conf', 'v_sc_dw_w', 'v_w_out_sc', 'v_w_o', 'v_post_norm_g']
TWIN_OUTPUTS = ['loss', 'grad_x', 'grad_meta_tokens', 'grad_pre_norm_g', 'grad_w_in', 'grad_gate_bias', 'grad_pool_w', 'grad_pool_scale', 'grad_w_out_pool', 'grad_q_norm_g', 'grad_w_uq', 'grad_kv_norm_g', 'grad_w_ukv', 'grad_w_out_mla', 'grad_conf_dw_w', 'grad_conf_dw_b', 'grad_conf_ln_g', 'grad_conf_ln_b', 'grad_w_out_conf', 'grad_sc_dw_w', 'grad_w_out_sc', 'grad_w_o', 'grad_post_norm_g', 'delta_meta_tokens', 'delta_pre_norm_g', 'delta_w_in', 'delta_gate_bias', 'delta_pool_w', 'delta_pool_scale', 'delta_w_out_pool', 'delta_q_norm_g', 'delta_w_uq', 'delta_kv_norm_g', 'delta_w_ukv', 'delta_w_out_mla', 'delta_conf_dw_w', 'delta_conf_dw_b', 'delta_conf_ln_g', 'delta_conf_ln_b', 'delta_w_out_conf', 'delta_sc_dw_w', 'delta_w_out_sc', 'delta_w_o', 'delta_post_norm_g', 'new_m_meta_tokens', 'new_m_pre_norm_g', 'new_m_w_in', 'new_m_gate_bias', 'new_m_pool_w', 'new_m_pool_scale', 'new_m_w_out_pool', 'new_m_q_norm_g', 'new_m_w_uq', 'new_m_kv_norm_g', 'new_m_w_ukv', 'new_m_w_out_mla', 'new_m_conf_dw_w', 'new_m_conf_dw_b', 'new_m_conf_ln_g', 'new_m_conf_ln_b', 'new_m_w_out_conf', 'new_m_sc_dw_w', 'new_m_w_out_sc', 'new_m_w_o', 'new_m_post_norm_g', 'new_v_meta_tokens', 'new_v_pre_norm_g', 'new_v_w_in', 'new_v_gate_bias', 'new_v_pool_w', 'new_v_pool_scale', 'new_v_w_out_pool', 'new_v_q_norm_g', 'new_v_w_uq', 'new_v_kv_norm_g', 'new_v_w_ukv', 'new_v_w_out_mla', 'new_v_conf_dw_w', 'new_v_conf_dw_b', 'new_v_conf_ln_g', 'new_v_conf_ln_b', 'new_v_w_out_conf', 'new_v_sc_dw_w', 'new_v_w_out_sc', 'new_v_w_o', 'new_v_post_norm_g']
TWIN_LEAF_KINDS = {'loss': 'loss', 'grad_x': 'grad_x', 'grad_meta_tokens': 'grad_w', 'grad_pre_norm_g': 'grad_w', 'grad_w_in': 'grad_w', 'grad_gate_bias': 'grad_w', 'grad_pool_w': 'grad_w', 'grad_pool_scale': 'grad_w', 'grad_w_out_pool': 'grad_w', 'grad_q_norm_g': 'grad_w', 'grad_w_uq': 'grad_w', 'grad_kv_norm_g': 'grad_w', 'grad_w_ukv': 'grad_w', 'grad_w_out_mla': 'grad_w', 'grad_conf_dw_w': 'grad_w', 'grad_conf_dw_b': 'grad_w', 'grad_conf_ln_g': 'grad_w', 'grad_conf_ln_b': 'grad_w', 'grad_w_out_conf': 'grad_w', 'grad_sc_dw_w': 'grad_w', 'grad_w_out_sc': 'grad_w', 'grad_w_o': 'grad_w', 'grad_post_norm_g': 'grad_w', 'delta_meta_tokens': 'delta_w', 'delta_pre_norm_g': 'delta_w', 'delta_w_in': 'delta_w', 'delta_gate_bias': 'delta_w', 'delta_pool_w': 'delta_w', 'delta_pool_scale': 'delta_w', 'delta_w_out_pool': 'delta_w', 'delta_q_norm_g': 'delta_w', 'delta_w_uq': 'delta_w', 'delta_kv_norm_g': 'delta_w', 'delta_w_ukv': 'delta_w', 'delta_w_out_mla': 'delta_w', 'delta_conf_dw_w': 'delta_w', 'delta_conf_dw_b': 'delta_w', 'delta_conf_ln_g': 'delta_w', 'delta_conf_ln_b': 'delta_w', 'delta_w_out_conf': 'delta_w', 'delta_sc_dw_w': 'delta_w', 'delta_w_out_sc': 'delta_w', 'delta_w_o': 'delta_w', 'delta_post_norm_g': 'delta_w', 'new_m_meta_tokens': 'new_m', 'new_m_pre_norm_g': 'new_m', 'new_m_w_in': 'new_m', 'new_m_gate_bias': 'new_m', 'new_m_pool_w': 'new_m', 'new_m_pool_scale': 'new_m', 'new_m_w_out_pool': 'new_m', 'new_m_q_norm_g': 'new_m', 'new_m_w_uq': 'new_m', 'new_m_kv_norm_g': 'new_m', 'new_m_w_ukv': 'new_m', 'new_m_w_out_mla': 'new_m', 'new_m_conf_dw_w': 'new_m', 'new_m_conf_dw_b': 'new_m', 'new_m_conf_ln_g': 'new_m', 'new_m_conf_ln_b': 'new_m', 'new_m_w_out_conf': 'new_m', 'new_m_sc_dw_w': 'new_m', 'new_m_w_out_sc': 'new_m', 'new_m_w_o': 'new_m', 'new_m_post_norm_g': 'new_m', 'new_v_meta_tokens': 'new_v', 'new_v_pre_norm_g': 'new_v', 'new_v_w_in': 'new_v', 'new_v_gate_bias': 'new_v', 'new_v_pool_w': 'new_v', 'new_v_pool_scale': 'new_v', 'new_v_w_out_pool': 'new_v', 'new_v_q_norm_g': 'new_v', 'new_v_w_uq': 'new_v', 'new_v_kv_norm_g': 'new_v', 'new_v_w_ukv': 'new_v', 'new_v_w_out_mla': 'new_v', 'new_v_conf_dw_w': 'new_v', 'new_v_conf_dw_b': 'new_v', 'new_v_conf_ln_g': 'new_v', 'new_v_conf_ln_b': 'new_v', 'new_v_w_out_conf': 'new_v', 'new_v_sc_dw_w': 'new_v', 'new_v_w_out_sc': 'new_v', 'new_v_w_o': 'new_v', 'new_v_post_norm_g': 'new_v'}


def _forward(args):
    return _fwd_reference(*[args[k] for k in FWD_PARAMS])


def _output_shape():
    out = _jax.eval_shape(lambda: _forward(_fwd_setup_inputs(0)))
    return out.shape, out.dtype

N_MICROBATCH = 1
ADAM_LR = 0.001
ADAM_B1 = 0.9
ADAM_B2 = 0.999
ADAM_EPS = 1e-08
ADAM_WD = 0.01
ADAM_STEP = 10
PER_EXAMPLE_BATCH_AXIS = {'x': 0, 'loss_target': 0}
SHARED_INPUTS = []
_WEIGHT_DTYPES = {'meta_tokens': _jnp.float32, 'pre_norm_g': _jnp.float32, 'w_in': _jnp.float32, 'gate_bias': _jnp.float32, 'pool_w': _jnp.float32, 'pool_scale': _jnp.float32, 'w_out_pool': _jnp.float32, 'q_norm_g': _jnp.float32, 'w_uq': _jnp.float32, 'kv_norm_g': _jnp.float32, 'w_ukv': _jnp.float32, 'w_out_mla': _jnp.float32, 'conf_dw_w': _jnp.float32, 'conf_dw_b': _jnp.float32, 'conf_ln_g': _jnp.float32, 'conf_ln_b': _jnp.float32, 'w_out_conf': _jnp.float32, 'sc_dw_w': _jnp.float32, 'w_out_sc': _jnp.float32, 'w_o': _jnp.float32, 'post_norm_g': _jnp.float32}
MOMENT_SCALE = {'meta_tokens': 8.830403e-02, 'pre_norm_g': 1.499461e+00, 'w_in': 5.506707e-01, 'gate_bias': 1.652718e-01, 'pool_w': 1.083670e+00, 'pool_scale': 1.114274e+00, 'w_out_pool': 5.438766e-01, 'q_norm_g': 2.014198e-01, 'w_uq': 1.155926e-01, 'kv_norm_g': 4.195412e-01, 'w_ukv': 1.394296e-01, 'w_out_mla': 1.105036e-01, 'conf_dw_w': 7.153837e-01, 'conf_dw_b': 1.935349e+00, 'conf_ln_g': 9.756043e-01, 'conf_ln_b': 1.049958e+00, 'w_out_conf': 3.768724e-01, 'sc_dw_w': 1.106790e+00, 'w_out_sc': 5.367885e-01, 'w_o': 8.595305e-01, 'post_norm_g': 3.185873e+01}


def _to_microbatches(a, axis):
    t = _jnp.moveaxis(a, axis, 0)
    t = t.reshape((N_MICROBATCH, t.shape[0] // N_MICROBATCH) + t.shape[1:])
    return _jnp.moveaxis(t, 1, axis + 1)


def setup_inputs(seed: int = 0) -> dict:
    inp = _fwd_setup_inputs(seed)
    key = _jax.random.fold_in(_jax.random.key(seed), 7919)
    shape, _ = _output_shape()
    out = dict(inp)
    out["loss_target"] = _jax.random.normal(_jax.random.fold_in(key, 0), shape, _jnp.float32)
    for i, name in enumerate(TWIN_WEIGHTS):
        w = inp[name].astype(_jnp.float32)
        if MOMENT_SCALE is None:
            s = _jnp.sqrt(_jnp.mean(_jnp.square(w)) + 1e-30)
        else:
            s = MOMENT_SCALE[name]
        km, kv = _jax.random.split(_jax.random.fold_in(key, i + 1))
        out[name] = w
        out["m_" + name] = s * _jax.random.normal(km, w.shape, _jnp.float32)
        out["v_" + name] = (s * s) * _jax.random.uniform(kv, w.shape, _jnp.float32, 0.5, 1.5)
    if N_MICROBATCH > 1:
        for name, axis in PER_EXAMPLE_BATCH_AXIS.items():
            out[name] = _to_microbatches(out[name], axis)
    return {'x': out['x'], 'meta_tokens': out['meta_tokens'], 'pre_norm_g': out['pre_norm_g'], 'w_in': out['w_in'], 'gate_bias': out['gate_bias'], 'pool_w': out['pool_w'], 'pool_scale': out['pool_scale'], 'w_out_pool': out['w_out_pool'], 'q_norm_g': out['q_norm_g'], 'w_uq': out['w_uq'], 'kv_norm_g': out['kv_norm_g'], 'w_ukv': out['w_ukv'], 'w_out_mla': out['w_out_mla'], 'conf_dw_w': out['conf_dw_w'], 'conf_dw_b': out['conf_dw_b'], 'conf_ln_g': out['conf_ln_g'], 'conf_ln_b': out['conf_ln_b'], 'w_out_conf': out['w_out_conf'], 'sc_dw_w': out['sc_dw_w'], 'w_out_sc': out['w_out_sc'], 'w_o': out['w_o'], 'post_norm_g': out['post_norm_g'], 'loss_target': out['loss_target'], 'm_meta_tokens': out['m_meta_tokens'], 'm_pre_norm_g': out['m_pre_norm_g'], 'm_w_in': out['m_w_in'], 'm_gate_bias': out['m_gate_bias'], 'm_pool_w': out['m_pool_w'], 'm_pool_scale': out['m_pool_scale'], 'm_w_out_pool': out['m_w_out_pool'], 'm_q_norm_g': out['m_q_norm_g'], 'm_w_uq': out['m_w_uq'], 'm_kv_norm_g': out['m_kv_norm_g'], 'm_w_ukv': out['m_w_ukv'], 'm_w_out_mla': out['m_w_out_mla'], 'm_conf_dw_w': out['m_conf_dw_w'], 'm_conf_dw_b': out['m_conf_dw_b'], 'm_conf_ln_g': out['m_conf_ln_g'], 'm_conf_ln_b': out['m_conf_ln_b'], 'm_w_out_conf': out['m_w_out_conf'], 'm_sc_dw_w': out['m_sc_dw_w'], 'm_w_out_sc': out['m_w_out_sc'], 'm_w_o': out['m_w_o'], 'm_post_norm_g': out['m_post_norm_g'], 'v_meta_tokens': out['v_meta_tokens'], 'v_pre_norm_g': out['v_pre_norm_g'], 'v_w_in': out['v_w_in'], 'v_gate_bias': out['v_gate_bias'], 'v_pool_w': out['v_pool_w'], 'v_pool_scale': out['v_pool_scale'], 'v_w_out_pool': out['v_w_out_pool'], 'v_q_norm_g': out['v_q_norm_g'], 'v_w_uq': out['v_w_uq'], 'v_kv_norm_g': out['v_kv_norm_g'], 'v_w_ukv': out['v_w_ukv'], 'v_w_out_mla': out['v_w_out_mla'], 'v_conf_dw_w': out['v_conf_dw_w'], 'v_conf_dw_b': out['v_conf_dw_b'], 'v_conf_ln_g': out['v_conf_ln_g'], 'v_conf_ln_b': out['v_conf_ln_b'], 'v_w_out_conf': out['v_w_out_conf'], 'v_sc_dw_w': out['v_sc_dw_w'], 'v_w_out_sc': out['v_w_out_sc'], 'v_w_o': out['v_w_o'], 'v_post_norm_g': out['v_post_norm_g']}


def _loss(weights, diff, rest, loss_target):
    with _jax.named_scope("forward"):
        args = {**rest, TWIN_DIFF_INPUT: diff, **{k: w.astype(_WEIGHT_DTYPES[k]) for k, w in weights.items()}}
        y = _forward(args)
    with _jax.named_scope("loss_head"):
        err = _jnp.square(y.astype(_jnp.float32) - loss_target)
        return 0.5 * _jnp.sum(_jnp.mean(err, axis=-1)) if err.ndim else 0.5 * err


def _adamw(w, g, m, v):
    m = ADAM_B1 * m + (1.0 - ADAM_B1) * g
    v = ADAM_B2 * v + (1.0 - ADAM_B2) * _jnp.square(g)
    m_hat = m / (1.0 - ADAM_B1 ** ADAM_STEP)
    v_hat = v / (1.0 - ADAM_B2 ** ADAM_STEP)
    delta = -ADAM_LR * (m_hat / (_jnp.sqrt(v_hat) + ADAM_EPS) + ADAM_WD * w)
    return delta, m, v


def reference(x, meta_tokens, pre_norm_g, w_in, gate_bias, pool_w, pool_scale, w_out_pool, q_norm_g, w_uq, kv_norm_g, w_ukv, w_out_mla, conf_dw_w, conf_dw_b, conf_ln_g, conf_ln_b, w_out_conf, sc_dw_w, w_out_sc, w_o, post_norm_g, loss_target, m_meta_tokens, m_pre_norm_g, m_w_in, m_gate_bias, m_pool_w, m_pool_scale, m_w_out_pool, m_q_norm_g, m_w_uq, m_kv_norm_g, m_w_ukv, m_w_out_mla, m_conf_dw_w, m_conf_dw_b, m_conf_ln_g, m_conf_ln_b, m_w_out_conf, m_sc_dw_w, m_w_out_sc, m_w_o, m_post_norm_g, v_meta_tokens, v_pre_norm_g, v_w_in, v_gate_bias, v_pool_w, v_pool_scale, v_w_out_pool, v_q_norm_g, v_w_uq, v_kv_norm_g, v_w_ukv, v_w_out_mla, v_conf_dw_w, v_conf_dw_b, v_conf_ln_g, v_conf_ln_b, v_w_out_conf, v_sc_dw_w, v_w_out_sc, v_w_o, v_post_norm_g):
    given = dict(x=x, meta_tokens=meta_tokens, pre_norm_g=pre_norm_g, w_in=w_in, gate_bias=gate_bias, pool_w=pool_w, pool_scale=pool_scale, w_out_pool=w_out_pool, q_norm_g=q_norm_g, w_uq=w_uq, kv_norm_g=kv_norm_g, w_ukv=w_ukv, w_out_mla=w_out_mla, conf_dw_w=conf_dw_w, conf_dw_b=conf_dw_b, conf_ln_g=conf_ln_g, conf_ln_b=conf_ln_b, w_out_conf=w_out_conf, sc_dw_w=sc_dw_w, w_out_sc=w_out_sc, w_o=w_o, post_norm_g=post_norm_g, loss_target=loss_target, m_meta_tokens=m_meta_tokens, m_pre_norm_g=m_pre_norm_g, m_w_in=m_w_in, m_gate_bias=m_gate_bias, m_pool_w=m_pool_w, m_pool_scale=m_pool_scale, m_w_out_pool=m_w_out_pool, m_q_norm_g=m_q_norm_g, m_w_uq=m_w_uq, m_kv_norm_g=m_kv_norm_g, m_w_ukv=m_w_ukv, m_w_out_mla=m_w_out_mla, m_conf_dw_w=m_conf_dw_w, m_conf_dw_b=m_conf_dw_b, m_conf_ln_g=m_conf_ln_g, m_conf_ln_b=m_conf_ln_b, m_w_out_conf=m_w_out_conf, m_sc_dw_w=m_sc_dw_w, m_w_out_sc=m_w_out_sc, m_w_o=m_w_o, m_post_norm_g=m_post_norm_g, v_meta_tokens=v_meta_tokens, v_pre_norm_g=v_pre_norm_g, v_w_in=v_w_in, v_gate_bias=v_gate_bias, v_pool_w=v_pool_w, v_pool_scale=v_pool_scale, v_w_out_pool=v_w_out_pool, v_q_norm_g=v_q_norm_g, v_w_uq=v_w_uq, v_kv_norm_g=v_kv_norm_g, v_w_ukv=v_w_ukv, v_w_out_mla=v_w_out_mla, v_conf_dw_w=v_conf_dw_w, v_conf_dw_b=v_conf_dw_b, v_conf_ln_g=v_conf_ln_g, v_conf_ln_b=v_conf_ln_b, v_w_out_conf=v_w_out_conf, v_sc_dw_w=v_sc_dw_w, v_w_out_sc=v_w_out_sc, v_w_o=v_w_o, v_post_norm_g=v_post_norm_g)
    weights = {n: given[n] for n in TWIN_WEIGHTS}
    shared = {n: given[n] for n in SHARED_INPUTS}
    per_example = {n: given[n] for n in ['x']}
    grad_fn = _jax.value_and_grad(_loss, argnums=(0, 1))

    def one_microbatch(ex, loss_target):
        ex = dict(ex)
        diff = ex.pop(TWIN_DIFF_INPUT)
        return grad_fn(weights, diff, {**shared, **ex}, loss_target)

    if N_MICROBATCH == 1:
        loss, (grad_w, grad_x) = one_microbatch(per_example, given["loss_target"])
    else:
        def body(carry, xs):
            loss_sum, grad_sum = carry
            l_k, (gw_k, gx_k) = one_microbatch(xs[0], xs[1])
            with _jax.named_scope("update"):
                return (loss_sum + l_k, _jax.tree.map(_jnp.add, grad_sum, gw_k)), gx_k

        init = (_jnp.zeros((), _jnp.float32), _jax.tree.map(_jnp.zeros_like, weights))
        (loss, grad_w), grad_x = _jax.lax.scan(body, init, (per_example, given["loss_target"]))
    with _jax.named_scope("update"):
        delta_w, new_m, new_v = {}, {}, {}
        for n in TWIN_WEIGHTS:
            delta_w[n], new_m[n], new_v[n] = _adamw(weights[n], grad_w[n], given["m_" + n], given["v_" + n])
    return (loss, grad_x, *[grad_w[n] for n in TWIN_WEIGHTS], *[delta_w[n] for n in TWIN_WEIGHTS],
            *[new_m[n] for n in TWIN_WEIGHTS], *[new_v[n] for n in TWIN_WEIGHTS])
```

```python
import functools

import jax
import jax.numpy as jnp
import numpy as np
from jax import lax
from jax.experimental import pallas as pl
from jax.experimental.pallas import tpu as pltpu

F32 = jnp.float32
BF16 = jnp.bfloat16

D = 1024
N_META = 16
DEPTH = 4
EPS = 1e-6
HEADS = 8
QK_NOPE = 64
QK_ROPE = 32
V_DIM = 64
Q_RANK = 256
KV_RANK = 128
ROPE_THETA = 10000.0
SCALE = (QK_NOPE + QK_ROPE) ** -0.5
CONF_K = 31
SC_K = 3
N_DEV = 8

ADAM_LR = 0.001
ADAM_B1 = 0.9
ADAM_B2 = 0.999
ADAM_EPS = 1e-08
ADAM_WD = 0.01
ADAM_STEP = 10

RB = 384
HB = 32
LANE = 128
VMEM_LIMIT = 56 * 1024 * 1024

GL0, PV0, PG0, CQ0, CKV0, KR0, MG0, CU0, CG0, SBB0, SBC0, SBX0, SG0, ZW = (
    0, 4096, 4352, 4608, 4864, 4992, 5120, 5632, 6144, 6400, 6656, 6912, 7168, 7424)

MESH = pl.DeviceIdType.MESH


def _cp(**kw):
    return pltpu.CompilerParams(vmem_limit_bytes=VMEM_LIMIT, **kw)


def _sig(x):
    return jax.nn.sigmoid(x)


def _silu(x):
    return x * _sig(x)


def _dsilu(x):
    s = _sig(x)
    return s * (1.0 + x * (1.0 - s))


def _dn(x, k):
    return x if k == 0 else pltpu.roll(x, k, 0)


def _up(x, k):
    return x if k == 0 else pltpu.roll(x, x.shape[0] - k, 0)


def _rope(t, c, s1, s2):
    return t * c + pltpu.roll(t, 16, 1) * s1 + pltpu.roll(t, LANE - 16, 1) * s2


def _rope_t(g, c, s1, s2):
    return g * c + pltpu.roll(g * s1, LANE - 16, 1) + pltpu.roll(g * s2, 16, 1)


def _mm(a, b, m, n, k, *, ta=False, tb=False, out_dtype=F32, tm, tn, tk, name,
        a_moff=0, a_koff=0, b_noff=0, b_koff=0):
    assert m % tm == 0 and n % tn == 0 and k % tk == 0, (name, m, n, k, tm, tn, tk)
    nk = k // tk
    dims = (((0,) if ta else (1,), (1,) if tb else (0,)), ((), ()))

    def body(a_ref, b_ref, o_ref, *scr):
        part = lax.dot_general(a_ref[...].astype(BF16), b_ref[...].astype(BF16), dims,
                               preferred_element_type=F32)
        if nk == 1:
            o_ref[...] = part.astype(out_dtype)
        else:
            acc = scr[0]
            kk = pl.program_id(2)

            @pl.when(kk == 0)
            def _():
                acc[...] = part

            @pl.when(kk > 0)
            def _():
                acc[...] += part

            @pl.when(kk == nk - 1)
            def _():
                o_ref[...] = acc[...].astype(out_dtype)

    if ta:
        a_spec = pl.BlockSpec((tk, tm), lambda i, j, q: (q + a_koff, i + a_moff))
    else:
        a_spec = pl.BlockSpec((tm, tk), lambda i, j, q: (i + a_moff, q + a_koff))
    if tb:
        b_spec = pl.BlockSpec((tn, tk), lambda i, j, q: (j + b_noff, q + b_koff))
    else:
        b_spec = pl.BlockSpec((tk, tn), lambda i, j, q: (q + b_koff, j + b_noff))
    return pl.pallas_call(
        body, grid=(m // tm, n // tn, nk), in_specs=[a_spec, b_spec],
        out_specs=pl.BlockSpec((tm, tn), lambda i, j, q: (i, j)),
        out_shape=jax.ShapeDtypeStruct((m, n), out_dtype),
        scratch_shapes=[pltpu.VMEM((tm, tn), F32)] if nk > 1 else [],
        name=name, compiler_params=_cp())(a, b)


def _rms_fwd(x, g, lp):
    def body(x_ref, g_ref, h_ref):
        xv = x_ref[...]
        r = lax.rsqrt(jnp.mean(xv * xv, axis=-1, keepdims=True) + EPS)
        h_ref[...] = (xv * r * g_ref[...]).astype(BF16)

    return pl.pallas_call(
        body, grid=(lp // RB,),
        in_specs=[pl.BlockSpec((RB, D), lambda i: (i, 0)), pl.BlockSpec((1, D), lambda i: (0, 0))],
        out_specs=pl.BlockSpec((RB, D), lambda i: (i, 0)),
        out_shape=jax.ShapeDtypeStruct((lp, D), BF16), name="rms_fwd", compiler_params=_cp())(x, g)


def _rms_bwd(x, g, dh, dx_in, lp):
    def body(x_ref, g_ref, dh_ref, dxi_ref, dx_ref, dg_ref):
        i = pl.program_id(0)
        xv = x_ref[...]
        r = lax.rsqrt(jnp.mean(xv * xv, axis=-1, keepdims=True) + EPS)
        dy = dh_ref[...]
        a = dy * g_ref[...]
        dx_ref[...] = dxi_ref[...] + r * a - xv * (r * r * r) * jnp.mean(a * xv, axis=-1, keepdims=True)
        part = jnp.sum(dy * xv * r, axis=0, keepdims=True)

        @pl.when(i == 0)
        def _():
            dg_ref[...] = part

        @pl.when(i > 0)
        def _():
            dg_ref[...] += part

    blk = pl.BlockSpec((RB, D), lambda i: (i, 0))
    vec = pl.BlockSpec((1, D), lambda i: (0, 0))
    return pl.pallas_call(
        body, grid=(lp // RB,), in_specs=[blk, vec, blk, blk], out_specs=[blk, vec],
        out_shape=[jax.ShapeDtypeStruct((lp, D), F32), jax.ShapeDtypeStruct((1, D), F32)],
        name="rms_bwd", compiler_params=_cp())(x, g, dh, dx_in)


def _mla_prep(z, qg, kvg, tabs, lp):
    def body(z_ref, qg_ref, kvg_ref, c_ref, s1_ref, s2_ref, qn_ref, kvn_ref, kr_ref):
        cq = z_ref[:, 0:256]
        ckv = z_ref[:, 256:384]
        kr = z_ref[:, 384:512]
        rq = lax.rsqrt(jnp.mean(cq * cq, axis=-1, keepdims=True) + EPS)
        rk = lax.rsqrt(jnp.mean(ckv * ckv, axis=-1, keepdims=True) + EPS)
        qn_ref[...] = (cq * rq * qg_ref[...]).astype(BF16)
        kvn_ref[...] = (ckv * rk * kvg_ref[...]).astype(BF16)
        kr_ref[...] = _rope(kr, c_ref[...], s1_ref[...], s2_ref[...])

    tab = pl.BlockSpec((RB, LANE), lambda i: (i, 0))
    return pl.pallas_call(
        body, grid=(lp // RB,),
        in_specs=[pl.BlockSpec((RB, 512), lambda i: (i, CQ0 // 512)),
                  pl.BlockSpec((1, 256), lambda i: (0, 0)), pl.BlockSpec((1, 128), lambda i: (0, 0)),
                  tab, tab, tab],
        out_specs=[pl.BlockSpec((RB, 256), lambda i: (i, 0)), tab, tab],
        out_shape=[jax.ShapeDtypeStruct((lp, 256), BF16), jax.ShapeDtypeStruct((lp, 128), BF16),
                   jax.ShapeDtypeStruct((lp, 128), F32)],
        name="mla_prep", compiler_params=_cp())(z, qg, kvg, *tabs)


def _mla_prep_bwd(z, qg, kvg, dqn, dkvn, dkr, lp):
    def body(z_ref, qg_ref, kvg_ref, dqn_ref, dkvn_ref, dkr_ref, dz_ref, dqg_ref, dkvg_ref):
        i = pl.program_id(0)

        def rms_b(xv, g, dy):
            r = lax.rsqrt(jnp.mean(xv * xv, axis=-1, keepdims=True) + EPS)
            a = dy * g
            dx = r * a - xv * (r * r * r) * jnp.mean(a * xv, axis=-1, keepdims=True)
            return dx, jnp.sum(dy * xv * r, axis=0, keepdims=True)

        dcq, pq = rms_b(z_ref[:, 0:256], qg_ref[...], dqn_ref[...])
        dckv, pk = rms_b(z_ref[:, 256:384], kvg_ref[...], dkvn_ref[...])
        dz_ref[:, 0:256] = dcq.astype(BF16)
        dz_ref[:, 256:384] = dckv.astype(BF16)
        dz_ref[:, 384:512] = dkr_ref[...].astype(BF16)

        @pl.when(i == 0)
        def _():
            dqg_ref[...] = pq
            dkvg_ref[...] = pk

        @pl.when(i > 0)
        def _():
            dqg_ref[...] += pq
            dkvg_ref[...] += pk

    tab = pl.BlockSpec((RB, LANE), lambda i: (i, 0))
    return pl.pallas_call(
        body, grid=(lp // RB,),
        in_specs=[pl.BlockSpec((RB, 512), lambda i: (i, CQ0 // 512)),
                  pl.BlockSpec((1, 256), lambda i: (0, 0)), pl.BlockSpec((1, 128), lambda i: (0, 0)),
                  pl.BlockSpec((RB, 256), lambda i: (i, 0)), tab, tab],
        out_specs=[pl.BlockSpec((RB, 512), lambda i: (i, 0)),
                   pl.BlockSpec((1, 256), lambda i: (0, 0)), pl.BlockSpec((1, 128), lambda i: (0, 0))],
        out_shape=[jax.ShapeDtypeStruct((lp, 512), BF16), jax.ShapeDtypeStruct((1, 256), F32),
                   jax.ShapeDtypeStruct((1, 128), F32)],
        name="mla_prep_bwd", compiler_params=_cp())(z, qg, kvg, dqn, dkvn, dkr)


def _mla_post(q_raw, kv_raw, krr, tabs, lp):
    def body(q_ref, kv_ref, kr_ref, c_ref, s1_ref, s2_ref, qo_ref, ko_ref, vo_ref):
        c, s1, s2, kr = c_ref[...], s1_ref[...], s2_ref[...], kr_ref[...]
        for h in range(HEADS):
            sl = slice(LANE * h, LANE * (h + 1))
            qo_ref[:, sl] = (_rope(q_ref[:, sl], c, s1, s2) * SCALE).astype(BF16)
            ko_ref[:, sl] = (kv_ref[:, sl] + kr).astype(BF16)
        vo_ref[...] = kv_ref[:, 1024:1536].astype(BF16)

    tab = pl.BlockSpec((RB, LANE), lambda i: (i, 0))
    wide = pl.BlockSpec((RB, 1024), lambda i: (i, 0))
    return pl.pallas_call(
        body, grid=(lp // RB,),
        in_specs=[wide, pl.BlockSpec((RB, 1536), lambda i: (i, 0)), tab, tab, tab, tab],
        out_specs=[wide, wide, pl.BlockSpec((RB, 512), lambda i: (i, 0))],
        out_shape=[jax.ShapeDtypeStruct((lp, 1024), BF16), jax.ShapeDtypeStruct((lp, 1024), BF16),
                   jax.ShapeDtypeStruct((lp, 512), BF16)],
        name="mla_post", compiler_params=_cp())(q_raw, kv_raw, krr, *tabs)


def _mla_post_bwd(dq, dk, dv, tabs, lp):
    def body(dq_ref, dk_ref, dv_ref, c_ref, s1_ref, s2_ref, dqr_ref, dkv_ref, dkr_ref):
        c, s1, s2 = c_ref[...], s1_ref[...], s2_ref[...]
        lane = lax.broadcasted_iota(jnp.int32, (1, LANE), 1)
        ropel = (lane >= QK_NOPE) & (lane < QK_NOPE + QK_ROPE)
        ksum = jnp.zeros((RB, LANE), F32)
        for h in range(HEADS):
            sl = slice(LANE * h, LANE * (h + 1))
            dqr_ref[:, sl] = _rope_t(dq_ref[:, sl] * SCALE, c, s1, s2).astype(BF16)
            dkt = dk_ref[:, sl]
            dkv_ref[:, sl] = dkt.astype(BF16)
            ksum = ksum + dkt
        dkv_ref[:, 1024:1536] = dv_ref[...].astype(BF16)
        dkr_ref[...] = jnp.where(ropel, _rope_t(jnp.where(ropel, ksum, 0.0), c, s1, s2), 0.0)

    tab = pl.BlockSpec((RB, LANE), lambda i: (i, 0))
    wide = pl.BlockSpec((RB, 1024), lambda i: (i, 0))
    return pl.pallas_call(
        body, grid=(lp // RB,),
        in_specs=[wide, wide, pl.BlockSpec((RB, 512), lambda i: (i, 0)), tab, tab, tab],
        out_specs=[wide, pl.BlockSpec((RB, 1536), lambda i: (i, 0)), tab],
        out_shape=[jax.ShapeDtypeStruct((lp, 1024), BF16), jax.ShapeDtypeStruct((lp, 1536), BF16),
                   jax.ShapeDtypeStruct((lp, 128), F32)],
        name="mla_post_bwd", compiler_params=_cp())(dq, dk, dv, *tabs)


def _attn_fwd(q, k, v, lp):
    nq = lp // RB

    def body(q_ref, k_ref, v_ref, o_ref, lse_ref, vm_scr):
        e = pl.program_id(0) % 2
        lane = lax.broadcasted_iota(jnp.int32, (1, LANE), 1)
        hm = (lane >= V_DIM) == (e == 1)
        vm_scr[...] = jnp.where(hm, v_ref[...], jnp.zeros_like(v_ref[...]))
        causal = (lax.broadcasted_iota(jnp.int32, (RB, RB), 1) <= lax.broadcasted_iota(jnp.int32, (RB, RB), 0))

        def qblock(i, _):
            rows = pl.ds(pl.multiple_of(i * RB, RB), RB)
            qb = q_ref[rows, :]

            def step(j, carry, masked):
                m, l, acc = carry
                cols = pl.ds(pl.multiple_of(j * RB, RB), RB)
                s = lax.dot_general(qb, k_ref[cols, :], (((1,), (1,)), ((), ())), preferred_element_type=F32)
                if masked:
                    s = jnp.where(causal, s, -jnp.inf)
                m_new = jnp.maximum(m, jnp.max(s, axis=-1, keepdims=True))
                al = jnp.exp(m - m_new)
                p = jnp.exp(s - m_new)
                l = al * l + jnp.sum(p, axis=-1, keepdims=True)
                acc = al * acc + jnp.dot(p.astype(BF16), vm_scr[cols, :], preferred_element_type=F32)
                return m_new, l, acc

            init = (jnp.full((RB, 1), -jnp.inf, F32), jnp.zeros((RB, 1), F32), jnp.zeros((RB, LANE), F32))
            carry = lax.fori_loop(0, i, lambda j, c: step(j, c, False), init)
            m, l, acc = step(i, carry, True)
            o = acc / l
            lse = jnp.where(hm, m + jnp.log(l), 0.0)

            @pl.when(e == 0)
            def _():
                o_ref[rows, :] = o
                lse_ref[rows, :] = lse

            @pl.when(e == 1)
            def _():
                o_ref[rows, :] += o
                lse_ref[rows, :] += lse

            return 0

        lax.fori_loop(0, nq, qblock, 0)

    head = pl.BlockSpec((lp, LANE), lambda h: (0, h))
    pair = pl.BlockSpec((lp, LANE), lambda h: (0, h // 2))
    return pl.pallas_call(
        body, grid=(HEADS,), in_specs=[head, head, pair], out_specs=[pair, pair],
        out_shape=[jax.ShapeDtypeStruct((lp, 512), F32), jax.ShapeDtypeStruct((lp, 512), F32)],
        scratch_shapes=[pltpu.VMEM((lp, LANE), BF16)],
        name="attn_fwd", compiler_params=_cp())(q, k, v)


def _attn_bwd(q, k, v, o, do, lse, lp):
    nq = lp // RB

    def body(q_ref, k_ref, v_ref, o_ref, do_ref, lse_ref, dq_ref, dk_ref, dv_ref,
             vm_scr, dom_scr, lse_scr, dl_scr):
        e = pl.program_id(0) % 2
        lane = lax.broadcasted_iota(jnp.int32, (1, LANE), 1)
        hm = (lane >= V_DIM) == (e == 1)
        vm_scr[...] = jnp.where(hm, v_ref[...], jnp.zeros_like(v_ref[...]))
        causal = (lax.broadcasted_iota(jnp.int32, (RB, RB), 1) <= lax.broadcasted_iota(jnp.int32, (RB, RB), 0))

        def prep(i, _):
            rows = pl.ds(pl.multiple_of(i * RB, RB), RB)
            dob = jnp.where(hm, do_ref[rows, :], 0.0)
            dom_scr[rows, :] = dob.astype(BF16)
            dl = jnp.sum(dob * o_ref[rows, :], axis=-1, keepdims=True)
            dl_scr[rows, :] = jnp.broadcast_to(dl, (RB, LANE))
            ls = jnp.max(jnp.where(hm, lse_ref[rows, :], -jnp.inf), axis=-1, keepdims=True)
            lse_scr[rows, :] = jnp.broadcast_to(ls, (RB, LANE))
            dq_ref[rows, :] = jnp.zeros((RB, LANE), F32)
            return 0

        lax.fori_loop(0, nq, prep, 0)

        def kvblock(j, _):
            cols = pl.ds(pl.multiple_of(j * RB, RB), RB)
            kb = k_ref[cols, :]
            vb = vm_scr[cols, :]

            def step(i, carry, masked):
                dk, dv = carry
                rows = pl.ds(pl.multiple_of(i * RB, RB), RB)
                qb = q_ref[rows, :]
                dob = dom_scr[rows, :]
                s = lax.dot_general(qb, kb, (((1,), (1,)), ((), ())), preferred_element_type=F32)
                if masked:
                    s = jnp.where(causal, s, -jnp.inf)
                p = jnp.exp(s - lse_scr[rows, :][:, 0:1])
                dv = dv + lax.dot_general(p.astype(BF16), dob, (((0,), (0,)), ((), ())), preferred_element_type=F32)
                dp = lax.dot_general(dob, vb, (((1,), (1,)), ((), ())), preferred_element_type=F32)
                ds = (p * (dp - dl_scr[rows, :][:, 0:1])).astype(BF16)
                dk = dk + lax.dot_general(ds, qb, (((0,), (0,)), ((), ())), preferred_element_type=F32)
                dq_ref[rows, :] += jnp.dot(ds, kb, preferred_element_type=F32)
                return dk, dv

            zero = jnp.zeros((RB, LANE), F32)
            carry = step(j, (zero, zero), True)
            dk, dv = lax.fori_loop(j + 1, nq, lambda i, c: step(i, c, False), carry)
            dk_ref[cols, :] = dk

            @pl.when(e == 0)
            def _():
                dv_ref[cols, :] = dv

            @pl.when(e == 1)
            def _():
                dv_ref[cols, :] += dv

            return 0

        lax.fori_loop(0, nq, kvblock, 0)

    head = pl.BlockSpec((lp, LANE), lambda h: (0, h))
    pair = pl.BlockSpec((lp, LANE), lambda h: (0, h // 2))
    return pl.pallas_call(
        body, grid=(HEADS,), in_specs=[head, head, pair, pair, pair, pair], out_specs=[head, head, pair],
        out_shape=[jax.ShapeDtypeStruct((lp, 1024), F32), jax.ShapeDtypeStruct((lp, 1024), F32),
                   jax.ShapeDtypeStruct((lp, 512), F32)],
        scratch_shapes=[pltpu.VMEM((lp, LANE), BF16), pltpu.VMEM((lp, LANE), BF16),
                        pltpu.VMEM((lp, LANE), F32), pltpu.VMEM((lp, LANE), F32)],
        name="attn_bwd", compiler_params=_cp())(q, k, v, o, do, lse)


def _pool_lane_windows():
    lane = lax.broadcasted_iota(jnp.int32, (1, 256), 1)
    return jnp.where(lane < 64, 2, jnp.where(lane < 128, 4, jnp.where(lane < 192, 8, 16)))


def _by_window(wl, s2, s4, s8, s16):
    return jnp.where(wl == 2, s2, jnp.where(wl == 4, s4, jnp.where(wl == 8, s8, s16)))


def _pool_fwd_rows(pv_ext, t0):
    n = pv_ext.shape[0]
    wl = _pool_lane_windows()
    s2 = pv_ext + _dn(pv_ext, 1)
    s4 = s2 + _dn(s2, 2)
    s8 = s4 + _dn(s4, 4)
    s16 = s8 + _dn(s8, 8)
    t = t0 + lax.broadcasted_iota(jnp.int32, (n, 1), 0)
    cnt = jnp.maximum(jnp.minimum(t + 1, wl), 1).astype(F32)
    return _by_window(wl, s2, s4, s8, s16) / cnt - pv_ext


def _conv_dn(x_ext, w_ref, taps):
    acc = w_ref[taps - 1:taps, :] * x_ext
    for j in range(1, taps):
        acc = acc + w_ref[taps - 1 - j:taps - j, :] * _dn(x_ext, j)
    return acc


def _conv_up(g_ext, w_ref, taps):
    acc = w_ref[taps - 1:taps, :] * g_ext
    for j in range(1, taps):
        acc = acc + w_ref[taps - 1 - j:taps - j, :] * _up(g_ext, j)
    return acc


def _ln_fwd(c, g, b):
    mu = jnp.mean(c, axis=-1, keepdims=True)
    xc = c - mu
    r = lax.rsqrt(jnp.mean(xc * xc, axis=-1, keepdims=True) + EPS)
    xh = xc * r
    return xh * g + b, xh, r


def _halo_specs(lp, width, col):
    per = RB // HB
    last = lp // HB - 1
    cur = pl.BlockSpec((RB, width), lambda i: (i, col))
    prev = pl.BlockSpec((HB, width), lambda i: (jnp.maximum(i * per - 1, 0), col))
    nxt = pl.BlockSpec((HB, width), lambda i: (jnp.minimum((i + 1) * per, last), col))
    return cur, prev, nxt


def _mix_fwd(z, oat, bd, pscale, cw, cb, lng, lnb, sw, lp):
    def body(za, zah, mg, oat_ref, cu, cuh, cg, sbb, sbc, sbch, sbx, sbxh, sg,
             bd_ref, ps_ref, cw_ref, cb_ref, lng_ref, lnb_ref, sw_ref, u_ref):
        i = pl.program_id(0)
        pm = jnp.where(i > 0, 1.0, 0.0).astype(F32)
        pv = jnp.concatenate([zah[:, 0:256] * pm, za[:, 0:256]], axis=0)
        p = _pool_fwd_rows(pv, i * RB - HB)[HB:]
        y = jnp.dot(p.astype(BF16), bd_ref[...], preferred_element_type=F32)
        u_ref[:, 0:256] = (y * ps_ref[...] * _silu(za[:, 256:512])).astype(BF16)
        u_ref[:, 256:768] = (oat_ref[...] * _silu(mg[...])).astype(BF16)
        ce = jnp.concatenate([cuh[...] * pm, cu[...]], axis=0)
        glu = ce[:, 0:256] * _sig(ce[:, 256:512])
        c = _conv_dn(glu, cw_ref, CONF_K)[HB:] + cb_ref[...]
        n, _, _ = _ln_fwd(c, lng_ref[...], lnb_ref[...])
        u_ref[:, 768:1024] = (_silu(n) * _silu(cg[...])).astype(BF16)
        qe = jnp.concatenate([sbch[...] * sbxh[...] * pm, sbc[...] * sbx[...]], axis=0)
        cv = _conv_dn(qe, sw_ref, SC_K)[HB:]
        u_ref[:, 1024:1280] = (sbb[...] * cv * _silu(sg[...])).astype(BF16)

    a_cur, a_prev, _ = _halo_specs(lp, 512, PV0 // 512)
    cu_cur, cu_prev, _ = _halo_specs(lp, 512, CU0 // 512)
    sc_cur, sc_prev, _ = _halo_specs(lp, 256, SBC0 // 256)
    sx_cur, sx_prev, _ = _halo_specs(lp, 256, SBX0 // 256)
    c256 = lambda c0: pl.BlockSpec((RB, 256), lambda i: (i, c0 // 256))
    full = lambda r, c: pl.BlockSpec((r, c), lambda i: (0, 0))
    return pl.pallas_call(
        body, grid=(lp // RB,),
        in_specs=[a_cur, a_prev, pl.BlockSpec((RB, 512), lambda i: (i, MG0 // 512)),
                  pl.BlockSpec((RB, 512), lambda i: (i, 0)),
                  cu_cur, cu_prev, c256(CG0), c256(SBB0), sc_cur, sc_prev, sx_cur, sx_prev, c256(SG0),
                  full(256, 256), full(1, 256), full(32, 256), full(1, 256), full(1, 256), full(1, 256),
                  full(8, 256)],
        out_specs=pl.BlockSpec((RB, 1280), lambda i: (i, 0)),
        out_shape=jax.ShapeDtypeStruct((lp, 1280), BF16),
        name="mix_fwd", compiler_params=_cp())(z, z, z, oat, z, z, z, z, z, z, z, z, z,
                                               bd, pscale, cw, cb, lng, lnb, sw)


def _mix_bwd(z, oat, du, bd, pscale, cw, cb, lng, lnb, sw, lp):
    nb = lp // RB
    ne = RB + 2 * HB
    nf = RB + HB

    def body(za, zah, zan, mg, oat_ref, cu, cuh, cun, cg, cgn, sbb, sbbn, sbc, sbch, sbcn, sbx, sbxh, sbxn,
             sg, sgn, du_ref, dun_ref, bd_ref, ps_ref, cw_ref, cb_ref, lng_ref, lnb_ref, sw_ref,
             dza_ref, dzm_ref, dzc_ref, doat_ref, dbd_ref, dcw_ref, dsw_ref, dsm_ref):
        i = pl.program_id(0)
        pm = jnp.where(i > 0, 1.0, 0.0).astype(F32)
        nm = jnp.where(i < nb - 1, 1.0, 0.0).astype(F32)

        def ext(cur, prev, nxt, sl=slice(None)):
            return jnp.concatenate([prev[:, sl] * pm, cur[:, sl], nxt[:, sl] * nm], axis=0)

        def fwd(cur, nxt, sl=slice(None)):
            return jnp.concatenate([cur[:, sl], nxt[:, sl] * nm], axis=0)

        def csum(x):
            return jnp.sum(x, axis=0, keepdims=True)

        @pl.when(i == 0)
        def _():
            dbd_ref[...] = jnp.zeros((256, 256), F32)
            dcw_ref[...] = jnp.zeros((32, 256), F32)
            dsw_ref[...] = jnp.zeros((8, 256), F32)
            dsm_ref[...] = jnp.zeros((8, 256), F32)

        a_cols, b_cols = slice(0, 256), slice(256, 512)
        pv_e = ext(za, zah, zan, a_cols)
        p = _pool_fwd_rows(pv_e, i * RB - HB)[HB:HB + RB]
        pb = p.astype(BF16)
        y = jnp.dot(pb, bd_ref[...], preferred_element_type=F32)
        pg_f = fwd(za, zan, b_cols)
        dua_f = fwd(du_ref, dun_ref, slice(0, 256))
        dyp_f = dua_f * ps_ref[...] * _silu(pg_f)
        dypb = dyp_f.astype(BF16)
        dp_f = lax.dot_general(dypb, bd_ref[...], (((1,), (1,)), ((), ())), preferred_element_type=F32)
        wl = _pool_lane_windows()
        t = i * RB + lax.broadcasted_iota(jnp.int32, (nf, 1), 0)
        cnt = jnp.minimum(t + 1, wl).astype(F32)
        qf = dp_f / cnt
        f2 = qf + _up(qf, 1)
        f4 = f2 + _up(f2, 2)
        f8 = f4 + _up(f4, 4)
        f16 = f8 + _up(f8, 8)
        dpv = (_by_window(wl, f2, f4, f8, f16) - dp_f)[0:RB]
        dua = du_ref[:, 0:256]
        pg = za[:, b_cols]
        dpg = dua * y * ps_ref[...] * _dsilu(pg)
        dza_ref[:, 0:256] = dpv.astype(BF16)
        dza_ref[:, 256:512] = dpg.astype(BF16)
        d_scale = csum(dua * y * _silu(pg))
        d_bd = lax.dot_general(pb, dypb[0:RB], (((0,), (0,)), ((), ())), preferred_element_type=F32)

        dub = du_ref[:, 256:768]
        mgv = mg[...]
        dzm_ref[...] = (dub * oat_ref[...] * _dsilu(mgv)).astype(BF16)
        doat_ref[...] = dub * _silu(mgv)

        a_e = ext(cu, cuh, cun, slice(0, 256))
        gt_e = ext(cu, cuh, cun, slice(256, 512))
        sg_e = _sig(gt_e)
        glu_e = a_e * sg_e
        c_f = _conv_dn(glu_e, cw_ref, CONF_K)[HB:] + cb_ref[...]
        n_f, xh_f, r_f = _ln_fwd(c_f, lng_ref[...], lnb_ref[...])
        cg_f = fwd(cg, cgn)
        duc_f = fwd(du_ref, dun_ref, slice(768, 1024))
        sw_f = _silu(n_f)
        dcg = (duc_f * sw_f * _dsilu(cg_f))[0:RB]
        dn_f = duc_f * _silu(cg_f) * _dsilu(n_f)
        a_f = dn_f * lng_ref[...]
        dc_f = r_f * (a_f - jnp.mean(a_f, axis=-1, keepdims=True)
                      - xh_f * jnp.mean(a_f * xh_f, axis=-1, keepdims=True))
        d_lng = csum((dn_f * xh_f)[0:RB])
        d_lnb = csum(dn_f[0:RB])
        d_cb = csum(dc_f[0:RB])
        dglu = _conv_up(dc_f, cw_ref, CONF_K)[0:RB]
        dc_c = dc_f[0:RB]
        for kk in range(CONF_K):
            j = CONF_K - 1 - kk
            dcw_ref[kk:kk + 1, :] += csum(dc_c * _dn(glu_e, j)[HB:HB + RB])

        sgc = sg_e[HB:HB + RB]
        a_c = a_e[HB:HB + RB]
        dzc_ref[:, 0:256] = (dglu * sgc).astype(BF16)
        dzc_ref[:, 256:512] = (dglu * a_c * sgc * (1.0 - sgc)).astype(BF16)
        dzc_ref[:, 512:768] = dcg.astype(BF16)

        c_e = ext(sbc, sbch, sbcn)
        x_e = ext(sbx, sbxh, sbxn)
        q_e = c_e * x_e
        cv_f = _conv_dn(q_e, sw_ref, SC_K)[HB:]
        bg_f = fwd(sbb, sbbn)
        sg_f = fwd(sg, sgn)
        dud_f = fwd(du_ref, dun_ref, slice(1024, 1280))
        ssg_f = _silu(sg_f)
        dcv_f = dud_f * bg_f * ssg_f
        dbg = (dud_f * cv_f * ssg_f)[0:RB]
        dsg = (dud_f * bg_f * cv_f * _dsilu(sg_f))[0:RB]
        dq = _conv_up(dcv_f, sw_ref, SC_K)[0:RB]
        dcv_c = dcv_f[0:RB]
        for kk in range(SC_K):
            j = SC_K - 1 - kk
            dsw_ref[kk:kk + 1, :] += csum(dcv_c * _dn(q_e, j)[HB:HB + RB])

        dzc_ref[:, 768:1024] = dbg.astype(BF16)
        dzc_ref[:, 1024:1280] = (dq * x_e[HB:HB + RB]).astype(BF16)
        dzc_ref[:, 1280:1536] = (dq * c_e[HB:HB + RB]).astype(BF16)
        dzc_ref[:, 1536:1792] = dsg.astype(BF16)

        dbd_ref[...] += d_bd
        dsm_ref[0:1, :] += d_scale
        dsm_ref[1:2, :] += d_cb
        dsm_ref[2:3, :] += d_lng
        dsm_ref[3:4, :] += d_lnb

    a3 = _halo_specs(lp, 512, PV0 // 512)
    cu3 = _halo_specs(lp, 512, CU0 // 512)
    cg3 = _halo_specs(lp, 256, CG0 // 256)
    sbb3 = _halo_specs(lp, 256, SBB0 // 256)
    sbc3 = _halo_specs(lp, 256, SBC0 // 256)
    sbx3 = _halo_specs(lp, 256, SBX0 // 256)
    sg3 = _halo_specs(lp, 256, SG0 // 256)
    du3 = _halo_specs(lp, 1280, 0)
    full = lambda r, c: pl.BlockSpec((r, c), lambda i: (0, 0))
    in_specs = [a3[0], a3[1], a3[2], pl.BlockSpec((RB, 512), lambda i: (i, MG0 // 512)),
                pl.BlockSpec((RB, 512), lambda i: (i, 0)),
                cu3[0], cu3[1], cu3[2], cg3[0], cg3[2], sbb3[0], sbb3[2],
                sbc3[0], sbc3[1], sbc3[2], sbx3[0], sbx3[1], sbx3[2], sg3[0], sg3[2],
                du3[0], du3[2],
                full(256, 256), full(1, 256), full(32, 256), full(1, 256), full(1, 256), full(1, 256),
                full(8, 256)]
    out_specs = [pl.BlockSpec((RB, 512), lambda i: (i, 0)), pl.BlockSpec((RB, 512), lambda i: (i, 0)),
                 pl.BlockSpec((RB, 1792), lambda i: (i, 0)), pl.BlockSpec((RB, 512), lambda i: (i, 0)),
                 full(256, 256), full(32, 256), full(8, 256), full(8, 256)]
    out_shape = [jax.ShapeDtypeStruct((lp, 512), BF16), jax.ShapeDtypeStruct((lp, 512), BF16),
                 jax.ShapeDtypeStruct((lp, 1792), BF16), jax.ShapeDtypeStruct((lp, 512), F32),
                 jax.ShapeDtypeStruct((256, 256), F32), jax.ShapeDtypeStruct((32, 256), F32),
                 jax.ShapeDtypeStruct((8, 256), F32), jax.ShapeDtypeStruct((8, 256), F32)]
    return pl.pallas_call(
        body, grid=(nb,), in_specs=in_specs, out_specs=out_specs, out_shape=out_shape,
        name="mix_bwd", compiler_params=_cp())(
            z, z, z, z, oat, z, z, z, z, z, z, z, z, z, z, z, z, z, z, z, du, du,
            bd, pscale, cw, cb, lng, lnb, sw)


U_OFF = (0, 256, 768, 1024, 1280)
MRB = 192


def _merge_fwd(x, u, z, gb, wout, wo, gpost, lp):
    def body(x_ref, u_ref, gl_ref, gb_ref, wout_ref, wo_ref, g_ref, xo_ref, m_ref, o2_ref):
        m = jnp.zeros((MRB, D), F32)
        for b in range(4):
            y = jnp.dot(u_ref[:, U_OFF[b]:U_OFF[b + 1]], wout_ref[U_OFF[b]:U_OFF[b + 1], :],
                        preferred_element_type=F32)
            sl = slice(D * b, D * (b + 1))
            m = m + _sig(gl_ref[:, sl] + gb_ref[:, sl]) * y
        mb = m.astype(BF16)
        m_ref[...] = mb
        o2 = jnp.dot(mb, wo_ref[...], preferred_element_type=F32)
        o2_ref[...] = o2
        r = lax.rsqrt(jnp.mean(o2 * o2, axis=-1, keepdims=True) + EPS)
        xo_ref[...] = x_ref[...] + o2 * r * g_ref[...]

    blk = pl.BlockSpec((MRB, D), lambda i: (i, 0))
    full = lambda r, c: pl.BlockSpec((r, c), lambda i: (0, 0))
    return pl.pallas_call(
        body, grid=(lp // MRB,),
        in_specs=[blk, pl.BlockSpec((MRB, 1280), lambda i: (i, 0)), pl.BlockSpec((MRB, 4096), lambda i: (i, 0)),
                  full(1, 4096), full(1280, D), full(D, D), full(1, D)],
        out_specs=[blk, blk, blk],
        out_shape=[jax.ShapeDtypeStruct((lp, D), F32), jax.ShapeDtypeStruct((lp, D), BF16),
                   jax.ShapeDtypeStruct((lp, D), F32)],
        name="merge_fwd", compiler_params=_cp())(x, u, z, gb, wout, wo, gpost)


def _merge_bwd(dx, o2, u, z, gb, wout, wo, gpost, lp):
    def body(dx_ref, o2_ref, u_ref, gl_ref, gb_ref, wout_ref, wo_ref, g_ref,
             do2_ref, dgl_ref, dy_ref, du_ref, dgb_ref, dg_ref):
        i = pl.program_id(0)
        o2 = o2_ref[...]
        dy = dx_ref[...]
        r = lax.rsqrt(jnp.mean(o2 * o2, axis=-1, keepdims=True) + EPS)
        a = dy * g_ref[...]
        do2 = (r * a - o2 * (r * r * r) * jnp.mean(a * o2, axis=-1, keepdims=True)).astype(BF16)
        do2_ref[...] = do2
        dg = jnp.sum(dy * o2 * r, axis=0, keepdims=True)
        dm = lax.dot_general(do2, wo_ref[...], (((1,), (1,)), ((), ())), preferred_element_type=F32)
        for b in range(4):
            rows = slice(U_OFF[b], U_OFF[b + 1])
            y = jnp.dot(u_ref[:, rows], wout_ref[rows, :], preferred_element_type=F32)
            sl = slice(D * b, D * (b + 1))
            gt = _sig(gl_ref[:, sl] + gb_ref[:, sl])
            dgl = dm * y * gt * (1.0 - gt)
            dgl_ref[:, sl] = dgl.astype(BF16)
            part = jnp.sum(dgl, axis=0, keepdims=True)

            @pl.when(i == 0)
            def _(part=part, sl=sl):
                dgb_ref[:, sl] = part

            @pl.when(i > 0)
            def _(part=part, sl=sl):
                dgb_ref[:, sl] += part

            dyb = (dm * gt).astype(BF16)
            dy_ref[:, sl] = dyb
            du_ref[:, rows] = lax.dot_general(dyb, wout_ref[rows, :], (((1,), (1,)), ((), ())),
                                              preferred_element_type=F32)

        @pl.when(i == 0)
        def _():
            dg_ref[...] = dg

        @pl.when(i > 0)
        def _():
            dg_ref[...] += dg

    blk = pl.BlockSpec((MRB, D), lambda i: (i, 0))
    wide = pl.BlockSpec((MRB, 4096), lambda i: (i, 0))
    ub = pl.BlockSpec((MRB, 1280), lambda i: (i, 0))
    full = lambda r, c: pl.BlockSpec((r, c), lambda i: (0, 0))
    return pl.pallas_call(
        body, grid=(lp // MRB,),
        in_specs=[blk, blk, ub, wide, full(1, 4096), full(1280, D), full(D, D), full(1, D)],
        out_specs=[blk, wide, wide, ub, full(1, 4096), full(1, D)],
        out_shape=[jax.ShapeDtypeStruct((lp, D), BF16), jax.ShapeDtypeStruct((lp, 4096), BF16),
                   jax.ShapeDtypeStruct((lp, 4096), BF16), jax.ShapeDtypeStruct((lp, 1280), F32),
                   jax.ShapeDtypeStruct((1, 4096), F32), jax.ShapeDtypeStruct((1, D), F32)],
        name="merge_bwd", compiler_params=_cp())(dx, o2, u, z, gb, wout, wo, gpost)


def _loss_head(xf, tgt, n_real, lp):
    def body(x_ref, t_ref, dy_ref, ls_ref):
        i = pl.program_id(0)
        t = i * RB + lax.broadcasted_iota(jnp.int32, (RB, 1), 0)
        real = (t >= N_META) & (t < n_real)
        err = jnp.where(real, x_ref[...] - t_ref[...], 0.0)
        dy_ref[...] = err / D
        part = 0.5 * jnp.sum(jnp.mean(err * err, axis=-1, keepdims=True), axis=0, keepdims=True)
        part = jnp.broadcast_to(part, (8, LANE))

        @pl.when(i == 0)
        def _():
            ls_ref[...] = part

        @pl.when(i > 0)
        def _():
            ls_ref[...] += part

    blk = pl.BlockSpec((RB, D), lambda i: (i, 0))
    return pl.pallas_call(
        body, grid=(lp // RB,), in_specs=[blk, blk],
        out_specs=[blk, pl.BlockSpec((8, LANE), lambda i: (0, 0))],
        out_shape=[jax.ShapeDtypeStruct((lp, D), F32), jax.ShapeDtypeStruct((8, LANE), F32)],
        name="loss_head", compiler_params=_cp())(xf, tgt)


def _peer(d):
    x, y, c = lax.axis_index("x"), lax.axis_index("y"), lax.axis_index("c")
    return (x ^ ((d >> 2) & 1), y ^ ((d >> 1) & 1), c ^ (d & 1))


def _index_of(p):
    return 4 * p[0] + 2 * p[1] + p[2]


def _all_gather(xs, name):
    p_rows, width = xs.shape

    def body(x_ref, out_ref, send_sems, recv_sems, local_sem):
        me = _peer(0)
        sibling = _peer(1)
        chips = [2, 4, 6]

        def slot(dev):
            return out_ref.at[_index_of(dev)]

        def copy(kk, block, to, src=None):
            return pltpu.make_async_remote_copy(
                src_ref=slot(block) if src is None else src, dst_ref=slot(block),
                send_sem=send_sems.at[kk], recv_sem=recv_sems.at[kk], device_id=to, device_id_type=MESH)

        mine = pltpu.make_async_copy(x_ref, slot(me), local_sem)
        mine.start()
        first = [copy(0, me, sibling, src=x_ref)]
        first += [copy(1 + j, me, _peer(d), src=x_ref) for j, d in enumerate(chips)]
        for cp in first:
            cp.start()
        passed = [copy(4 + j, _peer(d), sibling) for j, d in enumerate(chips)]
        for j, d in enumerate(chips):
            copy(1 + j, _peer(d), me).wait_recv()
            passed[j].start()
        copy(0, sibling, me).wait_recv()
        for j, d in enumerate(chips):
            copy(4 + j, _peer(d | 1), me).wait_recv()
        for cp in first + passed:
            cp.wait_send()
        mine.wait()

    anyspec = pl.BlockSpec(memory_space=pl.ANY)
    return pl.pallas_call(
        body, in_specs=[anyspec], out_specs=anyspec,
        out_shape=jax.ShapeDtypeStruct((N_DEV, p_rows, width), xs.dtype),
        scratch_shapes=[pltpu.SemaphoreType.DMA((7,)), pltpu.SemaphoreType.DMA((7,)), pltpu.SemaphoreType.DMA],
        name=name)(xs)


def _all_to_all(xs, name):
    _, q_rows, width = xs.shape

    def body(x_ref, out_ref, send_sems, recv_sems, local_sem):
        me = _peer(0)
        my = _index_of(me)
        mine = pltpu.make_async_copy(x_ref.at[my], out_ref.at[my], local_sem)
        mine.start()
        copies = []
        for d in range(1, N_DEV):
            to = _peer(d)
            copies.append(pltpu.make_async_remote_copy(
                src_ref=x_ref.at[_index_of(to)], dst_ref=out_ref.at[my],
                send_sem=send_sems.at[d - 1], recv_sem=recv_sems.at[d - 1], device_id=to, device_id_type=MESH))
        for cp in copies:
            cp.start()
        for d in range(1, N_DEV):
            frm = _index_of(_peer(d))
            pltpu.make_async_remote_copy(
                src_ref=x_ref.at[frm], dst_ref=out_ref.at[frm],
                send_sem=send_sems.at[d - 1], recv_sem=recv_sems.at[d - 1], device_id=_peer(d),
                device_id_type=MESH).wait_recv()
        for cp in copies:
            cp.wait_send()
        mine.wait()

    anyspec = pl.BlockSpec(memory_space=pl.ANY)
    return pl.pallas_call(
        body, in_specs=[anyspec], out_specs=anyspec,
        out_shape=jax.ShapeDtypeStruct((N_DEV, q_rows, width), xs.dtype),
        scratch_shapes=[pltpu.SemaphoreType.DMA((7,)), pltpu.SemaphoreType.DMA((7,)), pltpu.SemaphoreType.DMA],
        name=name)(xs)


def _reduce_adam(parts, w, m, v, rows_blk, name):
    _, q_rows, _ = parts.shape
    c1 = 1.0 - ADAM_B1 ** ADAM_STEP
    c2 = 1.0 - ADAM_B2 ** ADAM_STEP

    def body(p_ref, w_ref, m_ref, v_ref, g_ref, d_ref, mo_ref, vo_ref):
        g = p_ref[0].astype(F32)
        for j in range(1, N_DEV):
            g = g + p_ref[j].astype(F32)
        g_ref[...] = g
        mn = ADAM_B1 * m_ref[...] + (1.0 - ADAM_B1) * g
        vn = ADAM_B2 * v_ref[...] + (1.0 - ADAM_B2) * (g * g)
        mo_ref[...] = mn
        vo_ref[...] = vn
        d_ref[...] = -ADAM_LR * ((mn / c1) / (jnp.sqrt(vn / c2) + ADAM_EPS) + ADAM_WD * w_ref[...])

    blk = pl.BlockSpec((rows_blk, LANE), lambda i: (i, 0))
    out = jax.ShapeDtypeStruct((q_rows, LANE), F32)
    return pl.pallas_call(
        body, grid=(q_rows // rows_blk,),
        in_specs=[pl.BlockSpec((N_DEV, rows_blk, LANE), lambda i: (0, i, 0)), blk, blk, blk],
        out_specs=[blk, blk, blk, blk], out_shape=[out, out, out, out],
        name=name, compiler_params=_cp())(parts, w, m, v)


BIG = (("w_in", (D, 916)), ("w_out_pool", (256, 128)), ("w_uq", (256, 96)), ("w_ukv", (128, 128)),
       ("w_out_mla", (512, 128)), ("w_out_conf", (256, 128)), ("w_out_sc", (256, 128)), ("w_o", (128, D)))
BIG_ROWS = sum(r * c for _, (r, c) in BIG) // LANE
TAIL = (("meta_tokens", (N_META, 128)), ("conf_dw_w", (DEPTH, CONF_K, 32)), ("sc_dw_w", (DEPTH, SC_K, 32)))
TAIL_ROWS = sum(int(np.prod(s)) for _, s in TAIL) // LANE
TAIL_PAD = 56
SHARD_ROWS = DEPTH * BIG_ROWS + TAIL_ROWS
SHARD_BLK = 512
SHARD_PAD = -(-SHARD_ROWS // SHARD_BLK) * SHARD_BLK
SMALL = (("pre_norm_g", (DEPTH, D)), ("gate_bias", (DEPTH, 4096)), ("pool_w", (DEPTH, 4, 64, 64)),
         ("pool_scale", (DEPTH, 256)), ("q_norm_g", (DEPTH, 256)), ("kv_norm_g", (DEPTH, 128)),
         ("conf_dw_b", (DEPTH, 256)), ("conf_ln_g", (DEPTH, 256)), ("conf_ln_b", (DEPTH, 256)),
         ("post_norm_g", (DEPTH, D)))
SMALL_ROWS = sum(int(np.prod(s)) for _, s in SMALL) // LANE
SMALL_PAD = -(-(SMALL_ROWS + 1) // 8) * 8


def _pack_shards(t):
    parts = []
    for i in range(DEPTH):
        parts += [t[n][i].reshape(-1, LANE) for n, _ in BIG]
    parts += [t[n].reshape(-1, LANE) for n, _ in TAIL]
    parts.append(jnp.zeros((SHARD_PAD - SHARD_ROWS, LANE), F32))
    return jnp.concatenate(parts, axis=0)


def _unpack_shards(flat):
    out = {}
    for k, (n, (r, c)) in enumerate(BIG):
        off = sum(a * b for _, (a, b) in BIG[:k]) // LANE
        out[n] = jnp.stack([flat[i * BIG_ROWS + off:i * BIG_ROWS + off + r * c // LANE].reshape(r, c)
                            for i in range(DEPTH)])
    off = DEPTH * BIG_ROWS
    for n, s in TAIL:
        rows = int(np.prod(s)) // LANE
        out[n] = flat[off:off + rows].reshape(s)
        off += rows
    return out


def _pack_small(t, extra_row):
    parts = [t[n].reshape(-1, LANE) for n, _ in SMALL] + [extra_row]
    parts.append(jnp.zeros((SMALL_PAD - SMALL_ROWS - 1, LANE), F32))
    return jnp.concatenate(parts, axis=0)


def _unpack_small(flat):
    out, off = {}, 0
    for n, s in SMALL:
        rows = int(np.prod(s)) // LANE
        out[n] = flat[off:off + rows].reshape(s)
        off += rows
    return out


def _cols_by_dest(g, width):
    r = g.shape[0]
    return g.reshape(r, N_DEV, width).transpose(1, 0, 2).reshape(N_DEV, -1, LANE)


def _full_cols(gathered, off, r, c):
    blk = gathered[:, off:off + r * c // LANE].reshape(N_DEV, r, c)
    return blk.transpose(1, 0, 2).reshape(r, N_DEV * c)


def _pack_w_in(w):
    r = w.shape[0]
    z = lambda n: jnp.zeros((r, n), w.dtype)
    return jnp.concatenate([w[:, 3232:7328], w[:, 0:896], z(64), w[:, 896:928], z(32), w[:, 928:3232]], axis=1)


def _unpack_w_in(g):
    return jnp.concatenate([g[:, 4096:4992], g[:, 5056:5088], g[:, 5120:7424], g[:, 0:4096]], axis=1)


def _rope_tables(lp):
    inv = 1.0 / (ROPE_THETA ** (jnp.arange(0, QK_ROPE, 2, dtype=F32) / QK_ROPE))
    ang = jnp.arange(lp, dtype=F32)[:, None] * inv[None, :]
    cos, sin = jnp.cos(ang), jnp.sin(ang)
    one = jnp.ones((lp, QK_NOPE), F32)
    zero = jnp.zeros((lp, QK_NOPE), F32)
    z16 = jnp.zeros((lp, 16), F32)
    c = jnp.concatenate([one, cos, cos, jnp.ones((lp, 32), F32)], axis=1)
    s1 = jnp.concatenate([zero, z16, sin, jnp.zeros((lp, 32), F32)], axis=1)
    s2 = jnp.concatenate([zero, -sin, z16, jnp.zeros((lp, 32), F32)], axis=1)
    return c, s1, s2


def kernel(x, meta_tokens, pre_norm_g, w_in, gate_bias, pool_w, pool_scale, w_out_pool, q_norm_g, w_uq, kv_norm_g, w_ukv, w_out_mla, conf_dw_w, conf_dw_b, conf_ln_g, conf_ln_b, w_out_conf, sc_dw_w, w_out_sc, w_o, post_norm_g, loss_target, m_meta_tokens, m_pre_norm_g, m_w_in, m_gate_bias, m_pool_w, m_pool_scale, m_w_out_pool, m_q_norm_g, m_w_uq, m_kv_norm_g, m_w_ukv, m_w_out_mla, m_conf_dw_w, m_conf_dw_b, m_conf_ln_g, m_conf_ln_b, m_w_out_conf, m_sc_dw_w, m_w_out_sc, m_w_o, m_post_norm_g, v_meta_tokens, v_pre_norm_g, v_w_in, v_gate_bias, v_pool_w, v_pool_scale, v_w_out_pool, v_q_norm_g, v_w_uq, v_kv_norm_g, v_w_ukv, v_w_out_mla, v_conf_dw_w, v_conf_dw_b, v_conf_ln_g, v_conf_ln_b, v_w_out_conf, v_sc_dw_w, v_w_out_sc, v_w_o, v_post_norm_g):
    names = [n for n, _ in BIG] + [n for n, _ in TAIL] + [n for n, _ in SMALL]
    loc = locals()
    w = {n: loc[n] for n in names}
    mom = {n: loc["m_" + n] for n in names}
    vel = {n: loc["v_" + n] for n in names}

    seq = x.shape[1]
    n_real = N_META + seq
    lp = -(-n_real // RB) * RB
    tmb = lp // 3
    tabs = _rope_tables(lp)

    w_flat = _pack_shards(w)
    big = _all_gather(w_flat[:DEPTH * BIG_ROWS].astype(BF16), "gather_weights")
    tail_own = jnp.concatenate([w_flat[DEPTH * BIG_ROWS:SHARD_ROWS],
                                jnp.zeros((TAIL_PAD - TAIL_ROWS, LANE), F32)], axis=0)
    tail = _all_gather(tail_own, "gather_tail")
    meta_full = _full_cols(tail, 0, N_META, 128)
    conf_w_full = _full_cols(tail, N_META, DEPTH * CONF_K, 32).reshape(DEPTH, CONF_K, 256)
    sc_w_full = _full_cols(tail, N_META + DEPTH * CONF_K * 32 // LANE, DEPTH * SC_K, 32).reshape(DEPTH, SC_K, 256)

    def layer_weights(i):
        base = i * BIG_ROWS
        offs = {}
        o = 0
        for n, (r, c) in BIG:
            offs[n] = base + o
            o += r * c // LANE
        lw = {}
        lw["w_in"] = _pack_w_in(_full_cols(big, offs["w_in"], D, 916))
        lw["wout"] = jnp.concatenate([_full_cols(big, offs[n], r, 128) for n, r in
                                      (("w_out_pool", 256), ("w_out_mla", 512), ("w_out_conf", 256),
                                       ("w_out_sc", 256))], axis=0)
        wuq = _full_cols(big, offs["w_uq"], Q_RANK, 96).reshape(Q_RANK, HEADS, 96)
        lw["w_uq"] = jnp.pad(wuq, ((0, 0), (0, 0), (0, 32))).reshape(Q_RANK, HEADS * LANE)
        wukv = _full_cols(big, offs["w_ukv"], KV_RANK, 128).reshape(KV_RANK, HEADS, 128)
        wk = jnp.pad(wukv[:, :, :QK_NOPE], ((0, 0), (0, 0), (0, 64))).reshape(KV_RANK, HEADS * LANE)
        lw["w_ukv"] = jnp.concatenate([wk, wukv[:, :, QK_NOPE:].reshape(KV_RANK, HEADS * V_DIM)], axis=1)
        lw["w_o"] = big[:, offs["w_o"]:offs["w_o"] + 128 * D // LANE].reshape(D, D)
        bd = jnp.zeros((256, 256), F32)
        for g in range(4):
            bd = bd.at[64 * g:64 * (g + 1), 64 * g:64 * (g + 1)].set(pool_w[i, g])
        lw["bd"] = bd.astype(BF16)
        lw["cw"] = jnp.pad(conf_w_full[i], ((0, 1), (0, 0)))
        lw["sw"] = jnp.pad(sc_w_full[i], ((0, 8 - SC_K), (0, 0)))
        return lw

    pad_rows = lp - n_real
    xr = jnp.concatenate([meta_full, x[0], jnp.zeros((pad_rows, D), F32)], axis=0)
    tgt = jnp.pad(loss_target[0], ((N_META, pad_rows), (0, 0)))
    saved = []
    for i in range(DEPTH):
        lw = layer_weights(i)
        h = _rms_fwd(xr, pre_norm_g[i:i + 1], lp)
        z = _mm(h, lw["w_in"], lp, ZW, D, tm=tmb, tn=256, tk=D, name="mm_in")
        qn, kvn, krr = _mla_prep(z, q_norm_g[i:i + 1], kv_norm_g[i:i + 1], tabs, lp)
        q_raw = _mm(qn, lw["w_uq"], lp, 1024, Q_RANK, tm=tmb, tn=1024, tk=Q_RANK, name="mm_uq")
        kv_raw = _mm(kvn, lw["w_ukv"], lp, 1536, KV_RANK, tm=tmb, tn=512, tk=KV_RANK, name="mm_ukv")
        qt, kt, vt = _mla_post(q_raw, kv_raw, krr, tabs, lp)
        oat, lse = _attn_fwd(qt, kt, vt, lp)
        u = _mix_fwd(z, oat, lw["bd"], pool_scale[i:i + 1], lw["cw"], conf_dw_b[i:i + 1], conf_ln_g[i:i + 1],
                     conf_ln_b[i:i + 1], lw["sw"], lp)
        x_new, m_act, o2 = _merge_fwd(xr, u, z, gate_bias[i:i + 1], lw["wout"], lw["w_o"],
                                      post_norm_g[i:i + 1], lp)
        saved.append(dict(lw=lw, x=xr, h=h, z=z, qn=qn, kvn=kvn, qt=qt, kt=kt, vt=vt, oat=oat, lse=lse,
                          u=u, m=m_act, o2=o2))
        xr = x_new

    dx, loss_part = _loss_head(xr, tgt, n_real, lp)

    gsm = {n: [None] * DEPTH for n, _ in SMALL}
    gbig = [None] * DEPTH
    g_cw = [None] * DEPTH
    g_sw = [None] * DEPTH
    for i in reversed(range(DEPTH)):
        s = saved[i]
        lw = s["lw"]
        do2, dgl, dyb, du, dgb, dgpost = _merge_bwd(dx, s["o2"], s["u"], s["z"], gate_bias[i:i + 1],
                                                    lw["wout"], lw["w_o"], post_norm_g[i:i + 1], lp)
        d_wo = _mm(s["m"], do2, D, D, lp, ta=True, tm=512, tn=D, tk=tmb, name="mm_dwo")
        d_wout = []
        for b in range(4):
            rows = U_OFF[b + 1] - U_OFF[b]
            d_wout.append(_mm(s["u"], dyb, rows, D, lp, ta=True, tm=256, tn=D, tk=tmb,
                              a_moff=U_OFF[b] // 256, b_noff=b, name="mm_dwout%d" % b))
        dza, dzm, dzc, doat, dbd, dcw, dsw, dsm = _mix_bwd(
            s["z"], s["oat"], du, lw["bd"], pool_scale[i:i + 1], lw["cw"], conf_dw_b[i:i + 1],
            conf_ln_g[i:i + 1], conf_ln_b[i:i + 1], lw["sw"], lp)
        dqt, dkt, dvt = _attn_bwd(s["qt"], s["kt"], s["vt"], s["oat"], doat, s["lse"], lp)
        dq_raw, dkv_raw, dkr = _mla_post_bwd(dqt, dkt, dvt, tabs, lp)
        dqn = _mm(dq_raw, lw["w_uq"], lp, Q_RANK, 1024, tb=True, tm=tmb, tn=Q_RANK, tk=1024, name="mm_dqn")
        d_wuq = _mm(s["qn"], dq_raw, Q_RANK, 1024, lp, ta=True, tm=Q_RANK, tn=1024, tk=tmb, name="mm_dwuq")
        dkvn = _mm(dkv_raw, lw["w_ukv"], lp, KV_RANK, 1536, tb=True, tm=tmb, tn=KV_RANK, tk=1536, name="mm_dkvn")
        d_wukv = _mm(s["kvn"], dkv_raw, KV_RANK, 1536, lp, ta=True, tm=KV_RANK, tn=1536, tk=tmb, name="mm_dwukv")
        dzq, dqg, dkvg = _mla_prep_bwd(s["z"], q_norm_g[i:i + 1], kv_norm_g[i:i + 1], dqn, dkvn, dkr, lp)
        dz = jnp.concatenate([dgl, dza, dzq, dzm, dzc], axis=1)
        dh = _mm(dz, lw["w_in"], lp, D, ZW, tb=True, tm=RB, tn=D, tk=ZW // 2, name="mm_dh")
        d_win = _mm(s["h"], dz, D, ZW, lp, ta=True, tm=D, tn=256, tk=tmb, name="mm_dwin")
        dx, dgpre = _rms_bwd(s["x"], pre_norm_g[i:i + 1], dh, dx, lp)

        gsm["pre_norm_g"][i] = dgpre[0]
        gsm["gate_bias"][i] = dgb[0]
        gsm["pool_w"][i] = jnp.stack([dbd[64 * g:64 * (g + 1), 64 * g:64 * (g + 1)] for g in range(4)])
        gsm["pool_scale"][i] = dsm[0]
        gsm["conf_dw_b"][i] = dsm[1]
        gsm["conf_ln_g"][i] = dsm[2]
        gsm["conf_ln_b"][i] = dsm[3]
        gsm["q_norm_g"][i] = dqg[0]
        gsm["kv_norm_g"][i] = dkvg[0]
        gsm["post_norm_g"][i] = dgpost[0]
        g_cw[i] = dcw[:CONF_K]
        g_sw[i] = dsw[:SC_K]
        d_wuq_o = d_wuq.reshape(Q_RANK, HEADS, LANE)[:, :, :96].reshape(Q_RANK, HEADS * 96)
        d_wukv_o = jnp.concatenate([d_wukv[:, :1024].reshape(KV_RANK, HEADS, LANE)[:, :, :QK_NOPE],
                                    d_wukv[:, 1024:].reshape(KV_RANK, HEADS, V_DIM)], axis=2).reshape(KV_RANK, 1024)
        gbig[i] = jnp.concatenate([
            _cols_by_dest(_unpack_w_in(d_win), 916), _cols_by_dest(d_wout[0], 128),
            _cols_by_dest(d_wuq_o, 96), _cols_by_dest(d_wukv_o, 128), _cols_by_dest(d_wout[1], 128),
            _cols_by_dest(d_wout[2], 128), _cols_by_dest(d_wout[3], 128),
            d_wo.reshape(N_DEV, -1, LANE)], axis=1).astype(BF16)

    g_meta = _cols_by_dest(dx[:N_META], 128)
    g_cwd = _cols_by_dest(jnp.stack(g_cw).reshape(DEPTH * CONF_K, 256), 32)
    g_swd = _cols_by_dest(jnp.stack(g_sw).reshape(DEPTH * SC_K, 256), 32)
    by_dest = jnp.concatenate(gbig + [g_meta.astype(BF16), g_cwd.astype(BF16), g_swd.astype(BF16),
                                      jnp.zeros((N_DEV, SHARD_PAD - SHARD_ROWS, LANE), BF16)], axis=1)
    recv = _all_to_all(by_dest, "scatter_grads")
    g_f, d_f, m_f, v_f = _reduce_adam(recv, w_flat, _pack_shards(mom), _pack_shards(vel), SHARD_BLK,
                                      "reduce_adam_shards")

    small_g = {n: jnp.stack(gsm[n]) for n, _ in SMALL}
    loss_row = jnp.concatenate([loss_part[0:1, 0:1], jnp.zeros((1, LANE - 1), F32)], axis=1)
    zrow = jnp.zeros((1, LANE), F32)
    parts = _all_gather(_pack_small(small_g, loss_row), "gather_small_grads")
    gs_f, ds_f, ms_f, vs_f = _reduce_adam(parts, _pack_small(w, zrow), _pack_small(mom, zrow),
                                          _pack_small(vel, zrow), SMALL_PAD, "reduce_adam_small")
    loss = gs_f[SMALL_ROWS, 0]

    outs = []
    for big_f, small_f in ((g_f, gs_f), (d_f, ds_f), (m_f, ms_f), (v_f, vs_f)):
        t = dict(_unpack_shards(big_f))
        t.update(_unpack_small(small_f))
        outs.append(t)
    order = ["meta_tokens", "pre_norm_g", "w_in", "gate_bias", "pool_w", "pool_scale", "w_out_pool", "q_norm_g",
             "w_uq", "kv_norm_g", "w_ukv", "w_out_mla", "conf_dw_w", "conf_dw_b", "conf_ln_g", "conf_ln_b",
             "w_out_conf", "sc_dw_w", "w_out_sc", "w_o", "post_norm_g"]
    grad_x = dx[N_META:n_real][None]
    return (loss, grad_x, *[t[n] for t in outs for n in order])
```

```python
import functools

import jax
import jax.numpy as jnp
import numpy as np
from jax import lax
from jax.experimental import pallas as pl
from jax.experimental.pallas import tpu as pltpu

F32 = jnp.float32
BF16 = jnp.bfloat16

D = 1024
N_META = 16
DEPTH = 4
EPS = 1e-6
HEADS = 8
QK_NOPE = 64
QK_ROPE = 32
V_DIM = 64
Q_RANK = 256
KV_RANK = 128
ROPE_THETA = 10000.0
SCALE = (QK_NOPE + QK_ROPE) ** -0.5
CONF_K = 31
SC_K = 3
N_DEV = 8

ADAM_LR = 0.001
ADAM_B1 = 0.9
ADAM_B2 = 0.999
ADAM_EPS = 1e-08
ADAM_WD = 0.01
ADAM_STEP = 10

RB = 384
HB = 32
LANE = 128
VMEM_LIMIT = 56 * 1024 * 1024

GL0, CQ0, CKV0, KR0, PV0, PG0, MG0, CU0, CG0, SBB0, SBC0, SBX0, SG0, ZW = (
    0, 4096, 4352, 4480, 4608, 4864, 5120, 5632, 6144, 6400, 6656, 6912, 7168, 7424)
ZSEG = ((0, 4096), (4096, 512), (4608, 2816))
LOG2E = 1.4426950408889634
LN2 = 0.6931471805599453

MESH = pl.DeviceIdType.MESH


def _cp(**kw):
    return pltpu.CompilerParams(vmem_limit_bytes=VMEM_LIMIT, **kw)


def _sig(x):
    return jax.nn.sigmoid(x)


def _silu(x):
    return x * _sig(x)


def _dsilu(x):
    s = _sig(x)
    return s * (1.0 + x * (1.0 - s))


def _dn(x, k):
    return x if k == 0 else pltpu.roll(x, k, 0)


def _up(x, k):
    return x if k == 0 else pltpu.roll(x, x.shape[0] - k, 0)


def _rope(t, c, s1, s2):
    return t * c + pltpu.roll(t, 16, 1) * s1 + pltpu.roll(t, LANE - 16, 1) * s2


def _rope_t(g, c, s1, s2):
    return g * c + pltpu.roll(g * s1, LANE - 16, 1) + pltpu.roll(g * s2, 16, 1)


def _mm(a, b, m, n, k, *, ta=False, tb=False, out_dtype=F32, tm, tn, tk, name,
        a_moff=0, a_koff=0, b_noff=0, b_koff=0, c=None):
    assert m % tm == 0 and n % tn == 0 and k % tk == 0, (name, m, n, k, tm, tn, tk)
    nk = k // tk
    dims = (((0,) if ta else (1,), (1,) if tb else (0,)), ((), ()))
    has_c = c is not None

    def body(a_ref, b_ref, *rest):
        c_ref = rest[0] if has_c else None
        o_ref = rest[1] if has_c else rest[0]
        scr = rest[2:] if has_c else rest[1:]
        part = lax.dot_general(a_ref[...].astype(BF16), b_ref[...].astype(BF16), dims,
                               preferred_element_type=F32)

        def finish(total):
            if has_c:
                total = total + c_ref[...]
            o_ref[...] = total.astype(out_dtype)

        if nk == 1:
            finish(part)
        else:
            acc = scr[0]
            kk = pl.program_id(2)

            @pl.when(kk == 0)
            def _():
                acc[...] = part

            @pl.when(kk > 0)
            def _():
                acc[...] += part

            @pl.when(kk == nk - 1)
            def _():
                finish(acc[...])

    if ta:
        a_spec = pl.BlockSpec((tk, tm), lambda i, j, q: (q + a_koff, i + a_moff))
    else:
        a_spec = pl.BlockSpec((tm, tk), lambda i, j, q: (i + a_moff, q + a_koff))
    if tb:
        b_spec = pl.BlockSpec((tn, tk), lambda i, j, q: (j + b_noff, q + b_koff))
    else:
        b_spec = pl.BlockSpec((tk, tn), lambda i, j, q: (q + b_koff, j + b_noff))
    o_spec = pl.BlockSpec((tm, tn), lambda i, j, q: (i, j))
    return pl.pallas_call(
        body, grid=(m // tm, n // tn, nk), in_specs=[a_spec, b_spec] + ([o_spec] if has_c else []),
        out_specs=o_spec, out_shape=jax.ShapeDtypeStruct((m, n), out_dtype),
        scratch_shapes=[pltpu.VMEM((tm, tn), F32)] if nk > 1 else [],
        name=name, compiler_params=_cp())(*((a, b, c) if has_c else (a, b)))


def _rms_fwd(x, g, lp):
    def body(x_ref, g_ref, h_ref):
        xv = x_ref[...]
        r = lax.rsqrt(jnp.mean(xv * xv, axis=-1, keepdims=True) + EPS)
        h_ref[...] = (xv * r * g_ref[...]).astype(BF16)

    return pl.pallas_call(
        body, grid=(lp // RB,),
        in_specs=[pl.BlockSpec((RB, D), lambda i: (i, 0)), pl.BlockSpec((1, D), lambda i: (0, 0))],
        out_specs=pl.BlockSpec((RB, D), lambda i: (i, 0)),
        out_shape=jax.ShapeDtypeStruct((lp, D), BF16), name="rms_fwd", compiler_params=_cp())(x, g)


def _rms_bwd(x, g, dh, dx_in, lp):
    def body(x_ref, g_ref, dh_ref, dxi_ref, dx_ref, dg_ref):
        i = pl.program_id(0)
        xv = x_ref[...]
        r = lax.rsqrt(jnp.mean(xv * xv, axis=-1, keepdims=True) + EPS)
        dy = dh_ref[...]
        a = dy * g_ref[...]
        dx_ref[...] = dxi_ref[...] + r * a - xv * (r * r * r) * jnp.mean(a * xv, axis=-1, keepdims=True)
        part = jnp.sum(dy * xv * r, axis=0, keepdims=True)

        @pl.when(i == 0)
        def _():
            dg_ref[...] = part

        @pl.when(i > 0)
        def _():
            dg_ref[...] += part

    blk = pl.BlockSpec((RB, D), lambda i: (i, 0))
    vec = pl.BlockSpec((1, D), lambda i: (0, 0))
    return pl.pallas_call(
        body, grid=(lp // RB,), in_specs=[blk, vec, blk, blk], out_specs=[blk, vec],
        out_shape=[jax.ShapeDtypeStruct((lp, D), F32), jax.ShapeDtypeStruct((1, D), F32)],
        name="rms_bwd", compiler_params=_cp())(x, g, dh, dx_in)


def _mla_prep(z, qg, kvg, tabs, lp):
    def body(z_ref, qg_ref, kvg_ref, c_ref, s1_ref, s2_ref, qn_ref, kvn_ref, kr_ref):
        cq = z_ref[:, 0:256]
        ckv = z_ref[:, 256:384]
        kr = z_ref[:, 384:512]
        rq = lax.rsqrt(jnp.mean(cq * cq, axis=-1, keepdims=True) + EPS)
        rk = lax.rsqrt(jnp.mean(ckv * ckv, axis=-1, keepdims=True) + EPS)
        qn_ref[...] = (cq * rq * qg_ref[...]).astype(BF16)
        kvn_ref[...] = (ckv * rk * kvg_ref[...]).astype(BF16)
        kr_ref[...] = _rope(kr, c_ref[...], s1_ref[...], s2_ref[...])

    tab = pl.BlockSpec((RB, LANE), lambda i: (i, 0))
    return pl.pallas_call(
        body, grid=(lp // RB,),
        in_specs=[pl.BlockSpec((RB, 512), lambda i: (i, CQ0 // 512)),
                  pl.BlockSpec((1, 256), lambda i: (0, 0)), pl.BlockSpec((1, 128), lambda i: (0, 0)),
                  tab, tab, tab],
        out_specs=[pl.BlockSpec((RB, 256), lambda i: (i, 0)), tab, tab],
        out_shape=[jax.ShapeDtypeStruct((lp, 256), BF16), jax.ShapeDtypeStruct((lp, 128), BF16),
                   jax.ShapeDtypeStruct((lp, 128), F32)],
        name="mla_prep", compiler_params=_cp())(z, qg, kvg, *tabs)


def _mla_prep_bwd(z, qg, kvg, dqn, dkvn, dkr, lp):
    def body(z_ref, qg_ref, kvg_ref, dqn_ref, dkvn_ref, dkr_ref, dz_ref, dqg_ref, dkvg_ref):
        i = pl.program_id(0)

        def rms_b(xv, g, dy):
            r = lax.rsqrt(jnp.mean(xv * xv, axis=-1, keepdims=True) + EPS)
            a = dy * g
            dx = r * a - xv * (r * r * r) * jnp.mean(a * xv, axis=-1, keepdims=True)
            return dx, jnp.sum(dy * xv * r, axis=0, keepdims=True)

        dcq, pq = rms_b(z_ref[:, 0:256], qg_ref[...], dqn_ref[...])
        dckv, pk = rms_b(z_ref[:, 256:384], kvg_ref[...], dkvn_ref[...])
        dz_ref[:, 0:256] = dcq.astype(BF16)
        dz_ref[:, 256:384] = dckv.astype(BF16)
        dz_ref[:, 384:512] = dkr_ref[...].astype(BF16)

        @pl.when(i == 0)
        def _():
            dqg_ref[...] = pq
            dkvg_ref[...] = pk

        @pl.when(i > 0)
        def _():
            dqg_ref[...] += pq
            dkvg_ref[...] += pk

    tab = pl.BlockSpec((RB, LANE), lambda i: (i, 0))
    return pl.pallas_call(
        body, grid=(lp // RB,),
        in_specs=[pl.BlockSpec((RB, 512), lambda i: (i, CQ0 // 512)),
                  pl.BlockSpec((1, 256), lambda i: (0, 0)), pl.BlockSpec((1, 128), lambda i: (0, 0)),
                  pl.BlockSpec((RB, 256), lambda i: (i, 0)), tab, tab],
        out_specs=[pl.BlockSpec((RB, 512), lambda i: (i, 0)),
                   pl.BlockSpec((1, 256), lambda i: (0, 0)), pl.BlockSpec((1, 128), lambda i: (0, 0))],
        out_shape=[jax.ShapeDtypeStruct((lp, 512), BF16), jax.ShapeDtypeStruct((1, 256), F32),
                   jax.ShapeDtypeStruct((1, 128), F32)],
        name="mla_prep_bwd", compiler_params=_cp())(z, qg, kvg, dqn, dkvn, dkr)


def _mla_post(q_raw, kv_raw, krr, tabs, lp):
    def body(q_ref, kv_ref, kr_ref, c_ref, s1_ref, s2_ref, qo_ref, ko_ref, vo_ref):
        c, s1, s2, kr = c_ref[...], s1_ref[...], s2_ref[...], kr_ref[...]
        for h in range(HEADS):
            sl = slice(LANE * h, LANE * (h + 1))
            qo_ref[:, sl] = (_rope(q_ref[:, sl], c, s1, s2) * (SCALE * LOG2E)).astype(BF16)
            ko_ref[:, sl] = (kv_ref[:, sl] + kr).astype(BF16)
        vo_ref[...] = kv_ref[:, 1024:1536].astype(BF16)

    tab = pl.BlockSpec((RB, LANE), lambda i: (i, 0))
    wide = pl.BlockSpec((RB, 1024), lambda i: (i, 0))
    return pl.pallas_call(
        body, grid=(lp // RB,),
        in_specs=[wide, pl.BlockSpec((RB, 1536), lambda i: (i, 0)), tab, tab, tab, tab],
        out_specs=[wide, wide, pl.BlockSpec((RB, 512), lambda i: (i, 0))],
        out_shape=[jax.ShapeDtypeStruct((lp, 1024), BF16), jax.ShapeDtypeStruct((lp, 1024), BF16),
                   jax.ShapeDtypeStruct((lp, 512), BF16)],
        name="mla_post", compiler_params=_cp())(q_raw, kv_raw, krr, *tabs)


def _mla_post_bwd(dq, dk, dv, tabs, lp):
    def body(dq_ref, dk_ref, dv_ref, c_ref, s1_ref, s2_ref, dqr_ref, dkv_ref, dkr_ref):
        c, s1, s2 = c_ref[...], s1_ref[...], s2_ref[...]
        lane = lax.broadcasted_iota(jnp.int32, (1, LANE), 1)
        ropel = (lane >= QK_NOPE) & (lane < QK_NOPE + QK_ROPE)
        ksum = jnp.zeros((RB, LANE), F32)
        for h in range(HEADS):
            sl = slice(LANE * h, LANE * (h + 1))
            dqr_ref[:, sl] = _rope_t(dq_ref[:, sl].astype(F32) * SCALE, c, s1, s2).astype(BF16)
            dkt = dk_ref[:, sl]
            dkv_ref[:, sl] = dkt
            ksum = ksum + dkt.astype(F32)
        dkv_ref[:, 1024:1536] = dv_ref[...]
        dkr_ref[...] = jnp.where(ropel, _rope_t(jnp.where(ropel, ksum, 0.0), c, s1, s2), 0.0)

    tab = pl.BlockSpec((RB, LANE), lambda i: (i, 0))
    wide = pl.BlockSpec((RB, 1024), lambda i: (i, 0))
    return pl.pallas_call(
        body, grid=(lp // RB,),
        in_specs=[wide, wide, pl.BlockSpec((RB, 512), lambda i: (i, 0)), tab, tab, tab],
        out_specs=[wide, pl.BlockSpec((RB, 1536), lambda i: (i, 0)), tab],
        out_shape=[jax.ShapeDtypeStruct((lp, 1024), BF16), jax.ShapeDtypeStruct((lp, 1536), BF16),
                   jax.ShapeDtypeStruct((lp, 128), F32)],
        name="mla_post_bwd", compiler_params=_cp())(dq, dk, dv, *tabs)


def _head_lanes(e):
    lane = lax.broadcasted_iota(jnp.int32, (1, LANE), 1)
    return lane >= V_DIM if e else lane < V_DIM


ONE_LANE = (V_DIM, 0)


def _attn_fwd(q, k, v, lp):
    nq = lp // RB

    def body(q_ref, k_ref, v_ref, o_ref, lse_ref, vm_scr):
        lane = lax.broadcasted_iota(jnp.int32, (1, LANE), 1)
        vv = v_ref[...]
        for e in range(2):
            ones = jnp.where(lane == ONE_LANE[e], 1.0, 0.0).astype(BF16)
            vm_scr[e] = jnp.where(_head_lanes(e), vv, jnp.broadcast_to(ones, vv.shape))
        causal = (lax.broadcasted_iota(jnp.int32, (RB, RB), 1) <= lax.broadcasted_iota(jnp.int32, (RB, RB), 0))

        def qblock(i, _):
            rows = pl.ds(pl.multiple_of(i * RB, RB), RB)
            qs = [q_ref[rows, LANE * e:LANE * (e + 1)] for e in range(2)]

            def step(j, carry, masked):
                cols = pl.ds(pl.multiple_of(j * RB, RB), RB)
                out = []
                for e in range(2):
                    m, acc = carry[2 * e], carry[2 * e + 1]
                    s = lax.dot_general(qs[e], k_ref[cols, LANE * e:LANE * (e + 1)], (((1,), (1,)), ((), ())),
                                        preferred_element_type=F32)
                    if masked:
                        s = jnp.where(causal, s, -jnp.inf)
                    m_new = jnp.maximum(m, jnp.max(s, axis=-1, keepdims=True))
                    p = jnp.exp2(s - m_new).astype(BF16)
                    acc = jnp.exp2(m - m_new) * acc + jnp.dot(p, vm_scr[e, cols, :], preferred_element_type=F32)
                    out += [m_new, acc]
                return tuple(out)

            m0 = jnp.full((RB, 1), -jnp.inf, F32)
            a0 = jnp.zeros((RB, LANE), F32)
            carry = lax.fori_loop(0, i, lambda j, c: step(j, c, False), (m0, a0, m0, a0))
            carry = step(i, carry, True)
            o, lse = [], []
            for e in range(2):
                m, acc = carry[2 * e], carry[2 * e + 1]
                l = acc[:, ONE_LANE[e]:ONE_LANE[e] + 1]
                o.append(acc / l)
                lse.append(jnp.broadcast_to(m + jnp.log2(l), (RB, LANE)))
            o_ref[rows, :] = jnp.where(_head_lanes(0), o[0], o[1])
            lse_ref[rows, :] = jnp.where(_head_lanes(0), lse[0], lse[1])
            return 0

        lax.fori_loop(0, nq, qblock, 0)

    two = pl.BlockSpec((lp, 2 * LANE), lambda h: (0, h))
    one = pl.BlockSpec((lp, LANE), lambda h: (0, h))
    return pl.pallas_call(
        body, grid=(HEADS // 2,), in_specs=[two, two, one], out_specs=[one, one],
        out_shape=[jax.ShapeDtypeStruct((lp, 512), F32), jax.ShapeDtypeStruct((lp, 512), F32)],
        scratch_shapes=[pltpu.VMEM((2, lp, LANE), BF16)],
        name="attn_fwd", compiler_params=_cp())(q, k, v)


def _attn_bwd(q, k, v, o, do, lse, lp):
    nq = lp // RB

    def body(q_ref, k_ref, v_ref, o_ref, do_ref, lse_ref, dq_ref, dk_ref, dv_ref,
             vm_scr, dom_scr, dl_scr, dq_scr):
        causal = (lax.broadcasted_iota(jnp.int32, (RB, RB), 1) <= lax.broadcasted_iota(jnp.int32, (RB, RB), 0))
        vv = v_ref[...]
        for e in range(2):
            vm_scr[e] = jnp.where(_head_lanes(e), vv, jnp.zeros_like(vv))

        def prep(i, _):
            rows = pl.ds(pl.multiple_of(i * RB, RB), RB)
            prod = do_ref[rows, :] * o_ref[rows, :]
            dls = []
            for e in range(2):
                hm = _head_lanes(e)
                dom_scr[e, rows, :] = jnp.where(hm, do_ref[rows, :], 0.0).astype(BF16)
                dls.append(jnp.sum(jnp.where(hm, prod, 0.0), axis=-1, keepdims=True))
            dl_scr[rows, :] = jnp.where(_head_lanes(0), dls[0], dls[1])
            dq_scr[rows, :] = jnp.zeros((RB, 2 * LANE), F32)
            return 0

        lax.fori_loop(0, nq, prep, 0)

        def kvblock(j, _):
            cols = pl.ds(pl.multiple_of(j * RB, RB), RB)
            kbs = [k_ref[cols, LANE * e:LANE * (e + 1)] for e in range(2)]

            def step(i, carry, masked):
                rows = pl.ds(pl.multiple_of(i * RB, RB), RB)
                out = []
                for e in range(2):
                    dk, dv = carry[2 * e], carry[2 * e + 1]
                    sl = slice(LANE * e, LANE * (e + 1))
                    col1 = slice(V_DIM * e, V_DIM * e + 1)
                    qb = q_ref[rows, sl]
                    dob = dom_scr[e, rows, :]
                    s = lax.dot_general(qb, kbs[e], (((1,), (1,)), ((), ())), preferred_element_type=F32)
                    if masked:
                        s = jnp.where(causal, s, -jnp.inf)
                    p = jnp.exp2(s - lse_ref[rows, col1])
                    dv = dv + lax.dot_general(p.astype(BF16), dob, (((0,), (0,)), ((), ())),
                                              preferred_element_type=F32)
                    dp = lax.dot_general(dob, vm_scr[e, cols, :], (((1,), (1,)), ((), ())),
                                         preferred_element_type=F32)
                    ds = (p * (dp - dl_scr[rows, col1])).astype(BF16)
                    dk = dk + lax.dot_general(ds, qb, (((0,), (0,)), ((), ())), preferred_element_type=F32)
                    dq_scr[rows, sl] += jnp.dot(ds, kbs[e], preferred_element_type=F32)
                    out += [dk, dv]
                return tuple(out)

            zero = jnp.zeros((RB, LANE), F32)
            carry = step(j, (zero, zero, zero, zero), True)
            dk0, dv0, dk1, dv1 = lax.fori_loop(j + 1, nq, lambda i, c: step(i, c, False), carry)
            dk_ref[cols, 0:LANE] = (dk0 * LN2).astype(BF16)
            dk_ref[cols, LANE:2 * LANE] = (dk1 * LN2).astype(BF16)
            dv_ref[cols, :] = (dv0 + dv1).astype(BF16)
            return 0

        lax.fori_loop(0, nq, kvblock, 0)

        def fin(i, _):
            rows = pl.ds(pl.multiple_of(i * RB, RB), RB)
            dq_ref[rows, :] = dq_scr[rows, :].astype(BF16)
            return 0

        lax.fori_loop(0, nq, fin, 0)

    two = pl.BlockSpec((lp, 2 * LANE), lambda h: (0, h))
    one = pl.BlockSpec((lp, LANE), lambda h: (0, h))
    return pl.pallas_call(
        body, grid=(HEADS // 2,), in_specs=[two, two, one, one, one, one], out_specs=[two, two, one],
        out_shape=[jax.ShapeDtypeStruct((lp, 1024), BF16), jax.ShapeDtypeStruct((lp, 1024), BF16),
                   jax.ShapeDtypeStruct((lp, 512), BF16)],
        scratch_shapes=[pltpu.VMEM((2, lp, LANE), BF16), pltpu.VMEM((2, lp, LANE), BF16),
                        pltpu.VMEM((lp, LANE), F32), pltpu.VMEM((lp, 2 * LANE), F32)],
        name="attn_bwd", compiler_params=_cp())(q, k, v, o, do, lse)


def _pool_lane_windows():
    lane = lax.broadcasted_iota(jnp.int32, (1, 256), 1)
    return jnp.where(lane < 64, 2, jnp.where(lane < 128, 4, jnp.where(lane < 192, 8, 16)))


def _by_window(wl, s2, s4, s8, s16):
    return jnp.where(wl == 2, s2, jnp.where(wl == 4, s4, jnp.where(wl == 8, s8, s16)))


def _pool_fwd_rows(pv_ext, t0):
    n = pv_ext.shape[0]
    wl = _pool_lane_windows()
    s2 = pv_ext + _dn(pv_ext, 1)
    s4 = s2 + _dn(s2, 2)
    s8 = s4 + _dn(s4, 4)
    s16 = s8 + _dn(s8, 8)
    t = t0 + lax.broadcasted_iota(jnp.int32, (n, 1), 0)
    cnt = jnp.maximum(jnp.minimum(t + 1, wl), 1).astype(F32)
    return _by_window(wl, s2, s4, s8, s16) / cnt - pv_ext


def _conv_dn(x_ext, w_ref, taps):
    acc = w_ref[taps - 1:taps, :] * x_ext
    for j in range(1, taps):
        acc = acc + w_ref[taps - 1 - j:taps - j, :] * _dn(x_ext, j)
    return acc


def _conv_up(g_ext, w_ref, taps):
    acc = w_ref[taps - 1:taps, :] * g_ext
    for j in range(1, taps):
        acc = acc + w_ref[taps - 1 - j:taps - j, :] * _up(g_ext, j)
    return acc


def _ln_fwd(c, g, b):
    mu = jnp.mean(c, axis=-1, keepdims=True)
    xc = c - mu
    r = lax.rsqrt(jnp.mean(xc * xc, axis=-1, keepdims=True) + EPS)
    xh = xc * r
    return xh * g + b, xh, r


def _halo_specs(lp, width, col):
    per = RB // HB
    last = lp // HB - 1
    cur = pl.BlockSpec((RB, width), lambda i: (i, col))
    prev = pl.BlockSpec((HB, width), lambda i: (jnp.maximum(i * per - 1, 0), col))
    nxt = pl.BlockSpec((HB, width), lambda i: (jnp.minimum((i + 1) * per, last), col))
    return cur, prev, nxt


def _mix_fwd(z, oat, bd, pscale, cw, cb, lng, lnb, sw, lp):
    def body(za, zah, mg, oat_ref, cu, cuh, cg, sbb, sbc, sbch, sbx, sbxh, sg,
             bd_ref, ps_ref, cw_ref, cb_ref, lng_ref, lnb_ref, sw_ref, u_ref):
        i = pl.program_id(0)
        pm = jnp.where(i > 0, 1.0, 0.0).astype(F32)
        pv = jnp.concatenate([zah[:, 0:256] * pm, za[:, 0:256]], axis=0)
        p = _pool_fwd_rows(pv, i * RB - HB)[HB:]
        y = jnp.dot(p.astype(BF16), bd_ref[...], preferred_element_type=F32)
        u_ref[:, 0:256] = (y * ps_ref[...] * _silu(za[:, 256:512])).astype(BF16)
        u_ref[:, 256:768] = (oat_ref[...] * _silu(mg[...])).astype(BF16)
        ce = jnp.concatenate([cuh[...] * pm, cu[...]], axis=0)
        glu = ce[:, 0:256] * _sig(ce[:, 256:512])
        c = _conv_dn(glu, cw_ref, CONF_K)[HB:] + cb_ref[...]
        n, _, _ = _ln_fwd(c, lng_ref[...], lnb_ref[...])
        u_ref[:, 768:1024] = (_silu(n) * _silu(cg[...])).astype(BF16)
        qe = jnp.concatenate([sbch[...] * sbxh[...] * pm, sbc[...] * sbx[...]], axis=0)
        cv = _conv_dn(qe, sw_ref, SC_K)[HB:]
        u_ref[:, 1024:1280] = (sbb[...] * cv * _silu(sg[...])).astype(BF16)

    a_cur, a_prev, _ = _halo_specs(lp, 512, PV0 // 512)
    cu_cur, cu_prev, _ = _halo_specs(lp, 512, CU0 // 512)
    sc_cur, sc_prev, _ = _halo_specs(lp, 256, SBC0 // 256)
    sx_cur, sx_prev, _ = _halo_specs(lp, 256, SBX0 // 256)
    c256 = lambda c0: pl.BlockSpec((RB, 256), lambda i: (i, c0 // 256))
    full = lambda r, c: pl.BlockSpec((r, c), lambda i: (0, 0))
    return pl.pallas_call(
        body, grid=(lp // RB,),
        in_specs=[a_cur, a_prev, pl.BlockSpec((RB, 512), lambda i: (i, MG0 // 512)),
                  pl.BlockSpec((RB, 512), lambda i: (i, 0)),
                  cu_cur, cu_prev, c256(CG0), c256(SBB0), sc_cur, sc_prev, sx_cur, sx_prev, c256(SG0),
                  full(256, 256), full(1, 256), full(32, 256), full(1, 256), full(1, 256), full(1, 256),
                  full(8, 256)],
        out_specs=pl.BlockSpec((RB, 1280), lambda i: (i, 0)),
        out_shape=jax.ShapeDtypeStruct((lp, 1280), BF16),
        name="mix_fwd", compiler_params=_cp())(z, z, z, oat, z, z, z, z, z, z, z, z, z,
                                               bd, pscale, cw, cb, lng, lnb, sw)


def _mix_bwd(z, oat, du, bd, pscale, cw, cb, lng, lnb, sw, lp):
    nb = lp // RB
    ne = RB + 2 * HB
    nf = RB + HB

    def body(za, zah, zan, mg, oat_ref, cu, cuh, cun, cg, cgn, sbb, sbbn, sbc, sbch, sbcn, sbx, sbxh, sbxn,
             sg, sgn, du_ref, dun_ref, bd_ref, ps_ref, cw_ref, cb_ref, lng_ref, lnb_ref, sw_ref,
             dzx_ref, doat_ref, dbd_ref, dcw_ref, dsw_ref, dsm_ref):
        xa, xm, xc = 0, MG0 - PV0, CU0 - PV0
        i = pl.program_id(0)
        pm = jnp.where(i > 0, 1.0, 0.0).astype(F32)
        nm = jnp.where(i < nb - 1, 1.0, 0.0).astype(F32)

        def ext(cur, prev, nxt, sl=slice(None)):
            return jnp.concatenate([prev[:, sl] * pm, cur[:, sl], nxt[:, sl] * nm], axis=0)

        def fwd(cur, nxt, sl=slice(None)):
            return jnp.concatenate([cur[:, sl], nxt[:, sl] * nm], axis=0)

        def csum(x):
            return jnp.sum(x, axis=0, keepdims=True)

        @pl.when(i == 0)
        def _():
            dbd_ref[...] = jnp.zeros((256, 256), F32)
            dcw_ref[...] = jnp.zeros((32, 256), F32)
            dsw_ref[...] = jnp.zeros((8, 256), F32)
            dsm_ref[...] = jnp.zeros((8, 256), F32)

        a_cols, b_cols = slice(0, 256), slice(256, 512)
        pv_e = ext(za, zah, zan, a_cols)
        p = _pool_fwd_rows(pv_e, i * RB - HB)[HB:HB + RB]
        pb = p.astype(BF16)
        y = jnp.dot(pb, bd_ref[...], preferred_element_type=F32)
        pg_f = fwd(za, zan, b_cols)
        dua_f = fwd(du_ref, dun_ref, slice(0, 256))
        dyp_f = dua_f * ps_ref[...] * _silu(pg_f)
        dypb = dyp_f.astype(BF16)
        dp_f = lax.dot_general(dypb, bd_ref[...], (((1,), (1,)), ((), ())), preferred_element_type=F32)
        wl = _pool_lane_windows()
        t = i * RB + lax.broadcasted_iota(jnp.int32, (nf, 1), 0)
        cnt = jnp.minimum(t + 1, wl).astype(F32)
        qf = dp_f / cnt
        f2 = qf + _up(qf, 1)
        f4 = f2 + _up(f2, 2)
        f8 = f4 + _up(f4, 4)
        f16 = f8 + _up(f8, 8)
        dpv = (_by_window(wl, f2, f4, f8, f16) - dp_f)[0:RB]
        dua = du_ref[:, 0:256]
        pg = za[:, b_cols]
        dpg = dua * y * ps_ref[...] * _dsilu(pg)
        dzx_ref[:, xa:xa + 256] = dpv.astype(BF16)
        dzx_ref[:, xa + 256:xa + 512] = dpg.astype(BF16)
        d_scale = csum(dua * y * _silu(pg))
        d_bd = lax.dot_general(pb, dypb[0:RB], (((0,), (0,)), ((), ())), preferred_element_type=F32)

        dub = du_ref[:, 256:768]
        mgv = mg[...]
        dzx_ref[:, xm:xm + 512] = (dub * oat_ref[...] * _dsilu(mgv)).astype(BF16)
        doat_ref[...] = dub * _silu(mgv)

        a_e = ext(cu, cuh, cun, slice(0, 256))
        gt_e = ext(cu, cuh, cun, slice(256, 512))
        sg_e = _sig(gt_e)
        glu_e = a_e * sg_e
        c_f = _conv_dn(glu_e, cw_ref, CONF_K)[HB:] + cb_ref[...]
        n_f, xh_f, r_f = _ln_fwd(c_f, lng_ref[...], lnb_ref[...])
        cg_f = fwd(cg, cgn)
        duc_f = fwd(du_ref, dun_ref, slice(768, 1024))
        sw_f = _silu(n_f)
        dcg = (duc_f * sw_f * _dsilu(cg_f))[0:RB]
        dn_f = duc_f * _silu(cg_f) * _dsilu(n_f)
        a_f = dn_f * lng_ref[...]
        dc_f = r_f * (a_f - jnp.mean(a_f, axis=-1, keepdims=True)
                      - xh_f * jnp.mean(a_f * xh_f, axis=-1, keepdims=True))
        d_lng = csum((dn_f * xh_f)[0:RB])
        d_lnb = csum(dn_f[0:RB])
        d_cb = csum(dc_f[0:RB])
        dglu = _conv_up(dc_f, cw_ref, CONF_K)[0:RB]
        dc_c = dc_f[0:RB]
        for kk in range(CONF_K):
            j = CONF_K - 1 - kk
            dcw_ref[kk:kk + 1, :] += csum(dc_c * _dn(glu_e, j)[HB:HB + RB])

        sgc = sg_e[HB:HB + RB]
        a_c = a_e[HB:HB + RB]
        dzx_ref[:, xc:xc + 256] = (dglu * sgc).astype(BF16)
        dzx_ref[:, xc + 256:xc + 512] = (dglu * a_c * sgc * (1.0 - sgc)).astype(BF16)
        dzx_ref[:, xc + 512:xc + 768] = dcg.astype(BF16)

        c_e = ext(sbc, sbch, sbcn)
        x_e = ext(sbx, sbxh, sbxn)
        q_e = c_e * x_e
        cv_f = _conv_dn(q_e, sw_ref, SC_K)[HB:]
        bg_f = fwd(sbb, sbbn)
        sg_f = fwd(sg, sgn)
        dud_f = fwd(du_ref, dun_ref, slice(1024, 1280))
        ssg_f = _silu(sg_f)
        dcv_f = dud_f * bg_f * ssg_f
        dbg = (dud_f * cv_f * ssg_f)[0:RB]
        dsg = (dud_f * bg_f * cv_f * _dsilu(sg_f))[0:RB]
        dq = _conv_up(dcv_f, sw_ref, SC_K)[0:RB]
        dcv_c = dcv_f[0:RB]
        for kk in range(SC_K):
            j = SC_K - 1 - kk
            dsw_ref[kk:kk + 1, :] += csum(dcv_c * _dn(q_e, j)[HB:HB + RB])

        dzx_ref[:, xc + 768:xc + 1024] = dbg.astype(BF16)
        dzx_ref[:, xc + 1024:xc + 1280] = (dq * x_e[HB:HB + RB]).astype(BF16)
        dzx_ref[:, xc + 1280:xc + 1536] = (dq * c_e[HB:HB + RB]).astype(BF16)
        dzx_ref[:, xc + 1536:xc + 1792] = dsg.astype(BF16)

        dbd_ref[...] += d_bd
        dsm_ref[0:1, :] += d_scale
        dsm_ref[1:2, :] += d_cb
        dsm_ref[2:3, :] += d_lng
        dsm_ref[3:4, :] += d_lnb

    a3 = _halo_specs(lp, 512, PV0 // 512)
    cu3 = _halo_specs(lp, 512, CU0 // 512)
    cg3 = _halo_specs(lp, 256, CG0 // 256)
    sbb3 = _halo_specs(lp, 256, SBB0 // 256)
    sbc3 = _halo_specs(lp, 256, SBC0 // 256)
    sbx3 = _halo_specs(lp, 256, SBX0 // 256)
    sg3 = _halo_specs(lp, 256, SG0 // 256)
    du3 = _halo_specs(lp, 1280, 0)
    full = lambda r, c: pl.BlockSpec((r, c), lambda i: (0, 0))
    in_specs = [a3[0], a3[1], a3[2], pl.BlockSpec((RB, 512), lambda i: (i, MG0 // 512)),
                pl.BlockSpec((RB, 512), lambda i: (i, 0)),
                cu3[0], cu3[1], cu3[2], cg3[0], cg3[2], sbb3[0], sbb3[2],
                sbc3[0], sbc3[1], sbc3[2], sbx3[0], sbx3[1], sbx3[2], sg3[0], sg3[2],
                du3[0], du3[2],
                full(256, 256), full(1, 256), full(32, 256), full(1, 256), full(1, 256), full(1, 256),
                full(8, 256)]
    out_specs = [pl.BlockSpec((RB, ZW - PV0), lambda i: (i, 0)), pl.BlockSpec((RB, 512), lambda i: (i, 0)),
                 full(256, 256), full(32, 256), full(8, 256), full(8, 256)]
    out_shape = [jax.ShapeDtypeStruct((lp, ZW - PV0), BF16), jax.ShapeDtypeStruct((lp, 512), F32),
                 jax.ShapeDtypeStruct((256, 256), F32), jax.ShapeDtypeStruct((32, 256), F32),
                 jax.ShapeDtypeStruct((8, 256), F32), jax.ShapeDtypeStruct((8, 256), F32)]
    return pl.pallas_call(
        body, grid=(nb,), in_specs=in_specs, out_specs=out_specs, out_shape=out_shape,
        name="mix_bwd", compiler_params=_cp())(
            z, z, z, z, oat, z, z, z, z, z, z, z, z, z, z, z, z, z, z, z, du, du,
            bd, pscale, cw, cb, lng, lnb, sw)


U_OFF = (0, 256, 768, 1024, 1280)
MRB = 192


def _merge_fwd(x, u, z, gb, wout, wo, gpost, lp):
    def body(x_ref, u_ref, gl_ref, gb_ref, wout_ref, wo_ref, g_ref, xo_ref, m_ref, o2_ref):
        m = jnp.zeros((MRB, D), F32)
        for b in range(4):
            y = jnp.dot(u_ref[:, U_OFF[b]:U_OFF[b + 1]], wout_ref[U_OFF[b]:U_OFF[b + 1], :],
                        preferred_element_type=F32)
            sl = slice(D * b, D * (b + 1))
            m = m + _sig(gl_ref[:, sl] + gb_ref[:, sl]) * y
        mb = m.astype(BF16)
        m_ref[...] = mb
        o2 = jnp.dot(mb, wo_ref[...], preferred_element_type=F32)
        o2_ref[...] = o2
        r = lax.rsqrt(jnp.mean(o2 * o2, axis=-1, keepdims=True) + EPS)
        xo_ref[...] = x_ref[...] + o2 * r * g_ref[...]

    blk = pl.BlockSpec((MRB, D), lambda i: (i, 0))
    full = lambda r, c: pl.BlockSpec((r, c), lambda i: (0, 0))
    return pl.pallas_call(
        body, grid=(lp // MRB,),
        in_specs=[blk, pl.BlockSpec((MRB, 1280), lambda i: (i, 0)), pl.BlockSpec((MRB, 4096), lambda i: (i, 0)),
                  full(1, 4096), full(1280, D), full(D, D), full(1, D)],
        out_specs=[blk, blk, blk],
        out_shape=[jax.ShapeDtypeStruct((lp, D), F32), jax.ShapeDtypeStruct((lp, D), BF16),
                   jax.ShapeDtypeStruct((lp, D), F32)],
        name="merge_fwd", compiler_params=_cp())(x, u, z, gb, wout, wo, gpost)


def _merge_bwd(dx, o2, u, z, gb, wout, wo, gpost, lp):
    def body(dx_ref, o2_ref, u_ref, gl_ref, gb_ref, wout_ref, wo_ref, g_ref,
             do2_ref, dgl_ref, dy_ref, du_ref, dgb_ref, dg_ref):
        i = pl.program_id(0)
        o2 = o2_ref[...]
        dy = dx_ref[...]
        r = lax.rsqrt(jnp.mean(o2 * o2, axis=-1, keepdims=True) + EPS)
        a = dy * g_ref[...]
        do2 = (r * a - o2 * (r * r * r) * jnp.mean(a * o2, axis=-1, keepdims=True)).astype(BF16)
        do2_ref[...] = do2
        dg = jnp.sum(dy * o2 * r, axis=0, keepdims=True)
        dm = lax.dot_general(do2, wo_ref[...], (((1,), (1,)), ((), ())), preferred_element_type=F32)
        for b in range(4):
            rows = slice(U_OFF[b], U_OFF[b + 1])
            y = jnp.dot(u_ref[:, rows], wout_ref[rows, :], preferred_element_type=F32)
            sl = slice(D * b, D * (b + 1))
            gt = _sig(gl_ref[:, sl] + gb_ref[:, sl])
            dgl = dm * y * gt * (1.0 - gt)
            dgl_ref[:, sl] = dgl.astype(BF16)
            part = jnp.sum(dgl, axis=0, keepdims=True)

            @pl.when(i == 0)
            def _(part=part, sl=sl):
                dgb_ref[:, sl] = part

            @pl.when(i > 0)
            def _(part=part, sl=sl):
                dgb_ref[:, sl] += part

            dyb = (dm * gt).astype(BF16)
            dy_ref[:, sl] = dyb
            du_ref[:, rows] = lax.dot_general(dyb, wout_ref[rows, :], (((1,), (1,)), ((), ())),
                                              preferred_element_type=F32)

        @pl.when(i == 0)
        def _():
            dg_ref[...] = dg

        @pl.when(i > 0)
        def _():
            dg_ref[...] += dg

    blk = pl.BlockSpec((MRB, D), lambda i: (i, 0))
    wide = pl.BlockSpec((MRB, 4096), lambda i: (i, 0))
    ub = pl.BlockSpec((MRB, 1280), lambda i: (i, 0))
    full = lambda r, c: pl.BlockSpec((r, c), lambda i: (0, 0))
    return pl.pallas_call(
        body, grid=(lp // MRB,),
        in_specs=[blk, blk, ub, wide, full(1, 4096), full(1280, D), full(D, D), full(1, D)],
        out_specs=[blk, wide, wide, ub, full(1, 4096), full(1, D)],
        out_shape=[jax.ShapeDtypeStruct((lp, D), BF16), jax.ShapeDtypeStruct((lp, 4096), BF16),
                   jax.ShapeDtypeStruct((lp, 4096), BF16), jax.ShapeDtypeStruct((lp, 1280), F32),
                   jax.ShapeDtypeStruct((1, 4096), F32), jax.ShapeDtypeStruct((1, D), F32)],
        name="merge_bwd", compiler_params=_cp())(dx, o2, u, z, gb, wout, wo, gpost)


def _loss_head(xf, tgt, n_real, lp):
    def body(x_ref, t_ref, dy_ref, ls_ref):
        i = pl.program_id(0)
        t = i * RB + lax.broadcasted_iota(jnp.int32, (RB, 1), 0)
        real = (t >= N_META) & (t < n_real)
        err = jnp.where(real, x_ref[...] - t_ref[...], 0.0)
        dy_ref[...] = err / D
        part = 0.5 * jnp.sum(jnp.mean(err * err, axis=-1, keepdims=True), axis=0, keepdims=True)
        part = jnp.broadcast_to(part, (8, LANE))

        @pl.when(i == 0)
        def _():
            ls_ref[...] = part

        @pl.when(i > 0)
        def _():
            ls_ref[...] += part

    blk = pl.BlockSpec((RB, D), lambda i: (i, 0))
    return pl.pallas_call(
        body, grid=(lp // RB,), in_specs=[blk, blk],
        out_specs=[blk, pl.BlockSpec((8, LANE), lambda i: (0, 0))],
        out_shape=[jax.ShapeDtypeStruct((lp, D), F32), jax.ShapeDtypeStruct((8, LANE), F32)],
        name="loss_head", compiler_params=_cp())(xf, tgt)


def _peer(d):
    x, y, c = lax.axis_index("x"), lax.axis_index("y"), lax.axis_index("c")
    return (x ^ ((d >> 2) & 1), y ^ ((d >> 1) & 1), c ^ (d & 1))


def _index_of(p):
    return 4 * p[0] + 2 * p[1] + p[2]


def _all_gather(xs, name):
    n = len(xs)

    def body(*refs):
        x_refs, out_refs = refs[:n], refs[n:2 * n]
        send_sems, recv_sems, local_sems = refs[2 * n:]
        me = _peer(0)
        sibling = _peer(1)
        chips = [2, 4, 6]

        def copy(a, kk, block, to, src=None):
            slot = out_refs[a].at[_index_of(block)]
            return pltpu.make_async_remote_copy(
                src_ref=slot if src is None else src, dst_ref=slot,
                send_sem=send_sems.at[7 * a + kk], recv_sem=recv_sems.at[7 * a + kk], device_id=to,
                device_id_type=MESH)

        mine = [pltpu.make_async_copy(x_refs[a], out_refs[a].at[_index_of(me)], local_sems.at[a]) for a in range(n)]
        for cp in mine:
            cp.start()
        first = []
        for a in range(n):
            first.append(copy(a, 0, me, sibling, src=x_refs[a]))
            first += [copy(a, 1 + j, me, _peer(d), src=x_refs[a]) for j, d in enumerate(chips)]
        for cp in first:
            cp.start()
        passed = []
        for j, d in enumerate(chips):
            for a in range(n):
                copy(a, 1 + j, _peer(d), me).wait_recv()
                fwd = copy(a, 4 + j, _peer(d), sibling)
                fwd.start()
                passed.append(fwd)
        for a in range(n):
            copy(a, 0, sibling, me).wait_recv()
            for j, d in enumerate(chips):
                copy(a, 4 + j, _peer(d | 1), me).wait_recv()
        for cp in first + passed:
            cp.wait_send()
        for cp in mine:
            cp.wait()

    anyspec = pl.BlockSpec(memory_space=pl.ANY)
    return pl.pallas_call(
        body, in_specs=[anyspec] * n, out_specs=[anyspec] * n,
        out_shape=[jax.ShapeDtypeStruct((N_DEV,) + x.shape, x.dtype) for x in xs],
        scratch_shapes=[pltpu.SemaphoreType.DMA((7 * n,)), pltpu.SemaphoreType.DMA((7 * n,)),
                        pltpu.SemaphoreType.DMA((n,))],
        name=name)(*xs)


def _all_to_all(xs, bufs, layer, name):
    n = len(xs)

    def body(*refs):
        x_refs, out_refs = refs[:n], refs[2 * n:3 * n]
        send_sems, recv_sems, local_sems = refs[3 * n:]
        me = _peer(0)
        my = _index_of(me)

        def land(a, dev):
            slot = out_refs[a].at[dev]
            return slot if layer is None else slot.at[layer]

        mine = [pltpu.make_async_copy(x_refs[a].at[my], land(a, my), local_sems.at[a]) for a in range(n)]
        for cp in mine:
            cp.start()
        copies = []
        for d in range(1, N_DEV):
            to = _peer(d)
            for a in range(n):
                copies.append(pltpu.make_async_remote_copy(
                    src_ref=x_refs[a].at[_index_of(to)], dst_ref=land(a, my),
                    send_sem=send_sems.at[7 * a + d - 1], recv_sem=recv_sems.at[7 * a + d - 1], device_id=to,
                    device_id_type=MESH))
        for cp in copies:
            cp.start()
        for d in range(1, N_DEV):
            frm = _index_of(_peer(d))
            for a in range(n):
                pltpu.make_async_remote_copy(
                    src_ref=x_refs[a].at[frm], dst_ref=land(a, frm),
                    send_sem=send_sems.at[7 * a + d - 1], recv_sem=recv_sems.at[7 * a + d - 1],
                    device_id=_peer(d), device_id_type=MESH).wait_recv()
        for cp in copies:
            cp.wait_send()
        for cp in mine:
            cp.wait()

    anyspec = pl.BlockSpec(memory_space=pl.ANY)
    return pl.pallas_call(
        body, in_specs=[anyspec] * (2 * n), out_specs=[anyspec] * n,
        out_shape=[jax.ShapeDtypeStruct(b.shape, b.dtype) for b in bufs],
        input_output_aliases={n + a: a for a in range(n)},
        scratch_shapes=[pltpu.SemaphoreType.DMA((7 * n,)), pltpu.SemaphoreType.DMA((7 * n,)),
                        pltpu.SemaphoreType.DMA((n,))],
        name=name)(*xs, *bufs)


def _adam_math(g, w, m, v):
    c1 = 1.0 - ADAM_B1 ** ADAM_STEP
    c2 = 1.0 - ADAM_B2 ** ADAM_STEP
    mn = ADAM_B1 * m + (1.0 - ADAM_B1) * g
    vn = ADAM_B2 * v + (1.0 - ADAM_B2) * (g * g)
    return -ADAM_LR * ((mn / c1) / (jnp.sqrt(vn / c2) + ADAM_EPS) + ADAM_WD * w), mn, vn


def _reduce_adam(parts, w, m, v, rb, name, row_off=0):
    depth, rows, cols = w.shape
    assert rows % rb == 0 and row_off % rb == 0

    def body(p_ref, w_ref, m_ref, v_ref, g_ref, d_ref, mo_ref, vo_ref):
        g = p_ref[0, 0].astype(F32)
        for j in range(1, N_DEV):
            g = g + p_ref[j, 0].astype(F32)
        g_ref[0] = g
        d_ref[0], mo_ref[0], vo_ref[0] = _adam_math(g, w_ref[0], m_ref[0], v_ref[0])

    blk = pl.BlockSpec((1, rb, cols), lambda l, i: (l, i, 0))
    out = jax.ShapeDtypeStruct(w.shape, F32)
    return pl.pallas_call(
        body, grid=(depth, rows // rb),
        in_specs=[pl.BlockSpec((N_DEV, 1, rb, cols), lambda l, i: (0, l, i + row_off // rb, 0)), blk, blk, blk],
        out_specs=[blk, blk, blk, blk], out_shape=[out, out, out, out],
        name=name, compiler_params=_cp())(parts, w, m, v)


def _reduce_adam_flat(parts, w, m, v, name):
    q_rows = w.shape[0]

    def body(p_ref, w_ref, m_ref, v_ref, g_ref, d_ref, mo_ref, vo_ref):
        g = p_ref[0].astype(F32)
        for j in range(1, N_DEV):
            g = g + p_ref[j].astype(F32)
        g_ref[...] = g
        d_ref[...], mo_ref[...], vo_ref[...] = _adam_math(g, w_ref[...], m_ref[...], v_ref[...])

    blk = pl.BlockSpec((q_rows, LANE), lambda i: (0, 0))
    out = jax.ShapeDtypeStruct((q_rows, LANE), F32)
    return pl.pallas_call(
        body, grid=(1,), in_specs=[pl.BlockSpec((N_DEV, q_rows, LANE), lambda i: (0, 0, 0)), blk, blk, blk],
        out_specs=[blk, blk, blk, blk], out_shape=[out, out, out, out],
        name=name, compiler_params=_cp())(parts, w, m, v)


C128 = (("w_out_pool", 256), ("w_out_mla", 512), ("w_out_conf", 256), ("w_out_sc", 256), ("w_ukv", 128))
C128_ROWS = sum(r for _, r in C128)
TAIL = (("meta_tokens", (N_META, 128)), ("conf_dw_w", (DEPTH, CONF_K, 32)), ("sc_dw_w", (DEPTH, SC_K, 32)))
TAIL_ROWS = sum(int(np.prod(s)) for _, s in TAIL) // LANE
TAIL_PAD = 56
SMALL = (("pre_norm_g", (DEPTH, D)), ("gate_bias", (DEPTH, 4096)), ("pool_w", (DEPTH, 4, 64, 64)),
         ("pool_scale", (DEPTH, 256)), ("q_norm_g", (DEPTH, 256)), ("kv_norm_g", (DEPTH, 128)),
         ("conf_dw_b", (DEPTH, 256)), ("conf_ln_g", (DEPTH, 256)), ("conf_ln_b", (DEPTH, 256)),
         ("post_norm_g", (DEPTH, D)))
SMALL_ROWS = sum(int(np.prod(s)) for _, s in SMALL) // LANE
SMALL_PAD = -(-(SMALL_ROWS + 1) // 8) * 8


def _pack_tail(t):
    parts = [t[n].reshape(-1, LANE) for n, _ in TAIL]
    parts.append(jnp.zeros((TAIL_PAD - TAIL_ROWS, LANE), F32))
    return jnp.concatenate(parts, axis=0)


def _unpack_tail(flat):
    out, off = {}, 0
    for n, s in TAIL:
        rows = int(np.prod(s)) // LANE
        out[n] = flat[off:off + rows].reshape(s)
        off += rows
    return out


def _unpack_tail_full(g):
    out, off = {}, 0
    for n, s in TAIL:
        rows = int(np.prod(s)) // LANE
        blk = jnp.moveaxis(g[:, off:off + rows].reshape((N_DEV,) + s), 0, -2)
        out[n] = blk.reshape(s[:-1] + (N_DEV * s[-1],))
        off += rows
    return out


def _pack_small(t, extra_row):
    parts = [t[n].reshape(-1, LANE) for n, _ in SMALL] + [extra_row]
    parts.append(jnp.zeros((SMALL_PAD - SMALL_ROWS - 1, LANE), F32))
    return jnp.concatenate(parts, axis=0)


def _unpack_small(flat):
    out, off = {}, 0
    for n, s in SMALL:
        rows = int(np.prod(s)) // LANE
        out[n] = flat[off:off + rows].reshape(s)
        off += rows
    return out


def _cols_by_dest(g, width):
    r = g.shape[0]
    return g.reshape(r, N_DEV, width).transpose(1, 0, 2)


def _cols_full(gathered):
    _, r, c = gathered.shape
    return gathered.transpose(1, 0, 2).reshape(r, N_DEV * c)


def _pack_w_in(w):
    r = w.shape[0]
    z = lambda n: jnp.zeros((r, n), w.dtype)
    return jnp.concatenate([w[:, 3232:7328], w[:, 512:896], z(64), w[:, 896:928], z(32), w[:, 0:512],
                            w[:, 928:3232]], axis=1)


def _unpack_w_in(gl, mla, mix):
    return jnp.concatenate([mix[:, 0:512], mla[:, 0:384], mla[:, 448:480], mix[:, 512:2816], gl], axis=1)


def _rope_tables(lp):
    inv = 1.0 / (ROPE_THETA ** (jnp.arange(0, QK_ROPE, 2, dtype=F32) / QK_ROPE))
    ang = jnp.arange(lp, dtype=F32)[:, None] * inv[None, :]
    cos, sin = jnp.cos(ang), jnp.sin(ang)
    one = jnp.ones((lp, QK_NOPE), F32)
    zero = jnp.zeros((lp, QK_NOPE), F32)
    z16 = jnp.zeros((lp, 16), F32)
    c = jnp.concatenate([one, cos, cos, jnp.ones((lp, 32), F32)], axis=1)
    s1 = jnp.concatenate([zero, z16, sin, jnp.zeros((lp, 32), F32)], axis=1)
    s2 = jnp.concatenate([zero, -sin, z16, jnp.zeros((lp, 32), F32)], axis=1)
    return c, s1, s2


def kernel(x, meta_tokens, pre_norm_g, w_in, gate_bias, pool_w, pool_scale, w_out_pool, q_norm_g, w_uq, kv_norm_g, w_ukv, w_out_mla, conf_dw_w, conf_dw_b, conf_ln_g, conf_ln_b, w_out_conf, sc_dw_w, w_out_sc, w_o, post_norm_g, loss_target, m_meta_tokens, m_pre_norm_g, m_w_in, m_gate_bias, m_pool_w, m_pool_scale, m_w_out_pool, m_q_norm_g, m_w_uq, m_kv_norm_g, m_w_ukv, m_w_out_mla, m_conf_dw_w, m_conf_dw_b, m_conf_ln_g, m_conf_ln_b, m_w_out_conf, m_sc_dw_w, m_w_out_sc, m_w_o, m_post_norm_g, v_meta_tokens, v_pre_norm_g, v_w_in, v_gate_bias, v_pool_w, v_pool_scale, v_w_out_pool, v_q_norm_g, v_w_uq, v_kv_norm_g, v_w_ukv, v_w_out_mla, v_conf_dw_w, v_conf_dw_b, v_conf_ln_g, v_conf_ln_b, v_w_out_conf, v_sc_dw_w, v_w_out_sc, v_w_o, v_post_norm_g):
    names = ["w_in", "w_uq", "w_o"] + [n for n, _ in C128] + [n for n, _ in TAIL] + [n for n, _ in SMALL]
    loc = locals()
    w = {n: loc[n] for n in names}
    mom = {n: loc["m_" + n] for n in names}
    vel = {n: loc["v_" + n] for n in names}

    seq = x.shape[1]
    n_real = N_META + seq
    lp = -(-n_real // RB) * RB
    tmb = lp // 3
    tabs = _rope_tables(lp)

    c128 = jnp.concatenate([w[n] for n, _ in C128], axis=1)
    gathered = [_all_gather([w_in[i].astype(BF16), c128[i].astype(BF16), w_uq[i].astype(BF16),
                             w_o[i].astype(BF16)], "gather_weights") for i in range(DEPTH)]
    tail_w = _pack_tail(w)
    tail = _unpack_tail_full(_all_gather([tail_w], "gather_tail")[0])
    eye4 = jnp.eye(4, dtype=F32)
    bd_all = (pool_w[:, :, :, None, :] * eye4[None, :, None, :, None]).reshape(DEPTH, 256, 256).astype(BF16)

    def layer_weights(i):
        g_in, g_c128, g_uq, g_o = gathered[i]
        lw = {}
        lw["w_in"] = _pack_w_in(_cols_full(g_in))
        lw["wc"] = _cols_full(g_c128)
        wuq = _cols_full(g_uq).reshape(Q_RANK, HEADS, 96)
        lw["w_uq"] = jnp.pad(wuq, ((0, 0), (0, 0), (0, 32))).reshape(Q_RANK, HEADS * LANE)
        wukv = lw["wc"][U_OFF[4]:].reshape(KV_RANK, HEADS, 128)
        wk = jnp.pad(wukv[:, :, :QK_NOPE], ((0, 0), (0, 0), (0, 64))).reshape(KV_RANK, HEADS * LANE)
        lw["w_ukv"] = jnp.concatenate([wk, wukv[:, :, QK_NOPE:].reshape(KV_RANK, HEADS * V_DIM)], axis=1)
        lw["w_o"] = g_o.reshape(D, D)
        lw["bd"] = bd_all[i]
        lw["cw"] = jnp.pad(tail["conf_dw_w"][i], ((0, 1), (0, 0)))
        lw["sw"] = jnp.pad(tail["sc_dw_w"][i], ((0, 8 - SC_K), (0, 0)))
        return lw

    meta_full = tail["meta_tokens"]

    pad_rows = lp - n_real
    xr = jnp.concatenate([meta_full, x[0], jnp.zeros((pad_rows, D), F32)], axis=0)
    tgt = jnp.pad(loss_target[0], ((N_META, pad_rows), (0, 0)))
    saved = []
    for i in range(DEPTH):
        lw = layer_weights(i)
        h = _rms_fwd(xr, pre_norm_g[i:i + 1], lp)
        z = _mm(h, lw["w_in"], lp, ZW, D, tm=tmb, tn=256, tk=D, name="mm_in")
        qn, kvn, krr = _mla_prep(z, q_norm_g[i:i + 1], kv_norm_g[i:i + 1], tabs, lp)
        q_raw = _mm(qn, lw["w_uq"], lp, 1024, Q_RANK, tm=tmb, tn=1024, tk=Q_RANK, name="mm_uq")
        kv_raw = _mm(kvn, lw["w_ukv"], lp, 1536, KV_RANK, tm=tmb, tn=512, tk=KV_RANK, name="mm_ukv")
        qt, kt, vt = _mla_post(q_raw, kv_raw, krr, tabs, lp)
        oat, lse = _attn_fwd(qt, kt, vt, lp)
        u = _mix_fwd(z, oat, lw["bd"], pool_scale[i:i + 1], lw["cw"], conf_dw_b[i:i + 1], conf_ln_g[i:i + 1],
                     conf_ln_b[i:i + 1], lw["sw"], lp)
        x_new, m_act, o2 = _merge_fwd(xr, u, z, gate_bias[i:i + 1], lw["wc"], lw["w_o"],
                                      post_norm_g[i:i + 1], lp)
        saved.append(dict(lw=lw, x=xr, h=h, z=z, qn=qn, kvn=kvn, qt=qt, kt=kt, vt=vt, oat=oat, lse=lse,
                          u=u, m=m_act, o2=o2))
        xr = x_new

    dx, loss_part = _loss_head(xr, tgt, n_real, lp)

    gsm = {n: [None] * DEPTH for n, _ in SMALL}
    g_cw = [None] * DEPTH
    g_sw = [None] * DEPTH
    recv = [lax.empty((N_DEV, DEPTH) + s, BF16) for s in ((D, 916), (C128_ROWS, 128), (Q_RANK, 96), (128, D))]
    for i in reversed(range(DEPTH)):
        s = saved[i]
        lw = s["lw"]
        do2, dgl, dyb, du, dgb, dgpost = _merge_bwd(dx, s["o2"], s["u"], s["z"], gate_bias[i:i + 1],
                                                    lw["wc"], lw["w_o"], post_norm_g[i:i + 1], lp)
        d_wo = _mm(s["m"], do2, D, D, lp, ta=True, tm=512, tn=D, tk=tmb, name="mm_dwo")
        d_wout = []
        for b in range(4):
            rows = U_OFF[b + 1] - U_OFF[b]
            d_wout.append(_mm(s["u"], dyb, rows, D, lp, ta=True, tm=256, tn=D, tk=tmb,
                              a_moff=U_OFF[b] // 256, b_noff=b, name="mm_dwout%d" % b))
        dzx, doat, dbd, dcw, dsw, dsm = _mix_bwd(
            s["z"], s["oat"], du, lw["bd"], pool_scale[i:i + 1], lw["cw"], conf_dw_b[i:i + 1],
            conf_ln_g[i:i + 1], conf_ln_b[i:i + 1], lw["sw"], lp)
        dqt, dkt, dvt = _attn_bwd(s["qt"], s["kt"], s["vt"], s["oat"], doat, s["lse"], lp)
        dq_raw, dkv_raw, dkr = _mla_post_bwd(dqt, dkt, dvt, tabs, lp)
        dqn = _mm(dq_raw, lw["w_uq"], lp, Q_RANK, 1024, tb=True, tm=tmb, tn=Q_RANK, tk=1024, name="mm_dqn")
        d_wuq = _mm(s["qn"], dq_raw, Q_RANK, 1024, lp, ta=True, tm=Q_RANK, tn=1024, tk=tmb, name="mm_dwuq")
        dkvn = _mm(dkv_raw, lw["w_ukv"], lp, KV_RANK, 1536, tb=True, tm=tmb, tn=KV_RANK, tk=1536, name="mm_dkvn")
        d_wukv = _mm(s["kvn"], dkv_raw, KV_RANK, 1536, lp, ta=True, tm=KV_RANK, tn=1536, tk=tmb, name="mm_dwukv")
        dzq, dqg, dkvg = _mla_prep_bwd(s["z"], q_norm_g[i:i + 1], kv_norm_g[i:i + 1], dqn, dkvn, dkr, lp)
        dh = _mm(dgl, lw["w_in"], lp, D, 4096, tb=True, tm=RB, tn=D, tk=2048, name="mm_dh0")
        dh = _mm(dzq, lw["w_in"], lp, D, 512, tb=True, tm=tmb, tn=D, tk=512, b_koff=CQ0 // 512, c=dh,
                 name="mm_dh1")
        dh = _mm(dzx, lw["w_in"], lp, D, ZW - PV0, tb=True, tm=tmb, tn=D, tk=256, b_koff=PV0 // 256, c=dh,
                 name="mm_dh2")
        d_win = [_mm(s["h"], seg, D, seg.shape[1], lp, ta=True, tm=D, tn=256, tk=tmb, name="mm_dwin%d" % k)
                 for k, seg in enumerate((dgl, dzq, dzx))]
        dx, dgpre = _rms_bwd(s["x"], pre_norm_g[i:i + 1], dh, dx, lp)

        gsm["pre_norm_g"][i] = dgpre[0]
        gsm["gate_bias"][i] = dgb[0]
        gsm["pool_w"][i] = jnp.stack([dbd[64 * g:64 * (g + 1), 64 * g:64 * (g + 1)] for g in range(4)])
        gsm["pool_scale"][i] = dsm[0]
        gsm["conf_dw_b"][i] = dsm[1]
        gsm["conf_ln_g"][i] = dsm[2]
        gsm["conf_ln_b"][i] = dsm[3]
        gsm["q_norm_g"][i] = dqg[0]
        gsm["kv_norm_g"][i] = dkvg[0]
        gsm["post_norm_g"][i] = dgpost[0]
        g_cw[i] = dcw[:CONF_K]
        g_sw[i] = dsw[:SC_K]
        d_wuq_o = d_wuq.reshape(Q_RANK, HEADS, LANE)[:, :, :96].reshape(Q_RANK, HEADS * 96)
        d_wukv_o = jnp.concatenate([d_wukv[:, :1024].reshape(KV_RANK, HEADS, LANE)[:, :, :QK_NOPE],
                                    d_wukv[:, 1024:].reshape(KV_RANK, HEADS, V_DIM)], axis=2).reshape(KV_RANK, 1024)
        by_dest = [
            _cols_by_dest(_unpack_w_in(*d_win), 916).astype(BF16),
            _cols_by_dest(jnp.concatenate(d_wout + [d_wukv_o], axis=0), 128).astype(BF16),
            _cols_by_dest(d_wuq_o, 96).astype(BF16),
            d_wo.reshape(N_DEV, 128, D).astype(BF16)]
        recv = _all_to_all(by_dest, recv, i, "scatter_grads")

    outs = [dict() for _ in range(4)]

    def put(n, res):
        for t, r in zip(outs, res):
            t[n] = r

    put("w_in", _reduce_adam(recv[0], w["w_in"], mom["w_in"], vel["w_in"], 256, "adam_w_in"))
    off = 0
    for n, rows in C128:
        put(n, _reduce_adam(recv[1], w[n], mom[n], vel[n], 128, "adam_" + n, row_off=off))
        off += rows
    put("w_uq", _reduce_adam(recv[2], w["w_uq"], mom["w_uq"], vel["w_uq"], Q_RANK, "adam_w_uq"))
    put("w_o", _reduce_adam(recv[3], w["w_o"], mom["w_o"], vel["w_o"], 128, "adam_w_o"))

    tail_g = {"meta_tokens": _cols_by_dest(dx[:N_META], 128),
              "conf_dw_w": jnp.moveaxis(jnp.stack(g_cw).reshape(DEPTH, CONF_K, N_DEV, 32), 2, 0),
              "sc_dw_w": jnp.moveaxis(jnp.stack(g_sw).reshape(DEPTH, SC_K, N_DEV, 32), 2, 0)}
    tail_bd = jnp.concatenate([tail_g[n].reshape(N_DEV, -1, LANE) for n, _ in TAIL]
                              + [jnp.zeros((N_DEV, TAIL_PAD - TAIL_ROWS, LANE), F32)], axis=1)
    tail_recv = _all_to_all([tail_bd], [lax.empty((N_DEV, TAIL_PAD, LANE), F32)], None, "scatter_tail")[0]
    tail_res = _reduce_adam_flat(tail_recv, tail_w, _pack_tail(mom), _pack_tail(vel), "adam_tail")

    small_g = {n: jnp.stack(gsm[n]) for n, _ in SMALL}
    loss_row = jnp.concatenate([loss_part[0:1, 0:1], jnp.zeros((1, LANE - 1), F32)], axis=1)
    zrow = jnp.zeros((1, LANE), F32)
    parts = _all_gather([_pack_small(small_g, loss_row)], "gather_small_grads")[0]
    small_res = _reduce_adam_flat(parts, _pack_small(w, zrow), _pack_small(mom, zrow), _pack_small(vel, zrow),
                                  "adam_small")
    loss = small_res[0][SMALL_ROWS, 0]
    for t, tf, sf in zip(outs, tail_res, small_res):
        t.update(_unpack_tail(tf))
        t.update(_unpack_small(sf))
    order = ["meta_tokens", "pre_norm_g", "w_in", "gate_bias", "pool_w", "pool_scale", "w_out_pool", "q_norm_g",
             "w_uq", "kv_norm_g", "w_ukv", "w_out_mla", "conf_dw_w", "conf_dw_b", "conf_ln_g", "conf_ln_b",
             "w_out_conf", "sc_dw_w", "w_out_sc", "w_o", "post_norm_g"]
    grad_x = dx[N_META:n_real][None]
    return (loss, grad_x, *[t[n] for t in outs for n in order])
```

```python
import functools

import jax
import jax.numpy as jnp
import numpy as np
from jax import lax
from jax.experimental import pallas as pl
from jax.experimental.pallas import tpu as pltpu

F32 = jnp.float32
BF16 = jnp.bfloat16

D = 1024
N_META = 16
DEPTH = 4
EPS = 1e-6
HEADS = 8
QK_NOPE = 64
QK_ROPE = 32
V_DIM = 64
Q_RANK = 256
KV_RANK = 128
ROPE_THETA = 10000.0
SCALE = (QK_NOPE + QK_ROPE) ** -0.5
CONF_K = 31
SC_K = 3
N_DEV = 8

ADAM_LR = 0.001
ADAM_B1 = 0.9
ADAM_B2 = 0.999
ADAM_EPS = 1e-08
ADAM_WD = 0.01
ADAM_STEP = 10

RB = 384
HB = 32
LANE = 128
VMEM_LIMIT = 56 * 1024 * 1024

GL0, CQ0, CKV0, KR0, PV0, PG0, MG0, CU0, CG0, SBB0, SBC0, SBX0, SG0, ZW = (
    0, 4096, 4352, 4480, 4608, 4864, 5120, 5632, 6144, 6400, 6656, 6912, 7168, 7424)
ZSEG = ((0, 4096), (4096, 512), (4608, 2816))
LOG2E = 1.4426950408889634
LN2 = 0.6931471805599453

MESH = pl.DeviceIdType.MESH


def _cp(**kw):
    return pltpu.CompilerParams(vmem_limit_bytes=VMEM_LIMIT, **kw)


def _sig(x):
    return jax.nn.sigmoid(x)


def _silu(x):
    return x * _sig(x)


def _dsilu(x):
    s = _sig(x)
    return s * (1.0 + x * (1.0 - s))


def _dn(x, k):
    return x if k == 0 else pltpu.roll(x, k, 0)


def _up(x, k):
    return x if k == 0 else pltpu.roll(x, x.shape[0] - k, 0)


def _rope(t, c, s1, s2):
    return t * c + pltpu.roll(t, 16, 1) * s1 + pltpu.roll(t, LANE - 16, 1) * s2


def _rope_t(g, c, s1, s2):
    return g * c + pltpu.roll(g * s1, LANE - 16, 1) + pltpu.roll(g * s2, 16, 1)


def _mm(a, b, m, n, k, *, ta=False, tb=False, out_dtype=F32, tm, tn, tk, name,
        a_moff=0, a_koff=0, b_noff=0, b_koff=0, c=None):
    assert m % tm == 0 and n % tn == 0 and k % tk == 0, (name, m, n, k, tm, tn, tk)
    nk = k // tk
    dims = (((0,) if ta else (1,), (1,) if tb else (0,)), ((), ()))
    has_c = c is not None

    def body(a_ref, b_ref, *rest):
        c_ref = rest[0] if has_c else None
        o_ref = rest[1] if has_c else rest[0]
        scr = rest[2:] if has_c else rest[1:]
        part = lax.dot_general(a_ref[...].astype(BF16), b_ref[...].astype(BF16), dims,
                               preferred_element_type=F32)

        def finish(total):
            if has_c:
                total = total + c_ref[...]
            o_ref[...] = total.astype(out_dtype)

        if nk == 1:
            finish(part)
        else:
            acc = scr[0]
            kk = pl.program_id(2)

            @pl.when(kk == 0)
            def _():
                acc[...] = part

            @pl.when(kk > 0)
            def _():
                acc[...] += part

            @pl.when(kk == nk - 1)
            def _():
                finish(acc[...])

    if ta:
        a_spec = pl.BlockSpec((tk, tm), lambda i, j, q: (q + a_koff, i + a_moff))
    else:
        a_spec = pl.BlockSpec((tm, tk), lambda i, j, q: (i + a_moff, q + a_koff))
    if tb:
        b_spec = pl.BlockSpec((tn, tk), lambda i, j, q: (j + b_noff, q + b_koff))
    else:
        b_spec = pl.BlockSpec((tk, tn), lambda i, j, q: (q + b_koff, j + b_noff))
    o_spec = pl.BlockSpec((tm, tn), lambda i, j, q: (i, j))
    return pl.pallas_call(
        body, grid=(m // tm, n // tn, nk), in_specs=[a_spec, b_spec] + ([o_spec] if has_c else []),
        out_specs=o_spec, out_shape=jax.ShapeDtypeStruct((m, n), out_dtype),
        scratch_shapes=[pltpu.VMEM((tm, tn), F32)] if nk > 1 else [],
        name=name, compiler_params=_cp())(*((a, b, c) if has_c else (a, b)))


def _rms_fwd(x, g, lp):
    def body(x_ref, g_ref, h_ref):
        xv = x_ref[...]
        r = lax.rsqrt(jnp.mean(xv * xv, axis=-1, keepdims=True) + EPS)
        h_ref[...] = (xv * r * g_ref[...]).astype(BF16)

    return pl.pallas_call(
        body, grid=(lp // RB,),
        in_specs=[pl.BlockSpec((RB, D), lambda i: (i, 0)), pl.BlockSpec((1, D), lambda i: (0, 0))],
        out_specs=pl.BlockSpec((RB, D), lambda i: (i, 0)),
        out_shape=jax.ShapeDtypeStruct((lp, D), BF16), name="rms_fwd", compiler_params=_cp())(x, g)


def _rms_bwd(x, g, dh, dx_in, lp):
    def body(x_ref, g_ref, dh_ref, dxi_ref, dx_ref, dg_ref):
        i = pl.program_id(0)
        xv = x_ref[...]
        r = lax.rsqrt(jnp.mean(xv * xv, axis=-1, keepdims=True) + EPS)
        dy = dh_ref[...]
        a = dy * g_ref[...]
        dx_ref[...] = dxi_ref[...] + r * a - xv * (r * r * r) * jnp.mean(a * xv, axis=-1, keepdims=True)
        part = jnp.sum(dy * xv * r, axis=0, keepdims=True)

        @pl.when(i == 0)
        def _():
            dg_ref[...] = part

        @pl.when(i > 0)
        def _():
            dg_ref[...] += part

    blk = pl.BlockSpec((RB, D), lambda i: (i, 0))
    vec = pl.BlockSpec((1, D), lambda i: (0, 0))
    return pl.pallas_call(
        body, grid=(lp // RB,), in_specs=[blk, vec, blk, blk], out_specs=[blk, vec],
        out_shape=[jax.ShapeDtypeStruct((lp, D), F32), jax.ShapeDtypeStruct((1, D), F32)],
        name="rms_bwd", compiler_params=_cp())(x, g, dh, dx_in)


def _mla_prep(z, qg, kvg, tabs, lp):
    def body(z_ref, qg_ref, kvg_ref, c_ref, s1_ref, s2_ref, qn_ref, kvn_ref, kr_ref):
        cq = z_ref[:, 0:256]
        ckv = z_ref[:, 256:384]
        kr = z_ref[:, 384:512]
        rq = lax.rsqrt(jnp.mean(cq * cq, axis=-1, keepdims=True) + EPS)
        rk = lax.rsqrt(jnp.mean(ckv * ckv, axis=-1, keepdims=True) + EPS)
        qn_ref[...] = (cq * rq * qg_ref[...]).astype(BF16)
        kvn_ref[...] = (ckv * rk * kvg_ref[...]).astype(BF16)
        kr_ref[...] = _rope(kr, c_ref[...], s1_ref[...], s2_ref[...])

    tab = pl.BlockSpec((RB, LANE), lambda i: (i, 0))
    return pl.pallas_call(
        body, grid=(lp // RB,),
        in_specs=[pl.BlockSpec((RB, 512), lambda i: (i, CQ0 // 512)),
                  pl.BlockSpec((1, 256), lambda i: (0, 0)), pl.BlockSpec((1, 128), lambda i: (0, 0)),
                  tab, tab, tab],
        out_specs=[pl.BlockSpec((RB, 256), lambda i: (i, 0)), tab, tab],
        out_shape=[jax.ShapeDtypeStruct((lp, 256), BF16), jax.ShapeDtypeStruct((lp, 128), BF16),
                   jax.ShapeDtypeStruct((lp, 128), F32)],
        name="mla_prep", compiler_params=_cp())(z, qg, kvg, *tabs)


def _mla_prep_bwd(z, qg, kvg, dqn, dkvn, dkr, lp):
    def body(z_ref, qg_ref, kvg_ref, dqn_ref, dkvn_ref, dkr_ref, dz_ref, dqg_ref, dkvg_ref):
        i = pl.program_id(0)

        def rms_b(xv, g, dy):
            r = lax.rsqrt(jnp.mean(xv * xv, axis=-1, keepdims=True) + EPS)
            a = dy * g
            dx = r * a - xv * (r * r * r) * jnp.mean(a * xv, axis=-1, keepdims=True)
            return dx, jnp.sum(dy * xv * r, axis=0, keepdims=True)

        dcq, pq = rms_b(z_ref[:, 0:256], qg_ref[...], dqn_ref[...])
        dckv, pk = rms_b(z_ref[:, 256:384], kvg_ref[...], dkvn_ref[...])
        dz_ref[:, 0:256] = dcq.astype(BF16)
        dz_ref[:, 256:384] = dckv.astype(BF16)
        dz_ref[:, 384:512] = dkr_ref[...].astype(BF16)

        @pl.when(i == 0)
        def _():
            dqg_ref[...] = pq
            dkvg_ref[...] = pk

        @pl.when(i > 0)
        def _():
            dqg_ref[...] += pq
            dkvg_ref[...] += pk

    tab = pl.BlockSpec((RB, LANE), lambda i: (i, 0))
    return pl.pallas_call(
        body, grid=(lp // RB,),
        in_specs=[pl.BlockSpec((RB, 512), lambda i: (i, CQ0 // 512)),
                  pl.BlockSpec((1, 256), lambda i: (0, 0)), pl.BlockSpec((1, 128), lambda i: (0, 0)),
                  pl.BlockSpec((RB, 256), lambda i: (i, 0)), tab, tab],
        out_specs=[pl.BlockSpec((RB, 512), lambda i: (i, 0)),
                   pl.BlockSpec((1, 256), lambda i: (0, 0)), pl.BlockSpec((1, 128), lambda i: (0, 0))],
        out_shape=[jax.ShapeDtypeStruct((lp, 512), BF16), jax.ShapeDtypeStruct((1, 256), F32),
                   jax.ShapeDtypeStruct((1, 128), F32)],
        name="mla_prep_bwd", compiler_params=_cp())(z, qg, kvg, dqn, dkvn, dkr)


def _mla_post(q_raw, kv_raw, krr, tabs, lp):
    def body(q_ref, kv_ref, kr_ref, c_ref, s1_ref, s2_ref, qo_ref, ko_ref, vo_ref):
        c, s1, s2, kr = c_ref[...], s1_ref[...], s2_ref[...], kr_ref[...]
        for h in range(HEADS):
            sl = slice(LANE * h, LANE * (h + 1))
            qo_ref[:, sl] = (_rope(q_ref[:, sl], c, s1, s2) * (SCALE * LOG2E)).astype(BF16)
            ko_ref[:, sl] = (kv_ref[:, sl] + kr).astype(BF16)
        vo_ref[...] = kv_ref[:, 1024:1536].astype(BF16)

    tab = pl.BlockSpec((RB, LANE), lambda i: (i, 0))
    wide = pl.BlockSpec((RB, 1024), lambda i: (i, 0))
    return pl.pallas_call(
        body, grid=(lp // RB,),
        in_specs=[wide, pl.BlockSpec((RB, 1536), lambda i: (i, 0)), tab, tab, tab, tab],
        out_specs=[wide, wide, pl.BlockSpec((RB, 512), lambda i: (i, 0))],
        out_shape=[jax.ShapeDtypeStruct((lp, 1024), BF16), jax.ShapeDtypeStruct((lp, 1024), BF16),
                   jax.ShapeDtypeStruct((lp, 512), BF16)],
        name="mla_post", compiler_params=_cp())(q_raw, kv_raw, krr, *tabs)


def _mla_post_bwd(dq, dk, dv, tabs, lp):
    def body(dq_ref, dk_ref, dv_ref, c_ref, s1_ref, s2_ref, dqr_ref, dkv_ref, dkr_ref):
        c, s1, s2 = c_ref[...], s1_ref[...], s2_ref[...]
        lane = lax.broadcasted_iota(jnp.int32, (1, LANE), 1)
        ropel = (lane >= QK_NOPE) & (lane < QK_NOPE + QK_ROPE)
        ksum = jnp.zeros((RB, LANE), F32)
        for h in range(HEADS):
            sl = slice(LANE * h, LANE * (h + 1))
            dqr_ref[:, sl] = _rope_t(dq_ref[:, sl].astype(F32) * SCALE, c, s1, s2).astype(BF16)
            dkt = dk_ref[:, sl]
            dkv_ref[:, sl] = dkt
            ksum = ksum + dkt.astype(F32)
        dkv_ref[:, 1024:1536] = dv_ref[...]
        dkr_ref[...] = jnp.where(ropel, _rope_t(jnp.where(ropel, ksum, 0.0), c, s1, s2), 0.0)

    tab = pl.BlockSpec((RB, LANE), lambda i: (i, 0))
    wide = pl.BlockSpec((RB, 1024), lambda i: (i, 0))
    return pl.pallas_call(
        body, grid=(lp // RB,),
        in_specs=[wide, wide, pl.BlockSpec((RB, 512), lambda i: (i, 0)), tab, tab, tab],
        out_specs=[wide, pl.BlockSpec((RB, 1536), lambda i: (i, 0)), tab],
        out_shape=[jax.ShapeDtypeStruct((lp, 1024), BF16), jax.ShapeDtypeStruct((lp, 1536), BF16),
                   jax.ShapeDtypeStruct((lp, 128), F32)],
        name="mla_post_bwd", compiler_params=_cp())(dq, dk, dv, *tabs)


def _head_lanes(e):
    lane = lax.broadcasted_iota(jnp.int32, (1, LANE), 1)
    return lane >= V_DIM if e else lane < V_DIM


ONE_LANE = (V_DIM, 0)


def _attn_fwd(q, k, v, lp, gather=()):
    nq = lp // RB
    n = len(gather)
    steps = HEADS // 2

    def body(q_ref, k_ref, v_ref, *rest):
        o_ref, lse_ref = rest[n], rest[n + 1]
        vm_scr = rest[2 * n + 2]
        if n:
            g_start, g_forward, g_finish = _gather_phases(rest[:n], rest[n + 2:2 * n + 2], *rest[2 * n + 3:])
            pl.when(pl.program_id(0) == 0)(g_start)
            pl.when(pl.program_id(0) == steps // 2)(g_forward)
        lane = lax.broadcasted_iota(jnp.int32, (1, LANE), 1)
        vv = v_ref[...]
        for e in range(2):
            ones = jnp.where(lane == ONE_LANE[e], 1.0, 0.0).astype(BF16)
            vm_scr[e] = jnp.where(_head_lanes(e), vv, jnp.broadcast_to(ones, vv.shape))
        causal = (lax.broadcasted_iota(jnp.int32, (RB, RB), 1) <= lax.broadcasted_iota(jnp.int32, (RB, RB), 0))

        def qblock(i, _):
            rows = pl.ds(pl.multiple_of(i * RB, RB), RB)
            qs = [q_ref[rows, LANE * e:LANE * (e + 1)] for e in range(2)]

            def scores(j):
                cols = pl.ds(pl.multiple_of(j * RB, RB), RB)
                return tuple(lax.dot_general(qs[e], k_ref[cols, LANE * e:LANE * (e + 1)], (((1,), (1,)), ((), ())),
                                             preferred_element_type=F32) for e in range(2))

            def update(j, s, carry, masked):
                cols = pl.ds(pl.multiple_of(j * RB, RB), RB)
                out = []
                for e in range(2):
                    m, acc = carry[2 * e], carry[2 * e + 1]
                    se = jnp.where(causal, s[e], -jnp.inf) if masked else s[e]
                    m_new = jnp.maximum(m, jnp.max(se, axis=-1, keepdims=True))
                    p = jnp.exp2((se - m_new).astype(BF16))
                    acc = jnp.exp2(m - m_new) * acc + jnp.dot(p, vm_scr[e, cols, :], preferred_element_type=F32)
                    out += [m_new, acc]
                return tuple(out)

            def step(j, c):
                s_next = scores(j + 1)
                return update(j, c[4:], c[:4], False) + s_next

            m0 = jnp.full((RB, 1), -jnp.inf, F32)
            a0 = jnp.zeros((RB, LANE), F32)
            c = lax.fori_loop(0, i, step, (m0, a0, m0, a0) + scores(0))
            carry = update(i, c[4:], c[:4], True)
            o, lse = [], []
            for e in range(2):
                m, acc = carry[2 * e], carry[2 * e + 1]
                l = acc[:, ONE_LANE[e]:ONE_LANE[e] + 1]
                o.append(acc / l)
                lse.append(jnp.broadcast_to(m + jnp.log2(l), (RB, LANE)))
            o_ref[rows, :] = jnp.where(_head_lanes(0), o[0], o[1])
            lse_ref[rows, :] = jnp.where(_head_lanes(0), lse[0], lse[1])
            return 0

        lax.fori_loop(0, nq, qblock, 0)
        if n:
            pl.when(pl.program_id(0) == steps - 1)(g_finish)

    two = pl.BlockSpec((lp, 2 * LANE), lambda h: (0, h))
    one = pl.BlockSpec((lp, LANE), lambda h: (0, h))
    anyspec = pl.BlockSpec(memory_space=pl.ANY)
    return pl.pallas_call(
        body, grid=(steps,), in_specs=[two, two, one] + [anyspec] * n, out_specs=[one, one] + [anyspec] * n,
        out_shape=[jax.ShapeDtypeStruct((lp, 512), F32), jax.ShapeDtypeStruct((lp, 512), F32)]
        + _gather_shapes(gather),
        scratch_shapes=[pltpu.VMEM((2, lp, LANE), BF16)] + (_comm_sems(n) if n else []),
        name="attn_fwd_gather" if n else "attn_fwd", compiler_params=_cp())(q, k, v, *gather)


def _attn_bwd(q, k, v, o, do, lse, lp, scatter=None):
    nq = lp // RB
    xs, bufs, layer = scatter if scatter else ((), (), None)
    n = len(xs)
    steps = HEADS // 2

    def body(q_ref, k_ref, v_ref, o_ref, do_ref, lse_ref, *rest):
        dq_ref, dk_ref, dv_ref = rest[2 * n:2 * n + 3]
        vm_scr, dom_scr, dl_scr, dq_scr = rest[3 * n + 3:3 * n + 7]
        if n:
            s_start, s_finish = _scatter_phases(rest[:n], rest[2 * n + 3:3 * n + 3], *rest[3 * n + 7:], layer)
            pl.when(pl.program_id(0) == 0)(s_start)
        causal = (lax.broadcasted_iota(jnp.int32, (RB, RB), 1) <= lax.broadcasted_iota(jnp.int32, (RB, RB), 0))
        vv = v_ref[...]
        for e in range(2):
            vm_scr[e] = jnp.where(_head_lanes(e), vv, jnp.zeros_like(vv))

        def prep(i, _):
            rows = pl.ds(pl.multiple_of(i * RB, RB), RB)
            prod = do_ref[rows, :] * o_ref[rows, :]
            dls = []
            for e in range(2):
                hm = _head_lanes(e)
                dom_scr[e, rows, :] = jnp.where(hm, do_ref[rows, :], 0.0).astype(BF16)
                dls.append(jnp.sum(jnp.where(hm, prod, 0.0), axis=-1, keepdims=True))
            dl_scr[rows, :] = jnp.where(_head_lanes(0), dls[0], dls[1])
            dq_scr[rows, :] = jnp.zeros((RB, 2 * LANE), F32)
            return 0

        lax.fori_loop(0, nq, prep, 0)

        def kvblock(j, _):
            cols = pl.ds(pl.multiple_of(j * RB, RB), RB)
            kbs = [k_ref[cols, LANE * e:LANE * (e + 1)] for e in range(2)]

            def products(i):
                rows = pl.ds(pl.multiple_of(i * RB, RB), RB)
                out = []
                for e in range(2):
                    out.append(lax.dot_general(q_ref[rows, LANE * e:LANE * (e + 1)], kbs[e],
                                               (((1,), (1,)), ((), ())), preferred_element_type=F32))
                    out.append(lax.dot_general(dom_scr[e, rows, :], vm_scr[e, cols, :],
                                               (((1,), (1,)), ((), ())), preferred_element_type=F32))
                return tuple(out)

            def update(i, sd, carry, masked):
                rows = pl.ds(pl.multiple_of(i * RB, RB), RB)
                out = []
                for e in range(2):
                    dk, dv = carry[2 * e], carry[2 * e + 1]
                    sl = slice(LANE * e, LANE * (e + 1))
                    col1 = slice(V_DIM * e, V_DIM * e + 1)
                    s, dp = sd[2 * e], sd[2 * e + 1]
                    if masked:
                        s = jnp.where(causal, s, -jnp.inf)
                    p = jnp.exp2((s - lse_ref[rows, col1]).astype(BF16))
                    dv = dv + lax.dot_general(p, dom_scr[e, rows, :], (((0,), (0,)), ((), ())),
                                              preferred_element_type=F32)
                    ds = p * (dp - dl_scr[rows, col1]).astype(BF16)
                    dk = dk + lax.dot_general(ds, q_ref[rows, sl], (((0,), (0,)), ((), ())),
                                              preferred_element_type=F32)
                    dq_scr[rows, sl] += jnp.dot(ds, kbs[e], preferred_element_type=F32)
                    out += [dk, dv]
                return tuple(out)

            def step(i, c):
                nxt = products(jnp.minimum(i + 1, nq - 1))
                return update(i, c[4:], c[:4], False) + nxt

            zero = jnp.zeros((RB, LANE), F32)
            first = products(j)
            nxt = products(jnp.minimum(j + 1, nq - 1))
            carry = update(j, first, (zero, zero, zero, zero), True)
            dk0, dv0, dk1, dv1 = lax.fori_loop(j + 1, nq, step, carry + nxt)[:4]
            dk_ref[cols, 0:LANE] = (dk0 * LN2).astype(BF16)
            dk_ref[cols, LANE:2 * LANE] = (dk1 * LN2).astype(BF16)
            dv_ref[cols, :] = (dv0 + dv1).astype(BF16)
            return 0

        lax.fori_loop(0, nq, kvblock, 0)

        def fin(i, _):
            rows = pl.ds(pl.multiple_of(i * RB, RB), RB)
            dq_ref[rows, :] = dq_scr[rows, :].astype(BF16)
            return 0

        lax.fori_loop(0, nq, fin, 0)
        if n:
            pl.when(pl.program_id(0) == steps - 1)(s_finish)

    two = pl.BlockSpec((lp, 2 * LANE), lambda h: (0, h))
    one = pl.BlockSpec((lp, LANE), lambda h: (0, h))
    anyspec = pl.BlockSpec(memory_space=pl.ANY)
    return pl.pallas_call(
        body, grid=(steps,), in_specs=[two, two, one, one, one, one] + [anyspec] * (2 * n),
        out_specs=[two, two, one] + [anyspec] * n,
        out_shape=[jax.ShapeDtypeStruct((lp, 1024), BF16), jax.ShapeDtypeStruct((lp, 1024), BF16),
                   jax.ShapeDtypeStruct((lp, 512), BF16)] + [jax.ShapeDtypeStruct(b.shape, b.dtype) for b in bufs],
        input_output_aliases={6 + n + a: 3 + a for a in range(n)},
        scratch_shapes=[pltpu.VMEM((2, lp, LANE), BF16), pltpu.VMEM((2, lp, LANE), BF16),
                        pltpu.VMEM((lp, LANE), F32), pltpu.VMEM((lp, 2 * LANE), F32)]
        + (_comm_sems(n) if n else []),
        name="attn_bwd_scatter" if n else "attn_bwd", compiler_params=_cp())(q, k, v, o, do, lse, *xs, *bufs)


def _pool_lane_windows():
    lane = lax.broadcasted_iota(jnp.int32, (1, 256), 1)
    return jnp.where(lane < 64, 2, jnp.where(lane < 128, 4, jnp.where(lane < 192, 8, 16)))


def _by_window(wl, s2, s4, s8, s16):
    return jnp.where(wl == 2, s2, jnp.where(wl == 4, s4, jnp.where(wl == 8, s8, s16)))


def _pool_fwd_rows(pv_ext, t0):
    n = pv_ext.shape[0]
    wl = _pool_lane_windows()
    s2 = pv_ext + _dn(pv_ext, 1)
    s4 = s2 + _dn(s2, 2)
    s8 = s4 + _dn(s4, 4)
    s16 = s8 + _dn(s8, 8)
    t = t0 + lax.broadcasted_iota(jnp.int32, (n, 1), 0)
    cnt = jnp.maximum(jnp.minimum(t + 1, wl), 1).astype(F32)
    return _by_window(wl, s2, s4, s8, s16) / cnt - pv_ext


def _conv_dn(x_ext, w_ref, taps):
    acc = w_ref[taps - 1:taps, :] * x_ext
    for j in range(1, taps):
        acc = acc + w_ref[taps - 1 - j:taps - j, :] * _dn(x_ext, j)
    return acc


def _conv_up(g_ext, w_ref, taps):
    acc = w_ref[taps - 1:taps, :] * g_ext
    for j in range(1, taps):
        acc = acc + w_ref[taps - 1 - j:taps - j, :] * _up(g_ext, j)
    return acc


def _ln_fwd(c, g, b):
    mu = jnp.mean(c, axis=-1, keepdims=True)
    xc = c - mu
    r = lax.rsqrt(jnp.mean(xc * xc, axis=-1, keepdims=True) + EPS)
    xh = xc * r
    return xh * g + b, xh, r


def _halo_specs(lp, width, col):
    per = RB // HB
    last = lp // HB - 1
    cur = pl.BlockSpec((RB, width), lambda i: (i, col))
    prev = pl.BlockSpec((HB, width), lambda i: (jnp.maximum(i * per - 1, 0), col))
    nxt = pl.BlockSpec((HB, width), lambda i: (jnp.minimum((i + 1) * per, last), col))
    return cur, prev, nxt


def _mix_fwd(z, oat, bd, pscale, cw, cb, lng, lnb, sw, lp):
    def body(za, zah, mg, oat_ref, cu, cuh, cg, sbb, sbc, sbch, sbx, sbxh, sg,
             bd_ref, ps_ref, cw_ref, cb_ref, lng_ref, lnb_ref, sw_ref, u_ref):
        i = pl.program_id(0)
        pm = jnp.where(i > 0, 1.0, 0.0).astype(F32)
        pv = jnp.concatenate([zah[:, 0:256] * pm, za[:, 0:256]], axis=0)
        p = _pool_fwd_rows(pv, i * RB - HB)[HB:]
        y = jnp.dot(p.astype(BF16), bd_ref[...], preferred_element_type=F32)
        u_ref[:, 0:256] = (y * ps_ref[...] * _silu(za[:, 256:512])).astype(BF16)
        u_ref[:, 256:768] = (oat_ref[...] * _silu(mg[...])).astype(BF16)
        ce = jnp.concatenate([cuh[...] * pm, cu[...]], axis=0)
        glu = ce[:, 0:256] * _sig(ce[:, 256:512])
        c = _conv_dn(glu, cw_ref, CONF_K)[HB:] + cb_ref[...]
        n, _, _ = _ln_fwd(c, lng_ref[...], lnb_ref[...])
        u_ref[:, 768:1024] = (_silu(n) * _silu(cg[...])).astype(BF16)
        qe = jnp.concatenate([sbch[...] * sbxh[...] * pm, sbc[...] * sbx[...]], axis=0)
        cv = _conv_dn(qe, sw_ref, SC_K)[HB:]
        u_ref[:, 1024:1280] = (sbb[...] * cv * _silu(sg[...])).astype(BF16)

    a_cur, a_prev, _ = _halo_specs(lp, 512, PV0 // 512)
    cu_cur, cu_prev, _ = _halo_specs(lp, 512, CU0 // 512)
    sc_cur, sc_prev, _ = _halo_specs(lp, 256, SBC0 // 256)
    sx_cur, sx_prev, _ = _halo_specs(lp, 256, SBX0 // 256)
    c256 = lambda c0: pl.BlockSpec((RB, 256), lambda i: (i, c0 // 256))
    full = lambda r, c: pl.BlockSpec((r, c), lambda i: (0, 0))
    return pl.pallas_call(
        body, grid=(lp // RB,),
        in_specs=[a_cur, a_prev, pl.BlockSpec((RB, 512), lambda i: (i, MG0 // 512)),
                  pl.BlockSpec((RB, 512), lambda i: (i, 0)),
                  cu_cur, cu_prev, c256(CG0), c256(SBB0), sc_cur, sc_prev, sx_cur, sx_prev, c256(SG0),
                  full(256, 256), full(1, 256), full(32, 256), full(1, 256), full(1, 256), full(1, 256),
                  full(8, 256)],
        out_specs=pl.BlockSpec((RB, 1280), lambda i: (i, 0)),
        out_shape=jax.ShapeDtypeStruct((lp, 1280), BF16),
        name="mix_fwd", compiler_params=_cp())(z, z, z, oat, z, z, z, z, z, z, z, z, z,
                                               bd, pscale, cw, cb, lng, lnb, sw)


def _mix_bwd(z, oat, du, bd, pscale, cw, cb, lng, lnb, sw, lp):
    nb = lp // RB
    ne = RB + 2 * HB
    nf = RB + HB

    def body(za, zah, zan, mg, oat_ref, cu, cuh, cun, cg, cgn, sbb, sbbn, sbc, sbch, sbcn, sbx, sbxh, sbxn,
             sg, sgn, du_ref, dun_ref, bd_ref, ps_ref, cw_ref, cb_ref, lng_ref, lnb_ref, sw_ref,
             dzx_ref, doat_ref, dbd_ref, dcw_ref, dsw_ref, dsm_ref):
        xa, xm, xc = 0, MG0 - PV0, CU0 - PV0
        i = pl.program_id(0)
        pm = jnp.where(i > 0, 1.0, 0.0).astype(F32)
        nm = jnp.where(i < nb - 1, 1.0, 0.0).astype(F32)

        def ext(cur, prev, nxt, sl=slice(None)):
            return jnp.concatenate([prev[:, sl] * pm, cur[:, sl], nxt[:, sl] * nm], axis=0)

        def fwd(cur, nxt, sl=slice(None)):
            return jnp.concatenate([cur[:, sl], nxt[:, sl] * nm], axis=0)

        def csum(x):
            return jnp.sum(x, axis=0, keepdims=True)

        @pl.when(i == 0)
        def _():
            dbd_ref[...] = jnp.zeros((256, 256), F32)
            dcw_ref[...] = jnp.zeros((32, 256), F32)
            dsw_ref[...] = jnp.zeros((8, 256), F32)
            dsm_ref[...] = jnp.zeros((8, 256), F32)

        a_cols, b_cols = slice(0, 256), slice(256, 512)
        pv_e = ext(za, zah, zan, a_cols)
        p = _pool_fwd_rows(pv_e, i * RB - HB)[HB:HB + RB]
        pb = p.astype(BF16)
        y = jnp.dot(pb, bd_ref[...], preferred_element_type=F32)
        pg_f = fwd(za, zan, b_cols)
        dua_f = fwd(du_ref, dun_ref, slice(0, 256))
        dyp_f = dua_f * ps_ref[...] * _silu(pg_f)
        dypb = dyp_f.astype(BF16)
        dp_f = lax.dot_general(dypb, bd_ref[...], (((1,), (1,)), ((), ())), preferred_element_type=F32)
        wl = _pool_lane_windows()
        t = i * RB + lax.broadcasted_iota(jnp.int32, (nf, 1), 0)
        cnt = jnp.minimum(t + 1, wl).astype(F32)
        qf = dp_f / cnt
        f2 = qf + _up(qf, 1)
        f4 = f2 + _up(f2, 2)
        f8 = f4 + _up(f4, 4)
        f16 = f8 + _up(f8, 8)
        dpv = (_by_window(wl, f2, f4, f8, f16) - dp_f)[0:RB]
        dua = du_ref[:, 0:256]
        pg = za[:, b_cols]
        dpg = dua * y * ps_ref[...] * _dsilu(pg)
        dzx_ref[:, xa:xa + 256] = dpv.astype(BF16)
        dzx_ref[:, xa + 256:xa + 512] = dpg.astype(BF16)
        d_scale = csum(dua * y * _silu(pg))
        d_bd = lax.dot_general(pb, dypb[0:RB], (((0,), (0,)), ((), ())), preferred_element_type=F32)

        dub = du_ref[:, 256:768]
        mgv = mg[...]
        dzx_ref[:, xm:xm + 512] = (dub * oat_ref[...] * _dsilu(mgv)).astype(BF16)
        doat_ref[...] = dub * _silu(mgv)

        a_e = ext(cu, cuh, cun, slice(0, 256))
        gt_e = ext(cu, cuh, cun, slice(256, 512))
        sg_e = _sig(gt_e)
        glu_e = a_e * sg_e
        c_f = _conv_dn(glu_e, cw_ref, CONF_K)[HB:] + cb_ref[...]
        n_f, xh_f, r_f = _ln_fwd(c_f, lng_ref[...], lnb_ref[...])
        cg_f = fwd(cg, cgn)
        duc_f = fwd(du_ref, dun_ref, slice(768, 1024))
        sw_f = _silu(n_f)
        dcg = (duc_f * sw_f * _dsilu(cg_f))[0:RB]
        dn_f = duc_f * _silu(cg_f) * _dsilu(n_f)
        a_f = dn_f * lng_ref[...]
        dc_f = r_f * (a_f - jnp.mean(a_f, axis=-1, keepdims=True)
                      - xh_f * jnp.mean(a_f * xh_f, axis=-1, keepdims=True))
        d_lng = csum((dn_f * xh_f)[0:RB])
        d_lnb = csum(dn_f[0:RB])
        d_cb = csum(dc_f[0:RB])
        dglu = _conv_up(dc_f, cw_ref, CONF_K)[0:RB]
        dc_c = dc_f[0:RB]
        for kk in range(CONF_K):
            j = CONF_K - 1 - kk
            dcw_ref[kk:kk + 1, :] += csum(dc_c * _dn(glu_e, j)[HB:HB + RB])

        sgc = sg_e[HB:HB + RB]
        a_c = a_e[HB:HB + RB]
        dzx_ref[:, xc:xc + 256] = (dglu * sgc).astype(BF16)
        dzx_ref[:, xc + 256:xc + 512] = (dglu * a_c * sgc * (1.0 - sgc)).astype(BF16)
        dzx_ref[:, xc + 512:xc + 768] = dcg.astype(BF16)

        c_e = ext(sbc, sbch, sbcn)
        x_e = ext(sbx, sbxh, sbxn)
        q_e = c_e * x_e
        cv_f = _conv_dn(q_e, sw_ref, SC_K)[HB:]
        bg_f = fwd(sbb, sbbn)
        sg_f = fwd(sg, sgn)
        dud_f = fwd(du_ref, dun_ref, slice(1024, 1280))
        ssg_f = _silu(sg_f)
        dcv_f = dud_f * bg_f * ssg_f
        dbg = (dud_f * cv_f * ssg_f)[0:RB]
        dsg = (dud_f * bg_f * cv_f * _dsilu(sg_f))[0:RB]
        dq = _conv_up(dcv_f, sw_ref, SC_K)[0:RB]
        dcv_c = dcv_f[0:RB]
        for kk in range(SC_K):
            j = SC_K - 1 - kk
            dsw_ref[kk:kk + 1, :] += csum(dcv_c * _dn(q_e, j)[HB:HB + RB])

        dzx_ref[:, xc + 768:xc + 1024] = dbg.astype(BF16)
        dzx_ref[:, xc + 1024:xc + 1280] = (dq * x_e[HB:HB + RB]).astype(BF16)
        dzx_ref[:, xc + 1280:xc + 1536] = (dq * c_e[HB:HB + RB]).astype(BF16)
        dzx_ref[:, xc + 1536:xc + 1792] = dsg.astype(BF16)

        dbd_ref[...] += d_bd
        dsm_ref[0:1, :] += d_scale
        dsm_ref[1:2, :] += d_cb
        dsm_ref[2:3, :] += d_lng
        dsm_ref[3:4, :] += d_lnb

    a3 = _halo_specs(lp, 512, PV0 // 512)
    cu3 = _halo_specs(lp, 512, CU0 // 512)
    cg3 = _halo_specs(lp, 256, CG0 // 256)
    sbb3 = _halo_specs(lp, 256, SBB0 // 256)
    sbc3 = _halo_specs(lp, 256, SBC0 // 256)
    sbx3 = _halo_specs(lp, 256, SBX0 // 256)
    sg3 = _halo_specs(lp, 256, SG0 // 256)
    du3 = _halo_specs(lp, 1280, 0)
    full = lambda r, c: pl.BlockSpec((r, c), lambda i: (0, 0))
    in_specs = [a3[0], a3[1], a3[2], pl.BlockSpec((RB, 512), lambda i: (i, MG0 // 512)),
                pl.BlockSpec((RB, 512), lambda i: (i, 0)),
                cu3[0], cu3[1], cu3[2], cg3[0], cg3[2], sbb3[0], sbb3[2],
                sbc3[0], sbc3[1], sbc3[2], sbx3[0], sbx3[1], sbx3[2], sg3[0], sg3[2],
                du3[0], du3[2],
                full(256, 256), full(1, 256), full(32, 256), full(1, 256), full(1, 256), full(1, 256),
                full(8, 256)]
    out_specs = [pl.BlockSpec((RB, ZW - PV0), lambda i: (i, 0)), pl.BlockSpec((RB, 512), lambda i: (i, 0)),
                 full(256, 256), full(32, 256), full(8, 256), full(8, 256)]
    out_shape = [jax.ShapeDtypeStruct((lp, ZW - PV0), BF16), jax.ShapeDtypeStruct((lp, 512), F32),
                 jax.ShapeDtypeStruct((256, 256), F32), jax.ShapeDtypeStruct((32, 256), F32),
                 jax.ShapeDtypeStruct((8, 256), F32), jax.ShapeDtypeStruct((8, 256), F32)]
    return pl.pallas_call(
        body, grid=(nb,), in_specs=in_specs, out_specs=out_specs, out_shape=out_shape,
        name="mix_bwd", compiler_params=_cp())(
            z, z, z, z, oat, z, z, z, z, z, z, z, z, z, z, z, z, z, z, z, du, du,
            bd, pscale, cw, cb, lng, lnb, sw)


U_OFF = (0, 256, 768, 1024, 1280)
MRB = 192


def _merge_fwd(x, u, z, gb, wout, wo, gpost, lp):
    def body(x_ref, u_ref, gl_ref, gb_ref, wout_ref, wo_ref, g_ref, xo_ref, m_ref, o2_ref):
        m = jnp.zeros((MRB, D), F32)
        for b in range(4):
            y = jnp.dot(u_ref[:, U_OFF[b]:U_OFF[b + 1]], wout_ref[U_OFF[b]:U_OFF[b + 1], :],
                        preferred_element_type=F32)
            sl = slice(D * b, D * (b + 1))
            m = m + _sig(gl_ref[:, sl] + gb_ref[:, sl]) * y
        mb = m.astype(BF16)
        m_ref[...] = mb
        o2 = jnp.dot(mb, wo_ref[...], preferred_element_type=F32)
        o2_ref[...] = o2
        r = lax.rsqrt(jnp.mean(o2 * o2, axis=-1, keepdims=True) + EPS)
        xo_ref[...] = x_ref[...] + o2 * r * g_ref[...]

    blk = pl.BlockSpec((MRB, D), lambda i: (i, 0))
    full = lambda r, c: pl.BlockSpec((r, c), lambda i: (0, 0))
    return pl.pallas_call(
        body, grid=(lp // MRB,),
        in_specs=[blk, pl.BlockSpec((MRB, 1280), lambda i: (i, 0)), pl.BlockSpec((MRB, 4096), lambda i: (i, 0)),
                  full(1, 4096), full(1280, D), full(D, D), full(1, D)],
        out_specs=[blk, blk, blk],
        out_shape=[jax.ShapeDtypeStruct((lp, D), F32), jax.ShapeDtypeStruct((lp, D), BF16),
                   jax.ShapeDtypeStruct((lp, D), F32)],
        name="merge_fwd", compiler_params=_cp())(x, u, z, gb, wout, wo, gpost)


def _merge_bwd(dx, o2, u, z, gb, wout, wo, gpost, lp):
    def body(dx_ref, o2_ref, u_ref, gl_ref, gb_ref, wout_ref, wo_ref, g_ref,
             do2_ref, dgl_ref, dy_ref, du_ref, dgb_ref, dg_ref):
        i = pl.program_id(0)
        o2 = o2_ref[...]
        dy = dx_ref[...]
        r = lax.rsqrt(jnp.mean(o2 * o2, axis=-1, keepdims=True) + EPS)
        a = dy * g_ref[...]
        do2 = (r * a - o2 * (r * r * r) * jnp.mean(a * o2, axis=-1, keepdims=True)).astype(BF16)
        do2_ref[...] = do2
        dg = jnp.sum(dy * o2 * r, axis=0, keepdims=True)
        dm = lax.dot_general(do2, wo_ref[...], (((1,), (1,)), ((), ())), preferred_element_type=F32)
        for b in range(4):
            rows = slice(U_OFF[b], U_OFF[b + 1])
            y = jnp.dot(u_ref[:, rows], wout_ref[rows, :], preferred_element_type=F32)
            sl = slice(D * b, D * (b + 1))
            gt = _sig(gl_ref[:, sl] + gb_ref[:, sl])
            dgl = dm * y * gt * (1.0 - gt)
            dgl_ref[:, sl] = dgl.astype(BF16)
            part = jnp.sum(dgl, axis=0, keepdims=True)

            @pl.when(i == 0)
            def _(part=part, sl=sl):
                dgb_ref[:, sl] = part

            @pl.when(i > 0)
            def _(part=part, sl=sl):
                dgb_ref[:, sl] += part

            dyb = (dm * gt).astype(BF16)
            dy_ref[:, sl] = dyb
            du_ref[:, rows] = lax.dot_general(dyb, wout_ref[rows, :], (((1,), (1,)), ((), ())),
                                              preferred_element_type=F32)

        @pl.when(i == 0)
        def _():
            dg_ref[...] = dg

        @pl.when(i > 0)
        def _():
            dg_ref[...] += dg

    blk = pl.BlockSpec((MRB, D), lambda i: (i, 0))
    wide = pl.BlockSpec((MRB, 4096), lambda i: (i, 0))
    ub = pl.BlockSpec((MRB, 1280), lambda i: (i, 0))
    full = lambda r, c: pl.BlockSpec((r, c), lambda i: (0, 0))
    return pl.pallas_call(
        body, grid=(lp // MRB,),
        in_specs=[blk, blk, ub, wide, full(1, 4096), full(1280, D), full(D, D), full(1, D)],
        out_specs=[blk, wide, wide, ub, full(1, 4096), full(1, D)],
        out_shape=[jax.ShapeDtypeStruct((lp, D), BF16), jax.ShapeDtypeStruct((lp, 4096), BF16),
                   jax.ShapeDtypeStruct((lp, 4096), BF16), jax.ShapeDtypeStruct((lp, 1280), F32),
                   jax.ShapeDtypeStruct((1, 4096), F32), jax.ShapeDtypeStruct((1, D), F32)],
        name="merge_bwd", compiler_params=_cp())(dx, o2, u, z, gb, wout, wo, gpost)


def _loss_head(xf, tgt, n_real, lp):
    def body(x_ref, t_ref, dy_ref, ls_ref):
        i = pl.program_id(0)
        t = i * RB + lax.broadcasted_iota(jnp.int32, (RB, 1), 0)
        real = (t >= N_META) & (t < n_real)
        err = jnp.where(real, x_ref[...] - t_ref[...], 0.0)
        dy_ref[...] = err / D
        part = 0.5 * jnp.sum(jnp.mean(err * err, axis=-1, keepdims=True), axis=0, keepdims=True)
        part = jnp.broadcast_to(part, (8, LANE))

        @pl.when(i == 0)
        def _():
            ls_ref[...] = part

        @pl.when(i > 0)
        def _():
            ls_ref[...] += part

    blk = pl.BlockSpec((RB, D), lambda i: (i, 0))
    return pl.pallas_call(
        body, grid=(lp // RB,), in_specs=[blk, blk],
        out_specs=[blk, pl.BlockSpec((8, LANE), lambda i: (0, 0))],
        out_shape=[jax.ShapeDtypeStruct((lp, D), F32), jax.ShapeDtypeStruct((8, LANE), F32)],
        name="loss_head", compiler_params=_cp())(xf, tgt)


def _peer(d):
    x, y, c = lax.axis_index("x"), lax.axis_index("y"), lax.axis_index("c")
    return (x ^ ((d >> 2) & 1), y ^ ((d >> 1) & 1), c ^ (d & 1))


def _index_of(p):
    return 4 * p[0] + 2 * p[1] + p[2]


def _all_gather(xs, name):
    n = len(xs)

    def body(*refs):
        start, forward, finish = _gather_phases(refs[:n], refs[n:2 * n], *refs[2 * n:])
        start()
        forward()
        finish()

    anyspec = pl.BlockSpec(memory_space=pl.ANY)
    return pl.pallas_call(
        body, in_specs=[anyspec] * n, out_specs=[anyspec] * n,
        out_shape=_gather_shapes(xs), scratch_shapes=_comm_sems(n), name=name)(*xs)


def _gather_shapes(xs):
    return [jax.ShapeDtypeStruct((N_DEV,) + x.shape, x.dtype) for x in xs]


def _comm_sems(n):
    return [pltpu.SemaphoreType.DMA((7 * n,)), pltpu.SemaphoreType.DMA((7 * n,)), pltpu.SemaphoreType.DMA((n,))]


def _gather_phases(x_refs, out_refs, send_sems, recv_sems, local_sems):
    n = len(x_refs)
    chips = [2, 4, 6]

    def copy(a, kk, block, to, src=None):
        slot = out_refs[a].at[_index_of(block)]
        return pltpu.make_async_remote_copy(
            src_ref=slot if src is None else src, dst_ref=slot,
            send_sem=send_sems.at[7 * a + kk], recv_sem=recv_sems.at[7 * a + kk], device_id=to,
            device_id_type=MESH)

    def local(a):
        return pltpu.make_async_copy(x_refs[a], out_refs[a].at[_index_of(_peer(0))], local_sems.at[a])

    def firsts():
        out = []
        for a in range(n):
            out.append(copy(a, 0, _peer(0), _peer(1), src=x_refs[a]))
            out += [copy(a, 1 + j, _peer(0), _peer(d), src=x_refs[a]) for j, d in enumerate(chips)]
        return out

    def passes():
        return [copy(a, 4 + j, _peer(d), _peer(1)) for j, d in enumerate(chips) for a in range(n)]

    def start():
        for a in range(n):
            local(a).start()
        for cp in firsts():
            cp.start()

    def forward():
        for j, d in enumerate(chips):
            for a in range(n):
                copy(a, 1 + j, _peer(d), _peer(0)).wait_recv()
                copy(a, 4 + j, _peer(d), _peer(1)).start()

    def finish():
        for a in range(n):
            copy(a, 0, _peer(1), _peer(0)).wait_recv()
            for j, d in enumerate(chips):
                copy(a, 4 + j, _peer(d | 1), _peer(0)).wait_recv()
        for cp in firsts() + passes():
            cp.wait_send()
        for a in range(n):
            local(a).wait()

    return start, forward, finish


def _scatter_phases(x_refs, out_refs, send_sems, recv_sems, local_sems, layer):
    n = len(x_refs)

    def land(a, dev):
        slot = out_refs[a].at[dev]
        return slot if layer is None else slot.at[layer]

    def local(a):
        my = _index_of(_peer(0))
        return pltpu.make_async_copy(x_refs[a].at[my], land(a, my), local_sems.at[a])

    def copy(a, d):
        my = _index_of(_peer(0))
        return pltpu.make_async_remote_copy(
            src_ref=x_refs[a].at[_index_of(_peer(d))], dst_ref=land(a, my),
            send_sem=send_sems.at[7 * a + d - 1], recv_sem=recv_sems.at[7 * a + d - 1], device_id=_peer(d),
            device_id_type=MESH)

    def arrival(a, d):
        frm = _index_of(_peer(d))
        return pltpu.make_async_remote_copy(
            src_ref=x_refs[a].at[frm], dst_ref=land(a, frm),
            send_sem=send_sems.at[7 * a + d - 1], recv_sem=recv_sems.at[7 * a + d - 1], device_id=_peer(d),
            device_id_type=MESH)

    def start():
        for a in range(n):
            local(a).start()
        for d in range(1, N_DEV):
            for a in range(n):
                copy(a, d).start()

    def finish():
        for d in range(1, N_DEV):
            for a in range(n):
                arrival(a, d).wait_recv()
        for d in range(1, N_DEV):
            for a in range(n):
                copy(a, d).wait_send()
        for a in range(n):
            local(a).wait()

    return start, finish


def _all_to_all(xs, bufs, layer, name):
    n = len(xs)

    def body(*refs):
        start, finish = _scatter_phases(refs[:n], refs[2 * n:3 * n], *refs[3 * n:], layer)
        start()
        finish()

    anyspec = pl.BlockSpec(memory_space=pl.ANY)
    return pl.pallas_call(
        body, in_specs=[anyspec] * (2 * n), out_specs=[anyspec] * n,
        out_shape=[jax.ShapeDtypeStruct(b.shape, b.dtype) for b in bufs],
        input_output_aliases={n + a: a for a in range(n)},
        scratch_shapes=_comm_sems(n), name=name)(*xs, *bufs)


def _adam_math(g, w, m, v):
    c1 = 1.0 - ADAM_B1 ** ADAM_STEP
    c2 = 1.0 - ADAM_B2 ** ADAM_STEP
    mn = ADAM_B1 * m + (1.0 - ADAM_B1) * g
    vn = ADAM_B2 * v + (1.0 - ADAM_B2) * (g * g)
    return -ADAM_LR * ((mn / c1) / (jnp.sqrt(vn / c2) + ADAM_EPS) + ADAM_WD * w), mn, vn


def _reduce_adam(parts, w, m, v, rb, name, row_off=0):
    depth, rows, cols = w.shape
    assert rows % rb == 0 and row_off % rb == 0

    def body(p_ref, w_ref, m_ref, v_ref, g_ref, d_ref, mo_ref, vo_ref):
        g = p_ref[0, 0].astype(F32)
        for j in range(1, N_DEV):
            g = g + p_ref[j, 0].astype(F32)
        g_ref[0] = g
        d_ref[0], mo_ref[0], vo_ref[0] = _adam_math(g, w_ref[0], m_ref[0], v_ref[0])

    blk = pl.BlockSpec((1, rb, cols), lambda l, i: (l, i, 0))
    out = jax.ShapeDtypeStruct(w.shape, F32)
    return pl.pallas_call(
        body, grid=(depth, rows // rb),
        in_specs=[pl.BlockSpec((N_DEV, 1, rb, cols), lambda l, i: (0, l, i + row_off // rb, 0)), blk, blk, blk],
        out_specs=[blk, blk, blk, blk], out_shape=[out, out, out, out],
        name=name, compiler_params=_cp())(parts, w, m, v)


def _reduce_adam_flat(parts, w, m, v, name):
    q_rows = w.shape[0]

    def body(p_ref, w_ref, m_ref, v_ref, g_ref, d_ref, mo_ref, vo_ref):
        g = p_ref[0].astype(F32)
        for j in range(1, N_DEV):
            g = g + p_ref[j].astype(F32)
        g_ref[...] = g
        d_ref[...], mo_ref[...], vo_ref[...] = _adam_math(g, w_ref[...], m_ref[...], v_ref[...])

    blk = pl.BlockSpec((q_rows, LANE), lambda i: (0, 0))
    out = jax.ShapeDtypeStruct((q_rows, LANE), F32)
    return pl.pallas_call(
        body, grid=(1,), in_specs=[pl.BlockSpec((N_DEV, q_rows, LANE), lambda i: (0, 0, 0)), blk, blk, blk],
        out_specs=[blk, blk, blk, blk], out_shape=[out, out, out, out],
        name=name, compiler_params=_cp())(parts, w, m, v)


C128 = (("w_out_pool", 256), ("w_out_mla", 512), ("w_out_conf", 256), ("w_out_sc", 256), ("w_ukv", 128))
C128_ROWS = sum(r for _, r in C128)
TAIL = (("meta_tokens", (N_META, 128)), ("conf_dw_w", (DEPTH, CONF_K, 32)), ("sc_dw_w", (DEPTH, SC_K, 32)))
TAIL_ROWS = sum(int(np.prod(s)) for _, s in TAIL) // LANE
TAIL_PAD = 56
SMALL = (("pre_norm_g", (DEPTH, D)), ("gate_bias", (DEPTH, 4096)), ("pool_w", (DEPTH, 4, 64, 64)),
         ("pool_scale", (DEPTH, 256)), ("q_norm_g", (DEPTH, 256)), ("kv_norm_g", (DEPTH, 128)),
         ("conf_dw_b", (DEPTH, 256)), ("conf_ln_g", (DEPTH, 256)), ("conf_ln_b", (DEPTH, 256)),
         ("post_norm_g", (DEPTH, D)))
SMALL_ROWS = sum(int(np.prod(s)) for _, s in SMALL) // LANE
SMALL_PAD = -(-(SMALL_ROWS + 1) // 8) * 8


def _pack_tail(t):
    parts = [t[n].reshape(-1, LANE) for n, _ in TAIL]
    parts.append(jnp.zeros((TAIL_PAD - TAIL_ROWS, LANE), F32))
    return jnp.concatenate(parts, axis=0)


def _unpack_tail(flat):
    out, off = {}, 0
    for n, s in TAIL:
        rows = int(np.prod(s)) // LANE
        out[n] = flat[off:off + rows].reshape(s)
        off += rows
    return out


def _unpack_tail_full(g):
    out, off = {}, 0
    for n, s in TAIL:
        rows = int(np.prod(s)) // LANE
        blk = jnp.moveaxis(g[:, off:off + rows].reshape((N_DEV,) + s), 0, -2)
        out[n] = blk.reshape(s[:-1] + (N_DEV * s[-1],))
        off += rows
    return out


def _pack_small(t, extra_row):
    parts = [t[n].reshape(-1, LANE) for n, _ in SMALL] + [extra_row]
    parts.append(jnp.zeros((SMALL_PAD - SMALL_ROWS - 1, LANE), F32))
    return jnp.concatenate(parts, axis=0)


def _unpack_small(flat):
    out, off = {}, 0
    for n, s in SMALL:
        rows = int(np.prod(s)) // LANE
        out[n] = flat[off:off + rows].reshape(s)
        off += rows
    return out


def _cols_by_dest(g, width):
    r = g.shape[0]
    return g.reshape(r, N_DEV, width).transpose(1, 0, 2)


def _cols_full(gathered):
    _, r, c = gathered.shape
    return gathered.transpose(1, 0, 2).reshape(r, N_DEV * c)


def _pack_w_in(w):
    r = w.shape[0]
    z = lambda n: jnp.zeros((r, n), w.dtype)
    return jnp.concatenate([w[:, 3232:7328], w[:, 512:896], z(64), w[:, 896:928], z(32), w[:, 0:512],
                            w[:, 928:3232]], axis=1)


def _unpack_w_in(gl, mla, mix):
    return jnp.concatenate([mix[:, 0:512], mla[:, 0:384], mla[:, 448:480], mix[:, 512:2816], gl], axis=1)


def _rope_tables(lp):
    inv = 1.0 / (ROPE_THETA ** (jnp.arange(0, QK_ROPE, 2, dtype=F32) / QK_ROPE))
    ang = jnp.arange(lp, dtype=F32)[:, None] * inv[None, :]
    cos, sin = jnp.cos(ang), jnp.sin(ang)
    one = jnp.ones((lp, QK_NOPE), F32)
    zero = jnp.zeros((lp, QK_NOPE), F32)
    z16 = jnp.zeros((lp, 16), F32)
    c = jnp.concatenate([one, cos, cos, jnp.ones((lp, 32), F32)], axis=1)
    s1 = jnp.concatenate([zero, z16, sin, jnp.zeros((lp, 32), F32)], axis=1)
    s2 = jnp.concatenate([zero, -sin, z16, jnp.zeros((lp, 32), F32)], axis=1)
    return c, s1, s2


def kernel(x, meta_tokens, pre_norm_g, w_in, gate_bias, pool_w, pool_scale, w_out_pool, q_norm_g, w_uq, kv_norm_g, w_ukv, w_out_mla, conf_dw_w, conf_dw_b, conf_ln_g, conf_ln_b, w_out_conf, sc_dw_w, w_out_sc, w_o, post_norm_g, loss_target, m_meta_tokens, m_pre_norm_g, m_w_in, m_gate_bias, m_pool_w, m_pool_scale, m_w_out_pool, m_q_norm_g, m_w_uq, m_kv_norm_g, m_w_ukv, m_w_out_mla, m_conf_dw_w, m_conf_dw_b, m_conf_ln_g, m_conf_ln_b, m_w_out_conf, m_sc_dw_w, m_w_out_sc, m_w_o, m_post_norm_g, v_meta_tokens, v_pre_norm_g, v_w_in, v_gate_bias, v_pool_w, v_pool_scale, v_w_out_pool, v_q_norm_g, v_w_uq, v_kv_norm_g, v_w_ukv, v_w_out_mla, v_conf_dw_w, v_conf_dw_b, v_conf_ln_g, v_conf_ln_b, v_w_out_conf, v_sc_dw_w, v_w_out_sc, v_w_o, v_post_norm_g):
    names = ["w_in", "w_uq", "w_o"] + [n for n, _ in C128] + [n for n, _ in TAIL] + [n for n, _ in SMALL]
    loc = locals()
    w = {n: loc[n] for n in names}
    mom = {n: loc["m_" + n] for n in names}
    vel = {n: loc["v_" + n] for n in names}

    seq = x.shape[1]
    n_real = N_META + seq
    lp = -(-n_real // RB) * RB
    tmb = lp // 3
    tabs = _rope_tables(lp)

    c128 = jnp.concatenate([w[n] for n, _ in C128], axis=1)
    def shards_of(i):
        return [w_in[i].astype(BF16), c128[i].astype(BF16), w_uq[i].astype(BF16), w_o[i].astype(BF16)]

    gathered = [_all_gather(shards_of(0), "gather_weights")] + [None] * (DEPTH - 1)
    tail_w = _pack_tail(w)
    tail = _unpack_tail_full(_all_gather([tail_w], "gather_tail")[0])
    eye4 = jnp.eye(4, dtype=F32)
    bd_all = (pool_w[:, :, :, None, :] * eye4[None, :, None, :, None]).reshape(DEPTH, 256, 256).astype(BF16)

    def layer_weights(i):
        g_in, g_c128, g_uq, g_o = gathered[i]
        lw = {}
        lw["w_in"] = _pack_w_in(_cols_full(g_in))
        lw["wc"] = _cols_full(g_c128)
        wuq = _cols_full(g_uq).reshape(Q_RANK, HEADS, 96)
        lw["w_uq"] = jnp.pad(wuq, ((0, 0), (0, 0), (0, 32))).reshape(Q_RANK, HEADS * LANE)
        wukv = lw["wc"][U_OFF[4]:].reshape(KV_RANK, HEADS, 128)
        wk = jnp.pad(wukv[:, :, :QK_NOPE], ((0, 0), (0, 0), (0, 64))).reshape(KV_RANK, HEADS * LANE)
        lw["w_ukv"] = jnp.concatenate([wk, wukv[:, :, QK_NOPE:].reshape(KV_RANK, HEADS * V_DIM)], axis=1)
        lw["w_o"] = g_o.reshape(D, D)
        lw["bd"] = bd_all[i]
        lw["cw"] = jnp.pad(tail["conf_dw_w"][i], ((0, 1), (0, 0)))
        lw["sw"] = jnp.pad(tail["sc_dw_w"][i], ((0, 8 - SC_K), (0, 0)))
        return lw

    meta_full = tail["meta_tokens"]

    pad_rows = lp - n_real
    xr = jnp.concatenate([meta_full, x[0], jnp.zeros((pad_rows, D), F32)], axis=0)
    tgt = jnp.pad(loss_target[0], ((N_META, pad_rows), (0, 0)))
    saved = []
    for i in range(DEPTH):
        lw = layer_weights(i)
        h = _rms_fwd(xr, pre_norm_g[i:i + 1], lp)
        z = _mm(h, lw["w_in"], lp, ZW, D, tm=tmb, tn=256, tk=D, name="mm_in")
        qn, kvn, krr = _mla_prep(z, q_norm_g[i:i + 1], kv_norm_g[i:i + 1], tabs, lp)
        q_raw = _mm(qn, lw["w_uq"], lp, 1024, Q_RANK, tm=tmb, tn=1024, tk=Q_RANK, name="mm_uq")
        kv_raw = _mm(kvn, lw["w_ukv"], lp, 1536, KV_RANK, tm=tmb, tn=512, tk=KV_RANK, name="mm_ukv")
        qt, kt, vt = _mla_post(q_raw, kv_raw, krr, tabs, lp)
        res = _attn_fwd(qt, kt, vt, lp, gather=shards_of(i + 1) if i + 1 < DEPTH else ())
        oat, lse = res[0], res[1]
        if i + 1 < DEPTH:
            gathered[i + 1] = res[2:]
        u = _mix_fwd(z, oat, lw["bd"], pool_scale[i:i + 1], lw["cw"], conf_dw_b[i:i + 1], conf_ln_g[i:i + 1],
                     conf_ln_b[i:i + 1], lw["sw"], lp)
        x_new, m_act, o2 = _merge_fwd(xr, u, z, gate_bias[i:i + 1], lw["wc"], lw["w_o"],
                                      post_norm_g[i:i + 1], lp)
        saved.append(dict(lw=lw, x=xr, h=h, z=z, qn=qn, kvn=kvn, qt=qt, kt=kt, vt=vt, oat=oat, lse=lse,
                          u=u, m=m_act, o2=o2))
        xr = x_new

    dx, loss_part = _loss_head(xr, tgt, n_real, lp)

    gsm = {n: [None] * DEPTH for n, _ in SMALL}
    g_cw = [None] * DEPTH
    g_sw = [None] * DEPTH
    recv = [lax.empty((N_DEV, DEPTH) + s, BF16) for s in ((D, 916), (C128_ROWS, 128), (Q_RANK, 96), (128, D))]
    pending = None
    for i in reversed(range(DEPTH)):
        s = saved[i]
        lw = s["lw"]
        do2, dgl, dyb, du, dgb, dgpost = _merge_bwd(dx, s["o2"], s["u"], s["z"], gate_bias[i:i + 1],
                                                    lw["wc"], lw["w_o"], post_norm_g[i:i + 1], lp)
        d_wo = _mm(s["m"], do2, D, D, lp, ta=True, tm=512, tn=D, tk=tmb, name="mm_dwo")
        d_wout = []
        for b in range(4):
            rows = U_OFF[b + 1] - U_OFF[b]
            d_wout.append(_mm(s["u"], dyb, rows, D, lp, ta=True, tm=256, tn=D, tk=tmb,
                              a_moff=U_OFF[b] // 256, b_noff=b, name="mm_dwout%d" % b))
        dzx, doat, dbd, dcw, dsw, dsm = _mix_bwd(
            s["z"], s["oat"], du, lw["bd"], pool_scale[i:i + 1], lw["cw"], conf_dw_b[i:i + 1],
            conf_ln_g[i:i + 1], conf_ln_b[i:i + 1], lw["sw"], lp)
        res = _attn_bwd(s["qt"], s["kt"], s["vt"], s["oat"], doat, s["lse"], lp,
                        scatter=(pending, recv, i + 1) if pending else None)
        dqt, dkt, dvt = res[:3]
        if pending:
            recv = list(res[3:])
        dq_raw, dkv_raw, dkr = _mla_post_bwd(dqt, dkt, dvt, tabs, lp)
        dqn = _mm(dq_raw, lw["w_uq"], lp, Q_RANK, 1024, tb=True, tm=tmb, tn=Q_RANK, tk=1024, name="mm_dqn")
        d_wuq = _mm(s["qn"], dq_raw, Q_RANK, 1024, lp, ta=True, tm=Q_RANK, tn=1024, tk=tmb, name="mm_dwuq")
        dkvn = _mm(dkv_raw, lw["w_ukv"], lp, KV_RANK, 1536, tb=True, tm=tmb, tn=KV_RANK, tk=1536, name="mm_dkvn")
        d_wukv = _mm(s["kvn"], dkv_raw, KV_RANK, 1536, lp, ta=True, tm=KV_RANK, tn=1536, tk=tmb, name="mm_dwukv")
        dzq, dqg, dkvg = _mla_prep_bwd(s["z"], q_norm_g[i:i + 1], kv_norm_g[i:i + 1], dqn, dkvn, dkr, lp)
        dh = _mm(dgl, lw["w_in"], lp, D, 4096, tb=True, tm=RB, tn=D, tk=2048, name="mm_dh0")
        dh = _mm(dzq, lw["w_in"], lp, D, 512, tb=True, tm=tmb, tn=D, tk=512, b_koff=CQ0 // 512, c=dh,
                 name="mm_dh1")
        dh = _mm(dzx, lw["w_in"], lp, D, ZW - PV0, tb=True, tm=tmb, tn=D, tk=256, b_koff=PV0 // 256, c=dh,
                 name="mm_dh2")
        d_win = [_mm(s["h"], seg, D, seg.shape[1], lp, ta=True, tm=D, tn=256, tk=tmb, name="mm_dwin%d" % k)
                 for k, seg in enumerate((dgl, dzq, dzx))]
        dx, dgpre = _rms_bwd(s["x"], pre_norm_g[i:i + 1], dh, dx, lp)

        gsm["pre_norm_g"][i] = dgpre[0]
        gsm["gate_bias"][i] = dgb[0]
        gsm["pool_w"][i] = jnp.stack([dbd[64 * g:64 * (g + 1), 64 * g:64 * (g + 1)] for g in range(4)])
        gsm["pool_scale"][i] = dsm[0]
        gsm["conf_dw_b"][i] = dsm[1]
        gsm["conf_ln_g"][i] = dsm[2]
        gsm["conf_ln_b"][i] = dsm[3]
        gsm["q_norm_g"][i] = dqg[0]
        gsm["kv_norm_g"][i] = dkvg[0]
        gsm["post_norm_g"][i] = dgpost[0]
        g_cw[i] = dcw[:CONF_K]
        g_sw[i] = dsw[:SC_K]
        d_wuq_o = d_wuq.reshape(Q_RANK, HEADS, LANE)[:, :, :96].reshape(Q_RANK, HEADS * 96)
        d_wukv_o = jnp.concatenate([d_wukv[:, :1024].reshape(KV_RANK, HEADS, LANE)[:, :, :QK_NOPE],
                                    d_wukv[:, 1024:].reshape(KV_RANK, HEADS, V_DIM)], axis=2).reshape(KV_RANK, 1024)
        pending = [
            _cols_by_dest(_unpack_w_in(*d_win), 916).astype(BF16),
            _cols_by_dest(jnp.concatenate(d_wout + [d_wukv_o], axis=0), 128).astype(BF16),
            _cols_by_dest(d_wuq_o, 96).astype(BF16),
            d_wo.reshape(N_DEV, 128, D).astype(BF16)]
    recv = _all_to_all(pending, recv, 0, "scatter_grads")

    outs = [dict() for _ in range(4)]

    def put(n, res):
        for t, r in zip(outs, res):
            t[n] = r

    put("w_in", _reduce_adam(recv[0], w["w_in"], mom["w_in"], vel["w_in"], 256, "adam_w_in"))
    off = 0
    for n, rows in C128:
        put(n, _reduce_adam(recv[1], w[n], mom[n], vel[n], 128, "adam_" + n, row_off=off))
        off += rows
    put("w_uq", _reduce_adam(recv[2], w["w_uq"], mom["w_uq"], vel["w_uq"], Q_RANK, "adam_w_uq"))
    put("w_o", _reduce_adam(recv[3], w["w_o"], mom["w_o"], vel["w_o"], 128, "adam_w_o"))

    tail_g = {"meta_tokens": _cols_by_dest(dx[:N_META], 128),
              "conf_dw_w": jnp.moveaxis(jnp.stack(g_cw).reshape(DEPTH, CONF_K, N_DEV, 32), 2, 0),
              "sc_dw_w": jnp.moveaxis(jnp.stack(g_sw).reshape(DEPTH, SC_K, N_DEV, 32), 2, 0)}
    tail_bd = jnp.concatenate([tail_g[n].reshape(N_DEV, -1, LANE) for n, _ in TAIL]
                              + [jnp.zeros((N_DEV, TAIL_PAD - TAIL_ROWS, LANE), F32)], axis=1)
    tail_recv = _all_to_all([tail_bd], [lax.empty((N_DEV, TAIL_PAD, LANE), F32)], None, "scatter_tail")[0]
    tail_res = _reduce_adam_flat(tail_recv, tail_w, _pack_tail(mom), _pack_tail(vel), "adam_tail")

    small_g = {n: jnp.stack(gsm[n]) for n, _ in SMALL}
    loss_row = jnp.concatenate([loss_part[0:1, 0:1], jnp.zeros((1, LANE - 1), F32)], axis=1)
    zrow = jnp.zeros((1, LANE), F32)
    parts = _all_gather([_pack_small(small_g, loss_row)], "gather_small_grads")[0]
    small_res = _reduce_adam_flat(parts, _pack_small(w, zrow), _pack_small(mom, zrow), _pack_small(vel, zrow),
                                  "adam_small")
    loss = small_res[0][SMALL_ROWS, 0]
    for t, tf, sf in zip(outs, tail_res, small_res):
        t.update(_unpack_tail(tf))
        t.update(_unpack_small(sf))
    order = ["meta_tokens", "pre_norm_g", "w_in", "gate_bias", "pool_w", "pool_scale", "w_out_pool", "q_norm_g",
             "w_uq", "kv_norm_g", "w_ukv", "w_out_mla", "conf_dw_w", "conf_dw_b", "conf_ln_g", "conf_ln_b",
             "w_out_conf", "sc_dw_w", "w_out_sc", "w_o", "post_norm_g"]
    grad_x = dx[N_META:n_real][None]
    return (loss, grad_x, *[t[n] for t in outs for n in order])
```

```python
import functools

import jax
import jax.numpy as jnp
import numpy as np
from jax import lax
from jax.experimental import pallas as pl
from jax.experimental.pallas import tpu as pltpu

F32 = jnp.float32
BF16 = jnp.bfloat16

D = 1024
N_META = 16
DEPTH = 4
EPS = 1e-6
HEADS = 8
QK_NOPE = 64
QK_ROPE = 32
V_DIM = 64
Q_RANK = 256
KV_RANK = 128
ROPE_THETA = 10000.0
SCALE = (QK_NOPE + QK_ROPE) ** -0.5
CONF_K = 31
SC_K = 3
N_DEV = 8

ADAM_LR = 0.001
ADAM_B1 = 0.9
ADAM_B2 = 0.999
ADAM_EPS = 1e-08
ADAM_WD = 0.01
ADAM_STEP = 10

RB = 384
HB = 32
LANE = 128
VMEM_LIMIT = 56 * 1024 * 1024

GL0, CQ0, CKV0, KR0, PV0, PG0, MG0, CU0, CG0, SBB0, SBC0, SBX0, SG0, ZW = (
    0, 4096, 4352, 4480, 4608, 4864, 5120, 5632, 6144, 6400, 6656, 6912, 7168, 7424)
ZSEG = ((0, 4096), (4096, 512), (4608, 2816))
LOG2E = 1.4426950408889634
LN2 = 0.6931471805599453

MESH = pl.DeviceIdType.MESH


def _cp(**kw):
    return pltpu.CompilerParams(vmem_limit_bytes=VMEM_LIMIT, **kw)


def _sig(x):
    return jax.nn.sigmoid(x)


def _silu(x):
    return x * _sig(x)


def _dsilu(x):
    s = _sig(x)
    return s * (1.0 + x * (1.0 - s))


def _dn(x, k):
    return x if k == 0 else pltpu.roll(x, k, 0)


def _up(x, k):
    return x if k == 0 else pltpu.roll(x, x.shape[0] - k, 0)


def _rope(t, c, s1, s2):
    return t * c + pltpu.roll(t, 16, 1) * s1 + pltpu.roll(t, LANE - 16, 1) * s2


def _rope_t(g, c, s1, s2):
    return g * c + pltpu.roll(g * s1, LANE - 16, 1) + pltpu.roll(g * s2, 16, 1)


def _mm(a, b, m, n, k, *, ta=False, tb=False, out_dtype=F32, tm, tn, tk, name,
        a_moff=0, a_koff=0, b_noff=0, b_koff=0, c=None):
    assert m % tm == 0 and n % tn == 0 and k % tk == 0, (name, m, n, k, tm, tn, tk)
    nk = k // tk
    dims = (((0,) if ta else (1,), (1,) if tb else (0,)), ((), ()))
    has_c = c is not None

    def body(a_ref, b_ref, *rest):
        c_ref = rest[0] if has_c else None
        o_ref = rest[1] if has_c else rest[0]
        scr = rest[2:] if has_c else rest[1:]
        part = lax.dot_general(a_ref[...].astype(BF16), b_ref[...].astype(BF16), dims,
                               preferred_element_type=F32)

        def finish(total):
            if has_c:
                total = total + c_ref[...]
            o_ref[...] = total.astype(out_dtype)

        if nk == 1:
            finish(part)
        else:
            acc = scr[0]
            kk = pl.program_id(2)

            @pl.when(kk == 0)
            def _():
                acc[...] = part

            @pl.when(kk > 0)
            def _():
                acc[...] += part

            @pl.when(kk == nk - 1)
            def _():
                finish(acc[...])

    if ta:
        a_spec = pl.BlockSpec((tk, tm), lambda i, j, q: (q + a_koff, i + a_moff))
    else:
        a_spec = pl.BlockSpec((tm, tk), lambda i, j, q: (i + a_moff, q + a_koff))
    if tb:
        b_spec = pl.BlockSpec((tn, tk), lambda i, j, q: (j + b_noff, q + b_koff))
    else:
        b_spec = pl.BlockSpec((tk, tn), lambda i, j, q: (q + b_koff, j + b_noff))
    o_spec = pl.BlockSpec((tm, tn), lambda i, j, q: (i, j))
    return pl.pallas_call(
        body, grid=(m // tm, n // tn, nk), in_specs=[a_spec, b_spec] + ([o_spec] if has_c else []),
        out_specs=o_spec, out_shape=jax.ShapeDtypeStruct((m, n), out_dtype),
        scratch_shapes=[pltpu.VMEM((tm, tn), F32)] if nk > 1 else [],
        name=name, compiler_params=_cp())(*((a, b, c) if has_c else (a, b)))


def _rms_fwd(x, g, lp):
    def body(x_ref, g_ref, h_ref):
        xv = x_ref[...]
        r = lax.rsqrt(jnp.mean(xv * xv, axis=-1, keepdims=True) + EPS)
        h_ref[...] = (xv * r * g_ref[...]).astype(BF16)

    return pl.pallas_call(
        body, grid=(lp // RB,),
        in_specs=[pl.BlockSpec((RB, D), lambda i: (i, 0)), pl.BlockSpec((1, D), lambda i: (0, 0))],
        out_specs=pl.BlockSpec((RB, D), lambda i: (i, 0)),
        out_shape=jax.ShapeDtypeStruct((lp, D), BF16), name="rms_fwd", compiler_params=_cp())(x, g)


def _rms_bwd(x, g, dh, dx_in, lp):
    def body(x_ref, g_ref, dh_ref, dxi_ref, dx_ref, dg_ref):
        i = pl.program_id(0)
        xv = x_ref[...]
        r = lax.rsqrt(jnp.mean(xv * xv, axis=-1, keepdims=True) + EPS)
        dy = dh_ref[...]
        a = dy * g_ref[...]
        dx_ref[...] = dxi_ref[...] + r * a - xv * (r * r * r) * jnp.mean(a * xv, axis=-1, keepdims=True)
        part = jnp.sum(dy * xv * r, axis=0, keepdims=True)

        @pl.when(i == 0)
        def _():
            dg_ref[...] = part

        @pl.when(i > 0)
        def _():
            dg_ref[...] += part

    blk = pl.BlockSpec((RB, D), lambda i: (i, 0))
    vec = pl.BlockSpec((1, D), lambda i: (0, 0))
    return pl.pallas_call(
        body, grid=(lp // RB,), in_specs=[blk, vec, blk, blk], out_specs=[blk, vec],
        out_shape=[jax.ShapeDtypeStruct((lp, D), F32), jax.ShapeDtypeStruct((1, D), F32)],
        name="rms_bwd", compiler_params=_cp())(x, g, dh, dx_in)


def _mla_prep(z, qg, kvg, tabs, lp):
    def body(z_ref, qg_ref, kvg_ref, c_ref, s1_ref, s2_ref, qn_ref, kvn_ref, kr_ref):
        cq = z_ref[:, 0:256]
        ckv = z_ref[:, 256:384]
        kr = z_ref[:, 384:512]
        rq = lax.rsqrt(jnp.mean(cq * cq, axis=-1, keepdims=True) + EPS)
        rk = lax.rsqrt(jnp.mean(ckv * ckv, axis=-1, keepdims=True) + EPS)
        qn_ref[...] = (cq * rq * qg_ref[...]).astype(BF16)
        kvn_ref[...] = (ckv * rk * kvg_ref[...]).astype(BF16)
        kr_ref[...] = _rope(kr, c_ref[...], s1_ref[...], s2_ref[...])

    tab = pl.BlockSpec((RB, LANE), lambda i: (i, 0))
    return pl.pallas_call(
        body, grid=(lp // RB,),
        in_specs=[pl.BlockSpec((RB, 512), lambda i: (i, CQ0 // 512)),
                  pl.BlockSpec((1, 256), lambda i: (0, 0)), pl.BlockSpec((1, 128), lambda i: (0, 0)),
                  tab, tab, tab],
        out_specs=[pl.BlockSpec((RB, 256), lambda i: (i, 0)), tab, tab],
        out_shape=[jax.ShapeDtypeStruct((lp, 256), BF16), jax.ShapeDtypeStruct((lp, 128), BF16),
                   jax.ShapeDtypeStruct((lp, 128), F32)],
        name="mla_prep", compiler_params=_cp())(z, qg, kvg, *tabs)


def _mla_prep_bwd(z, qg, kvg, dqn, dkvn, dkr, lp):
    def body(z_ref, qg_ref, kvg_ref, dqn_ref, dkvn_ref, dkr_ref, dz_ref, dqg_ref, dkvg_ref):
        i = pl.program_id(0)

        def rms_b(xv, g, dy):
            r = lax.rsqrt(jnp.mean(xv * xv, axis=-1, keepdims=True) + EPS)
            a = dy * g
            dx = r * a - xv * (r * r * r) * jnp.mean(a * xv, axis=-1, keepdims=True)
            return dx, jnp.sum(dy * xv * r, axis=0, keepdims=True)

        dcq, pq = rms_b(z_ref[:, 0:256], qg_ref[...], dqn_ref[...])
        dckv, pk = rms_b(z_ref[:, 256:384], kvg_ref[...], dkvn_ref[...])
        dz_ref[:, 0:256] = dcq.astype(BF16)
        dz_ref[:, 256:384] = dckv.astype(BF16)
        dz_ref[:, 384:512] = dkr_ref[...].astype(BF16)

        @pl.when(i == 0)
        def _():
            dqg_ref[...] = pq
            dkvg_ref[...] = pk

        @pl.when(i > 0)
        def _():
            dqg_ref[...] += pq
            dkvg_ref[...] += pk

    tab = pl.BlockSpec((RB, LANE), lambda i: (i, 0))
    return pl.pallas_call(
        body, grid=(lp // RB,),
        in_specs=[pl.BlockSpec((RB, 512), lambda i: (i, CQ0 // 512)),
                  pl.BlockSpec((1, 256), lambda i: (0, 0)), pl.BlockSpec((1, 128), lambda i: (0, 0)),
                  pl.BlockSpec((RB, 256), lambda i: (i, 0)), tab, tab],
        out_specs=[pl.BlockSpec((RB, 512), lambda i: (i, 0)),
                   pl.BlockSpec((1, 256), lambda i: (0, 0)), pl.BlockSpec((1, 128), lambda i: (0, 0))],
        out_shape=[jax.ShapeDtypeStruct((lp, 512), BF16), jax.ShapeDtypeStruct((1, 256), F32),
                   jax.ShapeDtypeStruct((1, 128), F32)],
        name="mla_prep_bwd", compiler_params=_cp())(z, qg, kvg, dqn, dkvn, dkr)


def _mla_post(q_raw, kv_raw, krr, tabs, lp):
    def body(q_ref, kv_ref, kr_ref, c_ref, s1_ref, s2_ref, qo_ref, ko_ref, vo_ref):
        c, s1, s2, kr = c_ref[...], s1_ref[...], s2_ref[...], kr_ref[...]
        for h in range(HEADS):
            sl = slice(LANE * h, LANE * (h + 1))
            qo_ref[:, sl] = (_rope(q_ref[:, sl], c, s1, s2) * (SCALE * LOG2E)).astype(BF16)
            ko_ref[:, sl] = (kv_ref[:, sl] + kr).astype(BF16)
        vo_ref[...] = kv_ref[:, 1024:1536].astype(BF16)

    tab = pl.BlockSpec((RB, LANE), lambda i: (i, 0))
    wide = pl.BlockSpec((RB, 1024), lambda i: (i, 0))
    return pl.pallas_call(
        body, grid=(lp // RB,),
        in_specs=[wide, pl.BlockSpec((RB, 1536), lambda i: (i, 0)), tab, tab, tab, tab],
        out_specs=[wide, wide, pl.BlockSpec((RB, 512), lambda i: (i, 0))],
        out_shape=[jax.ShapeDtypeStruct((lp, 1024), BF16), jax.ShapeDtypeStruct((lp, 1024), BF16),
                   jax.ShapeDtypeStruct((lp, 512), BF16)],
        name="mla_post", compiler_params=_cp())(q_raw, kv_raw, krr, *tabs)


def _mla_post_bwd(dq, dk, dv, tabs, lp):
    def body(dq_ref, dk_ref, dv_ref, c_ref, s1_ref, s2_ref, dqr_ref, dkv_ref, dkr_ref):
        c, s1, s2 = c_ref[...], s1_ref[...], s2_ref[...]
        lane = lax.broadcasted_iota(jnp.int32, (1, LANE), 1)
        ropel = (lane >= QK_NOPE) & (lane < QK_NOPE + QK_ROPE)
        ksum = jnp.zeros((RB, LANE), F32)
        for h in range(HEADS):
            sl = slice(LANE * h, LANE * (h + 1))
            dqr_ref[:, sl] = _rope_t(dq_ref[:, sl].astype(F32) * SCALE, c, s1, s2).astype(BF16)
            dkt = dk_ref[:, sl]
            dkv_ref[:, sl] = dkt
            ksum = ksum + dkt.astype(F32)
        dkv_ref[:, 1024:1536] = dv_ref[...]
        dkr_ref[...] = jnp.where(ropel, _rope_t(jnp.where(ropel, ksum, 0.0), c, s1, s2), 0.0)

    tab = pl.BlockSpec((RB, LANE), lambda i: (i, 0))
    wide = pl.BlockSpec((RB, 1024), lambda i: (i, 0))
    return pl.pallas_call(
        body, grid=(lp // RB,),
        in_specs=[wide, wide, pl.BlockSpec((RB, 512), lambda i: (i, 0)), tab, tab, tab],
        out_specs=[wide, pl.BlockSpec((RB, 1536), lambda i: (i, 0)), tab],
        out_shape=[jax.ShapeDtypeStruct((lp, 1024), BF16), jax.ShapeDtypeStruct((lp, 1536), BF16),
                   jax.ShapeDtypeStruct((lp, 128), F32)],
        name="mla_post_bwd", compiler_params=_cp())(dq, dk, dv, *tabs)


def _head_lanes(e):
    lane = lax.broadcasted_iota(jnp.int32, (1, LANE), 1)
    return lane >= V_DIM if e else lane < V_DIM


ONE_LANE = (V_DIM, 0)


def _attn_fwd(q, k, v, lp, gather=()):
    nq = lp // RB
    n = len(gather)
    steps = HEADS // 2

    def body(q_ref, k_ref, v_ref, *rest):
        o_ref, lse_ref = rest[n], rest[n + 1]
        vm_scr = rest[2 * n + 2]
        if n:
            g_start, g_forward, g_finish = _gather_phases(rest[:n], rest[n + 2:2 * n + 2], *rest[2 * n + 3:])
            pl.when(pl.program_id(0) == 0)(g_start)
            pl.when(pl.program_id(0) == steps // 2)(g_forward)
        lane = lax.broadcasted_iota(jnp.int32, (1, LANE), 1)
        vv = v_ref[...]
        for e in range(2):
            ones = jnp.where(lane == ONE_LANE[e], 1.0, 0.0).astype(BF16)
            vm_scr[e] = jnp.where(_head_lanes(e), vv, jnp.broadcast_to(ones, vv.shape))
        causal = (lax.broadcasted_iota(jnp.int32, (RB, RB), 1) <= lax.broadcasted_iota(jnp.int32, (RB, RB), 0))

        def qblock(i, _):
            rows = pl.ds(pl.multiple_of(i * RB, RB), RB)
            qs = [q_ref[rows, LANE * e:LANE * (e + 1)] for e in range(2)]

            def scores(j):
                cols = pl.ds(pl.multiple_of(j * RB, RB), RB)
                return tuple(lax.dot_general(qs[e], k_ref[cols, LANE * e:LANE * (e + 1)], (((1,), (1,)), ((), ())),
                                             preferred_element_type=F32) for e in range(2))

            def update(j, s, carry, masked):
                cols = pl.ds(pl.multiple_of(j * RB, RB), RB)
                out = []
                for e in range(2):
                    m, acc = carry[2 * e], carry[2 * e + 1]
                    se = jnp.where(causal, s[e], -jnp.inf) if masked else s[e]
                    m_new = jnp.maximum(m, jnp.max(se, axis=-1, keepdims=True))
                    p = jnp.exp2((se - m_new).astype(BF16))
                    acc = jnp.exp2(m - m_new) * acc + jnp.dot(p, vm_scr[e, cols, :], preferred_element_type=F32)
                    out += [m_new, acc]
                return tuple(out)

            def step(j, c):
                s_next = scores(j + 1)
                return update(j, c[4:], c[:4], False) + s_next

            m0 = jnp.full((RB, 1), -jnp.inf, F32)
            a0 = jnp.zeros((RB, LANE), F32)
            c = lax.fori_loop(0, i, step, (m0, a0, m0, a0) + scores(0))
            carry = update(i, c[4:], c[:4], True)
            o, lse = [], []
            for e in range(2):
                m, acc = carry[2 * e], carry[2 * e + 1]
                l = acc[:, ONE_LANE[e]:ONE_LANE[e] + 1]
                o.append(acc / l)
                lse.append(jnp.broadcast_to(m + jnp.log2(l), (RB, LANE)))
            o_ref[rows, :] = jnp.where(_head_lanes(0), o[0], o[1])
            lse_ref[rows, :] = jnp.where(_head_lanes(0), lse[0], lse[1])
            return 0

        lax.fori_loop(0, nq, qblock, 0)
        if n:
            pl.when(pl.program_id(0) == steps - 1)(g_finish)

    two = pl.BlockSpec((lp, 2 * LANE), lambda h: (0, h))
    one = pl.BlockSpec((lp, LANE), lambda h: (0, h))
    anyspec = pl.BlockSpec(memory_space=pl.ANY)
    return pl.pallas_call(
        body, grid=(steps,), in_specs=[two, two, one] + [anyspec] * n, out_specs=[one, one] + [anyspec] * n,
        out_shape=[jax.ShapeDtypeStruct((lp, 512), F32), jax.ShapeDtypeStruct((lp, 512), F32)]
        + _gather_shapes(gather),
        scratch_shapes=[pltpu.VMEM((2, lp, LANE), BF16)] + (_comm_sems(n) if n else []),
        name="attn_fwd_gather" if n else "attn_fwd", compiler_params=_cp())(q, k, v, *gather)


def _attn_bwd(q, k, v, o, do, lse, lp, scatter=None):
    nq = lp // RB
    xs, bufs, layer = scatter if scatter else ((), (), None)
    n = len(xs)
    steps = HEADS // 2

    def body(q_ref, k_ref, v_ref, o_ref, do_ref, lse_ref, *rest):
        dq_ref, dk_ref, dv_ref = rest[2 * n:2 * n + 3]
        vm_scr, dom_scr, dl_scr, dq_scr = rest[3 * n + 3:3 * n + 7]
        if n:
            s_start, s_finish = _scatter_phases(rest[:n], rest[2 * n + 3:3 * n + 3], *rest[3 * n + 7:], layer)
            pl.when(pl.program_id(0) == 0)(s_start)
        causal = (lax.broadcasted_iota(jnp.int32, (RB, RB), 1) <= lax.broadcasted_iota(jnp.int32, (RB, RB), 0))
        vv = v_ref[...]
        for e in range(2):
            vm_scr[e] = jnp.where(_head_lanes(e), vv, jnp.zeros_like(vv))

        def prep(i, _):
            rows = pl.ds(pl.multiple_of(i * RB, RB), RB)
            prod = do_ref[rows, :] * o_ref[rows, :]
            dls = []
            for e in range(2):
                hm = _head_lanes(e)
                dom_scr[e, rows, :] = jnp.where(hm, do_ref[rows, :], 0.0).astype(BF16)
                dls.append(jnp.sum(jnp.where(hm, prod, 0.0), axis=-1, keepdims=True))
            dl_scr[rows, :] = jnp.where(_head_lanes(0), dls[0], dls[1])
            dq_scr[rows, :] = jnp.zeros((RB, 2 * LANE), F32)
            return 0

        lax.fori_loop(0, nq, prep, 0)

        def kvblock(j, _):
            cols = pl.ds(pl.multiple_of(j * RB, RB), RB)
            kbs = [k_ref[cols, LANE * e:LANE * (e + 1)] for e in range(2)]

            def products(i):
                rows = pl.ds(pl.multiple_of(i * RB, RB), RB)
                out = []
                for e in range(2):
                    out.append(lax.dot_general(q_ref[rows, LANE * e:LANE * (e + 1)], kbs[e],
                                               (((1,), (1,)), ((), ())), preferred_element_type=F32))
                    out.append(lax.dot_general(dom_scr[e, rows, :], vm_scr[e, cols, :],
                                               (((1,), (1,)), ((), ())), preferred_element_type=F32))
                return tuple(out)

            def update(i, sd, carry, masked):
                rows = pl.ds(pl.multiple_of(i * RB, RB), RB)
                out = []
                for e in range(2):
                    dk, dv = carry[2 * e], carry[2 * e + 1]
                    sl = slice(LANE * e, LANE * (e + 1))
                    col1 = slice(V_DIM * e, V_DIM * e + 1)
                    s, dp = sd[2 * e], sd[2 * e + 1]
                    if masked:
                        s = jnp.where(causal, s, -jnp.inf)
                    p = jnp.exp2((s - lse_ref[rows, col1]).astype(BF16))
                    dv = dv + lax.dot_general(p, dom_scr[e, rows, :], (((0,), (0,)), ((), ())),
                                              preferred_element_type=F32)
                    ds = p * (dp - dl_scr[rows, col1]).astype(BF16)
                    dk = dk + lax.dot_general(ds, q_ref[rows, sl], (((0,), (0,)), ((), ())),
                                              preferred_element_type=F32)
                    dq_scr[rows, sl] += jnp.dot(ds, kbs[e], preferred_element_type=F32)
                    out += [dk, dv]
                return tuple(out)

            def step(i, c):
                nxt = products(jnp.minimum(i + 1, nq - 1))
                return update(i, c[4:], c[:4], False) + nxt

            zero = jnp.zeros((RB, LANE), F32)
            first = products(j)
            nxt = products(jnp.minimum(j + 1, nq - 1))
            carry = update(j, first, (zero, zero, zero, zero), True)
            dk0, dv0, dk1, dv1 = lax.fori_loop(j + 1, nq, step, carry + nxt)[:4]
            dk_ref[cols, 0:LANE] = (dk0 * LN2).astype(BF16)
            dk_ref[cols, LANE:2 * LANE] = (dk1 * LN2).astype(BF16)
            dv_ref[cols, :] = (dv0 + dv1).astype(BF16)
            return 0

        lax.fori_loop(0, nq, kvblock, 0)

        def fin(i, _):
            rows = pl.ds(pl.multiple_of(i * RB, RB), RB)
            dq_ref[rows, :] = dq_scr[rows, :].astype(BF16)
            return 0

        lax.fori_loop(0, nq, fin, 0)
        if n:
            pl.when(pl.program_id(0) == steps - 1)(s_finish)

    two = pl.BlockSpec((lp, 2 * LANE), lambda h: (0, h))
    one = pl.BlockSpec((lp, LANE), lambda h: (0, h))
    anyspec = pl.BlockSpec(memory_space=pl.ANY)
    return pl.pallas_call(
        body, grid=(steps,), in_specs=[two, two, one, one, one, one] + [anyspec] * (2 * n),
        out_specs=[two, two, one] + [anyspec] * n,
        out_shape=[jax.ShapeDtypeStruct((lp, 1024), BF16), jax.ShapeDtypeStruct((lp, 1024), BF16),
                   jax.ShapeDtypeStruct((lp, 512), BF16)] + [jax.ShapeDtypeStruct(b.shape, b.dtype) for b in bufs],
        input_output_aliases={6 + n + a: 3 + a for a in range(n)},
        scratch_shapes=[pltpu.VMEM((2, lp, LANE), BF16), pltpu.VMEM((2, lp, LANE), BF16),
                        pltpu.VMEM((lp, LANE), F32), pltpu.VMEM((lp, 2 * LANE), F32)]
        + (_comm_sems(n) if n else []),
        name="attn_bwd_scatter" if n else "attn_bwd", compiler_params=_cp())(q, k, v, o, do, lse, *xs, *bufs)


def _pool_lane_windows():
    lane = lax.broadcasted_iota(jnp.int32, (1, 256), 1)
    return jnp.where(lane < 64, 2, jnp.where(lane < 128, 4, jnp.where(lane < 192, 8, 16)))


def _by_window(wl, s2, s4, s8, s16):
    return jnp.where(wl == 2, s2, jnp.where(wl == 4, s4, jnp.where(wl == 8, s8, s16)))


def _pool_fwd_rows(pv_ext, t0):
    n = pv_ext.shape[0]
    wl = _pool_lane_windows()
    s2 = pv_ext + _dn(pv_ext, 1)
    s4 = s2 + _dn(s2, 2)
    s8 = s4 + _dn(s4, 4)
    s16 = s8 + _dn(s8, 8)
    t = t0 + lax.broadcasted_iota(jnp.int32, (n, 1), 0)
    cnt = jnp.maximum(jnp.minimum(t + 1, wl), 1).astype(F32)
    return _by_window(wl, s2, s4, s8, s16) / cnt - pv_ext


def _conv_dn(x_ext, w_ref, taps):
    acc = w_ref[taps - 1:taps, :] * x_ext
    for j in range(1, taps):
        acc = acc + w_ref[taps - 1 - j:taps - j, :] * _dn(x_ext, j)
    return acc


def _conv_up(g_ext, w_ref, taps):
    acc = w_ref[taps - 1:taps, :] * g_ext
    for j in range(1, taps):
        acc = acc + w_ref[taps - 1 - j:taps - j, :] * _up(g_ext, j)
    return acc


def _ln_fwd(c, g, b):
    mu = jnp.mean(c, axis=-1, keepdims=True)
    xc = c - mu
    r = lax.rsqrt(jnp.mean(xc * xc, axis=-1, keepdims=True) + EPS)
    xh = xc * r
    return xh * g + b, xh, r


def _halo_specs(lp, width, col):
    per = RB // HB
    last = lp // HB - 1
    cur = pl.BlockSpec((RB, width), lambda i: (i, col))
    prev = pl.BlockSpec((HB, width), lambda i: (jnp.maximum(i * per - 1, 0), col))
    nxt = pl.BlockSpec((HB, width), lambda i: (jnp.minimum((i + 1) * per, last), col))
    return cur, prev, nxt


def _mix_fwd(z, oat, bd, pscale, cw, cb, lng, lnb, sw, lp):
    def body(za, zah, mg, oat_ref, cu, cuh, cg, sbb, sbc, sbch, sbx, sbxh, sg,
             bd_ref, ps_ref, cw_ref, cb_ref, lng_ref, lnb_ref, sw_ref, u_ref):
        i = pl.program_id(0)
        pm = jnp.where(i > 0, 1.0, 0.0).astype(F32)
        pv = jnp.concatenate([zah[:, 0:256] * pm, za[:, 0:256]], axis=0)
        p = _pool_fwd_rows(pv, i * RB - HB)[HB:]
        y = jnp.dot(p.astype(BF16), bd_ref[...], preferred_element_type=F32)
        u_ref[:, 0:256] = (y * ps_ref[...] * _silu(za[:, 256:512])).astype(BF16)
        u_ref[:, 256:768] = (oat_ref[...] * _silu(mg[...])).astype(BF16)
        ce = jnp.concatenate([cuh[...] * pm, cu[...]], axis=0)
        glu = ce[:, 0:256] * _sig(ce[:, 256:512])
        c = _conv_dn(glu, cw_ref, CONF_K)[HB:] + cb_ref[...]
        n, _, _ = _ln_fwd(c, lng_ref[...], lnb_ref[...])
        u_ref[:, 768:1024] = (_silu(n) * _silu(cg[...])).astype(BF16)
        qe = jnp.concatenate([sbch[...] * sbxh[...] * pm, sbc[...] * sbx[...]], axis=0)
        cv = _conv_dn(qe, sw_ref, SC_K)[HB:]
        u_ref[:, 1024:1280] = (sbb[...] * cv * _silu(sg[...])).astype(BF16)

    a_cur, a_prev, _ = _halo_specs(lp, 512, PV0 // 512)
    cu_cur, cu_prev, _ = _halo_specs(lp, 512, CU0 // 512)
    sc_cur, sc_prev, _ = _halo_specs(lp, 256, SBC0 // 256)
    sx_cur, sx_prev, _ = _halo_specs(lp, 256, SBX0 // 256)
    c256 = lambda c0: pl.BlockSpec((RB, 256), lambda i: (i, c0 // 256))
    full = lambda r, c: pl.BlockSpec((r, c), lambda i: (0, 0))
    return pl.pallas_call(
        body, grid=(lp // RB,),
        in_specs=[a_cur, a_prev, pl.BlockSpec((RB, 512), lambda i: (i, MG0 // 512)),
                  pl.BlockSpec((RB, 512), lambda i: (i, 0)),
                  cu_cur, cu_prev, c256(CG0), c256(SBB0), sc_cur, sc_prev, sx_cur, sx_prev, c256(SG0),
                  full(256, 256), full(1, 256), full(32, 256), full(1, 256), full(1, 256), full(1, 256),
                  full(8, 256)],
        out_specs=pl.BlockSpec((RB, 1280), lambda i: (i, 0)),
        out_shape=jax.ShapeDtypeStruct((lp, 1280), BF16),
        name="mix_fwd", compiler_params=_cp())(z, z, z, oat, z, z, z, z, z, z, z, z, z,
                                               bd, pscale, cw, cb, lng, lnb, sw)


def _mix_bwd(z, oat, du, bd, pscale, cw, cb, lng, lnb, sw, lp):
    nb = lp // RB
    ne = RB + 2 * HB
    nf = RB + HB

    def body(za, zah, zan, mg, oat_ref, cu, cuh, cun, cg, cgn, sbb, sbbn, sbc, sbch, sbcn, sbx, sbxh, sbxn,
             sg, sgn, du_ref, dun_ref, bd_ref, ps_ref, cw_ref, cb_ref, lng_ref, lnb_ref, sw_ref,
             dzx_ref, doat_ref, dbd_ref, dcw_ref, dsw_ref, dsm_ref):
        xa, xm, xc = 0, MG0 - PV0, CU0 - PV0
        i = pl.program_id(0)
        pm = jnp.where(i > 0, 1.0, 0.0).astype(F32)
        nm = jnp.where(i < nb - 1, 1.0, 0.0).astype(F32)

        def ext(cur, prev, nxt, sl=slice(None)):
            return jnp.concatenate([prev[:, sl] * pm, cur[:, sl], nxt[:, sl] * nm], axis=0)

        def fwd(cur, nxt, sl=slice(None)):
            return jnp.concatenate([cur[:, sl], nxt[:, sl] * nm], axis=0)

        def csum(x):
            return jnp.sum(x, axis=0, keepdims=True)

        @pl.when(i == 0)
        def _():
            dbd_ref[...] = jnp.zeros((256, 256), F32)
            dcw_ref[...] = jnp.zeros((32, 256), F32)
            dsw_ref[...] = jnp.zeros((8, 256), F32)
            dsm_ref[...] = jnp.zeros((8, 256), F32)

        a_cols, b_cols = slice(0, 256), slice(256, 512)
        pv_e = ext(za, zah, zan, a_cols)
        p = _pool_fwd_rows(pv_e, i * RB - HB)[HB:HB + RB]
        pb = p.astype(BF16)
        y = jnp.dot(pb, bd_ref[...], preferred_element_type=F32)
        pg_f = fwd(za, zan, b_cols)
        dua_f = fwd(du_ref, dun_ref, slice(0, 256))
        dyp_f = dua_f * ps_ref[...] * _silu(pg_f)
        dypb = dyp_f.astype(BF16)
        dp_f = lax.dot_general(dypb, bd_ref[...], (((1,), (1,)), ((), ())), preferred_element_type=F32)
        wl = _pool_lane_windows()
        t = i * RB + lax.broadcasted_iota(jnp.int32, (nf, 1), 0)
        cnt = jnp.minimum(t + 1, wl).astype(F32)
        qf = dp_f / cnt
        f2 = qf + _up(qf, 1)
        f4 = f2 + _up(f2, 2)
        f8 = f4 + _up(f4, 4)
        f16 = f8 + _up(f8, 8)
        dpv = (_by_window(wl, f2, f4, f8, f16) - dp_f)[0:RB]
        dua = du_ref[:, 0:256]
        pg = za[:, b_cols]
        dpg = dua * y * ps_ref[...] * _dsilu(pg)
        dzx_ref[:, xa:xa + 256] = dpv.astype(BF16)
        dzx_ref[:, xa + 256:xa + 512] = dpg.astype(BF16)
        d_scale = csum(dua * y * _silu(pg))
        d_bd = lax.dot_general(pb, dypb[0:RB], (((0,), (0,)), ((), ())), preferred_element_type=F32)

        dub = du_ref[:, 256:768]
        mgv = mg[...]
        dzx_ref[:, xm:xm + 512] = (dub * oat_ref[...] * _dsilu(mgv)).astype(BF16)
        doat_ref[...] = dub * _silu(mgv)

        a_e = ext(cu, cuh, cun, slice(0, 256))
        gt_e = ext(cu, cuh, cun, slice(256, 512))
        sg_e = _sig(gt_e)
        glu_e = a_e * sg_e
        c_f = _conv_dn(glu_e, cw_ref, CONF_K)[HB:] + cb_ref[...]
        n_f, xh_f, r_f = _ln_fwd(c_f, lng_ref[...], lnb_ref[...])
        cg_f = fwd(cg, cgn)
        duc_f = fwd(du_ref, dun_ref, slice(768, 1024))
        sw_f = _silu(n_f)
        dcg = (duc_f * sw_f * _dsilu(cg_f))[0:RB]
        dn_f = duc_f * _silu(cg_f) * _dsilu(n_f)
        a_f = dn_f * lng_ref[...]
        dc_f = r_f * (a_f - jnp.mean(a_f, axis=-1, keepdims=True)
                      - xh_f * jnp.mean(a_f * xh_f, axis=-1, keepdims=True))
        d_lng = csum((dn_f * xh_f)[0:RB])
        d_lnb = csum(dn_f[0:RB])
        d_cb = csum(dc_f[0:RB])
        dglu = _conv_up(dc_f, cw_ref, CONF_K)[0:RB]
        dc_c = dc_f[0:RB]
        for kk in range(CONF_K):
            j = CONF_K - 1 - kk
            dcw_ref[kk:kk + 1, :] += csum(dc_c * _dn(glu_e, j)[HB:HB + RB])

        sgc = sg_e[HB:HB + RB]
        a_c = a_e[HB:HB + RB]
        dzx_ref[:, xc:xc + 256] = (dglu * sgc).astype(BF16)
        dzx_ref[:, xc + 256:xc + 512] = (dglu * a_c * sgc * (1.0 - sgc)).astype(BF16)
        dzx_ref[:, xc + 512:xc + 768] = dcg.astype(BF16)

        c_e = ext(sbc, sbch, sbcn)
        x_e = ext(sbx, sbxh, sbxn)
        q_e = c_e * x_e
        cv_f = _conv_dn(q_e, sw_ref, SC_K)[HB:]
        bg_f = fwd(sbb, sbbn)
        sg_f = fwd(sg, sgn)
        dud_f = fwd(du_ref, dun_ref, slice(1024, 1280))
        ssg_f = _silu(sg_f)
        dcv_f = dud_f * bg_f * ssg_f
        dbg = (dud_f * cv_f * ssg_f)[0:RB]
        dsg = (dud_f * bg_f * cv_f * _dsilu(sg_f))[0:RB]
        dq = _conv_up(dcv_f, sw_ref, SC_K)[0:RB]
        dcv_c = dcv_f[0:RB]
        for kk in range(SC_K):
            j = SC_K - 1 - kk
            dsw_ref[kk:kk + 1, :] += csum(dcv_c * _dn(q_e, j)[HB:HB + RB])

        dzx_ref[:, xc + 768:xc + 1024] = dbg.astype(BF16)
        dzx_ref[:, xc + 1024:xc + 1280] = (dq * x_e[HB:HB + RB]).astype(BF16)
        dzx_ref[:, xc + 1280:xc + 1536] = (dq * c_e[HB:HB + RB]).astype(BF16)
        dzx_ref[:, xc + 1536:xc + 1792] = dsg.astype(BF16)

        dbd_ref[...] += d_bd
        dsm_ref[0:1, :] += d_scale
        dsm_ref[1:2, :] += d_cb
        dsm_ref[2:3, :] += d_lng
        dsm_ref[3:4, :] += d_lnb

    a3 = _halo_specs(lp, 512, PV0 // 512)
    cu3 = _halo_specs(lp, 512, CU0 // 512)
    cg3 = _halo_specs(lp, 256, CG0 // 256)
    sbb3 = _halo_specs(lp, 256, SBB0 // 256)
    sbc3 = _halo_specs(lp, 256, SBC0 // 256)
    sbx3 = _halo_specs(lp, 256, SBX0 // 256)
    sg3 = _halo_specs(lp, 256, SG0 // 256)
    du3 = _halo_specs(lp, 1280, 0)
    full = lambda r, c: pl.BlockSpec((r, c), lambda i: (0, 0))
    in_specs = [a3[0], a3[1], a3[2], pl.BlockSpec((RB, 512), lambda i: (i, MG0 // 512)),
                pl.BlockSpec((RB, 512), lambda i: (i, 0)),
                cu3[0], cu3[1], cu3[2], cg3[0], cg3[2], sbb3[0], sbb3[2],
                sbc3[0], sbc3[1], sbc3[2], sbx3[0], sbx3[1], sbx3[2], sg3[0], sg3[2],
                du3[0], du3[2],
                full(256, 256), full(1, 256), full(32, 256), full(1, 256), full(1, 256), full(1, 256),
                full(8, 256)]
    out_specs = [pl.BlockSpec((RB, ZW - PV0), lambda i: (i, 0)), pl.BlockSpec((RB, 512), lambda i: (i, 0)),
                 full(256, 256), full(32, 256), full(8, 256), full(8, 256)]
    out_shape = [jax.ShapeDtypeStruct((lp, ZW - PV0), BF16), jax.ShapeDtypeStruct((lp, 512), F32),
                 jax.ShapeDtypeStruct((256, 256), F32), jax.ShapeDtypeStruct((32, 256), F32),
                 jax.ShapeDtypeStruct((8, 256), F32), jax.ShapeDtypeStruct((8, 256), F32)]
    return pl.pallas_call(
        body, grid=(nb,), in_specs=in_specs, out_specs=out_specs, out_shape=out_shape,
        name="mix_bwd", compiler_params=_cp())(
            z, z, z, z, oat, z, z, z, z, z, z, z, z, z, z, z, z, z, z, z, du, du,
            bd, pscale, cw, cb, lng, lnb, sw)


U_OFF = (0, 256, 768, 1024, 1280)
MRB = 192


def _merge_fwd(x, u, z, gb, wout, wo, gpost, lp):
    def body(x_ref, u_ref, gl_ref, gb_ref, wout_ref, wo_ref, g_ref, xo_ref, m_ref, o2_ref):
        m = jnp.zeros((MRB, D), F32)
        for b in range(4):
            y = jnp.dot(u_ref[:, U_OFF[b]:U_OFF[b + 1]], wout_ref[U_OFF[b]:U_OFF[b + 1], :],
                        preferred_element_type=F32)
            sl = slice(D * b, D * (b + 1))
            m = m + _sig(gl_ref[:, sl] + gb_ref[:, sl]) * y
        mb = m.astype(BF16)
        m_ref[...] = mb
        o2 = jnp.dot(mb, wo_ref[...], preferred_element_type=F32)
        o2_ref[...] = o2
        r = lax.rsqrt(jnp.mean(o2 * o2, axis=-1, keepdims=True) + EPS)
        xo_ref[...] = x_ref[...] + o2 * r * g_ref[...]

    blk = pl.BlockSpec((MRB, D), lambda i: (i, 0))
    full = lambda r, c: pl.BlockSpec((r, c), lambda i: (0, 0))
    return pl.pallas_call(
        body, grid=(lp // MRB,),
        in_specs=[blk, pl.BlockSpec((MRB, 1280), lambda i: (i, 0)), pl.BlockSpec((MRB, 4096), lambda i: (i, 0)),
                  full(1, 4096), full(1280, D), full(D, D), full(1, D)],
        out_specs=[blk, blk, blk],
        out_shape=[jax.ShapeDtypeStruct((lp, D), F32), jax.ShapeDtypeStruct((lp, D), BF16),
                   jax.ShapeDtypeStruct((lp, D), F32)],
        name="merge_fwd", compiler_params=_cp())(x, u, z, gb, wout, wo, gpost)


def _merge_bwd(dx, o2, u, z, gb, wout, wo, gpost, lp):
    def body(dx_ref, o2_ref, u_ref, gl_ref, gb_ref, wout_ref, wo_ref, g_ref,
             do2_ref, dgl_ref, dy_ref, du_ref, dgb_ref, dg_ref):
        i = pl.program_id(0)
        o2 = o2_ref[...]
        dy = dx_ref[...]
        r = lax.rsqrt(jnp.mean(o2 * o2, axis=-1, keepdims=True) + EPS)
        a = dy * g_ref[...]
        do2 = (r * a - o2 * (r * r * r) * jnp.mean(a * o2, axis=-1, keepdims=True)).astype(BF16)
        do2_ref[...] = do2
        dg = jnp.sum(dy * o2 * r, axis=0, keepdims=True)
        dm = lax.dot_general(do2, wo_ref[...], (((1,), (1,)), ((), ())), preferred_element_type=F32)
        for b in range(4):
            rows = slice(U_OFF[b], U_OFF[b + 1])
            y = jnp.dot(u_ref[:, rows], wout_ref[rows, :], preferred_element_type=F32)
            sl = slice(D * b, D * (b + 1))
            gt = _sig(gl_ref[:, sl] + gb_ref[:, sl])
            dgl = dm * y * gt * (1.0 - gt)
            dgl_ref[:, sl] = dgl.astype(BF16)
            part = jnp.sum(dgl, axis=0, keepdims=True)

            @pl.when(i == 0)
            def _(part=part, sl=sl):
                dgb_ref[:, sl] = part

            @pl.when(i > 0)
            def _(part=part, sl=sl):
                dgb_ref[:, sl] += part

            dyb = (dm * gt).astype(BF16)
            dy_ref[:, sl] = dyb
            du_ref[:, rows] = lax.dot_general(dyb, wout_ref[rows, :], (((1,), (1,)), ((), ())),
                                              preferred_element_type=F32)

        @pl.when(i == 0)
        def _():
            dg_ref[...] = dg

        @pl.when(i > 0)
        def _():
            dg_ref[...] += dg

    blk = pl.BlockSpec((MRB, D), lambda i: (i, 0))
    wide = pl.BlockSpec((MRB, 4096), lambda i: (i, 0))
    ub = pl.BlockSpec((MRB, 1280), lambda i: (i, 0))
    full = lambda r, c: pl.BlockSpec((r, c), lambda i: (0, 0))
    return pl.pallas_call(
        body, grid=(lp // MRB,),
        in_specs=[blk, blk, ub, wide, full(1, 4096), full(1280, D), full(D, D), full(1, D)],
        out_specs=[blk, wide, wide, ub, full(1, 4096), full(1, D)],
        out_shape=[jax.ShapeDtypeStruct((lp, D), BF16), jax.ShapeDtypeStruct((lp, 4096), BF16),
                   jax.ShapeDtypeStruct((lp, 4096), BF16), jax.ShapeDtypeStruct((lp, 1280), F32),
                   jax.ShapeDtypeStruct((1, 4096), F32), jax.ShapeDtypeStruct((1, D), F32)],
        name="merge_bwd", compiler_params=_cp())(dx, o2, u, z, gb, wout, wo, gpost)


def _loss_head(xf, tgt, n_real, lp):
    def body(x_ref, t_ref, dy_ref, ls_ref):
        i = pl.program_id(0)
        t = i * RB + lax.broadcasted_iota(jnp.int32, (RB, 1), 0)
        real = (t >= N_META) & (t < n_real)
        err = jnp.where(real, x_ref[...] - t_ref[...], 0.0)
        dy_ref[...] = err / D
        part = 0.5 * jnp.sum(jnp.mean(err * err, axis=-1, keepdims=True), axis=0, keepdims=True)
        part = jnp.broadcast_to(part, (8, LANE))

        @pl.when(i == 0)
        def _():
            ls_ref[...] = part

        @pl.when(i > 0)
        def _():
            ls_ref[...] += part

    blk = pl.BlockSpec((RB, D), lambda i: (i, 0))
    return pl.pallas_call(
        body, grid=(lp // RB,), in_specs=[blk, blk],
        out_specs=[blk, pl.BlockSpec((8, LANE), lambda i: (0, 0))],
        out_shape=[jax.ShapeDtypeStruct((lp, D), F32), jax.ShapeDtypeStruct((8, LANE), F32)],
        name="loss_head", compiler_params=_cp())(xf, tgt)


def _peer(d):
    x, y, c = lax.axis_index("x"), lax.axis_index("y"), lax.axis_index("c")
    return (x ^ ((d >> 2) & 1), y ^ ((d >> 1) & 1), c ^ (d & 1))


def _index_of(p):
    return 4 * p[0] + 2 * p[1] + p[2]


def _all_gather(xs, name):
    n = len(xs)

    def body(*refs):
        start, forward, finish = _gather_phases(refs[:n], refs[n:2 * n], *refs[2 * n:])
        start()
        forward()
        finish()

    anyspec = pl.BlockSpec(memory_space=pl.ANY)
    return pl.pallas_call(
        body, in_specs=[anyspec] * n, out_specs=[anyspec] * n,
        out_shape=_gather_shapes(xs), scratch_shapes=_comm_sems(n), name=name)(*xs)


def _gather_shapes(xs):
    return [jax.ShapeDtypeStruct((N_DEV,) + x.shape, x.dtype) for x in xs]


def _comm_sems(n):
    return [pltpu.SemaphoreType.DMA((7 * n,)), pltpu.SemaphoreType.DMA((7 * n,)), pltpu.SemaphoreType.DMA((n,))]


def _gather_phases(x_refs, out_refs, send_sems, recv_sems, local_sems):
    n = len(x_refs)
    chips = [2, 4, 6]

    def copy(a, kk, block, to, src=None):
        slot = out_refs[a].at[_index_of(block)]
        return pltpu.make_async_remote_copy(
            src_ref=slot if src is None else src, dst_ref=slot,
            send_sem=send_sems.at[7 * a + kk], recv_sem=recv_sems.at[7 * a + kk], device_id=to,
            device_id_type=MESH)

    def local(a):
        return pltpu.make_async_copy(x_refs[a], out_refs[a].at[_index_of(_peer(0))], local_sems.at[a])

    def firsts():
        out = []
        for a in range(n):
            out.append(copy(a, 0, _peer(0), _peer(1), src=x_refs[a]))
            out += [copy(a, 1 + j, _peer(0), _peer(d), src=x_refs[a]) for j, d in enumerate(chips)]
        return out

    def passes():
        return [copy(a, 4 + j, _peer(d), _peer(1)) for j, d in enumerate(chips) for a in range(n)]

    def start():
        for a in range(n):
            local(a).start()
        for cp in firsts():
            cp.start()

    def forward():
        for j, d in enumerate(chips):
            for a in range(n):
                copy(a, 1 + j, _peer(d), _peer(0)).wait_recv()
                copy(a, 4 + j, _peer(d), _peer(1)).start()

    def finish():
        for a in range(n):
            copy(a, 0, _peer(1), _peer(0)).wait_recv()
            for j, d in enumerate(chips):
                copy(a, 4 + j, _peer(d | 1), _peer(0)).wait_recv()
        for cp in firsts() + passes():
            cp.wait_send()
        for a in range(n):
            local(a).wait()

    return start, forward, finish


def _scatter_phases(x_refs, out_refs, send_sems, recv_sems, local_sems, layer):
    n = len(x_refs)

    def land(a, dev):
        slot = out_refs[a].at[dev]
        return slot if layer is None else slot.at[layer]

    def local(a):
        my = _index_of(_peer(0))
        return pltpu.make_async_copy(x_refs[a].at[my], land(a, my), local_sems.at[a])

    def copy(a, d):
        my = _index_of(_peer(0))
        return pltpu.make_async_remote_copy(
            src_ref=x_refs[a].at[_index_of(_peer(d))], dst_ref=land(a, my),
            send_sem=send_sems.at[7 * a + d - 1], recv_sem=recv_sems.at[7 * a + d - 1], device_id=_peer(d),
            device_id_type=MESH)

    def arrival(a, d):
        frm = _index_of(_peer(d))
        return pltpu.make_async_remote_copy(
            src_ref=x_refs[a].at[frm], dst_ref=land(a, frm),
            send_sem=send_sems.at[7 * a + d - 1], recv_sem=recv_sems.at[7 * a + d - 1], device_id=_peer(d),
            device_id_type=MESH)

    def start():
        for a in range(n):
            local(a).start()
        for d in range(1, N_DEV):
            for a in range(n):
                copy(a, d).start()

    def finish():
        for d in range(1, N_DEV):
            for a in range(n):
                arrival(a, d).wait_recv()
        for d in range(1, N_DEV):
            for a in range(n):
                copy(a, d).wait_send()
        for a in range(n):
            local(a).wait()

    return start, finish


def _all_to_all(xs, bufs, layer, name):
    n = len(xs)

    def body(*refs):
        start, finish = _scatter_phases(refs[:n], refs[2 * n:3 * n], *refs[3 * n:], layer)
        start()
        finish()

    anyspec = pl.BlockSpec(memory_space=pl.ANY)
    return pl.pallas_call(
        body, in_specs=[anyspec] * (2 * n), out_specs=[anyspec] * n,
        out_shape=[jax.ShapeDtypeStruct(b.shape, b.dtype) for b in bufs],
        input_output_aliases={n + a: a for a in range(n)},
        scratch_shapes=_comm_sems(n), name=name)(*xs, *bufs)


def _adam_math(g, w, m, v):
    c1 = 1.0 - ADAM_B1 ** ADAM_STEP
    c2 = 1.0 - ADAM_B2 ** ADAM_STEP
    mn = ADAM_B1 * m + (1.0 - ADAM_B1) * g
    vn = ADAM_B2 * v + (1.0 - ADAM_B2) * (g * g)
    return -ADAM_LR * ((mn / c1) / (jnp.sqrt(vn / c2) + ADAM_EPS) + ADAM_WD * w), mn, vn


def _reduce_adam(parts, w, m, v, rb, name, row_off=0):
    depth, rows, cols = w.shape
    assert rows % rb == 0 and row_off % rb == 0

    def body(p_ref, w_ref, m_ref, v_ref, g_ref, d_ref, mo_ref, vo_ref):
        g = p_ref[0, 0].astype(F32)
        for j in range(1, N_DEV):
            g = g + p_ref[j, 0].astype(F32)
        g_ref[0] = g
        d_ref[0], mo_ref[0], vo_ref[0] = _adam_math(g, w_ref[0], m_ref[0], v_ref[0])

    blk = pl.BlockSpec((1, rb, cols), lambda l, i: (l, i, 0))
    out = jax.ShapeDtypeStruct(w.shape, F32)
    return pl.pallas_call(
        body, grid=(depth, rows // rb),
        in_specs=[pl.BlockSpec((N_DEV, 1, rb, cols), lambda l, i: (0, l, i + row_off // rb, 0)), blk, blk, blk],
        out_specs=[blk, blk, blk, blk], out_shape=[out, out, out, out],
        name=name, compiler_params=_cp())(parts, w, m, v)


def _reduce_adam_flat(parts, w, m, v, name):
    q_rows = w.shape[0]

    def body(p_ref, w_ref, m_ref, v_ref, g_ref, d_ref, mo_ref, vo_ref):
        g = p_ref[0].astype(F32)
        for j in range(1, N_DEV):
            g = g + p_ref[j].astype(F32)
        g_ref[...] = g
        d_ref[...], mo_ref[...], vo_ref[...] = _adam_math(g, w_ref[...], m_ref[...], v_ref[...])

    blk = pl.BlockSpec((q_rows, LANE), lambda i: (0, 0))
    out = jax.ShapeDtypeStruct((q_rows, LANE), F32)
    return pl.pallas_call(
        body, grid=(1,), in_specs=[pl.BlockSpec((N_DEV, q_rows, LANE), lambda i: (0, 0, 0)), blk, blk, blk],
        out_specs=[blk, blk, blk, blk], out_shape=[out, out, out, out],
        name=name, compiler_params=_cp())(parts, w, m, v)


C128 = (("w_out_pool", 256), ("w_out_mla", 512), ("w_out_conf", 256), ("w_out_sc", 256), ("w_ukv", 128))
C128_ROWS = sum(r for _, r in C128)
TAIL = (("meta_tokens", (N_META, 128)), ("conf_dw_w", (DEPTH, CONF_K, 32)), ("sc_dw_w", (DEPTH, SC_K, 32)))
TAIL_ROWS = sum(int(np.prod(s)) for _, s in TAIL) // LANE
TAIL_PAD = 56
SMALL = (("pre_norm_g", (DEPTH, D)), ("gate_bias", (DEPTH, 4096)), ("pool_w", (DEPTH, 4, 64, 64)),
         ("pool_scale", (DEPTH, 256)), ("q_norm_g", (DEPTH, 256)), ("kv_norm_g", (DEPTH, 128)),
         ("conf_dw_b", (DEPTH, 256)), ("conf_ln_g", (DEPTH, 256)), ("conf_ln_b", (DEPTH, 256)),
         ("post_norm_g", (DEPTH, D)))
SMALL_ROWS = sum(int(np.prod(s)) for _, s in SMALL) // LANE
SMALL_PAD = -(-(SMALL_ROWS + 1) // 8) * 8


def _pack_tail(t):
    parts = [t[n].reshape(-1, LANE) for n, _ in TAIL]
    parts.append(jnp.zeros((TAIL_PAD - TAIL_ROWS, LANE), F32))
    return jnp.concatenate(parts, axis=0)


def _unpack_tail(flat):
    out, off = {}, 0
    for n, s in TAIL:
        rows = int(np.prod(s)) // LANE
        out[n] = flat[off:off + rows].reshape(s)
        off += rows
    return out


def _unpack_tail_full(g):
    out, off = {}, 0
    for n, s in TAIL:
        rows = int(np.prod(s)) // LANE
        blk = jnp.moveaxis(g[:, off:off + rows].reshape((N_DEV,) + s), 0, -2)
        out[n] = blk.reshape(s[:-1] + (N_DEV * s[-1],))
        off += rows
    return out


def _pack_small(t, extra_row):
    parts = [t[n].reshape(-1, LANE) for n, _ in SMALL] + [extra_row]
    parts.append(jnp.zeros((SMALL_PAD - SMALL_ROWS - 1, LANE), F32))
    return jnp.concatenate(parts, axis=0)


def _unpack_small(flat):
    out, off = {}, 0
    for n, s in SMALL:
        rows = int(np.prod(s)) // LANE
        out[n] = flat[off:off + rows].reshape(s)
        off += rows
    return out


def _cols_by_dest(g, width):
    r = g.shape[0]
    return g.reshape(r, N_DEV, width).transpose(1, 0, 2)


def _cols_full(gathered):
    _, r, c = gathered.shape
    return gathered.transpose(1, 0, 2).reshape(r, N_DEV * c)


W_IN_SHARD = 916
PACKED_SEGS = ((3232, 7328), (512, 896), 64, (896, 928), 32, (0, 512), (928, 3232))


def _pack_w_in(g):
    parts = []
    for seg in PACKED_SEGS:
        if isinstance(seg, int):
            parts.append(jnp.zeros((g.shape[1], seg), g.dtype))
            continue
        a, b = seg
        while a < b:
            k = a // W_IN_SHARD
            hi = min(b, W_IN_SHARD * (k + 1))
            parts.append(g[k, :, a - W_IN_SHARD * k:hi - W_IN_SHARD * k])
            a = hi
    return jnp.concatenate(parts, axis=1)


def _w_in_grad_by_dest(gl, mla, mix):
    src = (((0, 512), mix, 0), ((512, 896), mla, 0), ((896, 928), mla, 448), ((928, 3232), mix, 512),
           ((3232, 7328), gl, 0))
    blocks = []
    for k in range(N_DEV):
        lo, hi = W_IN_SHARD * k, W_IN_SHARD * (k + 1)
        parts = []
        for (a, b), arr, off in src:
            s, e = max(a, lo), min(b, hi)
            if s < e:
                parts.append(arr[:, off + s - a:off + e - a])
        blocks.append(jnp.concatenate(parts, axis=1))
    return jnp.stack(blocks)


def _rope_tables(lp):
    inv = 1.0 / (ROPE_THETA ** (jnp.arange(0, QK_ROPE, 2, dtype=F32) / QK_ROPE))
    ang = jnp.arange(lp, dtype=F32)[:, None] * inv[None, :]
    cos, sin = jnp.cos(ang), jnp.sin(ang)
    one = jnp.ones((lp, QK_NOPE), F32)
    zero = jnp.zeros((lp, QK_NOPE), F32)
    z16 = jnp.zeros((lp, 16), F32)
    c = jnp.concatenate([one, cos, cos, jnp.ones((lp, 32), F32)], axis=1)
    s1 = jnp.concatenate([zero, z16, sin, jnp.zeros((lp, 32), F32)], axis=1)
    s2 = jnp.concatenate([zero, -sin, z16, jnp.zeros((lp, 32), F32)], axis=1)
    return c, s1, s2


def kernel(x, meta_tokens, pre_norm_g, w_in, gate_bias, pool_w, pool_scale, w_out_pool, q_norm_g, w_uq, kv_norm_g, w_ukv, w_out_mla, conf_dw_w, conf_dw_b, conf_ln_g, conf_ln_b, w_out_conf, sc_dw_w, w_out_sc, w_o, post_norm_g, loss_target, m_meta_tokens, m_pre_norm_g, m_w_in, m_gate_bias, m_pool_w, m_pool_scale, m_w_out_pool, m_q_norm_g, m_w_uq, m_kv_norm_g, m_w_ukv, m_w_out_mla, m_conf_dw_w, m_conf_dw_b, m_conf_ln_g, m_conf_ln_b, m_w_out_conf, m_sc_dw_w, m_w_out_sc, m_w_o, m_post_norm_g, v_meta_tokens, v_pre_norm_g, v_w_in, v_gate_bias, v_pool_w, v_pool_scale, v_w_out_pool, v_q_norm_g, v_w_uq, v_kv_norm_g, v_w_ukv, v_w_out_mla, v_conf_dw_w, v_conf_dw_b, v_conf_ln_g, v_conf_ln_b, v_w_out_conf, v_sc_dw_w, v_w_out_sc, v_w_o, v_post_norm_g):
    names = ["w_in", "w_uq", "w_o"] + [n for n, _ in C128] + [n for n, _ in TAIL] + [n for n, _ in SMALL]
    loc = locals()
    w = {n: loc[n] for n in names}
    mom = {n: loc["m_" + n] for n in names}
    vel = {n: loc["v_" + n] for n in names}

    seq = x.shape[1]
    n_real = N_META + seq
    lp = -(-n_real // RB) * RB
    tmb = lp // 3
    tabs = _rope_tables(lp)

    c128 = jnp.concatenate([w[n] for n, _ in C128], axis=1)
    def shards_of(i):
        return [w_in[i].astype(BF16), c128[i].astype(BF16), w_uq[i].astype(BF16), w_o[i].astype(BF16)]

    gathered = [_all_gather(shards_of(0), "gather_weights")] + [None] * (DEPTH - 1)
    tail_w = _pack_tail(w)
    tail = _unpack_tail_full(_all_gather([tail_w], "gather_tail")[0])
    eye4 = jnp.eye(4, dtype=F32)
    bd_all = (pool_w[:, :, :, None, :] * eye4[None, :, None, :, None]).reshape(DEPTH, 256, 256).astype(BF16)

    def layer_weights(i):
        g_in, g_c128, g_uq, g_o = gathered[i]
        lw = {}
        lw["w_in"] = _pack_w_in(g_in)
        lw["wc"] = _cols_full(g_c128)
        wuq = _cols_full(g_uq).reshape(Q_RANK, HEADS, 96)
        lw["w_uq"] = jnp.pad(wuq, ((0, 0), (0, 0), (0, 32))).reshape(Q_RANK, HEADS * LANE)
        wukv = lw["wc"][U_OFF[4]:].reshape(KV_RANK, HEADS, 128)
        wk = jnp.pad(wukv[:, :, :QK_NOPE], ((0, 0), (0, 0), (0, 64))).reshape(KV_RANK, HEADS * LANE)
        lw["w_ukv"] = jnp.concatenate([wk, wukv[:, :, QK_NOPE:].reshape(KV_RANK, HEADS * V_DIM)], axis=1)
        lw["w_o"] = g_o.reshape(D, D)
        lw["bd"] = bd_all[i]
        lw["cw"] = jnp.pad(tail["conf_dw_w"][i], ((0, 1), (0, 0)))
        lw["sw"] = jnp.pad(tail["sc_dw_w"][i], ((0, 8 - SC_K), (0, 0)))
        return lw

    meta_full = tail["meta_tokens"]

    pad_rows = lp - n_real
    xr = jnp.concatenate([meta_full, x[0], jnp.zeros((pad_rows, D), F32)], axis=0)
    tgt = jnp.pad(loss_target[0], ((N_META, pad_rows), (0, 0)))
    saved = []
    for i in range(DEPTH):
        lw = layer_weights(i)
        h = _rms_fwd(xr, pre_norm_g[i:i + 1], lp)
        z = _mm(h, lw["w_in"], lp, ZW, D, tm=tmb, tn=256, tk=D, name="mm_in")
        qn, kvn, krr = _mla_prep(z, q_norm_g[i:i + 1], kv_norm_g[i:i + 1], tabs, lp)
        q_raw = _mm(qn, lw["w_uq"], lp, 1024, Q_RANK, tm=tmb, tn=1024, tk=Q_RANK, name="mm_uq")
        kv_raw = _mm(kvn, lw["w_ukv"], lp, 1536, KV_RANK, tm=tmb, tn=512, tk=KV_RANK, name="mm_ukv")
        qt, kt, vt = _mla_post(q_raw, kv_raw, krr, tabs, lp)
        res = _attn_fwd(qt, kt, vt, lp, gather=shards_of(i + 1) if i + 1 < DEPTH else ())
        oat, lse = res[0], res[1]
        if i + 1 < DEPTH:
            gathered[i + 1] = res[2:]
        u = _mix_fwd(z, oat, lw["bd"], pool_scale[i:i + 1], lw["cw"], conf_dw_b[i:i + 1], conf_ln_g[i:i + 1],
                     conf_ln_b[i:i + 1], lw["sw"], lp)
        x_new, m_act, o2 = _merge_fwd(xr, u, z, gate_bias[i:i + 1], lw["wc"], lw["w_o"],
                                      post_norm_g[i:i + 1], lp)
        saved.append(dict(lw=lw, x=xr, h=h, z=z, qn=qn, kvn=kvn, qt=qt, kt=kt, vt=vt, oat=oat, lse=lse,
                          u=u, m=m_act, o2=o2))
        xr = x_new

    dx, loss_part = _loss_head(xr, tgt, n_real, lp)

    gsm = {n: [None] * DEPTH for n, _ in SMALL}
    g_cw = [None] * DEPTH
    g_sw = [None] * DEPTH
    recv = [lax.empty((N_DEV, DEPTH) + s, BF16) for s in ((D, 916), (C128_ROWS, 128), (Q_RANK, 96), (128, D))]
    pending = None
    for i in reversed(range(DEPTH)):
        s = saved[i]
        lw = s["lw"]
        do2, dgl, dyb, du, dgb, dgpost = _merge_bwd(dx, s["o2"], s["u"], s["z"], gate_bias[i:i + 1],
                                                    lw["wc"], lw["w_o"], post_norm_g[i:i + 1], lp)
        d_wo = _mm(s["m"], do2, D, D, lp, ta=True, tm=512, tn=D, tk=tmb, out_dtype=BF16, name="mm_dwo")
        d_wout = []
        for b in range(4):
            rows = U_OFF[b + 1] - U_OFF[b]
            d_wout.append(_mm(s["u"], dyb, rows, D, lp, ta=True, tm=256, tn=D, tk=tmb, out_dtype=BF16,
                              a_moff=U_OFF[b] // 256, b_noff=b, name="mm_dwout%d" % b))
        dzx, doat, dbd, dcw, dsw, dsm = _mix_bwd(
            s["z"], s["oat"], du, lw["bd"], pool_scale[i:i + 1], lw["cw"], conf_dw_b[i:i + 1],
            conf_ln_g[i:i + 1], conf_ln_b[i:i + 1], lw["sw"], lp)
        res = _attn_bwd(s["qt"], s["kt"], s["vt"], s["oat"], doat, s["lse"], lp,
                        scatter=(pending, recv, i + 1) if pending else None)
        dqt, dkt, dvt = res[:3]
        if pending:
            recv = list(res[3:])
        dq_raw, dkv_raw, dkr = _mla_post_bwd(dqt, dkt, dvt, tabs, lp)
        dqn = _mm(dq_raw, lw["w_uq"], lp, Q_RANK, 1024, tb=True, tm=tmb, tn=Q_RANK, tk=1024, name="mm_dqn")
        d_wuq = _mm(s["qn"], dq_raw, Q_RANK, 1024, lp, ta=True, tm=Q_RANK, tn=1024, tk=tmb, out_dtype=BF16,
                    name="mm_dwuq")
        dkvn = _mm(dkv_raw, lw["w_ukv"], lp, KV_RANK, 1536, tb=True, tm=tmb, tn=KV_RANK, tk=1536, name="mm_dkvn")
        d_wukv = _mm(s["kvn"], dkv_raw, KV_RANK, 1536, lp, ta=True, tm=KV_RANK, tn=1536, tk=tmb, out_dtype=BF16,
                     name="mm_dwukv")
        dzq, dqg, dkvg = _mla_prep_bwd(s["z"], q_norm_g[i:i + 1], kv_norm_g[i:i + 1], dqn, dkvn, dkr, lp)
        dh = _mm(dgl, lw["w_in"], lp, D, 4096, tb=True, tm=RB, tn=D, tk=2048, name="mm_dh0")
        dh = _mm(dzq, lw["w_in"], lp, D, 512, tb=True, tm=tmb, tn=D, tk=512, b_koff=CQ0 // 512, c=dh,
                 name="mm_dh1")
        dh = _mm(dzx, lw["w_in"], lp, D, ZW - PV0, tb=True, tm=tmb, tn=D, tk=256, b_koff=PV0 // 256, c=dh,
                 name="mm_dh2")
        d_win = [_mm(s["h"], seg, D, seg.shape[1], lp, ta=True, tm=D, tn=256, tk=tmb, out_dtype=BF16,
                     name="mm_dwin%d" % k) for k, seg in enumerate((dgl, dzq, dzx))]
        dx, dgpre = _rms_bwd(s["x"], pre_norm_g[i:i + 1], dh, dx, lp)

        gsm["pre_norm_g"][i] = dgpre[0]
        gsm["gate_bias"][i] = dgb[0]
        gsm["pool_w"][i] = jnp.stack([dbd[64 * g:64 * (g + 1), 64 * g:64 * (g + 1)] for g in range(4)])
        gsm["pool_scale"][i] = dsm[0]
        gsm["conf_dw_b"][i] = dsm[1]
        gsm["conf_ln_g"][i] = dsm[2]
        gsm["conf_ln_b"][i] = dsm[3]
        gsm["q_norm_g"][i] = dqg[0]
        gsm["kv_norm_g"][i] = dkvg[0]
        gsm["post_norm_g"][i] = dgpost[0]
        g_cw[i] = dcw[:CONF_K]
        g_sw[i] = dsw[:SC_K]
        d_wuq_o = d_wuq.reshape(Q_RANK, HEADS, LANE)[:, :, :96].reshape(Q_RANK, HEADS * 96)
        d_wukv_o = jnp.concatenate([d_wukv[:, :1024].reshape(KV_RANK, HEADS, LANE)[:, :, :QK_NOPE],
                                    d_wukv[:, 1024:].reshape(KV_RANK, HEADS, V_DIM)], axis=2).reshape(KV_RANK, 1024)
        pending = [
            _w_in_grad_by_dest(*d_win),
            _cols_by_dest(jnp.concatenate(d_wout + [d_wukv_o], axis=0), 128),
            _cols_by_dest(d_wuq_o, 96),
            d_wo.reshape(N_DEV, 128, D)]
    recv = _all_to_all(pending, recv, 0, "scatter_grads")

    outs = [dict() for _ in range(4)]

    def put(n, res):
        for t, r in zip(outs, res):
            t[n] = r

    put("w_in", _reduce_adam(recv[0], w["w_in"], mom["w_in"], vel["w_in"], 256, "adam_w_in"))
    off = 0
    for n, rows in C128:
        put(n, _reduce_adam(recv[1], w[n], mom[n], vel[n], 128, "adam_" + n, row_off=off))
        off += rows
    put("w_uq", _reduce_adam(recv[2], w["w_uq"], mom["w_uq"], vel["w_uq"], Q_RANK, "adam_w_uq"))
    put("w_o", _reduce_adam(recv[3], w["w_o"], mom["w_o"], vel["w_o"], 128, "adam_w_o"))

    tail_g = {"meta_tokens": _cols_by_dest(dx[:N_META], 128),
              "conf_dw_w": jnp.moveaxis(jnp.stack(g_cw).reshape(DEPTH, CONF_K, N_DEV, 32), 2, 0),
              "sc_dw_w": jnp.moveaxis(jnp.stack(g_sw).reshape(DEPTH, SC_K, N_DEV, 32), 2, 0)}
    tail_bd = jnp.concatenate([tail_g[n].reshape(N_DEV, -1, LANE) for n, _ in TAIL]
                              + [jnp.zeros((N_DEV, TAIL_PAD - TAIL_ROWS, LANE), F32)], axis=1)
    tail_recv = _all_to_all([tail_bd], [lax.empty((N_DEV, TAIL_PAD, LANE), F32)], None, "scatter_tail")[0]
    tail_res = _reduce_adam_flat(tail_recv, tail_w, _pack_tail(mom), _pack_tail(vel), "adam_tail")

    small_g = {n: jnp.stack(gsm[n]) for n, _ in SMALL}
    loss_row = jnp.concatenate([loss_part[0:1, 0:1], jnp.zeros((1, LANE - 1), F32)], axis=1)
    zrow = jnp.zeros((1, LANE), F32)
    parts = _all_gather([_pack_small(small_g, loss_row)], "gather_small_grads")[0]
    small_res = _reduce_adam_flat(parts, _pack_small(w, zrow), _pack_small(mom, zrow), _pack_small(vel, zrow),
                                  "adam_small")
    loss = small_res[0][SMALL_ROWS, 0]
    for t, tf, sf in zip(outs, tail_res, small_res):
        t.update(_unpack_tail(tf))
        t.update(_unpack_small(sf))
    order = ["meta_tokens", "pre_norm_g", "w_in", "gate_bias", "pool_w", "pool_scale", "w_out_pool", "q_norm_g",
             "w_uq", "kv_norm_g", "w_ukv", "w_out_mla", "conf_dw_w", "conf_dw_b", "conf_ln_g", "conf_ln_b",
             "w_out_conf", "sc_dw_w", "w_out_sc", "w_o", "post_norm_g"]
    grad_x = dx[N_META:n_real][None]
    return (loss, grad_x, *[t[n] for t in outs for n in order])
```

```python
import functools

import jax
import jax.numpy as jnp
import numpy as np
from jax import lax
from jax.experimental import pallas as pl
from jax.experimental.pallas import tpu as pltpu

F32 = jnp.float32
BF16 = jnp.bfloat16

D = 1024
N_META = 16
DEPTH = 4
EPS = 1e-6
HEADS = 8
QK_NOPE = 64
QK_ROPE = 32
V_DIM = 64
Q_RANK = 256
KV_RANK = 128
ROPE_THETA = 10000.0
SCALE = (QK_NOPE + QK_ROPE) ** -0.5
CONF_K = 31
SC_K = 3
N_DEV = 8

ADAM_LR = 0.001
ADAM_B1 = 0.9
ADAM_B2 = 0.999
ADAM_EPS = 1e-08
ADAM_WD = 0.01
ADAM_STEP = 10

RB = 384
HB = 32
LANE = 128
VMEM_LIMIT = 56 * 1024 * 1024

GL0, CQ0, CKV0, KR0, PV0, PG0, MG0, CU0, CG0, SBB0, SBC0, SBX0, SG0, ZW = (
    0, 4096, 4352, 4480, 4608, 4864, 5120, 5632, 6144, 6400, 6656, 6912, 7168, 7424)
ZSEG = ((0, 4096), (4096, 512), (4608, 2816))
LOG2E = 1.4426950408889634
LN2 = 0.6931471805599453

MESH = pl.DeviceIdType.MESH


def _cp(**kw):
    return pltpu.CompilerParams(vmem_limit_bytes=VMEM_LIMIT, **kw)


def _sig(x):
    return jax.nn.sigmoid(x)


def _silu(x):
    return x * _sig(x)


def _dsilu(x):
    s = _sig(x)
    return s * (1.0 + x * (1.0 - s))


def _dn(x, k):
    return x if k == 0 else pltpu.roll(x, k, 0)


def _up(x, k):
    return x if k == 0 else pltpu.roll(x, x.shape[0] - k, 0)


def _rope(t, c, s1, s2):
    return t * c + pltpu.roll(t, 16, 1) * s1 + pltpu.roll(t, LANE - 16, 1) * s2


def _rope_t(g, c, s1, s2):
    return g * c + pltpu.roll(g * s1, LANE - 16, 1) + pltpu.roll(g * s2, 16, 1)


def _mm(a, b, m, n, k, *, ta=False, tb=False, out_dtype=F32, tm, tn, tk, name,
        a_moff=0, a_koff=0, b_noff=0, b_koff=0, c=None):
    assert m % tm == 0 and n % tn == 0 and k % tk == 0, (name, m, n, k, tm, tn, tk)
    nk = k // tk
    dims = (((0,) if ta else (1,), (1,) if tb else (0,)), ((), ()))
    has_c = c is not None

    def body(a_ref, b_ref, *rest):
        c_ref = rest[0] if has_c else None
        o_ref = rest[1] if has_c else rest[0]
        scr = rest[2:] if has_c else rest[1:]
        part = lax.dot_general(a_ref[...].astype(BF16), b_ref[...].astype(BF16), dims,
                               preferred_element_type=F32)

        def finish(total):
            if has_c:
                total = total + c_ref[...]
            o_ref[...] = total.astype(out_dtype)

        if nk == 1:
            finish(part)
        else:
            acc = scr[0]
            kk = pl.program_id(2)

            @pl.when(kk == 0)
            def _():
                acc[...] = part

            @pl.when(kk > 0)
            def _():
                acc[...] += part

            @pl.when(kk == nk - 1)
            def _():
                finish(acc[...])

    if ta:
        a_spec = pl.BlockSpec((tk, tm), lambda i, j, q: (q + a_koff, i + a_moff))
    else:
        a_spec = pl.BlockSpec((tm, tk), lambda i, j, q: (i + a_moff, q + a_koff))
    if tb:
        b_spec = pl.BlockSpec((tn, tk), lambda i, j, q: (j + b_noff, q + b_koff))
    else:
        b_spec = pl.BlockSpec((tk, tn), lambda i, j, q: (q + b_koff, j + b_noff))
    o_spec = pl.BlockSpec((tm, tn), lambda i, j, q: (i, j))
    return pl.pallas_call(
        body, grid=(m // tm, n // tn, nk), in_specs=[a_spec, b_spec] + ([o_spec] if has_c else []),
        out_specs=o_spec, out_shape=jax.ShapeDtypeStruct((m, n), out_dtype),
        scratch_shapes=[pltpu.VMEM((tm, tn), F32)] if nk > 1 else [],
        name=name, compiler_params=_cp())(*((a, b, c) if has_c else (a, b)))


def _rms_fwd(x, g, lp):
    def body(x_ref, g_ref, h_ref):
        xv = x_ref[...]
        r = lax.rsqrt(jnp.mean(xv * xv, axis=-1, keepdims=True) + EPS)
        h_ref[...] = (xv * r * g_ref[...]).astype(BF16)

    return pl.pallas_call(
        body, grid=(lp // RB,),
        in_specs=[pl.BlockSpec((RB, D), lambda i: (i, 0)), pl.BlockSpec((1, D), lambda i: (0, 0))],
        out_specs=pl.BlockSpec((RB, D), lambda i: (i, 0)),
        out_shape=jax.ShapeDtypeStruct((lp, D), BF16), name="rms_fwd", compiler_params=_cp())(x, g)


def _rms_bwd(x, g, dh, dx_in, lp):
    def body(x_ref, g_ref, dh_ref, dxi_ref, dx_ref, dg_ref):
        i = pl.program_id(0)
        xv = x_ref[...]
        r = lax.rsqrt(jnp.mean(xv * xv, axis=-1, keepdims=True) + EPS)
        dy = dh_ref[...]
        a = dy * g_ref[...]
        dx_ref[...] = dxi_ref[...] + r * a - xv * (r * r * r) * jnp.mean(a * xv, axis=-1, keepdims=True)
        part = jnp.sum(dy * xv * r, axis=0, keepdims=True)

        @pl.when(i == 0)
        def _():
            dg_ref[...] = part

        @pl.when(i > 0)
        def _():
            dg_ref[...] += part

    blk = pl.BlockSpec((RB, D), lambda i: (i, 0))
    vec = pl.BlockSpec((1, D), lambda i: (0, 0))
    return pl.pallas_call(
        body, grid=(lp // RB,), in_specs=[blk, vec, blk, blk], out_specs=[blk, vec],
        out_shape=[jax.ShapeDtypeStruct((lp, D), F32), jax.ShapeDtypeStruct((1, D), F32)],
        name="rms_bwd", compiler_params=_cp())(x, g, dh, dx_in)


def _mla_prep(z, qg, kvg, tabs, lp):
    def body(z_ref, qg_ref, kvg_ref, c_ref, s1_ref, s2_ref, qn_ref, kvn_ref, kr_ref):
        cq = z_ref[:, 0:256]
        ckv = z_ref[:, 256:384]
        kr = z_ref[:, 384:512]
        rq = lax.rsqrt(jnp.mean(cq * cq, axis=-1, keepdims=True) + EPS)
        rk = lax.rsqrt(jnp.mean(ckv * ckv, axis=-1, keepdims=True) + EPS)
        qn_ref[...] = (cq * rq * qg_ref[...]).astype(BF16)
        kvn_ref[...] = (ckv * rk * kvg_ref[...]).astype(BF16)
        kr_ref[...] = _rope(kr, c_ref[...], s1_ref[...], s2_ref[...])

    tab = pl.BlockSpec((RB, LANE), lambda i: (i, 0))
    return pl.pallas_call(
        body, grid=(lp // RB,),
        in_specs=[pl.BlockSpec((RB, 512), lambda i: (i, CQ0 // 512)),
                  pl.BlockSpec((1, 256), lambda i: (0, 0)), pl.BlockSpec((1, 128), lambda i: (0, 0)),
                  tab, tab, tab],
        out_specs=[pl.BlockSpec((RB, 256), lambda i: (i, 0)), tab, tab],
        out_shape=[jax.ShapeDtypeStruct((lp, 256), BF16), jax.ShapeDtypeStruct((lp, 128), BF16),
                   jax.ShapeDtypeStruct((lp, 128), F32)],
        name="mla_prep", compiler_params=_cp())(z, qg, kvg, *tabs)


def _mla_prep_bwd(z, qg, kvg, dqn, dkvn, dkr, lp):
    def body(z_ref, qg_ref, kvg_ref, dqn_ref, dkvn_ref, dkr_ref, dz_ref, dqg_ref, dkvg_ref):
        i = pl.program_id(0)

        def rms_b(xv, g, dy):
            r = lax.rsqrt(jnp.mean(xv * xv, axis=-1, keepdims=True) + EPS)
            a = dy * g
            dx = r * a - xv * (r * r * r) * jnp.mean(a * xv, axis=-1, keepdims=True)
            return dx, jnp.sum(dy * xv * r, axis=0, keepdims=True)

        dcq, pq = rms_b(z_ref[:, 0:256], qg_ref[...], dqn_ref[...])
        dckv, pk = rms_b(z_ref[:, 256:384], kvg_ref[...], dkvn_ref[...])
        dz_ref[:, 0:256] = dcq.astype(BF16)
        dz_ref[:, 256:384] = dckv.astype(BF16)
        dz_ref[:, 384:512] = dkr_ref[...].astype(BF16)

        @pl.when(i == 0)
        def _():
            dqg_ref[...] = pq
            dkvg_ref[...] = pk

        @pl.when(i > 0)
        def _():
            dqg_ref[...] += pq
            dkvg_ref[...] += pk

    tab = pl.BlockSpec((RB, LANE), lambda i: (i, 0))
    return pl.pallas_call(
        body, grid=(lp // RB,),
        in_specs=[pl.BlockSpec((RB, 512), lambda i: (i, CQ0 // 512)),
                  pl.BlockSpec((1, 256), lambda i: (0, 0)), pl.BlockSpec((1, 128), lambda i: (0, 0)),
                  pl.BlockSpec((RB, 256), lambda i: (i, 0)), tab, tab],
        out_specs=[pl.BlockSpec((RB, 512), lambda i: (i, 0)),
                   pl.BlockSpec((1, 256), lambda i: (0, 0)), pl.BlockSpec((1, 128), lambda i: (0, 0))],
        out_shape=[jax.ShapeDtypeStruct((lp, 512), BF16), jax.ShapeDtypeStruct((1, 256), F32),
                   jax.ShapeDtypeStruct((1, 128), F32)],
        name="mla_prep_bwd", compiler_params=_cp())(z, qg, kvg, dqn, dkvn, dkr)


def _mla_post(q_raw, kv_raw, krr, tabs, lp):
    def body(q_ref, kv_ref, kr_ref, c_ref, s1_ref, s2_ref, qo_ref, ko_ref, vo_ref):
        c, s1, s2, kr = c_ref[...], s1_ref[...], s2_ref[...], kr_ref[...]
        for h in range(HEADS):
            sl = slice(LANE * h, LANE * (h + 1))
            qo_ref[:, sl] = (_rope(q_ref[:, sl], c, s1, s2) * (SCALE * LOG2E)).astype(BF16)
            ko_ref[:, sl] = (kv_ref[:, sl] + kr).astype(BF16)
        vo_ref[...] = kv_ref[:, 1024:1536].astype(BF16)

    tab = pl.BlockSpec((RB, LANE), lambda i: (i, 0))
    wide = pl.BlockSpec((RB, 1024), lambda i: (i, 0))
    return pl.pallas_call(
        body, grid=(lp // RB,),
        in_specs=[wide, pl.BlockSpec((RB, 1536), lambda i: (i, 0)), tab, tab, tab, tab],
        out_specs=[wide, wide, pl.BlockSpec((RB, 512), lambda i: (i, 0))],
        out_shape=[jax.ShapeDtypeStruct((lp, 1024), BF16), jax.ShapeDtypeStruct((lp, 1024), BF16),
                   jax.ShapeDtypeStruct((lp, 512), BF16)],
        name="mla_post", compiler_params=_cp())(q_raw, kv_raw, krr, *tabs)


def _mla_post_bwd(dq, dk, dv, tabs, lp):
    def body(dq_ref, dk_ref, dv_ref, c_ref, s1_ref, s2_ref, dqr_ref, dkv_ref, dkr_ref):
        c, s1, s2 = c_ref[...], s1_ref[...], s2_ref[...]
        lane = lax.broadcasted_iota(jnp.int32, (1, LANE), 1)
        ropel = (lane >= QK_NOPE) & (lane < QK_NOPE + QK_ROPE)
        ksum = jnp.zeros((RB, LANE), F32)
        for h in range(HEADS):
            sl = slice(LANE * h, LANE * (h + 1))
            dqr_ref[:, sl] = _rope_t(dq_ref[:, sl].astype(F32) * SCALE, c, s1, s2).astype(BF16)
            dkt = dk_ref[:, sl]
            dkv_ref[:, sl] = dkt
            ksum = ksum + dkt.astype(F32)
        dkv_ref[:, 1024:1536] = dv_ref[...]
        dkr_ref[...] = jnp.where(ropel, _rope_t(jnp.where(ropel, ksum, 0.0), c, s1, s2), 0.0)

    tab = pl.BlockSpec((RB, LANE), lambda i: (i, 0))
    wide = pl.BlockSpec((RB, 1024), lambda i: (i, 0))
    return pl.pallas_call(
        body, grid=(lp // RB,),
        in_specs=[wide, wide, pl.BlockSpec((RB, 512), lambda i: (i, 0)), tab, tab, tab],
        out_specs=[wide, pl.BlockSpec((RB, 1536), lambda i: (i, 0)), tab],
        out_shape=[jax.ShapeDtypeStruct((lp, 1024), BF16), jax.ShapeDtypeStruct((lp, 1536), BF16),
                   jax.ShapeDtypeStruct((lp, 128), F32)],
        name="mla_post_bwd", compiler_params=_cp())(dq, dk, dv, *tabs)


def _head_lanes(e):
    lane = lax.broadcasted_iota(jnp.int32, (1, LANE), 1)
    return lane >= V_DIM if e else lane < V_DIM


ONE_LANE = (V_DIM, 0)


def _attn_fwd(q, k, v, lp, gather=()):
    nq = lp // RB
    n = len(gather)
    steps = HEADS // 2

    def body(q_ref, k_ref, v_ref, *rest):
        o_ref, lse_ref = rest[n], rest[n + 1]
        vm_scr = rest[2 * n + 2]
        if n:
            g_start, g_forward, g_finish = _gather_phases(rest[:n], rest[n + 2:2 * n + 2], *rest[2 * n + 3:])
            pl.when(pl.program_id(0) == 0)(g_start)
            pl.when(pl.program_id(0) == steps // 2)(g_forward)
        lane = lax.broadcasted_iota(jnp.int32, (1, LANE), 1)
        vv = v_ref[...]
        for e in range(2):
            ones = jnp.where(lane == ONE_LANE[e], 1.0, 0.0).astype(BF16)
            vm_scr[e] = jnp.where(_head_lanes(e), vv, jnp.broadcast_to(ones, vv.shape))
        causal = (lax.broadcasted_iota(jnp.int32, (RB, RB), 1) <= lax.broadcasted_iota(jnp.int32, (RB, RB), 0))

        def qblock(i, _):
            rows = pl.ds(pl.multiple_of(i * RB, RB), RB)
            qs = [q_ref[rows, LANE * e:LANE * (e + 1)] for e in range(2)]

            def scores(j):
                cols = pl.ds(pl.multiple_of(j * RB, RB), RB)
                return tuple(lax.dot_general(qs[e], k_ref[cols, LANE * e:LANE * (e + 1)], (((1,), (1,)), ((), ())),
                                             preferred_element_type=F32) for e in range(2))

            def update(j, s, carry, masked):
                cols = pl.ds(pl.multiple_of(j * RB, RB), RB)
                out = []
                for e in range(2):
                    m, acc = carry[2 * e], carry[2 * e + 1]
                    se = jnp.where(causal, s[e], -jnp.inf) if masked else s[e]
                    m_new = jnp.maximum(m, jnp.max(se, axis=-1, keepdims=True))
                    p = jnp.exp2((se - m_new).astype(BF16))
                    acc = jnp.exp2(m - m_new) * acc + jnp.dot(p, vm_scr[e, cols, :], preferred_element_type=F32)
                    out += [m_new, acc]
                return tuple(out)

            def pair(jj, c):
                sa, sb = scores(2 * jj), scores(2 * jj + 1)
                return update(2 * jj + 1, sb, update(2 * jj, sa, c, False), False)

            m0 = jnp.full((RB, 1), -jnp.inf, F32)
            a0 = jnp.zeros((RB, LANE), F32)
            carry = lax.fori_loop(0, i // 2, pair, (m0, a0, m0, a0))
            carry = lax.cond(i % 2 == 1, lambda c: update(i - 1, scores(i - 1), c, False), lambda c: c, carry)
            carry = update(i, scores(i), carry, True)
            o, lse = [], []
            for e in range(2):
                m, acc = carry[2 * e], carry[2 * e + 1]
                l = acc[:, ONE_LANE[e]:ONE_LANE[e] + 1]
                o.append(acc / l)
                lse.append(jnp.broadcast_to(m + jnp.log2(l), (RB, LANE)))
            o_ref[rows, :] = jnp.where(_head_lanes(0), o[0], o[1])
            lse_ref[rows, :] = jnp.where(_head_lanes(0), lse[0], lse[1])
            return 0

        lax.fori_loop(0, nq, qblock, 0)
        if n:
            pl.when(pl.program_id(0) == steps - 1)(g_finish)

    two = pl.BlockSpec((lp, 2 * LANE), lambda h: (0, h))
    one = pl.BlockSpec((lp, LANE), lambda h: (0, h))
    anyspec = pl.BlockSpec(memory_space=pl.ANY)
    return pl.pallas_call(
        body, grid=(steps,), in_specs=[two, two, one] + [anyspec] * n, out_specs=[one, one] + [anyspec] * n,
        out_shape=[jax.ShapeDtypeStruct((lp, 512), F32), jax.ShapeDtypeStruct((lp, 512), F32)]
        + _gather_shapes(gather),
        scratch_shapes=[pltpu.VMEM((2, lp, LANE), BF16)] + (_comm_sems(n) if n else []),
        name="attn_fwd_gather" if n else "attn_fwd", compiler_params=_cp())(q, k, v, *gather)


def _attn_bwd(q, k, v, o, do, lse, lp, scatter=None):
    nq = lp // RB
    xs, bufs, layer = scatter if scatter else ((), (), None)
    n = len(xs)
    steps = HEADS // 2

    def body(q_ref, k_ref, v_ref, o_ref, do_ref, lse_ref, *rest):
        dq_ref, dk_ref, dv_ref = rest[2 * n:2 * n + 3]
        vm_scr, dom_scr, dl_scr, dq_scr = rest[3 * n + 3:3 * n + 7]
        if n:
            s_start, s_finish = _scatter_phases(rest[:n], rest[2 * n + 3:3 * n + 3], *rest[3 * n + 7:], layer)
            pl.when(pl.program_id(0) == 0)(s_start)
        causal = (lax.broadcasted_iota(jnp.int32, (RB, RB), 1) <= lax.broadcasted_iota(jnp.int32, (RB, RB), 0))
        vv = v_ref[...]
        for e in range(2):
            vm_scr[e] = jnp.where(_head_lanes(e), vv, jnp.zeros_like(vv))

        def prep(i, _):
            rows = pl.ds(pl.multiple_of(i * RB, RB), RB)
            prod = do_ref[rows, :] * o_ref[rows, :]
            dls = []
            for e in range(2):
                hm = _head_lanes(e)
                dom_scr[e, rows, :] = jnp.where(hm, do_ref[rows, :], 0.0).astype(BF16)
                dls.append(jnp.sum(jnp.where(hm, prod, 0.0), axis=-1, keepdims=True))
            dl_scr[rows, :] = jnp.where(_head_lanes(0), dls[0], dls[1])
            dq_scr[rows, :] = jnp.zeros((RB, 2 * LANE), F32)
            return 0

        lax.fori_loop(0, nq, prep, 0)

        def kvblock(j, _):
            cols = pl.ds(pl.multiple_of(j * RB, RB), RB)
            kbs = [k_ref[cols, LANE * e:LANE * (e + 1)] for e in range(2)]

            def products(i):
                rows = pl.ds(pl.multiple_of(i * RB, RB), RB)
                out = []
                for e in range(2):
                    out.append(lax.dot_general(q_ref[rows, LANE * e:LANE * (e + 1)], kbs[e],
                                               (((1,), (1,)), ((), ())), preferred_element_type=F32))
                    out.append(lax.dot_general(dom_scr[e, rows, :], vm_scr[e, cols, :],
                                               (((1,), (1,)), ((), ())), preferred_element_type=F32))
                return tuple(out)

            def update(i, sd, carry, masked):
                rows = pl.ds(pl.multiple_of(i * RB, RB), RB)
                out = []
                for e in range(2):
                    dk, dv = carry[2 * e], carry[2 * e + 1]
                    sl = slice(LANE * e, LANE * (e + 1))
                    col1 = slice(V_DIM * e, V_DIM * e + 1)
                    s, dp = sd[2 * e], sd[2 * e + 1]
                    if masked:
                        s = jnp.where(causal, s, -jnp.inf)
                    p = jnp.exp2((s - lse_ref[rows, col1]).astype(BF16))
                    dv = dv + lax.dot_general(p, dom_scr[e, rows, :], (((0,), (0,)), ((), ())),
                                              preferred_element_type=F32)
                    ds = p * (dp - dl_scr[rows, col1]).astype(BF16)
                    dk = dk + lax.dot_general(ds, q_ref[rows, sl], (((0,), (0,)), ((), ())),
                                              preferred_element_type=F32)
                    dq_scr[rows, sl] += jnp.dot(ds, kbs[e], preferred_element_type=F32)
                    out += [dk, dv]
                return tuple(out)

            def pair(t, c):
                i0 = j + 1 + 2 * t
                pa, pb = products(i0), products(i0 + 1)
                return update(i0 + 1, pb, update(i0, pa, c, False), False)

            zero = jnp.zeros((RB, LANE), F32)
            carry = update(j, products(j), (zero, zero, zero, zero), True)
            below = nq - 1 - j
            carry = lax.fori_loop(0, below // 2, pair, carry)
            dk0, dv0, dk1, dv1 = lax.cond(below % 2 == 1,
                                          lambda c: update(nq - 1, products(nq - 1), c, False), lambda c: c, carry)
            dk_ref[cols, 0:LANE] = (dk0 * LN2).astype(BF16)
            dk_ref[cols, LANE:2 * LANE] = (dk1 * LN2).astype(BF16)
            dv_ref[cols, :] = (dv0 + dv1).astype(BF16)
            return 0

        lax.fori_loop(0, nq, kvblock, 0)

        def fin(i, _):
            rows = pl.ds(pl.multiple_of(i * RB, RB), RB)
            dq_ref[rows, :] = dq_scr[rows, :].astype(BF16)
            return 0

        lax.fori_loop(0, nq, fin, 0)
        if n:
            pl.when(pl.program_id(0) == steps - 1)(s_finish)

    two = pl.BlockSpec((lp, 2 * LANE), lambda h: (0, h))
    one = pl.BlockSpec((lp, LANE), lambda h: (0, h))
    anyspec = pl.BlockSpec(memory_space=pl.ANY)
    return pl.pallas_call(
        body, grid=(steps,), in_specs=[two, two, one, one, one, one] + [anyspec] * (2 * n),
        out_specs=[two, two, one] + [anyspec] * n,
        out_shape=[jax.ShapeDtypeStruct((lp, 1024), BF16), jax.ShapeDtypeStruct((lp, 1024), BF16),
                   jax.ShapeDtypeStruct((lp, 512), BF16)] + [jax.ShapeDtypeStruct(b.shape, b.dtype) for b in bufs],
        input_output_aliases={6 + n + a: 3 + a for a in range(n)},
        scratch_shapes=[pltpu.VMEM((2, lp, LANE), BF16), pltpu.VMEM((2, lp, LANE), BF16),
                        pltpu.VMEM((lp, LANE), F32), pltpu.VMEM((lp, 2 * LANE), F32)]
        + (_comm_sems(n) if n else []),
        name="attn_bwd_scatter" if n else "attn_bwd", compiler_params=_cp())(q, k, v, o, do, lse, *xs, *bufs)


def _pool_lane_windows():
    lane = lax.broadcasted_iota(jnp.int32, (1, 256), 1)
    return jnp.where(lane < 64, 2, jnp.where(lane < 128, 4, jnp.where(lane < 192, 8, 16)))


def _by_window(wl, s2, s4, s8, s16):
    return jnp.where(wl == 2, s2, jnp.where(wl == 4, s4, jnp.where(wl == 8, s8, s16)))


def _pool_fwd_rows(pv_ext, t0):
    n = pv_ext.shape[0]
    wl = _pool_lane_windows()
    s2 = pv_ext + _dn(pv_ext, 1)
    s4 = s2 + _dn(s2, 2)
    s8 = s4 + _dn(s4, 4)
    s16 = s8 + _dn(s8, 8)
    t = t0 + lax.broadcasted_iota(jnp.int32, (n, 1), 0)
    cnt = jnp.maximum(jnp.minimum(t + 1, wl), 1).astype(F32)
    return _by_window(wl, s2, s4, s8, s16) / cnt - pv_ext


def _conv_dn(x_ext, w_ref, taps):
    acc = w_ref[taps - 1:taps, :] * x_ext
    for j in range(1, taps):
        acc = acc + w_ref[taps - 1 - j:taps - j, :] * _dn(x_ext, j)
    return acc


def _conv_up(g_ext, w_ref, taps):
    acc = w_ref[taps - 1:taps, :] * g_ext
    for j in range(1, taps):
        acc = acc + w_ref[taps - 1 - j:taps - j, :] * _up(g_ext, j)
    return acc


def _ln_fwd(c, g, b):
    mu = jnp.mean(c, axis=-1, keepdims=True)
    xc = c - mu
    r = lax.rsqrt(jnp.mean(xc * xc, axis=-1, keepdims=True) + EPS)
    xh = xc * r
    return xh * g + b, xh, r


def _halo_specs(lp, width, col):
    per = RB // HB
    last = lp // HB - 1
    cur = pl.BlockSpec((RB, width), lambda i: (i, col))
    prev = pl.BlockSpec((HB, width), lambda i: (jnp.maximum(i * per - 1, 0), col))
    nxt = pl.BlockSpec((HB, width), lambda i: (jnp.minimum((i + 1) * per, last), col))
    return cur, prev, nxt


def _mix_fwd(z, oat, bd, pscale, cw, cb, lng, lnb, sw, lp):
    def body(za, zah, mg, oat_ref, cu, cuh, cg, sbb, sbc, sbch, sbx, sbxh, sg,
             bd_ref, ps_ref, cw_ref, cb_ref, lng_ref, lnb_ref, sw_ref, u_ref):
        i = pl.program_id(0)
        pm = jnp.where(i > 0, 1.0, 0.0).astype(F32)
        pv = jnp.concatenate([zah[:, 0:256] * pm, za[:, 0:256]], axis=0)
        p = _pool_fwd_rows(pv, i * RB - HB)[HB:]
        y = jnp.dot(p.astype(BF16), bd_ref[...], preferred_element_type=F32)
        u_ref[:, 0:256] = (y * ps_ref[...] * _silu(za[:, 256:512])).astype(BF16)
        u_ref[:, 256:768] = (oat_ref[...] * _silu(mg[...])).astype(BF16)
        ce = jnp.concatenate([cuh[...] * pm, cu[...]], axis=0)
        glu = ce[:, 0:256] * _sig(ce[:, 256:512])
        c = _conv_dn(glu, cw_ref, CONF_K)[HB:] + cb_ref[...]
        n, _, _ = _ln_fwd(c, lng_ref[...], lnb_ref[...])
        u_ref[:, 768:1024] = (_silu(n) * _silu(cg[...])).astype(BF16)
        qe = jnp.concatenate([sbch[...] * sbxh[...] * pm, sbc[...] * sbx[...]], axis=0)
        cv = _conv_dn(qe, sw_ref, SC_K)[HB:]
        u_ref[:, 1024:1280] = (sbb[...] * cv * _silu(sg[...])).astype(BF16)

    a_cur, a_prev, _ = _halo_specs(lp, 512, PV0 // 512)
    cu_cur, cu_prev, _ = _halo_specs(lp, 512, CU0 // 512)
    sc_cur, sc_prev, _ = _halo_specs(lp, 256, SBC0 // 256)
    sx_cur, sx_prev, _ = _halo_specs(lp, 256, SBX0 // 256)
    c256 = lambda c0: pl.BlockSpec((RB, 256), lambda i: (i, c0 // 256))
    full = lambda r, c: pl.BlockSpec((r, c), lambda i: (0, 0))
    return pl.pallas_call(
        body, grid=(lp // RB,),
        in_specs=[a_cur, a_prev, pl.BlockSpec((RB, 512), lambda i: (i, MG0 // 512)),
                  pl.BlockSpec((RB, 512), lambda i: (i, 0)),
                  cu_cur, cu_prev, c256(CG0), c256(SBB0), sc_cur, sc_prev, sx_cur, sx_prev, c256(SG0),
                  full(256, 256), full(1, 256), full(32, 256), full(1, 256), full(1, 256), full(1, 256),
                  full(8, 256)],
        out_specs=pl.BlockSpec((RB, 1280), lambda i: (i, 0)),
        out_shape=jax.ShapeDtypeStruct((lp, 1280), BF16),
        name="mix_fwd", compiler_params=_cp())(z, z, z, oat, z, z, z, z, z, z, z, z, z,
                                               bd, pscale, cw, cb, lng, lnb, sw)


def _mix_bwd(z, oat, du, bd, pscale, cw, cb, lng, lnb, sw, lp):
    nb = lp // RB
    ne = RB + 2 * HB
    nf = RB + HB

    def body(za, zah, zan, mg, oat_ref, cu, cuh, cun, cg, cgn, sbb, sbbn, sbc, sbch, sbcn, sbx, sbxh, sbxn,
             sg, sgn, du_ref, dun_ref, bd_ref, ps_ref, cw_ref, cb_ref, lng_ref, lnb_ref, sw_ref,
             dzx_ref, doat_ref, dbd_ref, dcw_ref, dsw_ref, dsm_ref):
        xa, xm, xc = 0, MG0 - PV0, CU0 - PV0
        i = pl.program_id(0)
        pm = jnp.where(i > 0, 1.0, 0.0).astype(F32)
        nm = jnp.where(i < nb - 1, 1.0, 0.0).astype(F32)

        def ext(cur, prev, nxt, sl=slice(None)):
            return jnp.concatenate([prev[:, sl] * pm, cur[:, sl], nxt[:, sl] * nm], axis=0)

        def fwd(cur, nxt, sl=slice(None)):
            return jnp.concatenate([cur[:, sl], nxt[:, sl] * nm], axis=0)

        def csum(x):
            return jnp.sum(x, axis=0, keepdims=True)

        @pl.when(i == 0)
        def _():
            dbd_ref[...] = jnp.zeros((256, 256), F32)
            dcw_ref[...] = jnp.zeros((32, 256), F32)
            dsw_ref[...] = jnp.zeros((8, 256), F32)
            dsm_ref[...] = jnp.zeros((8, 256), F32)

        a_cols, b_cols = slice(0, 256), slice(256, 512)
        pv_e = ext(za, zah, zan, a_cols)
        p = _pool_fwd_rows(pv_e, i * RB - HB)[HB:HB + RB]
        pb = p.astype(BF16)
        y = jnp.dot(pb, bd_ref[...], preferred_element_type=F32)
        pg_f = fwd(za, zan, b_cols)
        dua_f = fwd(du_ref, dun_ref, slice(0, 256))
        dyp_f = dua_f * ps_ref[...] * _silu(pg_f)
        dypb = dyp_f.astype(BF16)
        dp_f = lax.dot_general(dypb, bd_ref[...], (((1,), (1,)), ((), ())), preferred_element_type=F32)
        wl = _pool_lane_windows()
        t = i * RB + lax.broadcasted_iota(jnp.int32, (nf, 1), 0)
        cnt = jnp.minimum(t + 1, wl).astype(F32)
        qf = dp_f / cnt
        f2 = qf + _up(qf, 1)
        f4 = f2 + _up(f2, 2)
        f8 = f4 + _up(f4, 4)
        f16 = f8 + _up(f8, 8)
        dpv = (_by_window(wl, f2, f4, f8, f16) - dp_f)[0:RB]
        dua = du_ref[:, 0:256]
        pg = za[:, b_cols]
        dpg = dua * y * ps_ref[...] * _dsilu(pg)
        dzx_ref[:, xa:xa + 256] = dpv.astype(BF16)
        dzx_ref[:, xa + 256:xa + 512] = dpg.astype(BF16)
        d_scale = csum(dua * y * _silu(pg))
        d_bd = lax.dot_general(pb, dypb[0:RB], (((0,), (0,)), ((), ())), preferred_element_type=F32)

        dub = du_ref[:, 256:768]
        mgv = mg[...]
        dzx_ref[:, xm:xm + 512] = (dub * oat_ref[...] * _dsilu(mgv)).astype(BF16)
        doat_ref[...] = dub * _silu(mgv)

        a_e = ext(cu, cuh, cun, slice(0, 256))
        gt_e = ext(cu, cuh, cun, slice(256, 512))
        sg_e = _sig(gt_e)
        glu_e = a_e * sg_e
        c_f = _conv_dn(glu_e, cw_ref, CONF_K)[HB:] + cb_ref[...]
        n_f, xh_f, r_f = _ln_fwd(c_f, lng_ref[...], lnb_ref[...])
        cg_f = fwd(cg, cgn)
        duc_f = fwd(du_ref, dun_ref, slice(768, 1024))
        sw_f = _silu(n_f)
        dcg = (duc_f * sw_f * _dsilu(cg_f))[0:RB]
        dn_f = duc_f * _silu(cg_f) * _dsilu(n_f)
        a_f = dn_f * lng_ref[...]
        dc_f = r_f * (a_f - jnp.mean(a_f, axis=-1, keepdims=True)
                      - xh_f * jnp.mean(a_f * xh_f, axis=-1, keepdims=True))
        d_lng = csum((dn_f * xh_f)[0:RB])
        d_lnb = csum(dn_f[0:RB])
        d_cb = csum(dc_f[0:RB])
        dglu = _conv_up(dc_f, cw_ref, CONF_K)[0:RB]
        dc_c = dc_f[0:RB]
        for kk in range(CONF_K):
            j = CONF_K - 1 - kk
            dcw_ref[kk:kk + 1, :] += csum(dc_c * _dn(glu_e, j)[HB:HB + RB])

        sgc = sg_e[HB:HB + RB]
        a_c = a_e[HB:HB + RB]
        dzx_ref[:, xc:xc + 256] = (dglu * sgc).astype(BF16)
        dzx_ref[:, xc + 256:xc + 512] = (dglu * a_c * sgc * (1.0 - sgc)).astype(BF16)
        dzx_ref[:, xc + 512:xc + 768] = dcg.astype(BF16)

        c_e = ext(sbc, sbch, sbcn)
        x_e = ext(sbx, sbxh, sbxn)
        q_e = c_e * x_e
        cv_f = _conv_dn(q_e, sw_ref, SC_K)[HB:]
        bg_f = fwd(sbb, sbbn)
        sg_f = fwd(sg, sgn)
        dud_f = fwd(du_ref, dun_ref, slice(1024, 1280))
        ssg_f = _silu(sg_f)
        dcv_f = dud_f * bg_f * ssg_f
        dbg = (dud_f * cv_f * ssg_f)[0:RB]
        dsg = (dud_f * bg_f * cv_f * _dsilu(sg_f))[0:RB]
        dq = _conv_up(dcv_f, sw_ref, SC_K)[0:RB]
        dcv_c = dcv_f[0:RB]
        for kk in range(SC_K):
            j = SC_K - 1 - kk
            dsw_ref[kk:kk + 1, :] += csum(dcv_c * _dn(q_e, j)[HB:HB + RB])

        dzx_ref[:, xc + 768:xc + 1024] = dbg.astype(BF16)
        dzx_ref[:, xc + 1024:xc + 1280] = (dq * x_e[HB:HB + RB]).astype(BF16)
        dzx_ref[:, xc + 1280:xc + 1536] = (dq * c_e[HB:HB + RB]).astype(BF16)
        dzx_ref[:, xc + 1536:xc + 1792] = dsg.astype(BF16)

        dbd_ref[...] += d_bd
        dsm_ref[0:1, :] += d_scale
        dsm_ref[1:2, :] += d_cb
        dsm_ref[2:3, :] += d_lng
        dsm_ref[3:4, :] += d_lnb

    a3 = _halo_specs(lp, 512, PV0 // 512)
    cu3 = _halo_specs(lp, 512, CU0 // 512)
    cg3 = _halo_specs(lp, 256, CG0 // 256)
    sbb3 = _halo_specs(lp, 256, SBB0 // 256)
    sbc3 = _halo_specs(lp, 256, SBC0 // 256)
    sbx3 = _halo_specs(lp, 256, SBX0 // 256)
    sg3 = _halo_specs(lp, 256, SG0 // 256)
    du3 = _halo_specs(lp, 1280, 0)
    full = lambda r, c: pl.BlockSpec((r, c), lambda i: (0, 0))
    in_specs = [a3[0], a3[1], a3[2], pl.BlockSpec((RB, 512), lambda i: (i, MG0 // 512)),
                pl.BlockSpec((RB, 512), lambda i: (i, 0)),
                cu3[0], cu3[1], cu3[2], cg3[0], cg3[2], sbb3[0], sbb3[2],
                sbc3[0], sbc3[1], sbc3[2], sbx3[0], sbx3[1], sbx3[2], sg3[0], sg3[2],
                du3[0], du3[2],
                full(256, 256), full(1, 256), full(32, 256), full(1, 256), full(1, 256), full(1, 256),
                full(8, 256)]
    out_specs = [pl.BlockSpec((RB, ZW - PV0), lambda i: (i, 0)), pl.BlockSpec((RB, 512), lambda i: (i, 0)),
                 full(256, 256), full(32, 256), full(8, 256), full(8, 256)]
    out_shape = [jax.ShapeDtypeStruct((lp, ZW - PV0), BF16), jax.ShapeDtypeStruct((lp, 512), F32),
                 jax.ShapeDtypeStruct((256, 256), F32), jax.ShapeDtypeStruct((32, 256), F32),
                 jax.ShapeDtypeStruct((8, 256), F32), jax.ShapeDtypeStruct((8, 256), F32)]
    return pl.pallas_call(
        body, grid=(nb,), in_specs=in_specs, out_specs=out_specs, out_shape=out_shape,
        name="mix_bwd", compiler_params=_cp())(
            z, z, z, z, oat, z, z, z, z, z, z, z, z, z, z, z, z, z, z, z, du, du,
            bd, pscale, cw, cb, lng, lnb, sw)


U_OFF = (0, 256, 768, 1024, 1280)
MRB = 192


def _merge_fwd(x, u, z, gb, wout, wo, gpost, lp):
    MRB = RB

    def body(x_ref, u_ref, gl_ref, gb_ref, wout_ref, wo_ref, g_ref, xo_ref, m_ref, o2_ref):
        m = jnp.zeros((MRB, D), F32)
        for b in range(4):
            y = jnp.dot(u_ref[:, U_OFF[b]:U_OFF[b + 1]], wout_ref[U_OFF[b]:U_OFF[b + 1], :],
                        preferred_element_type=F32)
            sl = slice(D * b, D * (b + 1))
            m = m + _sig(gl_ref[:, sl] + gb_ref[:, sl]) * y
        mb = m.astype(BF16)
        m_ref[...] = mb
        o2 = jnp.dot(mb, wo_ref[...], preferred_element_type=F32)
        o2_ref[...] = o2
        r = lax.rsqrt(jnp.mean(o2 * o2, axis=-1, keepdims=True) + EPS)
        xo_ref[...] = x_ref[...] + o2 * r * g_ref[...]

    blk = pl.BlockSpec((MRB, D), lambda i: (i, 0))
    full = lambda r, c: pl.BlockSpec((r, c), lambda i: (0, 0))
    return pl.pallas_call(
        body, grid=(lp // MRB,),
        in_specs=[blk, pl.BlockSpec((MRB, 1280), lambda i: (i, 0)), pl.BlockSpec((MRB, 4096), lambda i: (i, 0)),
                  full(1, 4096), full(1280, D), full(D, D), full(1, D)],
        out_specs=[blk, blk, blk],
        out_shape=[jax.ShapeDtypeStruct((lp, D), F32), jax.ShapeDtypeStruct((lp, D), BF16),
                   jax.ShapeDtypeStruct((lp, D), F32)],
        name="merge_fwd", compiler_params=_cp())(x, u, z, gb, wout, wo, gpost)


def _merge_bwd(dx, o2, u, z, gb, wout, wo, gpost, lp):
    def body(dx_ref, o2_ref, u_ref, gl_ref, gb_ref, wout_ref, wo_ref, g_ref,
             do2_ref, dgl_ref, dy_ref, du_ref, dgb_ref, dg_ref):
        i = pl.program_id(0)
        o2 = o2_ref[...]
        dy = dx_ref[...]
        r = lax.rsqrt(jnp.mean(o2 * o2, axis=-1, keepdims=True) + EPS)
        a = dy * g_ref[...]
        do2 = (r * a - o2 * (r * r * r) * jnp.mean(a * o2, axis=-1, keepdims=True)).astype(BF16)
        do2_ref[...] = do2
        dg = jnp.sum(dy * o2 * r, axis=0, keepdims=True)
        dm = lax.dot_general(do2, wo_ref[...], (((1,), (1,)), ((), ())), preferred_element_type=F32)
        for b in range(4):
            rows = slice(U_OFF[b], U_OFF[b + 1])
            y = jnp.dot(u_ref[:, rows], wout_ref[rows, :], preferred_element_type=F32)
            sl = slice(D * b, D * (b + 1))
            gt = _sig(gl_ref[:, sl] + gb_ref[:, sl])
            dgl = dm * y * gt * (1.0 - gt)
            dgl_ref[:, sl] = dgl.astype(BF16)
            part = jnp.sum(dgl, axis=0, keepdims=True)

            @pl.when(i == 0)
            def _(part=part, sl=sl):
                dgb_ref[:, sl] = part

            @pl.when(i > 0)
            def _(part=part, sl=sl):
                dgb_ref[:, sl] += part

            dyb = (dm * gt).astype(BF16)
            dy_ref[:, sl] = dyb
            du_ref[:, rows] = lax.dot_general(dyb, wout_ref[rows, :], (((1,), (1,)), ((), ())),
                                              preferred_element_type=F32)

        @pl.when(i == 0)
        def _():
            dg_ref[...] = dg

        @pl.when(i > 0)
        def _():
            dg_ref[...] += dg

    blk = pl.BlockSpec((MRB, D), lambda i: (i, 0))
    wide = pl.BlockSpec((MRB, 4096), lambda i: (i, 0))
    ub = pl.BlockSpec((MRB, 1280), lambda i: (i, 0))
    full = lambda r, c: pl.BlockSpec((r, c), lambda i: (0, 0))
    return pl.pallas_call(
        body, grid=(lp // MRB,),
        in_specs=[blk, blk, ub, wide, full(1, 4096), full(1280, D), full(D, D), full(1, D)],
        out_specs=[blk, wide, wide, ub, full(1, 4096), full(1, D)],
        out_shape=[jax.ShapeDtypeStruct((lp, D), BF16), jax.ShapeDtypeStruct((lp, 4096), BF16),
                   jax.ShapeDtypeStruct((lp, 4096), BF16), jax.ShapeDtypeStruct((lp, 1280), F32),
                   jax.ShapeDtypeStruct((1, 4096), F32), jax.ShapeDtypeStruct((1, D), F32)],
        name="merge_bwd", compiler_params=_cp())(dx, o2, u, z, gb, wout, wo, gpost)


def _dh(dgl, dzq, dzx, w_in, lp, scatter=None):
    xs, bufs, layer = scatter if scatter else ((), (), None)
    n = len(xs)
    steps = lp // RB
    segs = ((0, CQ0), (CQ0, PV0), (PV0, ZW))

    def body(gl_ref, zq_ref, zx_ref, w_ref, *rest):
        o_ref = rest[2 * n]
        if n:
            s_start, s_finish = _scatter_phases(rest[:n], rest[2 * n + 1:3 * n + 1], *rest[3 * n + 1:], layer)
            pl.when(pl.program_id(0) == 0)(s_start)
        acc = None
        for a_ref, (lo, hi) in zip((gl_ref, zq_ref, zx_ref), segs):
            part = lax.dot_general(a_ref[...], w_ref[:, lo:hi], (((1,), (1,)), ((), ())),
                                   preferred_element_type=F32)
            acc = part if acc is None else acc + part
        o_ref[...] = acc
        if n:
            pl.when(pl.program_id(0) == steps - 1)(s_finish)

    anyspec = pl.BlockSpec(memory_space=pl.ANY)
    row = lambda w: pl.BlockSpec((RB, w), lambda i: (i, 0))
    return pl.pallas_call(
        body, grid=(steps,),
        in_specs=[row(CQ0), row(PV0 - CQ0), row(ZW - PV0), pl.BlockSpec((D, ZW), lambda i: (0, 0))]
        + [anyspec] * (2 * n),
        out_specs=[row(D)] + [anyspec] * n,
        out_shape=[jax.ShapeDtypeStruct((lp, D), F32)] + [jax.ShapeDtypeStruct(b.shape, b.dtype) for b in bufs],
        input_output_aliases={4 + n + a: 1 + a for a in range(n)},
        scratch_shapes=_comm_sems(n) if n else [],
        name="dh_scatter" if n else "dh", compiler_params=_cp())(dgl, dzq, dzx, w_in, *xs, *bufs)


def _loss_head(xf, tgt, n_real, lp):
    def body(x_ref, t_ref, dy_ref, ls_ref):
        i = pl.program_id(0)
        t = i * RB + lax.broadcasted_iota(jnp.int32, (RB, 1), 0)
        real = (t >= N_META) & (t < n_real)
        err = jnp.where(real, x_ref[...] - t_ref[...], 0.0)
        dy_ref[...] = err / D
        part = 0.5 * jnp.sum(jnp.mean(err * err, axis=-1, keepdims=True), axis=0, keepdims=True)
        part = jnp.broadcast_to(part, (8, LANE))

        @pl.when(i == 0)
        def _():
            ls_ref[...] = part

        @pl.when(i > 0)
        def _():
            ls_ref[...] += part

    blk = pl.BlockSpec((RB, D), lambda i: (i, 0))
    return pl.pallas_call(
        body, grid=(lp // RB,), in_specs=[blk, blk],
        out_specs=[blk, pl.BlockSpec((8, LANE), lambda i: (0, 0))],
        out_shape=[jax.ShapeDtypeStruct((lp, D), F32), jax.ShapeDtypeStruct((8, LANE), F32)],
        name="loss_head", compiler_params=_cp())(xf, tgt)


def _peer(d):
    x, y, c = lax.axis_index("x"), lax.axis_index("y"), lax.axis_index("c")
    return (x ^ ((d >> 2) & 1), y ^ ((d >> 1) & 1), c ^ (d & 1))


def _index_of(p):
    return 4 * p[0] + 2 * p[1] + p[2]


def _all_gather(xs, name):
    n = len(xs)

    def body(*refs):
        start, forward, finish = _gather_phases(refs[:n], refs[n:2 * n], *refs[2 * n:])
        start()
        forward()
        finish()

    anyspec = pl.BlockSpec(memory_space=pl.ANY)
    return pl.pallas_call(
        body, in_specs=[anyspec] * n, out_specs=[anyspec] * n,
        out_shape=_gather_shapes(xs), scratch_shapes=_comm_sems(n), name=name)(*xs)


def _gather_shapes(xs):
    return [jax.ShapeDtypeStruct((N_DEV,) + x.shape, x.dtype) for x in xs]


def _comm_sems(n):
    return [pltpu.SemaphoreType.DMA((7 * n,)), pltpu.SemaphoreType.DMA((7 * n,)), pltpu.SemaphoreType.DMA((n,))]


def _gather_phases(x_refs, out_refs, send_sems, recv_sems, local_sems):
    n = len(x_refs)
    chips = [2, 4, 6]

    def copy(a, kk, block, to, src=None):
        slot = out_refs[a].at[_index_of(block)]
        return pltpu.make_async_remote_copy(
            src_ref=slot if src is None else src, dst_ref=slot,
            send_sem=send_sems.at[7 * a + kk], recv_sem=recv_sems.at[7 * a + kk], device_id=to,
            device_id_type=MESH)

    def local(a):
        return pltpu.make_async_copy(x_refs[a], out_refs[a].at[_index_of(_peer(0))], local_sems.at[a])

    def firsts():
        out = []
        for a in range(n):
            out.append(copy(a, 0, _peer(0), _peer(1), src=x_refs[a]))
            out += [copy(a, 1 + j, _peer(0), _peer(d), src=x_refs[a]) for j, d in enumerate(chips)]
        return out

    def passes():
        return [copy(a, 4 + j, _peer(d), _peer(1)) for j, d in enumerate(chips) for a in range(n)]

    def start():
        for a in range(n):
            local(a).start()
        for cp in firsts():
            cp.start()

    def forward():
        for j, d in enumerate(chips):
            for a in range(n):
                copy(a, 1 + j, _peer(d), _peer(0)).wait_recv()
                copy(a, 4 + j, _peer(d), _peer(1)).start()

    def finish():
        for a in range(n):
            copy(a, 0, _peer(1), _peer(0)).wait_recv()
            for j, d in enumerate(chips):
                copy(a, 4 + j, _peer(d | 1), _peer(0)).wait_recv()
        for cp in firsts() + passes():
            cp.wait_send()
        for a in range(n):
            local(a).wait()

    return start, forward, finish


def _scatter_phases(x_refs, out_refs, send_sems, recv_sems, local_sems, layer):
    n = len(x_refs)

    def land(a, dev):
        slot = out_refs[a].at[dev]
        return slot if layer is None else slot.at[layer]

    def local(a):
        my = _index_of(_peer(0))
        return pltpu.make_async_copy(x_refs[a].at[my], land(a, my), local_sems.at[a])

    def copy(a, d):
        my = _index_of(_peer(0))
        return pltpu.make_async_remote_copy(
            src_ref=x_refs[a].at[_index_of(_peer(d))], dst_ref=land(a, my),
            send_sem=send_sems.at[7 * a + d - 1], recv_sem=recv_sems.at[7 * a + d - 1], device_id=_peer(d),
            device_id_type=MESH)

    def arrival(a, d):
        frm = _index_of(_peer(d))
        return pltpu.make_async_remote_copy(
            src_ref=x_refs[a].at[frm], dst_ref=land(a, frm),
            send_sem=send_sems.at[7 * a + d - 1], recv_sem=recv_sems.at[7 * a + d - 1], device_id=_peer(d),
            device_id_type=MESH)

    def start():
        for a in range(n):
            local(a).start()
        for d in range(1, N_DEV):
            for a in range(n):
                copy(a, d).start()

    def finish():
        for d in range(1, N_DEV):
            for a in range(n):
                arrival(a, d).wait_recv()
        for d in range(1, N_DEV):
            for a in range(n):
                copy(a, d).wait_send()
        for a in range(n):
            local(a).wait()

    return start, finish


def _all_to_all(xs, bufs, layer, name):
    n = len(xs)

    def body(*refs):
        start, finish = _scatter_phases(refs[:n], refs[2 * n:3 * n], *refs[3 * n:], layer)
        start()
        finish()

    anyspec = pl.BlockSpec(memory_space=pl.ANY)
    return pl.pallas_call(
        body, in_specs=[anyspec] * (2 * n), out_specs=[anyspec] * n,
        out_shape=[jax.ShapeDtypeStruct(b.shape, b.dtype) for b in bufs],
        input_output_aliases={n + a: a for a in range(n)},
        scratch_shapes=_comm_sems(n), name=name)(*xs, *bufs)


def _adam_math(g, w, m, v):
    c1 = 1.0 - ADAM_B1 ** ADAM_STEP
    c2 = 1.0 - ADAM_B2 ** ADAM_STEP
    mn = ADAM_B1 * m + (1.0 - ADAM_B1) * g
    vn = ADAM_B2 * v + (1.0 - ADAM_B2) * (g * g)
    return -ADAM_LR * ((mn / c1) / (jnp.sqrt(vn / c2) + ADAM_EPS) + ADAM_WD * w), mn, vn


def _reduce_adam(parts, w, m, v, rb, name, row_off=0):
    depth, rows, cols = w.shape
    assert rows % rb == 0 and row_off % rb == 0

    def body(p_ref, w_ref, m_ref, v_ref, g_ref, d_ref, mo_ref, vo_ref):
        g = p_ref[0, 0].astype(F32)
        for j in range(1, N_DEV):
            g = g + p_ref[j, 0].astype(F32)
        g_ref[0] = g
        d_ref[0], mo_ref[0], vo_ref[0] = _adam_math(g, w_ref[0], m_ref[0], v_ref[0])

    blk = pl.BlockSpec((1, rb, cols), lambda l, i: (l, i, 0))
    out = jax.ShapeDtypeStruct(w.shape, F32)
    return pl.pallas_call(
        body, grid=(depth, rows // rb),
        in_specs=[pl.BlockSpec((N_DEV, 1, rb, cols), lambda l, i: (0, l, i + row_off // rb, 0)), blk, blk, blk],
        out_specs=[blk, blk, blk, blk], out_shape=[out, out, out, out],
        name=name, compiler_params=_cp())(parts, w, m, v)


def _reduce_adam_flat(parts, w, m, v, name):
    q_rows = w.shape[0]

    def body(p_ref, w_ref, m_ref, v_ref, g_ref, d_ref, mo_ref, vo_ref):
        g = p_ref[0].astype(F32)
        for j in range(1, N_DEV):
            g = g + p_ref[j].astype(F32)
        g_ref[...] = g
        d_ref[...], mo_ref[...], vo_ref[...] = _adam_math(g, w_ref[...], m_ref[...], v_ref[...])

    blk = pl.BlockSpec((q_rows, LANE), lambda i: (0, 0))
    out = jax.ShapeDtypeStruct((q_rows, LANE), F32)
    return pl.pallas_call(
        body, grid=(1,), in_specs=[pl.BlockSpec((N_DEV, q_rows, LANE), lambda i: (0, 0, 0)), blk, blk, blk],
        out_specs=[blk, blk, blk, blk], out_shape=[out, out, out, out],
        name=name, compiler_params=_cp())(parts, w, m, v)


C128 = (("w_out_pool", 256), ("w_out_mla", 512), ("w_out_conf", 256), ("w_out_sc", 256), ("w_ukv", 128))
C128_ROWS = sum(r for _, r in C128)
TAIL = (("meta_tokens", (N_META, 128)), ("conf_dw_w", (DEPTH, CONF_K, 32)), ("sc_dw_w", (DEPTH, SC_K, 32)))
TAIL_ROWS = sum(int(np.prod(s)) for _, s in TAIL) // LANE
TAIL_PAD = 56
SMALL = (("pre_norm_g", (DEPTH, D)), ("gate_bias", (DEPTH, 4096)), ("pool_w", (DEPTH, 4, 64, 64)),
         ("pool_scale", (DEPTH, 256)), ("q_norm_g", (DEPTH, 256)), ("kv_norm_g", (DEPTH, 128)),
         ("conf_dw_b", (DEPTH, 256)), ("conf_ln_g", (DEPTH, 256)), ("conf_ln_b", (DEPTH, 256)),
         ("post_norm_g", (DEPTH, D)))
SMALL_ROWS = sum(int(np.prod(s)) for _, s in SMALL) // LANE
SMALL_PAD = -(-(SMALL_ROWS + 1) // 8) * 8


def _pack_tail(t):
    parts = [t[n].reshape(-1, LANE) for n, _ in TAIL]
    parts.append(jnp.zeros((TAIL_PAD - TAIL_ROWS, LANE), F32))
    return jnp.concatenate(parts, axis=0)


def _unpack_tail(flat):
    out, off = {}, 0
    for n, s in TAIL:
        rows = int(np.prod(s)) // LANE
        out[n] = flat[off:off + rows].reshape(s)
        off += rows
    return out


def _unpack_tail_full(g):
    out, off = {}, 0
    for n, s in TAIL:
        rows = int(np.prod(s)) // LANE
        blk = jnp.moveaxis(g[:, off:off + rows].reshape((N_DEV,) + s), 0, -2)
        out[n] = blk.reshape(s[:-1] + (N_DEV * s[-1],))
        off += rows
    return out


def _pack_small(t, extra_row):
    parts = [t[n].reshape(-1, LANE) for n, _ in SMALL] + [extra_row]
    parts.append(jnp.zeros((SMALL_PAD - SMALL_ROWS - 1, LANE), F32))
    return jnp.concatenate(parts, axis=0)


def _unpack_small(flat):
    out, off = {}, 0
    for n, s in SMALL:
        rows = int(np.prod(s)) // LANE
        out[n] = flat[off:off + rows].reshape(s)
        off += rows
    return out


def _cols_by_dest(g, width):
    r = g.shape[0]
    return g.reshape(r, N_DEV, width).transpose(1, 0, 2)


def _cols_full(gathered):
    _, r, c = gathered.shape
    return gathered.transpose(1, 0, 2).reshape(r, N_DEV * c)


W_IN_SHARD = 916
PACKED_SEGS = ((3232, 7328), (512, 896), 64, (896, 928), 32, (0, 512), (928, 3232))


def _pack_w_in(g):
    parts = []
    for seg in PACKED_SEGS:
        if isinstance(seg, int):
            parts.append(jnp.zeros((g.shape[1], seg), g.dtype))
            continue
        a, b = seg
        while a < b:
            k = a // W_IN_SHARD
            hi = min(b, W_IN_SHARD * (k + 1))
            parts.append(g[k, :, a - W_IN_SHARD * k:hi - W_IN_SHARD * k])
            a = hi
    return jnp.concatenate(parts, axis=1)


def _w_in_grad_by_dest(gl, mla, mix):
    src = (((0, 512), mix, 0), ((512, 896), mla, 0), ((896, 928), mla, 448), ((928, 3232), mix, 512),
           ((3232, 7328), gl, 0))
    blocks = []
    for k in range(N_DEV):
        lo, hi = W_IN_SHARD * k, W_IN_SHARD * (k + 1)
        parts = []
        for (a, b), arr, off in src:
            s, e = max(a, lo), min(b, hi)
            if s < e:
                parts.append(arr[:, off + s - a:off + e - a])
        blocks.append(jnp.concatenate(parts, axis=1))
    return jnp.stack(blocks)


def _rope_tables(lp):
    inv = 1.0 / (ROPE_THETA ** (jnp.arange(0, QK_ROPE, 2, dtype=F32) / QK_ROPE))
    ang = jnp.arange(lp, dtype=F32)[:, None] * inv[None, :]
    cos, sin = jnp.cos(ang), jnp.sin(ang)
    one = jnp.ones((lp, QK_NOPE), F32)
    zero = jnp.zeros((lp, QK_NOPE), F32)
    z16 = jnp.zeros((lp, 16), F32)
    c = jnp.concatenate([one, cos, cos, jnp.ones((lp, 32), F32)], axis=1)
    s1 = jnp.concatenate([zero, z16, sin, jnp.zeros((lp, 32), F32)], axis=1)
    s2 = jnp.concatenate([zero, -sin, z16, jnp.zeros((lp, 32), F32)], axis=1)
    return c, s1, s2


def kernel(x, meta_tokens, pre_norm_g, w_in, gate_bias, pool_w, pool_scale, w_out_pool, q_norm_g, w_uq, kv_norm_g, w_ukv, w_out_mla, conf_dw_w, conf_dw_b, conf_ln_g, conf_ln_b, w_out_conf, sc_dw_w, w_out_sc, w_o, post_norm_g, loss_target, m_meta_tokens, m_pre_norm_g, m_w_in, m_gate_bias, m_pool_w, m_pool_scale, m_w_out_pool, m_q_norm_g, m_w_uq, m_kv_norm_g, m_w_ukv, m_w_out_mla, m_conf_dw_w, m_conf_dw_b, m_conf_ln_g, m_conf_ln_b, m_w_out_conf, m_sc_dw_w, m_w_out_sc, m_w_o, m_post_norm_g, v_meta_tokens, v_pre_norm_g, v_w_in, v_gate_bias, v_pool_w, v_pool_scale, v_w_out_pool, v_q_norm_g, v_w_uq, v_kv_norm_g, v_w_ukv, v_w_out_mla, v_conf_dw_w, v_conf_dw_b, v_conf_ln_g, v_conf_ln_b, v_w_out_conf, v_sc_dw_w, v_w_out_sc, v_w_o, v_post_norm_g):
    names = ["w_in", "w_uq", "w_o"] + [n for n, _ in C128] + [n for n, _ in TAIL] + [n for n, _ in SMALL]
    loc = locals()
    w = {n: loc[n] for n in names}
    mom = {n: loc["m_" + n] for n in names}
    vel = {n: loc["v_" + n] for n in names}

    seq = x.shape[1]
    n_real = N_META + seq
    lp = -(-n_real // RB) * RB
    tmb = lp // 3
    tabs = _rope_tables(lp)

    c128 = jnp.concatenate([w[n] for n, _ in C128], axis=1)
    def shards_of(i):
        return [w_in[i].astype(BF16), c128[i].astype(BF16), w_uq[i].astype(BF16), w_o[i].astype(BF16)]

    gathered = [_all_gather(shards_of(0), "gather_weights")] + [None] * (DEPTH - 1)
    tail_w = _pack_tail(w)
    tail = _unpack_tail_full(_all_gather([tail_w], "gather_tail")[0])
    eye4 = jnp.eye(4, dtype=F32)
    bd_all = (pool_w[:, :, :, None, :] * eye4[None, :, None, :, None]).reshape(DEPTH, 256, 256).astype(BF16)

    def layer_weights(i):
        g_in, g_c128, g_uq, g_o = gathered[i]
        lw = {}
        lw["w_in"] = _pack_w_in(g_in)
        lw["wc"] = _cols_full(g_c128)
        wuq = _cols_full(g_uq).reshape(Q_RANK, HEADS, 96)
        lw["w_uq"] = jnp.pad(wuq, ((0, 0), (0, 0), (0, 32))).reshape(Q_RANK, HEADS * LANE)
        wukv = lw["wc"][U_OFF[4]:].reshape(KV_RANK, HEADS, 128)
        wk = jnp.pad(wukv[:, :, :QK_NOPE], ((0, 0), (0, 0), (0, 64))).reshape(KV_RANK, HEADS * LANE)
        lw["w_ukv"] = jnp.concatenate([wk, wukv[:, :, QK_NOPE:].reshape(KV_RANK, HEADS * V_DIM)], axis=1)
        lw["w_o"] = g_o.reshape(D, D)
        lw["bd"] = bd_all[i]
        lw["cw"] = jnp.pad(tail["conf_dw_w"][i], ((0, 1), (0, 0)))
        lw["sw"] = jnp.pad(tail["sc_dw_w"][i], ((0, 8 - SC_K), (0, 0)))
        return lw

    meta_full = tail["meta_tokens"]

    pad_rows = lp - n_real
    xr = jnp.concatenate([meta_full, x[0], jnp.zeros((pad_rows, D), F32)], axis=0)
    tgt = jnp.pad(loss_target[0], ((N_META, pad_rows), (0, 0)))
    saved = []
    for i in range(DEPTH):
        lw = layer_weights(i)
        h = _rms_fwd(xr, pre_norm_g[i:i + 1], lp)
        z = _mm(h, lw["w_in"], lp, ZW, D, tm=tmb, tn=256, tk=D, name="mm_in")
        qn, kvn, krr = _mla_prep(z, q_norm_g[i:i + 1], kv_norm_g[i:i + 1], tabs, lp)
        q_raw = _mm(qn, lw["w_uq"], lp, 1024, Q_RANK, tm=tmb, tn=1024, tk=Q_RANK, name="mm_uq")
        kv_raw = _mm(kvn, lw["w_ukv"], lp, 1536, KV_RANK, tm=tmb, tn=512, tk=KV_RANK, name="mm_ukv")
        qt, kt, vt = _mla_post(q_raw, kv_raw, krr, tabs, lp)
        res = _attn_fwd(qt, kt, vt, lp, gather=shards_of(i + 1) if i + 1 < DEPTH else ())
        oat, lse = res[0], res[1]
        if i + 1 < DEPTH:
            gathered[i + 1] = res[2:]
        u = _mix_fwd(z, oat, lw["bd"], pool_scale[i:i + 1], lw["cw"], conf_dw_b[i:i + 1], conf_ln_g[i:i + 1],
                     conf_ln_b[i:i + 1], lw["sw"], lp)
        x_new, m_act, o2 = _merge_fwd(xr, u, z, gate_bias[i:i + 1], lw["wc"], lw["w_o"],
                                      post_norm_g[i:i + 1], lp)
        saved.append(dict(lw=lw, x=xr, h=h, z=z, qn=qn, kvn=kvn, qt=qt, kt=kt, vt=vt, oat=oat, lse=lse,
                          u=u, m=m_act, o2=o2))
        xr = x_new

    dx, loss_part = _loss_head(xr, tgt, n_real, lp)

    gsm = {n: [None] * DEPTH for n, _ in SMALL}
    g_cw = [None] * DEPTH
    g_sw = [None] * DEPTH
    recv = [lax.empty((N_DEV, DEPTH) + s, BF16) for s in ((D, 916), (C128_ROWS, 128), (Q_RANK, 96), (128, D))]
    pending = None
    for i in reversed(range(DEPTH)):
        s = saved[i]
        lw = s["lw"]
        do2, dgl, dyb, du, dgb, dgpost = _merge_bwd(dx, s["o2"], s["u"], s["z"], gate_bias[i:i + 1],
                                                    lw["wc"], lw["w_o"], post_norm_g[i:i + 1], lp)
        d_wo = _mm(s["m"], do2, D, D, lp, ta=True, tm=512, tn=D, tk=tmb, out_dtype=BF16, name="mm_dwo")
        d_wout = []
        for b in range(4):
            rows = U_OFF[b + 1] - U_OFF[b]
            d_wout.append(_mm(s["u"], dyb, rows, D, lp, ta=True, tm=256, tn=D, tk=tmb, out_dtype=BF16,
                              a_moff=U_OFF[b] // 256, b_noff=b, name="mm_dwout%d" % b))
        dzx, doat, dbd, dcw, dsw, dsm = _mix_bwd(
            s["z"], s["oat"], du, lw["bd"], pool_scale[i:i + 1], lw["cw"], conf_dw_b[i:i + 1],
            conf_ln_g[i:i + 1], conf_ln_b[i:i + 1], lw["sw"], lp)
        res = _attn_bwd(s["qt"], s["kt"], s["vt"], s["oat"], doat, s["lse"], lp,
                        scatter=(pending, recv, i + 1) if pending else None)
        dqt, dkt, dvt = res[:3]
        if pending:
            recv = list(res[3:])
        dq_raw, dkv_raw, dkr = _mla_post_bwd(dqt, dkt, dvt, tabs, lp)
        dqn = _mm(dq_raw, lw["w_uq"], lp, Q_RANK, 1024, tb=True, tm=tmb, tn=Q_RANK, tk=1024, name="mm_dqn")
        d_wuq = _mm(s["qn"], dq_raw, Q_RANK, 1024, lp, ta=True, tm=Q_RANK, tn=1024, tk=tmb, out_dtype=BF16,
                    name="mm_dwuq")
        dkvn = _mm(dkv_raw, lw["w_ukv"], lp, KV_RANK, 1536, tb=True, tm=tmb, tn=KV_RANK, tk=1536, name="mm_dkvn")
        d_wukv = _mm(s["kvn"], dkv_raw, KV_RANK, 1536, lp, ta=True, tm=KV_RANK, tn=1536, tk=tmb, out_dtype=BF16,
                     name="mm_dwukv")
        dzq, dqg, dkvg = _mla_prep_bwd(s["z"], q_norm_g[i:i + 1], kv_norm_g[i:i + 1], dqn, dkvn, dkr, lp)
        d_win = [_mm(s["h"], seg, D, seg.shape[1], lp, ta=True, tm=D, tn=256, tk=tmb, out_dtype=BF16,
                     name="mm_dwin%d" % k) for k, seg in enumerate((dgl, dzq, dzx))]
        d_wuq_o = d_wuq.reshape(Q_RANK, HEADS, LANE)[:, :, :96].reshape(Q_RANK, HEADS * 96)
        d_wukv_o = jnp.concatenate([d_wukv[:, :1024].reshape(KV_RANK, HEADS, LANE)[:, :, :QK_NOPE],
                                    d_wukv[:, 1024:].reshape(KV_RANK, HEADS, V_DIM)], axis=2).reshape(KV_RANK, 1024)
        pending = [
            _w_in_grad_by_dest(*d_win),
            _cols_by_dest(jnp.concatenate(d_wout + [d_wukv_o], axis=0), 128),
            _cols_by_dest(d_wuq_o, 96),
            d_wo.reshape(N_DEV, 128, D)]
        res = _dh(dgl, dzq, dzx, lw["w_in"], lp, scatter=(pending, recv, 0) if i == 0 else None)
        dh = res[0]
        if i == 0:
            recv = list(res[1:])
        dx, dgpre = _rms_bwd(s["x"], pre_norm_g[i:i + 1], dh, dx, lp)

        gsm["pre_norm_g"][i] = dgpre[0]
        gsm["gate_bias"][i] = dgb[0]
        gsm["pool_w"][i] = jnp.stack([dbd[64 * g:64 * (g + 1), 64 * g:64 * (g + 1)] for g in range(4)])
        gsm["pool_scale"][i] = dsm[0]
        gsm["conf_dw_b"][i] = dsm[1]
        gsm["conf_ln_g"][i] = dsm[2]
        gsm["conf_ln_b"][i] = dsm[3]
        gsm["q_norm_g"][i] = dqg[0]
        gsm["kv_norm_g"][i] = dkvg[0]
        gsm["post_norm_g"][i] = dgpost[0]
        g_cw[i] = dcw[:CONF_K]
        g_sw[i] = dsw[:SC_K]

    outs = [dict() for _ in range(4)]

    def put(n, res):
        for t, r in zip(outs, res):
            t[n] = r

    put("w_in", _reduce_adam(recv[0], w["w_in"], mom["w_in"], vel["w_in"], 256, "adam_w_in"))
    off = 0
    for n, rows in C128:
        put(n, _reduce_adam(recv[1], w[n], mom[n], vel[n], 128, "adam_" + n, row_off=off))
        off += rows
    put("w_uq", _reduce_adam(recv[2], w["w_uq"], mom["w_uq"], vel["w_uq"], Q_RANK, "adam_w_uq"))
    put("w_o", _reduce_adam(recv[3], w["w_o"], mom["w_o"], vel["w_o"], 128, "adam_w_o"))

    tail_g = {"meta_tokens": _cols_by_dest(dx[:N_META], 128),
              "conf_dw_w": jnp.moveaxis(jnp.stack(g_cw).reshape(DEPTH, CONF_K, N_DEV, 32), 2, 0),
              "sc_dw_w": jnp.moveaxis(jnp.stack(g_sw).reshape(DEPTH, SC_K, N_DEV, 32), 2, 0)}
    tail_bd = jnp.concatenate([tail_g[n].reshape(N_DEV, -1, LANE) for n, _ in TAIL]
                              + [jnp.zeros((N_DEV, TAIL_PAD - TAIL_ROWS, LANE), F32)], axis=1)
    tail_recv = _all_to_all([tail_bd], [lax.empty((N_DEV, TAIL_PAD, LANE), F32)], None, "scatter_tail")[0]
    tail_res = _reduce_adam_flat(tail_recv, tail_w, _pack_tail(mom), _pack_tail(vel), "adam_tail")

    small_g = {n: jnp.stack(gsm[n]) for n, _ in SMALL}
    loss_row = jnp.concatenate([loss_part[0:1, 0:1], jnp.zeros((1, LANE - 1), F32)], axis=1)
    zrow = jnp.zeros((1, LANE), F32)
    parts = _all_gather([_pack_small(small_g, loss_row)], "gather_small_grads")[0]
    small_res = _reduce_adam_flat(parts, _pack_small(w, zrow), _pack_small(mom, zrow), _pack_small(vel, zrow),
                                  "adam_small")
    loss = small_res[0][SMALL_ROWS, 0]
    for t, tf, sf in zip(outs, tail_res, small_res):
        t.update(_unpack_tail(tf))
        t.update(_unpack_small(sf))
    order = ["meta_tokens", "pre_norm_g", "w_in", "gate_bias", "pool_w", "pool_scale", "w_out_pool", "q_norm_g",
             "w_uq", "kv_norm_g", "w_ukv", "w_out_mla", "conf_dw_w", "conf_dw_b", "conf_ln_g", "conf_ln_b",
             "w_out_conf", "sc_dw_w", "w_out_sc", "w_o", "post_norm_g"]
    grad_x = dx[N_META:n_real][None]
    return (loss, grad_x, *[t[n] for t in outs for n in order])
```

```python
import functools

import jax
import jax.numpy as jnp
import numpy as np
from jax import lax
from jax.experimental import pallas as pl
from jax.experimental.pallas import tpu as pltpu

F32 = jnp.float32
BF16 = jnp.bfloat16

D = 1024
N_META = 16
DEPTH = 4
EPS = 1e-6
HEADS = 8
QK_NOPE = 64
QK_ROPE = 32
V_DIM = 64
Q_RANK = 256
KV_RANK = 128
ROPE_THETA = 10000.0
SCALE = (QK_NOPE + QK_ROPE) ** -0.5
CONF_K = 31
SC_K = 3
N_DEV = 8

ADAM_LR = 0.001
ADAM_B1 = 0.9
ADAM_B2 = 0.999
ADAM_EPS = 1e-08
ADAM_WD = 0.01
ADAM_STEP = 10

RB = 384
HB = 32
LANE = 128
VMEM_LIMIT = 56 * 1024 * 1024

GL0, CQ0, CKV0, KR0, PV0, PG0, MG0, CU0, CG0, SBB0, SBC0, SBX0, SG0, ZW = (
    0, 4096, 4352, 4480, 4608, 4864, 5120, 5632, 6144, 6400, 6656, 6912, 7168, 7424)
ZSEG = ((0, 4096), (4096, 512), (4608, 2816))
LOG2E = 1.4426950408889634
LN2 = 0.6931471805599453

MESH = pl.DeviceIdType.MESH


def _cp(**kw):
    return pltpu.CompilerParams(vmem_limit_bytes=VMEM_LIMIT, **kw)


def _sig(x):
    return jax.nn.sigmoid(x)


def _silu(x):
    return x * _sig(x)


def _dsilu(x):
    s = _sig(x)
    return s * (1.0 + x * (1.0 - s))


def _dn(x, k):
    return x if k == 0 else pltpu.roll(x, k, 0)


def _up(x, k):
    return x if k == 0 else pltpu.roll(x, x.shape[0] - k, 0)


def _rope(t, c, s1, s2):
    return t * c + pltpu.roll(t, 16, 1) * s1 + pltpu.roll(t, LANE - 16, 1) * s2


def _rope_t(g, c, s1, s2):
    return g * c + pltpu.roll(g * s1, LANE - 16, 1) + pltpu.roll(g * s2, 16, 1)


def _mm(a, b, m, n, k, *, ta=False, tb=False, out_dtype=F32, tm, tn, tk, name,
        a_moff=0, a_koff=0, b_noff=0, b_koff=0, c=None):
    assert m % tm == 0 and n % tn == 0 and k % tk == 0, (name, m, n, k, tm, tn, tk)
    nk = k // tk
    dims = (((0,) if ta else (1,), (1,) if tb else (0,)), ((), ()))
    has_c = c is not None

    def body(a_ref, b_ref, *rest):
        c_ref = rest[0] if has_c else None
        o_ref = rest[1] if has_c else rest[0]
        scr = rest[2:] if has_c else rest[1:]
        part = lax.dot_general(a_ref[...].astype(BF16), b_ref[...].astype(BF16), dims,
                               preferred_element_type=F32)

        def finish(total):
            if has_c:
                total = total + c_ref[...]
            o_ref[...] = total.astype(out_dtype)

        if nk == 1:
            finish(part)
        else:
            acc = scr[0]
            kk = pl.program_id(2)

            @pl.when(kk == 0)
            def _():
                acc[...] = part

            @pl.when(kk > 0)
            def _():
                acc[...] += part

            @pl.when(kk == nk - 1)
            def _():
                finish(acc[...])

    if ta:
        a_spec = pl.BlockSpec((tk, tm), lambda i, j, q: (q + a_koff, i + a_moff))
    else:
        a_spec = pl.BlockSpec((tm, tk), lambda i, j, q: (i + a_moff, q + a_koff))
    if tb:
        b_spec = pl.BlockSpec((tn, tk), lambda i, j, q: (j + b_noff, q + b_koff))
    else:
        b_spec = pl.BlockSpec((tk, tn), lambda i, j, q: (q + b_koff, j + b_noff))
    o_spec = pl.BlockSpec((tm, tn), lambda i, j, q: (i, j))
    return pl.pallas_call(
        body, grid=(m // tm, n // tn, nk), in_specs=[a_spec, b_spec] + ([o_spec] if has_c else []),
        out_specs=o_spec, out_shape=jax.ShapeDtypeStruct((m, n), out_dtype),
        scratch_shapes=[pltpu.VMEM((tm, tn), F32)] if nk > 1 else [],
        name=name, compiler_params=_cp())(*((a, b, c) if has_c else (a, b)))


def _rms_fwd(x, g, lp):
    def body(x_ref, g_ref, h_ref):
        xv = x_ref[...]
        r = lax.rsqrt(jnp.mean(xv * xv, axis=-1, keepdims=True) + EPS)
        h_ref[...] = (xv * r * g_ref[...]).astype(BF16)

    return pl.pallas_call(
        body, grid=(lp // RB,),
        in_specs=[pl.BlockSpec((RB, D), lambda i: (i, 0)), pl.BlockSpec((1, D), lambda i: (0, 0))],
        out_specs=pl.BlockSpec((RB, D), lambda i: (i, 0)),
        out_shape=jax.ShapeDtypeStruct((lp, D), BF16), name="rms_fwd", compiler_params=_cp())(x, g)


def _rms_bwd(x, g, dh, dx_in, lp):
    def body(x_ref, g_ref, dh_ref, dxi_ref, dx_ref, dg_ref):
        i = pl.program_id(0)
        xv = x_ref[...]
        r = lax.rsqrt(jnp.mean(xv * xv, axis=-1, keepdims=True) + EPS)
        dy = dh_ref[...]
        a = dy * g_ref[...]
        dx_ref[...] = dxi_ref[...] + r * a - xv * (r * r * r) * jnp.mean(a * xv, axis=-1, keepdims=True)
        part = jnp.sum(dy * xv * r, axis=0, keepdims=True)

        @pl.when(i == 0)
        def _():
            dg_ref[...] = part

        @pl.when(i > 0)
        def _():
            dg_ref[...] += part

    blk = pl.BlockSpec((RB, D), lambda i: (i, 0))
    vec = pl.BlockSpec((1, D), lambda i: (0, 0))
    return pl.pallas_call(
        body, grid=(lp // RB,), in_specs=[blk, vec, blk, blk], out_specs=[blk, vec],
        out_shape=[jax.ShapeDtypeStruct((lp, D), F32), jax.ShapeDtypeStruct((1, D), F32)],
        name="rms_bwd", compiler_params=_cp())(x, g, dh, dx_in)


def _mla_prep(z, qg, kvg, tabs, lp):
    def body(z_ref, qg_ref, kvg_ref, c_ref, s1_ref, s2_ref, qn_ref, kvn_ref, kr_ref):
        cq = z_ref[:, 0:256]
        ckv = z_ref[:, 256:384]
        kr = z_ref[:, 384:512]
        rq = lax.rsqrt(jnp.mean(cq * cq, axis=-1, keepdims=True) + EPS)
        rk = lax.rsqrt(jnp.mean(ckv * ckv, axis=-1, keepdims=True) + EPS)
        qn_ref[...] = (cq * rq * qg_ref[...]).astype(BF16)
        kvn_ref[...] = (ckv * rk * kvg_ref[...]).astype(BF16)
        kr_ref[...] = _rope(kr, c_ref[...], s1_ref[...], s2_ref[...])

    tab = pl.BlockSpec((RB, LANE), lambda i: (i, 0))
    return pl.pallas_call(
        body, grid=(lp // RB,),
        in_specs=[pl.BlockSpec((RB, 512), lambda i: (i, CQ0 // 512)),
                  pl.BlockSpec((1, 256), lambda i: (0, 0)), pl.BlockSpec((1, 128), lambda i: (0, 0)),
                  tab, tab, tab],
        out_specs=[pl.BlockSpec((RB, 256), lambda i: (i, 0)), tab, tab],
        out_shape=[jax.ShapeDtypeStruct((lp, 256), BF16), jax.ShapeDtypeStruct((lp, 128), BF16),
                   jax.ShapeDtypeStruct((lp, 128), F32)],
        name="mla_prep", compiler_params=_cp())(z, qg, kvg, *tabs)


def _mla_prep_bwd(z, qg, kvg, dqn, dkvn, dkr, lp):
    def body(z_ref, qg_ref, kvg_ref, dqn_ref, dkvn_ref, dkr_ref, dz_ref, dqg_ref, dkvg_ref):
        i = pl.program_id(0)

        def rms_b(xv, g, dy):
            r = lax.rsqrt(jnp.mean(xv * xv, axis=-1, keepdims=True) + EPS)
            a = dy * g
            dx = r * a - xv * (r * r * r) * jnp.mean(a * xv, axis=-1, keepdims=True)
            return dx, jnp.sum(dy * xv * r, axis=0, keepdims=True)

        dcq, pq = rms_b(z_ref[:, 0:256], qg_ref[...], dqn_ref[...])
        dckv, pk = rms_b(z_ref[:, 256:384], kvg_ref[...], dkvn_ref[...])
        dz_ref[:, 0:256] = dcq.astype(BF16)
        dz_ref[:, 256:384] = dckv.astype(BF16)
        dz_ref[:, 384:512] = dkr_ref[...].astype(BF16)

        @pl.when(i == 0)
        def _():
            dqg_ref[...] = pq
            dkvg_ref[...] = pk

        @pl.when(i > 0)
        def _():
            dqg_ref[...] += pq
            dkvg_ref[...] += pk

    tab = pl.BlockSpec((RB, LANE), lambda i: (i, 0))
    return pl.pallas_call(
        body, grid=(lp // RB,),
        in_specs=[pl.BlockSpec((RB, 512), lambda i: (i, CQ0 // 512)),
                  pl.BlockSpec((1, 256), lambda i: (0, 0)), pl.BlockSpec((1, 128), lambda i: (0, 0)),
                  pl.BlockSpec((RB, 256), lambda i: (i, 0)), tab, tab],
        out_specs=[pl.BlockSpec((RB, 512), lambda i: (i, 0)),
                   pl.BlockSpec((1, 256), lambda i: (0, 0)), pl.BlockSpec((1, 128), lambda i: (0, 0))],
        out_shape=[jax.ShapeDtypeStruct((lp, 512), BF16), jax.ShapeDtypeStruct((1, 256), F32),
                   jax.ShapeDtypeStruct((1, 128), F32)],
        name="mla_prep_bwd", compiler_params=_cp())(z, qg, kvg, dqn, dkvn, dkr)


def _mla_post(q_raw, kv_raw, krr, tabs, lp):
    def body(q_ref, kv_ref, kr_ref, c_ref, s1_ref, s2_ref, qo_ref, ko_ref, vo_ref):
        c, s1, s2, kr = c_ref[...], s1_ref[...], s2_ref[...], kr_ref[...]
        for h in range(HEADS):
            sl = slice(LANE * h, LANE * (h + 1))
            qo_ref[:, sl] = (_rope(q_ref[:, sl], c, s1, s2) * (SCALE * LOG2E)).astype(BF16)
            ko_ref[:, sl] = (kv_ref[:, sl] + kr).astype(BF16)
        vo_ref[...] = kv_ref[:, 1024:1536].astype(BF16)

    tab = pl.BlockSpec((RB, LANE), lambda i: (i, 0))
    wide = pl.BlockSpec((RB, 1024), lambda i: (i, 0))
    return pl.pallas_call(
        body, grid=(lp // RB,),
        in_specs=[wide, pl.BlockSpec((RB, 1536), lambda i: (i, 0)), tab, tab, tab, tab],
        out_specs=[wide, wide, pl.BlockSpec((RB, 512), lambda i: (i, 0))],
        out_shape=[jax.ShapeDtypeStruct((lp, 1024), BF16), jax.ShapeDtypeStruct((lp, 1024), BF16),
                   jax.ShapeDtypeStruct((lp, 512), BF16)],
        name="mla_post", compiler_params=_cp())(q_raw, kv_raw, krr, *tabs)


def _mla_post_bwd(dq, dk, dv, tabs, lp):
    def body(dq_ref, dk_ref, dv_ref, c_ref, s1_ref, s2_ref, dqr_ref, dkv_ref, dkr_ref):
        c, s1, s2 = c_ref[...], s1_ref[...], s2_ref[...]
        lane = lax.broadcasted_iota(jnp.int32, (1, LANE), 1)
        ropel = (lane >= QK_NOPE) & (lane < QK_NOPE + QK_ROPE)
        ksum = jnp.zeros((RB, LANE), F32)
        for h in range(HEADS):
            sl = slice(LANE * h, LANE * (h + 1))
            dqr_ref[:, sl] = _rope_t(dq_ref[:, sl].astype(F32) * SCALE, c, s1, s2).astype(BF16)
            dkt = dk_ref[:, sl]
            dkv_ref[:, sl] = dkt
            ksum = ksum + dkt.astype(F32)
        dkv_ref[:, 1024:1536] = dv_ref[...]
        dkr_ref[...] = jnp.where(ropel, _rope_t(jnp.where(ropel, ksum, 0.0), c, s1, s2), 0.0)

    tab = pl.BlockSpec((RB, LANE), lambda i: (i, 0))
    wide = pl.BlockSpec((RB, 1024), lambda i: (i, 0))
    return pl.pallas_call(
        body, grid=(lp // RB,),
        in_specs=[wide, wide, pl.BlockSpec((RB, 512), lambda i: (i, 0)), tab, tab, tab],
        out_specs=[wide, pl.BlockSpec((RB, 1536), lambda i: (i, 0)), tab],
        out_shape=[jax.ShapeDtypeStruct((lp, 1024), BF16), jax.ShapeDtypeStruct((lp, 1536), BF16),
                   jax.ShapeDtypeStruct((lp, 128), F32)],
        name="mla_post_bwd", compiler_params=_cp())(dq, dk, dv, *tabs)


def _head_lanes(e):
    lane = lax.broadcasted_iota(jnp.int32, (1, LANE), 1)
    return lane >= V_DIM if e else lane < V_DIM


ONE_LANE = (V_DIM, 0)


def _attn_fwd(q, k, v, lp, gather=()):
    nq = lp // RB
    n = len(gather)
    steps = HEADS // 2

    def body(q_ref, k_ref, v_ref, *rest):
        o_ref, lse_ref = rest[n], rest[n + 1]
        vm_scr = rest[2 * n + 2]
        if n:
            g_start, g_forward, g_finish = _gather_phases(rest[:n], rest[n + 2:2 * n + 2], *rest[2 * n + 3:])
            pl.when(pl.program_id(0) == 0)(g_start)
            pl.when(pl.program_id(0) == steps - 1)(g_forward)
        lane = lax.broadcasted_iota(jnp.int32, (1, LANE), 1)
        vv = v_ref[...]
        for e in range(2):
            ones = jnp.where(lane == ONE_LANE[e], 1.0, 0.0).astype(BF16)
            vm_scr[e] = jnp.where(_head_lanes(e), vv, jnp.broadcast_to(ones, vv.shape))
        causal = (lax.broadcasted_iota(jnp.int32, (RB, RB), 1) <= lax.broadcasted_iota(jnp.int32, (RB, RB), 0))

        def qblock(i, _):
            rows = pl.ds(pl.multiple_of(i * RB, RB), RB)
            qs = [q_ref[rows, LANE * e:LANE * (e + 1)] for e in range(2)]

            def scores(j):
                cols = pl.ds(pl.multiple_of(j * RB, RB), RB)
                return tuple(lax.dot_general(qs[e], k_ref[cols, LANE * e:LANE * (e + 1)], (((1,), (1,)), ((), ())),
                                             preferred_element_type=F32) for e in range(2))

            def update(j, s, carry, masked):
                cols = pl.ds(pl.multiple_of(j * RB, RB), RB)
                out = []
                for e in range(2):
                    m, acc = carry[2 * e], carry[2 * e + 1]
                    se = jnp.where(causal, s[e], -jnp.inf) if masked else s[e]
                    m_new = jnp.maximum(m, jnp.max(se, axis=-1, keepdims=True))
                    p = jnp.exp2((se - m_new).astype(BF16))
                    acc = jnp.exp2(m - m_new) * acc + jnp.dot(p, vm_scr[e, cols, :], preferred_element_type=F32)
                    out += [m_new, acc]
                return tuple(out)

            def pair(jj, c):
                sa, sb = scores(2 * jj), scores(2 * jj + 1)
                return update(2 * jj + 1, sb, update(2 * jj, sa, c, False), False)

            m0 = jnp.full((RB, 1), -jnp.inf, F32)
            a0 = jnp.zeros((RB, LANE), F32)
            carry = lax.fori_loop(0, i // 2, pair, (m0, a0, m0, a0))
            carry = lax.cond(i % 2 == 1, lambda c: update(i - 1, scores(i - 1), c, False), lambda c: c, carry)
            carry = update(i, scores(i), carry, True)
            o, lse = [], []
            for e in range(2):
                m, acc = carry[2 * e], carry[2 * e + 1]
                l = acc[:, ONE_LANE[e]:ONE_LANE[e] + 1]
                o.append(acc / l)
                lse.append(jnp.broadcast_to(m + jnp.log2(l), (RB, LANE)))
            o_ref[rows, :] = jnp.where(_head_lanes(0), o[0], o[1])
            lse_ref[rows, :] = jnp.where(_head_lanes(0), lse[0], lse[1])
            return 0

        lax.fori_loop(0, nq, qblock, 0)
        if n:
            pl.when(pl.program_id(0) == steps - 1)(g_finish)

    two = pl.BlockSpec((lp, 2 * LANE), lambda h: (0, h))
    one = pl.BlockSpec((lp, LANE), lambda h: (0, h))
    anyspec = pl.BlockSpec(memory_space=pl.ANY)
    return pl.pallas_call(
        body, grid=(steps,), in_specs=[two, two, one] + [anyspec] * n, out_specs=[one, one] + [anyspec] * n,
        out_shape=[jax.ShapeDtypeStruct((lp, 512), F32), jax.ShapeDtypeStruct((lp, 512), F32)]
        + _gather_shapes(gather),
        scratch_shapes=[pltpu.VMEM((2, lp, LANE), BF16)] + (_comm_sems(n) if n else []),
        name="attn_fwd_gather" if n else "attn_fwd", compiler_params=_cp())(q, k, v, *gather)


def _attn_bwd(q, k, v, o, do, lse, lp, scatter=None):
    nq = lp // RB
    xs, bufs, layer = scatter if scatter else ((), (), None)
    n = len(xs)
    steps = HEADS // 2

    def body(q_ref, k_ref, v_ref, o_ref, do_ref, lse_ref, *rest):
        dq_ref, dk_ref, dv_ref = rest[2 * n:2 * n + 3]
        vm_scr, dom_scr, dl_scr, dq_scr = rest[3 * n + 3:3 * n + 7]
        if n:
            s_start, s_finish = _scatter_phases(rest[:n], rest[2 * n + 3:3 * n + 3], *rest[3 * n + 7:], layer)
            pl.when(pl.program_id(0) == 0)(s_start)
        causal = (lax.broadcasted_iota(jnp.int32, (RB, RB), 1) <= lax.broadcasted_iota(jnp.int32, (RB, RB), 0))
        vv = v_ref[...]
        for e in range(2):
            vm_scr[e] = jnp.where(_head_lanes(e), vv, jnp.zeros_like(vv))

        def prep(i, _):
            rows = pl.ds(pl.multiple_of(i * RB, RB), RB)
            prod = do_ref[rows, :] * o_ref[rows, :]
            dls = []
            for e in range(2):
                hm = _head_lanes(e)
                dom_scr[e, rows, :] = jnp.where(hm, do_ref[rows, :], 0.0).astype(BF16)
                dls.append(jnp.sum(jnp.where(hm, prod, 0.0), axis=-1, keepdims=True))
            dl_scr[rows, :] = jnp.where(_head_lanes(0), dls[0], dls[1])
            dq_scr[rows, :] = jnp.zeros((RB, 2 * LANE), F32)
            return 0

        lax.fori_loop(0, nq, prep, 0)

        def kvblock(j, _):
            cols = pl.ds(pl.multiple_of(j * RB, RB), RB)
            kbs = [k_ref[cols, LANE * e:LANE * (e + 1)] for e in range(2)]

            def products(i):
                rows = pl.ds(pl.multiple_of(i * RB, RB), RB)
                out = []
                for e in range(2):
                    out.append(lax.dot_general(q_ref[rows, LANE * e:LANE * (e + 1)], kbs[e],
                                               (((1,), (1,)), ((), ())), preferred_element_type=F32))
                    out.append(lax.dot_general(dom_scr[e, rows, :], vm_scr[e, cols, :],
                                               (((1,), (1,)), ((), ())), preferred_element_type=F32))
                return tuple(out)

            def update(i, sd, carry, masked):
                rows = pl.ds(pl.multiple_of(i * RB, RB), RB)
                out = []
                for e in range(2):
                    dk, dv = carry[2 * e], carry[2 * e + 1]
                    sl = slice(LANE * e, LANE * (e + 1))
                    col1 = slice(V_DIM * e, V_DIM * e + 1)
                    s, dp = sd[2 * e], sd[2 * e + 1]
                    if masked:
                        s = jnp.where(causal, s, -jnp.inf)
                    p = jnp.exp2((s - lse_ref[rows, col1]).astype(BF16))
                    dv = dv + lax.dot_general(p, dom_scr[e, rows, :], (((0,), (0,)), ((), ())),
                                              preferred_element_type=F32)
                    ds = p * (dp - dl_scr[rows, col1]).astype(BF16)
                    dk = dk + lax.dot_general(ds, q_ref[rows, sl], (((0,), (0,)), ((), ())),
                                              preferred_element_type=F32)
                    dq_scr[rows, sl] += jnp.dot(ds, kbs[e], preferred_element_type=F32)
                    out += [dk, dv]
                return tuple(out)

            def pair(t, c):
                i0 = j + 1 + 2 * t
                pa, pb = products(i0), products(i0 + 1)
                return update(i0 + 1, pb, update(i0, pa, c, False), False)

            zero = jnp.zeros((RB, LANE), F32)
            carry = update(j, products(j), (zero, zero, zero, zero), True)
            below = nq - 1 - j
            carry = lax.fori_loop(0, below // 2, pair, carry)
            dk0, dv0, dk1, dv1 = lax.cond(below % 2 == 1,
                                          lambda c: update(nq - 1, products(nq - 1), c, False), lambda c: c, carry)
            dk_ref[cols, 0:LANE] = (dk0 * LN2).astype(BF16)
            dk_ref[cols, LANE:2 * LANE] = (dk1 * LN2).astype(BF16)
            dv_ref[cols, :] = (dv0 + dv1).astype(BF16)
            return 0

        lax.fori_loop(0, nq, kvblock, 0)

        def fin(i, _):
            rows = pl.ds(pl.multiple_of(i * RB, RB), RB)
            dq_ref[rows, :] = dq_scr[rows, :].astype(BF16)
            return 0

        lax.fori_loop(0, nq, fin, 0)
        if n:
            pl.when(pl.program_id(0) == steps - 1)(s_finish)

    two = pl.BlockSpec((lp, 2 * LANE), lambda h: (0, h))
    one = pl.BlockSpec((lp, LANE), lambda h: (0, h))
    anyspec = pl.BlockSpec(memory_space=pl.ANY)
    return pl.pallas_call(
        body, grid=(steps,), in_specs=[two, two, one, one, one, one] + [anyspec] * (2 * n),
        out_specs=[two, two, one] + [anyspec] * n,
        out_shape=[jax.ShapeDtypeStruct((lp, 1024), BF16), jax.ShapeDtypeStruct((lp, 1024), BF16),
                   jax.ShapeDtypeStruct((lp, 512), BF16)] + [jax.ShapeDtypeStruct(b.shape, b.dtype) for b in bufs],
        input_output_aliases={6 + n + a: 3 + a for a in range(n)},
        scratch_shapes=[pltpu.VMEM((2, lp, LANE), BF16), pltpu.VMEM((2, lp, LANE), BF16),
                        pltpu.VMEM((lp, LANE), F32), pltpu.VMEM((lp, 2 * LANE), F32)]
        + (_comm_sems(n) if n else []),
        name="attn_bwd_scatter" if n else "attn_bwd", compiler_params=_cp())(q, k, v, o, do, lse, *xs, *bufs)


def _pool_lane_windows():
    lane = lax.broadcasted_iota(jnp.int32, (1, 256), 1)
    return jnp.where(lane < 64, 2, jnp.where(lane < 128, 4, jnp.where(lane < 192, 8, 16)))


def _by_window(wl, s2, s4, s8, s16):
    return jnp.where(wl == 2, s2, jnp.where(wl == 4, s4, jnp.where(wl == 8, s8, s16)))


def _pool_fwd_rows(pv_ext, t0):
    n = pv_ext.shape[0]
    wl = _pool_lane_windows()
    s2 = pv_ext + _dn(pv_ext, 1)
    s4 = s2 + _dn(s2, 2)
    s8 = s4 + _dn(s4, 4)
    s16 = s8 + _dn(s8, 8)
    t = t0 + lax.broadcasted_iota(jnp.int32, (n, 1), 0)
    cnt = jnp.maximum(jnp.minimum(t + 1, wl), 1).astype(F32)
    return _by_window(wl, s2, s4, s8, s16) / cnt - pv_ext


def _conv_dn(x_ext, w_ref, taps):
    acc = w_ref[taps - 1:taps, :] * x_ext
    for j in range(1, taps):
        acc = acc + w_ref[taps - 1 - j:taps - j, :] * _dn(x_ext, j)
    return acc


def _conv_up(g_ext, w_ref, taps):
    acc = w_ref[taps - 1:taps, :] * g_ext
    for j in range(1, taps):
        acc = acc + w_ref[taps - 1 - j:taps - j, :] * _up(g_ext, j)
    return acc


def _ln_fwd(c, g, b):
    mu = jnp.mean(c, axis=-1, keepdims=True)
    xc = c - mu
    r = lax.rsqrt(jnp.mean(xc * xc, axis=-1, keepdims=True) + EPS)
    xh = xc * r
    return xh * g + b, xh, r


def _halo_specs(lp, width, col):
    per = RB // HB
    last = lp // HB - 1
    cur = pl.BlockSpec((RB, width), lambda i: (i, col))
    prev = pl.BlockSpec((HB, width), lambda i: (jnp.maximum(i * per - 1, 0), col))
    nxt = pl.BlockSpec((HB, width), lambda i: (jnp.minimum((i + 1) * per, last), col))
    return cur, prev, nxt


def _mix_fwd(z, oat, bd, pscale, cw, cb, lng, lnb, sw, lp):
    def body(za, zah, mg, oat_ref, cu, cuh, cg, sbb, sbc, sbch, sbx, sbxh, sg,
             bd_ref, ps_ref, cw_ref, cb_ref, lng_ref, lnb_ref, sw_ref, u_ref):
        i = pl.program_id(0)
        pm = jnp.where(i > 0, 1.0, 0.0).astype(F32)
        pv = jnp.concatenate([zah[:, 0:256] * pm, za[:, 0:256]], axis=0)
        p = _pool_fwd_rows(pv, i * RB - HB)[HB:]
        y = jnp.dot(p.astype(BF16), bd_ref[...], preferred_element_type=F32)
        u_ref[:, 0:256] = (y * ps_ref[...] * _silu(za[:, 256:512])).astype(BF16)
        u_ref[:, 256:768] = (oat_ref[...] * _silu(mg[...])).astype(BF16)
        ce = jnp.concatenate([cuh[...] * pm, cu[...]], axis=0)
        glu = ce[:, 0:256] * _sig(ce[:, 256:512])
        c = _conv_dn(glu, cw_ref, CONF_K)[HB:] + cb_ref[...]
        n, _, _ = _ln_fwd(c, lng_ref[...], lnb_ref[...])
        u_ref[:, 768:1024] = (_silu(n) * _silu(cg[...])).astype(BF16)
        qe = jnp.concatenate([sbch[...] * sbxh[...] * pm, sbc[...] * sbx[...]], axis=0)
        cv = _conv_dn(qe, sw_ref, SC_K)[HB:]
        u_ref[:, 1024:1280] = (sbb[...] * cv * _silu(sg[...])).astype(BF16)

    a_cur, a_prev, _ = _halo_specs(lp, 512, PV0 // 512)
    cu_cur, cu_prev, _ = _halo_specs(lp, 512, CU0 // 512)
    sc_cur, sc_prev, _ = _halo_specs(lp, 256, SBC0 // 256)
    sx_cur, sx_prev, _ = _halo_specs(lp, 256, SBX0 // 256)
    c256 = lambda c0: pl.BlockSpec((RB, 256), lambda i: (i, c0 // 256))
    full = lambda r, c: pl.BlockSpec((r, c), lambda i: (0, 0))
    return pl.pallas_call(
        body, grid=(lp // RB,),
        in_specs=[a_cur, a_prev, pl.BlockSpec((RB, 512), lambda i: (i, MG0 // 512)),
                  pl.BlockSpec((RB, 512), lambda i: (i, 0)),
                  cu_cur, cu_prev, c256(CG0), c256(SBB0), sc_cur, sc_prev, sx_cur, sx_prev, c256(SG0),
                  full(256, 256), full(1, 256), full(32, 256), full(1, 256), full(1, 256), full(1, 256),
                  full(8, 256)],
        out_specs=pl.BlockSpec((RB, 1280), lambda i: (i, 0)),
        out_shape=jax.ShapeDtypeStruct((lp, 1280), BF16),
        name="mix_fwd", compiler_params=_cp())(z, z, z, oat, z, z, z, z, z, z, z, z, z,
                                               bd, pscale, cw, cb, lng, lnb, sw)


def _mix_bwd(z, oat, du, bd, pscale, cw, cb, lng, lnb, sw, lp):
    nb = lp // RB
    ne = RB + 2 * HB
    nf = RB + HB

    def body(za, zah, zan, mg, oat_ref, cu, cuh, cun, cg, cgn, sbb, sbbn, sbc, sbch, sbcn, sbx, sbxh, sbxn,
             sg, sgn, du_ref, dun_ref, bd_ref, ps_ref, cw_ref, cb_ref, lng_ref, lnb_ref, sw_ref,
             dzx_ref, doat_ref, dbd_ref, dcw_ref, dsw_ref, dsm_ref):
        xa, xm, xc = 0, MG0 - PV0, CU0 - PV0
        i = pl.program_id(0)
        pm = jnp.where(i > 0, 1.0, 0.0).astype(F32)
        nm = jnp.where(i < nb - 1, 1.0, 0.0).astype(F32)

        def ext(cur, prev, nxt, sl=slice(None)):
            return jnp.concatenate([prev[:, sl] * pm, cur[:, sl], nxt[:, sl] * nm], axis=0)

        def fwd(cur, nxt, sl=slice(None)):
            return jnp.concatenate([cur[:, sl], nxt[:, sl] * nm], axis=0)

        def csum(x):
            return jnp.sum(x, axis=0, keepdims=True)

        @pl.when(i == 0)
        def _():
            dbd_ref[...] = jnp.zeros((256, 256), F32)
            dcw_ref[...] = jnp.zeros((32, 256), F32)
            dsw_ref[...] = jnp.zeros((8, 256), F32)
            dsm_ref[...] = jnp.zeros((8, 256), F32)

        a_cols, b_cols = slice(0, 256), slice(256, 512)
        pv_e = ext(za, zah, zan, a_cols)
        p = _pool_fwd_rows(pv_e, i * RB - HB)[HB:HB + RB]
        pb = p.astype(BF16)
        y = jnp.dot(pb, bd_ref[...], preferred_element_type=F32)
        pg_f = fwd(za, zan, b_cols)
        dua_f = fwd(du_ref, dun_ref, slice(0, 256))
        dyp_f = dua_f * ps_ref[...] * _silu(pg_f)
        dypb = dyp_f.astype(BF16)
        dp_f = lax.dot_general(dypb, bd_ref[...], (((1,), (1,)), ((), ())), preferred_element_type=F32)
        wl = _pool_lane_windows()
        t = i * RB + lax.broadcasted_iota(jnp.int32, (nf, 1), 0)
        cnt = jnp.minimum(t + 1, wl).astype(F32)
        qf = dp_f / cnt
        f2 = qf + _up(qf, 1)
        f4 = f2 + _up(f2, 2)
        f8 = f4 + _up(f4, 4)
        f16 = f8 + _up(f8, 8)
        dpv = (_by_window(wl, f2, f4, f8, f16) - dp_f)[0:RB]
        dua = du_ref[:, 0:256]
        pg = za[:, b_cols]
        dpg = dua * y * ps_ref[...] * _dsilu(pg)
        dzx_ref[:, xa:xa + 256] = dpv.astype(BF16)
        dzx_ref[:, xa + 256:xa + 512] = dpg.astype(BF16)
        d_scale = csum(dua * y * _silu(pg))
        d_bd = lax.dot_general(pb, dypb[0:RB], (((0,), (0,)), ((), ())), preferred_element_type=F32)

        dub = du_ref[:, 256:768]
        mgv = mg[...]
        dzx_ref[:, xm:xm + 512] = (dub * oat_ref[...] * _dsilu(mgv)).astype(BF16)
        doat_ref[...] = dub * _silu(mgv)

        a_e = ext(cu, cuh, cun, slice(0, 256))
        gt_e = ext(cu, cuh, cun, slice(256, 512))
        sg_e = _sig(gt_e)
        glu_e = a_e * sg_e
        c_f = _conv_dn(glu_e, cw_ref, CONF_K)[HB:] + cb_ref[...]
        n_f, xh_f, r_f = _ln_fwd(c_f, lng_ref[...], lnb_ref[...])
        cg_f = fwd(cg, cgn)
        duc_f = fwd(du_ref, dun_ref, slice(768, 1024))
        sw_f = _silu(n_f)
        dcg = (duc_f * sw_f * _dsilu(cg_f))[0:RB]
        dn_f = duc_f * _silu(cg_f) * _dsilu(n_f)
        a_f = dn_f * lng_ref[...]
        dc_f = r_f * (a_f - jnp.mean(a_f, axis=-1, keepdims=True)
                      - xh_f * jnp.mean(a_f * xh_f, axis=-1, keepdims=True))
        d_lng = csum((dn_f * xh_f)[0:RB])
        d_lnb = csum(dn_f[0:RB])
        d_cb = csum(dc_f[0:RB])
        dglu = _conv_up(dc_f, cw_ref, CONF_K)[0:RB]
        dc_c = dc_f[0:RB]
        for kk in range(CONF_K):
            j = CONF_K - 1 - kk
            dcw_ref[kk:kk + 1, :] += csum(dc_c * _dn(glu_e, j)[HB:HB + RB])

        sgc = sg_e[HB:HB + RB]
        a_c = a_e[HB:HB + RB]
        dzx_ref[:, xc:xc + 256] = (dglu * sgc).astype(BF16)
        dzx_ref[:, xc + 256:xc + 512] = (dglu * a_c * sgc * (1.0 - sgc)).astype(BF16)
        dzx_ref[:, xc + 512:xc + 768] = dcg.astype(BF16)

        c_e = ext(sbc, sbch, sbcn)
        x_e = ext(sbx, sbxh, sbxn)
        q_e = c_e * x_e
        cv_f = _conv_dn(q_e, sw_ref, SC_K)[HB:]
        bg_f = fwd(sbb, sbbn)
        sg_f = fwd(sg, sgn)
        dud_f = fwd(du_ref, dun_ref, slice(1024, 1280))
        ssg_f = _silu(sg_f)
        dcv_f = dud_f * bg_f * ssg_f
        dbg = (dud_f * cv_f * ssg_f)[0:RB]
        dsg = (dud_f * bg_f * cv_f * _dsilu(sg_f))[0:RB]
        dq = _conv_up(dcv_f, sw_ref, SC_K)[0:RB]
        dcv_c = dcv_f[0:RB]
        for kk in range(SC_K):
            j = SC_K - 1 - kk
            dsw_ref[kk:kk + 1, :] += csum(dcv_c * _dn(q_e, j)[HB:HB + RB])

        dzx_ref[:, xc + 768:xc + 1024] = dbg.astype(BF16)
        dzx_ref[:, xc + 1024:xc + 1280] = (dq * x_e[HB:HB + RB]).astype(BF16)
        dzx_ref[:, xc + 1280:xc + 1536] = (dq * c_e[HB:HB + RB]).astype(BF16)
        dzx_ref[:, xc + 1536:xc + 1792] = dsg.astype(BF16)

        dbd_ref[...] += d_bd
        dsm_ref[0:1, :] += d_scale
        dsm_ref[1:2, :] += d_cb
        dsm_ref[2:3, :] += d_lng
        dsm_ref[3:4, :] += d_lnb

    a3 = _halo_specs(lp, 512, PV0 // 512)
    cu3 = _halo_specs(lp, 512, CU0 // 512)
    cg3 = _halo_specs(lp, 256, CG0 // 256)
    sbb3 = _halo_specs(lp, 256, SBB0 // 256)
    sbc3 = _halo_specs(lp, 256, SBC0 // 256)
    sbx3 = _halo_specs(lp, 256, SBX0 // 256)
    sg3 = _halo_specs(lp, 256, SG0 // 256)
    du3 = _halo_specs(lp, 1280, 0)
    full = lambda r, c: pl.BlockSpec((r, c), lambda i: (0, 0))
    in_specs = [a3[0], a3[1], a3[2], pl.BlockSpec((RB, 512), lambda i: (i, MG0 // 512)),
                pl.BlockSpec((RB, 512), lambda i: (i, 0)),
                cu3[0], cu3[1], cu3[2], cg3[0], cg3[2], sbb3[0], sbb3[2],
                sbc3[0], sbc3[1], sbc3[2], sbx3[0], sbx3[1], sbx3[2], sg3[0], sg3[2],
                du3[0], du3[2],
                full(256, 256), full(1, 256), full(32, 256), full(1, 256), full(1, 256), full(1, 256),
                full(8, 256)]
    out_specs = [pl.BlockSpec((RB, ZW - PV0), lambda i: (i, 0)), pl.BlockSpec((RB, 512), lambda i: (i, 0)),
                 full(256, 256), full(32, 256), full(8, 256), full(8, 256)]
    out_shape = [jax.ShapeDtypeStruct((lp, ZW - PV0), BF16), jax.ShapeDtypeStruct((lp, 512), F32),
                 jax.ShapeDtypeStruct((256, 256), F32), jax.ShapeDtypeStruct((32, 256), F32),
                 jax.ShapeDtypeStruct((8, 256), F32), jax.ShapeDtypeStruct((8, 256), F32)]
    return pl.pallas_call(
        body, grid=(nb,), in_specs=in_specs, out_specs=out_specs, out_shape=out_shape,
        name="mix_bwd", compiler_params=_cp())(
            z, z, z, z, oat, z, z, z, z, z, z, z, z, z, z, z, z, z, z, z, du, du,
            bd, pscale, cw, cb, lng, lnb, sw)


U_OFF = (0, 256, 768, 1024, 1280)
MRB = 192


def _merge_fwd(x, u, z, gb, wout, wo, gpost, lp):
    MRB = RB

    def body(x_ref, u_ref, gl_ref, gb_ref, wout_ref, wo_ref, g_ref, xo_ref, m_ref, o2_ref):
        m = jnp.zeros((MRB, D), F32)
        for b in range(4):
            y = jnp.dot(u_ref[:, U_OFF[b]:U_OFF[b + 1]], wout_ref[U_OFF[b]:U_OFF[b + 1], :],
                        preferred_element_type=F32)
            sl = slice(D * b, D * (b + 1))
            m = m + _sig(gl_ref[:, sl] + gb_ref[:, sl]) * y
        mb = m.astype(BF16)
        m_ref[...] = mb
        o2 = jnp.dot(mb, wo_ref[...], preferred_element_type=F32)
        o2_ref[...] = o2
        r = lax.rsqrt(jnp.mean(o2 * o2, axis=-1, keepdims=True) + EPS)
        xo_ref[...] = x_ref[...] + o2 * r * g_ref[...]

    blk = pl.BlockSpec((MRB, D), lambda i: (i, 0))
    full = lambda r, c: pl.BlockSpec((r, c), lambda i: (0, 0))
    return pl.pallas_call(
        body, grid=(lp // MRB,),
        in_specs=[blk, pl.BlockSpec((MRB, 1280), lambda i: (i, 0)), pl.BlockSpec((MRB, 4096), lambda i: (i, 0)),
                  full(1, 4096), full(1280, D), full(D, D), full(1, D)],
        out_specs=[blk, blk, blk],
        out_shape=[jax.ShapeDtypeStruct((lp, D), F32), jax.ShapeDtypeStruct((lp, D), BF16),
                   jax.ShapeDtypeStruct((lp, D), F32)],
        name="merge_fwd", compiler_params=_cp())(x, u, z, gb, wout, wo, gpost)


def _merge_bwd(dx, o2, u, z, gb, wout, wo, gpost, lp):
    def body(dx_ref, o2_ref, u_ref, gl_ref, gb_ref, wout_ref, wo_ref, g_ref,
             do2_ref, dgl_ref, dy_ref, du_ref, dgb_ref, dg_ref):
        i = pl.program_id(0)
        o2 = o2_ref[...]
        dy = dx_ref[...]
        r = lax.rsqrt(jnp.mean(o2 * o2, axis=-1, keepdims=True) + EPS)
        a = dy * g_ref[...]
        do2 = (r * a - o2 * (r * r * r) * jnp.mean(a * o2, axis=-1, keepdims=True)).astype(BF16)
        do2_ref[...] = do2
        dg = jnp.sum(dy * o2 * r, axis=0, keepdims=True)
        dm = lax.dot_general(do2, wo_ref[...], (((1,), (1,)), ((), ())), preferred_element_type=F32)
        for b in range(4):
            rows = slice(U_OFF[b], U_OFF[b + 1])
            y = jnp.dot(u_ref[:, rows], wout_ref[rows, :], preferred_element_type=F32)
            sl = slice(D * b, D * (b + 1))
            gt = _sig(gl_ref[:, sl] + gb_ref[:, sl])
            dgl = dm * y * gt * (1.0 - gt)
            dgl_ref[:, sl] = dgl.astype(BF16)
            part = jnp.sum(dgl, axis=0, keepdims=True)

            @pl.when(i == 0)
            def _(part=part, sl=sl):
                dgb_ref[:, sl] = part

            @pl.when(i > 0)
            def _(part=part, sl=sl):
                dgb_ref[:, sl] += part

            dyb = (dm * gt).astype(BF16)
            dy_ref[:, sl] = dyb
            du_ref[:, rows] = lax.dot_general(dyb, wout_ref[rows, :], (((1,), (1,)), ((), ())),
                                              preferred_element_type=F32)

        @pl.when(i == 0)
        def _():
            dg_ref[...] = dg

        @pl.when(i > 0)
        def _():
            dg_ref[...] += dg

    blk = pl.BlockSpec((MRB, D), lambda i: (i, 0))
    wide = pl.BlockSpec((MRB, 4096), lambda i: (i, 0))
    ub = pl.BlockSpec((MRB, 1280), lambda i: (i, 0))
    full = lambda r, c: pl.BlockSpec((r, c), lambda i: (0, 0))
    return pl.pallas_call(
        body, grid=(lp // MRB,),
        in_specs=[blk, blk, ub, wide, full(1, 4096), full(1280, D), full(D, D), full(1, D)],
        out_specs=[blk, wide, wide, ub, full(1, 4096), full(1, D)],
        out_shape=[jax.ShapeDtypeStruct((lp, D), BF16), jax.ShapeDtypeStruct((lp, 4096), BF16),
                   jax.ShapeDtypeStruct((lp, 4096), BF16), jax.ShapeDtypeStruct((lp, 1280), F32),
                   jax.ShapeDtypeStruct((1, 4096), F32), jax.ShapeDtypeStruct((1, D), F32)],
        name="merge_bwd", compiler_params=_cp())(dx, o2, u, z, gb, wout, wo, gpost)


def _dh(dgl, dzq, dzx, w_in, lp, scatter=None):
    xs, bufs, layer = scatter if scatter else ((), (), None)
    n = len(xs)
    steps = lp // RB
    segs = ((0, CQ0), (CQ0, PV0), (PV0, ZW))

    def body(gl_ref, zq_ref, zx_ref, w_ref, *rest):
        o_ref = rest[2 * n]
        if n:
            s_start, s_finish = _scatter_phases(rest[:n], rest[2 * n + 1:3 * n + 1], *rest[3 * n + 1:], layer)
            pl.when(pl.program_id(0) == 0)(s_start)
        acc = None
        for a_ref, (lo, hi) in zip((gl_ref, zq_ref, zx_ref), segs):
            part = lax.dot_general(a_ref[...], w_ref[:, lo:hi], (((1,), (1,)), ((), ())),
                                   preferred_element_type=F32)
            acc = part if acc is None else acc + part
        o_ref[...] = acc
        if n:
            pl.when(pl.program_id(0) == steps - 1)(s_finish)

    anyspec = pl.BlockSpec(memory_space=pl.ANY)
    row = lambda w: pl.BlockSpec((RB, w), lambda i: (i, 0))
    return pl.pallas_call(
        body, grid=(steps,),
        in_specs=[row(CQ0), row(PV0 - CQ0), row(ZW - PV0), pl.BlockSpec((D, ZW), lambda i: (0, 0))]
        + [anyspec] * (2 * n),
        out_specs=[row(D)] + [anyspec] * n,
        out_shape=[jax.ShapeDtypeStruct((lp, D), F32)] + [jax.ShapeDtypeStruct(b.shape, b.dtype) for b in bufs],
        input_output_aliases={4 + n + a: 1 + a for a in range(n)},
        scratch_shapes=_comm_sems(n) if n else [],
        name="dh_scatter" if n else "dh", compiler_params=_cp())(dgl, dzq, dzx, w_in, *xs, *bufs)


def _loss_head(xf, tgt, n_real, lp):
    def body(x_ref, t_ref, dy_ref, ls_ref):
        i = pl.program_id(0)
        t = i * RB + lax.broadcasted_iota(jnp.int32, (RB, 1), 0)
        real = (t >= N_META) & (t < n_real)
        err = jnp.where(real, x_ref[...] - t_ref[...], 0.0)
        dy_ref[...] = err / D
        part = 0.5 * jnp.sum(jnp.mean(err * err, axis=-1, keepdims=True), axis=0, keepdims=True)
        part = jnp.broadcast_to(part, (8, LANE))

        @pl.when(i == 0)
        def _():
            ls_ref[...] = part

        @pl.when(i > 0)
        def _():
            ls_ref[...] += part

    blk = pl.BlockSpec((RB, D), lambda i: (i, 0))
    return pl.pallas_call(
        body, grid=(lp // RB,), in_specs=[blk, blk],
        out_specs=[blk, pl.BlockSpec((8, LANE), lambda i: (0, 0))],
        out_shape=[jax.ShapeDtypeStruct((lp, D), F32), jax.ShapeDtypeStruct((8, LANE), F32)],
        name="loss_head", compiler_params=_cp())(xf, tgt)


def _peer(d):
    x, y, c = lax.axis_index("x"), lax.axis_index("y"), lax.axis_index("c")
    return (x ^ ((d >> 2) & 1), y ^ ((d >> 1) & 1), c ^ (d & 1))


def _index_of(p):
    return 4 * p[0] + 2 * p[1] + p[2]


def _all_gather(xs, name):
    n = len(xs)

    def body(*refs):
        start, forward, finish = _gather_phases(refs[:n], refs[n:2 * n], *refs[2 * n:])
        start()
        forward()
        finish()

    anyspec = pl.BlockSpec(memory_space=pl.ANY)
    return pl.pallas_call(
        body, in_specs=[anyspec] * n, out_specs=[anyspec] * n,
        out_shape=_gather_shapes(xs), scratch_shapes=_comm_sems(n), name=name)(*xs)


def _gather_shapes(xs):
    return [jax.ShapeDtypeStruct((N_DEV,) + x.shape, x.dtype) for x in xs]


def _comm_sems(n):
    return [pltpu.SemaphoreType.DMA((7 * n,)), pltpu.SemaphoreType.DMA((7 * n,)), pltpu.SemaphoreType.DMA((n,))]


def _gather_phases(x_refs, out_refs, send_sems, recv_sems, local_sems):
    n = len(x_refs)
    chips = [2, 4, 6]

    def copy(a, kk, block, to, src=None):
        slot = out_refs[a].at[_index_of(block)]
        return pltpu.make_async_remote_copy(
            src_ref=slot if src is None else src, dst_ref=slot,
            send_sem=send_sems.at[7 * a + kk], recv_sem=recv_sems.at[7 * a + kk], device_id=to,
            device_id_type=MESH)

    def local(a):
        return pltpu.make_async_copy(x_refs[a], out_refs[a].at[_index_of(_peer(0))], local_sems.at[a])

    def firsts():
        out = []
        for a in range(n):
            out.append(copy(a, 0, _peer(0), _peer(1), src=x_refs[a]))
            out += [copy(a, 1 + j, _peer(0), _peer(d), src=x_refs[a]) for j, d in enumerate(chips)]
        return out

    def passes():
        return [copy(a, 4 + j, _peer(d), _peer(1)) for j, d in enumerate(chips) for a in range(n)]

    def start():
        for a in range(n):
            local(a).start()
        for cp in firsts():
            cp.start()

    def forward():
        for j, d in enumerate(chips):
            for a in range(n):
                copy(a, 1 + j, _peer(d), _peer(0)).wait_recv()
                copy(a, 4 + j, _peer(d), _peer(1)).start()

    def finish():
        for a in range(n):
            copy(a, 0, _peer(1), _peer(0)).wait_recv()
            for j, d in enumerate(chips):
                copy(a, 4 + j, _peer(d | 1), _peer(0)).wait_recv()
        for cp in firsts() + passes():
            cp.wait_send()
        for a in range(n):
            local(a).wait()

    return start, forward, finish


def _scatter_phases(x_refs, out_refs, send_sems, recv_sems, local_sems, layer):
    n = len(x_refs)

    def land(a, dev):
        slot = out_refs[a].at[dev]
        return slot if layer is None else slot.at[layer]

    def local(a):
        my = _index_of(_peer(0))
        return pltpu.make_async_copy(x_refs[a].at[my], land(a, my), local_sems.at[a])

    def copy(a, d):
        my = _index_of(_peer(0))
        return pltpu.make_async_remote_copy(
            src_ref=x_refs[a].at[_index_of(_peer(d))], dst_ref=land(a, my),
            send_sem=send_sems.at[7 * a + d - 1], recv_sem=recv_sems.at[7 * a + d - 1], device_id=_peer(d),
            device_id_type=MESH)

    def arrival(a, d):
        frm = _index_of(_peer(d))
        return pltpu.make_async_remote_copy(
            src_ref=x_refs[a].at[frm], dst_ref=land(a, frm),
            send_sem=send_sems.at[7 * a + d - 1], recv_sem=recv_sems.at[7 * a + d - 1], device_id=_peer(d),
            device_id_type=MESH)

    def start():
        for a in range(n):
            local(a).start()
        for d in range(1, N_DEV):
            for a in range(n):
                copy(a, d).start()

    def finish():
        for d in range(1, N_DEV):
            for a in range(n):
                arrival(a, d).wait_recv()
        for d in range(1, N_DEV):
            for a in range(n):
                copy(a, d).wait_send()
        for a in range(n):
            local(a).wait()

    return start, finish


def _all_to_all(xs, bufs, layer, name):
    n = len(xs)

    def body(*refs):
        start, finish = _scatter_phases(refs[:n], refs[2 * n:3 * n], *refs[3 * n:], layer)
        start()
        finish()

    anyspec = pl.BlockSpec(memory_space=pl.ANY)
    return pl.pallas_call(
        body, in_specs=[anyspec] * (2 * n), out_specs=[anyspec] * n,
        out_shape=[jax.ShapeDtypeStruct(b.shape, b.dtype) for b in bufs],
        input_output_aliases={n + a: a for a in range(n)},
        scratch_shapes=_comm_sems(n), name=name)(*xs, *bufs)


def _adam_math(g, w, m, v):
    c1 = 1.0 - ADAM_B1 ** ADAM_STEP
    c2 = 1.0 - ADAM_B2 ** ADAM_STEP
    mn = ADAM_B1 * m + (1.0 - ADAM_B1) * g
    vn = ADAM_B2 * v + (1.0 - ADAM_B2) * (g * g)
    return -ADAM_LR * ((mn / c1) / (jnp.sqrt(vn / c2) + ADAM_EPS) + ADAM_WD * w), mn, vn


def _reduce_adam(parts, w, m, v, rb, name, row_off=0):
    depth, rows, cols = w.shape
    assert rows % rb == 0 and row_off % rb == 0

    def body(p_ref, w_ref, m_ref, v_ref, g_ref, d_ref, mo_ref, vo_ref):
        g = p_ref[0, 0].astype(F32)
        for j in range(1, N_DEV):
            g = g + p_ref[j, 0].astype(F32)
        g_ref[0] = g
        d_ref[0], mo_ref[0], vo_ref[0] = _adam_math(g, w_ref[0], m_ref[0], v_ref[0])

    blk = pl.BlockSpec((1, rb, cols), lambda l, i: (l, i, 0))
    out = jax.ShapeDtypeStruct(w.shape, F32)
    return pl.pallas_call(
        body, grid=(depth, rows // rb),
        in_specs=[pl.BlockSpec((N_DEV, 1, rb, cols), lambda l, i: (0, l, i + row_off // rb, 0)), blk, blk, blk],
        out_specs=[blk, blk, blk, blk], out_shape=[out, out, out, out],
        name=name, compiler_params=_cp())(parts, w, m, v)


def _reduce_adam_flat(parts, w, m, v, name):
    q_rows = w.shape[0]

    def body(p_ref, w_ref, m_ref, v_ref, g_ref, d_ref, mo_ref, vo_ref):
        g = p_ref[0].astype(F32)
        for j in range(1, N_DEV):
            g = g + p_ref[j].astype(F32)
        g_ref[...] = g
        d_ref[...], mo_ref[...], vo_ref[...] = _adam_math(g, w_ref[...], m_ref[...], v_ref[...])

    blk = pl.BlockSpec((q_rows, LANE), lambda i: (0, 0))
    out = jax.ShapeDtypeStruct((q_rows, LANE), F32)
    return pl.pallas_call(
        body, grid=(1,), in_specs=[pl.BlockSpec((N_DEV, q_rows, LANE), lambda i: (0, 0, 0)), blk, blk, blk],
        out_specs=[blk, blk, blk, blk], out_shape=[out, out, out, out],
        name=name, compiler_params=_cp())(parts, w, m, v)


C128 = (("w_out_pool", 256), ("w_out_mla", 512), ("w_out_conf", 256), ("w_out_sc", 256), ("w_ukv", 128))
C128_ROWS = sum(r for _, r in C128)
TAIL = (("meta_tokens", (N_META, 128)), ("conf_dw_w", (DEPTH, CONF_K, 32)), ("sc_dw_w", (DEPTH, SC_K, 32)))
TAIL_ROWS = sum(int(np.prod(s)) for _, s in TAIL) // LANE
TAIL_PAD = 56
SMALL = (("pre_norm_g", (DEPTH, D)), ("gate_bias", (DEPTH, 4096)), ("pool_w", (DEPTH, 4, 64, 64)),
         ("pool_scale", (DEPTH, 256)), ("q_norm_g", (DEPTH, 256)), ("kv_norm_g", (DEPTH, 128)),
         ("conf_dw_b", (DEPTH, 256)), ("conf_ln_g", (DEPTH, 256)), ("conf_ln_b", (DEPTH, 256)),
         ("post_norm_g", (DEPTH, D)))
SMALL_ROWS = sum(int(np.prod(s)) for _, s in SMALL) // LANE
SMALL_PAD = -(-(SMALL_ROWS + 1) // 8) * 8


def _pack_tail(t):
    parts = [t[n].reshape(-1, LANE) for n, _ in TAIL]
    parts.append(jnp.zeros((TAIL_PAD - TAIL_ROWS, LANE), F32))
    return jnp.concatenate(parts, axis=0)


def _unpack_tail(flat):
    out, off = {}, 0
    for n, s in TAIL:
        rows = int(np.prod(s)) // LANE
        out[n] = flat[off:off + rows].reshape(s)
        off += rows
    return out


def _unpack_tail_full(g):
    out, off = {}, 0
    for n, s in TAIL:
        rows = int(np.prod(s)) // LANE
        blk = jnp.moveaxis(g[:, off:off + rows].reshape((N_DEV,) + s), 0, -2)
        out[n] = blk.reshape(s[:-1] + (N_DEV * s[-1],))
        off += rows
    return out


def _pack_small(t, extra_row):
    parts = [t[n].reshape(-1, LANE) for n, _ in SMALL] + [extra_row]
    parts.append(jnp.zeros((SMALL_PAD - SMALL_ROWS - 1, LANE), F32))
    return jnp.concatenate(parts, axis=0)


def _unpack_small(flat):
    out, off = {}, 0
    for n, s in SMALL:
        rows = int(np.prod(s)) // LANE
        out[n] = flat[off:off + rows].reshape(s)
        off += rows
    return out


def _cols_by_dest(g, width):
    r = g.shape[0]
    return g.reshape(r, N_DEV, width).transpose(1, 0, 2)


def _cols_full(gathered):
    _, r, c = gathered.shape
    return gathered.transpose(1, 0, 2).reshape(r, N_DEV * c)


W_IN_SHARD = 916
PACKED_SEGS = ((3232, 7328), (512, 896), 64, (896, 928), 32, (0, 512), (928, 3232))


def _pack_w_in(g):
    parts = []
    for seg in PACKED_SEGS:
        if isinstance(seg, int):
            parts.append(jnp.zeros((g.shape[1], seg), g.dtype))
            continue
        a, b = seg
        while a < b:
            k = a // W_IN_SHARD
            hi = min(b, W_IN_SHARD * (k + 1))
            parts.append(g[k, :, a - W_IN_SHARD * k:hi - W_IN_SHARD * k])
            a = hi
    return jnp.concatenate(parts, axis=1)


def _w_in_grad_by_dest(gl, mla, mix):
    src = (((0, 512), mix, 0), ((512, 896), mla, 0), ((896, 928), mla, 448), ((928, 3232), mix, 512),
           ((3232, 7328), gl, 0))
    blocks = []
    for k in range(N_DEV):
        lo, hi = W_IN_SHARD * k, W_IN_SHARD * (k + 1)
        parts = []
        for (a, b), arr, off in src:
            s, e = max(a, lo), min(b, hi)
            if s < e:
                parts.append(arr[:, off + s - a:off + e - a])
        blocks.append(jnp.concatenate(parts, axis=1))
    return jnp.stack(blocks)


def _rope_tables(lp):
    inv = 1.0 / (ROPE_THETA ** (jnp.arange(0, QK_ROPE, 2, dtype=F32) / QK_ROPE))
    ang = jnp.arange(lp, dtype=F32)[:, None] * inv[None, :]
    cos, sin = jnp.cos(ang), jnp.sin(ang)
    one = jnp.ones((lp, QK_NOPE), F32)
    zero = jnp.zeros((lp, QK_NOPE), F32)
    z16 = jnp.zeros((lp, 16), F32)
    c = jnp.concatenate([one, cos, cos, jnp.ones((lp, 32), F32)], axis=1)
    s1 = jnp.concatenate([zero, z16, sin, jnp.zeros((lp, 32), F32)], axis=1)
    s2 = jnp.concatenate([zero, -sin, z16, jnp.zeros((lp, 32), F32)], axis=1)
    return c, s1, s2


def kernel(x, meta_tokens, pre_norm_g, w_in, gate_bias, pool_w, pool_scale, w_out_pool, q_norm_g, w_uq, kv_norm_g, w_ukv, w_out_mla, conf_dw_w, conf_dw_b, conf_ln_g, conf_ln_b, w_out_conf, sc_dw_w, w_out_sc, w_o, post_norm_g, loss_target, m_meta_tokens, m_pre_norm_g, m_w_in, m_gate_bias, m_pool_w, m_pool_scale, m_w_out_pool, m_q_norm_g, m_w_uq, m_kv_norm_g, m_w_ukv, m_w_out_mla, m_conf_dw_w, m_conf_dw_b, m_conf_ln_g, m_conf_ln_b, m_w_out_conf, m_sc_dw_w, m_w_out_sc, m_w_o, m_post_norm_g, v_meta_tokens, v_pre_norm_g, v_w_in, v_gate_bias, v_pool_w, v_pool_scale, v_w_out_pool, v_q_norm_g, v_w_uq, v_kv_norm_g, v_w_ukv, v_w_out_mla, v_conf_dw_w, v_conf_dw_b, v_conf_ln_g, v_conf_ln_b, v_w_out_conf, v_sc_dw_w, v_w_out_sc, v_w_o, v_post_norm_g):
    names = ["w_in", "w_uq", "w_o"] + [n for n, _ in C128] + [n for n, _ in TAIL] + [n for n, _ in SMALL]
    loc = locals()
    w = {n: loc[n] for n in names}
    mom = {n: loc["m_" + n] for n in names}
    vel = {n: loc["v_" + n] for n in names}

    seq = x.shape[1]
    n_real = N_META + seq
    lp = -(-n_real // RB) * RB
    tmb = lp // 3
    tabs = _rope_tables(lp)

    c128 = jnp.concatenate([w[n] for n, _ in C128], axis=1)
    def shards_of(i):
        return [w_in[i].astype(BF16), c128[i].astype(BF16), w_uq[i].astype(BF16), w_o[i].astype(BF16)]

    gathered = [_all_gather(shards_of(0), "gather_weights")] + [None] * (DEPTH - 1)
    tail_w = _pack_tail(w)
    tail = _unpack_tail_full(_all_gather([tail_w], "gather_tail")[0])
    eye4 = jnp.eye(4, dtype=F32)
    bd_all = (pool_w[:, :, :, None, :] * eye4[None, :, None, :, None]).reshape(DEPTH, 256, 256).astype(BF16)

    def layer_weights(i):
        g_in, g_c128, g_uq, g_o = gathered[i]
        lw = {}
        lw["w_in"] = _pack_w_in(g_in)
        lw["wc"] = _cols_full(g_c128)
        wuq = _cols_full(g_uq).reshape(Q_RANK, HEADS, 96)
        lw["w_uq"] = jnp.pad(wuq, ((0, 0), (0, 0), (0, 32))).reshape(Q_RANK, HEADS * LANE)
        wukv = lw["wc"][U_OFF[4]:].reshape(KV_RANK, HEADS, 128)
        wk = jnp.pad(wukv[:, :, :QK_NOPE], ((0, 0), (0, 0), (0, 64))).reshape(KV_RANK, HEADS * LANE)
        lw["w_ukv"] = jnp.concatenate([wk, wukv[:, :, QK_NOPE:].reshape(KV_RANK, HEADS * V_DIM)], axis=1)
        lw["w_o"] = g_o.reshape(D, D)
        lw["bd"] = bd_all[i]
        lw["cw"] = jnp.pad(tail["conf_dw_w"][i], ((0, 1), (0, 0)))
        lw["sw"] = jnp.pad(tail["sc_dw_w"][i], ((0, 8 - SC_K), (0, 0)))
        return lw

    meta_full = tail["meta_tokens"]

    pad_rows = lp - n_real
    xr = jnp.concatenate([meta_full, x[0], jnp.zeros((pad_rows, D), F32)], axis=0)
    tgt = jnp.pad(loss_target[0], ((N_META, pad_rows), (0, 0)))
    saved = []
    for i in range(DEPTH):
        lw = layer_weights(i)
        h = _rms_fwd(xr, pre_norm_g[i:i + 1], lp)
        z = _mm(h, lw["w_in"], lp, ZW, D, tm=tmb, tn=256, tk=D, name="mm_in")
        qn, kvn, krr = _mla_prep(z, q_norm_g[i:i + 1], kv_norm_g[i:i + 1], tabs, lp)
        q_raw = _mm(qn, lw["w_uq"], lp, 1024, Q_RANK, tm=tmb, tn=1024, tk=Q_RANK, name="mm_uq")
        kv_raw = _mm(kvn, lw["w_ukv"], lp, 1536, KV_RANK, tm=tmb, tn=512, tk=KV_RANK, name="mm_ukv")
        qt, kt, vt = _mla_post(q_raw, kv_raw, krr, tabs, lp)
        res = _attn_fwd(qt, kt, vt, lp, gather=shards_of(i + 1) if i + 1 < DEPTH else ())
        oat, lse = res[0], res[1]
        if i + 1 < DEPTH:
            gathered[i + 1] = res[2:]
        u = _mix_fwd(z, oat, lw["bd"], pool_scale[i:i + 1], lw["cw"], conf_dw_b[i:i + 1], conf_ln_g[i:i + 1],
                     conf_ln_b[i:i + 1], lw["sw"], lp)
        x_new, m_act, o2 = _merge_fwd(xr, u, z, gate_bias[i:i + 1], lw["wc"], lw["w_o"],
                                      post_norm_g[i:i + 1], lp)
        saved.append(dict(lw=lw, x=xr, h=h, z=z, qn=qn, kvn=kvn, qt=qt, kt=kt, vt=vt, oat=oat, lse=lse,
                          u=u, m=m_act, o2=o2))
        xr = x_new

    dx, loss_part = _loss_head(xr, tgt, n_real, lp)

    gsm = {n: [None] * DEPTH for n, _ in SMALL}
    g_cw = [None] * DEPTH
    g_sw = [None] * DEPTH
    recv = [lax.empty((N_DEV, DEPTH) + s, BF16) for s in ((D, 916), (C128_ROWS, 128), (Q_RANK, 96), (128, D))]
    pending = None
    for i in reversed(range(DEPTH)):
        s = saved[i]
        lw = s["lw"]
        do2, dgl, dyb, du, dgb, dgpost = _merge_bwd(dx, s["o2"], s["u"], s["z"], gate_bias[i:i + 1],
                                                    lw["wc"], lw["w_o"], post_norm_g[i:i + 1], lp)
        d_wo = _mm(s["m"], do2, D, D, lp, ta=True, tm=512, tn=D, tk=tmb, out_dtype=BF16, name="mm_dwo")
        d_wout = []
        for b in range(4):
            rows = U_OFF[b + 1] - U_OFF[b]
            d_wout.append(_mm(s["u"], dyb, rows, D, lp, ta=True, tm=256, tn=D, tk=tmb, out_dtype=BF16,
                              a_moff=U_OFF[b] // 256, b_noff=b, name="mm_dwout%d" % b))
        dzx, doat, dbd, dcw, dsw, dsm = _mix_bwd(
            s["z"], s["oat"], du, lw["bd"], pool_scale[i:i + 1], lw["cw"], conf_dw_b[i:i + 1],
            conf_ln_g[i:i + 1], conf_ln_b[i:i + 1], lw["sw"], lp)
        res = _attn_bwd(s["qt"], s["kt"], s["vt"], s["oat"], doat, s["lse"], lp,
                        scatter=(pending, recv, i + 1) if pending else None)
        dqt, dkt, dvt = res[:3]
        if pending:
            recv = list(res[3:])
        dq_raw, dkv_raw, dkr = _mla_post_bwd(dqt, dkt, dvt, tabs, lp)
        dqn = _mm(dq_raw, lw["w_uq"], lp, Q_RANK, 1024, tb=True, tm=tmb, tn=Q_RANK, tk=1024, name="mm_dqn")
        d_wuq = _mm(s["qn"], dq_raw, Q_RANK, 1024, lp, ta=True, tm=Q_RANK, tn=1024, tk=tmb, out_dtype=BF16,
                    name="mm_dwuq")
        dkvn = _mm(dkv_raw, lw["w_ukv"], lp, KV_RANK, 1536, tb=True, tm=tmb, tn=KV_RANK, tk=1536, name="mm_dkvn")
        d_wukv = _mm(s["kvn"], dkv_raw, KV_RANK, 1536, lp, ta=True, tm=KV_RANK, tn=1536, tk=tmb, out_dtype=BF16,
                     name="mm_dwukv")
        dzq, dqg, dkvg = _mla_prep_bwd(s["z"], q_norm_g[i:i + 1], kv_norm_g[i:i + 1], dqn, dkvn, dkr, lp)
        d_win = [_mm(s["h"], seg, D, seg.shape[1], lp, ta=True, tm=D, tn=tn, tk=tmb, out_dtype=BF16,
                     name="mm_dwin%d" % k) for k, (seg, tn) in enumerate(((dgl, 1024), (dzq, 512), (dzx, 1408)))]
        d_wuq_o = d_wuq.reshape(Q_RANK, HEADS, LANE)[:, :, :96].reshape(Q_RANK, HEADS * 96)
        d_wukv_o = jnp.concatenate([d_wukv[:, :1024].reshape(KV_RANK, HEADS, LANE)[:, :, :QK_NOPE],
                                    d_wukv[:, 1024:].reshape(KV_RANK, HEADS, V_DIM)], axis=2).reshape(KV_RANK, 1024)
        pending = [
            _w_in_grad_by_dest(*d_win),
            _cols_by_dest(jnp.concatenate(d_wout + [d_wukv_o], axis=0), 128),
            _cols_by_dest(d_wuq_o, 96),
            d_wo.reshape(N_DEV, 128, D)]
        res = _dh(dgl, dzq, dzx, lw["w_in"], lp, scatter=(pending, recv, 0) if i == 0 else None)
        dh = res[0]
        if i == 0:
            recv = list(res[1:])
        dx, dgpre = _rms_bwd(s["x"], pre_norm_g[i:i + 1], dh, dx, lp)

        gsm["pre_norm_g"][i] = dgpre[0]
        gsm["gate_bias"][i] = dgb[0]
        gsm["pool_w"][i] = jnp.stack([dbd[64 * g:64 * (g + 1), 64 * g:64 * (g + 1)] for g in range(4)])
        gsm["pool_scale"][i] = dsm[0]
        gsm["conf_dw_b"][i] = dsm[1]
        gsm["conf_ln_g"][i] = dsm[2]
        gsm["conf_ln_b"][i] = dsm[3]
        gsm["q_norm_g"][i] = dqg[0]
        gsm["kv_norm_g"][i] = dkvg[0]
        gsm["post_norm_g"][i] = dgpost[0]
        g_cw[i] = dcw[:CONF_K]
        g_sw[i] = dsw[:SC_K]

    outs = [dict() for _ in range(4)]

    def put(n, res):
        for t, r in zip(outs, res):
            t[n] = r

    put("w_in", _reduce_adam(recv[0], w["w_in"], mom["w_in"], vel["w_in"], 256, "adam_w_in"))
    off = 0
    for n, rows in C128:
        put(n, _reduce_adam(recv[1], w[n], mom[n], vel[n], 128, "adam_" + n, row_off=off))
        off += rows
    put("w_uq", _reduce_adam(recv[2], w["w_uq"], mom["w_uq"], vel["w_uq"], Q_RANK, "adam_w_uq"))
    put("w_o", _reduce_adam(recv[3], w["w_o"], mom["w_o"], vel["w_o"], 128, "adam_w_o"))

    tail_g = {"meta_tokens": _cols_by_dest(dx[:N_META], 128),
              "conf_dw_w": jnp.moveaxis(jnp.stack(g_cw).reshape(DEPTH, CONF_K, N_DEV, 32), 2, 0),
              "sc_dw_w": jnp.moveaxis(jnp.stack(g_sw).reshape(DEPTH, SC_K, N_DEV, 32), 2, 0)}
    tail_bd = jnp.concatenate([tail_g[n].reshape(N_DEV, -1, LANE) for n, _ in TAIL]
                              + [jnp.zeros((N_DEV, TAIL_PAD - TAIL_ROWS, LANE), F32)], axis=1)
    tail_recv = _all_to_all([tail_bd], [lax.empty((N_DEV, TAIL_PAD, LANE), F32)], None, "scatter_tail")[0]
    tail_res = _reduce_adam_flat(tail_recv, tail_w, _pack_tail(mom), _pack_tail(vel), "adam_tail")

    small_g = {n: jnp.stack(gsm[n]) for n, _ in SMALL}
    loss_row = jnp.concatenate([loss_part[0:1, 0:1], jnp.zeros((1, LANE - 1), F32)], axis=1)
    zrow = jnp.zeros((1, LANE), F32)
    parts = _all_gather([_pack_small(small_g, loss_row)], "gather_small_grads")[0]
    small_res = _reduce_adam_flat(parts, _pack_small(w, zrow), _pack_small(mom, zrow), _pack_small(vel, zrow),
                                  "adam_small")
    loss = small_res[0][SMALL_ROWS, 0]
    for t, tf, sf in zip(outs, tail_res, small_res):
        t.update(_unpack_tail(tf))
        t.update(_unpack_small(sf))
    order = ["meta_tokens", "pre_norm_g", "w_in", "gate_bias", "pool_w", "pool_scale", "w_out_pool", "q_norm_g",
             "w_uq", "kv_norm_g", "w_ukv", "w_out_mla", "conf_dw_w", "conf_dw_b", "conf_ln_g", "conf_ln_b",
             "w_out_conf", "sc_dw_w", "w_out_sc", "w_o", "post_norm_g"]
    grad_x = dx[N_META:n_real][None]
    return (loss, grad_x, *[t[n] for t in outs for n in order])
```

```python
import functools

import jax
import jax.numpy as jnp
import numpy as np
from jax import lax
from jax.experimental import pallas as pl
from jax.experimental.pallas import tpu as pltpu

F32 = jnp.float32
BF16 = jnp.bfloat16

D = 1024
N_META = 16
DEPTH = 4
EPS = 1e-6
HEADS = 8
QK_NOPE = 64
QK_ROPE = 32
V_DIM = 64
Q_RANK = 256
KV_RANK = 128
ROPE_THETA = 10000.0
SCALE = (QK_NOPE + QK_ROPE) ** -0.5
CONF_K = 31
SC_K = 3
N_DEV = 8

ADAM_LR = 0.001
ADAM_B1 = 0.9
ADAM_B2 = 0.999
ADAM_EPS = 1e-08
ADAM_WD = 0.01
ADAM_STEP = 10

RB = 384
HB = 32
LANE = 128
VMEM_LIMIT = 56 * 1024 * 1024

GL0, CQ0, CKV0, KR0, PV0, PG0, MG0, CU0, CG0, SBB0, SBC0, SBX0, SG0, ZW = (
    0, 4096, 4352, 4480, 4608, 4864, 5120, 5632, 6144, 6400, 6656, 6912, 7168, 7424)
ZSEG = ((0, 4096), (4096, 512), (4608, 2816))
LOG2E = 1.4426950408889634
LN2 = 0.6931471805599453

MESH = pl.DeviceIdType.MESH


def _cp(**kw):
    return pltpu.CompilerParams(vmem_limit_bytes=VMEM_LIMIT, **kw)


def _sig(x):
    return jax.nn.sigmoid(x)


def _silu(x):
    return x * _sig(x)


def _dsilu(x):
    s = _sig(x)
    return s * (1.0 + x * (1.0 - s))


def _dn(x, k):
    return x if k == 0 else pltpu.roll(x, k, 0)


def _up(x, k):
    return x if k == 0 else pltpu.roll(x, x.shape[0] - k, 0)


def _rope(t, c, s1, s2):
    return t * c + pltpu.roll(t, 16, 1) * s1 + pltpu.roll(t, LANE - 16, 1) * s2


def _rope_t(g, c, s1, s2):
    return g * c + pltpu.roll(g * s1, LANE - 16, 1) + pltpu.roll(g * s2, 16, 1)


def _mm(a, b, m, n, k, *, ta=False, tb=False, out_dtype=F32, tm, tn, tk, name,
        a_moff=0, a_koff=0, b_noff=0, b_koff=0, c=None, n_outer=False):
    assert m % tm == 0 and n % tn == 0 and k % tk == 0, (name, m, n, k, tm, tn, tk)
    nk = k // tk
    dims = (((0,) if ta else (1,), (1,) if tb else (0,)), ((), ()))
    has_c = c is not None

    def body(a_ref, b_ref, *rest):
        c_ref = rest[0] if has_c else None
        o_ref = rest[1] if has_c else rest[0]
        scr = rest[2:] if has_c else rest[1:]
        part = lax.dot_general(a_ref[...].astype(BF16), b_ref[...].astype(BF16), dims,
                               preferred_element_type=F32)

        def finish(total):
            if has_c:
                total = total + c_ref[...]
            o_ref[...] = total.astype(out_dtype)

        if nk == 1:
            finish(part)
        else:
            acc = scr[0]
            kk = pl.program_id(2)

            @pl.when(kk == 0)
            def _():
                acc[...] = part

            @pl.when(kk > 0)
            def _():
                acc[...] += part

            @pl.when(kk == nk - 1)
            def _():
                finish(acc[...])

    def im(f):
        return (lambda g0, g1, q: f(g1, g0, q)) if n_outer else f

    if ta:
        a_spec = pl.BlockSpec((tk, tm), im(lambda i, j, q: (q + a_koff, i + a_moff)))
    else:
        a_spec = pl.BlockSpec((tm, tk), im(lambda i, j, q: (i + a_moff, q + a_koff)))
    if tb:
        b_spec = pl.BlockSpec((tn, tk), im(lambda i, j, q: (j + b_noff, q + b_koff)))
    else:
        b_spec = pl.BlockSpec((tk, tn), im(lambda i, j, q: (q + b_koff, j + b_noff)))
    o_spec = pl.BlockSpec((tm, tn), im(lambda i, j, q: (i, j)))
    grid = (n // tn, m // tm, nk) if n_outer else (m // tm, n // tn, nk)
    return pl.pallas_call(
        body, grid=grid, in_specs=[a_spec, b_spec] + ([o_spec] if has_c else []),
        out_specs=o_spec, out_shape=jax.ShapeDtypeStruct((m, n), out_dtype),
        scratch_shapes=[pltpu.VMEM((tm, tn), F32)] if nk > 1 else [],
        name=name, compiler_params=_cp())(*((a, b, c) if has_c else (a, b)))


def _rms_fwd(x, g, lp):
    def body(x_ref, g_ref, h_ref):
        xv = x_ref[...]
        r = lax.rsqrt(jnp.mean(xv * xv, axis=-1, keepdims=True) + EPS)
        h_ref[...] = (xv * r * g_ref[...]).astype(BF16)

    return pl.pallas_call(
        body, grid=(lp // RB,),
        in_specs=[pl.BlockSpec((RB, D), lambda i: (i, 0)), pl.BlockSpec((1, D), lambda i: (0, 0))],
        out_specs=pl.BlockSpec((RB, D), lambda i: (i, 0)),
        out_shape=jax.ShapeDtypeStruct((lp, D), BF16), name="rms_fwd", compiler_params=_cp())(x, g)


def _rms_bwd(x, g, dh, dx_in, lp):
    def body(x_ref, g_ref, dh_ref, dxi_ref, dx_ref, dg_ref):
        i = pl.program_id(0)
        xv = x_ref[...]
        r = lax.rsqrt(jnp.mean(xv * xv, axis=-1, keepdims=True) + EPS)
        dy = dh_ref[...]
        a = dy * g_ref[...]
        dx_ref[...] = dxi_ref[...] + r * a - xv * (r * r * r) * jnp.mean(a * xv, axis=-1, keepdims=True)
        part = jnp.sum(dy * xv * r, axis=0, keepdims=True)

        @pl.when(i == 0)
        def _():
            dg_ref[...] = part

        @pl.when(i > 0)
        def _():
            dg_ref[...] += part

    blk = pl.BlockSpec((RB, D), lambda i: (i, 0))
    vec = pl.BlockSpec((1, D), lambda i: (0, 0))
    return pl.pallas_call(
        body, grid=(lp // RB,), in_specs=[blk, vec, blk, blk], out_specs=[blk, vec],
        out_shape=[jax.ShapeDtypeStruct((lp, D), F32), jax.ShapeDtypeStruct((1, D), F32)],
        name="rms_bwd", compiler_params=_cp())(x, g, dh, dx_in)


def _mla_prep(z, qg, kvg, tabs, lp):
    def body(z_ref, qg_ref, kvg_ref, c_ref, s1_ref, s2_ref, qn_ref, kvn_ref, kr_ref):
        cq = z_ref[:, 0:256]
        ckv = z_ref[:, 256:384]
        kr = z_ref[:, 384:512]
        rq = lax.rsqrt(jnp.mean(cq * cq, axis=-1, keepdims=True) + EPS)
        rk = lax.rsqrt(jnp.mean(ckv * ckv, axis=-1, keepdims=True) + EPS)
        qn_ref[...] = (cq * rq * qg_ref[...]).astype(BF16)
        kvn_ref[...] = (ckv * rk * kvg_ref[...]).astype(BF16)
        kr_ref[...] = _rope(kr, c_ref[...], s1_ref[...], s2_ref[...])

    tab = pl.BlockSpec((RB, LANE), lambda i: (i, 0))
    return pl.pallas_call(
        body, grid=(lp // RB,),
        in_specs=[pl.BlockSpec((RB, 512), lambda i: (i, CQ0 // 512)),
                  pl.BlockSpec((1, 256), lambda i: (0, 0)), pl.BlockSpec((1, 128), lambda i: (0, 0)),
                  tab, tab, tab],
        out_specs=[pl.BlockSpec((RB, 256), lambda i: (i, 0)), tab, tab],
        out_shape=[jax.ShapeDtypeStruct((lp, 256), BF16), jax.ShapeDtypeStruct((lp, 128), BF16),
                   jax.ShapeDtypeStruct((lp, 128), F32)],
        name="mla_prep", compiler_params=_cp())(z, qg, kvg, *tabs)


def _mla_prep_bwd(z, qg, kvg, dqn, dkvn, dkr, lp):
    def body(z_ref, qg_ref, kvg_ref, dqn_ref, dkvn_ref, dkr_ref, dz_ref, dqg_ref, dkvg_ref):
        i = pl.program_id(0)

        def rms_b(xv, g, dy):
            r = lax.rsqrt(jnp.mean(xv * xv, axis=-1, keepdims=True) + EPS)
            a = dy * g
            dx = r * a - xv * (r * r * r) * jnp.mean(a * xv, axis=-1, keepdims=True)
            return dx, jnp.sum(dy * xv * r, axis=0, keepdims=True)

        dcq, pq = rms_b(z_ref[:, 0:256], qg_ref[...], dqn_ref[...])
        dckv, pk = rms_b(z_ref[:, 256:384], kvg_ref[...], dkvn_ref[...])
        dz_ref[:, 0:256] = dcq.astype(BF16)
        dz_ref[:, 256:384] = dckv.astype(BF16)
        dz_ref[:, 384:512] = dkr_ref[...].astype(BF16)

        @pl.when(i == 0)
        def _():
            dqg_ref[...] = pq
            dkvg_ref[...] = pk

        @pl.when(i > 0)
        def _():
            dqg_ref[...] += pq
            dkvg_ref[...] += pk

    tab = pl.BlockSpec((RB, LANE), lambda i: (i, 0))
    return pl.pallas_call(
        body, grid=(lp // RB,),
        in_specs=[pl.BlockSpec((RB, 512), lambda i: (i, CQ0 // 512)),
                  pl.BlockSpec((1, 256), lambda i: (0, 0)), pl.BlockSpec((1, 128), lambda i: (0, 0)),
                  pl.BlockSpec((RB, 256), lambda i: (i, 0)), tab, tab],
        out_specs=[pl.BlockSpec((RB, 512), lambda i: (i, 0)),
                   pl.BlockSpec((1, 256), lambda i: (0, 0)), pl.BlockSpec((1, 128), lambda i: (0, 0))],
        out_shape=[jax.ShapeDtypeStruct((lp, 512), BF16), jax.ShapeDtypeStruct((1, 256), F32),
                   jax.ShapeDtypeStruct((1, 128), F32)],
        name="mla_prep_bwd", compiler_params=_cp())(z, qg, kvg, dqn, dkvn, dkr)


def _mla_post(q_raw, kv_raw, krr, tabs, lp):
    def body(q_ref, kv_ref, kr_ref, c_ref, s1_ref, s2_ref, qo_ref, ko_ref, vo_ref):
        c, s1, s2, kr = c_ref[...], s1_ref[...], s2_ref[...], kr_ref[...]
        for h in range(HEADS):
            sl = slice(LANE * h, LANE * (h + 1))
            qo_ref[:, sl] = (_rope(q_ref[:, sl], c, s1, s2) * (SCALE * LOG2E)).astype(BF16)
            ko_ref[:, sl] = (kv_ref[:, sl] + kr).astype(BF16)
        vo_ref[...] = kv_ref[:, 1024:1536].astype(BF16)

    tab = pl.BlockSpec((RB, LANE), lambda i: (i, 0))
    wide = pl.BlockSpec((RB, 1024), lambda i: (i, 0))
    return pl.pallas_call(
        body, grid=(lp // RB,),
        in_specs=[wide, pl.BlockSpec((RB, 1536), lambda i: (i, 0)), tab, tab, tab, tab],
        out_specs=[wide, wide, pl.BlockSpec((RB, 512), lambda i: (i, 0))],
        out_shape=[jax.ShapeDtypeStruct((lp, 1024), BF16), jax.ShapeDtypeStruct((lp, 1024), BF16),
                   jax.ShapeDtypeStruct((lp, 512), BF16)],
        name="mla_post", compiler_params=_cp())(q_raw, kv_raw, krr, *tabs)


def _mla_post_bwd(dq, dk, dv, tabs, lp):
    def body(dq_ref, dk_ref, dv_ref, c_ref, s1_ref, s2_ref, dqr_ref, dkv_ref, dkr_ref):
        c, s1, s2 = c_ref[...], s1_ref[...], s2_ref[...]
        lane = lax.broadcasted_iota(jnp.int32, (1, LANE), 1)
        ropel = (lane >= QK_NOPE) & (lane < QK_NOPE + QK_ROPE)
        ksum = jnp.zeros((RB, LANE), F32)
        for h in range(HEADS):
            sl = slice(LANE * h, LANE * (h + 1))
            dqr_ref[:, sl] = _rope_t(dq_ref[:, sl].astype(F32) * SCALE, c, s1, s2).astype(BF16)
            dkt = dk_ref[:, sl]
            dkv_ref[:, sl] = dkt
            ksum = ksum + dkt.astype(F32)
        dkv_ref[:, 1024:1536] = dv_ref[...]
        dkr_ref[...] = jnp.where(ropel, _rope_t(jnp.where(ropel, ksum, 0.0), c, s1, s2), 0.0)

    tab = pl.BlockSpec((RB, LANE), lambda i: (i, 0))
    wide = pl.BlockSpec((RB, 1024), lambda i: (i, 0))
    return pl.pallas_call(
        body, grid=(lp // RB,),
        in_specs=[wide, wide, pl.BlockSpec((RB, 512), lambda i: (i, 0)), tab, tab, tab],
        out_specs=[wide, pl.BlockSpec((RB, 1536), lambda i: (i, 0)), tab],
        out_shape=[jax.ShapeDtypeStruct((lp, 1024), BF16), jax.ShapeDtypeStruct((lp, 1536), BF16),
                   jax.ShapeDtypeStruct((lp, 128), F32)],
        name="mla_post_bwd", compiler_params=_cp())(dq, dk, dv, *tabs)


def _head_lanes(e):
    lane = lax.broadcasted_iota(jnp.int32, (1, LANE), 1)
    return lane >= V_DIM if e else lane < V_DIM


ONE_LANE = (V_DIM, 0)


def _attn_fwd(q, k, v, lp, gather=()):
    nq = lp // RB
    n = len(gather)
    steps = HEADS // 2

    def body(q_ref, k_ref, v_ref, *rest):
        o_ref, lse_ref = rest[n], rest[n + 1]
        vm_scr = rest[2 * n + 2]
        if n:
            g_start, g_forward, g_finish = _gather_phases(rest[:n], rest[n + 2:2 * n + 2], *rest[2 * n + 3:])
            pl.when(pl.program_id(0) == 0)(g_start)
            pl.when(pl.program_id(0) == steps - 1)(g_forward)
        lane = lax.broadcasted_iota(jnp.int32, (1, LANE), 1)
        vv = v_ref[...]
        for e in range(2):
            ones = jnp.where(lane == ONE_LANE[e], 1.0, 0.0).astype(BF16)
            vm_scr[e] = jnp.where(_head_lanes(e), vv, jnp.broadcast_to(ones, vv.shape))
        causal = (lax.broadcasted_iota(jnp.int32, (RB, RB), 1) <= lax.broadcasted_iota(jnp.int32, (RB, RB), 0))

        def qblock(i, _):
            rows = pl.ds(pl.multiple_of(i * RB, RB), RB)
            qs = [q_ref[rows, LANE * e:LANE * (e + 1)] for e in range(2)]

            def scores(j):
                cols = pl.ds(pl.multiple_of(j * RB, RB), RB)
                return tuple(lax.dot_general(qs[e], k_ref[cols, LANE * e:LANE * (e + 1)], (((1,), (1,)), ((), ())),
                                             preferred_element_type=F32) for e in range(2))

            def update(j, s, carry, masked):
                cols = pl.ds(pl.multiple_of(j * RB, RB), RB)
                out = []
                for e in range(2):
                    m, acc = carry[2 * e], carry[2 * e + 1]
                    se = jnp.where(causal, s[e], -jnp.inf) if masked else s[e]
                    m_new = jnp.maximum(m, jnp.max(se, axis=-1, keepdims=True))
                    p = jnp.exp2((se - m_new).astype(BF16))
                    acc = jnp.exp2(m - m_new) * acc + jnp.dot(p, vm_scr[e, cols, :], preferred_element_type=F32)
                    out += [m_new, acc]
                return tuple(out)

            def pair(jj, c):
                sa, sb = scores(2 * jj), scores(2 * jj + 1)
                return update(2 * jj + 1, sb, update(2 * jj, sa, c, False), False)

            m0 = jnp.full((RB, 1), -jnp.inf, F32)
            a0 = jnp.zeros((RB, LANE), F32)
            carry = lax.fori_loop(0, i // 2, pair, (m0, a0, m0, a0))
            carry = lax.cond(i % 2 == 1, lambda c: update(i - 1, scores(i - 1), c, False), lambda c: c, carry)
            carry = update(i, scores(i), carry, True)
            o, lse = [], []
            for e in range(2):
                m, acc = carry[2 * e], carry[2 * e + 1]
                l = acc[:, ONE_LANE[e]:ONE_LANE[e] + 1]
                o.append(acc / l)
                lse.append(jnp.broadcast_to(m + jnp.log2(l), (RB, LANE)))
            o_ref[rows, :] = jnp.where(_head_lanes(0), o[0], o[1])
            lse_ref[rows, :] = jnp.where(_head_lanes(0), lse[0], lse[1])
            return 0

        lax.fori_loop(0, nq, qblock, 0)
        if n:
            pl.when(pl.program_id(0) == steps - 1)(g_finish)

    two = pl.BlockSpec((lp, 2 * LANE), lambda h: (0, h))
    one = pl.BlockSpec((lp, LANE), lambda h: (0, h))
    anyspec = pl.BlockSpec(memory_space=pl.ANY)
    return pl.pallas_call(
        body, grid=(steps,), in_specs=[two, two, one] + [anyspec] * n, out_specs=[one, one] + [anyspec] * n,
        out_shape=[jax.ShapeDtypeStruct((lp, 512), F32), jax.ShapeDtypeStruct((lp, 512), F32)]
        + _gather_shapes(gather),
        scratch_shapes=[pltpu.VMEM((2, lp, LANE), BF16)] + (_comm_sems(n) if n else []),
        name="attn_fwd_gather" if n else "attn_fwd", compiler_params=_cp())(q, k, v, *gather)


def _attn_bwd(q, k, v, o, do, lse, lp, scatter=None):
    nq = lp // RB
    xs, bufs, layer = scatter if scatter else ((), (), None)
    n = len(xs)
    steps = HEADS // 2

    def body(q_ref, k_ref, v_ref, o_ref, do_ref, lse_ref, *rest):
        dq_ref, dk_ref, dv_ref = rest[2 * n:2 * n + 3]
        vm_scr, dom_scr, dl_scr, dq_scr = rest[3 * n + 3:3 * n + 7]
        if n:
            s_start, s_finish = _scatter_phases(rest[:n], rest[2 * n + 3:3 * n + 3], *rest[3 * n + 7:], layer)
            pl.when(pl.program_id(0) == 0)(s_start)
        causal = (lax.broadcasted_iota(jnp.int32, (RB, RB), 1) <= lax.broadcasted_iota(jnp.int32, (RB, RB), 0))
        vv = v_ref[...]
        for e in range(2):
            vm_scr[e] = jnp.where(_head_lanes(e), vv, jnp.zeros_like(vv))

        def prep(i, _):
            rows = pl.ds(pl.multiple_of(i * RB, RB), RB)
            prod = do_ref[rows, :] * o_ref[rows, :]
            dls = []
            for e in range(2):
                hm = _head_lanes(e)
                dom_scr[e, rows, :] = jnp.where(hm, do_ref[rows, :], 0.0).astype(BF16)
                dls.append(jnp.sum(jnp.where(hm, prod, 0.0), axis=-1, keepdims=True))
            dl_scr[rows, :] = jnp.where(_head_lanes(0), dls[0], dls[1])
            dq_scr[rows, :] = jnp.zeros((RB, 2 * LANE), F32)
            return 0

        lax.fori_loop(0, nq, prep, 0)

        def kvblock(j, _):
            cols = pl.ds(pl.multiple_of(j * RB, RB), RB)
            kbs = [k_ref[cols, LANE * e:LANE * (e + 1)] for e in range(2)]

            def products(i):
                rows = pl.ds(pl.multiple_of(i * RB, RB), RB)
                out = []
                for e in range(2):
                    out.append(lax.dot_general(q_ref[rows, LANE * e:LANE * (e + 1)], kbs[e],
                                               (((1,), (1,)), ((), ())), preferred_element_type=F32))
                    out.append(lax.dot_general(dom_scr[e, rows, :], vm_scr[e, cols, :],
                                               (((1,), (1,)), ((), ())), preferred_element_type=F32))
                return tuple(out)

            def update(i, sd, carry, masked):
                rows = pl.ds(pl.multiple_of(i * RB, RB), RB)
                out = []
                for e in range(2):
                    dk, dv = carry[2 * e], carry[2 * e + 1]
                    sl = slice(LANE * e, LANE * (e + 1))
                    col1 = slice(V_DIM * e, V_DIM * e + 1)
                    s, dp = sd[2 * e], sd[2 * e + 1]
                    if masked:
                        s = jnp.where(causal, s, -jnp.inf)
                    p = jnp.exp2((s - lse_ref[rows, col1]).astype(BF16))
                    dv = dv + lax.dot_general(p, dom_scr[e, rows, :], (((0,), (0,)), ((), ())),
                                              preferred_element_type=F32)
                    ds = p * (dp - dl_scr[rows, col1]).astype(BF16)
                    dk = dk + lax.dot_general(ds, q_ref[rows, sl], (((0,), (0,)), ((), ())),
                                              preferred_element_type=F32)
                    dq_scr[rows, sl] += jnp.dot(ds, kbs[e], preferred_element_type=F32)
                    out += [dk, dv]
                return tuple(out)

            def pair(t, c):
                i0 = j + 1 + 2 * t
                pa, pb = products(i0), products(i0 + 1)
                return update(i0 + 1, pb, update(i0, pa, c, False), False)

            zero = jnp.zeros((RB, LANE), F32)
            carry = update(j, products(j), (zero, zero, zero, zero), True)
            below = nq - 1 - j
            carry = lax.fori_loop(0, below // 2, pair, carry)
            dk0, dv0, dk1, dv1 = lax.cond(below % 2 == 1,
                                          lambda c: update(nq - 1, products(nq - 1), c, False), lambda c: c, carry)
            dk_ref[cols, 0:LANE] = (dk0 * LN2).astype(BF16)
            dk_ref[cols, LANE:2 * LANE] = (dk1 * LN2).astype(BF16)
            dv_ref[cols, :] = (dv0 + dv1).astype(BF16)
            return 0

        lax.fori_loop(0, nq, kvblock, 0)

        def fin(i, _):
            rows = pl.ds(pl.multiple_of(i * RB, RB), RB)
            dq_ref[rows, :] = dq_scr[rows, :].astype(BF16)
            return 0

        lax.fori_loop(0, nq, fin, 0)
        if n:
            pl.when(pl.program_id(0) == steps - 1)(s_finish)

    two = pl.BlockSpec((lp, 2 * LANE), lambda h: (0, h))
    one = pl.BlockSpec((lp, LANE), lambda h: (0, h))
    anyspec = pl.BlockSpec(memory_space=pl.ANY)
    return pl.pallas_call(
        body, grid=(steps,), in_specs=[two, two, one, one, one, one] + [anyspec] * (2 * n),
        out_specs=[two, two, one] + [anyspec] * n,
        out_shape=[jax.ShapeDtypeStruct((lp, 1024), BF16), jax.ShapeDtypeStruct((lp, 1024), BF16),
                   jax.ShapeDtypeStruct((lp, 512), BF16)] + [jax.ShapeDtypeStruct(b.shape, b.dtype) for b in bufs],
        input_output_aliases={6 + n + a: 3 + a for a in range(n)},
        scratch_shapes=[pltpu.VMEM((2, lp, LANE), BF16), pltpu.VMEM((2, lp, LANE), BF16),
                        pltpu.VMEM((lp, LANE), F32), pltpu.VMEM((lp, 2 * LANE), F32)]
        + (_comm_sems(n) if n else []),
        name="attn_bwd_scatter" if n else "attn_bwd", compiler_params=_cp())(q, k, v, o, do, lse, *xs, *bufs)


def _pool_lane_windows():
    lane = lax.broadcasted_iota(jnp.int32, (1, 256), 1)
    return jnp.where(lane < 64, 2, jnp.where(lane < 128, 4, jnp.where(lane < 192, 8, 16)))


def _by_window(wl, s2, s4, s8, s16):
    return jnp.where(wl == 2, s2, jnp.where(wl == 4, s4, jnp.where(wl == 8, s8, s16)))


def _pool_fwd_rows(pv_ext, t0):
    n = pv_ext.shape[0]
    wl = _pool_lane_windows()
    s2 = pv_ext + _dn(pv_ext, 1)
    s4 = s2 + _dn(s2, 2)
    s8 = s4 + _dn(s4, 4)
    s16 = s8 + _dn(s8, 8)
    t = t0 + lax.broadcasted_iota(jnp.int32, (n, 1), 0)
    cnt = jnp.maximum(jnp.minimum(t + 1, wl), 1).astype(F32)
    return _by_window(wl, s2, s4, s8, s16) / cnt - pv_ext


def _conv_dn(x_ext, w_ref, taps):
    acc = w_ref[taps - 1:taps, :] * x_ext
    for j in range(1, taps):
        acc = acc + w_ref[taps - 1 - j:taps - j, :] * _dn(x_ext, j)
    return acc


def _conv_up(g_ext, w_ref, taps):
    acc = w_ref[taps - 1:taps, :] * g_ext
    for j in range(1, taps):
        acc = acc + w_ref[taps - 1 - j:taps - j, :] * _up(g_ext, j)
    return acc


def _ln_fwd(c, g, b):
    mu = jnp.mean(c, axis=-1, keepdims=True)
    xc = c - mu
    r = lax.rsqrt(jnp.mean(xc * xc, axis=-1, keepdims=True) + EPS)
    xh = xc * r
    return xh * g + b, xh, r


def _halo_specs(lp, width, col):
    per = RB // HB
    last = lp // HB - 1
    cur = pl.BlockSpec((RB, width), lambda i: (i, col))
    prev = pl.BlockSpec((HB, width), lambda i: (jnp.maximum(i * per - 1, 0), col))
    nxt = pl.BlockSpec((HB, width), lambda i: (jnp.minimum((i + 1) * per, last), col))
    return cur, prev, nxt


def _mix_fwd(z, oat, bd, pscale, cw, cb, lng, lnb, sw, lp):
    def body(za, zah, mg, oat_ref, cu, cuh, cg, sbb, sbc, sbch, sbx, sbxh, sg,
             bd_ref, ps_ref, cw_ref, cb_ref, lng_ref, lnb_ref, sw_ref, u_ref):
        i = pl.program_id(0)
        pm = jnp.where(i > 0, 1.0, 0.0).astype(F32)
        pv = jnp.concatenate([zah[:, 0:256] * pm, za[:, 0:256]], axis=0)
        p = _pool_fwd_rows(pv, i * RB - HB)[HB:]
        y = jnp.dot(p.astype(BF16), bd_ref[...], preferred_element_type=F32)
        u_ref[:, 0:256] = (y * ps_ref[...] * _silu(za[:, 256:512])).astype(BF16)
        u_ref[:, 256:768] = (oat_ref[...] * _silu(mg[...])).astype(BF16)
        ce = jnp.concatenate([cuh[...] * pm, cu[...]], axis=0)
        glu = ce[:, 0:256] * _sig(ce[:, 256:512])
        c = _conv_dn(glu, cw_ref, CONF_K)[HB:] + cb_ref[...]
        n, _, _ = _ln_fwd(c, lng_ref[...], lnb_ref[...])
        u_ref[:, 768:1024] = (_silu(n) * _silu(cg[...])).astype(BF16)
        qe = jnp.concatenate([sbch[...] * sbxh[...] * pm, sbc[...] * sbx[...]], axis=0)
        cv = _conv_dn(qe, sw_ref, SC_K)[HB:]
        u_ref[:, 1024:1280] = (sbb[...] * cv * _silu(sg[...])).astype(BF16)

    a_cur, a_prev, _ = _halo_specs(lp, 512, PV0 // 512)
    cu_cur, cu_prev, _ = _halo_specs(lp, 512, CU0 // 512)
    sc_cur, sc_prev, _ = _halo_specs(lp, 256, SBC0 // 256)
    sx_cur, sx_prev, _ = _halo_specs(lp, 256, SBX0 // 256)
    c256 = lambda c0: pl.BlockSpec((RB, 256), lambda i: (i, c0 // 256))
    full = lambda r, c: pl.BlockSpec((r, c), lambda i: (0, 0))
    return pl.pallas_call(
        body, grid=(lp // RB,),
        in_specs=[a_cur, a_prev, pl.BlockSpec((RB, 512), lambda i: (i, MG0 // 512)),
                  pl.BlockSpec((RB, 512), lambda i: (i, 0)),
                  cu_cur, cu_prev, c256(CG0), c256(SBB0), sc_cur, sc_prev, sx_cur, sx_prev, c256(SG0),
                  full(256, 256), full(1, 256), full(32, 256), full(1, 256), full(1, 256), full(1, 256),
                  full(8, 256)],
        out_specs=pl.BlockSpec((RB, 1280), lambda i: (i, 0)),
        out_shape=jax.ShapeDtypeStruct((lp, 1280), BF16),
        name="mix_fwd", compiler_params=_cp())(z, z, z, oat, z, z, z, z, z, z, z, z, z,
                                               bd, pscale, cw, cb, lng, lnb, sw)


def _mix_bwd(z, oat, du, bd, pscale, cw, cb, lng, lnb, sw, lp):
    nb = lp // RB
    ne = RB + 2 * HB
    nf = RB + HB

    def body(za, zah, zan, mg, oat_ref, cu, cuh, cun, cg, cgn, sbb, sbbn, sbc, sbch, sbcn, sbx, sbxh, sbxn,
             sg, sgn, du_ref, dun_ref, bd_ref, ps_ref, cw_ref, cb_ref, lng_ref, lnb_ref, sw_ref,
             dzx_ref, doat_ref, dbd_ref, dcw_ref, dsw_ref, dsm_ref):
        xa, xm, xc = 0, MG0 - PV0, CU0 - PV0
        i = pl.program_id(0)
        pm = jnp.where(i > 0, 1.0, 0.0).astype(F32)
        nm = jnp.where(i < nb - 1, 1.0, 0.0).astype(F32)

        def ext(cur, prev, nxt, sl=slice(None)):
            return jnp.concatenate([prev[:, sl] * pm, cur[:, sl], nxt[:, sl] * nm], axis=0)

        def fwd(cur, nxt, sl=slice(None)):
            return jnp.concatenate([cur[:, sl], nxt[:, sl] * nm], axis=0)

        def csum(x):
            return jnp.sum(x, axis=0, keepdims=True)

        @pl.when(i == 0)
        def _():
            dbd_ref[...] = jnp.zeros((256, 256), F32)
            dcw_ref[...] = jnp.zeros((32, 256), F32)
            dsw_ref[...] = jnp.zeros((8, 256), F32)
            dsm_ref[...] = jnp.zeros((8, 256), F32)

        a_cols, b_cols = slice(0, 256), slice(256, 512)
        pv_e = ext(za, zah, zan, a_cols)
        p = _pool_fwd_rows(pv_e, i * RB - HB)[HB:HB + RB]
        pb = p.astype(BF16)
        y = jnp.dot(pb, bd_ref[...], preferred_element_type=F32)
        pg_f = fwd(za, zan, b_cols)
        dua_f = fwd(du_ref, dun_ref, slice(0, 256))
        dyp_f = dua_f * ps_ref[...] * _silu(pg_f)
        dypb = dyp_f.astype(BF16)
        dp_f = lax.dot_general(dypb, bd_ref[...], (((1,), (1,)), ((), ())), preferred_element_type=F32)
        wl = _pool_lane_windows()
        t = i * RB + lax.broadcasted_iota(jnp.int32, (nf, 1), 0)
        cnt = jnp.minimum(t + 1, wl).astype(F32)
        qf = dp_f / cnt
        f2 = qf + _up(qf, 1)
        f4 = f2 + _up(f2, 2)
        f8 = f4 + _up(f4, 4)
        f16 = f8 + _up(f8, 8)
        dpv = (_by_window(wl, f2, f4, f8, f16) - dp_f)[0:RB]
        dua = du_ref[:, 0:256]
        pg = za[:, b_cols]
        dpg = dua * y * ps_ref[...] * _dsilu(pg)
        dzx_ref[:, xa:xa + 256] = dpv.astype(BF16)
        dzx_ref[:, xa + 256:xa + 512] = dpg.astype(BF16)
        d_scale = csum(dua * y * _silu(pg))
        d_bd = lax.dot_general(pb, dypb[0:RB], (((0,), (0,)), ((), ())), preferred_element_type=F32)

        dub = du_ref[:, 256:768]
        mgv = mg[...]
        dzx_ref[:, xm:xm + 512] = (dub * oat_ref[...] * _dsilu(mgv)).astype(BF16)
        doat_ref[...] = dub * _silu(mgv)

        a_e = ext(cu, cuh, cun, slice(0, 256))
        gt_e = ext(cu, cuh, cun, slice(256, 512))
        sg_e = _sig(gt_e)
        glu_e = a_e * sg_e
        c_f = _conv_dn(glu_e, cw_ref, CONF_K)[HB:] + cb_ref[...]
        n_f, xh_f, r_f = _ln_fwd(c_f, lng_ref[...], lnb_ref[...])
        cg_f = fwd(cg, cgn)
        duc_f = fwd(du_ref, dun_ref, slice(768, 1024))
        sw_f = _silu(n_f)
        dcg = (duc_f * sw_f * _dsilu(cg_f))[0:RB]
        dn_f = duc_f * _silu(cg_f) * _dsilu(n_f)
        a_f = dn_f * lng_ref[...]
        dc_f = r_f * (a_f - jnp.mean(a_f, axis=-1, keepdims=True)
                      - xh_f * jnp.mean(a_f * xh_f, axis=-1, keepdims=True))
        d_lng = csum((dn_f * xh_f)[0:RB])
        d_lnb = csum(dn_f[0:RB])
        d_cb = csum(dc_f[0:RB])
        dglu = _conv_up(dc_f, cw_ref, CONF_K)[0:RB]
        dc_c = dc_f[0:RB]
        for kk in range(CONF_K):
            j = CONF_K - 1 - kk
            dcw_ref[kk:kk + 1, :] += csum(dc_c * _dn(glu_e, j)[HB:HB + RB])

        sgc = sg_e[HB:HB + RB]
        a_c = a_e[HB:HB + RB]
        dzx_ref[:, xc:xc + 256] = (dglu * sgc).astype(BF16)
        dzx_ref[:, xc + 256:xc + 512] = (dglu * a_c * sgc * (1.0 - sgc)).astype(BF16)
        dzx_ref[:, xc + 512:xc + 768] = dcg.astype(BF16)

        c_e = ext(sbc, sbch, sbcn)
        x_e = ext(sbx, sbxh, sbxn)
        q_e = c_e * x_e
        cv_f = _conv_dn(q_e, sw_ref, SC_K)[HB:]
        bg_f = fwd(sbb, sbbn)
        sg_f = fwd(sg, sgn)
        dud_f = fwd(du_ref, dun_ref, slice(1024, 1280))
        ssg_f = _silu(sg_f)
        dcv_f = dud_f * bg_f * ssg_f
        dbg = (dud_f * cv_f * ssg_f)[0:RB]
        dsg = (dud_f * bg_f * cv_f * _dsilu(sg_f))[0:RB]
        dq = _conv_up(dcv_f, sw_ref, SC_K)[0:RB]
        dcv_c = dcv_f[0:RB]
        for kk in range(SC_K):
            j = SC_K - 1 - kk
            dsw_ref[kk:kk + 1, :] += csum(dcv_c * _dn(q_e, j)[HB:HB + RB])

        dzx_ref[:, xc + 768:xc + 1024] = dbg.astype(BF16)
        dzx_ref[:, xc + 1024:xc + 1280] = (dq * x_e[HB:HB + RB]).astype(BF16)
        dzx_ref[:, xc + 1280:xc + 1536] = (dq * c_e[HB:HB + RB]).astype(BF16)
        dzx_ref[:, xc + 1536:xc + 1792] = dsg.astype(BF16)

        dbd_ref[...] += d_bd
        dsm_ref[0:1, :] += d_scale
        dsm_ref[1:2, :] += d_cb
        dsm_ref[2:3, :] += d_lng
        dsm_ref[3:4, :] += d_lnb

    a3 = _halo_specs(lp, 512, PV0 // 512)
    cu3 = _halo_specs(lp, 512, CU0 // 512)
    cg3 = _halo_specs(lp, 256, CG0 // 256)
    sbb3 = _halo_specs(lp, 256, SBB0 // 256)
    sbc3 = _halo_specs(lp, 256, SBC0 // 256)
    sbx3 = _halo_specs(lp, 256, SBX0 // 256)
    sg3 = _halo_specs(lp, 256, SG0 // 256)
    du3 = _halo_specs(lp, 1280, 0)
    full = lambda r, c: pl.BlockSpec((r, c), lambda i: (0, 0))
    in_specs = [a3[0], a3[1], a3[2], pl.BlockSpec((RB, 512), lambda i: (i, MG0 // 512)),
                pl.BlockSpec((RB, 512), lambda i: (i, 0)),
                cu3[0], cu3[1], cu3[2], cg3[0], cg3[2], sbb3[0], sbb3[2],
                sbc3[0], sbc3[1], sbc3[2], sbx3[0], sbx3[1], sbx3[2], sg3[0], sg3[2],
                du3[0], du3[2],
                full(256, 256), full(1, 256), full(32, 256), full(1, 256), full(1, 256), full(1, 256),
                full(8, 256)]
    out_specs = [pl.BlockSpec((RB, ZW - PV0), lambda i: (i, 0)), pl.BlockSpec((RB, 512), lambda i: (i, 0)),
                 full(256, 256), full(32, 256), full(8, 256), full(8, 256)]
    out_shape = [jax.ShapeDtypeStruct((lp, ZW - PV0), BF16), jax.ShapeDtypeStruct((lp, 512), F32),
                 jax.ShapeDtypeStruct((256, 256), F32), jax.ShapeDtypeStruct((32, 256), F32),
                 jax.ShapeDtypeStruct((8, 256), F32), jax.ShapeDtypeStruct((8, 256), F32)]
    return pl.pallas_call(
        body, grid=(nb,), in_specs=in_specs, out_specs=out_specs, out_shape=out_shape,
        name="mix_bwd", compiler_params=_cp())(
            z, z, z, z, oat, z, z, z, z, z, z, z, z, z, z, z, z, z, z, z, du, du,
            bd, pscale, cw, cb, lng, lnb, sw)


U_OFF = (0, 256, 768, 1024, 1280)
MRB = 192


def _merge_fwd(x, u, z, gb, wout, wo, gpost, lp):
    MRB = RB

    def body(x_ref, u_ref, gl_ref, gb_ref, wout_ref, wo_ref, g_ref, xo_ref, m_ref, o2_ref):
        m = jnp.zeros((MRB, D), F32)
        for b in range(4):
            y = jnp.dot(u_ref[:, U_OFF[b]:U_OFF[b + 1]], wout_ref[U_OFF[b]:U_OFF[b + 1], :],
                        preferred_element_type=F32)
            sl = slice(D * b, D * (b + 1))
            m = m + _sig(gl_ref[:, sl] + gb_ref[:, sl]) * y
        mb = m.astype(BF16)
        m_ref[...] = mb
        o2 = jnp.dot(mb, wo_ref[...], preferred_element_type=F32)
        o2_ref[...] = o2
        r = lax.rsqrt(jnp.mean(o2 * o2, axis=-1, keepdims=True) + EPS)
        xo_ref[...] = x_ref[...] + o2 * r * g_ref[...]

    blk = pl.BlockSpec((MRB, D), lambda i: (i, 0))
    full = lambda r, c: pl.BlockSpec((r, c), lambda i: (0, 0))
    return pl.pallas_call(
        body, grid=(lp // MRB,),
        in_specs=[blk, pl.BlockSpec((MRB, 1280), lambda i: (i, 0)), pl.BlockSpec((MRB, 4096), lambda i: (i, 0)),
                  full(1, 4096), full(1280, D), full(D, D), full(1, D)],
        out_specs=[blk, blk, blk],
        out_shape=[jax.ShapeDtypeStruct((lp, D), F32), jax.ShapeDtypeStruct((lp, D), BF16),
                   jax.ShapeDtypeStruct((lp, D), F32)],
        name="merge_fwd", compiler_params=_cp())(x, u, z, gb, wout, wo, gpost)


def _merge_bwd(dx, o2, u, z, gb, wout, wo, gpost, lp):
    def body(dx_ref, o2_ref, u_ref, gl_ref, gb_ref, wout_ref, wo_ref, g_ref,
             do2_ref, dgl_ref, dy_ref, du_ref, dgb_ref, dg_ref):
        i = pl.program_id(0)
        o2 = o2_ref[...]
        dy = dx_ref[...]
        r = lax.rsqrt(jnp.mean(o2 * o2, axis=-1, keepdims=True) + EPS)
        a = dy * g_ref[...]
        do2 = (r * a - o2 * (r * r * r) * jnp.mean(a * o2, axis=-1, keepdims=True)).astype(BF16)
        do2_ref[...] = do2
        dg = jnp.sum(dy * o2 * r, axis=0, keepdims=True)
        dm = lax.dot_general(do2, wo_ref[...], (((1,), (1,)), ((), ())), preferred_element_type=F32)
        for b in range(4):
            rows = slice(U_OFF[b], U_OFF[b + 1])
            y = jnp.dot(u_ref[:, rows], wout_ref[rows, :], preferred_element_type=F32)
            sl = slice(D * b, D * (b + 1))
            gt = _sig(gl_ref[:, sl] + gb_ref[:, sl])
            dgl = dm * y * gt * (1.0 - gt)
            dgl_ref[:, sl] = dgl.astype(BF16)
            part = jnp.sum(dgl, axis=0, keepdims=True)

            @pl.when(i == 0)
            def _(part=part, sl=sl):
                dgb_ref[:, sl] = part

            @pl.when(i > 0)
            def _(part=part, sl=sl):
                dgb_ref[:, sl] += part

            dyb = (dm * gt).astype(BF16)
            dy_ref[:, sl] = dyb
            du_ref[:, rows] = lax.dot_general(dyb, wout_ref[rows, :], (((1,), (1,)), ((), ())),
                                              preferred_element_type=F32)

        @pl.when(i == 0)
        def _():
            dg_ref[...] = dg

        @pl.when(i > 0)
        def _():
            dg_ref[...] += dg

    blk = pl.BlockSpec((MRB, D), lambda i: (i, 0))
    wide = pl.BlockSpec((MRB, 4096), lambda i: (i, 0))
    ub = pl.BlockSpec((MRB, 1280), lambda i: (i, 0))
    full = lambda r, c: pl.BlockSpec((r, c), lambda i: (0, 0))
    return pl.pallas_call(
        body, grid=(lp // MRB,),
        in_specs=[blk, blk, ub, wide, full(1, 4096), full(1280, D), full(D, D), full(1, D)],
        out_specs=[blk, wide, wide, ub, full(1, 4096), full(1, D)],
        out_shape=[jax.ShapeDtypeStruct((lp, D), BF16), jax.ShapeDtypeStruct((lp, 4096), BF16),
                   jax.ShapeDtypeStruct((lp, 4096), BF16), jax.ShapeDtypeStruct((lp, 1280), F32),
                   jax.ShapeDtypeStruct((1, 4096), F32), jax.ShapeDtypeStruct((1, D), F32)],
        name="merge_bwd", compiler_params=_cp())(dx, o2, u, z, gb, wout, wo, gpost)


def _dh(dgl, dzq, dzx, w_in, lp, scatter=None):
    xs, bufs, layer = scatter if scatter else ((), (), None)
    n = len(xs)
    steps = lp // RB
    segs = ((0, CQ0), (CQ0, PV0), (PV0, ZW))

    def body(gl_ref, zq_ref, zx_ref, w_ref, *rest):
        o_ref = rest[2 * n]
        if n:
            s_start, s_finish = _scatter_phases(rest[:n], rest[2 * n + 1:3 * n + 1], *rest[3 * n + 1:], layer)
            pl.when(pl.program_id(0) == 0)(s_start)
        acc = None
        for a_ref, (lo, hi) in zip((gl_ref, zq_ref, zx_ref), segs):
            part = lax.dot_general(a_ref[...], w_ref[:, lo:hi], (((1,), (1,)), ((), ())),
                                   preferred_element_type=F32)
            acc = part if acc is None else acc + part
        o_ref[...] = acc
        if n:
            pl.when(pl.program_id(0) == steps - 1)(s_finish)

    anyspec = pl.BlockSpec(memory_space=pl.ANY)
    row = lambda w: pl.BlockSpec((RB, w), lambda i: (i, 0))
    return pl.pallas_call(
        body, grid=(steps,),
        in_specs=[row(CQ0), row(PV0 - CQ0), row(ZW - PV0), pl.BlockSpec((D, ZW), lambda i: (0, 0))]
        + [anyspec] * (2 * n),
        out_specs=[row(D)] + [anyspec] * n,
        out_shape=[jax.ShapeDtypeStruct((lp, D), F32)] + [jax.ShapeDtypeStruct(b.shape, b.dtype) for b in bufs],
        input_output_aliases={4 + n + a: 1 + a for a in range(n)},
        scratch_shapes=_comm_sems(n) if n else [],
        name="dh_scatter" if n else "dh", compiler_params=_cp())(dgl, dzq, dzx, w_in, *xs, *bufs)


def _loss_head(xf, tgt, n_real, lp):
    def body(x_ref, t_ref, dy_ref, ls_ref):
        i = pl.program_id(0)
        t = i * RB + lax.broadcasted_iota(jnp.int32, (RB, 1), 0)
        real = (t >= N_META) & (t < n_real)
        err = jnp.where(real, x_ref[...] - t_ref[...], 0.0)
        dy_ref[...] = err / D
        part = 0.5 * jnp.sum(jnp.mean(err * err, axis=-1, keepdims=True), axis=0, keepdims=True)
        part = jnp.broadcast_to(part, (8, LANE))

        @pl.when(i == 0)
        def _():
            ls_ref[...] = part

        @pl.when(i > 0)
        def _():
            ls_ref[...] += part

    blk = pl.BlockSpec((RB, D), lambda i: (i, 0))
    return pl.pallas_call(
        body, grid=(lp // RB,), in_specs=[blk, blk],
        out_specs=[blk, pl.BlockSpec((8, LANE), lambda i: (0, 0))],
        out_shape=[jax.ShapeDtypeStruct((lp, D), F32), jax.ShapeDtypeStruct((8, LANE), F32)],
        name="loss_head", compiler_params=_cp())(xf, tgt)


def _peer(d):
    x, y, c = lax.axis_index("x"), lax.axis_index("y"), lax.axis_index("c")
    return (x ^ ((d >> 2) & 1), y ^ ((d >> 1) & 1), c ^ (d & 1))


def _index_of(p):
    return 4 * p[0] + 2 * p[1] + p[2]


def _all_gather(xs, name):
    n = len(xs)

    def body(*refs):
        start, forward, finish = _gather_phases(refs[:n], refs[n:2 * n], *refs[2 * n:])
        start()
        forward()
        finish()

    anyspec = pl.BlockSpec(memory_space=pl.ANY)
    return pl.pallas_call(
        body, in_specs=[anyspec] * n, out_specs=[anyspec] * n,
        out_shape=_gather_shapes(xs), scratch_shapes=_comm_sems(n), name=name)(*xs)


def _gather_shapes(xs):
    return [jax.ShapeDtypeStruct((N_DEV,) + x.shape, x.dtype) for x in xs]


def _comm_sems(n):
    return [pltpu.SemaphoreType.DMA((7 * n,)), pltpu.SemaphoreType.DMA((7 * n,)), pltpu.SemaphoreType.DMA((n,))]


def _gather_phases(x_refs, out_refs, send_sems, recv_sems, local_sems):
    n = len(x_refs)
    chips = [2, 4, 6]

    def copy(a, kk, block, to, src=None):
        slot = out_refs[a].at[_index_of(block)]
        return pltpu.make_async_remote_copy(
            src_ref=slot if src is None else src, dst_ref=slot,
            send_sem=send_sems.at[7 * a + kk], recv_sem=recv_sems.at[7 * a + kk], device_id=to,
            device_id_type=MESH)

    def local(a):
        return pltpu.make_async_copy(x_refs[a], out_refs[a].at[_index_of(_peer(0))], local_sems.at[a])

    def firsts():
        out = []
        for a in range(n):
            out.append(copy(a, 0, _peer(0), _peer(1), src=x_refs[a]))
            out += [copy(a, 1 + j, _peer(0), _peer(d), src=x_refs[a]) for j, d in enumerate(chips)]
        return out

    def passes():
        return [copy(a, 4 + j, _peer(d), _peer(1)) for j, d in enumerate(chips) for a in range(n)]

    def start():
        for a in range(n):
            local(a).start()
        for cp in firsts():
            cp.start()

    def forward():
        for j, d in enumerate(chips):
            for a in range(n):
                copy(a, 1 + j, _peer(d), _peer(0)).wait_recv()
                copy(a, 4 + j, _peer(d), _peer(1)).start()

    def finish():
        for a in range(n):
            copy(a, 0, _peer(1), _peer(0)).wait_recv()
            for j, d in enumerate(chips):
                copy(a, 4 + j, _peer(d | 1), _peer(0)).wait_recv()
        for cp in firsts() + passes():
            cp.wait_send()
        for a in range(n):
            local(a).wait()

    return start, forward, finish


def _scatter_phases(x_refs, out_refs, send_sems, recv_sems, local_sems, layer):
    n = len(x_refs)

    def land(a, dev):
        slot = out_refs[a].at[dev]
        return slot if layer is None else slot.at[layer]

    def local(a):
        my = _index_of(_peer(0))
        return pltpu.make_async_copy(x_refs[a].at[my], land(a, my), local_sems.at[a])

    def copy(a, d):
        my = _index_of(_peer(0))
        return pltpu.make_async_remote_copy(
            src_ref=x_refs[a].at[_index_of(_peer(d))], dst_ref=land(a, my),
            send_sem=send_sems.at[7 * a + d - 1], recv_sem=recv_sems.at[7 * a + d - 1], device_id=_peer(d),
            device_id_type=MESH)

    def arrival(a, d):
        frm = _index_of(_peer(d))
        return pltpu.make_async_remote_copy(
            src_ref=x_refs[a].at[frm], dst_ref=land(a, frm),
            send_sem=send_sems.at[7 * a + d - 1], recv_sem=recv_sems.at[7 * a + d - 1], device_id=_peer(d),
            device_id_type=MESH)

    def start():
        for a in range(n):
            local(a).start()
        for d in range(1, N_DEV):
            for a in range(n):
                copy(a, d).start()

    def finish():
        for d in range(1, N_DEV):
            for a in range(n):
                arrival(a, d).wait_recv()
        for d in range(1, N_DEV):
            for a in range(n):
                copy(a, d).wait_send()
        for a in range(n):
            local(a).wait()

    return start, finish


def _all_to_all(xs, bufs, layer, name):
    n = len(xs)

    def body(*refs):
        start, finish = _scatter_phases(refs[:n], refs[2 * n:3 * n], *refs[3 * n:], layer)
        start()
        finish()

    anyspec = pl.BlockSpec(memory_space=pl.ANY)
    return pl.pallas_call(
        body, in_specs=[anyspec] * (2 * n), out_specs=[anyspec] * n,
        out_shape=[jax.ShapeDtypeStruct(b.shape, b.dtype) for b in bufs],
        input_output_aliases={n + a: a for a in range(n)},
        scratch_shapes=_comm_sems(n), name=name)(*xs, *bufs)


def _adam_math(g, w, m, v):
    c1 = 1.0 - ADAM_B1 ** ADAM_STEP
    c2 = 1.0 - ADAM_B2 ** ADAM_STEP
    mn = ADAM_B1 * m + (1.0 - ADAM_B1) * g
    vn = ADAM_B2 * v + (1.0 - ADAM_B2) * (g * g)
    return -ADAM_LR * ((mn / c1) / (jnp.sqrt(vn / c2) + ADAM_EPS) + ADAM_WD * w), mn, vn


def _reduce_adam(parts, w, m, v, rb, name, row_off=0):
    depth, rows, cols = w.shape
    assert rows % rb == 0 and row_off % rb == 0

    def body(p_ref, w_ref, m_ref, v_ref, g_ref, d_ref, mo_ref, vo_ref):
        g = p_ref[0, 0].astype(F32)
        for j in range(1, N_DEV):
            g = g + p_ref[j, 0].astype(F32)
        g_ref[0] = g
        d_ref[0], mo_ref[0], vo_ref[0] = _adam_math(g, w_ref[0], m_ref[0], v_ref[0])

    blk = pl.BlockSpec((1, rb, cols), lambda l, i: (l, i, 0))
    out = jax.ShapeDtypeStruct(w.shape, F32)
    return pl.pallas_call(
        body, grid=(depth, rows // rb),
        in_specs=[pl.BlockSpec((N_DEV, 1, rb, cols), lambda l, i: (0, l, i + row_off // rb, 0)), blk, blk, blk],
        out_specs=[blk, blk, blk, blk], out_shape=[out, out, out, out],
        name=name, compiler_params=_cp())(parts, w, m, v)


def _reduce_adam_flat(parts, w, m, v, name):
    q_rows = w.shape[0]

    def body(p_ref, w_ref, m_ref, v_ref, g_ref, d_ref, mo_ref, vo_ref):
        g = p_ref[0].astype(F32)
        for j in range(1, N_DEV):
            g = g + p_ref[j].astype(F32)
        g_ref[...] = g
        d_ref[...], mo_ref[...], vo_ref[...] = _adam_math(g, w_ref[...], m_ref[...], v_ref[...])

    blk = pl.BlockSpec((q_rows, LANE), lambda i: (0, 0))
    out = jax.ShapeDtypeStruct((q_rows, LANE), F32)
    return pl.pallas_call(
        body, grid=(1,), in_specs=[pl.BlockSpec((N_DEV, q_rows, LANE), lambda i: (0, 0, 0)), blk, blk, blk],
        out_specs=[blk, blk, blk, blk], out_shape=[out, out, out, out],
        name=name, compiler_params=_cp())(parts, w, m, v)


C128 = (("w_out_pool", 256), ("w_out_mla", 512), ("w_out_conf", 256), ("w_out_sc", 256), ("w_ukv", 128))
C128_ROWS = sum(r for _, r in C128)
TAIL = (("meta_tokens", (N_META, 128)), ("conf_dw_w", (DEPTH, CONF_K, 32)), ("sc_dw_w", (DEPTH, SC_K, 32)))
TAIL_ROWS = sum(int(np.prod(s)) for _, s in TAIL) // LANE
TAIL_PAD = 56
SMALL = (("pre_norm_g", (DEPTH, D)), ("gate_bias", (DEPTH, 4096)), ("pool_w", (DEPTH, 4, 64, 64)),
         ("pool_scale", (DEPTH, 256)), ("q_norm_g", (DEPTH, 256)), ("kv_norm_g", (DEPTH, 128)),
         ("conf_dw_b", (DEPTH, 256)), ("conf_ln_g", (DEPTH, 256)), ("conf_ln_b", (DEPTH, 256)),
         ("post_norm_g", (DEPTH, D)))
SMALL_ROWS = sum(int(np.prod(s)) for _, s in SMALL) // LANE
SMALL_PAD = -(-(SMALL_ROWS + 1) // 8) * 8


def _pack_tail(t):
    parts = [t[n].reshape(-1, LANE) for n, _ in TAIL]
    parts.append(jnp.zeros((TAIL_PAD - TAIL_ROWS, LANE), F32))
    return jnp.concatenate(parts, axis=0)


def _unpack_tail(flat):
    out, off = {}, 0
    for n, s in TAIL:
        rows = int(np.prod(s)) // LANE
        out[n] = flat[off:off + rows].reshape(s)
        off += rows
    return out


def _unpack_tail_full(g):
    out, off = {}, 0
    for n, s in TAIL:
        rows = int(np.prod(s)) // LANE
        blk = jnp.moveaxis(g[:, off:off + rows].reshape((N_DEV,) + s), 0, -2)
        out[n] = blk.reshape(s[:-1] + (N_DEV * s[-1],))
        off += rows
    return out


def _pack_small(t, extra_row):
    parts = [t[n].reshape(-1, LANE) for n, _ in SMALL] + [extra_row]
    parts.append(jnp.zeros((SMALL_PAD - SMALL_ROWS - 1, LANE), F32))
    return jnp.concatenate(parts, axis=0)


def _unpack_small(flat):
    out, off = {}, 0
    for n, s in SMALL:
        rows = int(np.prod(s)) // LANE
        out[n] = flat[off:off + rows].reshape(s)
        off += rows
    return out


def _cols_by_dest(g, width):
    r = g.shape[0]
    return g.reshape(r, N_DEV, width).transpose(1, 0, 2)


def _cols_full(gathered):
    _, r, c = gathered.shape
    return gathered.transpose(1, 0, 2).reshape(r, N_DEV * c)


W_IN_SHARD = 916
PACKED_SEGS = ((3232, 7328), (512, 896), 64, (896, 928), 32, (0, 512), (928, 3232))


def _pack_w_in(g):
    parts = []
    for seg in PACKED_SEGS:
        if isinstance(seg, int):
            parts.append(jnp.zeros((g.shape[1], seg), g.dtype))
            continue
        a, b = seg
        while a < b:
            k = a // W_IN_SHARD
            hi = min(b, W_IN_SHARD * (k + 1))
            parts.append(g[k, :, a - W_IN_SHARD * k:hi - W_IN_SHARD * k])
            a = hi
    return jnp.concatenate(parts, axis=1)


def _w_in_grad_by_dest(gl, mla, mix):
    src = (((0, 512), mix, 0), ((512, 896), mla, 0), ((896, 928), mla, 448), ((928, 3232), mix, 512),
           ((3232, 7328), gl, 0))
    blocks = []
    for k in range(N_DEV):
        lo, hi = W_IN_SHARD * k, W_IN_SHARD * (k + 1)
        parts = []
        for (a, b), arr, off in src:
            s, e = max(a, lo), min(b, hi)
            if s < e:
                parts.append(arr[:, off + s - a:off + e - a])
        blocks.append(jnp.concatenate(parts, axis=1))
    return jnp.stack(blocks)


def _rope_tables(lp):
    inv = 1.0 / (ROPE_THETA ** (jnp.arange(0, QK_ROPE, 2, dtype=F32) / QK_ROPE))
    ang = jnp.arange(lp, dtype=F32)[:, None] * inv[None, :]
    cos, sin = jnp.cos(ang), jnp.sin(ang)
    one = jnp.ones((lp, QK_NOPE), F32)
    zero = jnp.zeros((lp, QK_NOPE), F32)
    z16 = jnp.zeros((lp, 16), F32)
    c = jnp.concatenate([one, cos, cos, jnp.ones((lp, 32), F32)], axis=1)
    s1 = jnp.concatenate([zero, z16, sin, jnp.zeros((lp, 32), F32)], axis=1)
    s2 = jnp.concatenate([zero, -sin, z16, jnp.zeros((lp, 32), F32)], axis=1)
    return c, s1, s2


def kernel(x, meta_tokens, pre_norm_g, w_in, gate_bias, pool_w, pool_scale, w_out_pool, q_norm_g, w_uq, kv_norm_g, w_ukv, w_out_mla, conf_dw_w, conf_dw_b, conf_ln_g, conf_ln_b, w_out_conf, sc_dw_w, w_out_sc, w_o, post_norm_g, loss_target, m_meta_tokens, m_pre_norm_g, m_w_in, m_gate_bias, m_pool_w, m_pool_scale, m_w_out_pool, m_q_norm_g, m_w_uq, m_kv_norm_g, m_w_ukv, m_w_out_mla, m_conf_dw_w, m_conf_dw_b, m_conf_ln_g, m_conf_ln_b, m_w_out_conf, m_sc_dw_w, m_w_out_sc, m_w_o, m_post_norm_g, v_meta_tokens, v_pre_norm_g, v_w_in, v_gate_bias, v_pool_w, v_pool_scale, v_w_out_pool, v_q_norm_g, v_w_uq, v_kv_norm_g, v_w_ukv, v_w_out_mla, v_conf_dw_w, v_conf_dw_b, v_conf_ln_g, v_conf_ln_b, v_w_out_conf, v_sc_dw_w, v_w_out_sc, v_w_o, v_post_norm_g):
    names = ["w_in", "w_uq", "w_o"] + [n for n, _ in C128] + [n for n, _ in TAIL] + [n for n, _ in SMALL]
    loc = locals()
    w = {n: loc[n] for n in names}
    mom = {n: loc["m_" + n] for n in names}
    vel = {n: loc["v_" + n] for n in names}

    seq = x.shape[1]
    n_real = N_META + seq
    lp = -(-n_real // RB) * RB
    tmb = lp // 3
    tabs = _rope_tables(lp)

    c128 = jnp.concatenate([w[n] for n, _ in C128], axis=1)
    def shards_of(i):
        return [w_in[i].astype(BF16), c128[i].astype(BF16), w_uq[i].astype(BF16), w_o[i].astype(BF16)]

    gathered = [_all_gather(shards_of(0), "gather_weights")] + [None] * (DEPTH - 1)
    tail_w = _pack_tail(w)
    tail = _unpack_tail_full(_all_gather([tail_w], "gather_tail")[0])
    eye4 = jnp.eye(4, dtype=F32)
    bd_all = (pool_w[:, :, :, None, :] * eye4[None, :, None, :, None]).reshape(DEPTH, 256, 256).astype(BF16)

    def layer_weights(i):
        g_in, g_c128, g_uq, g_o = gathered[i]
        lw = {}
        lw["w_in"] = _pack_w_in(g_in)
        lw["wc"] = _cols_full(g_c128)
        wuq = _cols_full(g_uq).reshape(Q_RANK, HEADS, 96)
        lw["w_uq"] = jnp.pad(wuq, ((0, 0), (0, 0), (0, 32))).reshape(Q_RANK, HEADS * LANE)
        wukv = lw["wc"][U_OFF[4]:].reshape(KV_RANK, HEADS, 128)
        wk = jnp.pad(wukv[:, :, :QK_NOPE], ((0, 0), (0, 0), (0, 64))).reshape(KV_RANK, HEADS * LANE)
        lw["w_ukv"] = jnp.concatenate([wk, wukv[:, :, QK_NOPE:].reshape(KV_RANK, HEADS * V_DIM)], axis=1)
        lw["w_o"] = g_o.reshape(D, D)
        lw["bd"] = bd_all[i]
        lw["cw"] = jnp.pad(tail["conf_dw_w"][i], ((0, 1), (0, 0)))
        lw["sw"] = jnp.pad(tail["sc_dw_w"][i], ((0, 8 - SC_K), (0, 0)))
        return lw

    meta_full = tail["meta_tokens"]

    pad_rows = lp - n_real
    xr = jnp.concatenate([meta_full, x[0], jnp.zeros((pad_rows, D), F32)], axis=0)
    tgt = jnp.pad(loss_target[0], ((N_META, pad_rows), (0, 0)))
    saved = []
    for i in range(DEPTH):
        lw = layer_weights(i)
        h = _rms_fwd(xr, pre_norm_g[i:i + 1], lp)
        z = _mm(h, lw["w_in"], lp, ZW, D, tm=RB, tn=ZW // 2, tk=D, n_outer=True, name="mm_in")
        qn, kvn, krr = _mla_prep(z, q_norm_g[i:i + 1], kv_norm_g[i:i + 1], tabs, lp)
        q_raw = _mm(qn, lw["w_uq"], lp, 1024, Q_RANK, tm=tmb, tn=1024, tk=Q_RANK, name="mm_uq")
        kv_raw = _mm(kvn, lw["w_ukv"], lp, 1536, KV_RANK, tm=tmb, tn=512, tk=KV_RANK, name="mm_ukv")
        qt, kt, vt = _mla_post(q_raw, kv_raw, krr, tabs, lp)
        res = _attn_fwd(qt, kt, vt, lp, gather=shards_of(i + 1) if i + 1 < DEPTH else ())
        oat, lse = res[0], res[1]
        if i + 1 < DEPTH:
            gathered[i + 1] = res[2:]
        u = _mix_fwd(z, oat, lw["bd"], pool_scale[i:i + 1], lw["cw"], conf_dw_b[i:i + 1], conf_ln_g[i:i + 1],
                     conf_ln_b[i:i + 1], lw["sw"], lp)
        x_new, m_act, o2 = _merge_fwd(xr, u, z, gate_bias[i:i + 1], lw["wc"], lw["w_o"],
                                      post_norm_g[i:i + 1], lp)
        saved.append(dict(lw=lw, x=xr, h=h, z=z, qn=qn, kvn=kvn, qt=qt, kt=kt, vt=vt, oat=oat, lse=lse,
                          u=u, m=m_act, o2=o2))
        xr = x_new

    dx, loss_part = _loss_head(xr, tgt, n_real, lp)

    gsm = {n: [None] * DEPTH for n, _ in SMALL}
    g_cw = [None] * DEPTH
    g_sw = [None] * DEPTH
    recv = [lax.empty((N_DEV, DEPTH) + s, BF16) for s in ((D, 916), (C128_ROWS, 128), (Q_RANK, 96), (128, D))]
    pending = None
    for i in reversed(range(DEPTH)):
        s = saved[i]
        lw = s["lw"]
        do2, dgl, dyb, du, dgb, dgpost = _merge_bwd(dx, s["o2"], s["u"], s["z"], gate_bias[i:i + 1],
                                                    lw["wc"], lw["w_o"], post_norm_g[i:i + 1], lp)
        d_wo = _mm(s["m"], do2, D, D, lp, ta=True, tm=512, tn=D, tk=tmb, out_dtype=BF16, name="mm_dwo")
        d_wout = []
        for b in range(4):
            rows = U_OFF[b + 1] - U_OFF[b]
            d_wout.append(_mm(s["u"], dyb, rows, D, lp, ta=True, tm=256, tn=D, tk=tmb, out_dtype=BF16,
                              a_moff=U_OFF[b] // 256, b_noff=b, name="mm_dwout%d" % b))
        dzx, doat, dbd, dcw, dsw, dsm = _mix_bwd(
            s["z"], s["oat"], du, lw["bd"], pool_scale[i:i + 1], lw["cw"], conf_dw_b[i:i + 1],
            conf_ln_g[i:i + 1], conf_ln_b[i:i + 1], lw["sw"], lp)
        res = _attn_bwd(s["qt"], s["kt"], s["vt"], s["oat"], doat, s["lse"], lp,
                        scatter=(pending, recv, i + 1) if pending else None)
        dqt, dkt, dvt = res[:3]
        if pending:
            recv = list(res[3:])
        dq_raw, dkv_raw, dkr = _mla_post_bwd(dqt, dkt, dvt, tabs, lp)
        dqn = _mm(dq_raw, lw["w_uq"], lp, Q_RANK, 1024, tb=True, tm=tmb, tn=Q_RANK, tk=1024, name="mm_dqn")
        d_wuq = _mm(s["qn"], dq_raw, Q_RANK, 1024, lp, ta=True, tm=Q_RANK, tn=1024, tk=tmb, out_dtype=BF16,
                    name="mm_dwuq")
        dkvn = _mm(dkv_raw, lw["w_ukv"], lp, KV_RANK, 1536, tb=True, tm=tmb, tn=KV_RANK, tk=1536, name="mm_dkvn")
        d_wukv = _mm(s["kvn"], dkv_raw, KV_RANK, 1536, lp, ta=True, tm=KV_RANK, tn=1536, tk=tmb, out_dtype=BF16,
                     name="mm_dwukv")
        dzq, dqg, dkvg = _mla_prep_bwd(s["z"], q_norm_g[i:i + 1], kv_norm_g[i:i + 1], dqn, dkvn, dkr, lp)
        d_win = [_mm(s["h"], seg, D, seg.shape[1], lp, ta=True, tm=D, tn=tn, tk=tmb, out_dtype=BF16,
                     name="mm_dwin%d" % k) for k, (seg, tn) in enumerate(((dgl, 1024), (dzq, 512), (dzx, 1408)))]
        d_wuq_o = d_wuq.reshape(Q_RANK, HEADS, LANE)[:, :, :96].reshape(Q_RANK, HEADS * 96)
        d_wukv_o = jnp.concatenate([d_wukv[:, :1024].reshape(KV_RANK, HEADS, LANE)[:, :, :QK_NOPE],
                                    d_wukv[:, 1024:].reshape(KV_RANK, HEADS, V_DIM)], axis=2).reshape(KV_RANK, 1024)
        pending = [
            _w_in_grad_by_dest(*d_win),
            _cols_by_dest(jnp.concatenate(d_wout + [d_wukv_o], axis=0), 128),
            _cols_by_dest(d_wuq_o, 96),
            d_wo.reshape(N_DEV, 128, D)]
        res = _dh(dgl, dzq, dzx, lw["w_in"], lp, scatter=(pending, recv, 0) if i == 0 else None)
        dh = res[0]
        if i == 0:
            recv = list(res[1:])
        dx, dgpre = _rms_bwd(s["x"], pre_norm_g[i:i + 1], dh, dx, lp)

        gsm["pre_norm_g"][i] = dgpre[0]
        gsm["gate_bias"][i] = dgb[0]
        gsm["pool_w"][i] = jnp.stack([dbd[64 * g:64 * (g + 1), 64 * g:64 * (g + 1)] for g in range(4)])
        gsm["pool_scale"][i] = dsm[0]
        gsm["conf_dw_b"][i] = dsm[1]
        gsm["conf_ln_g"][i] = dsm[2]
        gsm["conf_ln_b"][i] = dsm[3]
        gsm["q_norm_g"][i] = dqg[0]
        gsm["kv_norm_g"][i] = dkvg[0]
        gsm["post_norm_g"][i] = dgpost[0]
        g_cw[i] = dcw[:CONF_K]
        g_sw[i] = dsw[:SC_K]

    outs = [dict() for _ in range(4)]

    def put(n, res):
        for t, r in zip(outs, res):
            t[n] = r

    put("w_in", _reduce_adam(recv[0], w["w_in"], mom["w_in"], vel["w_in"], 256, "adam_w_in"))
    off = 0
    for n, rows in C128:
        put(n, _reduce_adam(recv[1], w[n], mom[n], vel[n], 128, "adam_" + n, row_off=off))
        off += rows
    put("w_uq", _reduce_adam(recv[2], w["w_uq"], mom["w_uq"], vel["w_uq"], Q_RANK, "adam_w_uq"))
    put("w_o", _reduce_adam(recv[3], w["w_o"], mom["w_o"], vel["w_o"], 128, "adam_w_o"))

    tail_g = {"meta_tokens": _cols_by_dest(dx[:N_META], 128),
              "conf_dw_w": jnp.moveaxis(jnp.stack(g_cw).reshape(DEPTH, CONF_K, N_DEV, 32), 2, 0),
              "sc_dw_w": jnp.moveaxis(jnp.stack(g_sw).reshape(DEPTH, SC_K, N_DEV, 32), 2, 0)}
    tail_bd = jnp.concatenate([tail_g[n].reshape(N_DEV, -1, LANE) for n, _ in TAIL]
                              + [jnp.zeros((N_DEV, TAIL_PAD - TAIL_ROWS, LANE), F32)], axis=1)
    tail_recv = _all_to_all([tail_bd], [lax.empty((N_DEV, TAIL_PAD, LANE), F32)], None, "scatter_tail")[0]
    tail_res = _reduce_adam_flat(tail_recv, tail_w, _pack_tail(mom), _pack_tail(vel), "adam_tail")

    small_g = {n: jnp.stack(gsm[n]) for n, _ in SMALL}
    loss_row = jnp.concatenate([loss_part[0:1, 0:1], jnp.zeros((1, LANE - 1), F32)], axis=1)
    zrow = jnp.zeros((1, LANE), F32)
    parts = _all_gather([_pack_small(small_g, loss_row)], "gather_small_grads")[0]
    small_res = _reduce_adam_flat(parts, _pack_small(w, zrow), _pack_small(mom, zrow), _pack_small(vel, zrow),
                                  "adam_small")
    loss = small_res[0][SMALL_ROWS, 0]
    for t, tf, sf in zip(outs, tail_res, small_res):
        t.update(_unpack_tail(tf))
        t.update(_unpack_small(sf))
    order = ["meta_tokens", "pre_norm_g", "w_in", "gate_bias", "pool_w", "pool_scale", "w_out_pool", "q_norm_g",
             "w_uq", "kv_norm_g", "w_ukv", "w_out_mla", "conf_dw_w", "conf_dw_b", "conf_ln_g", "conf_ln_b",
             "w_out_conf", "sc_dw_w", "w_out_sc", "w_o", "post_norm_g"]
    grad_x = dx[N_META:n_real][None]
    return (loss, grad_x, *[t[n] for t in outs for n in order])
```

```python
import functools

import jax
import jax.numpy as jnp
import numpy as np
from jax import lax
from jax.experimental import pallas as pl
from jax.experimental.pallas import tpu as pltpu

F32 = jnp.float32
BF16 = jnp.bfloat16

D = 1024
N_META = 16
DEPTH = 4
EPS = 1e-6
HEADS = 8
QK_NOPE = 64
QK_ROPE = 32
V_DIM = 64
Q_RANK = 256
KV_RANK = 128
ROPE_THETA = 10000.0
SCALE = (QK_NOPE + QK_ROPE) ** -0.5
CONF_K = 31
SC_K = 3
N_DEV = 8

ADAM_LR = 0.001
ADAM_B1 = 0.9
ADAM_B2 = 0.999
ADAM_EPS = 1e-08
ADAM_WD = 0.01
ADAM_STEP = 10

RB = 384
HB = 32
LANE = 128
VMEM_LIMIT = 56 * 1024 * 1024

GL0, CQ0, CKV0, KR0, PV0, PG0, MG0, CU0, CG0, SBB0, SBC0, SBX0, SG0, ZW = (
    0, 4096, 4352, 4480, 4608, 4864, 5120, 5632, 6144, 6400, 6656, 6912, 7168, 7424)
ZSEG = ((0, 4096), (4096, 512), (4608, 2816))
LOG2E = 1.4426950408889634
LN2 = 0.6931471805599453

MESH = pl.DeviceIdType.MESH


def _cp(**kw):
    return pltpu.CompilerParams(vmem_limit_bytes=VMEM_LIMIT, **kw)


def _sig(x):
    return jax.nn.sigmoid(x)


def _silu(x):
    return x * _sig(x)


def _dsilu(x):
    s = _sig(x)
    return s * (1.0 + x * (1.0 - s))


def _dn(x, k):
    return x if k == 0 else pltpu.roll(x, k, 0)


def _up(x, k):
    return x if k == 0 else pltpu.roll(x, x.shape[0] - k, 0)


def _rope(t, c, s1, s2):
    return t * c + pltpu.roll(t, 16, 1) * s1 + pltpu.roll(t, LANE - 16, 1) * s2


def _rope_t(g, c, s1, s2):
    return g * c + pltpu.roll(g * s1, LANE - 16, 1) + pltpu.roll(g * s2, 16, 1)


def _mm(a, b, m, n, k, *, ta=False, tb=False, out_dtype=F32, tm, tn, tk, name,
        a_moff=0, a_koff=0, b_noff=0, b_koff=0, c=None, n_outer=False):
    assert m % tm == 0 and n % tn == 0 and k % tk == 0, (name, m, n, k, tm, tn, tk)
    nk = k // tk
    dims = (((0,) if ta else (1,), (1,) if tb else (0,)), ((), ()))
    has_c = c is not None

    def body(a_ref, b_ref, *rest):
        c_ref = rest[0] if has_c else None
        o_ref = rest[1] if has_c else rest[0]
        scr = rest[2:] if has_c else rest[1:]
        part = lax.dot_general(a_ref[...].astype(BF16), b_ref[...].astype(BF16), dims,
                               preferred_element_type=F32)

        def finish(total):
            if has_c:
                total = total + c_ref[...]
            o_ref[...] = total.astype(out_dtype)

        if nk == 1:
            finish(part)
        else:
            acc = scr[0]
            kk = pl.program_id(2)

            @pl.when(kk == 0)
            def _():
                acc[...] = part

            @pl.when(kk > 0)
            def _():
                acc[...] += part

            @pl.when(kk == nk - 1)
            def _():
                finish(acc[...])

    def im(f):
        return (lambda g0, g1, q: f(g1, g0, q)) if n_outer else f

    if ta:
        a_spec = pl.BlockSpec((tk, tm), im(lambda i, j, q: (q + a_koff, i + a_moff)))
    else:
        a_spec = pl.BlockSpec((tm, tk), im(lambda i, j, q: (i + a_moff, q + a_koff)))
    if tb:
        b_spec = pl.BlockSpec((tn, tk), im(lambda i, j, q: (j + b_noff, q + b_koff)))
    else:
        b_spec = pl.BlockSpec((tk, tn), im(lambda i, j, q: (q + b_koff, j + b_noff)))
    o_spec = pl.BlockSpec((tm, tn), im(lambda i, j, q: (i, j)))
    grid = (n // tn, m // tm, nk) if n_outer else (m // tm, n // tn, nk)
    return pl.pallas_call(
        body, grid=grid, in_specs=[a_spec, b_spec] + ([o_spec] if has_c else []),
        out_specs=o_spec, out_shape=jax.ShapeDtypeStruct((m, n), out_dtype),
        scratch_shapes=[pltpu.VMEM((tm, tn), F32)] if nk > 1 else [],
        name=name, compiler_params=_cp())(*((a, b, c) if has_c else (a, b)))


def _rms_fwd(x, g, lp):
    def body(x_ref, g_ref, h_ref):
        xv = x_ref[...]
        r = lax.rsqrt(jnp.mean(xv * xv, axis=-1, keepdims=True) + EPS)
        h_ref[...] = (xv * r * g_ref[...]).astype(BF16)

    return pl.pallas_call(
        body, grid=(lp // RB,),
        in_specs=[pl.BlockSpec((RB, D), lambda i: (i, 0)), pl.BlockSpec((1, D), lambda i: (0, 0))],
        out_specs=pl.BlockSpec((RB, D), lambda i: (i, 0)),
        out_shape=jax.ShapeDtypeStruct((lp, D), BF16), name="rms_fwd", compiler_params=_cp())(x, g)


def _rms_bwd(x, g, dh, dx_in, lp):
    def body(x_ref, g_ref, dh_ref, dxi_ref, dx_ref, dg_ref):
        i = pl.program_id(0)
        xv = x_ref[...]
        r = lax.rsqrt(jnp.mean(xv * xv, axis=-1, keepdims=True) + EPS)
        dy = dh_ref[...]
        a = dy * g_ref[...]
        dx_ref[...] = dxi_ref[...] + r * a - xv * (r * r * r) * jnp.mean(a * xv, axis=-1, keepdims=True)
        part = jnp.sum(dy * xv * r, axis=0, keepdims=True)

        @pl.when(i == 0)
        def _():
            dg_ref[...] = part

        @pl.when(i > 0)
        def _():
            dg_ref[...] += part

    blk = pl.BlockSpec((RB, D), lambda i: (i, 0))
    vec = pl.BlockSpec((1, D), lambda i: (0, 0))
    return pl.pallas_call(
        body, grid=(lp // RB,), in_specs=[blk, vec, blk, blk], out_specs=[blk, vec],
        out_shape=[jax.ShapeDtypeStruct((lp, D), F32), jax.ShapeDtypeStruct((1, D), F32)],
        name="rms_bwd", compiler_params=_cp())(x, g, dh, dx_in)


def _mla_prep(z, qg, kvg, tabs, lp):
    def body(z_ref, qg_ref, kvg_ref, c_ref, s1_ref, s2_ref, qn_ref, kvn_ref, kr_ref):
        cq = z_ref[:, 0:256]
        ckv = z_ref[:, 256:384]
        kr = z_ref[:, 384:512]
        rq = lax.rsqrt(jnp.mean(cq * cq, axis=-1, keepdims=True) + EPS)
        rk = lax.rsqrt(jnp.mean(ckv * ckv, axis=-1, keepdims=True) + EPS)
        qn_ref[...] = (cq * rq * qg_ref[...]).astype(BF16)
        kvn_ref[...] = (ckv * rk * kvg_ref[...]).astype(BF16)
        kr_ref[...] = _rope(kr, c_ref[...], s1_ref[...], s2_ref[...])

    tab = pl.BlockSpec((RB, LANE), lambda i: (i, 0))
    return pl.pallas_call(
        body, grid=(lp // RB,),
        in_specs=[pl.BlockSpec((RB, 512), lambda i: (i, CQ0 // 512)),
                  pl.BlockSpec((1, 256), lambda i: (0, 0)), pl.BlockSpec((1, 128), lambda i: (0, 0)),
                  tab, tab, tab],
        out_specs=[pl.BlockSpec((RB, 256), lambda i: (i, 0)), tab, tab],
        out_shape=[jax.ShapeDtypeStruct((lp, 256), BF16), jax.ShapeDtypeStruct((lp, 128), BF16),
                   jax.ShapeDtypeStruct((lp, 128), F32)],
        name="mla_prep", compiler_params=_cp())(z, qg, kvg, *tabs)


def _mla_prep_bwd(z, qg, kvg, dqn, dkvn, dkr, lp):
    def body(z_ref, qg_ref, kvg_ref, dqn_ref, dkvn_ref, dkr_ref, dz_ref, dqg_ref, dkvg_ref):
        i = pl.program_id(0)

        def rms_b(xv, g, dy):
            r = lax.rsqrt(jnp.mean(xv * xv, axis=-1, keepdims=True) + EPS)
            a = dy * g
            dx = r * a - xv * (r * r * r) * jnp.mean(a * xv, axis=-1, keepdims=True)
            return dx, jnp.sum(dy * xv * r, axis=0, keepdims=True)

        dcq, pq = rms_b(z_ref[:, 0:256], qg_ref[...], dqn_ref[...])
        dckv, pk = rms_b(z_ref[:, 256:384], kvg_ref[...], dkvn_ref[...])
        dz_ref[:, 0:256] = dcq.astype(BF16)
        dz_ref[:, 256:384] = dckv.astype(BF16)
        dz_ref[:, 384:512] = dkr_ref[...].astype(BF16)

        @pl.when(i == 0)
        def _():
            dqg_ref[...] = pq
            dkvg_ref[...] = pk

        @pl.when(i > 0)
        def _():
            dqg_ref[...] += pq
            dkvg_ref[...] += pk

    tab = pl.BlockSpec((RB, LANE), lambda i: (i, 0))
    return pl.pallas_call(
        body, grid=(lp // RB,),
        in_specs=[pl.BlockSpec((RB, 512), lambda i: (i, CQ0 // 512)),
                  pl.BlockSpec((1, 256), lambda i: (0, 0)), pl.BlockSpec((1, 128), lambda i: (0, 0)),
                  pl.BlockSpec((RB, 256), lambda i: (i, 0)), tab, tab],
        out_specs=[pl.BlockSpec((RB, 512), lambda i: (i, 0)),
                   pl.BlockSpec((1, 256), lambda i: (0, 0)), pl.BlockSpec((1, 128), lambda i: (0, 0))],
        out_shape=[jax.ShapeDtypeStruct((lp, 512), BF16), jax.ShapeDtypeStruct((1, 256), F32),
                   jax.ShapeDtypeStruct((1, 128), F32)],
        name="mla_prep_bwd", compiler_params=_cp())(z, qg, kvg, dqn, dkvn, dkr)


def _mla_post(q_raw, kv_raw, krr, tabs, lp):
    def body(q_ref, kv_ref, kr_ref, c_ref, s1_ref, s2_ref, qo_ref, ko_ref, vo_ref):
        c, s1, s2, kr = c_ref[...], s1_ref[...], s2_ref[...], kr_ref[...]
        for h in range(HEADS):
            sl = slice(LANE * h, LANE * (h + 1))
            qo_ref[:, sl] = (_rope(q_ref[:, sl], c, s1, s2) * (SCALE * LOG2E)).astype(BF16)
            ko_ref[:, sl] = (kv_ref[:, sl] + kr).astype(BF16)
        vo_ref[...] = kv_ref[:, 1024:1536].astype(BF16)

    tab = pl.BlockSpec((RB, LANE), lambda i: (i, 0))
    wide = pl.BlockSpec((RB, 1024), lambda i: (i, 0))
    return pl.pallas_call(
        body, grid=(lp // RB,),
        in_specs=[wide, pl.BlockSpec((RB, 1536), lambda i: (i, 0)), tab, tab, tab, tab],
        out_specs=[wide, wide, pl.BlockSpec((RB, 512), lambda i: (i, 0))],
        out_shape=[jax.ShapeDtypeStruct((lp, 1024), BF16), jax.ShapeDtypeStruct((lp, 1024), BF16),
                   jax.ShapeDtypeStruct((lp, 512), BF16)],
        name="mla_post", compiler_params=_cp())(q_raw, kv_raw, krr, *tabs)


def _mla_post_bwd(dq, dk, dv, tabs, lp):
    def body(dq_ref, dk_ref, dv_ref, c_ref, s1_ref, s2_ref, dqr_ref, dkv_ref, dkr_ref):
        c, s1, s2 = c_ref[...], s1_ref[...], s2_ref[...]
        lane = lax.broadcasted_iota(jnp.int32, (1, LANE), 1)
        ropel = (lane >= QK_NOPE) & (lane < QK_NOPE + QK_ROPE)
        ksum = jnp.zeros((RB, LANE), F32)
        for h in range(HEADS):
            sl = slice(LANE * h, LANE * (h + 1))
            dqr_ref[:, sl] = _rope_t(dq_ref[:, sl].astype(F32) * SCALE, c, s1, s2).astype(BF16)
            dkt = dk_ref[:, sl]
            dkv_ref[:, sl] = dkt
            ksum = ksum + dkt.astype(F32)
        dkv_ref[:, 1024:1536] = dv_ref[...]
        dkr_ref[...] = jnp.where(ropel, _rope_t(jnp.where(ropel, ksum, 0.0), c, s1, s2), 0.0)

    tab = pl.BlockSpec((RB, LANE), lambda i: (i, 0))
    wide = pl.BlockSpec((RB, 1024), lambda i: (i, 0))
    return pl.pallas_call(
        body, grid=(lp // RB,),
        in_specs=[wide, wide, pl.BlockSpec((RB, 512), lambda i: (i, 0)), tab, tab, tab],
        out_specs=[wide, pl.BlockSpec((RB, 1536), lambda i: (i, 0)), tab],
        out_shape=[jax.ShapeDtypeStruct((lp, 1024), BF16), jax.ShapeDtypeStruct((lp, 1536), BF16),
                   jax.ShapeDtypeStruct((lp, 128), F32)],
        name="mla_post_bwd", compiler_params=_cp())(dq, dk, dv, *tabs)


def _head_lanes(e):
    lane = lax.broadcasted_iota(jnp.int32, (1, LANE), 1)
    return lane >= V_DIM if e else lane < V_DIM


ONE_LANE = (V_DIM, 0)


def _attn_fwd(q, k, v, lp, gather=()):
    nq = lp // RB
    n = len(gather)
    steps = HEADS // 2

    def body(q_ref, k_ref, v_ref, *rest):
        o_ref, lse_ref = rest[n], rest[n + 1]
        vm_scr = rest[2 * n + 2]
        if n:
            g_start, g_forward, g_finish = _gather_phases(rest[:n], rest[n + 2:2 * n + 2], *rest[2 * n + 3:])
            pl.when(pl.program_id(0) == 0)(g_start)
            pl.when(pl.program_id(0) == steps - 1)(g_forward)
        lane = lax.broadcasted_iota(jnp.int32, (1, LANE), 1)
        vv = v_ref[...]
        for e in range(2):
            ones = jnp.where(lane == ONE_LANE[e], 1.0, 0.0).astype(BF16)
            vm_scr[e] = jnp.where(_head_lanes(e), vv, jnp.broadcast_to(ones, vv.shape))
        causal = (lax.broadcasted_iota(jnp.int32, (RB, RB), 1) <= lax.broadcasted_iota(jnp.int32, (RB, RB), 0))

        def qblock(i, _):
            rows = pl.ds(pl.multiple_of(i * RB, RB), RB)
            qs = [q_ref[rows, LANE * e:LANE * (e + 1)] for e in range(2)]

            def scores(j):
                cols = pl.ds(pl.multiple_of(j * RB, RB), RB)
                return tuple(lax.dot_general(qs[e], k_ref[cols, LANE * e:LANE * (e + 1)], (((1,), (1,)), ((), ())),
                                             preferred_element_type=F32) for e in range(2))

            def update(j, s, carry, masked):
                cols = pl.ds(pl.multiple_of(j * RB, RB), RB)
                out = []
                for e in range(2):
                    m, acc = carry[2 * e], carry[2 * e + 1]
                    se = jnp.where(causal, s[e], -jnp.inf) if masked else s[e]
                    m_new = jnp.maximum(m, jnp.max(se, axis=-1, keepdims=True))
                    p = jnp.exp2((se - m_new).astype(BF16))
                    acc = jnp.exp2(m - m_new) * acc + jnp.dot(p, vm_scr[e, cols, :], preferred_element_type=F32)
                    out += [m_new, acc]
                return tuple(out)

            def pair(jj, c):
                sa, sb = scores(2 * jj), scores(2 * jj + 1)
                return update(2 * jj + 1, sb, update(2 * jj, sa, c, False), False)

            m0 = jnp.full((RB, 1), -jnp.inf, F32)
            a0 = jnp.zeros((RB, LANE), F32)
            carry = lax.fori_loop(0, i // 2, pair, (m0, a0, m0, a0))

            def last_two(c):
                sa, sb = scores(i - 1), scores(i)
                return update(i, sb, update(i - 1, sa, c, False), True)

            carry = lax.cond(i % 2 == 1, last_two, lambda c: update(i, scores(i), c, True), carry)
            o, lse = [], []
            for e in range(2):
                m, acc = carry[2 * e], carry[2 * e + 1]
                l = acc[:, ONE_LANE[e]:ONE_LANE[e] + 1]
                o.append(acc / l)
                lse.append(jnp.broadcast_to(m + jnp.log2(l), (RB, LANE)))
            o_ref[rows, :] = jnp.where(_head_lanes(0), o[0], o[1])
            lse_ref[rows, :] = jnp.where(_head_lanes(0), lse[0], lse[1])
            return 0

        lax.fori_loop(0, nq, qblock, 0)
        if n:
            pl.when(pl.program_id(0) == steps - 1)(g_finish)

    two = pl.BlockSpec((lp, 2 * LANE), lambda h: (0, h))
    one = pl.BlockSpec((lp, LANE), lambda h: (0, h))
    anyspec = pl.BlockSpec(memory_space=pl.ANY)
    return pl.pallas_call(
        body, grid=(steps,), in_specs=[two, two, one] + [anyspec] * n, out_specs=[one, one] + [anyspec] * n,
        out_shape=[jax.ShapeDtypeStruct((lp, 512), F32), jax.ShapeDtypeStruct((lp, 512), F32)]
        + _gather_shapes(gather),
        scratch_shapes=[pltpu.VMEM((2, lp, LANE), BF16)] + (_comm_sems(n) if n else []),
        name="attn_fwd_gather" if n else "attn_fwd", compiler_params=_cp())(q, k, v, *gather)


def _attn_bwd(q, k, v, o, do, lse, lp, scatter=None):
    nq = lp // RB
    xs, bufs, layer = scatter if scatter else ((), (), None)
    n = len(xs)
    steps = HEADS // 2

    def body(q_ref, k_ref, v_ref, o_ref, do_ref, lse_ref, *rest):
        dq_ref, dk_ref, dv_ref = rest[2 * n:2 * n + 3]
        vm_scr, dom_scr, dl_scr, dq_scr = rest[3 * n + 3:3 * n + 7]
        if n:
            s_start, s_finish = _scatter_phases(rest[:n], rest[2 * n + 3:3 * n + 3], *rest[3 * n + 7:], layer)
            pl.when(pl.program_id(0) == 0)(s_start)
        causal = (lax.broadcasted_iota(jnp.int32, (RB, RB), 1) <= lax.broadcasted_iota(jnp.int32, (RB, RB), 0))
        vv = v_ref[...]
        for e in range(2):
            vm_scr[e] = jnp.where(_head_lanes(e), vv, jnp.zeros_like(vv))

        def prep(i, _):
            rows = pl.ds(pl.multiple_of(i * RB, RB), RB)
            prod = do_ref[rows, :] * o_ref[rows, :]
            dls = []
            for e in range(2):
                hm = _head_lanes(e)
                dom_scr[e, rows, :] = jnp.where(hm, do_ref[rows, :], 0.0).astype(BF16)
                dls.append(jnp.sum(jnp.where(hm, prod, 0.0), axis=-1, keepdims=True))
            dl_scr[rows, :] = jnp.where(_head_lanes(0), dls[0], dls[1])
            dq_scr[rows, :] = jnp.zeros((RB, 2 * LANE), F32)
            return 0

        lax.fori_loop(0, nq, prep, 0)

        def kvblock(j, _):
            cols = pl.ds(pl.multiple_of(j * RB, RB), RB)
            kbs = [k_ref[cols, LANE * e:LANE * (e + 1)] for e in range(2)]

            def products(i):
                rows = pl.ds(pl.multiple_of(i * RB, RB), RB)
                out = []
                for e in range(2):
                    out.append(lax.dot_general(q_ref[rows, LANE * e:LANE * (e + 1)], kbs[e],
                                               (((1,), (1,)), ((), ())), preferred_element_type=F32))
                    out.append(lax.dot_general(dom_scr[e, rows, :], vm_scr[e, cols, :],
                                               (((1,), (1,)), ((), ())), preferred_element_type=F32))
                return tuple(out)

            def update(i, sd, carry, masked):
                rows = pl.ds(pl.multiple_of(i * RB, RB), RB)
                out = []
                for e in range(2):
                    dk, dv = carry[2 * e], carry[2 * e + 1]
                    sl = slice(LANE * e, LANE * (e + 1))
                    col1 = slice(V_DIM * e, V_DIM * e + 1)
                    s, dp = sd[2 * e], sd[2 * e + 1]
                    if masked:
                        s = jnp.where(causal, s, -jnp.inf)
                    p = jnp.exp2((s - lse_ref[rows, col1]).astype(BF16))
                    dv = dv + lax.dot_general(p, dom_scr[e, rows, :], (((0,), (0,)), ((), ())),
                                              preferred_element_type=F32)
                    ds = p * (dp - dl_scr[rows, col1]).astype(BF16)
                    dk = dk + lax.dot_general(ds, q_ref[rows, sl], (((0,), (0,)), ((), ())),
                                              preferred_element_type=F32)
                    dq_scr[rows, sl] += jnp.dot(ds, kbs[e], preferred_element_type=F32)
                    out += [dk, dv]
                return tuple(out)

            def pair(t, c):
                i0 = nq - 2 - 2 * t
                pa, pb = products(i0), products(i0 + 1)
                return update(i0 + 1, pb, update(i0, pa, c, False), False)

            def first_two(c):
                pa, pb = products(j), products(j + 1)
                return update(j + 1, pb, update(j, pa, c, True), False)

            zero = jnp.zeros((RB, LANE), F32)
            below = nq - 1 - j
            carry = lax.fori_loop(0, below // 2, pair, (zero, zero, zero, zero))
            dk0, dv0, dk1, dv1 = lax.cond(below % 2 == 1, first_two,
                                          lambda c: update(j, products(j), c, True), carry)
            dk_ref[cols, 0:LANE] = (dk0 * LN2).astype(BF16)
            dk_ref[cols, LANE:2 * LANE] = (dk1 * LN2).astype(BF16)
            dv_ref[cols, :] = (dv0 + dv1).astype(BF16)
            return 0

        lax.fori_loop(0, nq, kvblock, 0)

        def fin(i, _):
            rows = pl.ds(pl.multiple_of(i * RB, RB), RB)
            dq_ref[rows, :] = dq_scr[rows, :].astype(BF16)
            return 0

        lax.fori_loop(0, nq, fin, 0)
        if n:
            pl.when(pl.program_id(0) == steps - 1)(s_finish)

    two = pl.BlockSpec((lp, 2 * LANE), lambda h: (0, h))
    one = pl.BlockSpec((lp, LANE), lambda h: (0, h))
    anyspec = pl.BlockSpec(memory_space=pl.ANY)
    return pl.pallas_call(
        body, grid=(steps,), in_specs=[two, two, one, one, one, one] + [anyspec] * (2 * n),
        out_specs=[two, two, one] + [anyspec] * n,
        out_shape=[jax.ShapeDtypeStruct((lp, 1024), BF16), jax.ShapeDtypeStruct((lp, 1024), BF16),
                   jax.ShapeDtypeStruct((lp, 512), BF16)] + [jax.ShapeDtypeStruct(b.shape, b.dtype) for b in bufs],
        input_output_aliases={6 + n + a: 3 + a for a in range(n)},
        scratch_shapes=[pltpu.VMEM((2, lp, LANE), BF16), pltpu.VMEM((2, lp, LANE), BF16),
                        pltpu.VMEM((lp, LANE), F32), pltpu.VMEM((lp, 2 * LANE), F32)]
        + (_comm_sems(n) if n else []),
        name="attn_bwd_scatter" if n else "attn_bwd", compiler_params=_cp())(q, k, v, o, do, lse, *xs, *bufs)


def _pool_lane_windows():
    lane = lax.broadcasted_iota(jnp.int32, (1, 256), 1)
    return jnp.where(lane < 64, 2, jnp.where(lane < 128, 4, jnp.where(lane < 192, 8, 16)))


def _by_window(wl, s2, s4, s8, s16):
    return jnp.where(wl == 2, s2, jnp.where(wl == 4, s4, jnp.where(wl == 8, s8, s16)))


def _pool_fwd_rows(pv_ext, t0):
    n = pv_ext.shape[0]
    wl = _pool_lane_windows()
    s2 = pv_ext + _dn(pv_ext, 1)
    s4 = s2 + _dn(s2, 2)
    s8 = s4 + _dn(s4, 4)
    s16 = s8 + _dn(s8, 8)
    t = t0 + lax.broadcasted_iota(jnp.int32, (n, 1), 0)
    cnt = jnp.maximum(jnp.minimum(t + 1, wl), 1).astype(F32)
    return _by_window(wl, s2, s4, s8, s16) / cnt - pv_ext


def _conv_dn(x_ext, w_ref, taps):
    acc = w_ref[taps - 1:taps, :] * x_ext
    for j in range(1, taps):
        acc = acc + w_ref[taps - 1 - j:taps - j, :] * _dn(x_ext, j)
    return acc


def _conv_up(g_ext, w_ref, taps):
    acc = w_ref[taps - 1:taps, :] * g_ext
    for j in range(1, taps):
        acc = acc + w_ref[taps - 1 - j:taps - j, :] * _up(g_ext, j)
    return acc


def _ln_fwd(c, g, b):
    mu = jnp.mean(c, axis=-1, keepdims=True)
    xc = c - mu
    r = lax.rsqrt(jnp.mean(xc * xc, axis=-1, keepdims=True) + EPS)
    xh = xc * r
    return xh * g + b, xh, r


def _halo_specs(lp, width, col):
    per = RB // HB
    last = lp // HB - 1
    cur = pl.BlockSpec((RB, width), lambda i: (i, col))
    prev = pl.BlockSpec((HB, width), lambda i: (jnp.maximum(i * per - 1, 0), col))
    nxt = pl.BlockSpec((HB, width), lambda i: (jnp.minimum((i + 1) * per, last), col))
    return cur, prev, nxt


def _mix_fwd(z, oat, bd, pscale, cw, cb, lng, lnb, sw, lp):
    def body(za, zah, mg, oat_ref, cu, cuh, cg, sbb, sbc, sbch, sbx, sbxh, sg,
             bd_ref, ps_ref, cw_ref, cb_ref, lng_ref, lnb_ref, sw_ref, u_ref):
        i = pl.program_id(0)
        pm = jnp.where(i > 0, 1.0, 0.0).astype(F32)
        pv = jnp.concatenate([zah[:, 0:256] * pm, za[:, 0:256]], axis=0)
        p = _pool_fwd_rows(pv, i * RB - HB)[HB:]
        y = jnp.dot(p.astype(BF16), bd_ref[...], preferred_element_type=F32)
        u_ref[:, 0:256] = (y * ps_ref[...] * _silu(za[:, 256:512])).astype(BF16)
        u_ref[:, 256:768] = (oat_ref[...] * _silu(mg[...])).astype(BF16)
        ce = jnp.concatenate([cuh[...] * pm, cu[...]], axis=0)
        glu = ce[:, 0:256] * _sig(ce[:, 256:512])
        c = _conv_dn(glu, cw_ref, CONF_K)[HB:] + cb_ref[...]
        n, _, _ = _ln_fwd(c, lng_ref[...], lnb_ref[...])
        u_ref[:, 768:1024] = (_silu(n) * _silu(cg[...])).astype(BF16)
        qe = jnp.concatenate([sbch[...] * sbxh[...] * pm, sbc[...] * sbx[...]], axis=0)
        cv = _conv_dn(qe, sw_ref, SC_K)[HB:]
        u_ref[:, 1024:1280] = (sbb[...] * cv * _silu(sg[...])).astype(BF16)

    a_cur, a_prev, _ = _halo_specs(lp, 512, PV0 // 512)
    cu_cur, cu_prev, _ = _halo_specs(lp, 512, CU0 // 512)
    sc_cur, sc_prev, _ = _halo_specs(lp, 256, SBC0 // 256)
    sx_cur, sx_prev, _ = _halo_specs(lp, 256, SBX0 // 256)
    c256 = lambda c0: pl.BlockSpec((RB, 256), lambda i: (i, c0 // 256))
    full = lambda r, c: pl.BlockSpec((r, c), lambda i: (0, 0))
    return pl.pallas_call(
        body, grid=(lp // RB,),
        in_specs=[a_cur, a_prev, pl.BlockSpec((RB, 512), lambda i: (i, MG0 // 512)),
                  pl.BlockSpec((RB, 512), lambda i: (i, 0)),
                  cu_cur, cu_prev, c256(CG0), c256(SBB0), sc_cur, sc_prev, sx_cur, sx_prev, c256(SG0),
                  full(256, 256), full(1, 256), full(32, 256), full(1, 256), full(1, 256), full(1, 256),
                  full(8, 256)],
        out_specs=pl.BlockSpec((RB, 1280), lambda i: (i, 0)),
        out_shape=jax.ShapeDtypeStruct((lp, 1280), BF16),
        name="mix_fwd", compiler_params=_cp())(z, z, z, oat, z, z, z, z, z, z, z, z, z,
                                               bd, pscale, cw, cb, lng, lnb, sw)


def _mix_bwd(z, oat, du, bd, pscale, cw, cb, lng, lnb, sw, lp):
    nb = lp // RB
    ne = RB + 2 * HB
    nf = RB + HB

    def body(za, zah, zan, mg, oat_ref, cu, cuh, cun, cg, cgn, sbb, sbbn, sbc, sbch, sbcn, sbx, sbxh, sbxn,
             sg, sgn, du_ref, dun_ref, bd_ref, ps_ref, cw_ref, cb_ref, lng_ref, lnb_ref, sw_ref,
             dzx_ref, doat_ref, dbd_ref, dcw_ref, dsw_ref, dsm_ref):
        xa, xm, xc = 0, MG0 - PV0, CU0 - PV0
        i = pl.program_id(0)
        pm = jnp.where(i > 0, 1.0, 0.0).astype(F32)
        nm = jnp.where(i < nb - 1, 1.0, 0.0).astype(F32)

        def ext(cur, prev, nxt, sl=slice(None)):
            return jnp.concatenate([prev[:, sl] * pm, cur[:, sl], nxt[:, sl] * nm], axis=0)

        def fwd(cur, nxt, sl=slice(None)):
            return jnp.concatenate([cur[:, sl], nxt[:, sl] * nm], axis=0)

        def csum(x):
            return jnp.sum(x, axis=0, keepdims=True)

        @pl.when(i == 0)
        def _():
            dbd_ref[...] = jnp.zeros((256, 256), F32)
            dcw_ref[...] = jnp.zeros((32, 256), F32)
            dsw_ref[...] = jnp.zeros((8, 256), F32)
            dsm_ref[...] = jnp.zeros((8, 256), F32)

        a_cols, b_cols = slice(0, 256), slice(256, 512)
        pv_e = ext(za, zah, zan, a_cols)
        p = _pool_fwd_rows(pv_e, i * RB - HB)[HB:HB + RB]
        pb = p.astype(BF16)
        y = jnp.dot(pb, bd_ref[...], preferred_element_type=F32)
        pg_f = fwd(za, zan, b_cols)
        dua_f = fwd(du_ref, dun_ref, slice(0, 256))
        dyp_f = dua_f * ps_ref[...] * _silu(pg_f)
        dypb = dyp_f.astype(BF16)
        dp_f = lax.dot_general(dypb, bd_ref[...], (((1,), (1,)), ((), ())), preferred_element_type=F32)
        wl = _pool_lane_windows()
        t = i * RB + lax.broadcasted_iota(jnp.int32, (nf, 1), 0)
        cnt = jnp.minimum(t + 1, wl).astype(F32)
        qf = dp_f / cnt
        f2 = qf + _up(qf, 1)
        f4 = f2 + _up(f2, 2)
        f8 = f4 + _up(f4, 4)
        f16 = f8 + _up(f8, 8)
        dpv = (_by_window(wl, f2, f4, f8, f16) - dp_f)[0:RB]
        dua = du_ref[:, 0:256]
        pg = za[:, b_cols]
        dpg = dua * y * ps_ref[...] * _dsilu(pg)
        dzx_ref[:, xa:xa + 256] = dpv.astype(BF16)
        dzx_ref[:, xa + 256:xa + 512] = dpg.astype(BF16)
        d_scale = csum(dua * y * _silu(pg))
        d_bd = lax.dot_general(pb, dypb[0:RB], (((0,), (0,)), ((), ())), preferred_element_type=F32)

        dub = du_ref[:, 256:768]
        mgv = mg[...]
        dzx_ref[:, xm:xm + 512] = (dub * oat_ref[...] * _dsilu(mgv)).astype(BF16)
        doat_ref[...] = dub * _silu(mgv)

        a_e = ext(cu, cuh, cun, slice(0, 256))
        gt_e = ext(cu, cuh, cun, slice(256, 512))
        sg_e = _sig(gt_e)
        glu_e = a_e * sg_e
        c_f = _conv_dn(glu_e, cw_ref, CONF_K)[HB:] + cb_ref[...]
        n_f, xh_f, r_f = _ln_fwd(c_f, lng_ref[...], lnb_ref[...])
        cg_f = fwd(cg, cgn)
        duc_f = fwd(du_ref, dun_ref, slice(768, 1024))
        sw_f = _silu(n_f)
        dcg = (duc_f * sw_f * _dsilu(cg_f))[0:RB]
        dn_f = duc_f * _silu(cg_f) * _dsilu(n_f)
        a_f = dn_f * lng_ref[...]
        dc_f = r_f * (a_f - jnp.mean(a_f, axis=-1, keepdims=True)
                      - xh_f * jnp.mean(a_f * xh_f, axis=-1, keepdims=True))
        d_lng = csum((dn_f * xh_f)[0:RB])
        d_lnb = csum(dn_f[0:RB])
        d_cb = csum(dc_f[0:RB])
        dglu = _conv_up(dc_f, cw_ref, CONF_K)[0:RB]
        dc_c = dc_f[0:RB]
        for kk in range(CONF_K):
            j = CONF_K - 1 - kk
            dcw_ref[kk:kk + 1, :] += csum(dc_c * _dn(glu_e, j)[HB:HB + RB])

        sgc = sg_e[HB:HB + RB]
        a_c = a_e[HB:HB + RB]
        dzx_ref[:, xc:xc + 256] = (dglu * sgc).astype(BF16)
        dzx_ref[:, xc + 256:xc + 512] = (dglu * a_c * sgc * (1.0 - sgc)).astype(BF16)
        dzx_ref[:, xc + 512:xc + 768] = dcg.astype(BF16)

        c_e = ext(sbc, sbch, sbcn)
        x_e = ext(sbx, sbxh, sbxn)
        q_e = c_e * x_e
        cv_f = _conv_dn(q_e, sw_ref, SC_K)[HB:]
        bg_f = fwd(sbb, sbbn)
        sg_f = fwd(sg, sgn)
        dud_f = fwd(du_ref, dun_ref, slice(1024, 1280))
        ssg_f = _silu(sg_f)
        dcv_f = dud_f * bg_f * ssg_f
        dbg = (dud_f * cv_f * ssg_f)[0:RB]
        dsg = (dud_f * bg_f * cv_f * _dsilu(sg_f))[0:RB]
        dq = _conv_up(dcv_f, sw_ref, SC_K)[0:RB]
        dcv_c = dcv_f[0:RB]
        for kk in range(SC_K):
            j = SC_K - 1 - kk
            dsw_ref[kk:kk + 1, :] += csum(dcv_c * _dn(q_e, j)[HB:HB + RB])

        dzx_ref[:, xc + 768:xc + 1024] = dbg.astype(BF16)
        dzx_ref[:, xc + 1024:xc + 1280] = (dq * x_e[HB:HB + RB]).astype(BF16)
        dzx_ref[:, xc + 1280:xc + 1536] = (dq * c_e[HB:HB + RB]).astype(BF16)
        dzx_ref[:, xc + 1536:xc + 1792] = dsg.astype(BF16)

        dbd_ref[...] += d_bd
        dsm_ref[0:1, :] += d_scale
        dsm_ref[1:2, :] += d_cb
        dsm_ref[2:3, :] += d_lng
        dsm_ref[3:4, :] += d_lnb

    a3 = _halo_specs(lp, 512, PV0 // 512)
    cu3 = _halo_specs(lp, 512, CU0 // 512)
    cg3 = _halo_specs(lp, 256, CG0 // 256)
    sbb3 = _halo_specs(lp, 256, SBB0 // 256)
    sbc3 = _halo_specs(lp, 256, SBC0 // 256)
    sbx3 = _halo_specs(lp, 256, SBX0 // 256)
    sg3 = _halo_specs(lp, 256, SG0 // 256)
    du3 = _halo_specs(lp, 1280, 0)
    full = lambda r, c: pl.BlockSpec((r, c), lambda i: (0, 0))
    in_specs = [a3[0], a3[1], a3[2], pl.BlockSpec((RB, 512), lambda i: (i, MG0 // 512)),
                pl.BlockSpec((RB, 512), lambda i: (i, 0)),
                cu3[0], cu3[1], cu3[2], cg3[0], cg3[2], sbb3[0], sbb3[2],
                sbc3[0], sbc3[1], sbc3[2], sbx3[0], sbx3[1], sbx3[2], sg3[0], sg3[2],
                du3[0], du3[2],
                full(256, 256), full(1, 256), full(32, 256), full(1, 256), full(1, 256), full(1, 256),
                full(8, 256)]
    out_specs = [pl.BlockSpec((RB, ZW - PV0), lambda i: (i, 0)), pl.BlockSpec((RB, 512), lambda i: (i, 0)),
                 full(256, 256), full(32, 256), full(8, 256), full(8, 256)]
    out_shape = [jax.ShapeDtypeStruct((lp, ZW - PV0), BF16), jax.ShapeDtypeStruct((lp, 512), F32),
                 jax.ShapeDtypeStruct((256, 256), F32), jax.ShapeDtypeStruct((32, 256), F32),
                 jax.ShapeDtypeStruct((8, 256), F32), jax.ShapeDtypeStruct((8, 256), F32)]
    return pl.pallas_call(
        body, grid=(nb,), in_specs=in_specs, out_specs=out_specs, out_shape=out_shape,
        name="mix_bwd", compiler_params=_cp())(
            z, z, z, z, oat, z, z, z, z, z, z, z, z, z, z, z, z, z, z, z, du, du,
            bd, pscale, cw, cb, lng, lnb, sw)


U_OFF = (0, 256, 768, 1024, 1280)
MRB = 192


def _merge_fwd(x, u, z, gb, wout, wo, gpost, lp):
    MRB = RB

    def body(x_ref, u_ref, gl_ref, gb_ref, wout_ref, wo_ref, g_ref, xo_ref, m_ref, o2_ref):
        m = jnp.zeros((MRB, D), F32)
        for b in range(4):
            y = jnp.dot(u_ref[:, U_OFF[b]:U_OFF[b + 1]], wout_ref[U_OFF[b]:U_OFF[b + 1], :],
                        preferred_element_type=F32)
            sl = slice(D * b, D * (b + 1))
            m = m + _sig(gl_ref[:, sl] + gb_ref[:, sl]) * y
        mb = m.astype(BF16)
        m_ref[...] = mb
        o2 = jnp.dot(mb, wo_ref[...], preferred_element_type=F32)
        o2_ref[...] = o2
        r = lax.rsqrt(jnp.mean(o2 * o2, axis=-1, keepdims=True) + EPS)
        xo_ref[...] = x_ref[...] + o2 * r * g_ref[...]

    blk = pl.BlockSpec((MRB, D), lambda i: (i, 0))
    full = lambda r, c: pl.BlockSpec((r, c), lambda i: (0, 0))
    return pl.pallas_call(
        body, grid=(lp // MRB,),
        in_specs=[blk, pl.BlockSpec((MRB, 1280), lambda i: (i, 0)), pl.BlockSpec((MRB, 4096), lambda i: (i, 0)),
                  full(1, 4096), full(1280, D), full(D, D), full(1, D)],
        out_specs=[blk, blk, blk],
        out_shape=[jax.ShapeDtypeStruct((lp, D), F32), jax.ShapeDtypeStruct((lp, D), BF16),
                   jax.ShapeDtypeStruct((lp, D), F32)],
        name="merge_fwd", compiler_params=_cp())(x, u, z, gb, wout, wo, gpost)


def _merge_bwd(dx, o2, u, z, gb, wout, wo, gpost, lp):
    MRB = RB

    def body(dx_ref, o2_ref, u_ref, gl_ref, gb_ref, wout_ref, wo_ref, g_ref,
             do2_ref, dgl_ref, dy_ref, du_ref, dgb_ref, dg_ref):
        i = pl.program_id(0)
        o2 = o2_ref[...]
        dy = dx_ref[...]
        r = lax.rsqrt(jnp.mean(o2 * o2, axis=-1, keepdims=True) + EPS)
        a = dy * g_ref[...]
        do2 = (r * a - o2 * (r * r * r) * jnp.mean(a * o2, axis=-1, keepdims=True)).astype(BF16)
        do2_ref[...] = do2
        dg = jnp.sum(dy * o2 * r, axis=0, keepdims=True)
        dm = lax.dot_general(do2, wo_ref[...], (((1,), (1,)), ((), ())), preferred_element_type=F32)
        for b in range(4):
            rows = slice(U_OFF[b], U_OFF[b + 1])
            y = jnp.dot(u_ref[:, rows], wout_ref[rows, :], preferred_element_type=F32)
            sl = slice(D * b, D * (b + 1))
            gt = _sig(gl_ref[:, sl] + gb_ref[:, sl])
            dgl = dm * y * gt * (1.0 - gt)
            dgl_ref[:, sl] = dgl.astype(BF16)
            part = jnp.sum(dgl, axis=0, keepdims=True)

            @pl.when(i == 0)
            def _(part=part, sl=sl):
                dgb_ref[:, sl] = part

            @pl.when(i > 0)
            def _(part=part, sl=sl):
                dgb_ref[:, sl] += part

            dyb = (dm * gt).astype(BF16)
            dy_ref[:, sl] = dyb
            du_ref[:, rows] = lax.dot_general(dyb, wout_ref[rows, :], (((1,), (1,)), ((), ())),
                                              preferred_element_type=F32)

        @pl.when(i == 0)
        def _():
            dg_ref[...] = dg

        @pl.when(i > 0)
        def _():
            dg_ref[...] += dg

    blk = pl.BlockSpec((MRB, D), lambda i: (i, 0))
    wide = pl.BlockSpec((MRB, 4096), lambda i: (i, 0))
    ub = pl.BlockSpec((MRB, 1280), lambda i: (i, 0))
    full = lambda r, c: pl.BlockSpec((r, c), lambda i: (0, 0))
    once = lambda r, c: pl.BlockSpec((r, c), lambda i: (0, 0), pipeline_mode=pl.Buffered(1))
    return pl.pallas_call(
        body, grid=(lp // MRB,),
        in_specs=[blk, blk, ub, wide, full(1, 4096), once(1280, D), once(D, D), full(1, D)],
        out_specs=[blk, wide, wide, ub, full(1, 4096), full(1, D)],
        out_shape=[jax.ShapeDtypeStruct((lp, D), BF16), jax.ShapeDtypeStruct((lp, 4096), BF16),
                   jax.ShapeDtypeStruct((lp, 4096), BF16), jax.ShapeDtypeStruct((lp, 1280), F32),
                   jax.ShapeDtypeStruct((1, 4096), F32), jax.ShapeDtypeStruct((1, D), F32)],
        name="merge_bwd", compiler_params=_cp())(dx, o2, u, z, gb, wout, wo, gpost)


def _dh(dgl, dzq, dzx, w_in, lp, scatter=None):
    xs, bufs, layer = scatter if scatter else ((), (), None)
    n = len(xs)
    steps = lp // RB
    segs = ((0, CQ0), (CQ0, PV0), (PV0, ZW))

    def body(gl_ref, zq_ref, zx_ref, w_ref, *rest):
        o_ref = rest[2 * n]
        if n:
            s_start, s_finish = _scatter_phases(rest[:n], rest[2 * n + 1:3 * n + 1], *rest[3 * n + 1:], layer)
            pl.when(pl.program_id(0) == 0)(s_start)
        acc = None
        for a_ref, (lo, hi) in zip((gl_ref, zq_ref, zx_ref), segs):
            part = lax.dot_general(a_ref[...], w_ref[:, lo:hi], (((1,), (1,)), ((), ())),
                                   preferred_element_type=F32)
            acc = part if acc is None else acc + part
        o_ref[...] = acc
        if n:
            pl.when(pl.program_id(0) == steps - 1)(s_finish)

    anyspec = pl.BlockSpec(memory_space=pl.ANY)
    row = lambda w: pl.BlockSpec((RB, w), lambda i: (i, 0))
    return pl.pallas_call(
        body, grid=(steps,),
        in_specs=[row(CQ0), row(PV0 - CQ0), row(ZW - PV0), pl.BlockSpec((D, ZW), lambda i: (0, 0))]
        + [anyspec] * (2 * n),
        out_specs=[row(D)] + [anyspec] * n,
        out_shape=[jax.ShapeDtypeStruct((lp, D), F32)] + [jax.ShapeDtypeStruct(b.shape, b.dtype) for b in bufs],
        input_output_aliases={4 + n + a: 1 + a for a in range(n)},
        scratch_shapes=_comm_sems(n) if n else [],
        name="dh_scatter" if n else "dh", compiler_params=_cp())(dgl, dzq, dzx, w_in, *xs, *bufs)


def _loss_head(xf, tgt, n_real, lp):
    def body(x_ref, t_ref, dy_ref, ls_ref):
        i = pl.program_id(0)
        t = i * RB + lax.broadcasted_iota(jnp.int32, (RB, 1), 0)
        real = (t >= N_META) & (t < n_real)
        err = jnp.where(real, x_ref[...] - t_ref[...], 0.0)
        dy_ref[...] = err / D
        part = 0.5 * jnp.sum(jnp.mean(err * err, axis=-1, keepdims=True), axis=0, keepdims=True)
        part = jnp.broadcast_to(part, (8, LANE))

        @pl.when(i == 0)
        def _():
            ls_ref[...] = part

        @pl.when(i > 0)
        def _():
            ls_ref[...] += part

    blk = pl.BlockSpec((RB, D), lambda i: (i, 0))
    return pl.pallas_call(
        body, grid=(lp // RB,), in_specs=[blk, blk],
        out_specs=[blk, pl.BlockSpec((8, LANE), lambda i: (0, 0))],
        out_shape=[jax.ShapeDtypeStruct((lp, D), F32), jax.ShapeDtypeStruct((8, LANE), F32)],
        name="loss_head", compiler_params=_cp())(xf, tgt)


def _peer(d):
    x, y, c = lax.axis_index("x"), lax.axis_index("y"), lax.axis_index("c")
    return (x ^ ((d >> 2) & 1), y ^ ((d >> 1) & 1), c ^ (d & 1))


def _index_of(p):
    return 4 * p[0] + 2 * p[1] + p[2]


def _all_gather(xs, name):
    n = len(xs)

    def body(*refs):
        start, forward, finish = _gather_phases(refs[:n], refs[n:2 * n], *refs[2 * n:])
        start()
        forward()
        finish()

    anyspec = pl.BlockSpec(memory_space=pl.ANY)
    return pl.pallas_call(
        body, in_specs=[anyspec] * n, out_specs=[anyspec] * n,
        out_shape=_gather_shapes(xs), scratch_shapes=_comm_sems(n), name=name)(*xs)


def _gather_shapes(xs):
    return [jax.ShapeDtypeStruct((N_DEV,) + x.shape, x.dtype) for x in xs]


def _comm_sems(n):
    return [pltpu.SemaphoreType.DMA((7 * n,)), pltpu.SemaphoreType.DMA((7 * n,)), pltpu.SemaphoreType.DMA((n,))]


def _gather_phases(x_refs, out_refs, send_sems, recv_sems, local_sems):
    n = len(x_refs)
    chips = [2, 4, 6]

    def copy(a, kk, block, to, src=None):
        slot = out_refs[a].at[_index_of(block)]
        return pltpu.make_async_remote_copy(
            src_ref=slot if src is None else src, dst_ref=slot,
            send_sem=send_sems.at[7 * a + kk], recv_sem=recv_sems.at[7 * a + kk], device_id=to,
            device_id_type=MESH)

    def local(a):
        return pltpu.make_async_copy(x_refs[a], out_refs[a].at[_index_of(_peer(0))], local_sems.at[a])

    def firsts():
        out = []
        for a in range(n):
            out.append(copy(a, 0, _peer(0), _peer(1), src=x_refs[a]))
            out += [copy(a, 1 + j, _peer(0), _peer(d), src=x_refs[a]) for j, d in enumerate(chips)]
        return out

    def passes():
        return [copy(a, 4 + j, _peer(d), _peer(1)) for j, d in enumerate(chips) for a in range(n)]

    def start():
        for a in range(n):
            local(a).start()
        for cp in firsts():
            cp.start()

    def forward():
        for j, d in enumerate(chips):
            for a in range(n):
                copy(a, 1 + j, _peer(d), _peer(0)).wait_recv()
                copy(a, 4 + j, _peer(d), _peer(1)).start()

    def finish():
        for a in range(n):
            copy(a, 0, _peer(1), _peer(0)).wait_recv()
            for j, d in enumerate(chips):
                copy(a, 4 + j, _peer(d | 1), _peer(0)).wait_recv()
        for cp in firsts() + passes():
            cp.wait_send()
        for a in range(n):
            local(a).wait()

    return start, forward, finish


def _scatter_phases(x_refs, out_refs, send_sems, recv_sems, local_sems, layer):
    n = len(x_refs)

    def land(a, dev):
        slot = out_refs[a].at[dev]
        return slot if layer is None else slot.at[layer]

    def local(a):
        my = _index_of(_peer(0))
        return pltpu.make_async_copy(x_refs[a].at[my], land(a, my), local_sems.at[a])

    def copy(a, d):
        my = _index_of(_peer(0))
        return pltpu.make_async_remote_copy(
            src_ref=x_refs[a].at[_index_of(_peer(d))], dst_ref=land(a, my),
            send_sem=send_sems.at[7 * a + d - 1], recv_sem=recv_sems.at[7 * a + d - 1], device_id=_peer(d),
            device_id_type=MESH)

    def arrival(a, d):
        frm = _index_of(_peer(d))
        return pltpu.make_async_remote_copy(
            src_ref=x_refs[a].at[frm], dst_ref=land(a, frm),
            send_sem=send_sems.at[7 * a + d - 1], recv_sem=recv_sems.at[7 * a + d - 1], device_id=_peer(d),
            device_id_type=MESH)

    def start():
        for a in range(n):
            local(a).start()
        for d in range(1, N_DEV):
            for a in range(n):
                copy(a, d).start()

    def finish():
        for d in range(1, N_DEV):
            for a in range(n):
                arrival(a, d).wait_recv()
        for d in range(1, N_DEV):
            for a in range(n):
                copy(a, d).wait_send()
        for a in range(n):
            local(a).wait()

    return start, finish


def _all_to_all(xs, bufs, layer, name):
    n = len(xs)

    def body(*refs):
        start, finish = _scatter_phases(refs[:n], refs[2 * n:3 * n], *refs[3 * n:], layer)
        start()
        finish()

    anyspec = pl.BlockSpec(memory_space=pl.ANY)
    return pl.pallas_call(
        body, in_specs=[anyspec] * (2 * n), out_specs=[anyspec] * n,
        out_shape=[jax.ShapeDtypeStruct(b.shape, b.dtype) for b in bufs],
        input_output_aliases={n + a: a for a in range(n)},
        scratch_shapes=_comm_sems(n), name=name)(*xs, *bufs)


def _adam_math(g, w, m, v):
    c1 = 1.0 - ADAM_B1 ** ADAM_STEP
    c2 = 1.0 - ADAM_B2 ** ADAM_STEP
    mn = ADAM_B1 * m + (1.0 - ADAM_B1) * g
    vn = ADAM_B2 * v + (1.0 - ADAM_B2) * (g * g)
    return -ADAM_LR * ((mn / c1) / (jnp.sqrt(vn / c2) + ADAM_EPS) + ADAM_WD * w), mn, vn


def _reduce_adam(parts, w, m, v, rb, name, row_off=0):
    depth, rows, cols = w.shape
    assert rows % rb == 0 and row_off % rb == 0

    def body(p_ref, w_ref, m_ref, v_ref, g_ref, d_ref, mo_ref, vo_ref):
        g = p_ref[0, 0].astype(F32)
        for j in range(1, N_DEV):
            g = g + p_ref[j, 0].astype(F32)
        g_ref[0] = g
        d_ref[0], mo_ref[0], vo_ref[0] = _adam_math(g, w_ref[0], m_ref[0], v_ref[0])

    blk = pl.BlockSpec((1, rb, cols), lambda l, i: (l, i, 0))
    out = jax.ShapeDtypeStruct(w.shape, F32)
    return pl.pallas_call(
        body, grid=(depth, rows // rb),
        in_specs=[pl.BlockSpec((N_DEV, 1, rb, cols), lambda l, i: (0, l, i + row_off // rb, 0)), blk, blk, blk],
        out_specs=[blk, blk, blk, blk], out_shape=[out, out, out, out],
        name=name, compiler_params=_cp())(parts, w, m, v)


def _reduce_adam_flat(parts, w, m, v, name):
    q_rows = w.shape[0]

    def body(p_ref, w_ref, m_ref, v_ref, g_ref, d_ref, mo_ref, vo_ref):
        g = p_ref[0].astype(F32)
        for j in range(1, N_DEV):
            g = g + p_ref[j].astype(F32)
        g_ref[...] = g
        d_ref[...], mo_ref[...], vo_ref[...] = _adam_math(g, w_ref[...], m_ref[...], v_ref[...])

    blk = pl.BlockSpec((q_rows, LANE), lambda i: (0, 0))
    out = jax.ShapeDtypeStruct((q_rows, LANE), F32)
    return pl.pallas_call(
        body, grid=(1,), in_specs=[pl.BlockSpec((N_DEV, q_rows, LANE), lambda i: (0, 0, 0)), blk, blk, blk],
        out_specs=[blk, blk, blk, blk], out_shape=[out, out, out, out],
        name=name, compiler_params=_cp())(parts, w, m, v)


C128 = (("w_out_pool", 256), ("w_out_mla", 512), ("w_out_conf", 256), ("w_out_sc", 256), ("w_ukv", 128))
C128_ROWS = sum(r for _, r in C128)
TAIL = (("meta_tokens", (N_META, 128)), ("conf_dw_w", (DEPTH, CONF_K, 32)), ("sc_dw_w", (DEPTH, SC_K, 32)))
TAIL_ROWS = sum(int(np.prod(s)) for _, s in TAIL) // LANE
TAIL_PAD = 56
SMALL = (("pre_norm_g", (DEPTH, D)), ("gate_bias", (DEPTH, 4096)), ("pool_w", (DEPTH, 4, 64, 64)),
         ("pool_scale", (DEPTH, 256)), ("q_norm_g", (DEPTH, 256)), ("kv_norm_g", (DEPTH, 128)),
         ("conf_dw_b", (DEPTH, 256)), ("conf_ln_g", (DEPTH, 256)), ("conf_ln_b", (DEPTH, 256)),
         ("post_norm_g", (DEPTH, D)))
SMALL_ROWS = sum(int(np.prod(s)) for _, s in SMALL) // LANE
SMALL_PAD = -(-(SMALL_ROWS + 1) // 8) * 8


def _pack_tail(t):
    parts = [t[n].reshape(-1, LANE) for n, _ in TAIL]
    parts.append(jnp.zeros((TAIL_PAD - TAIL_ROWS, LANE), F32))
    return jnp.concatenate(parts, axis=0)


def _unpack_tail(flat):
    out, off = {}, 0
    for n, s in TAIL:
        rows = int(np.prod(s)) // LANE
        out[n] = flat[off:off + rows].reshape(s)
        off += rows
    return out


def _unpack_tail_full(g):
    out, off = {}, 0
    for n, s in TAIL:
        rows = int(np.prod(s)) // LANE
        blk = jnp.moveaxis(g[:, off:off + rows].reshape((N_DEV,) + s), 0, -2)
        out[n] = blk.reshape(s[:-1] + (N_DEV * s[-1],))
        off += rows
    return out


def _pack_small(t, extra_row):
    parts = [t[n].reshape(-1, LANE) for n, _ in SMALL] + [extra_row]
    parts.append(jnp.zeros((SMALL_PAD - SMALL_ROWS - 1, LANE), F32))
    return jnp.concatenate(parts, axis=0)


def _unpack_small(flat):
    out, off = {}, 0
    for n, s in SMALL:
        rows = int(np.prod(s)) // LANE
        out[n] = flat[off:off + rows].reshape(s)
        off += rows
    return out


def _cols_by_dest(g, width):
    r = g.shape[0]
    return g.reshape(r, N_DEV, width).transpose(1, 0, 2)


def _cols_full(gathered):
    _, r, c = gathered.shape
    return gathered.transpose(1, 0, 2).reshape(r, N_DEV * c)


W_IN_SHARD = 916
PACKED_SEGS = ((3232, 7328), (512, 896), 64, (896, 928), 32, (0, 512), (928, 3232))


def _pack_w_in(g):
    parts = []
    for seg in PACKED_SEGS:
        if isinstance(seg, int):
            parts.append(jnp.zeros((g.shape[1], seg), g.dtype))
            continue
        a, b = seg
        while a < b:
            k = a // W_IN_SHARD
            hi = min(b, W_IN_SHARD * (k + 1))
            parts.append(g[k, :, a - W_IN_SHARD * k:hi - W_IN_SHARD * k])
            a = hi
    return jnp.concatenate(parts, axis=1)


def _w_in_grad_by_dest(gl, mla, mix):
    src = (((0, 512), mix, 0), ((512, 896), mla, 0), ((896, 928), mla, 448), ((928, 3232), mix, 512),
           ((3232, 7328), gl, 0))
    blocks = []
    for k in range(N_DEV):
        lo, hi = W_IN_SHARD * k, W_IN_SHARD * (k + 1)
        parts = []
        for (a, b), arr, off in src:
            s, e = max(a, lo), min(b, hi)
            if s < e:
                parts.append(arr[:, off + s - a:off + e - a])
        blocks.append(jnp.concatenate(parts, axis=1))
    return jnp.stack(blocks)


def _rope_tables(lp):
    inv = 1.0 / (ROPE_THETA ** (jnp.arange(0, QK_ROPE, 2, dtype=F32) / QK_ROPE))
    ang = jnp.arange(lp, dtype=F32)[:, None] * inv[None, :]
    cos, sin = jnp.cos(ang), jnp.sin(ang)
    one = jnp.ones((lp, QK_NOPE), F32)
    zero = jnp.zeros((lp, QK_NOPE), F32)
    z16 = jnp.zeros((lp, 16), F32)
    c = jnp.concatenate([one, cos, cos, jnp.ones((lp, 32), F32)], axis=1)
    s1 = jnp.concatenate([zero, z16, sin, jnp.zeros((lp, 32), F32)], axis=1)
    s2 = jnp.concatenate([zero, -sin, z16, jnp.zeros((lp, 32), F32)], axis=1)
    return c, s1, s2


def kernel(x, meta_tokens, pre_norm_g, w_in, gate_bias, pool_w, pool_scale, w_out_pool, q_norm_g, w_uq, kv_norm_g, w_ukv, w_out_mla, conf_dw_w, conf_dw_b, conf_ln_g, conf_ln_b, w_out_conf, sc_dw_w, w_out_sc, w_o, post_norm_g, loss_target, m_meta_tokens, m_pre_norm_g, m_w_in, m_gate_bias, m_pool_w, m_pool_scale, m_w_out_pool, m_q_norm_g, m_w_uq, m_kv_norm_g, m_w_ukv, m_w_out_mla, m_conf_dw_w, m_conf_dw_b, m_conf_ln_g, m_conf_ln_b, m_w_out_conf, m_sc_dw_w, m_w_out_sc, m_w_o, m_post_norm_g, v_meta_tokens, v_pre_norm_g, v_w_in, v_gate_bias, v_pool_w, v_pool_scale, v_w_out_pool, v_q_norm_g, v_w_uq, v_kv_norm_g, v_w_ukv, v_w_out_mla, v_conf_dw_w, v_conf_dw_b, v_conf_ln_g, v_conf_ln_b, v_w_out_conf, v_sc_dw_w, v_w_out_sc, v_w_o, v_post_norm_g):
    names = ["w_in", "w_uq", "w_o"] + [n for n, _ in C128] + [n for n, _ in TAIL] + [n for n, _ in SMALL]
    loc = locals()
    w = {n: loc[n] for n in names}
    mom = {n: loc["m_" + n] for n in names}
    vel = {n: loc["v_" + n] for n in names}

    seq = x.shape[1]
    n_real = N_META + seq
    lp = -(-n_real // RB) * RB
    tmb = lp // 3
    tabs = _rope_tables(lp)

    c128 = jnp.concatenate([w[n] for n, _ in C128], axis=1)
    def shards_of(i):
        return [w_in[i].astype(BF16), c128[i].astype(BF16), w_uq[i].astype(BF16), w_o[i].astype(BF16)]

    gathered = [_all_gather(shards_of(0), "gather_weights")] + [None] * (DEPTH - 1)
    tail_w = _pack_tail(w)
    tail = _unpack_tail_full(_all_gather([tail_w], "gather_tail")[0])
    eye4 = jnp.eye(4, dtype=F32)
    bd_all = (pool_w[:, :, :, None, :] * eye4[None, :, None, :, None]).reshape(DEPTH, 256, 256).astype(BF16)

    def layer_weights(i):
        g_in, g_c128, g_uq, g_o = gathered[i]
        lw = {}
        lw["w_in"] = _pack_w_in(g_in)
        lw["wc"] = _cols_full(g_c128)
        wuq = _cols_full(g_uq).reshape(Q_RANK, HEADS, 96)
        lw["w_uq"] = jnp.pad(wuq, ((0, 0), (0, 0), (0, 32))).reshape(Q_RANK, HEADS * LANE)
        wukv = lw["wc"][U_OFF[4]:].reshape(KV_RANK, HEADS, 128)
        wk = jnp.pad(wukv[:, :, :QK_NOPE], ((0, 0), (0, 0), (0, 64))).reshape(KV_RANK, HEADS * LANE)
        lw["w_ukv"] = jnp.concatenate([wk, wukv[:, :, QK_NOPE:].reshape(KV_RANK, HEADS * V_DIM)], axis=1)
        lw["w_o"] = g_o.reshape(D, D)
        lw["bd"] = bd_all[i]
        lw["cw"] = jnp.pad(tail["conf_dw_w"][i], ((0, 1), (0, 0)))
        lw["sw"] = jnp.pad(tail["sc_dw_w"][i], ((0, 8 - SC_K), (0, 0)))
        return lw

    meta_full = tail["meta_tokens"]

    pad_rows = lp - n_real
    xr = jnp.concatenate([meta_full, x[0], jnp.zeros((pad_rows, D), F32)], axis=0)
    tgt = jnp.pad(loss_target[0], ((N_META, pad_rows), (0, 0)))
    saved = []
    for i in range(DEPTH):
        lw = layer_weights(i)
        h = _rms_fwd(xr, pre_norm_g[i:i + 1], lp)
        z = _mm(h, lw["w_in"], lp, ZW, D, tm=RB, tn=ZW // 2, tk=D, n_outer=True, name="mm_in")
        qn, kvn, krr = _mla_prep(z, q_norm_g[i:i + 1], kv_norm_g[i:i + 1], tabs, lp)
        q_raw = _mm(qn, lw["w_uq"], lp, 1024, Q_RANK, tm=tmb, tn=1024, tk=Q_RANK, name="mm_uq")
        kv_raw = _mm(kvn, lw["w_ukv"], lp, 1536, KV_RANK, tm=tmb, tn=512, tk=KV_RANK, name="mm_ukv")
        qt, kt, vt = _mla_post(q_raw, kv_raw, krr, tabs, lp)
        res = _attn_fwd(qt, kt, vt, lp, gather=shards_of(i + 1) if i + 1 < DEPTH else ())
        oat, lse = res[0], res[1]
        if i + 1 < DEPTH:
            gathered[i + 1] = res[2:]
        u = _mix_fwd(z, oat, lw["bd"], pool_scale[i:i + 1], lw["cw"], conf_dw_b[i:i + 1], conf_ln_g[i:i + 1],
                     conf_ln_b[i:i + 1], lw["sw"], lp)
        x_new, m_act, o2 = _merge_fwd(xr, u, z, gate_bias[i:i + 1], lw["wc"], lw["w_o"],
                                      post_norm_g[i:i + 1], lp)
        saved.append(dict(lw=lw, x=xr, h=h, z=z, qn=qn, kvn=kvn, qt=qt, kt=kt, vt=vt, oat=oat, lse=lse,
                          u=u, m=m_act, o2=o2))
        xr = x_new

    dx, loss_part = _loss_head(xr, tgt, n_real, lp)

    gsm = {n: [None] * DEPTH for n, _ in SMALL}
    g_cw = [None] * DEPTH
    g_sw = [None] * DEPTH
    recv = [lax.empty((N_DEV, DEPTH) + s, BF16) for s in ((D, 916), (C128_ROWS, 128), (Q_RANK, 96), (128, D))]
    pending = None
    for i in reversed(range(DEPTH)):
        s = saved[i]
        lw = s["lw"]
        do2, dgl, dyb, du, dgb, dgpost = _merge_bwd(dx, s["o2"], s["u"], s["z"], gate_bias[i:i + 1],
                                                    lw["wc"], lw["w_o"], post_norm_g[i:i + 1], lp)
        d_wo = _mm(s["m"], do2, D, D, lp, ta=True, tm=512, tn=D, tk=tmb, out_dtype=BF16, name="mm_dwo")
        d_wout = []
        for b in range(4):
            rows = U_OFF[b + 1] - U_OFF[b]
            d_wout.append(_mm(s["u"], dyb, rows, D, lp, ta=True, tm=256, tn=D, tk=tmb, out_dtype=BF16,
                              a_moff=U_OFF[b] // 256, b_noff=b, name="mm_dwout%d" % b))
        dzx, doat, dbd, dcw, dsw, dsm = _mix_bwd(
            s["z"], s["oat"], du, lw["bd"], pool_scale[i:i + 1], lw["cw"], conf_dw_b[i:i + 1],
            conf_ln_g[i:i + 1], conf_ln_b[i:i + 1], lw["sw"], lp)
        res = _attn_bwd(s["qt"], s["kt"], s["vt"], s["oat"], doat, s["lse"], lp,
                        scatter=(pending, recv, i + 1) if pending else None)
        dqt, dkt, dvt = res[:3]
        if pending:
            recv = list(res[3:])
        dq_raw, dkv_raw, dkr = _mla_post_bwd(dqt, dkt, dvt, tabs, lp)
        dqn = _mm(dq_raw, lw["w_uq"], lp, Q_RANK, 1024, tb=True, tm=tmb, tn=Q_RANK, tk=1024, name="mm_dqn")
        d_wuq = _mm(s["qn"], dq_raw, Q_RANK, 1024, lp, ta=True, tm=Q_RANK, tn=1024, tk=tmb, out_dtype=BF16,
                    name="mm_dwuq")
        dkvn = _mm(dkv_raw, lw["w_ukv"], lp, KV_RANK, 1536, tb=True, tm=tmb, tn=KV_RANK, tk=1536, name="mm_dkvn")
        d_wukv = _mm(s["kvn"], dkv_raw, KV_RANK, 1536, lp, ta=True, tm=KV_RANK, tn=1536, tk=tmb, out_dtype=BF16,
                     name="mm_dwukv")
        dzq, dqg, dkvg = _mla_prep_bwd(s["z"], q_norm_g[i:i + 1], kv_norm_g[i:i + 1], dqn, dkvn, dkr, lp)
        d_win = [_mm(s["h"], seg, D, seg.shape[1], lp, ta=True, tm=D, tn=tn, tk=tmb, out_dtype=BF16,
                     name="mm_dwin%d" % k) for k, (seg, tn) in enumerate(((dgl, 1024), (dzq, 512), (dzx, 1408)))]
        d_wuq_o = d_wuq.reshape(Q_RANK, HEADS, LANE)[:, :, :96].reshape(Q_RANK, HEADS * 96)
        d_wukv_o = jnp.concatenate([d_wukv[:, :1024].reshape(KV_RANK, HEADS, LANE)[:, :, :QK_NOPE],
                                    d_wukv[:, 1024:].reshape(KV_RANK, HEADS, V_DIM)], axis=2).reshape(KV_RANK, 1024)
        pending = [
            _w_in_grad_by_dest(*d_win),
            _cols_by_dest(jnp.concatenate(d_wout + [d_wukv_o], axis=0), 128),
            _cols_by_dest(d_wuq_o, 96),
            d_wo.reshape(N_DEV, 128, D)]
        res = _dh(dgl, dzq, dzx, lw["w_in"], lp, scatter=(pending, recv, 0) if i == 0 else None)
        dh = res[0]
        if i == 0:
            recv = list(res[1:])
        dx, dgpre = _rms_bwd(s["x"], pre_norm_g[i:i + 1], dh, dx, lp)

        gsm["pre_norm_g"][i] = dgpre[0]
        gsm["gate_bias"][i] = dgb[0]
        gsm["pool_w"][i] = jnp.stack([dbd[64 * g:64 * (g + 1), 64 * g:64 * (g + 1)] for g in range(4)])
        gsm["pool_scale"][i] = dsm[0]
        gsm["conf_dw_b"][i] = dsm[1]
        gsm["conf_ln_g"][i] = dsm[2]
        gsm["conf_ln_b"][i] = dsm[3]
        gsm["q_norm_g"][i] = dqg[0]
        gsm["kv_norm_g"][i] = dkvg[0]
        gsm["post_norm_g"][i] = dgpost[0]
        g_cw[i] = dcw[:CONF_K]
        g_sw[i] = dsw[:SC_K]

    outs = [dict() for _ in range(4)]

    def put(n, res):
        for t, r in zip(outs, res):
            t[n] = r

    put("w_in", _reduce_adam(recv[0], w["w_in"], mom["w_in"], vel["w_in"], 256, "adam_w_in"))
    off = 0
    for n, rows in C128:
        put(n, _reduce_adam(recv[1], w[n], mom[n], vel[n], 128, "adam_" + n, row_off=off))
        off += rows
    put("w_uq", _reduce_adam(recv[2], w["w_uq"], mom["w_uq"], vel["w_uq"], Q_RANK, "adam_w_uq"))
    put("w_o", _reduce_adam(recv[3], w["w_o"], mom["w_o"], vel["w_o"], 128, "adam_w_o"))

    tail_g = {"meta_tokens": _cols_by_dest(dx[:N_META], 128),
              "conf_dw_w": jnp.moveaxis(jnp.stack(g_cw).reshape(DEPTH, CONF_K, N_DEV, 32), 2, 0),
              "sc_dw_w": jnp.moveaxis(jnp.stack(g_sw).reshape(DEPTH, SC_K, N_DEV, 32), 2, 0)}
    tail_bd = jnp.concatenate([tail_g[n].reshape(N_DEV, -1, LANE) for n, _ in TAIL]
                              + [jnp.zeros((N_DEV, TAIL_PAD - TAIL_ROWS, LANE), F32)], axis=1)
    tail_recv = _all_to_all([tail_bd], [lax.empty((N_DEV, TAIL_PAD, LANE), F32)], None, "scatter_tail")[0]
    tail_res = _reduce_adam_flat(tail_recv, tail_w, _pack_tail(mom), _pack_tail(vel), "adam_tail")

    small_g = {n: jnp.stack(gsm[n]) for n, _ in SMALL}
    loss_row = jnp.concatenate([loss_part[0:1, 0:1], jnp.zeros((1, LANE - 1), F32)], axis=1)
    zrow = jnp.zeros((1, LANE), F32)
    parts = _all_gather([_pack_small(small_g, loss_row)], "gather_small_grads")[0]
    small_res = _reduce_adam_flat(parts, _pack_small(w, zrow), _pack_small(mom, zrow), _pack_small(vel, zrow),
                                  "adam_small")
    loss = small_res[0][SMALL_ROWS, 0]
    for t, tf, sf in zip(outs, tail_res, small_res):
        t.update(_unpack_tail(tf))
        t.update(_unpack_small(sf))
    order = ["meta_tokens", "pre_norm_g", "w_in", "gate_bias", "pool_w", "pool_scale", "w_out_pool", "q_norm_g",
             "w_uq", "kv_norm_g", "w_ukv", "w_out_mla", "conf_dw_w", "conf_dw_b", "conf_ln_g", "conf_ln_b",
             "w_out_conf", "sc_dw_w", "w_out_sc", "w_o", "post_norm_g"]
    grad_x = dx[N_META:n_real][None]
    return (loss, grad_x, *[t[n] for t in outs for n in order])
```

```python
import functools

import jax
import jax.numpy as jnp
import numpy as np
from jax import lax
from jax.experimental import pallas as pl
from jax.experimental.pallas import tpu as pltpu

F32 = jnp.float32
BF16 = jnp.bfloat16

D = 1024
N_META = 16
DEPTH = 4
EPS = 1e-6
HEADS = 8
QK_NOPE = 64
QK_ROPE = 32
V_DIM = 64
Q_RANK = 256
KV_RANK = 128
ROPE_THETA = 10000.0
SCALE = (QK_NOPE + QK_ROPE) ** -0.5
CONF_K = 31
SC_K = 3
N_DEV = 8

ADAM_LR = 0.001
ADAM_B1 = 0.9
ADAM_B2 = 0.999
ADAM_EPS = 1e-08
ADAM_WD = 0.01
ADAM_STEP = 10

RB = 384
HB = 32
LANE = 128
VMEM_LIMIT = 56 * 1024 * 1024

GL0, CQ0, CKV0, KR0, PV0, PG0, MG0, CU0, CG0, SBB0, SBC0, SBX0, SG0, ZW = (
    0, 4096, 4352, 4480, 4608, 4864, 5120, 5632, 6144, 6400, 6656, 6912, 7168, 7424)
ZSEG = ((0, 4096), (4096, 512), (4608, 2816))
LOG2E = 1.4426950408889634
LN2 = 0.6931471805599453

MESH = pl.DeviceIdType.MESH


def _cp(**kw):
    return pltpu.CompilerParams(vmem_limit_bytes=VMEM_LIMIT, **kw)


def _sig(x):
    return jax.nn.sigmoid(x)


def _silu(x):
    return x * _sig(x)


def _dsilu(x):
    s = _sig(x)
    return s * (1.0 + x * (1.0 - s))


def _dn(x, k):
    return x if k == 0 else pltpu.roll(x, k, 0)


def _up(x, k):
    return x if k == 0 else pltpu.roll(x, x.shape[0] - k, 0)


def _rope(t, c, s1, s2):
    return t * c + pltpu.roll(t, 16, 1) * s1 + pltpu.roll(t, LANE - 16, 1) * s2


def _rope_t(g, c, s1, s2):
    return g * c + pltpu.roll(g * s1, LANE - 16, 1) + pltpu.roll(g * s2, 16, 1)


def _mm(a, b, m, n, k, *, ta=False, tb=False, out_dtype=F32, tm, tn, tk, name,
        a_moff=0, a_koff=0, b_noff=0, b_koff=0, c=None, n_outer=False):
    assert m % tm == 0 and n % tn == 0 and k % tk == 0, (name, m, n, k, tm, tn, tk)
    nk = k // tk
    dims = (((0,) if ta else (1,), (1,) if tb else (0,)), ((), ()))
    has_c = c is not None

    def body(a_ref, b_ref, *rest):
        c_ref = rest[0] if has_c else None
        o_ref = rest[1] if has_c else rest[0]
        scr = rest[2:] if has_c else rest[1:]
        part = lax.dot_general(a_ref[...].astype(BF16), b_ref[...].astype(BF16), dims,
                               preferred_element_type=F32)

        def finish(total):
            if has_c:
                total = total + c_ref[...]
            o_ref[...] = total.astype(out_dtype)

        if nk == 1:
            finish(part)
        else:
            acc = scr[0]
            kk = pl.program_id(2)

            @pl.when(kk == 0)
            def _():
                acc[...] = part

            @pl.when(kk > 0)
            def _():
                acc[...] += part

            @pl.when(kk == nk - 1)
            def _():
                finish(acc[...])

    def im(f):
        return (lambda g0, g1, q: f(g1, g0, q)) if n_outer else f

    if ta:
        a_spec = pl.BlockSpec((tk, tm), im(lambda i, j, q: (q + a_koff, i + a_moff)))
    else:
        a_spec = pl.BlockSpec((tm, tk), im(lambda i, j, q: (i + a_moff, q + a_koff)))
    if tb:
        b_spec = pl.BlockSpec((tn, tk), im(lambda i, j, q: (j + b_noff, q + b_koff)))
    else:
        b_spec = pl.BlockSpec((tk, tn), im(lambda i, j, q: (q + b_koff, j + b_noff)))
    o_spec = pl.BlockSpec((tm, tn), im(lambda i, j, q: (i, j)))
    grid = (n // tn, m // tm, nk) if n_outer else (m // tm, n // tn, nk)
    return pl.pallas_call(
        body, grid=grid, in_specs=[a_spec, b_spec] + ([o_spec] if has_c else []),
        out_specs=o_spec, out_shape=jax.ShapeDtypeStruct((m, n), out_dtype),
        scratch_shapes=[pltpu.VMEM((tm, tn), F32)] if nk > 1 else [],
        name=name, compiler_params=_cp())(*((a, b, c) if has_c else (a, b)))


def _rms_fwd(x, g, lp):
    def body(x_ref, g_ref, h_ref):
        xv = x_ref[...]
        r = lax.rsqrt(jnp.mean(xv * xv, axis=-1, keepdims=True) + EPS)
        h_ref[...] = (xv * r * g_ref[...]).astype(BF16)

    return pl.pallas_call(
        body, grid=(lp // RB,),
        in_specs=[pl.BlockSpec((RB, D), lambda i: (i, 0)), pl.BlockSpec((1, D), lambda i: (0, 0))],
        out_specs=pl.BlockSpec((RB, D), lambda i: (i, 0)),
        out_shape=jax.ShapeDtypeStruct((lp, D), BF16), name="rms_fwd", compiler_params=_cp())(x, g)


def _rms_bwd(x, g, dh, dx_in, lp):
    def body(x_ref, g_ref, dh_ref, dxi_ref, dx_ref, dg_ref):
        i = pl.program_id(0)
        xv = x_ref[...]
        r = lax.rsqrt(jnp.mean(xv * xv, axis=-1, keepdims=True) + EPS)
        dy = dh_ref[...]
        a = dy * g_ref[...]
        dx_ref[...] = dxi_ref[...] + r * a - xv * (r * r * r) * jnp.mean(a * xv, axis=-1, keepdims=True)
        part = jnp.sum(dy * xv * r, axis=0, keepdims=True)

        @pl.when(i == 0)
        def _():
            dg_ref[...] = part

        @pl.when(i > 0)
        def _():
            dg_ref[...] += part

    blk = pl.BlockSpec((RB, D), lambda i: (i, 0))
    vec = pl.BlockSpec((1, D), lambda i: (0, 0))
    return pl.pallas_call(
        body, grid=(lp // RB,), in_specs=[blk, vec, blk, blk], out_specs=[blk, vec],
        out_shape=[jax.ShapeDtypeStruct((lp, D), F32), jax.ShapeDtypeStruct((1, D), F32)],
        name="rms_bwd", compiler_params=_cp())(x, g, dh, dx_in)


def _mla_prep(z, qg, kvg, tabs, lp):
    def body(z_ref, qg_ref, kvg_ref, c_ref, s1_ref, s2_ref, qn_ref, kvn_ref, kr_ref):
        cq = z_ref[:, 0:256]
        ckv = z_ref[:, 256:384]
        kr = z_ref[:, 384:512]
        rq = lax.rsqrt(jnp.mean(cq * cq, axis=-1, keepdims=True) + EPS)
        rk = lax.rsqrt(jnp.mean(ckv * ckv, axis=-1, keepdims=True) + EPS)
        qn_ref[...] = (cq * rq * qg_ref[...]).astype(BF16)
        kvn_ref[...] = (ckv * rk * kvg_ref[...]).astype(BF16)
        kr_ref[...] = _rope(kr, c_ref[...], s1_ref[...], s2_ref[...])

    tab = pl.BlockSpec((RB, LANE), lambda i: (i, 0))
    return pl.pallas_call(
        body, grid=(lp // RB,),
        in_specs=[pl.BlockSpec((RB, 512), lambda i: (i, CQ0 // 512)),
                  pl.BlockSpec((1, 256), lambda i: (0, 0)), pl.BlockSpec((1, 128), lambda i: (0, 0)),
                  tab, tab, tab],
        out_specs=[pl.BlockSpec((RB, 256), lambda i: (i, 0)), tab, tab],
        out_shape=[jax.ShapeDtypeStruct((lp, 256), BF16), jax.ShapeDtypeStruct((lp, 128), BF16),
                   jax.ShapeDtypeStruct((lp, 128), F32)],
        name="mla_prep", compiler_params=_cp())(z, qg, kvg, *tabs)


def _mla_up(z, qg, kvg, w_uq, w_ukv, tabs, lp):
    def body(z_ref, qg_ref, kvg_ref, wuq_ref, wukv_ref, c_ref, s1_ref, s2_ref,
             qn_ref, kvn_ref, qo_ref, ko_ref, vo_ref):
        c, s1, s2 = c_ref[...], s1_ref[...], s2_ref[...]
        cq = z_ref[:, 0:256]
        ckv = z_ref[:, 256:384]
        rq = lax.rsqrt(jnp.mean(cq * cq, axis=-1, keepdims=True) + EPS)
        rk = lax.rsqrt(jnp.mean(ckv * ckv, axis=-1, keepdims=True) + EPS)
        qn = (cq * rq * qg_ref[...]).astype(BF16)
        kvn = (ckv * rk * kvg_ref[...]).astype(BF16)
        qn_ref[...] = qn
        kvn_ref[...] = kvn
        kr = _rope(z_ref[:, 384:512], c, s1, s2)
        q_raw = jnp.dot(qn, wuq_ref[...], preferred_element_type=F32)
        kv_raw = jnp.dot(kvn, wukv_ref[...], preferred_element_type=F32)
        for h in range(HEADS):
            sl = slice(LANE * h, LANE * (h + 1))
            qo_ref[:, sl] = (_rope(q_raw[:, sl], c, s1, s2) * (SCALE * LOG2E)).astype(BF16)
            ko_ref[:, sl] = (kv_raw[:, sl] + kr).astype(BF16)
        vo_ref[...] = kv_raw[:, 1024:1536].astype(BF16)

    tab = pl.BlockSpec((RB, LANE), lambda i: (i, 0))
    wide = pl.BlockSpec((RB, 1024), lambda i: (i, 0))
    full = lambda r, c: pl.BlockSpec((r, c), lambda i: (0, 0))
    return pl.pallas_call(
        body, grid=(lp // RB,),
        in_specs=[pl.BlockSpec((RB, 512), lambda i: (i, CQ0 // 512)), full(1, 256), full(1, 128),
                  full(Q_RANK, 1024), full(KV_RANK, 1536), tab, tab, tab],
        out_specs=[pl.BlockSpec((RB, 256), lambda i: (i, 0)), tab, wide, wide,
                   pl.BlockSpec((RB, 512), lambda i: (i, 0))],
        out_shape=[jax.ShapeDtypeStruct((lp, 256), BF16), jax.ShapeDtypeStruct((lp, 128), BF16),
                   jax.ShapeDtypeStruct((lp, 1024), BF16), jax.ShapeDtypeStruct((lp, 1024), BF16),
                   jax.ShapeDtypeStruct((lp, 512), BF16)],
        name="mla_up", compiler_params=_cp())(z, qg, kvg, w_uq, w_ukv, *tabs)


def _mla_up_bwd(z, qg, kvg, w_uq, w_ukv, dq, dk, dv, tabs, lp):
    def body(z_ref, qg_ref, kvg_ref, wuq_ref, wukv_ref, dq_ref, dk_ref, dv_ref, c_ref, s1_ref, s2_ref,
             dqr_ref, dkv_ref, dz_ref, dqg_ref, dkvg_ref):
        i = pl.program_id(0)
        c, s1, s2 = c_ref[...], s1_ref[...], s2_ref[...]
        lane = lax.broadcasted_iota(jnp.int32, (1, LANE), 1)
        ropel = (lane >= QK_NOPE) & (lane < QK_NOPE + QK_ROPE)
        ksum = jnp.zeros((RB, LANE), F32)
        for h in range(HEADS):
            sl = slice(LANE * h, LANE * (h + 1))
            dqr_ref[:, sl] = _rope_t(dq_ref[:, sl].astype(F32) * SCALE, c, s1, s2).astype(BF16)
            dkt = dk_ref[:, sl]
            dkv_ref[:, sl] = dkt
            ksum = ksum + dkt.astype(F32)
        dkv_ref[:, 1024:1536] = dv_ref[...]
        dkr = jnp.where(ropel, _rope_t(jnp.where(ropel, ksum, 0.0), c, s1, s2), 0.0)
        dqn = lax.dot_general(dqr_ref[...], wuq_ref[...], (((1,), (1,)), ((), ())), preferred_element_type=F32)
        dkvn = lax.dot_general(dkv_ref[...], wukv_ref[...], (((1,), (1,)), ((), ())), preferred_element_type=F32)

        def rms_b(xv, g, dy):
            r = lax.rsqrt(jnp.mean(xv * xv, axis=-1, keepdims=True) + EPS)
            a = dy * g
            dx = r * a - xv * (r * r * r) * jnp.mean(a * xv, axis=-1, keepdims=True)
            return dx, jnp.sum(dy * xv * r, axis=0, keepdims=True)

        dcq, pq = rms_b(z_ref[:, 0:256], qg_ref[...], dqn)
        dckv, pk = rms_b(z_ref[:, 256:384], kvg_ref[...], dkvn)
        dz_ref[:, 0:256] = dcq.astype(BF16)
        dz_ref[:, 256:384] = dckv.astype(BF16)
        dz_ref[:, 384:512] = dkr.astype(BF16)

        @pl.when(i == 0)
        def _():
            dqg_ref[...] = pq
            dkvg_ref[...] = pk

        @pl.when(i > 0)
        def _():
            dqg_ref[...] += pq
            dkvg_ref[...] += pk

    tab = pl.BlockSpec((RB, LANE), lambda i: (i, 0))
    wide = pl.BlockSpec((RB, 1024), lambda i: (i, 0))
    half = pl.BlockSpec((RB, 512), lambda i: (i, 0))
    full = lambda r, c: pl.BlockSpec((r, c), lambda i: (0, 0))
    return pl.pallas_call(
        body, grid=(lp // RB,),
        in_specs=[pl.BlockSpec((RB, 512), lambda i: (i, CQ0 // 512)), full(1, 256), full(1, 128),
                  full(Q_RANK, 1024), full(KV_RANK, 1536), wide, wide, half, tab, tab, tab],
        out_specs=[wide, pl.BlockSpec((RB, 1536), lambda i: (i, 0)), half, full(1, 256), full(1, 128)],
        out_shape=[jax.ShapeDtypeStruct((lp, 1024), BF16), jax.ShapeDtypeStruct((lp, 1536), BF16),
                   jax.ShapeDtypeStruct((lp, 512), BF16), jax.ShapeDtypeStruct((1, 256), F32),
                   jax.ShapeDtypeStruct((1, 128), F32)],
        name="mla_up_bwd", compiler_params=_cp())(z, qg, kvg, w_uq, w_ukv, dq, dk, dv, *tabs)


def _mla_prep_bwd(z, qg, kvg, dqn, dkvn, dkr, lp):
    def body(z_ref, qg_ref, kvg_ref, dqn_ref, dkvn_ref, dkr_ref, dz_ref, dqg_ref, dkvg_ref):
        i = pl.program_id(0)

        def rms_b(xv, g, dy):
            r = lax.rsqrt(jnp.mean(xv * xv, axis=-1, keepdims=True) + EPS)
            a = dy * g
            dx = r * a - xv * (r * r * r) * jnp.mean(a * xv, axis=-1, keepdims=True)
            return dx, jnp.sum(dy * xv * r, axis=0, keepdims=True)

        dcq, pq = rms_b(z_ref[:, 0:256], qg_ref[...], dqn_ref[...])
        dckv, pk = rms_b(z_ref[:, 256:384], kvg_ref[...], dkvn_ref[...])
        dz_ref[:, 0:256] = dcq.astype(BF16)
        dz_ref[:, 256:384] = dckv.astype(BF16)
        dz_ref[:, 384:512] = dkr_ref[...].astype(BF16)

        @pl.when(i == 0)
        def _():
            dqg_ref[...] = pq
            dkvg_ref[...] = pk

        @pl.when(i > 0)
        def _():
            dqg_ref[...] += pq
            dkvg_ref[...] += pk

    tab = pl.BlockSpec((RB, LANE), lambda i: (i, 0))
    return pl.pallas_call(
        body, grid=(lp // RB,),
        in_specs=[pl.BlockSpec((RB, 512), lambda i: (i, CQ0 // 512)),
                  pl.BlockSpec((1, 256), lambda i: (0, 0)), pl.BlockSpec((1, 128), lambda i: (0, 0)),
                  pl.BlockSpec((RB, 256), lambda i: (i, 0)), tab, tab],
        out_specs=[pl.BlockSpec((RB, 512), lambda i: (i, 0)),
                   pl.BlockSpec((1, 256), lambda i: (0, 0)), pl.BlockSpec((1, 128), lambda i: (0, 0))],
        out_shape=[jax.ShapeDtypeStruct((lp, 512), BF16), jax.ShapeDtypeStruct((1, 256), F32),
                   jax.ShapeDtypeStruct((1, 128), F32)],
        name="mla_prep_bwd", compiler_params=_cp())(z, qg, kvg, dqn, dkvn, dkr)


def _mla_post(q_raw, kv_raw, krr, tabs, lp):
    def body(q_ref, kv_ref, kr_ref, c_ref, s1_ref, s2_ref, qo_ref, ko_ref, vo_ref):
        c, s1, s2, kr = c_ref[...], s1_ref[...], s2_ref[...], kr_ref[...]
        for h in range(HEADS):
            sl = slice(LANE * h, LANE * (h + 1))
            qo_ref[:, sl] = (_rope(q_ref[:, sl], c, s1, s2) * (SCALE * LOG2E)).astype(BF16)
            ko_ref[:, sl] = (kv_ref[:, sl] + kr).astype(BF16)
        vo_ref[...] = kv_ref[:, 1024:1536].astype(BF16)

    tab = pl.BlockSpec((RB, LANE), lambda i: (i, 0))
    wide = pl.BlockSpec((RB, 1024), lambda i: (i, 0))
    return pl.pallas_call(
        body, grid=(lp // RB,),
        in_specs=[wide, pl.BlockSpec((RB, 1536), lambda i: (i, 0)), tab, tab, tab, tab],
        out_specs=[wide, wide, pl.BlockSpec((RB, 512), lambda i: (i, 0))],
        out_shape=[jax.ShapeDtypeStruct((lp, 1024), BF16), jax.ShapeDtypeStruct((lp, 1024), BF16),
                   jax.ShapeDtypeStruct((lp, 512), BF16)],
        name="mla_post", compiler_params=_cp())(q_raw, kv_raw, krr, *tabs)


def _mla_post_bwd(dq, dk, dv, tabs, lp):
    def body(dq_ref, dk_ref, dv_ref, c_ref, s1_ref, s2_ref, dqr_ref, dkv_ref, dkr_ref):
        c, s1, s2 = c_ref[...], s1_ref[...], s2_ref[...]
        lane = lax.broadcasted_iota(jnp.int32, (1, LANE), 1)
        ropel = (lane >= QK_NOPE) & (lane < QK_NOPE + QK_ROPE)
        ksum = jnp.zeros((RB, LANE), F32)
        for h in range(HEADS):
            sl = slice(LANE * h, LANE * (h + 1))
            dqr_ref[:, sl] = _rope_t(dq_ref[:, sl].astype(F32) * SCALE, c, s1, s2).astype(BF16)
            dkt = dk_ref[:, sl]
            dkv_ref[:, sl] = dkt
            ksum = ksum + dkt.astype(F32)
        dkv_ref[:, 1024:1536] = dv_ref[...]
        dkr_ref[...] = jnp.where(ropel, _rope_t(jnp.where(ropel, ksum, 0.0), c, s1, s2), 0.0)

    tab = pl.BlockSpec((RB, LANE), lambda i: (i, 0))
    wide = pl.BlockSpec((RB, 1024), lambda i: (i, 0))
    return pl.pallas_call(
        body, grid=(lp // RB,),
        in_specs=[wide, wide, pl.BlockSpec((RB, 512), lambda i: (i, 0)), tab, tab, tab],
        out_specs=[wide, pl.BlockSpec((RB, 1536), lambda i: (i, 0)), tab],
        out_shape=[jax.ShapeDtypeStruct((lp, 1024), BF16), jax.ShapeDtypeStruct((lp, 1536), BF16),
                   jax.ShapeDtypeStruct((lp, 128), F32)],
        name="mla_post_bwd", compiler_params=_cp())(dq, dk, dv, *tabs)


def _head_lanes(e):
    lane = lax.broadcasted_iota(jnp.int32, (1, LANE), 1)
    return lane >= V_DIM if e else lane < V_DIM


ONE_LANE = (V_DIM, 0)


def _attn_fwd(q, k, v, lp, gather=()):
    nq = lp // RB
    n = len(gather)
    steps = HEADS // 2

    def body(q_ref, k_ref, v_ref, *rest):
        o_ref, lse_ref = rest[n], rest[n + 1]
        vm_scr = rest[2 * n + 2]
        if n:
            g_start, g_forward, g_finish = _gather_phases(rest[:n], rest[n + 2:2 * n + 2], *rest[2 * n + 3:])
            pl.when(pl.program_id(0) == 0)(g_start)
            pl.when(pl.program_id(0) == steps - 1)(g_forward)
        lane = lax.broadcasted_iota(jnp.int32, (1, LANE), 1)
        vv = v_ref[...]
        for e in range(2):
            ones = jnp.where(lane == ONE_LANE[e], 1.0, 0.0).astype(BF16)
            vm_scr[e] = jnp.where(_head_lanes(e), vv, jnp.broadcast_to(ones, vv.shape))
        causal = (lax.broadcasted_iota(jnp.int32, (RB, RB), 1) <= lax.broadcasted_iota(jnp.int32, (RB, RB), 0))

        def qblock(i, _):
            rows = pl.ds(pl.multiple_of(i * RB, RB), RB)
            qs = [q_ref[rows, LANE * e:LANE * (e + 1)] for e in range(2)]

            def scores(j):
                cols = pl.ds(pl.multiple_of(j * RB, RB), RB)
                return tuple(lax.dot_general(qs[e], k_ref[cols, LANE * e:LANE * (e + 1)], (((1,), (1,)), ((), ())),
                                             preferred_element_type=F32) for e in range(2))

            def update(j, s, carry, masked):
                cols = pl.ds(pl.multiple_of(j * RB, RB), RB)
                out = []
                for e in range(2):
                    m, acc = carry[2 * e], carry[2 * e + 1]
                    se = jnp.where(causal, s[e], -jnp.inf) if masked else s[e]
                    m_new = jnp.maximum(m, jnp.max(se, axis=-1, keepdims=True))
                    p = jnp.exp2((se - m_new).astype(BF16))
                    acc = jnp.exp2(m - m_new) * acc + jnp.dot(p, vm_scr[e, cols, :], preferred_element_type=F32)
                    out += [m_new, acc]
                return tuple(out)

            def pair(jj, c):
                sa, sb = scores(2 * jj), scores(2 * jj + 1)
                return update(2 * jj + 1, sb, update(2 * jj, sa, c, False), False)

            m0 = jnp.full((RB, 1), -jnp.inf, F32)
            a0 = jnp.zeros((RB, LANE), F32)
            carry = lax.fori_loop(0, i // 2, pair, (m0, a0, m0, a0))

            def last_two(c):
                sa, sb = scores(i - 1), scores(i)
                return update(i, sb, update(i - 1, sa, c, False), True)

            carry = lax.cond(i % 2 == 1, last_two, lambda c: update(i, scores(i), c, True), carry)
            o, lse = [], []
            for e in range(2):
                m, acc = carry[2 * e], carry[2 * e + 1]
                l = acc[:, ONE_LANE[e]:ONE_LANE[e] + 1]
                o.append(acc / l)
                lse.append(jnp.broadcast_to(m + jnp.log2(l), (RB, LANE)))
            o_ref[rows, :] = jnp.where(_head_lanes(0), o[0], o[1])
            lse_ref[rows, :] = jnp.where(_head_lanes(0), lse[0], lse[1])
            return 0

        lax.fori_loop(0, nq, qblock, 0)
        if n:
            pl.when(pl.program_id(0) == steps - 1)(g_finish)

    two = pl.BlockSpec((lp, 2 * LANE), lambda h: (0, h))
    one = pl.BlockSpec((lp, LANE), lambda h: (0, h))
    anyspec = pl.BlockSpec(memory_space=pl.ANY)
    return pl.pallas_call(
        body, grid=(steps,), in_specs=[two, two, one] + [anyspec] * n, out_specs=[one, one] + [anyspec] * n,
        out_shape=[jax.ShapeDtypeStruct((lp, 512), F32), jax.ShapeDtypeStruct((lp, 512), F32)]
        + _gather_shapes(gather),
        scratch_shapes=[pltpu.VMEM((2, lp, LANE), BF16)] + (_comm_sems(n) if n else []),
        name="attn_fwd_gather" if n else "attn_fwd", compiler_params=_cp())(q, k, v, *gather)


def _attn_bwd(q, k, v, o, do, lse, lp, scatter=None):
    nq = lp // RB
    xs, bufs, layer = scatter if scatter else ((), (), None)
    n = len(xs)
    steps = HEADS // 2

    def body(q_ref, k_ref, v_ref, o_ref, do_ref, lse_ref, *rest):
        dq_ref, dk_ref, dv_ref = rest[2 * n:2 * n + 3]
        vm_scr, dom_scr, dl_scr, dq_scr = rest[3 * n + 3:3 * n + 7]
        if n:
            s_start, s_finish = _scatter_phases(rest[:n], rest[2 * n + 3:3 * n + 3], *rest[3 * n + 7:], layer)
            pl.when(pl.program_id(0) == 0)(s_start)
        causal = (lax.broadcasted_iota(jnp.int32, (RB, RB), 1) <= lax.broadcasted_iota(jnp.int32, (RB, RB), 0))
        vv = v_ref[...]
        for e in range(2):
            vm_scr[e] = jnp.where(_head_lanes(e), vv, jnp.zeros_like(vv))

        def prep(i, _):
            rows = pl.ds(pl.multiple_of(i * RB, RB), RB)
            prod = do_ref[rows, :] * o_ref[rows, :]
            dls = []
            for e in range(2):
                hm = _head_lanes(e)
                dom_scr[e, rows, :] = jnp.where(hm, do_ref[rows, :], 0.0).astype(BF16)
                dls.append(jnp.sum(jnp.where(hm, prod, 0.0), axis=-1, keepdims=True))
            dl_scr[rows, :] = jnp.where(_head_lanes(0), dls[0], dls[1])
            dq_scr[rows, :] = jnp.zeros((RB, 2 * LANE), F32)
            return 0

        lax.fori_loop(0, nq, prep, 0)

        def kvblock(j, _):
            cols = pl.ds(pl.multiple_of(j * RB, RB), RB)
            kbs = [k_ref[cols, LANE * e:LANE * (e + 1)] for e in range(2)]

            def products(i):
                rows = pl.ds(pl.multiple_of(i * RB, RB), RB)
                out = []
                for e in range(2):
                    out.append(lax.dot_general(q_ref[rows, LANE * e:LANE * (e + 1)], kbs[e],
                                               (((1,), (1,)), ((), ())), preferred_element_type=F32))
                    out.append(lax.dot_general(dom_scr[e, rows, :], vm_scr[e, cols, :],
                                               (((1,), (1,)), ((), ())), preferred_element_type=F32))
                return tuple(out)

            def update(i, sd, carry, masked):
                rows = pl.ds(pl.multiple_of(i * RB, RB), RB)
                out = []
                for e in range(2):
                    dk, dv = carry[2 * e], carry[2 * e + 1]
                    sl = slice(LANE * e, LANE * (e + 1))
                    col1 = slice(V_DIM * e, V_DIM * e + 1)
                    s, dp = sd[2 * e], sd[2 * e + 1]
                    if masked:
                        s = jnp.where(causal, s, -jnp.inf)
                    p = jnp.exp2((s - lse_ref[rows, col1]).astype(BF16))
                    dv = dv + lax.dot_general(p, dom_scr[e, rows, :], (((0,), (0,)), ((), ())),
                                              preferred_element_type=F32)
                    ds = p * (dp - dl_scr[rows, col1]).astype(BF16)
                    dk = dk + lax.dot_general(ds, q_ref[rows, sl], (((0,), (0,)), ((), ())),
                                              preferred_element_type=F32)
                    dq_scr[rows, sl] += jnp.dot(ds, kbs[e], preferred_element_type=F32)
                    out += [dk, dv]
                return tuple(out)

            def pair(t, c):
                i0 = nq - 2 - 2 * t
                pa, pb = products(i0), products(i0 + 1)
                return update(i0 + 1, pb, update(i0, pa, c, False), False)

            def first_two(c):
                pa, pb = products(j), products(j + 1)
                return update(j + 1, pb, update(j, pa, c, True), False)

            zero = jnp.zeros((RB, LANE), F32)
            below = nq - 1 - j
            carry = lax.fori_loop(0, below // 2, pair, (zero, zero, zero, zero))
            dk0, dv0, dk1, dv1 = lax.cond(below % 2 == 1, first_two,
                                          lambda c: update(j, products(j), c, True), carry)
            dk_ref[cols, 0:LANE] = (dk0 * LN2).astype(BF16)
            dk_ref[cols, LANE:2 * LANE] = (dk1 * LN2).astype(BF16)
            dv_ref[cols, :] = (dv0 + dv1).astype(BF16)
            return 0

        lax.fori_loop(0, nq, kvblock, 0)

        def fin(i, _):
            rows = pl.ds(pl.multiple_of(i * RB, RB), RB)
            dq_ref[rows, :] = dq_scr[rows, :].astype(BF16)
            return 0

        lax.fori_loop(0, nq, fin, 0)
        if n:
            pl.when(pl.program_id(0) == steps - 1)(s_finish)

    two = pl.BlockSpec((lp, 2 * LANE), lambda h: (0, h))
    one = pl.BlockSpec((lp, LANE), lambda h: (0, h))
    anyspec = pl.BlockSpec(memory_space=pl.ANY)
    return pl.pallas_call(
        body, grid=(steps,), in_specs=[two, two, one, one, one, one] + [anyspec] * (2 * n),
        out_specs=[two, two, one] + [anyspec] * n,
        out_shape=[jax.ShapeDtypeStruct((lp, 1024), BF16), jax.ShapeDtypeStruct((lp, 1024), BF16),
                   jax.ShapeDtypeStruct((lp, 512), BF16)] + [jax.ShapeDtypeStruct(b.shape, b.dtype) for b in bufs],
        input_output_aliases={6 + n + a: 3 + a for a in range(n)},
        scratch_shapes=[pltpu.VMEM((2, lp, LANE), BF16), pltpu.VMEM((2, lp, LANE), BF16),
                        pltpu.VMEM((lp, LANE), F32), pltpu.VMEM((lp, 2 * LANE), F32)]
        + (_comm_sems(n) if n else []),
        name="attn_bwd_scatter" if n else "attn_bwd", compiler_params=_cp())(q, k, v, o, do, lse, *xs, *bufs)


def _pool_lane_windows():
    lane = lax.broadcasted_iota(jnp.int32, (1, 256), 1)
    return jnp.where(lane < 64, 2, jnp.where(lane < 128, 4, jnp.where(lane < 192, 8, 16)))


def _by_window(wl, s2, s4, s8, s16):
    return jnp.where(wl == 2, s2, jnp.where(wl == 4, s4, jnp.where(wl == 8, s8, s16)))


def _pool_fwd_rows(pv_ext, t0):
    n = pv_ext.shape[0]
    wl = _pool_lane_windows()
    s2 = pv_ext + _dn(pv_ext, 1)
    s4 = s2 + _dn(s2, 2)
    s8 = s4 + _dn(s4, 4)
    s16 = s8 + _dn(s8, 8)
    t = t0 + lax.broadcasted_iota(jnp.int32, (n, 1), 0)
    cnt = jnp.maximum(jnp.minimum(t + 1, wl), 1).astype(F32)
    return _by_window(wl, s2, s4, s8, s16) / cnt - pv_ext


def _conv_dn(x_ext, w_ref, taps):
    acc = w_ref[taps - 1:taps, :] * x_ext
    for j in range(1, taps):
        acc = acc + w_ref[taps - 1 - j:taps - j, :] * _dn(x_ext, j)
    return acc


def _conv_up(g_ext, w_ref, taps):
    acc = w_ref[taps - 1:taps, :] * g_ext
    for j in range(1, taps):
        acc = acc + w_ref[taps - 1 - j:taps - j, :] * _up(g_ext, j)
    return acc


def _ln_fwd(c, g, b):
    mu = jnp.mean(c, axis=-1, keepdims=True)
    xc = c - mu
    r = lax.rsqrt(jnp.mean(xc * xc, axis=-1, keepdims=True) + EPS)
    xh = xc * r
    return xh * g + b, xh, r


def _halo_specs(lp, width, col):
    per = RB // HB
    last = lp // HB - 1
    cur = pl.BlockSpec((RB, width), lambda i: (i, col))
    prev = pl.BlockSpec((HB, width), lambda i: (jnp.maximum(i * per - 1, 0), col))
    nxt = pl.BlockSpec((HB, width), lambda i: (jnp.minimum((i + 1) * per, last), col))
    return cur, prev, nxt


def _mix_fwd(z, oat, bd, pscale, cw, cb, lng, lnb, sw, lp):
    def body(za, zah, mg, oat_ref, cu, cuh, cg, sbb, sbc, sbch, sbx, sbxh, sg,
             bd_ref, ps_ref, cw_ref, cb_ref, lng_ref, lnb_ref, sw_ref, u_ref):
        i = pl.program_id(0)
        pm = jnp.where(i > 0, 1.0, 0.0).astype(F32)
        pv = jnp.concatenate([zah[:, 0:256] * pm, za[:, 0:256]], axis=0)
        p = _pool_fwd_rows(pv, i * RB - HB)[HB:]
        y = jnp.dot(p.astype(BF16), bd_ref[...], preferred_element_type=F32)
        u_ref[:, 0:256] = (y * ps_ref[...] * _silu(za[:, 256:512])).astype(BF16)
        u_ref[:, 256:768] = (oat_ref[...] * _silu(mg[...])).astype(BF16)
        ce = jnp.concatenate([cuh[...] * pm, cu[...]], axis=0)
        glu = ce[:, 0:256] * _sig(ce[:, 256:512])
        c = _conv_dn(glu, cw_ref, CONF_K)[HB:] + cb_ref[...]
        n, _, _ = _ln_fwd(c, lng_ref[...], lnb_ref[...])
        u_ref[:, 768:1024] = (_silu(n) * _silu(cg[...])).astype(BF16)
        qe = jnp.concatenate([sbch[...] * sbxh[...] * pm, sbc[...] * sbx[...]], axis=0)
        cv = _conv_dn(qe, sw_ref, SC_K)[HB:]
        u_ref[:, 1024:1280] = (sbb[...] * cv * _silu(sg[...])).astype(BF16)

    a_cur, a_prev, _ = _halo_specs(lp, 512, PV0 // 512)
    cu_cur, cu_prev, _ = _halo_specs(lp, 512, CU0 // 512)
    sc_cur, sc_prev, _ = _halo_specs(lp, 256, SBC0 // 256)
    sx_cur, sx_prev, _ = _halo_specs(lp, 256, SBX0 // 256)
    c256 = lambda c0: pl.BlockSpec((RB, 256), lambda i: (i, c0 // 256))
    full = lambda r, c: pl.BlockSpec((r, c), lambda i: (0, 0))
    return pl.pallas_call(
        body, grid=(lp // RB,),
        in_specs=[a_cur, a_prev, pl.BlockSpec((RB, 512), lambda i: (i, MG0 // 512)),
                  pl.BlockSpec((RB, 512), lambda i: (i, 0)),
                  cu_cur, cu_prev, c256(CG0), c256(SBB0), sc_cur, sc_prev, sx_cur, sx_prev, c256(SG0),
                  full(256, 256), full(1, 256), full(32, 256), full(1, 256), full(1, 256), full(1, 256),
                  full(8, 256)],
        out_specs=pl.BlockSpec((RB, 1280), lambda i: (i, 0)),
        out_shape=jax.ShapeDtypeStruct((lp, 1280), BF16),
        name="mix_fwd", compiler_params=_cp())(z, z, z, oat, z, z, z, z, z, z, z, z, z,
                                               bd, pscale, cw, cb, lng, lnb, sw)


def _mix_bwd(z, oat, du, bd, pscale, cw, cb, lng, lnb, sw, lp):
    nb = lp // RB
    ne = RB + 2 * HB
    nf = RB + HB

    def body(za, zah, zan, mg, oat_ref, cu, cuh, cun, cg, cgn, sbb, sbbn, sbc, sbch, sbcn, sbx, sbxh, sbxn,
             sg, sgn, du_ref, dun_ref, bd_ref, ps_ref, cw_ref, cb_ref, lng_ref, lnb_ref, sw_ref,
             dzx_ref, doat_ref, dbd_ref, dcw_ref, dsw_ref, dsm_ref):
        xa, xm, xc = 0, MG0 - PV0, CU0 - PV0
        i = pl.program_id(0)
        pm = jnp.where(i > 0, 1.0, 0.0).astype(F32)
        nm = jnp.where(i < nb - 1, 1.0, 0.0).astype(F32)

        def ext(cur, prev, nxt, sl=slice(None)):
            return jnp.concatenate([prev[:, sl] * pm, cur[:, sl], nxt[:, sl] * nm], axis=0)

        def fwd(cur, nxt, sl=slice(None)):
            return jnp.concatenate([cur[:, sl], nxt[:, sl] * nm], axis=0)

        def csum(x):
            return jnp.sum(x, axis=0, keepdims=True)

        @pl.when(i == 0)
        def _():
            dbd_ref[...] = jnp.zeros((256, 256), F32)
            dcw_ref[...] = jnp.zeros((32, 256), F32)
            dsw_ref[...] = jnp.zeros((8, 256), F32)
            dsm_ref[...] = jnp.zeros((8, 256), F32)

        a_cols, b_cols = slice(0, 256), slice(256, 512)
        pv_e = ext(za, zah, zan, a_cols)
        p = _pool_fwd_rows(pv_e, i * RB - HB)[HB:HB + RB]
        pb = p.astype(BF16)
        y = jnp.dot(pb, bd_ref[...], preferred_element_type=F32)
        pg_f = fwd(za, zan, b_cols)
        dua_f = fwd(du_ref, dun_ref, slice(0, 256))
        dyp_f = dua_f * ps_ref[...] * _silu(pg_f)
        dypb = dyp_f.astype(BF16)
        dp_f = lax.dot_general(dypb, bd_ref[...], (((1,), (1,)), ((), ())), preferred_element_type=F32)
        wl = _pool_lane_windows()
        t = i * RB + lax.broadcasted_iota(jnp.int32, (nf, 1), 0)
        cnt = jnp.minimum(t + 1, wl).astype(F32)
        qf = dp_f / cnt
        f2 = qf + _up(qf, 1)
        f4 = f2 + _up(f2, 2)
        f8 = f4 + _up(f4, 4)
        f16 = f8 + _up(f8, 8)
        dpv = (_by_window(wl, f2, f4, f8, f16) - dp_f)[0:RB]
        dua = du_ref[:, 0:256]
        pg = za[:, b_cols]
        dpg = dua * y * ps_ref[...] * _dsilu(pg)
        dzx_ref[:, xa:xa + 256] = dpv.astype(BF16)
        dzx_ref[:, xa + 256:xa + 512] = dpg.astype(BF16)
        d_scale = csum(dua * y * _silu(pg))
        d_bd = lax.dot_general(pb, dypb[0:RB], (((0,), (0,)), ((), ())), preferred_element_type=F32)

        dub = du_ref[:, 256:768]
        mgv = mg[...]
        dzx_ref[:, xm:xm + 512] = (dub * oat_ref[...] * _dsilu(mgv)).astype(BF16)
        doat_ref[...] = dub * _silu(mgv)

        a_e = ext(cu, cuh, cun, slice(0, 256))
        gt_e = ext(cu, cuh, cun, slice(256, 512))
        sg_e = _sig(gt_e)
        glu_e = a_e * sg_e
        c_f = _conv_dn(glu_e, cw_ref, CONF_K)[HB:] + cb_ref[...]
        n_f, xh_f, r_f = _ln_fwd(c_f, lng_ref[...], lnb_ref[...])
        cg_f = fwd(cg, cgn)
        duc_f = fwd(du_ref, dun_ref, slice(768, 1024))
        sw_f = _silu(n_f)
        dcg = (duc_f * sw_f * _dsilu(cg_f))[0:RB]
        dn_f = duc_f * _silu(cg_f) * _dsilu(n_f)
        a_f = dn_f * lng_ref[...]
        dc_f = r_f * (a_f - jnp.mean(a_f, axis=-1, keepdims=True)
                      - xh_f * jnp.mean(a_f * xh_f, axis=-1, keepdims=True))
        d_lng = csum((dn_f * xh_f)[0:RB])
        d_lnb = csum(dn_f[0:RB])
        d_cb = csum(dc_f[0:RB])
        dglu = _conv_up(dc_f, cw_ref, CONF_K)[0:RB]
        dc_c = dc_f[0:RB]
        for kk in range(CONF_K):
            j = CONF_K - 1 - kk
            dcw_ref[kk:kk + 1, :] += csum(dc_c * _dn(glu_e, j)[HB:HB + RB])

        sgc = sg_e[HB:HB + RB]
        a_c = a_e[HB:HB + RB]
        dzx_ref[:, xc:xc + 256] = (dglu * sgc).astype(BF16)
        dzx_ref[:, xc + 256:xc + 512] = (dglu * a_c * sgc * (1.0 - sgc)).astype(BF16)
        dzx_ref[:, xc + 512:xc + 768] = dcg.astype(BF16)

        c_e = ext(sbc, sbch, sbcn)
        x_e = ext(sbx, sbxh, sbxn)
        q_e = c_e * x_e
        cv_f = _conv_dn(q_e, sw_ref, SC_K)[HB:]
        bg_f = fwd(sbb, sbbn)
        sg_f = fwd(sg, sgn)
        dud_f = fwd(du_ref, dun_ref, slice(1024, 1280))
        ssg_f = _silu(sg_f)
        dcv_f = dud_f * bg_f * ssg_f
        dbg = (dud_f * cv_f * ssg_f)[0:RB]
        dsg = (dud_f * bg_f * cv_f * _dsilu(sg_f))[0:RB]
        dq = _conv_up(dcv_f, sw_ref, SC_K)[0:RB]
        dcv_c = dcv_f[0:RB]
        for kk in range(SC_K):
            j = SC_K - 1 - kk
            dsw_ref[kk:kk + 1, :] += csum(dcv_c * _dn(q_e, j)[HB:HB + RB])

        dzx_ref[:, xc + 768:xc + 1024] = dbg.astype(BF16)
        dzx_ref[:, xc + 1024:xc + 1280] = (dq * x_e[HB:HB + RB]).astype(BF16)
        dzx_ref[:, xc + 1280:xc + 1536] = (dq * c_e[HB:HB + RB]).astype(BF16)
        dzx_ref[:, xc + 1536:xc + 1792] = dsg.astype(BF16)

        dbd_ref[...] += d_bd
        dsm_ref[0:1, :] += d_scale
        dsm_ref[1:2, :] += d_cb
        dsm_ref[2:3, :] += d_lng
        dsm_ref[3:4, :] += d_lnb

    a3 = _halo_specs(lp, 512, PV0 // 512)
    cu3 = _halo_specs(lp, 512, CU0 // 512)
    cg3 = _halo_specs(lp, 256, CG0 // 256)
    sbb3 = _halo_specs(lp, 256, SBB0 // 256)
    sbc3 = _halo_specs(lp, 256, SBC0 // 256)
    sbx3 = _halo_specs(lp, 256, SBX0 // 256)
    sg3 = _halo_specs(lp, 256, SG0 // 256)
    du3 = _halo_specs(lp, 1280, 0)
    full = lambda r, c: pl.BlockSpec((r, c), lambda i: (0, 0))
    in_specs = [a3[0], a3[1], a3[2], pl.BlockSpec((RB, 512), lambda i: (i, MG0 // 512)),
                pl.BlockSpec((RB, 512), lambda i: (i, 0)),
                cu3[0], cu3[1], cu3[2], cg3[0], cg3[2], sbb3[0], sbb3[2],
                sbc3[0], sbc3[1], sbc3[2], sbx3[0], sbx3[1], sbx3[2], sg3[0], sg3[2],
                du3[0], du3[2],
                full(256, 256), full(1, 256), full(32, 256), full(1, 256), full(1, 256), full(1, 256),
                full(8, 256)]
    out_specs = [pl.BlockSpec((RB, ZW - PV0), lambda i: (i, 0)), pl.BlockSpec((RB, 512), lambda i: (i, 0)),
                 full(256, 256), full(32, 256), full(8, 256), full(8, 256)]
    out_shape = [jax.ShapeDtypeStruct((lp, ZW - PV0), BF16), jax.ShapeDtypeStruct((lp, 512), F32),
                 jax.ShapeDtypeStruct((256, 256), F32), jax.ShapeDtypeStruct((32, 256), F32),
                 jax.ShapeDtypeStruct((8, 256), F32), jax.ShapeDtypeStruct((8, 256), F32)]
    return pl.pallas_call(
        body, grid=(nb,), in_specs=in_specs, out_specs=out_specs, out_shape=out_shape,
        name="mix_bwd", compiler_params=_cp())(
            z, z, z, z, oat, z, z, z, z, z, z, z, z, z, z, z, z, z, z, z, du, du,
            bd, pscale, cw, cb, lng, lnb, sw)


U_OFF = (0, 256, 768, 1024, 1280)
MRB = 192


def _merge_fwd(x, u, z, gb, wout, wo, gpost, lp):
    MRB = RB

    def body(x_ref, u_ref, gl_ref, gb_ref, wout_ref, wo_ref, g_ref, xo_ref, m_ref, o2_ref):
        m = jnp.zeros((MRB, D), F32)
        for b in range(4):
            y = jnp.dot(u_ref[:, U_OFF[b]:U_OFF[b + 1]], wout_ref[U_OFF[b]:U_OFF[b + 1], :],
                        preferred_element_type=F32)
            sl = slice(D * b, D * (b + 1))
            m = m + _sig(gl_ref[:, sl] + gb_ref[:, sl]) * y
        mb = m.astype(BF16)
        m_ref[...] = mb
        o2 = jnp.dot(mb, wo_ref[...], preferred_element_type=F32)
        o2_ref[...] = o2
        r = lax.rsqrt(jnp.mean(o2 * o2, axis=-1, keepdims=True) + EPS)
        xo_ref[...] = x_ref[...] + o2 * r * g_ref[...]

    blk = pl.BlockSpec((MRB, D), lambda i: (i, 0))
    full = lambda r, c: pl.BlockSpec((r, c), lambda i: (0, 0))
    return pl.pallas_call(
        body, grid=(lp // MRB,),
        in_specs=[blk, pl.BlockSpec((MRB, 1280), lambda i: (i, 0)), pl.BlockSpec((MRB, 4096), lambda i: (i, 0)),
                  full(1, 4096), full(1280, D), full(D, D), full(1, D)],
        out_specs=[blk, blk, blk],
        out_shape=[jax.ShapeDtypeStruct((lp, D), F32), jax.ShapeDtypeStruct((lp, D), BF16),
                   jax.ShapeDtypeStruct((lp, D), F32)],
        name="merge_fwd", compiler_params=_cp())(x, u, z, gb, wout, wo, gpost)


def _merge_bwd(dx, o2, u, z, gb, wout, wo, gpost, lp):
    MRB = RB

    def body(dx_ref, o2_ref, u_ref, gl_ref, gb_ref, wout_ref, wo_ref, g_ref,
             do2_ref, dgl_ref, dy_ref, du_ref, dgb_ref, dg_ref):
        i = pl.program_id(0)
        o2 = o2_ref[...]
        dy = dx_ref[...]
        r = lax.rsqrt(jnp.mean(o2 * o2, axis=-1, keepdims=True) + EPS)
        a = dy * g_ref[...]
        do2 = (r * a - o2 * (r * r * r) * jnp.mean(a * o2, axis=-1, keepdims=True)).astype(BF16)
        do2_ref[...] = do2
        dg = jnp.sum(dy * o2 * r, axis=0, keepdims=True)
        dm = lax.dot_general(do2, wo_ref[...], (((1,), (1,)), ((), ())), preferred_element_type=F32)
        for b in range(4):
            rows = slice(U_OFF[b], U_OFF[b + 1])
            y = jnp.dot(u_ref[:, rows], wout_ref[rows, :], preferred_element_type=F32)
            sl = slice(D * b, D * (b + 1))
            gt = _sig(gl_ref[:, sl] + gb_ref[:, sl])
            dgl = dm * y * gt * (1.0 - gt)
            dgl_ref[:, sl] = dgl.astype(BF16)
            part = jnp.sum(dgl, axis=0, keepdims=True)

            @pl.when(i == 0)
            def _(part=part, sl=sl):
                dgb_ref[:, sl] = part

            @pl.when(i > 0)
            def _(part=part, sl=sl):
                dgb_ref[:, sl] += part

            dyb = (dm * gt).astype(BF16)
            dy_ref[:, sl] = dyb
            du_ref[:, rows] = lax.dot_general(dyb, wout_ref[rows, :], (((1,), (1,)), ((), ())),
                                              preferred_element_type=F32)

        @pl.when(i == 0)
        def _():
            dg_ref[...] = dg

        @pl.when(i > 0)
        def _():
            dg_ref[...] += dg

    blk = pl.BlockSpec((MRB, D), lambda i: (i, 0))
    wide = pl.BlockSpec((MRB, 4096), lambda i: (i, 0))
    ub = pl.BlockSpec((MRB, 1280), lambda i: (i, 0))
    full = lambda r, c: pl.BlockSpec((r, c), lambda i: (0, 0))
    once = lambda r, c: pl.BlockSpec((r, c), lambda i: (0, 0), pipeline_mode=pl.Buffered(1))
    return pl.pallas_call(
        body, grid=(lp // MRB,),
        in_specs=[blk, blk, ub, wide, full(1, 4096), once(1280, D), once(D, D), full(1, D)],
        out_specs=[blk, wide, wide, ub, full(1, 4096), full(1, D)],
        out_shape=[jax.ShapeDtypeStruct((lp, D), BF16), jax.ShapeDtypeStruct((lp, 4096), BF16),
                   jax.ShapeDtypeStruct((lp, 4096), BF16), jax.ShapeDtypeStruct((lp, 1280), F32),
                   jax.ShapeDtypeStruct((1, 4096), F32), jax.ShapeDtypeStruct((1, D), F32)],
        name="merge_bwd", compiler_params=_cp())(dx, o2, u, z, gb, wout, wo, gpost)


def _dh(dgl, dzq, dzx, w_in, x, g, dx_in, lp, scatter=None):
    xs, bufs, layer = scatter if scatter else ((), (), None)
    n = len(xs)
    steps = lp // RB
    segs = ((0, CQ0), (CQ0, PV0), (PV0, ZW))

    def body(gl_ref, zq_ref, zx_ref, w_ref, x_ref, g_ref, dxi_ref, *rest):
        dx_ref, dg_ref = rest[2 * n], rest[2 * n + 1]
        i = pl.program_id(0)
        if n:
            s_start, s_finish = _scatter_phases(rest[:n], rest[2 * n + 2:3 * n + 2], *rest[3 * n + 2:], layer)
            pl.when(i == 0)(s_start)
        dh = None
        for a_ref, (lo, hi) in zip((gl_ref, zq_ref, zx_ref), segs):
            part = lax.dot_general(a_ref[...], w_ref[:, lo:hi], (((1,), (1,)), ((), ())),
                                   preferred_element_type=F32)
            dh = part if dh is None else dh + part
        xv = x_ref[...]
        r = lax.rsqrt(jnp.mean(xv * xv, axis=-1, keepdims=True) + EPS)
        a = dh * g_ref[...]
        dx_ref[...] = dxi_ref[...] + r * a - xv * (r * r * r) * jnp.mean(a * xv, axis=-1, keepdims=True)
        part = jnp.sum(dh * xv * r, axis=0, keepdims=True)

        @pl.when(i == 0)
        def _():
            dg_ref[...] = part

        @pl.when(i > 0)
        def _():
            dg_ref[...] += part

        if n:
            pl.when(i == steps - 1)(s_finish)

    anyspec = pl.BlockSpec(memory_space=pl.ANY)
    row = lambda w: pl.BlockSpec((RB, w), lambda i: (i, 0))
    vec = pl.BlockSpec((1, D), lambda i: (0, 0))
    resident = pl.BlockSpec((D, ZW), lambda i: (0, 0), pipeline_mode=pl.Buffered(1))
    return pl.pallas_call(
        body, grid=(steps,),
        in_specs=[row(CQ0), row(PV0 - CQ0), row(ZW - PV0), resident, row(D), vec, row(D)] + [anyspec] * (2 * n),
        out_specs=[row(D), vec] + [anyspec] * n,
        out_shape=[jax.ShapeDtypeStruct((lp, D), F32), jax.ShapeDtypeStruct((1, D), F32)]
        + [jax.ShapeDtypeStruct(b.shape, b.dtype) for b in bufs],
        input_output_aliases={7 + n + a: 2 + a for a in range(n)},
        scratch_shapes=_comm_sems(n) if n else [],
        name="dh_scatter" if n else "dh", compiler_params=_cp())(dgl, dzq, dzx, w_in, x, g, dx_in, *xs, *bufs)


def _loss_head(xf, tgt, n_real, lp):
    def body(x_ref, t_ref, dy_ref, ls_ref):
        i = pl.program_id(0)
        t = i * RB + lax.broadcasted_iota(jnp.int32, (RB, 1), 0)
        real = (t >= N_META) & (t < n_real)
        err = jnp.where(real, x_ref[...] - t_ref[...], 0.0)
        dy_ref[...] = err / D
        part = 0.5 * jnp.sum(jnp.mean(err * err, axis=-1, keepdims=True), axis=0, keepdims=True)
        part = jnp.broadcast_to(part, (8, LANE))

        @pl.when(i == 0)
        def _():
            ls_ref[...] = part

        @pl.when(i > 0)
        def _():
            ls_ref[...] += part

    blk = pl.BlockSpec((RB, D), lambda i: (i, 0))
    return pl.pallas_call(
        body, grid=(lp // RB,), in_specs=[blk, blk],
        out_specs=[blk, pl.BlockSpec((8, LANE), lambda i: (0, 0))],
        out_shape=[jax.ShapeDtypeStruct((lp, D), F32), jax.ShapeDtypeStruct((8, LANE), F32)],
        name="loss_head", compiler_params=_cp())(xf, tgt)


def _peer(d):
    x, y, c = lax.axis_index("x"), lax.axis_index("y"), lax.axis_index("c")
    return (x ^ ((d >> 2) & 1), y ^ ((d >> 1) & 1), c ^ (d & 1))


def _index_of(p):
    return 4 * p[0] + 2 * p[1] + p[2]


def _all_gather(xs, name):
    n = len(xs)

    def body(*refs):
        start, forward, finish = _gather_phases(refs[:n], refs[n:2 * n], *refs[2 * n:])
        start()
        forward()
        finish()

    anyspec = pl.BlockSpec(memory_space=pl.ANY)
    return pl.pallas_call(
        body, in_specs=[anyspec] * n, out_specs=[anyspec] * n,
        out_shape=_gather_shapes(xs), scratch_shapes=_comm_sems(n), name=name)(*xs)


def _gather_shapes(xs):
    return [jax.ShapeDtypeStruct((N_DEV,) + x.shape, x.dtype) for x in xs]


def _comm_sems(n):
    return [pltpu.SemaphoreType.DMA((7 * n,)), pltpu.SemaphoreType.DMA((7 * n,)), pltpu.SemaphoreType.DMA((n,))]


def _gather_phases(x_refs, out_refs, send_sems, recv_sems, local_sems):
    n = len(x_refs)
    chips = [2, 4, 6]

    def copy(a, kk, block, to, src=None):
        slot = out_refs[a].at[_index_of(block)]
        return pltpu.make_async_remote_copy(
            src_ref=slot if src is None else src, dst_ref=slot,
            send_sem=send_sems.at[7 * a + kk], recv_sem=recv_sems.at[7 * a + kk], device_id=to,
            device_id_type=MESH)

    def local(a):
        return pltpu.make_async_copy(x_refs[a], out_refs[a].at[_index_of(_peer(0))], local_sems.at[a])

    def firsts():
        out = []
        for a in range(n):
            out.append(copy(a, 0, _peer(0), _peer(1), src=x_refs[a]))
            out += [copy(a, 1 + j, _peer(0), _peer(d), src=x_refs[a]) for j, d in enumerate(chips)]
        return out

    def passes():
        return [copy(a, 4 + j, _peer(d), _peer(1)) for j, d in enumerate(chips) for a in range(n)]

    def start():
        for a in range(n):
            local(a).start()
        for cp in firsts():
            cp.start()

    def forward():
        for j, d in enumerate(chips):
            for a in range(n):
                copy(a, 1 + j, _peer(d), _peer(0)).wait_recv()
                copy(a, 4 + j, _peer(d), _peer(1)).start()

    def finish():
        for a in range(n):
            copy(a, 0, _peer(1), _peer(0)).wait_recv()
            for j, d in enumerate(chips):
                copy(a, 4 + j, _peer(d | 1), _peer(0)).wait_recv()
        for cp in firsts() + passes():
            cp.wait_send()
        for a in range(n):
            local(a).wait()

    return start, forward, finish


def _scatter_phases(x_refs, out_refs, send_sems, recv_sems, local_sems, layer):
    n = len(x_refs)

    def land(a, dev):
        slot = out_refs[a].at[dev]
        return slot if layer is None else slot.at[layer]

    def local(a):
        my = _index_of(_peer(0))
        return pltpu.make_async_copy(x_refs[a].at[my], land(a, my), local_sems.at[a])

    def copy(a, d):
        my = _index_of(_peer(0))
        return pltpu.make_async_remote_copy(
            src_ref=x_refs[a].at[_index_of(_peer(d))], dst_ref=land(a, my),
            send_sem=send_sems.at[7 * a + d - 1], recv_sem=recv_sems.at[7 * a + d - 1], device_id=_peer(d),
            device_id_type=MESH)

    def arrival(a, d):
        frm = _index_of(_peer(d))
        return pltpu.make_async_remote_copy(
            src_ref=x_refs[a].at[frm], dst_ref=land(a, frm),
            send_sem=send_sems.at[7 * a + d - 1], recv_sem=recv_sems.at[7 * a + d - 1], device_id=_peer(d),
            device_id_type=MESH)

    def start():
        for a in range(n):
            local(a).start()
        for d in range(1, N_DEV):
            for a in range(n):
                copy(a, d).start()

    def finish():
        for d in range(1, N_DEV):
            for a in range(n):
                arrival(a, d).wait_recv()
        for d in range(1, N_DEV):
            for a in range(n):
                copy(a, d).wait_send()
        for a in range(n):
            local(a).wait()

    return start, finish


def _all_to_all(xs, bufs, layer, name):
    n = len(xs)

    def body(*refs):
        start, finish = _scatter_phases(refs[:n], refs[2 * n:3 * n], *refs[3 * n:], layer)
        start()
        finish()

    anyspec = pl.BlockSpec(memory_space=pl.ANY)
    return pl.pallas_call(
        body, in_specs=[anyspec] * (2 * n), out_specs=[anyspec] * n,
        out_shape=[jax.ShapeDtypeStruct(b.shape, b.dtype) for b in bufs],
        input_output_aliases={n + a: a for a in range(n)},
        scratch_shapes=_comm_sems(n), name=name)(*xs, *bufs)


def _adam_math(g, w, m, v):
    c1 = 1.0 - ADAM_B1 ** ADAM_STEP
    c2 = 1.0 - ADAM_B2 ** ADAM_STEP
    mn = ADAM_B1 * m + (1.0 - ADAM_B1) * g
    vn = ADAM_B2 * v + (1.0 - ADAM_B2) * (g * g)
    return -ADAM_LR * ((mn / c1) / (jnp.sqrt(vn / c2) + ADAM_EPS) + ADAM_WD * w), mn, vn


def _reduce_adam(parts, w, m, v, rb, name, row_off=0):
    depth, rows, cols = w.shape
    assert rows % rb == 0 and row_off % rb == 0

    def body(p_ref, w_ref, m_ref, v_ref, g_ref, d_ref, mo_ref, vo_ref):
        g = p_ref[0, 0].astype(F32)
        for j in range(1, N_DEV):
            g = g + p_ref[j, 0].astype(F32)
        g_ref[0] = g
        d_ref[0], mo_ref[0], vo_ref[0] = _adam_math(g, w_ref[0], m_ref[0], v_ref[0])

    blk = pl.BlockSpec((1, rb, cols), lambda l, i: (l, i, 0))
    out = jax.ShapeDtypeStruct(w.shape, F32)
    return pl.pallas_call(
        body, grid=(depth, rows // rb),
        in_specs=[pl.BlockSpec((N_DEV, 1, rb, cols), lambda l, i: (0, l, i + row_off // rb, 0)), blk, blk, blk],
        out_specs=[blk, blk, blk, blk], out_shape=[out, out, out, out],
        name=name, compiler_params=_cp())(parts, w, m, v)


def _reduce_adam_flat(parts, w, m, v, name):
    q_rows = w.shape[0]

    def body(p_ref, w_ref, m_ref, v_ref, g_ref, d_ref, mo_ref, vo_ref):
        g = p_ref[0].astype(F32)
        for j in range(1, N_DEV):
            g = g + p_ref[j].astype(F32)
        g_ref[...] = g
        d_ref[...], mo_ref[...], vo_ref[...] = _adam_math(g, w_ref[...], m_ref[...], v_ref[...])

    blk = pl.BlockSpec((q_rows, LANE), lambda i: (0, 0))
    out = jax.ShapeDtypeStruct((q_rows, LANE), F32)
    return pl.pallas_call(
        body, grid=(1,), in_specs=[pl.BlockSpec((N_DEV, q_rows, LANE), lambda i: (0, 0, 0)), blk, blk, blk],
        out_specs=[blk, blk, blk, blk], out_shape=[out, out, out, out],
        name=name, compiler_params=_cp())(parts, w, m, v)


C128 = (("w_out_pool", 256), ("w_out_mla", 512), ("w_out_conf", 256), ("w_out_sc", 256), ("w_ukv", 128))
C128_ROWS = sum(r for _, r in C128)
TAIL = (("meta_tokens", (N_META, 128)), ("conf_dw_w", (DEPTH, CONF_K, 32)), ("sc_dw_w", (DEPTH, SC_K, 32)))
TAIL_ROWS = sum(int(np.prod(s)) for _, s in TAIL) // LANE
TAIL_PAD = 56
SMALL = (("pre_norm_g", (DEPTH, D)), ("gate_bias", (DEPTH, 4096)), ("pool_w", (DEPTH, 4, 64, 64)),
         ("pool_scale", (DEPTH, 256)), ("q_norm_g", (DEPTH, 256)), ("kv_norm_g", (DEPTH, 128)),
         ("conf_dw_b", (DEPTH, 256)), ("conf_ln_g", (DEPTH, 256)), ("conf_ln_b", (DEPTH, 256)),
         ("post_norm_g", (DEPTH, D)))
SMALL_ROWS = sum(int(np.prod(s)) for _, s in SMALL) // LANE
SMALL_PAD = -(-(SMALL_ROWS + 1) // 8) * 8


def _pack_tail(t):
    parts = [t[n].reshape(-1, LANE) for n, _ in TAIL]
    parts.append(jnp.zeros((TAIL_PAD - TAIL_ROWS, LANE), F32))
    return jnp.concatenate(parts, axis=0)


def _unpack_tail(flat):
    out, off = {}, 0
    for n, s in TAIL:
        rows = int(np.prod(s)) // LANE
        out[n] = flat[off:off + rows].reshape(s)
        off += rows
    return out


def _unpack_tail_full(g):
    out, off = {}, 0
    for n, s in TAIL:
        rows = int(np.prod(s)) // LANE
        blk = jnp.moveaxis(g[:, off:off + rows].reshape((N_DEV,) + s), 0, -2)
        out[n] = blk.reshape(s[:-1] + (N_DEV * s[-1],))
        off += rows
    return out


def _pack_small(t, extra_row):
    parts = [t[n].reshape(-1, LANE) for n, _ in SMALL] + [extra_row]
    parts.append(jnp.zeros((SMALL_PAD - SMALL_ROWS - 1, LANE), F32))
    return jnp.concatenate(parts, axis=0)


def _unpack_small(flat):
    out, off = {}, 0
    for n, s in SMALL:
        rows = int(np.prod(s)) // LANE
        out[n] = flat[off:off + rows].reshape(s)
        off += rows
    return out


def _cols_by_dest(g, width):
    r = g.shape[0]
    return g.reshape(r, N_DEV, width).transpose(1, 0, 2)


def _cols_full(gathered):
    _, r, c = gathered.shape
    return gathered.transpose(1, 0, 2).reshape(r, N_DEV * c)


W_IN_SHARD = 916
PACKED_SEGS = ((3232, 7328), (512, 896), 64, (896, 928), 32, (0, 512), (928, 3232))


def _pack_w_in(g):
    parts = []
    for seg in PACKED_SEGS:
        if isinstance(seg, int):
            parts.append(jnp.zeros((g.shape[1], seg), g.dtype))
            continue
        a, b = seg
        while a < b:
            k = a // W_IN_SHARD
            hi = min(b, W_IN_SHARD * (k + 1))
            parts.append(g[k, :, a - W_IN_SHARD * k:hi - W_IN_SHARD * k])
            a = hi
    return jnp.concatenate(parts, axis=1)


def _w_in_grad_by_dest(gl, mla, mix):
    src = (((0, 512), mix, 0), ((512, 896), mla, 0), ((896, 928), mla, 448), ((928, 3232), mix, 512),
           ((3232, 7328), gl, 0))
    blocks = []
    for k in range(N_DEV):
        lo, hi = W_IN_SHARD * k, W_IN_SHARD * (k + 1)
        parts = []
        for (a, b), arr, off in src:
            s, e = max(a, lo), min(b, hi)
            if s < e:
                parts.append(arr[:, off + s - a:off + e - a])
        blocks.append(jnp.concatenate(parts, axis=1))
    return jnp.stack(blocks)


def _rope_tables(lp):
    inv = 1.0 / (ROPE_THETA ** (jnp.arange(0, QK_ROPE, 2, dtype=F32) / QK_ROPE))
    ang = jnp.arange(lp, dtype=F32)[:, None] * inv[None, :]
    cos, sin = jnp.cos(ang), jnp.sin(ang)
    one = jnp.ones((lp, QK_NOPE), F32)
    zero = jnp.zeros((lp, QK_NOPE), F32)
    z16 = jnp.zeros((lp, 16), F32)
    c = jnp.concatenate([one, cos, cos, jnp.ones((lp, 32), F32)], axis=1)
    s1 = jnp.concatenate([zero, z16, sin, jnp.zeros((lp, 32), F32)], axis=1)
    s2 = jnp.concatenate([zero, -sin, z16, jnp.zeros((lp, 32), F32)], axis=1)
    return c, s1, s2


def kernel(x, meta_tokens, pre_norm_g, w_in, gate_bias, pool_w, pool_scale, w_out_pool, q_norm_g, w_uq, kv_norm_g, w_ukv, w_out_mla, conf_dw_w, conf_dw_b, conf_ln_g, conf_ln_b, w_out_conf, sc_dw_w, w_out_sc, w_o, post_norm_g, loss_target, m_meta_tokens, m_pre_norm_g, m_w_in, m_gate_bias, m_pool_w, m_pool_scale, m_w_out_pool, m_q_norm_g, m_w_uq, m_kv_norm_g, m_w_ukv, m_w_out_mla, m_conf_dw_w, m_conf_dw_b, m_conf_ln_g, m_conf_ln_b, m_w_out_conf, m_sc_dw_w, m_w_out_sc, m_w_o, m_post_norm_g, v_meta_tokens, v_pre_norm_g, v_w_in, v_gate_bias, v_pool_w, v_pool_scale, v_w_out_pool, v_q_norm_g, v_w_uq, v_kv_norm_g, v_w_ukv, v_w_out_mla, v_conf_dw_w, v_conf_dw_b, v_conf_ln_g, v_conf_ln_b, v_w_out_conf, v_sc_dw_w, v_w_out_sc, v_w_o, v_post_norm_g):
    names = ["w_in", "w_uq", "w_o"] + [n for n, _ in C128] + [n for n, _ in TAIL] + [n for n, _ in SMALL]
    loc = locals()
    w = {n: loc[n] for n in names}
    mom = {n: loc["m_" + n] for n in names}
    vel = {n: loc["v_" + n] for n in names}

    seq = x.shape[1]
    n_real = N_META + seq
    lp = -(-n_real // RB) * RB
    tmb = lp // 3
    tabs = _rope_tables(lp)

    c128 = jnp.concatenate([w[n] for n, _ in C128], axis=1)
    def shards_of(i):
        return [w_in[i].astype(BF16), c128[i].astype(BF16), w_uq[i].astype(BF16), w_o[i].astype(BF16)]

    gathered = [_all_gather(shards_of(0), "gather_weights")] + [None] * (DEPTH - 1)
    tail_w = _pack_tail(w)
    tail = _unpack_tail_full(_all_gather([tail_w], "gather_tail")[0])
    eye4 = jnp.eye(4, dtype=F32)
    bd_all = (pool_w[:, :, :, None, :] * eye4[None, :, None, :, None]).reshape(DEPTH, 256, 256).astype(BF16)

    def layer_weights(i):
        g_in, g_c128, g_uq, g_o = gathered[i]
        lw = {}
        lw["w_in"] = _pack_w_in(g_in)
        lw["wc"] = _cols_full(g_c128)
        wuq = _cols_full(g_uq).reshape(Q_RANK, HEADS, 96)
        lw["w_uq"] = jnp.pad(wuq, ((0, 0), (0, 0), (0, 32))).reshape(Q_RANK, HEADS * LANE)
        wukv = lw["wc"][U_OFF[4]:].reshape(KV_RANK, HEADS, 128)
        wk = jnp.pad(wukv[:, :, :QK_NOPE], ((0, 0), (0, 0), (0, 64))).reshape(KV_RANK, HEADS * LANE)
        lw["w_ukv"] = jnp.concatenate([wk, wukv[:, :, QK_NOPE:].reshape(KV_RANK, HEADS * V_DIM)], axis=1)
        lw["w_o"] = g_o.reshape(D, D)
        lw["bd"] = bd_all[i]
        lw["cw"] = jnp.pad(tail["conf_dw_w"][i], ((0, 1), (0, 0)))
        lw["sw"] = jnp.pad(tail["sc_dw_w"][i], ((0, 8 - SC_K), (0, 0)))
        return lw

    meta_full = tail["meta_tokens"]

    pad_rows = lp - n_real
    xr = jnp.concatenate([meta_full, x[0], jnp.zeros((pad_rows, D), F32)], axis=0)
    tgt = jnp.pad(loss_target[0], ((N_META, pad_rows), (0, 0)))
    saved = []
    for i in range(DEPTH):
        lw = layer_weights(i)
        h = _rms_fwd(xr, pre_norm_g[i:i + 1], lp)
        z = _mm(h, lw["w_in"], lp, ZW, D, tm=RB, tn=ZW // 2, tk=D, n_outer=True, name="mm_in")
        qn, kvn, qt, kt, vt = _mla_up(z, q_norm_g[i:i + 1], kv_norm_g[i:i + 1], lw["w_uq"], lw["w_ukv"], tabs, lp)
        res = _attn_fwd(qt, kt, vt, lp, gather=shards_of(i + 1) if i + 1 < DEPTH else ())
        oat, lse = res[0], res[1]
        if i + 1 < DEPTH:
            gathered[i + 1] = res[2:]
        u = _mix_fwd(z, oat, lw["bd"], pool_scale[i:i + 1], lw["cw"], conf_dw_b[i:i + 1], conf_ln_g[i:i + 1],
                     conf_ln_b[i:i + 1], lw["sw"], lp)
        x_new, m_act, o2 = _merge_fwd(xr, u, z, gate_bias[i:i + 1], lw["wc"], lw["w_o"],
                                      post_norm_g[i:i + 1], lp)
        saved.append(dict(lw=lw, x=xr, h=h, z=z, qn=qn, kvn=kvn, qt=qt, kt=kt, vt=vt, oat=oat, lse=lse,
                          u=u, m=m_act, o2=o2))
        xr = x_new

    dx, loss_part = _loss_head(xr, tgt, n_real, lp)

    gsm = {n: [None] * DEPTH for n, _ in SMALL}
    g_cw = [None] * DEPTH
    g_sw = [None] * DEPTH
    recv = [lax.empty((N_DEV, DEPTH) + s, BF16) for s in ((D, 916), (C128_ROWS, 128), (Q_RANK, 96), (128, D))]
    pending = None
    for i in reversed(range(DEPTH)):
        s = saved[i]
        lw = s["lw"]
        do2, dgl, dyb, du, dgb, dgpost = _merge_bwd(dx, s["o2"], s["u"], s["z"], gate_bias[i:i + 1],
                                                    lw["wc"], lw["w_o"], post_norm_g[i:i + 1], lp)
        d_wo = _mm(s["m"], do2, D, D, lp, ta=True, tm=512, tn=D, tk=tmb, out_dtype=BF16, name="mm_dwo")
        d_wout = []
        for b in range(4):
            rows = U_OFF[b + 1] - U_OFF[b]
            d_wout.append(_mm(s["u"], dyb, rows, D, lp, ta=True, tm=256, tn=D, tk=tmb, out_dtype=BF16,
                              a_moff=U_OFF[b] // 256, b_noff=b, name="mm_dwout%d" % b))
        dzx, doat, dbd, dcw, dsw, dsm = _mix_bwd(
            s["z"], s["oat"], du, lw["bd"], pool_scale[i:i + 1], lw["cw"], conf_dw_b[i:i + 1],
            conf_ln_g[i:i + 1], conf_ln_b[i:i + 1], lw["sw"], lp)
        res = _attn_bwd(s["qt"], s["kt"], s["vt"], s["oat"], doat, s["lse"], lp,
                        scatter=(pending, recv, i + 1) if pending else None)
        dqt, dkt, dvt = res[:3]
        if pending:
            recv = list(res[3:])
        dq_raw, dkv_raw, dzq, dqg, dkvg = _mla_up_bwd(s["z"], q_norm_g[i:i + 1], kv_norm_g[i:i + 1], lw["w_uq"],
                                                      lw["w_ukv"], dqt, dkt, dvt, tabs, lp)
        d_wuq = _mm(s["qn"], dq_raw, Q_RANK, 1024, lp, ta=True, tm=Q_RANK, tn=1024, tk=tmb, out_dtype=BF16,
                    name="mm_dwuq")
        d_wukv = _mm(s["kvn"], dkv_raw, KV_RANK, 1536, lp, ta=True, tm=KV_RANK, tn=1536, tk=tmb, out_dtype=BF16,
                     name="mm_dwukv")
        d_win = [_mm(s["h"], seg, D, seg.shape[1], lp, ta=True, tm=D, tn=tn, tk=tmb, out_dtype=BF16,
                     name="mm_dwin%d" % k) for k, (seg, tn) in enumerate(((dgl, 1024), (dzq, 512), (dzx, 1408)))]
        d_wuq_o = d_wuq.reshape(Q_RANK, HEADS, LANE)[:, :, :96].reshape(Q_RANK, HEADS * 96)
        d_wukv_o = jnp.concatenate([d_wukv[:, :1024].reshape(KV_RANK, HEADS, LANE)[:, :, :QK_NOPE],
                                    d_wukv[:, 1024:].reshape(KV_RANK, HEADS, V_DIM)], axis=2).reshape(KV_RANK, 1024)
        pending = [
            _w_in_grad_by_dest(*d_win),
            _cols_by_dest(jnp.concatenate(d_wout + [d_wukv_o], axis=0), 128),
            _cols_by_dest(d_wuq_o, 96),
            d_wo.reshape(N_DEV, 128, D)]
        res = _dh(dgl, dzq, dzx, lw["w_in"], s["x"], pre_norm_g[i:i + 1], dx, lp,
                  scatter=(pending, recv, 0) if i == 0 else None)
        dx, dgpre = res[0], res[1]
        if i == 0:
            recv = list(res[2:])

        gsm["pre_norm_g"][i] = dgpre[0]
        gsm["gate_bias"][i] = dgb[0]
        gsm["pool_w"][i] = jnp.stack([dbd[64 * g:64 * (g + 1), 64 * g:64 * (g + 1)] for g in range(4)])
        gsm["pool_scale"][i] = dsm[0]
        gsm["conf_dw_b"][i] = dsm[1]
        gsm["conf_ln_g"][i] = dsm[2]
        gsm["conf_ln_b"][i] = dsm[3]
        gsm["q_norm_g"][i] = dqg[0]
        gsm["kv_norm_g"][i] = dkvg[0]
        gsm["post_norm_g"][i] = dgpost[0]
        g_cw[i] = dcw[:CONF_K]
        g_sw[i] = dsw[:SC_K]

    outs = [dict() for _ in range(4)]

    def put(n, res):
        for t, r in zip(outs, res):
            t[n] = r

    put("w_in", _reduce_adam(recv[0], w["w_in"], mom["w_in"], vel["w_in"], 256, "adam_w_in"))
    off = 0
    for n, rows in C128:
        put(n, _reduce_adam(recv[1], w[n], mom[n], vel[n], 128, "adam_" + n, row_off=off))
        off += rows
    put("w_uq", _reduce_adam(recv[2], w["w_uq"], mom["w_uq"], vel["w_uq"], Q_RANK, "adam_w_uq"))
    put("w_o", _reduce_adam(recv[3], w["w_o"], mom["w_o"], vel["w_o"], 128, "adam_w_o"))

    tail_g = {"meta_tokens": _cols_by_dest(dx[:N_META], 128),
              "conf_dw_w": jnp.moveaxis(jnp.stack(g_cw).reshape(DEPTH, CONF_K, N_DEV, 32), 2, 0),
              "sc_dw_w": jnp.moveaxis(jnp.stack(g_sw).reshape(DEPTH, SC_K, N_DEV, 32), 2, 0)}
    tail_bd = jnp.concatenate([tail_g[n].reshape(N_DEV, -1, LANE) for n, _ in TAIL]
                              + [jnp.zeros((N_DEV, TAIL_PAD - TAIL_ROWS, LANE), F32)], axis=1)
    tail_recv = _all_to_all([tail_bd], [lax.empty((N_DEV, TAIL_PAD, LANE), F32)], None, "scatter_tail")[0]
    tail_res = _reduce_adam_flat(tail_recv, tail_w, _pack_tail(mom), _pack_tail(vel), "adam_tail")

    small_g = {n: jnp.stack(gsm[n]) for n, _ in SMALL}
    loss_row = jnp.concatenate([loss_part[0:1, 0:1], jnp.zeros((1, LANE - 1), F32)], axis=1)
    zrow = jnp.zeros((1, LANE), F32)
    parts = _all_gather([_pack_small(small_g, loss_row)], "gather_small_grads")[0]
    small_res = _reduce_adam_flat(parts, _pack_small(w, zrow), _pack_small(mom, zrow), _pack_small(vel, zrow),
                                  "adam_small")
    loss = small_res[0][SMALL_ROWS, 0]
    for t, tf, sf in zip(outs, tail_res, small_res):
        t.update(_unpack_tail(tf))
        t.update(_unpack_small(sf))
    order = ["meta_tokens", "pre_norm_g", "w_in", "gate_bias", "pool_w", "pool_scale", "w_out_pool", "q_norm_g",
             "w_uq", "kv_norm_g", "w_ukv", "w_out_mla", "conf_dw_w", "conf_dw_b", "conf_ln_g", "conf_ln_b",
             "w_out_conf", "sc_dw_w", "w_out_sc", "w_o", "post_norm_g"]
    grad_x = dx[N_META:n_real][None]
    return (loss, grad_x, *[t[n] for t in outs for n in order])
```

```python
import jax
import jax.numpy as jnp
import numpy as np
from jax import lax
from jax.experimental import pallas as pl
from jax.experimental.pallas import tpu as pltpu

F32 = jnp.float32
BF16 = jnp.bfloat16

D = 1024
N_META = 16
DEPTH = 4
EPS = 1e-6
HEADS = 8
QK_NOPE = 64
QK_ROPE = 32
V_DIM = 64
Q_RANK = 256
KV_RANK = 128
ROPE_THETA = 10000.0
SCALE = (QK_NOPE + QK_ROPE) ** -0.5
CONF_K = 31
SC_K = 3
N_DEV = 8

ADAM_LR = 0.001
ADAM_B1 = 0.9
ADAM_B2 = 0.999
ADAM_EPS = 1e-08
ADAM_WD = 0.01
ADAM_STEP = 10

RB = 384
HB = 32
LANE = 128
VMEM_LIMIT = 56 * 1024 * 1024

GL0, CQ0, CKV0, KR0, PV0, PG0, MG0, CU0, CG0, SBB0, SBC0, SBX0, SG0, ZW = (
    0, 4096, 4352, 4480, 4608, 4864, 5120, 5632, 6144, 6400, 6656, 6912, 7168, 7424)
ZSEG = ((0, 4096), (4096, 512), (4608, 2816))
LOG2E = 1.4426950408889634
LN2 = 0.6931471805599453

MESH = pl.DeviceIdType.MESH


def _cp(**kw):
    return pltpu.CompilerParams(vmem_limit_bytes=VMEM_LIMIT, **kw)


def _sig(x):
    return jax.nn.sigmoid(x)


def _silu(x):
    return x * _sig(x)


def _dsilu(x):
    s = _sig(x)
    return s * (1.0 + x * (1.0 - s))


def _dn(x, k):
    return x if k == 0 else pltpu.roll(x, k, 0)


def _up(x, k):
    return x if k == 0 else pltpu.roll(x, x.shape[0] - k, 0)


def _rope(t, c, s1, s2):
    return t * c + pltpu.roll(t, 16, 1) * s1 + pltpu.roll(t, LANE - 16, 1) * s2


def _rope_t(g, c, s1, s2):
    return g * c + pltpu.roll(g * s1, LANE - 16, 1) + pltpu.roll(g * s2, 16, 1)


def _mm(a, b, m, n, k, *, ta=False, tb=False, out_dtype=F32, tm, tn, tk, name,
        a_moff=0, a_koff=0, b_noff=0, b_koff=0, c=None, n_outer=False):
    assert m % tm == 0 and n % tn == 0 and k % tk == 0, (name, m, n, k, tm, tn, tk)
    nk = k // tk
    dims = (((0,) if ta else (1,), (1,) if tb else (0,)), ((), ()))
    has_c = c is not None

    def body(a_ref, b_ref, *rest):
        c_ref = rest[0] if has_c else None
        o_ref = rest[1] if has_c else rest[0]
        scr = rest[2:] if has_c else rest[1:]
        part = lax.dot_general(a_ref[...].astype(BF16), b_ref[...].astype(BF16), dims,
                               preferred_element_type=F32)

        def finish(total):
            if has_c:
                total = total + c_ref[...]
            o_ref[...] = total.astype(out_dtype)

        if nk == 1:
            finish(part)
        else:
            acc = scr[0]
            kk = pl.program_id(2)

            @pl.when(kk == 0)
            def _():
                acc[...] = part

            @pl.when(kk > 0)
            def _():
                acc[...] += part

            @pl.when(kk == nk - 1)
            def _():
                finish(acc[...])

    def im(f):
        return (lambda g0, g1, q: f(g1, g0, q)) if n_outer else f

    if ta:
        a_spec = pl.BlockSpec((tk, tm), im(lambda i, j, q: (q + a_koff, i + a_moff)))
    else:
        a_spec = pl.BlockSpec((tm, tk), im(lambda i, j, q: (i + a_moff, q + a_koff)))
    if tb:
        b_spec = pl.BlockSpec((tn, tk), im(lambda i, j, q: (j + b_noff, q + b_koff)))
    else:
        b_spec = pl.BlockSpec((tk, tn), im(lambda i, j, q: (q + b_koff, j + b_noff)))
    o_spec = pl.BlockSpec((tm, tn), im(lambda i, j, q: (i, j)))
    grid = (n // tn, m // tm, nk) if n_outer else (m // tm, n // tn, nk)
    return pl.pallas_call(
        body, grid=grid, in_specs=[a_spec, b_spec] + ([o_spec] if has_c else []),
        out_specs=o_spec, out_shape=jax.ShapeDtypeStruct((m, n), out_dtype),
        scratch_shapes=[pltpu.VMEM((tm, tn), F32)] if nk > 1 else [],
        name=name, compiler_params=_cp())(*((a, b, c) if has_c else (a, b)))


def _rms_fwd(x, g, lp):
    def body(x_ref, g_ref, h_ref):
        xv = x_ref[...]
        r = lax.rsqrt(jnp.mean(xv * xv, axis=-1, keepdims=True) + EPS)
        h_ref[...] = (xv * r * g_ref[...]).astype(BF16)

    return pl.pallas_call(
        body, grid=(lp // RB,),
        in_specs=[pl.BlockSpec((RB, D), lambda i: (i, 0)), pl.BlockSpec((1, D), lambda i: (0, 0))],
        out_specs=pl.BlockSpec((RB, D), lambda i: (i, 0)),
        out_shape=jax.ShapeDtypeStruct((lp, D), BF16), name="rms_fwd", compiler_params=_cp())(x, g)


def _dwout(u, dyb, lp):
    tk = lp // 3

    def body(u_ref, dy_ref, o_ref, acc):
        kk = pl.program_id(0)
        for b in range(4):
            rows = slice(U_OFF[b], U_OFF[b + 1])
            part = lax.dot_general(u_ref[:, rows], dy_ref[:, D * b:D * (b + 1)], (((0,), (0,)), ((), ())),
                                   preferred_element_type=F32)

            @pl.when(kk == 0)
            def _(part=part, rows=rows):
                acc[rows, :] = part

            @pl.when(kk > 0)
            def _(part=part, rows=rows):
                acc[rows, :] += part

        @pl.when(kk == 2)
        def _():
            o_ref[...] = acc[...].astype(BF16)

    return pl.pallas_call(
        body, grid=(3,),
        in_specs=[pl.BlockSpec((tk, U_OFF[4]), lambda q: (q, 0)), pl.BlockSpec((tk, 4 * D), lambda q: (q, 0))],
        out_specs=pl.BlockSpec((U_OFF[4], D), lambda q: (0, 0)),
        out_shape=jax.ShapeDtypeStruct((U_OFF[4], D), BF16),
        scratch_shapes=[pltpu.VMEM((U_OFF[4], D), F32)],
        name="dwout", compiler_params=_cp())(u, dyb)


def _mla_up(z, qg, kvg, w_uq, w_ukv, tabs, lp):
    def body(z_ref, qg_ref, kvg_ref, wuq_ref, wukv_ref, c_ref, s1_ref, s2_ref,
             qn_ref, kvn_ref, qo_ref, ko_ref, vo_ref):
        c, s1, s2 = c_ref[...], s1_ref[...], s2_ref[...]
        cq = z_ref[:, 0:256]
        ckv = z_ref[:, 256:384]
        rq = lax.rsqrt(jnp.mean(cq * cq, axis=-1, keepdims=True) + EPS)
        rk = lax.rsqrt(jnp.mean(ckv * ckv, axis=-1, keepdims=True) + EPS)
        qn = (cq * rq * qg_ref[...]).astype(BF16)
        kvn = (ckv * rk * kvg_ref[...]).astype(BF16)
        qn_ref[...] = qn
        kvn_ref[...] = kvn
        kr = _rope(z_ref[:, 384:512], c, s1, s2)
        q_raw = jnp.dot(qn, wuq_ref[...], preferred_element_type=F32)
        kv_raw = jnp.dot(kvn, wukv_ref[...], preferred_element_type=F32)
        for h in range(HEADS):
            sl = slice(LANE * h, LANE * (h + 1))
            qo_ref[:, sl] = (_rope(q_raw[:, sl], c, s1, s2) * (SCALE * LOG2E)).astype(BF16)
            ko_ref[:, sl] = (kv_raw[:, sl] + kr).astype(BF16)
        vo_ref[...] = kv_raw[:, 1024:1536].astype(BF16)

    tab = pl.BlockSpec((RB, LANE), lambda i: (i, 0))
    wide = pl.BlockSpec((RB, 1024), lambda i: (i, 0))
    full = lambda r, c: pl.BlockSpec((r, c), lambda i: (0, 0))
    return pl.pallas_call(
        body, grid=(lp // RB,),
        in_specs=[pl.BlockSpec((RB, 512), lambda i: (i, CQ0 // 512)), full(1, 256), full(1, 128),
                  full(Q_RANK, 1024), full(KV_RANK, 1536), tab, tab, tab],
        out_specs=[pl.BlockSpec((RB, 256), lambda i: (i, 0)), tab, wide, wide,
                   pl.BlockSpec((RB, 512), lambda i: (i, 0))],
        out_shape=[jax.ShapeDtypeStruct((lp, 256), BF16), jax.ShapeDtypeStruct((lp, 128), BF16),
                   jax.ShapeDtypeStruct((lp, 1024), BF16), jax.ShapeDtypeStruct((lp, 1024), BF16),
                   jax.ShapeDtypeStruct((lp, 512), BF16)],
        name="mla_up", compiler_params=_cp())(z, qg, kvg, w_uq, w_ukv, *tabs)


def _mla_up_bwd(z, qg, kvg, w_uq, w_ukv, dq, dk, dv, tabs, lp):
    def body(z_ref, qg_ref, kvg_ref, wuq_ref, wukv_ref, dq_ref, dk_ref, dv_ref, c_ref, s1_ref, s2_ref,
             dqr_ref, dkv_ref, dz_ref, dqg_ref, dkvg_ref):
        i = pl.program_id(0)
        c, s1, s2 = c_ref[...], s1_ref[...], s2_ref[...]
        lane = lax.broadcasted_iota(jnp.int32, (1, LANE), 1)
        ropel = (lane >= QK_NOPE) & (lane < QK_NOPE + QK_ROPE)
        ksum = jnp.zeros((RB, LANE), F32)
        for h in range(HEADS):
            sl = slice(LANE * h, LANE * (h + 1))
            dqr_ref[:, sl] = _rope_t(dq_ref[:, sl].astype(F32) * SCALE, c, s1, s2).astype(BF16)
            dkt = dk_ref[:, sl]
            dkv_ref[:, sl] = dkt
            ksum = ksum + dkt.astype(F32)
        dkv_ref[:, 1024:1536] = dv_ref[...]
        dkr = jnp.where(ropel, _rope_t(jnp.where(ropel, ksum, 0.0), c, s1, s2), 0.0)
        dqn = lax.dot_general(dqr_ref[...], wuq_ref[...], (((1,), (1,)), ((), ())), preferred_element_type=F32)
        dkvn = lax.dot_general(dkv_ref[...], wukv_ref[...], (((1,), (1,)), ((), ())), preferred_element_type=F32)

        def rms_b(xv, g, dy):
            r = lax.rsqrt(jnp.mean(xv * xv, axis=-1, keepdims=True) + EPS)
            a = dy * g
            dx = r * a - xv * (r * r * r) * jnp.mean(a * xv, axis=-1, keepdims=True)
            return dx, jnp.sum(dy * xv * r, axis=0, keepdims=True)

        dcq, pq = rms_b(z_ref[:, 0:256], qg_ref[...], dqn)
        dckv, pk = rms_b(z_ref[:, 256:384], kvg_ref[...], dkvn)
        dz_ref[:, 0:256] = dcq.astype(BF16)
        dz_ref[:, 256:384] = dckv.astype(BF16)
        dz_ref[:, 384:512] = dkr.astype(BF16)

        @pl.when(i == 0)
        def _():
            dqg_ref[...] = pq
            dkvg_ref[...] = pk

        @pl.when(i > 0)
        def _():
            dqg_ref[...] += pq
            dkvg_ref[...] += pk

    tab = pl.BlockSpec((RB, LANE), lambda i: (i, 0))
    wide = pl.BlockSpec((RB, 1024), lambda i: (i, 0))
    half = pl.BlockSpec((RB, 512), lambda i: (i, 0))
    full = lambda r, c: pl.BlockSpec((r, c), lambda i: (0, 0))
    return pl.pallas_call(
        body, grid=(lp // RB,),
        in_specs=[pl.BlockSpec((RB, 512), lambda i: (i, CQ0 // 512)), full(1, 256), full(1, 128),
                  full(Q_RANK, 1024), full(KV_RANK, 1536), wide, wide, half, tab, tab, tab],
        out_specs=[wide, pl.BlockSpec((RB, 1536), lambda i: (i, 0)), half, full(1, 256), full(1, 128)],
        out_shape=[jax.ShapeDtypeStruct((lp, 1024), BF16), jax.ShapeDtypeStruct((lp, 1536), BF16),
                   jax.ShapeDtypeStruct((lp, 512), BF16), jax.ShapeDtypeStruct((1, 256), F32),
                   jax.ShapeDtypeStruct((1, 128), F32)],
        name="mla_up_bwd", compiler_params=_cp())(z, qg, kvg, w_uq, w_ukv, dq, dk, dv, *tabs)


def _head_lanes(e):
    lane = lax.broadcasted_iota(jnp.int32, (1, LANE), 1)
    return lane >= V_DIM if e else lane < V_DIM


ONE_LANE = (V_DIM, 0)


def _attn_fwd(q, k, v, lp, gather=()):
    nq = lp // RB
    n = len(gather)
    steps = HEADS // 2

    def body(q_ref, k_ref, v_ref, *rest):
        o_ref, lse_ref = rest[n], rest[n + 1]
        vm_scr = rest[2 * n + 2]
        if n:
            g_start, g_forward, g_finish = _gather_phases(rest[:n], rest[n + 2:2 * n + 2], *rest[2 * n + 3:])
            pl.when(pl.program_id(0) == 0)(g_start)
            pl.when(pl.program_id(0) == steps - 1)(g_forward)
        lane = lax.broadcasted_iota(jnp.int32, (1, LANE), 1)
        vv = v_ref[...]
        for e in range(2):
            ones = jnp.where(lane == ONE_LANE[e], 1.0, 0.0).astype(BF16)
            vm_scr[e] = jnp.where(_head_lanes(e), vv, jnp.broadcast_to(ones, vv.shape))
        causal = (lax.broadcasted_iota(jnp.int32, (RB, RB), 1) <= lax.broadcasted_iota(jnp.int32, (RB, RB), 0))

        def qblock(i, _):
            rows = pl.ds(pl.multiple_of(i * RB, RB), RB)
            qs = [q_ref[rows, LANE * e:LANE * (e + 1)] for e in range(2)]

            def scores(j):
                cols = pl.ds(pl.multiple_of(j * RB, RB), RB)
                return tuple(lax.dot_general(qs[e], k_ref[cols, LANE * e:LANE * (e + 1)], (((1,), (1,)), ((), ())),
                                             preferred_element_type=F32) for e in range(2))

            def update(j, s, carry, masked):
                cols = pl.ds(pl.multiple_of(j * RB, RB), RB)
                out = []
                for e in range(2):
                    m, acc = carry[2 * e], carry[2 * e + 1]
                    se = jnp.where(causal, s[e], -jnp.inf) if masked else s[e]
                    m_new = jnp.maximum(m, jnp.max(se, axis=-1, keepdims=True))
                    p = jnp.exp2((se - m_new).astype(BF16))
                    acc = jnp.exp2(m - m_new) * acc + jnp.dot(p, vm_scr[e, cols, :], preferred_element_type=F32)
                    out += [m_new, acc]
                return tuple(out)

            def pair(jj, c):
                sa, sb = scores(2 * jj), scores(2 * jj + 1)
                return update(2 * jj + 1, sb, update(2 * jj, sa, c, False), False)

            m0 = jnp.full((RB, 1), -jnp.inf, F32)
            a0 = jnp.zeros((RB, LANE), F32)
            carry = lax.fori_loop(0, i // 2, pair, (m0, a0, m0, a0))

            def last_two(c):
                sa, sb = scores(i - 1), scores(i)
                return update(i, sb, update(i - 1, sa, c, False), True)

            carry = lax.cond(i % 2 == 1, last_two, lambda c: update(i, scores(i), c, True), carry)
            o, lse = [], []
            for e in range(2):
                m, acc = carry[2 * e], carry[2 * e + 1]
                l = acc[:, ONE_LANE[e]:ONE_LANE[e] + 1]
                o.append(acc / l)
                lse.append(jnp.broadcast_to(m + jnp.log2(l), (RB, LANE)))
            o_ref[rows, :] = jnp.where(_head_lanes(0), o[0], o[1])
            lse_ref[rows, :] = jnp.where(_head_lanes(0), lse[0], lse[1])
            return 0

        lax.fori_loop(0, nq, qblock, 0)
        if n:
            pl.when(pl.program_id(0) == steps - 1)(g_finish)

    two = pl.BlockSpec((lp, 2 * LANE), lambda h: (0, h))
    one = pl.BlockSpec((lp, LANE), lambda h: (0, h))
    anyspec = pl.BlockSpec(memory_space=pl.ANY)
    return pl.pallas_call(
        body, grid=(steps,), in_specs=[two, two, one] + [anyspec] * n, out_specs=[one, one] + [anyspec] * n,
        out_shape=[jax.ShapeDtypeStruct((lp, 512), F32), jax.ShapeDtypeStruct((lp, 512), F32)]
        + _gather_shapes(gather),
        scratch_shapes=[pltpu.VMEM((2, lp, LANE), BF16)] + (_comm_sems(n) if n else []),
        name="attn_fwd_gather" if n else "attn_fwd", compiler_params=_cp())(q, k, v, *gather)


def _attn_bwd(q, k, v, o, do, lse, lp, scatter=None):
    nq = lp // RB
    xs, bufs, layer = scatter if scatter else ((), (), None)
    n = len(xs)
    steps = HEADS // 2

    def body(q_ref, k_ref, v_ref, o_ref, do_ref, lse_ref, *rest):
        dq_ref, dk_ref, dv_ref = rest[2 * n:2 * n + 3]
        vm_scr, dom_scr, dl_scr, dq_scr = rest[3 * n + 3:3 * n + 7]
        if n:
            s_start, s_finish = _scatter_phases(rest[:n], rest[2 * n + 3:3 * n + 3], *rest[3 * n + 7:], layer)
            pl.when(pl.program_id(0) == 0)(s_start)
        causal = (lax.broadcasted_iota(jnp.int32, (RB, RB), 1) <= lax.broadcasted_iota(jnp.int32, (RB, RB), 0))
        vv = v_ref[...]
        for e in range(2):
            vm_scr[e] = jnp.where(_head_lanes(e), vv, jnp.zeros_like(vv))

        def prep(i, _):
            rows = pl.ds(pl.multiple_of(i * RB, RB), RB)
            prod = do_ref[rows, :] * o_ref[rows, :]
            dls = []
            for e in range(2):
                hm = _head_lanes(e)
                dom_scr[e, rows, :] = jnp.where(hm, do_ref[rows, :], 0.0).astype(BF16)
                dls.append(jnp.sum(jnp.where(hm, prod, 0.0), axis=-1, keepdims=True))
            dl_scr[rows, :] = jnp.where(_head_lanes(0), dls[0], dls[1])
            dq_scr[rows, :] = jnp.zeros((RB, 2 * LANE), F32)
            return 0

        lax.fori_loop(0, nq, prep, 0)

        def kvblock(j, _):
            cols = pl.ds(pl.multiple_of(j * RB, RB), RB)
            kbs = [k_ref[cols, LANE * e:LANE * (e + 1)] for e in range(2)]

            def products(i):
                rows = pl.ds(pl.multiple_of(i * RB, RB), RB)
                out = []
                for e in range(2):
                    out.append(lax.dot_general(q_ref[rows, LANE * e:LANE * (e + 1)], kbs[e],
                                               (((1,), (1,)), ((), ())), preferred_element_type=F32))
                    out.append(lax.dot_general(dom_scr[e, rows, :], vm_scr[e, cols, :],
                                               (((1,), (1,)), ((), ())), preferred_element_type=F32))
                return tuple(out)

            def update(i, sd, carry, masked):
                rows = pl.ds(pl.multiple_of(i * RB, RB), RB)
                out = []
                for e in range(2):
                    dk, dv = carry[2 * e], carry[2 * e + 1]
                    sl = slice(LANE * e, LANE * (e + 1))
                    col1 = slice(V_DIM * e, V_DIM * e + 1)
                    s, dp = sd[2 * e], sd[2 * e + 1]
                    if masked:
                        s = jnp.where(causal, s, -jnp.inf)
                    p = jnp.exp2((s - lse_ref[rows, col1]).astype(BF16))
                    dv = dv + lax.dot_general(p, dom_scr[e, rows, :], (((0,), (0,)), ((), ())),
                                              preferred_element_type=F32)
                    ds = p * (dp - dl_scr[rows, col1]).astype(BF16)
                    dk = dk + lax.dot_general(ds, q_ref[rows, sl], (((0,), (0,)), ((), ())),
                                              preferred_element_type=F32)
                    dq_scr[rows, sl] += jnp.dot(ds, kbs[e], preferred_element_type=F32)
                    out += [dk, dv]
                return tuple(out)

            def pair(t, c):
                i0 = nq - 2 - 2 * t
                pa, pb = products(i0), products(i0 + 1)
                return update(i0 + 1, pb, update(i0, pa, c, False), False)

            def first_two(c):
                pa, pb = products(j), products(j + 1)
                return update(j + 1, pb, update(j, pa, c, True), False)

            zero = jnp.zeros((RB, LANE), F32)
            below = nq - 1 - j
            carry = lax.fori_loop(0, below // 2, pair, (zero, zero, zero, zero))
            dk0, dv0, dk1, dv1 = lax.cond(below % 2 == 1, first_two,
                                          lambda c: update(j, products(j), c, True), carry)
            dk_ref[cols, 0:LANE] = (dk0 * LN2).astype(BF16)
            dk_ref[cols, LANE:2 * LANE] = (dk1 * LN2).astype(BF16)
            dv_ref[cols, :] = (dv0 + dv1).astype(BF16)
            return 0

        lax.fori_loop(0, nq, kvblock, 0)

        def fin(i, _):
            rows = pl.ds(pl.multiple_of(i * RB, RB), RB)
            dq_ref[rows, :] = dq_scr[rows, :].astype(BF16)
            return 0

        lax.fori_loop(0, nq, fin, 0)
        if n:
            pl.when(pl.program_id(0) == steps - 1)(s_finish)

    two = pl.BlockSpec((lp, 2 * LANE), lambda h: (0, h))
    one = pl.BlockSpec((lp, LANE), lambda h: (0, h))
    anyspec = pl.BlockSpec(memory_space=pl.ANY)
    return pl.pallas_call(
        body, grid=(steps,), in_specs=[two, two, one, one, one, one] + [anyspec] * (2 * n),
        out_specs=[two, two, one] + [anyspec] * n,
        out_shape=[jax.ShapeDtypeStruct((lp, 1024), BF16), jax.ShapeDtypeStruct((lp, 1024), BF16),
                   jax.ShapeDtypeStruct((lp, 512), BF16)] + [jax.ShapeDtypeStruct(b.shape, b.dtype) for b in bufs],
        input_output_aliases={6 + n + a: 3 + a for a in range(n)},
        scratch_shapes=[pltpu.VMEM((2, lp, LANE), BF16), pltpu.VMEM((2, lp, LANE), BF16),
                        pltpu.VMEM((lp, LANE), F32), pltpu.VMEM((lp, 2 * LANE), F32)]
        + (_comm_sems(n) if n else []),
        name="attn_bwd_scatter" if n else "attn_bwd", compiler_params=_cp())(q, k, v, o, do, lse, *xs, *bufs)


def _pool_lane_windows():
    lane = lax.broadcasted_iota(jnp.int32, (1, 256), 1)
    return jnp.where(lane < 64, 2, jnp.where(lane < 128, 4, jnp.where(lane < 192, 8, 16)))


def _by_window(wl, s2, s4, s8, s16):
    return jnp.where(wl == 2, s2, jnp.where(wl == 4, s4, jnp.where(wl == 8, s8, s16)))


def _pool_fwd_rows(pv_ext, t0):
    n = pv_ext.shape[0]
    wl = _pool_lane_windows()
    s2 = pv_ext + _dn(pv_ext, 1)
    s4 = s2 + _dn(s2, 2)
    s8 = s4 + _dn(s4, 4)
    s16 = s8 + _dn(s8, 8)
    t = t0 + lax.broadcasted_iota(jnp.int32, (n, 1), 0)
    cnt = jnp.maximum(jnp.minimum(t + 1, wl), 1).astype(F32)
    return _by_window(wl, s2, s4, s8, s16) / cnt - pv_ext


def _conv_dn(x_ext, w_ref, taps):
    acc = w_ref[taps - 1:taps, :] * x_ext
    for j in range(1, taps):
        acc = acc + w_ref[taps - 1 - j:taps - j, :] * _dn(x_ext, j)
    return acc


def _conv_up(g_ext, w_ref, taps):
    acc = w_ref[taps - 1:taps, :] * g_ext
    for j in range(1, taps):
        acc = acc + w_ref[taps - 1 - j:taps - j, :] * _up(g_ext, j)
    return acc


def _ln_fwd(c, g, b):
    mu = jnp.mean(c, axis=-1, keepdims=True)
    xc = c - mu
    r = lax.rsqrt(jnp.mean(xc * xc, axis=-1, keepdims=True) + EPS)
    xh = xc * r
    return xh * g + b, xh, r


def _halo_specs(lp, width, col):
    per = RB // HB
    last = lp // HB - 1
    cur = pl.BlockSpec((RB, width), lambda i: (i, col))
    prev = pl.BlockSpec((HB, width), lambda i: (jnp.maximum(i * per - 1, 0), col))
    nxt = pl.BlockSpec((HB, width), lambda i: (jnp.minimum((i + 1) * per, last), col))
    return cur, prev, nxt


def _mix_fwd(z, oat, bd, pscale, cw, cb, lng, lnb, sw, lp):
    def body(za, zah, mg, oat_ref, cu, cuh, cg, sbb, sbc, sbch, sbx, sbxh, sg,
             bd_ref, ps_ref, cw_ref, cb_ref, lng_ref, lnb_ref, sw_ref, u_ref):
        i = pl.program_id(0)
        pm = jnp.where(i > 0, 1.0, 0.0).astype(F32)
        pv = jnp.concatenate([zah[:, 0:256] * pm, za[:, 0:256]], axis=0)
        p = _pool_fwd_rows(pv, i * RB - HB)[HB:]
        y = jnp.dot(p.astype(BF16), bd_ref[...], preferred_element_type=F32)
        u_ref[:, 0:256] = (y * ps_ref[...] * _silu(za[:, 256:512])).astype(BF16)
        u_ref[:, 256:768] = (oat_ref[...] * _silu(mg[...])).astype(BF16)
        ce = jnp.concatenate([cuh[...] * pm, cu[...]], axis=0)
        glu = ce[:, 0:256] * _sig(ce[:, 256:512])
        c = _conv_dn(glu, cw_ref, CONF_K)[HB:] + cb_ref[...]
        n, _, _ = _ln_fwd(c, lng_ref[...], lnb_ref[...])
        u_ref[:, 768:1024] = (_silu(n) * _silu(cg[...])).astype(BF16)
        qe = jnp.concatenate([sbch[...] * sbxh[...] * pm, sbc[...] * sbx[...]], axis=0)
        cv = _conv_dn(qe, sw_ref, SC_K)[HB:]
        u_ref[:, 1024:1280] = (sbb[...] * cv * _silu(sg[...])).astype(BF16)

    a_cur, a_prev, _ = _halo_specs(lp, 512, PV0 // 512)
    cu_cur, cu_prev, _ = _halo_specs(lp, 512, CU0 // 512)
    sc_cur, sc_prev, _ = _halo_specs(lp, 256, SBC0 // 256)
    sx_cur, sx_prev, _ = _halo_specs(lp, 256, SBX0 // 256)
    c256 = lambda c0: pl.BlockSpec((RB, 256), lambda i: (i, c0 // 256))
    full = lambda r, c: pl.BlockSpec((r, c), lambda i: (0, 0))
    return pl.pallas_call(
        body, grid=(lp // RB,),
        in_specs=[a_cur, a_prev, pl.BlockSpec((RB, 512), lambda i: (i, MG0 // 512)),
                  pl.BlockSpec((RB, 512), lambda i: (i, 0)),
                  cu_cur, cu_prev, c256(CG0), c256(SBB0), sc_cur, sc_prev, sx_cur, sx_prev, c256(SG0),
                  full(256, 256), full(1, 256), full(32, 256), full(1, 256), full(1, 256), full(1, 256),
                  full(8, 256)],
        out_specs=pl.BlockSpec((RB, 1280), lambda i: (i, 0)),
        out_shape=jax.ShapeDtypeStruct((lp, 1280), BF16),
        name="mix_fwd", compiler_params=_cp())(z, z, z, oat, z, z, z, z, z, z, z, z, z,
                                               bd, pscale, cw, cb, lng, lnb, sw)


def _mix_bwd(z, oat, du, bd, pscale, cw, cb, lng, lnb, sw, lp):
    nb = lp // RB
    ne = RB + 2 * HB
    nf = RB + HB

    def body(za, zah, zan, mg, oat_ref, cu, cuh, cun, cg, cgn, sbb, sbbn, sbc, sbch, sbcn, sbx, sbxh, sbxn,
             sg, sgn, du_ref, dun_ref, bd_ref, ps_ref, cw_ref, cb_ref, lng_ref, lnb_ref, sw_ref,
             dzx_ref, doat_ref, dbd_ref, dcw_ref, dsw_ref, dsm_ref):
        xa, xm, xc = 0, MG0 - PV0, CU0 - PV0
        i = pl.program_id(0)
        pm = jnp.where(i > 0, 1.0, 0.0).astype(F32)
        nm = jnp.where(i < nb - 1, 1.0, 0.0).astype(F32)

        def ext(cur, prev, nxt, sl=slice(None)):
            return jnp.concatenate([prev[:, sl] * pm, cur[:, sl], nxt[:, sl] * nm], axis=0)

        def fwd(cur, nxt, sl=slice(None)):
            return jnp.concatenate([cur[:, sl], nxt[:, sl] * nm], axis=0)

        def csum(x):
            return jnp.sum(x, axis=0, keepdims=True)

        @pl.when(i == 0)
        def _():
            dbd_ref[...] = jnp.zeros((256, 256), F32)
            dcw_ref[...] = jnp.zeros((32, 256), F32)
            dsw_ref[...] = jnp.zeros((8, 256), F32)
            dsm_ref[...] = jnp.zeros((8, 256), F32)

        a_cols, b_cols = slice(0, 256), slice(256, 512)
        pv_e = ext(za, zah, zan, a_cols)
        p = _pool_fwd_rows(pv_e, i * RB - HB)[HB:HB + RB]
        pb = p.astype(BF16)
        y = jnp.dot(pb, bd_ref[...], preferred_element_type=F32)
        pg_f = fwd(za, zan, b_cols)
        dua_f = fwd(du_ref, dun_ref, slice(0, 256))
        dyp_f = dua_f * ps_ref[...] * _silu(pg_f)
        dypb = dyp_f.astype(BF16)
        dp_f = lax.dot_general(dypb, bd_ref[...], (((1,), (1,)), ((), ())), preferred_element_type=F32)
        wl = _pool_lane_windows()
        t = i * RB + lax.broadcasted_iota(jnp.int32, (nf, 1), 0)
        cnt = jnp.minimum(t + 1, wl).astype(F32)
        qf = dp_f / cnt
        f2 = qf + _up(qf, 1)
        f4 = f2 + _up(f2, 2)
        f8 = f4 + _up(f4, 4)
        f16 = f8 + _up(f8, 8)
        dpv = (_by_window(wl, f2, f4, f8, f16) - dp_f)[0:RB]
        dua = du_ref[:, 0:256]
        pg = za[:, b_cols]
        dpg = dua * y * ps_ref[...] * _dsilu(pg)
        dzx_ref[:, xa:xa + 256] = dpv.astype(BF16)
        dzx_ref[:, xa + 256:xa + 512] = dpg.astype(BF16)
        d_scale = csum(dua * y * _silu(pg))
        d_bd = lax.dot_general(pb, dypb[0:RB], (((0,), (0,)), ((), ())), preferred_element_type=F32)

        dub = du_ref[:, 256:768]
        mgv = mg[...]
        dzx_ref[:, xm:xm + 512] = (dub * oat_ref[...] * _dsilu(mgv)).astype(BF16)
        doat_ref[...] = dub * _silu(mgv)

        a_e = ext(cu, cuh, cun, slice(0, 256))
        gt_e = ext(cu, cuh, cun, slice(256, 512))
        sg_e = _sig(gt_e)
        glu_e = a_e * sg_e
        c_f = _conv_dn(glu_e, cw_ref, CONF_K)[HB:] + cb_ref[...]
        n_f, xh_f, r_f = _ln_fwd(c_f, lng_ref[...], lnb_ref[...])
        cg_f = fwd(cg, cgn)
        duc_f = fwd(du_ref, dun_ref, slice(768, 1024))
        sw_f = _silu(n_f)
        dcg = (duc_f * sw_f * _dsilu(cg_f))[0:RB]
        dn_f = duc_f * _silu(cg_f) * _dsilu(n_f)
        a_f = dn_f * lng_ref[...]
        dc_f = r_f * (a_f - jnp.mean(a_f, axis=-1, keepdims=True)
                      - xh_f * jnp.mean(a_f * xh_f, axis=-1, keepdims=True))
        d_lng = csum((dn_f * xh_f)[0:RB])
        d_lnb = csum(dn_f[0:RB])
        d_cb = csum(dc_f[0:RB])
        dglu = _conv_up(dc_f, cw_ref, CONF_K)[0:RB]
        dc_c = dc_f[0:RB]
        for kk in range(CONF_K):
            j = CONF_K - 1 - kk
            dcw_ref[kk:kk + 1, :] += csum(dc_c * _dn(glu_e, j)[HB:HB + RB])

        sgc = sg_e[HB:HB + RB]
        a_c = a_e[HB:HB + RB]
        dzx_ref[:, xc:xc + 256] = (dglu * sgc).astype(BF16)
        dzx_ref[:, xc + 256:xc + 512] = (dglu * a_c * sgc * (1.0 - sgc)).astype(BF16)
        dzx_ref[:, xc + 512:xc + 768] = dcg.astype(BF16)

        c_e = ext(sbc, sbch, sbcn)
        x_e = ext(sbx, sbxh, sbxn)
        q_e = c_e * x_e
        cv_f = _conv_dn(q_e, sw_ref, SC_K)[HB:]
        bg_f = fwd(sbb, sbbn)
        sg_f = fwd(sg, sgn)
        dud_f = fwd(du_ref, dun_ref, slice(1024, 1280))
        ssg_f = _silu(sg_f)
        dcv_f = dud_f * bg_f * ssg_f
        dbg = (dud_f * cv_f * ssg_f)[0:RB]
        dsg = (dud_f * bg_f * cv_f * _dsilu(sg_f))[0:RB]
        dq = _conv_up(dcv_f, sw_ref, SC_K)[0:RB]
        dcv_c = dcv_f[0:RB]
        for kk in range(SC_K):
            j = SC_K - 1 - kk
            dsw_ref[kk:kk + 1, :] += csum(dcv_c * _dn(q_e, j)[HB:HB + RB])

        dzx_ref[:, xc + 768:xc + 1024] = dbg.astype(BF16)
        dzx_ref[:, xc + 1024:xc + 1280] = (dq * x_e[HB:HB + RB]).astype(BF16)
        dzx_ref[:, xc + 1280:xc + 1536] = (dq * c_e[HB:HB + RB]).astype(BF16)
        dzx_ref[:, xc + 1536:xc + 1792] = dsg.astype(BF16)

        dbd_ref[...] += d_bd
        dsm_ref[0:1, :] += d_scale
        dsm_ref[1:2, :] += d_cb
        dsm_ref[2:3, :] += d_lng
        dsm_ref[3:4, :] += d_lnb

    a3 = _halo_specs(lp, 512, PV0 // 512)
    cu3 = _halo_specs(lp, 512, CU0 // 512)
    cg3 = _halo_specs(lp, 256, CG0 // 256)
    sbb3 = _halo_specs(lp, 256, SBB0 // 256)
    sbc3 = _halo_specs(lp, 256, SBC0 // 256)
    sbx3 = _halo_specs(lp, 256, SBX0 // 256)
    sg3 = _halo_specs(lp, 256, SG0 // 256)
    du3 = _halo_specs(lp, 1280, 0)
    full = lambda r, c: pl.BlockSpec((r, c), lambda i: (0, 0))
    in_specs = [a3[0], a3[1], a3[2], pl.BlockSpec((RB, 512), lambda i: (i, MG0 // 512)),
                pl.BlockSpec((RB, 512), lambda i: (i, 0)),
                cu3[0], cu3[1], cu3[2], cg3[0], cg3[2], sbb3[0], sbb3[2],
                sbc3[0], sbc3[1], sbc3[2], sbx3[0], sbx3[1], sbx3[2], sg3[0], sg3[2],
                du3[0], du3[2],
                full(256, 256), full(1, 256), full(32, 256), full(1, 256), full(1, 256), full(1, 256),
                full(8, 256)]
    out_specs = [pl.BlockSpec((RB, ZW - PV0), lambda i: (i, 0)), pl.BlockSpec((RB, 512), lambda i: (i, 0)),
                 full(256, 256), full(32, 256), full(8, 256), full(8, 256)]
    out_shape = [jax.ShapeDtypeStruct((lp, ZW - PV0), BF16), jax.ShapeDtypeStruct((lp, 512), F32),
                 jax.ShapeDtypeStruct((256, 256), F32), jax.ShapeDtypeStruct((32, 256), F32),
                 jax.ShapeDtypeStruct((8, 256), F32), jax.ShapeDtypeStruct((8, 256), F32)]
    return pl.pallas_call(
        body, grid=(nb,), in_specs=in_specs, out_specs=out_specs, out_shape=out_shape,
        name="mix_bwd", compiler_params=_cp())(
            z, z, z, z, oat, z, z, z, z, z, z, z, z, z, z, z, z, z, z, z, du, du,
            bd, pscale, cw, cb, lng, lnb, sw)


U_OFF = (0, 256, 768, 1024, 1280)


def _merge_fwd(x, u, z, gb, wout, wo, gpost, lp):
    MRB = RB

    def body(x_ref, u_ref, gl_ref, gb_ref, wout_ref, wo_ref, g_ref, xo_ref, m_ref, o2_ref):
        m = jnp.zeros((MRB, D), F32)
        for b in range(4):
            y = jnp.dot(u_ref[:, U_OFF[b]:U_OFF[b + 1]], wout_ref[U_OFF[b]:U_OFF[b + 1], :],
                        preferred_element_type=F32)
            sl = slice(D * b, D * (b + 1))
            m = m + _sig(gl_ref[:, sl] + gb_ref[:, sl]) * y
        mb = m.astype(BF16)
        m_ref[...] = mb
        o2 = jnp.dot(mb, wo_ref[...], preferred_element_type=F32)
        o2_ref[...] = o2
        r = lax.rsqrt(jnp.mean(o2 * o2, axis=-1, keepdims=True) + EPS)
        xo_ref[...] = x_ref[...] + o2 * r * g_ref[...]

    blk = pl.BlockSpec((MRB, D), lambda i: (i, 0))
    full = lambda r, c: pl.BlockSpec((r, c), lambda i: (0, 0))
    return pl.pallas_call(
        body, grid=(lp // MRB,),
        in_specs=[blk, pl.BlockSpec((MRB, 1280), lambda i: (i, 0)), pl.BlockSpec((MRB, 4096), lambda i: (i, 0)),
                  full(1, 4096), full(1280, D), full(D, D), full(1, D)],
        out_specs=[blk, blk, blk],
        out_shape=[jax.ShapeDtypeStruct((lp, D), F32), jax.ShapeDtypeStruct((lp, D), BF16),
                   jax.ShapeDtypeStruct((lp, D), F32)],
        name="merge_fwd", compiler_params=_cp())(x, u, z, gb, wout, wo, gpost)


def _merge_bwd(dx, o2, u, z, gb, wout, wo, gpost, lp):
    MRB = RB

    def body(dx_ref, o2_ref, u_ref, gl_ref, gb_ref, wout_ref, wo_ref, g_ref,
             do2_ref, dgl_ref, dy_ref, du_ref, dgb_ref, dg_ref):
        i = pl.program_id(0)
        o2 = o2_ref[...]
        dy = dx_ref[...]
        r = lax.rsqrt(jnp.mean(o2 * o2, axis=-1, keepdims=True) + EPS)
        a = dy * g_ref[...]
        do2 = (r * a - o2 * (r * r * r) * jnp.mean(a * o2, axis=-1, keepdims=True)).astype(BF16)
        do2_ref[...] = do2
        dg = jnp.sum(dy * o2 * r, axis=0, keepdims=True)
        dm = lax.dot_general(do2, wo_ref[...], (((1,), (1,)), ((), ())), preferred_element_type=F32)
        for b in range(4):
            rows = slice(U_OFF[b], U_OFF[b + 1])
            y = jnp.dot(u_ref[:, rows], wout_ref[rows, :], preferred_element_type=F32)
            sl = slice(D * b, D * (b + 1))
            gt = _sig(gl_ref[:, sl] + gb_ref[:, sl])
            dgl = dm * y * gt * (1.0 - gt)
            dgl_ref[:, sl] = dgl.astype(BF16)
            part = jnp.sum(dgl, axis=0, keepdims=True)

            @pl.when(i == 0)
            def _(part=part, sl=sl):
                dgb_ref[:, sl] = part

            @pl.when(i > 0)
            def _(part=part, sl=sl):
                dgb_ref[:, sl] += part

            dyb = (dm * gt).astype(BF16)
            dy_ref[:, sl] = dyb
            du_ref[:, rows] = lax.dot_general(dyb, wout_ref[rows, :], (((1,), (1,)), ((), ())),
                                              preferred_element_type=F32)

        @pl.when(i == 0)
        def _():
            dg_ref[...] = dg

        @pl.when(i > 0)
        def _():
            dg_ref[...] += dg

    blk = pl.BlockSpec((MRB, D), lambda i: (i, 0))
    wide = pl.BlockSpec((MRB, 4096), lambda i: (i, 0))
    ub = pl.BlockSpec((MRB, 1280), lambda i: (i, 0))
    full = lambda r, c: pl.BlockSpec((r, c), lambda i: (0, 0))
    once = lambda r, c: pl.BlockSpec((r, c), lambda i: (0, 0), pipeline_mode=pl.Buffered(1))
    return pl.pallas_call(
        body, grid=(lp // MRB,),
        in_specs=[blk, blk, ub, wide, full(1, 4096), once(1280, D), once(D, D), full(1, D)],
        out_specs=[blk, wide, wide, ub, full(1, 4096), full(1, D)],
        out_shape=[jax.ShapeDtypeStruct((lp, D), BF16), jax.ShapeDtypeStruct((lp, 4096), BF16),
                   jax.ShapeDtypeStruct((lp, 4096), BF16), jax.ShapeDtypeStruct((lp, 1280), F32),
                   jax.ShapeDtypeStruct((1, 4096), F32), jax.ShapeDtypeStruct((1, D), F32)],
        name="merge_bwd", compiler_params=_cp())(dx, o2, u, z, gb, wout, wo, gpost)


def _dh(dgl, dzq, dzx, w_in, x, g, dx_in, lp, scatter=None):
    xs, bufs, layer = scatter if scatter else ((), (), None)
    n = len(xs)
    steps = lp // RB
    segs = ((0, CQ0), (CQ0, PV0), (PV0, ZW))

    def body(gl_ref, zq_ref, zx_ref, w_ref, x_ref, g_ref, dxi_ref, *rest):
        dx_ref, dg_ref = rest[2 * n], rest[2 * n + 1]
        i = pl.program_id(0)
        if n:
            s_start, s_finish = _scatter_phases(rest[:n], rest[2 * n + 2:3 * n + 2], *rest[3 * n + 2:], layer)
            pl.when(i == 0)(s_start)
        dh = None
        for a_ref, (lo, hi) in zip((gl_ref, zq_ref, zx_ref), segs):
            part = lax.dot_general(a_ref[...], w_ref[:, lo:hi], (((1,), (1,)), ((), ())),
                                   preferred_element_type=F32)
            dh = part if dh is None else dh + part
        xv = x_ref[...]
        r = lax.rsqrt(jnp.mean(xv * xv, axis=-1, keepdims=True) + EPS)
        a = dh * g_ref[...]
        dx_ref[...] = dxi_ref[...] + r * a - xv * (r * r * r) * jnp.mean(a * xv, axis=-1, keepdims=True)
        part = jnp.sum(dh * xv * r, axis=0, keepdims=True)

        @pl.when(i == 0)
        def _():
            dg_ref[...] = part

        @pl.when(i > 0)
        def _():
            dg_ref[...] += part

        if n:
            pl.when(i == steps - 1)(s_finish)

    anyspec = pl.BlockSpec(memory_space=pl.ANY)
    row = lambda w: pl.BlockSpec((RB, w), lambda i: (i, 0))
    vec = pl.BlockSpec((1, D), lambda i: (0, 0))
    resident = pl.BlockSpec((D, ZW), lambda i: (0, 0), pipeline_mode=pl.Buffered(1))
    return pl.pallas_call(
        body, grid=(steps,),
        in_specs=[row(CQ0), row(PV0 - CQ0), row(ZW - PV0), resident, row(D), vec, row(D)] + [anyspec] * (2 * n),
        out_specs=[row(D), vec] + [anyspec] * n,
        out_shape=[jax.ShapeDtypeStruct((lp, D), F32), jax.ShapeDtypeStruct((1, D), F32)]
        + [jax.ShapeDtypeStruct(b.shape, b.dtype) for b in bufs],
        input_output_aliases={7 + n + a: 2 + a for a in range(n)},
        scratch_shapes=_comm_sems(n) if n else [],
        name="dh_scatter" if n else "dh", compiler_params=_cp())(dgl, dzq, dzx, w_in, x, g, dx_in, *xs, *bufs)


def _loss_head(xf, tgt, n_real, lp):
    def body(x_ref, t_ref, dy_ref, ls_ref):
        i = pl.program_id(0)
        t = i * RB + lax.broadcasted_iota(jnp.int32, (RB, 1), 0)
        real = (t >= N_META) & (t < n_real)
        err = jnp.where(real, x_ref[...] - t_ref[...], 0.0)
        dy_ref[...] = err / D
        part = 0.5 * jnp.sum(jnp.mean(err * err, axis=-1, keepdims=True), axis=0, keepdims=True)
        part = jnp.broadcast_to(part, (8, LANE))

        @pl.when(i == 0)
        def _():
            ls_ref[...] = part

        @pl.when(i > 0)
        def _():
            ls_ref[...] += part

    blk = pl.BlockSpec((RB, D), lambda i: (i, 0))
    return pl.pallas_call(
        body, grid=(lp // RB,), in_specs=[blk, blk],
        out_specs=[blk, pl.BlockSpec((8, LANE), lambda i: (0, 0))],
        out_shape=[jax.ShapeDtypeStruct((lp, D), F32), jax.ShapeDtypeStruct((8, LANE), F32)],
        name="loss_head", compiler_params=_cp())(xf, tgt)


def _peer(d):
    x, y, c = lax.axis_index("x"), lax.axis_index("y"), lax.axis_index("c")
    return (x ^ ((d >> 2) & 1), y ^ ((d >> 1) & 1), c ^ (d & 1))


def _index_of(p):
    return 4 * p[0] + 2 * p[1] + p[2]


def _all_gather(xs, name):
    n = len(xs)

    def body(*refs):
        start, forward, finish = _gather_phases(refs[:n], refs[n:2 * n], *refs[2 * n:])
        start()
        forward()
        finish()

    anyspec = pl.BlockSpec(memory_space=pl.ANY)
    return pl.pallas_call(
        body, in_specs=[anyspec] * n, out_specs=[anyspec] * n,
        out_shape=_gather_shapes(xs), scratch_shapes=_comm_sems(n), name=name)(*xs)


def _gather_shapes(xs):
    return [jax.ShapeDtypeStruct((N_DEV,) + x.shape, x.dtype) for x in xs]


def _comm_sems(n):
    return [pltpu.SemaphoreType.DMA((7 * n,)), pltpu.SemaphoreType.DMA((7 * n,)), pltpu.SemaphoreType.DMA((n,))]


def _gather_phases(x_refs, out_refs, send_sems, recv_sems, local_sems):
    n = len(x_refs)
    chips = [2, 4, 6]

    def copy(a, kk, block, to, src=None):
        slot = out_refs[a].at[_index_of(block)]
        return pltpu.make_async_remote_copy(
            src_ref=slot if src is None else src, dst_ref=slot,
            send_sem=send_sems.at[7 * a + kk], recv_sem=recv_sems.at[7 * a + kk], device_id=to,
            device_id_type=MESH)

    def local(a):
        return pltpu.make_async_copy(x_refs[a], out_refs[a].at[_index_of(_peer(0))], local_sems.at[a])

    def firsts():
        out = []
        for a in range(n):
            out.append(copy(a, 0, _peer(0), _peer(1), src=x_refs[a]))
            out += [copy(a, 1 + j, _peer(0), _peer(d), src=x_refs[a]) for j, d in enumerate(chips)]
        return out

    def passes():
        return [copy(a, 4 + j, _peer(d), _peer(1)) for j, d in enumerate(chips) for a in range(n)]

    def start():
        for a in range(n):
            local(a).start()
        for cp in firsts():
            cp.start()

    def forward():
        for j, d in enumerate(chips):
            for a in range(n):
                copy(a, 1 + j, _peer(d), _peer(0)).wait_recv()
                copy(a, 4 + j, _peer(d), _peer(1)).start()

    def finish():
        for a in range(n):
            copy(a, 0, _peer(1), _peer(0)).wait_recv()
            for j, d in enumerate(chips):
                copy(a, 4 + j, _peer(d | 1), _peer(0)).wait_recv()
        for cp in firsts() + passes():
            cp.wait_send()
        for a in range(n):
            local(a).wait()

    return start, forward, finish


def _scatter_phases(x_refs, out_refs, send_sems, recv_sems, local_sems, layer):
    n = len(x_refs)

    def land(a, dev):
        slot = out_refs[a].at[dev]
        return slot if layer is None else slot.at[layer]

    def local(a):
        my = _index_of(_peer(0))
        return pltpu.make_async_copy(x_refs[a].at[my], land(a, my), local_sems.at[a])

    def copy(a, d):
        my = _index_of(_peer(0))
        return pltpu.make_async_remote_copy(
            src_ref=x_refs[a].at[_index_of(_peer(d))], dst_ref=land(a, my),
            send_sem=send_sems.at[7 * a + d - 1], recv_sem=recv_sems.at[7 * a + d - 1], device_id=_peer(d),
            device_id_type=MESH)

    def arrival(a, d):
        frm = _index_of(_peer(d))
        return pltpu.make_async_remote_copy(
            src_ref=x_refs[a].at[frm], dst_ref=land(a, frm),
            send_sem=send_sems.at[7 * a + d - 1], recv_sem=recv_sems.at[7 * a + d - 1], device_id=_peer(d),
            device_id_type=MESH)

    def start():
        for a in range(n):
            local(a).start()
        for d in range(1, N_DEV):
            for a in range(n):
                copy(a, d).start()

    def finish():
        for d in range(1, N_DEV):
            for a in range(n):
                arrival(a, d).wait_recv()
        for d in range(1, N_DEV):
            for a in range(n):
                copy(a, d).wait_send()
        for a in range(n):
            local(a).wait()

    return start, finish


def _all_to_all(xs, bufs, layer, name):
    n = len(xs)

    def body(*refs):
        start, finish = _scatter_phases(refs[:n], refs[2 * n:3 * n], *refs[3 * n:], layer)
        start()
        finish()

    anyspec = pl.BlockSpec(memory_space=pl.ANY)
    return pl.pallas_call(
        body, in_specs=[anyspec] * (2 * n), out_specs=[anyspec] * n,
        out_shape=[jax.ShapeDtypeStruct(b.shape, b.dtype) for b in bufs],
        input_output_aliases={n + a: a for a in range(n)},
        scratch_shapes=_comm_sems(n), name=name)(*xs, *bufs)


def _adam_math(g, w, m, v):
    c1 = 1.0 - ADAM_B1 ** ADAM_STEP
    c2 = 1.0 - ADAM_B2 ** ADAM_STEP
    mn = ADAM_B1 * m + (1.0 - ADAM_B1) * g
    vn = ADAM_B2 * v + (1.0 - ADAM_B2) * (g * g)
    return -ADAM_LR * ((mn / c1) / (jnp.sqrt(vn / c2) + ADAM_EPS) + ADAM_WD * w), mn, vn


def _reduce_adam(parts, w, m, v, rb, name, row_off=0):
    depth, rows, cols = w.shape
    assert rows % rb == 0 and row_off % rb == 0

    def body(p_ref, w_ref, m_ref, v_ref, g_ref, d_ref, mo_ref, vo_ref):
        g = p_ref[0, 0].astype(F32)
        for j in range(1, N_DEV):
            g = g + p_ref[j, 0].astype(F32)
        g_ref[0] = g
        d_ref[0], mo_ref[0], vo_ref[0] = _adam_math(g, w_ref[0], m_ref[0], v_ref[0])

    blk = pl.BlockSpec((1, rb, cols), lambda l, i: (l, i, 0))
    out = jax.ShapeDtypeStruct(w.shape, F32)
    return pl.pallas_call(
        body, grid=(depth, rows // rb),
        in_specs=[pl.BlockSpec((N_DEV, 1, rb, cols), lambda l, i: (0, l, i + row_off // rb, 0)), blk, blk, blk],
        out_specs=[blk, blk, blk, blk], out_shape=[out, out, out, out],
        name=name, compiler_params=_cp())(parts, w, m, v)


def _reduce_adam_flat(parts, w, m, v, name):
    q_rows = w.shape[0]

    def body(p_ref, w_ref, m_ref, v_ref, g_ref, d_ref, mo_ref, vo_ref):
        g = p_ref[0].astype(F32)
        for j in range(1, N_DEV):
            g = g + p_ref[j].astype(F32)
        g_ref[...] = g
        d_ref[...], mo_ref[...], vo_ref[...] = _adam_math(g, w_ref[...], m_ref[...], v_ref[...])

    blk = pl.BlockSpec((q_rows, LANE), lambda i: (0, 0))
    out = jax.ShapeDtypeStruct((q_rows, LANE), F32)
    return pl.pallas_call(
        body, grid=(1,), in_specs=[pl.BlockSpec((N_DEV, q_rows, LANE), lambda i: (0, 0, 0)), blk, blk, blk],
        out_specs=[blk, blk, blk, blk], out_shape=[out, out, out, out],
        name=name, compiler_params=_cp())(parts, w, m, v)


C128 = (("w_out_pool", 256), ("w_out_mla", 512), ("w_out_conf", 256), ("w_out_sc", 256), ("w_ukv", 128))
C128_ROWS = sum(r for _, r in C128)
TAIL = (("meta_tokens", (N_META, 128)), ("conf_dw_w", (DEPTH, CONF_K, 32)), ("sc_dw_w", (DEPTH, SC_K, 32)))
TAIL_ROWS = sum(int(np.prod(s)) for _, s in TAIL) // LANE
TAIL_PAD = 56
SMALL = (("pre_norm_g", (DEPTH, D)), ("gate_bias", (DEPTH, 4096)), ("pool_w", (DEPTH, 4, 64, 64)),
         ("pool_scale", (DEPTH, 256)), ("q_norm_g", (DEPTH, 256)), ("kv_norm_g", (DEPTH, 128)),
         ("conf_dw_b", (DEPTH, 256)), ("conf_ln_g", (DEPTH, 256)), ("conf_ln_b", (DEPTH, 256)),
         ("post_norm_g", (DEPTH, D)))
SMALL_ROWS = sum(int(np.prod(s)) for _, s in SMALL) // LANE
SMALL_PAD = -(-(SMALL_ROWS + 1) // 8) * 8


def _pack_tail(t):
    parts = [t[n].reshape(-1, LANE) for n, _ in TAIL]
    parts.append(jnp.zeros((TAIL_PAD - TAIL_ROWS, LANE), F32))
    return jnp.concatenate(parts, axis=0)


def _unpack_tail(flat):
    out, off = {}, 0
    for n, s in TAIL:
        rows = int(np.prod(s)) // LANE
        out[n] = flat[off:off + rows].reshape(s)
        off += rows
    return out


def _unpack_tail_full(g):
    out, off = {}, 0
    for n, s in TAIL:
        rows = int(np.prod(s)) // LANE
        blk = jnp.moveaxis(g[:, off:off + rows].reshape((N_DEV,) + s), 0, -2)
        out[n] = blk.reshape(s[:-1] + (N_DEV * s[-1],))
        off += rows
    return out


def _pack_small(t, extra_row):
    parts = [t[n].reshape(-1, LANE) for n, _ in SMALL] + [extra_row]
    parts.append(jnp.zeros((SMALL_PAD - SMALL_ROWS - 1, LANE), F32))
    return jnp.concatenate(parts, axis=0)


def _unpack_small(flat):
    out, off = {}, 0
    for n, s in SMALL:
        rows = int(np.prod(s)) // LANE
        out[n] = flat[off:off + rows].reshape(s)
        off += rows
    return out


def _cols_by_dest(g, width):
    r = g.shape[0]
    return g.reshape(r, N_DEV, width).transpose(1, 0, 2)


def _cols_full(gathered):
    _, r, c = gathered.shape
    return gathered.transpose(1, 0, 2).reshape(r, N_DEV * c)


W_IN_SHARD = 916
PACKED_SEGS = ((3232, 7328), (512, 896), 64, (896, 928), 32, (0, 512), (928, 3232))


def _pack_w_in(g):
    parts = []
    for seg in PACKED_SEGS:
        if isinstance(seg, int):
            parts.append(jnp.zeros((g.shape[1], seg), g.dtype))
            continue
        a, b = seg
        while a < b:
            k = a // W_IN_SHARD
            hi = min(b, W_IN_SHARD * (k + 1))
            parts.append(g[k, :, a - W_IN_SHARD * k:hi - W_IN_SHARD * k])
            a = hi
    return jnp.concatenate(parts, axis=1)


def _w_in_grad_by_dest(gl, mla, mix):
    src = (((0, 512), mix, 0), ((512, 896), mla, 0), ((896, 928), mla, 448), ((928, 3232), mix, 512),
           ((3232, 7328), gl, 0))
    blocks = []
    for k in range(N_DEV):
        lo, hi = W_IN_SHARD * k, W_IN_SHARD * (k + 1)
        parts = []
        for (a, b), arr, off in src:
            s, e = max(a, lo), min(b, hi)
            if s < e:
                parts.append(arr[:, off + s - a:off + e - a])
        blocks.append(jnp.concatenate(parts, axis=1))
    return jnp.stack(blocks)


def _rope_tables(lp):
    inv = 1.0 / (ROPE_THETA ** (jnp.arange(0, QK_ROPE, 2, dtype=F32) / QK_ROPE))
    ang = jnp.arange(lp, dtype=F32)[:, None] * inv[None, :]
    cos, sin = jnp.cos(ang), jnp.sin(ang)
    one = jnp.ones((lp, QK_NOPE), F32)
    zero = jnp.zeros((lp, QK_NOPE), F32)
    z16 = jnp.zeros((lp, 16), F32)
    c = jnp.concatenate([one, cos, cos, jnp.ones((lp, 32), F32)], axis=1)
    s1 = jnp.concatenate([zero, z16, sin, jnp.zeros((lp, 32), F32)], axis=1)
    s2 = jnp.concatenate([zero, -sin, z16, jnp.zeros((lp, 32), F32)], axis=1)
    return c, s1, s2


def kernel(x, meta_tokens, pre_norm_g, w_in, gate_bias, pool_w, pool_scale, w_out_pool, q_norm_g, w_uq, kv_norm_g, w_ukv, w_out_mla, conf_dw_w, conf_dw_b, conf_ln_g, conf_ln_b, w_out_conf, sc_dw_w, w_out_sc, w_o, post_norm_g, loss_target, m_meta_tokens, m_pre_norm_g, m_w_in, m_gate_bias, m_pool_w, m_pool_scale, m_w_out_pool, m_q_norm_g, m_w_uq, m_kv_norm_g, m_w_ukv, m_w_out_mla, m_conf_dw_w, m_conf_dw_b, m_conf_ln_g, m_conf_ln_b, m_w_out_conf, m_sc_dw_w, m_w_out_sc, m_w_o, m_post_norm_g, v_meta_tokens, v_pre_norm_g, v_w_in, v_gate_bias, v_pool_w, v_pool_scale, v_w_out_pool, v_q_norm_g, v_w_uq, v_kv_norm_g, v_w_ukv, v_w_out_mla, v_conf_dw_w, v_conf_dw_b, v_conf_ln_g, v_conf_ln_b, v_w_out_conf, v_sc_dw_w, v_w_out_sc, v_w_o, v_post_norm_g):
    names = ["w_in", "w_uq", "w_o"] + [n for n, _ in C128] + [n for n, _ in TAIL] + [n for n, _ in SMALL]
    loc = locals()
    w = {n: loc[n] for n in names}
    mom = {n: loc["m_" + n] for n in names}
    vel = {n: loc["v_" + n] for n in names}

    seq = x.shape[1]
    n_real = N_META + seq
    lp = -(-n_real // RB) * RB
    tmb = lp // 3
    tabs = _rope_tables(lp)

    c128 = jnp.concatenate([w[n] for n, _ in C128], axis=1)
    def shards_of(i):
        return [w_in[i].astype(BF16), c128[i].astype(BF16), w_uq[i].astype(BF16), w_o[i].astype(BF16)]

    gathered = [_all_gather(shards_of(0), "gather_weights")] + [None] * (DEPTH - 1)
    tail_w = _pack_tail(w)
    tail = _unpack_tail_full(_all_gather([tail_w], "gather_tail")[0])
    eye4 = jnp.eye(4, dtype=F32)
    bd_all = (pool_w[:, :, :, None, :] * eye4[None, :, None, :, None]).reshape(DEPTH, 256, 256).astype(BF16)

    def layer_weights(i):
        g_in, g_c128, g_uq, g_o = gathered[i]
        lw = {}
        lw["w_in"] = _pack_w_in(g_in)
        lw["wc"] = _cols_full(g_c128)
        wuq = _cols_full(g_uq).reshape(Q_RANK, HEADS, 96)
        lw["w_uq"] = jnp.pad(wuq, ((0, 0), (0, 0), (0, 32))).reshape(Q_RANK, HEADS * LANE)
        wukv = lw["wc"][U_OFF[4]:].reshape(KV_RANK, HEADS, 128)
        wk = jnp.pad(wukv[:, :, :QK_NOPE], ((0, 0), (0, 0), (0, 64))).reshape(KV_RANK, HEADS * LANE)
        lw["w_ukv"] = jnp.concatenate([wk, wukv[:, :, QK_NOPE:].reshape(KV_RANK, HEADS * V_DIM)], axis=1)
        lw["w_o"] = g_o.reshape(D, D)
        lw["bd"] = bd_all[i]
        lw["cw"] = jnp.pad(tail["conf_dw_w"][i], ((0, 1), (0, 0)))
        lw["sw"] = jnp.pad(tail["sc_dw_w"][i], ((0, 8 - SC_K), (0, 0)))
        return lw

    meta_full = tail["meta_tokens"]

    pad_rows = lp - n_real
    xr = jnp.concatenate([meta_full, x[0], jnp.zeros((pad_rows, D), F32)], axis=0)
    tgt = jnp.pad(loss_target[0], ((N_META, pad_rows), (0, 0)))
    saved = []
    for i in range(DEPTH):
        lw = layer_weights(i)
        h = _rms_fwd(xr, pre_norm_g[i:i + 1], lp)
        z = _mm(h, lw["w_in"], lp, ZW, D, tm=RB, tn=ZW // 2, tk=D, n_outer=True, name="mm_in")
        qn, kvn, qt, kt, vt = _mla_up(z, q_norm_g[i:i + 1], kv_norm_g[i:i + 1], lw["w_uq"], lw["w_ukv"], tabs, lp)
        res = _attn_fwd(qt, kt, vt, lp, gather=shards_of(i + 1) if i + 1 < DEPTH else ())
        oat, lse = res[0], res[1]
        if i + 1 < DEPTH:
            gathered[i + 1] = res[2:]
        u = _mix_fwd(z, oat, lw["bd"], pool_scale[i:i + 1], lw["cw"], conf_dw_b[i:i + 1], conf_ln_g[i:i + 1],
                     conf_ln_b[i:i + 1], lw["sw"], lp)
        x_new, m_act, o2 = _merge_fwd(xr, u, z, gate_bias[i:i + 1], lw["wc"], lw["w_o"],
                                      post_norm_g[i:i + 1], lp)
        saved.append(dict(lw=lw, x=xr, h=h, z=z, qn=qn, kvn=kvn, qt=qt, kt=kt, vt=vt, oat=oat, lse=lse,
                          u=u, m=m_act, o2=o2))
        xr = x_new

    dx, loss_part = _loss_head(xr, tgt, n_real, lp)

    gsm = {n: [None] * DEPTH for n, _ in SMALL}
    g_cw = [None] * DEPTH
    g_sw = [None] * DEPTH
    recv = [lax.empty((N_DEV, DEPTH) + s, BF16) for s in ((D, 916), (C128_ROWS, 128), (Q_RANK, 96), (128, D))]
    pending = None
    for i in reversed(range(DEPTH)):
        s = saved[i]
        lw = s["lw"]
        do2, dgl, dyb, du, dgb, dgpost = _merge_bwd(dx, s["o2"], s["u"], s["z"], gate_bias[i:i + 1],
                                                    lw["wc"], lw["w_o"], post_norm_g[i:i + 1], lp)
        d_wo = _mm(s["m"], do2, D, D, lp, ta=True, tm=512, tn=D, tk=tmb, out_dtype=BF16, name="mm_dwo")
        d_wout = _dwout(s["u"], dyb, lp)
        dzx, doat, dbd, dcw, dsw, dsm = _mix_bwd(
            s["z"], s["oat"], du, lw["bd"], pool_scale[i:i + 1], lw["cw"], conf_dw_b[i:i + 1],
            conf_ln_g[i:i + 1], conf_ln_b[i:i + 1], lw["sw"], lp)
        res = _attn_bwd(s["qt"], s["kt"], s["vt"], s["oat"], doat, s["lse"], lp,
                        scatter=(pending, recv, i + 1) if pending else None)
        dqt, dkt, dvt = res[:3]
        if pending:
            recv = list(res[3:])
        dq_raw, dkv_raw, dzq, dqg, dkvg = _mla_up_bwd(s["z"], q_norm_g[i:i + 1], kv_norm_g[i:i + 1], lw["w_uq"],
                                                      lw["w_ukv"], dqt, dkt, dvt, tabs, lp)
        d_wuq = _mm(s["qn"], dq_raw, Q_RANK, 1024, lp, ta=True, tm=Q_RANK, tn=1024, tk=tmb, out_dtype=BF16,
                    name="mm_dwuq")
        d_wukv = _mm(s["kvn"], dkv_raw, KV_RANK, 1536, lp, ta=True, tm=KV_RANK, tn=1536, tk=tmb, out_dtype=BF16,
                     name="mm_dwukv")
        d_win = [_mm(s["h"], seg, D, seg.shape[1], lp, ta=True, tm=D, tn=tn, tk=tmb, out_dtype=BF16,
                     name="mm_dwin%d" % k) for k, (seg, tn) in enumerate(((dgl, 1024), (dzq, 512), (dzx, 1408)))]
        d_wuq_o = d_wuq.reshape(Q_RANK, HEADS, LANE)[:, :, :96].reshape(Q_RANK, HEADS * 96)
        d_wukv_o = jnp.concatenate([d_wukv[:, :1024].reshape(KV_RANK, HEADS, LANE)[:, :, :QK_NOPE],
                                    d_wukv[:, 1024:].reshape(KV_RANK, HEADS, V_DIM)], axis=2).reshape(KV_RANK, 1024)
        pending = [
            _w_in_grad_by_dest(*d_win),
            _cols_by_dest(jnp.concatenate([d_wout, d_wukv_o], axis=0), 128),
            _cols_by_dest(d_wuq_o, 96),
            d_wo.reshape(N_DEV, 128, D)]
        res = _dh(dgl, dzq, dzx, lw["w_in"], s["x"], pre_norm_g[i:i + 1], dx, lp,
                  scatter=(pending, recv, 0) if i == 0 else None)
        dx, dgpre = res[0], res[1]
        if i == 0:
            recv = list(res[2:])

        gsm["pre_norm_g"][i] = dgpre[0]
        gsm["gate_bias"][i] = dgb[0]
        gsm["pool_w"][i] = jnp.stack([dbd[64 * g:64 * (g + 1), 64 * g:64 * (g + 1)] for g in range(4)])
        gsm["pool_scale"][i] = dsm[0]
        gsm["conf_dw_b"][i] = dsm[1]
        gsm["conf_ln_g"][i] = dsm[2]
        gsm["conf_ln_b"][i] = dsm[3]
        gsm["q_norm_g"][i] = dqg[0]
        gsm["kv_norm_g"][i] = dkvg[0]
        gsm["post_norm_g"][i] = dgpost[0]
        g_cw[i] = dcw[:CONF_K]
        g_sw[i] = dsw[:SC_K]

    outs = [dict() for _ in range(4)]

    def put(n, res):
        for t, r in zip(outs, res):
            t[n] = r

    put("w_in", _reduce_adam(recv[0], w["w_in"], mom["w_in"], vel["w_in"], 256, "adam_w_in"))
    off = 0
    for n, rows in C128:
        put(n, _reduce_adam(recv[1], w[n], mom[n], vel[n], 128, "adam_" + n, row_off=off))
        off += rows
    put("w_uq", _reduce_adam(recv[2], w["w_uq"], mom["w_uq"], vel["w_uq"], Q_RANK, "adam_w_uq"))
    put("w_o", _reduce_adam(recv[3], w["w_o"], mom["w_o"], vel["w_o"], 128, "adam_w_o"))

    tail_g = {"meta_tokens": _cols_by_dest(dx[:N_META], 128),
              "conf_dw_w": jnp.moveaxis(jnp.stack(g_cw).reshape(DEPTH, CONF_K, N_DEV, 32), 2, 0),
              "sc_dw_w": jnp.moveaxis(jnp.stack(g_sw).reshape(DEPTH, SC_K, N_DEV, 32), 2, 0)}
    tail_bd = jnp.concatenate([tail_g[n].reshape(N_DEV, -1, LANE) for n, _ in TAIL]
                              + [jnp.zeros((N_DEV, TAIL_PAD - TAIL_ROWS, LANE), F32)], axis=1)
    tail_recv = _all_to_all([tail_bd], [lax.empty((N_DEV, TAIL_PAD, LANE), F32)], None, "scatter_tail")[0]
    tail_res = _reduce_adam_flat(tail_recv, tail_w, _pack_tail(mom), _pack_tail(vel), "adam_tail")

    small_g = {n: jnp.stack(gsm[n]) for n, _ in SMALL}
    loss_row = jnp.concatenate([loss_part[0:1, 0:1], jnp.zeros((1, LANE - 1), F32)], axis=1)
    zrow = jnp.zeros((1, LANE), F32)
    parts = _all_gather([_pack_small(small_g, loss_row)], "gather_small_grads")[0]
    small_res = _reduce_adam_flat(parts, _pack_small(w, zrow), _pack_small(mom, zrow), _pack_small(vel, zrow),
                                  "adam_small")
    loss = small_res[0][SMALL_ROWS, 0]
    for t, tf, sf in zip(outs, tail_res, small_res):
        t.update(_unpack_tail(tf))
        t.update(_unpack_small(sf))
    order = ["meta_tokens", "pre_norm_g", "w_in", "gate_bias", "pool_w", "pool_scale", "w_out_pool", "q_norm_g",
             "w_uq", "kv_norm_g", "w_ukv", "w_out_mla", "conf_dw_w", "conf_dw_b", "conf_ln_g", "conf_ln_b",
             "w_out_conf", "sc_dw_w", "w_out_sc", "w_o", "post_norm_g"]
    grad_x = dx[N_META:n_real][None]
    return (loss, grad_x, *[t[n] for t in outs for n in order])
```

```python
import jax
import jax.numpy as jnp
import numpy as np
from jax import lax
from jax.experimental import pallas as pl
from jax.experimental.pallas import tpu as pltpu

F32 = jnp.float32
BF16 = jnp.bfloat16

D = 1024
N_META = 16
DEPTH = 4
EPS = 1e-6
HEADS = 8
QK_NOPE = 64
QK_ROPE = 32
V_DIM = 64
Q_RANK = 256
KV_RANK = 128
ROPE_THETA = 10000.0
SCALE = (QK_NOPE + QK_ROPE) ** -0.5
CONF_K = 31
SC_K = 3
N_DEV = 8

ADAM_LR = 0.001
ADAM_B1 = 0.9
ADAM_B2 = 0.999
ADAM_EPS = 1e-08
ADAM_WD = 0.01
ADAM_STEP = 10

RB = 384
HB = 32
LANE = 128
VMEM_LIMIT = 56 * 1024 * 1024

GL0, CQ0, CKV0, KR0, PV0, PG0, MG0, CU0, CG0, SBB0, SBC0, SBX0, SG0, ZW = (
    0, 4096, 4352, 4480, 4608, 4864, 5120, 5632, 6144, 6400, 6656, 6912, 7168, 7424)
ZSEG = ((0, 4096), (4096, 512), (4608, 2816))
LOG2E = 1.4426950408889634
LN2 = 0.6931471805599453

MESH = pl.DeviceIdType.MESH


def _cp(**kw):
    return pltpu.CompilerParams(vmem_limit_bytes=VMEM_LIMIT, **kw)


def _sig(x):
    return jax.nn.sigmoid(x)


def _silu(x):
    return x * _sig(x)


def _dsilu(x):
    s = _sig(x)
    return s * (1.0 + x * (1.0 - s))


def _dn(x, k):
    return x if k == 0 else pltpu.roll(x, k, 0)


def _up(x, k):
    return x if k == 0 else pltpu.roll(x, x.shape[0] - k, 0)


def _rope(t, c, s1, s2):
    return t * c + pltpu.roll(t, 16, 1) * s1 + pltpu.roll(t, LANE - 16, 1) * s2


def _rope_t(g, c, s1, s2):
    return g * c + pltpu.roll(g * s1, LANE - 16, 1) + pltpu.roll(g * s2, 16, 1)


def _mm(a, b, m, n, k, *, ta=False, tb=False, out_dtype=F32, tm, tn, tk, name,
        a_moff=0, a_koff=0, b_noff=0, b_koff=0, c=None, n_outer=False):
    assert m % tm == 0 and n % tn == 0 and k % tk == 0, (name, m, n, k, tm, tn, tk)
    nk = k // tk
    dims = (((0,) if ta else (1,), (1,) if tb else (0,)), ((), ()))
    has_c = c is not None

    def body(a_ref, b_ref, *rest):
        c_ref = rest[0] if has_c else None
        o_ref = rest[1] if has_c else rest[0]
        scr = rest[2:] if has_c else rest[1:]
        part = lax.dot_general(a_ref[...].astype(BF16), b_ref[...].astype(BF16), dims,
                               preferred_element_type=F32)

        def finish(total):
            if has_c:
                total = total + c_ref[...]
            o_ref[...] = total.astype(out_dtype)

        if nk == 1:
            finish(part)
        else:
            acc = scr[0]
            kk = pl.program_id(2)

            @pl.when(kk == 0)
            def _():
                acc[...] = part

            @pl.when(kk > 0)
            def _():
                acc[...] += part

            @pl.when(kk == nk - 1)
            def _():
                finish(acc[...])

    def im(f):
        return (lambda g0, g1, q: f(g1, g0, q)) if n_outer else f

    if ta:
        a_spec = pl.BlockSpec((tk, tm), im(lambda i, j, q: (q + a_koff, i + a_moff)))
    else:
        a_spec = pl.BlockSpec((tm, tk), im(lambda i, j, q: (i + a_moff, q + a_koff)))
    if tb:
        b_spec = pl.BlockSpec((tn, tk), im(lambda i, j, q: (j + b_noff, q + b_koff)))
    else:
        b_spec = pl.BlockSpec((tk, tn), im(lambda i, j, q: (q + b_koff, j + b_noff)))
    o_spec = pl.BlockSpec((tm, tn), im(lambda i, j, q: (i, j)))
    grid = (n // tn, m // tm, nk) if n_outer else (m // tm, n // tn, nk)
    return pl.pallas_call(
        body, grid=grid, in_specs=[a_spec, b_spec] + ([o_spec] if has_c else []),
        out_specs=o_spec, out_shape=jax.ShapeDtypeStruct((m, n), out_dtype),
        scratch_shapes=[pltpu.VMEM((tm, tn), F32)] if nk > 1 else [],
        name=name, compiler_params=_cp())(*((a, b, c) if has_c else (a, b)))


def _rms_fwd(x, g, lp):
    def body(x_ref, g_ref, h_ref):
        xv = x_ref[...]
        r = lax.rsqrt(jnp.mean(xv * xv, axis=-1, keepdims=True) + EPS)
        h_ref[...] = (xv * r * g_ref[...]).astype(BF16)

    return pl.pallas_call(
        body, grid=(lp // RB,),
        in_specs=[pl.BlockSpec((RB, D), lambda i: (i, 0)), pl.BlockSpec((1, D), lambda i: (0, 0))],
        out_specs=pl.BlockSpec((RB, D), lambda i: (i, 0)),
        out_shape=jax.ShapeDtypeStruct((lp, D), BF16), name="rms_fwd", compiler_params=_cp())(x, g)


def _dwout(u, dyb, lp):
    tk = lp // 3

    def body(u_ref, dy_ref, o_ref, acc):
        kk = pl.program_id(0)
        for b in range(4):
            rows = slice(U_OFF[b], U_OFF[b + 1])
            part = lax.dot_general(u_ref[:, rows], dy_ref[:, D * b:D * (b + 1)], (((0,), (0,)), ((), ())),
                                   preferred_element_type=F32)

            @pl.when(kk == 0)
            def _(part=part, rows=rows):
                acc[rows, :] = part

            @pl.when(kk > 0)
            def _(part=part, rows=rows):
                acc[rows, :] += part

        @pl.when(kk == 2)
        def _():
            o_ref[...] = acc[...].astype(BF16)

    return pl.pallas_call(
        body, grid=(3,),
        in_specs=[pl.BlockSpec((tk, U_OFF[4]), lambda q: (q, 0)), pl.BlockSpec((tk, 4 * D), lambda q: (q, 0))],
        out_specs=pl.BlockSpec((U_OFF[4], D), lambda q: (0, 0)),
        out_shape=jax.ShapeDtypeStruct((U_OFF[4], D), BF16),
        scratch_shapes=[pltpu.VMEM((U_OFF[4], D), F32)],
        name="dwout", compiler_params=_cp())(u, dyb)


def _mla_up(z, qg, kvg, w_uq, w_ukv, tabs, lp):
    def body(z_ref, qg_ref, kvg_ref, wuq_ref, wukv_ref, c_ref, s1_ref, s2_ref,
             qn_ref, kvn_ref, qo_ref, ko_ref, vo_ref):
        c, s1, s2 = c_ref[...], s1_ref[...], s2_ref[...]
        cq = z_ref[:, 0:256]
        ckv = z_ref[:, 256:384]
        rq = lax.rsqrt(jnp.mean(cq * cq, axis=-1, keepdims=True) + EPS)
        rk = lax.rsqrt(jnp.mean(ckv * ckv, axis=-1, keepdims=True) + EPS)
        qn = (cq * rq * qg_ref[...]).astype(BF16)
        kvn = (ckv * rk * kvg_ref[...]).astype(BF16)
        qn_ref[...] = qn
        kvn_ref[...] = kvn
        kr = _rope(z_ref[:, 384:512], c, s1, s2)
        q_raw = jnp.dot(qn, wuq_ref[...], preferred_element_type=F32)
        kv_raw = jnp.dot(kvn, wukv_ref[...], preferred_element_type=F32)
        for h in range(HEADS):
            sl = slice(LANE * h, LANE * (h + 1))
            qo_ref[:, sl] = (_rope(q_raw[:, sl], c, s1, s2) * (SCALE * LOG2E)).astype(BF16)
            ko_ref[:, sl] = (kv_raw[:, sl] + kr).astype(BF16)
        vo_ref[...] = kv_raw[:, 1024:1536].astype(BF16)

    tab = pl.BlockSpec((RB, LANE), lambda i: (i, 0))
    wide = pl.BlockSpec((RB, 1024), lambda i: (i, 0))
    full = lambda r, c: pl.BlockSpec((r, c), lambda i: (0, 0))
    return pl.pallas_call(
        body, grid=(lp // RB,),
        in_specs=[pl.BlockSpec((RB, 512), lambda i: (i, CQ0 // 512)), full(1, 256), full(1, 128),
                  full(Q_RANK, 1024), full(KV_RANK, 1536), tab, tab, tab],
        out_specs=[pl.BlockSpec((RB, 256), lambda i: (i, 0)), tab, wide, wide,
                   pl.BlockSpec((RB, 512), lambda i: (i, 0))],
        out_shape=[jax.ShapeDtypeStruct((lp, 256), BF16), jax.ShapeDtypeStruct((lp, 128), BF16),
                   jax.ShapeDtypeStruct((lp, 1024), BF16), jax.ShapeDtypeStruct((lp, 1024), BF16),
                   jax.ShapeDtypeStruct((lp, 512), BF16)],
        name="mla_up", compiler_params=_cp())(z, qg, kvg, w_uq, w_ukv, *tabs)


def _mla_up_bwd(z, qg, kvg, w_uq, w_ukv, qn, kvn, dq, dk, dv, tabs, lp):
    def body(z_ref, qg_ref, kvg_ref, wuq_ref, wukv_ref, qn_ref, kvn_ref, dq_ref, dk_ref, dv_ref,
             c_ref, s1_ref, s2_ref, dz_ref, dqg_ref, dkvg_ref, dwuq_ref, dwukv_ref, dqr_ref, dkv_ref):
        i = pl.program_id(0)
        c, s1, s2 = c_ref[...], s1_ref[...], s2_ref[...]
        lane = lax.broadcasted_iota(jnp.int32, (1, LANE), 1)
        ropel = (lane >= QK_NOPE) & (lane < QK_NOPE + QK_ROPE)
        ksum = jnp.zeros((RB, LANE), F32)
        for h in range(HEADS):
            sl = slice(LANE * h, LANE * (h + 1))
            dqr_ref[:, sl] = _rope_t(dq_ref[:, sl].astype(F32) * SCALE, c, s1, s2).astype(BF16)
            dkt = dk_ref[:, sl]
            dkv_ref[:, sl] = dkt
            ksum = ksum + dkt.astype(F32)
        dkv_ref[:, 1024:1536] = dv_ref[...]
        dkr = jnp.where(ropel, _rope_t(jnp.where(ropel, ksum, 0.0), c, s1, s2), 0.0)
        dqn = lax.dot_general(dqr_ref[...], wuq_ref[...], (((1,), (1,)), ((), ())), preferred_element_type=F32)
        dkvn = lax.dot_general(dkv_ref[...], wukv_ref[...], (((1,), (1,)), ((), ())), preferred_element_type=F32)

        def rms_b(xv, g, dy):
            r = lax.rsqrt(jnp.mean(xv * xv, axis=-1, keepdims=True) + EPS)
            a = dy * g
            dx = r * a - xv * (r * r * r) * jnp.mean(a * xv, axis=-1, keepdims=True)
            return dx, jnp.sum(dy * xv * r, axis=0, keepdims=True)

        dcq, pq = rms_b(z_ref[:, 0:256], qg_ref[...], dqn)
        dckv, pk = rms_b(z_ref[:, 256:384], kvg_ref[...], dkvn)
        dz_ref[:, 0:256] = dcq.astype(BF16)
        dz_ref[:, 256:384] = dckv.astype(BF16)
        dz_ref[:, 384:512] = dkr.astype(BF16)

        rows0 = (((0,), (0,)), ((), ()))
        pwq = lax.dot_general(qn_ref[...], dqr_ref[...], rows0, preferred_element_type=F32)
        pwk = lax.dot_general(kvn_ref[...], dkv_ref[...], rows0, preferred_element_type=F32)

        @pl.when(i == 0)
        def _():
            dqg_ref[...] = pq
            dkvg_ref[...] = pk
            dwuq_ref[...] = pwq
            dwukv_ref[...] = pwk

        @pl.when(i > 0)
        def _():
            dqg_ref[...] += pq
            dkvg_ref[...] += pk
            dwuq_ref[...] += pwq
            dwukv_ref[...] += pwk

    tab = pl.BlockSpec((RB, LANE), lambda i: (i, 0))
    wide = pl.BlockSpec((RB, 1024), lambda i: (i, 0))
    half = pl.BlockSpec((RB, 512), lambda i: (i, 0))
    full = lambda r, c: pl.BlockSpec((r, c), lambda i: (0, 0))
    return pl.pallas_call(
        body, grid=(lp // RB,),
        in_specs=[pl.BlockSpec((RB, 512), lambda i: (i, CQ0 // 512)), full(1, 256), full(1, 128),
                  full(Q_RANK, 1024), full(KV_RANK, 1536), pl.BlockSpec((RB, Q_RANK), lambda i: (i, 0)), tab,
                  wide, wide, half, tab, tab, tab],
        out_specs=[half, full(1, 256), full(1, 128), full(Q_RANK, 1024), full(KV_RANK, 1536)],
        out_shape=[jax.ShapeDtypeStruct((lp, 512), BF16), jax.ShapeDtypeStruct((1, 256), F32),
                   jax.ShapeDtypeStruct((1, 128), F32), jax.ShapeDtypeStruct((Q_RANK, 1024), F32),
                   jax.ShapeDtypeStruct((KV_RANK, 1536), F32)],
        scratch_shapes=[pltpu.VMEM((RB, 1024), BF16), pltpu.VMEM((RB, 1536), BF16)],
        name="mla_up_bwd", compiler_params=_cp())(z, qg, kvg, w_uq, w_ukv, qn, kvn, dq, dk, dv, *tabs)


def _head_lanes(e):
    lane = lax.broadcasted_iota(jnp.int32, (1, LANE), 1)
    return lane >= V_DIM if e else lane < V_DIM


ONE_LANE = (V_DIM, 0)


def _attn_fwd(q, k, v, lp, gather=()):
    nq = lp // RB
    n = len(gather)
    steps = HEADS // 2

    def body(q_ref, k_ref, v_ref, *rest):
        o_ref, lse_ref = rest[n], rest[n + 1]
        vm_scr = rest[2 * n + 2]
        if n:
            g_start, g_forward, g_finish = _gather_phases(rest[:n], rest[n + 2:2 * n + 2], *rest[2 * n + 3:])
            pl.when(pl.program_id(0) == 0)(g_start)
            pl.when(pl.program_id(0) == steps - 1)(g_forward)
        lane = lax.broadcasted_iota(jnp.int32, (1, LANE), 1)
        vv = v_ref[...]
        for e in range(2):
            ones = jnp.where(lane == ONE_LANE[e], 1.0, 0.0).astype(BF16)
            vm_scr[e] = jnp.where(_head_lanes(e), vv, jnp.broadcast_to(ones, vv.shape))
        causal = (lax.broadcasted_iota(jnp.int32, (RB, RB), 1) <= lax.broadcasted_iota(jnp.int32, (RB, RB), 0))

        def qblock(i, _):
            rows = pl.ds(pl.multiple_of(i * RB, RB), RB)
            qs = [q_ref[rows, LANE * e:LANE * (e + 1)] for e in range(2)]

            def scores(j):
                cols = pl.ds(pl.multiple_of(j * RB, RB), RB)
                return tuple(lax.dot_general(qs[e], k_ref[cols, LANE * e:LANE * (e + 1)], (((1,), (1,)), ((), ())),
                                             preferred_element_type=F32) for e in range(2))

            def update(j, s, carry, masked):
                cols = pl.ds(pl.multiple_of(j * RB, RB), RB)
                out = []
                for e in range(2):
                    m, acc = carry[2 * e], carry[2 * e + 1]
                    se = jnp.where(causal, s[e], -jnp.inf) if masked else s[e]
                    m_new = jnp.maximum(m, jnp.max(se, axis=-1, keepdims=True))
                    p = jnp.exp2((se - m_new).astype(BF16))
                    acc = jnp.exp2(m - m_new) * acc + jnp.dot(p, vm_scr[e, cols, :], preferred_element_type=F32)
                    out += [m_new, acc]
                return tuple(out)

            def pair(jj, c):
                sa, sb = scores(2 * jj), scores(2 * jj + 1)
                return update(2 * jj + 1, sb, update(2 * jj, sa, c, False), False)

            m0 = jnp.full((RB, 1), -jnp.inf, F32)
            a0 = jnp.zeros((RB, LANE), F32)
            carry = lax.fori_loop(0, i // 2, pair, (m0, a0, m0, a0))

            def last_two(c):
                sa, sb = scores(i - 1), scores(i)
                return update(i, sb, update(i - 1, sa, c, False), True)

            carry = lax.cond(i % 2 == 1, last_two, lambda c: update(i, scores(i), c, True), carry)
            o, lse = [], []
            for e in range(2):
                m, acc = carry[2 * e], carry[2 * e + 1]
                l = acc[:, ONE_LANE[e]:ONE_LANE[e] + 1]
                o.append(acc / l)
                lse.append(jnp.broadcast_to(m + jnp.log2(l), (RB, LANE)))
            o_ref[rows, :] = jnp.where(_head_lanes(0), o[0], o[1])
            lse_ref[rows, :] = jnp.where(_head_lanes(0), lse[0], lse[1])
            return 0

        lax.fori_loop(0, nq, qblock, 0)
        if n:
            pl.when(pl.program_id(0) == steps - 1)(g_finish)

    two = pl.BlockSpec((lp, 2 * LANE), lambda h: (0, h))
    one = pl.BlockSpec((lp, LANE), lambda h: (0, h))
    anyspec = pl.BlockSpec(memory_space=pl.ANY)
    return pl.pallas_call(
        body, grid=(steps,), in_specs=[two, two, one] + [anyspec] * n, out_specs=[one, one] + [anyspec] * n,
        out_shape=[jax.ShapeDtypeStruct((lp, 512), F32), jax.ShapeDtypeStruct((lp, 512), F32)]
        + _gather_shapes(gather),
        scratch_shapes=[pltpu.VMEM((2, lp, LANE), BF16)] + (_comm_sems(n) if n else []),
        name="attn_fwd_gather" if n else "attn_fwd", compiler_params=_cp())(q, k, v, *gather)


def _attn_bwd(q, k, v, o, do, lse, lp, scatter=None):
    nq = lp // RB
    xs, bufs, layer = scatter if scatter else ((), (), None)
    n = len(xs)
    steps = HEADS // 2

    def body(q_ref, k_ref, v_ref, o_ref, do_ref, lse_ref, *rest):
        dq_ref, dk_ref, dv_ref = rest[2 * n:2 * n + 3]
        vm_scr, dom_scr, dl_scr, dq_scr = rest[3 * n + 3:3 * n + 7]
        if n:
            s_start, s_finish = _scatter_phases(rest[:n], rest[2 * n + 3:3 * n + 3], *rest[3 * n + 7:], layer)
            pl.when(pl.program_id(0) == 0)(s_start)
        causal = (lax.broadcasted_iota(jnp.int32, (RB, RB), 1) <= lax.broadcasted_iota(jnp.int32, (RB, RB), 0))
        vv = v_ref[...]
        for e in range(2):
            vm_scr[e] = jnp.where(_head_lanes(e), vv, jnp.zeros_like(vv))

        def prep(i, _):
            rows = pl.ds(pl.multiple_of(i * RB, RB), RB)
            prod = do_ref[rows, :] * o_ref[rows, :]
            dls = []
            for e in range(2):
                hm = _head_lanes(e)
                dom_scr[e, rows, :] = jnp.where(hm, do_ref[rows, :], 0.0).astype(BF16)
                dls.append(jnp.sum(jnp.where(hm, prod, 0.0), axis=-1, keepdims=True))
            dl_scr[rows, :] = jnp.where(_head_lanes(0), dls[0], dls[1])
            dq_scr[rows, :] = jnp.zeros((RB, 2 * LANE), F32)
            return 0

        lax.fori_loop(0, nq, prep, 0)

        def kvblock(j, _):
            cols = pl.ds(pl.multiple_of(j * RB, RB), RB)
            kbs = [k_ref[cols, LANE * e:LANE * (e + 1)] for e in range(2)]

            def products(i):
                rows = pl.ds(pl.multiple_of(i * RB, RB), RB)
                out = []
                for e in range(2):
                    out.append(lax.dot_general(q_ref[rows, LANE * e:LANE * (e + 1)], kbs[e],
                                               (((1,), (1,)), ((), ())), preferred_element_type=F32))
                    out.append(lax.dot_general(dom_scr[e, rows, :], vm_scr[e, cols, :],
                                               (((1,), (1,)), ((), ())), preferred_element_type=F32))
                return tuple(out)

            def update(i, sd, carry, masked):
                rows = pl.ds(pl.multiple_of(i * RB, RB), RB)
                out = []
                for e in range(2):
                    dk, dv = carry[2 * e], carry[2 * e + 1]
                    sl = slice(LANE * e, LANE * (e + 1))
                    col1 = slice(V_DIM * e, V_DIM * e + 1)
                    s, dp = sd[2 * e], sd[2 * e + 1]
                    if masked:
                        s = jnp.where(causal, s, -jnp.inf)
                    p = jnp.exp2((s - lse_ref[rows, col1]).astype(BF16))
                    dv = dv + lax.dot_general(p, dom_scr[e, rows, :], (((0,), (0,)), ((), ())),
                                              preferred_element_type=F32)
                    ds = p * (dp - dl_scr[rows, col1]).astype(BF16)
                    dk = dk + lax.dot_general(ds, q_ref[rows, sl], (((0,), (0,)), ((), ())),
                                              preferred_element_type=F32)
                    dq_scr[rows, sl] += jnp.dot(ds, kbs[e], preferred_element_type=F32)
                    out += [dk, dv]
                return tuple(out)

            def pair(t, c):
                i0 = nq - 2 - 2 * t
                pa, pb = products(i0), products(i0 + 1)
                return update(i0 + 1, pb, update(i0, pa, c, False), False)

            def first_two(c):
                pa, pb = products(j), products(j + 1)
                return update(j + 1, pb, update(j, pa, c, True), False)

            zero = jnp.zeros((RB, LANE), F32)
            below = nq - 1 - j
            carry = lax.fori_loop(0, below // 2, pair, (zero, zero, zero, zero))
            dk0, dv0, dk1, dv1 = lax.cond(below % 2 == 1, first_two,
                                          lambda c: update(j, products(j), c, True), carry)
            dk_ref[cols, 0:LANE] = (dk0 * LN2).astype(BF16)
            dk_ref[cols, LANE:2 * LANE] = (dk1 * LN2).astype(BF16)
            dv_ref[cols, :] = (dv0 + dv1).astype(BF16)
            return 0

        lax.fori_loop(0, nq, kvblock, 0)

        def fin(i, _):
            rows = pl.ds(pl.multiple_of(i * RB, RB), RB)
            dq_ref[rows, :] = dq_scr[rows, :].astype(BF16)
            return 0

        lax.fori_loop(0, nq, fin, 0)
        if n:
            pl.when(pl.program_id(0) == steps - 1)(s_finish)

    two = pl.BlockSpec((lp, 2 * LANE), lambda h: (0, h))
    one = pl.BlockSpec((lp, LANE), lambda h: (0, h))
    anyspec = pl.BlockSpec(memory_space=pl.ANY)
    return pl.pallas_call(
        body, grid=(steps,), in_specs=[two, two, one, one, one, one] + [anyspec] * (2 * n),
        out_specs=[two, two, one] + [anyspec] * n,
        out_shape=[jax.ShapeDtypeStruct((lp, 1024), BF16), jax.ShapeDtypeStruct((lp, 1024), BF16),
                   jax.ShapeDtypeStruct((lp, 512), BF16)] + [jax.ShapeDtypeStruct(b.shape, b.dtype) for b in bufs],
        input_output_aliases={6 + n + a: 3 + a for a in range(n)},
        scratch_shapes=[pltpu.VMEM((2, lp, LANE), BF16), pltpu.VMEM((2, lp, LANE), BF16),
                        pltpu.VMEM((lp, LANE), F32), pltpu.VMEM((lp, 2 * LANE), F32)]
        + (_comm_sems(n) if n else []),
        name="attn_bwd_scatter" if n else "attn_bwd", compiler_params=_cp())(q, k, v, o, do, lse, *xs, *bufs)


def _pool_lane_windows():
    lane = lax.broadcasted_iota(jnp.int32, (1, 256), 1)
    return jnp.where(lane < 64, 2, jnp.where(lane < 128, 4, jnp.where(lane < 192, 8, 16)))


def _by_window(wl, s2, s4, s8, s16):
    return jnp.where(wl == 2, s2, jnp.where(wl == 4, s4, jnp.where(wl == 8, s8, s16)))


def _pool_fwd_rows(pv_ext, t0):
    n = pv_ext.shape[0]
    wl = _pool_lane_windows()
    s2 = pv_ext + _dn(pv_ext, 1)
    s4 = s2 + _dn(s2, 2)
    s8 = s4 + _dn(s4, 4)
    s16 = s8 + _dn(s8, 8)
    t = t0 + lax.broadcasted_iota(jnp.int32, (n, 1), 0)
    cnt = jnp.maximum(jnp.minimum(t + 1, wl), 1).astype(F32)
    return _by_window(wl, s2, s4, s8, s16) / cnt - pv_ext


def _conv_dn(x_ext, w_ref, taps):
    acc = w_ref[taps - 1:taps, :] * x_ext
    for j in range(1, taps):
        acc = acc + w_ref[taps - 1 - j:taps - j, :] * _dn(x_ext, j)
    return acc


def _conv_up(g_ext, w_ref, taps):
    acc = w_ref[taps - 1:taps, :] * g_ext
    for j in range(1, taps):
        acc = acc + w_ref[taps - 1 - j:taps - j, :] * _up(g_ext, j)
    return acc


def _ln_fwd(c, g, b):
    mu = jnp.mean(c, axis=-1, keepdims=True)
    xc = c - mu
    r = lax.rsqrt(jnp.mean(xc * xc, axis=-1, keepdims=True) + EPS)
    xh = xc * r
    return xh * g + b, xh, r


def _halo_specs(lp, width, col):
    per = RB // HB
    last = lp // HB - 1
    cur = pl.BlockSpec((RB, width), lambda i: (i, col))
    prev = pl.BlockSpec((HB, width), lambda i: (jnp.maximum(i * per - 1, 0), col))
    nxt = pl.BlockSpec((HB, width), lambda i: (jnp.minimum((i + 1) * per, last), col))
    return cur, prev, nxt


def _mix_fwd(z, oat, bd, pscale, cw, cb, lng, lnb, sw, lp):
    def body(za, zah, mg, oat_ref, cu, cuh, cg, sbb, sbc, sbch, sbx, sbxh, sg,
             bd_ref, ps_ref, cw_ref, cb_ref, lng_ref, lnb_ref, sw_ref, u_ref):
        i = pl.program_id(0)
        pm = jnp.where(i > 0, 1.0, 0.0).astype(F32)
        pv = jnp.concatenate([zah[:, 0:256] * pm, za[:, 0:256]], axis=0)
        p = _pool_fwd_rows(pv, i * RB - HB)[HB:]
        y = jnp.dot(p.astype(BF16), bd_ref[...], preferred_element_type=F32)
        u_ref[:, 0:256] = (y * ps_ref[...] * _silu(za[:, 256:512])).astype(BF16)
        u_ref[:, 256:768] = (oat_ref[...] * _silu(mg[...])).astype(BF16)
        ce = jnp.concatenate([cuh[...] * pm, cu[...]], axis=0)
        glu = ce[:, 0:256] * _sig(ce[:, 256:512])
        c = _conv_dn(glu, cw_ref, CONF_K)[HB:] + cb_ref[...]
        n, _, _ = _ln_fwd(c, lng_ref[...], lnb_ref[...])
        u_ref[:, 768:1024] = (_silu(n) * _silu(cg[...])).astype(BF16)
        qe = jnp.concatenate([sbch[...] * sbxh[...] * pm, sbc[...] * sbx[...]], axis=0)
        cv = _conv_dn(qe, sw_ref, SC_K)[HB:]
        u_ref[:, 1024:1280] = (sbb[...] * cv * _silu(sg[...])).astype(BF16)

    a_cur, a_prev, _ = _halo_specs(lp, 512, PV0 // 512)
    cu_cur, cu_prev, _ = _halo_specs(lp, 512, CU0 // 512)
    sc_cur, sc_prev, _ = _halo_specs(lp, 256, SBC0 // 256)
    sx_cur, sx_prev, _ = _halo_specs(lp, 256, SBX0 // 256)
    c256 = lambda c0: pl.BlockSpec((RB, 256), lambda i: (i, c0 // 256))
    full = lambda r, c: pl.BlockSpec((r, c), lambda i: (0, 0))
    return pl.pallas_call(
        body, grid=(lp // RB,),
        in_specs=[a_cur, a_prev, pl.BlockSpec((RB, 512), lambda i: (i, MG0 // 512)),
                  pl.BlockSpec((RB, 512), lambda i: (i, 0)),
                  cu_cur, cu_prev, c256(CG0), c256(SBB0), sc_cur, sc_prev, sx_cur, sx_prev, c256(SG0),
                  full(256, 256), full(1, 256), full(32, 256), full(1, 256), full(1, 256), full(1, 256),
                  full(8, 256)],
        out_specs=pl.BlockSpec((RB, 1280), lambda i: (i, 0)),
        out_shape=jax.ShapeDtypeStruct((lp, 1280), BF16),
        name="mix_fwd", compiler_params=_cp())(z, z, z, oat, z, z, z, z, z, z, z, z, z,
                                               bd, pscale, cw, cb, lng, lnb, sw)


def _mix_bwd(z, oat, du, bd, pscale, cw, cb, lng, lnb, sw, lp):
    nb = lp // RB
    ne = RB + 2 * HB
    nf = RB + HB

    def body(za, zah, zan, mg, oat_ref, cu, cuh, cun, cg, cgn, sbb, sbbn, sbc, sbch, sbcn, sbx, sbxh, sbxn,
             sg, sgn, du_ref, dun_ref, bd_ref, ps_ref, cw_ref, cb_ref, lng_ref, lnb_ref, sw_ref,
             dzx_ref, doat_ref, dbd_ref, dcw_ref, dsw_ref, dsm_ref):
        xa, xm, xc = 0, MG0 - PV0, CU0 - PV0
        i = pl.program_id(0)
        pm = jnp.where(i > 0, 1.0, 0.0).astype(F32)
        nm = jnp.where(i < nb - 1, 1.0, 0.0).astype(F32)

        def ext(cur, prev, nxt, sl=slice(None)):
            return jnp.concatenate([prev[:, sl] * pm, cur[:, sl], nxt[:, sl] * nm], axis=0)

        def fwd(cur, nxt, sl=slice(None)):
            return jnp.concatenate([cur[:, sl], nxt[:, sl] * nm], axis=0)

        def csum(x):
            return jnp.sum(x, axis=0, keepdims=True)

        @pl.when(i == 0)
        def _():
            dbd_ref[...] = jnp.zeros((256, 256), F32)
            dcw_ref[...] = jnp.zeros((32, 256), F32)
            dsw_ref[...] = jnp.zeros((8, 256), F32)
            dsm_ref[...] = jnp.zeros((8, 256), F32)

        a_cols, b_cols = slice(0, 256), slice(256, 512)
        pv_e = ext(za, zah, zan, a_cols)
        p = _pool_fwd_rows(pv_e, i * RB - HB)[HB:HB + RB]
        pb = p.astype(BF16)
        y = jnp.dot(pb, bd_ref[...], preferred_element_type=F32)
        pg_f = fwd(za, zan, b_cols)
        dua_f = fwd(du_ref, dun_ref, slice(0, 256))
        dyp_f = dua_f * ps_ref[...] * _silu(pg_f)
        dypb = dyp_f.astype(BF16)
        dp_f = lax.dot_general(dypb, bd_ref[...], (((1,), (1,)), ((), ())), preferred_element_type=F32)
        wl = _pool_lane_windows()
        t = i * RB + lax.broadcasted_iota(jnp.int32, (nf, 1), 0)
        cnt = jnp.minimum(t + 1, wl).astype(F32)
        qf = dp_f / cnt
        f2 = qf + _up(qf, 1)
        f4 = f2 + _up(f2, 2)
        f8 = f4 + _up(f4, 4)
        f16 = f8 + _up(f8, 8)
        dpv = (_by_window(wl, f2, f4, f8, f16) - dp_f)[0:RB]
        dua = du_ref[:, 0:256]
        pg = za[:, b_cols]
        dpg = dua * y * ps_ref[...] * _dsilu(pg)
        dzx_ref[:, xa:xa + 256] = dpv.astype(BF16)
        dzx_ref[:, xa + 256:xa + 512] = dpg.astype(BF16)
        d_scale = csum(dua * y * _silu(pg))
        d_bd = lax.dot_general(pb, dypb[0:RB], (((0,), (0,)), ((), ())), preferred_element_type=F32)

        dub = du_ref[:, 256:768]
        mgv = mg[...]
        dzx_ref[:, xm:xm + 512] = (dub * oat_ref[...] * _dsilu(mgv)).astype(BF16)
        doat_ref[...] = dub * _silu(mgv)

        a_e = ext(cu, cuh, cun, slice(0, 256))
        gt_e = ext(cu, cuh, cun, slice(256, 512))
        sg_e = _sig(gt_e)
        glu_e = a_e * sg_e
        c_f = _conv_dn(glu_e, cw_ref, CONF_K)[HB:] + cb_ref[...]
        n_f, xh_f, r_f = _ln_fwd(c_f, lng_ref[...], lnb_ref[...])
        cg_f = fwd(cg, cgn)
        duc_f = fwd(du_ref, dun_ref, slice(768, 1024))
        sw_f = _silu(n_f)
        dcg = (duc_f * sw_f * _dsilu(cg_f))[0:RB]
        dn_f = duc_f * _silu(cg_f) * _dsilu(n_f)
        a_f = dn_f * lng_ref[...]
        dc_f = r_f * (a_f - jnp.mean(a_f, axis=-1, keepdims=True)
                      - xh_f * jnp.mean(a_f * xh_f, axis=-1, keepdims=True))
        d_lng = csum((dn_f * xh_f)[0:RB])
        d_lnb = csum(dn_f[0:RB])
        d_cb = csum(dc_f[0:RB])
        dglu = _conv_up(dc_f, cw_ref, CONF_K)[0:RB]
        dc_c = dc_f[0:RB]
        for kk in range(CONF_K):
            j = CONF_K - 1 - kk
            dcw_ref[kk:kk + 1, :] += csum(dc_c * _dn(glu_e, j)[HB:HB + RB])

        sgc = sg_e[HB:HB + RB]
        a_c = a_e[HB:HB + RB]
        dzx_ref[:, xc:xc + 256] = (dglu * sgc).astype(BF16)
        dzx_ref[:, xc + 256:xc + 512] = (dglu * a_c * sgc * (1.0 - sgc)).astype(BF16)
        dzx_ref[:, xc + 512:xc + 768] = dcg.astype(BF16)

        c_e = ext(sbc, sbch, sbcn)
        x_e = ext(sbx, sbxh, sbxn)
        q_e = c_e * x_e
        cv_f = _conv_dn(q_e, sw_ref, SC_K)[HB:]
        bg_f = fwd(sbb, sbbn)
        sg_f = fwd(sg, sgn)
        dud_f = fwd(du_ref, dun_ref, slice(1024, 1280))
        ssg_f = _silu(sg_f)
        dcv_f = dud_f * bg_f * ssg_f
        dbg = (dud_f * cv_f * ssg_f)[0:RB]
        dsg = (dud_f * bg_f * cv_f * _dsilu(sg_f))[0:RB]
        dq = _conv_up(dcv_f, sw_ref, SC_K)[0:RB]
        dcv_c = dcv_f[0:RB]
        for kk in range(SC_K):
            j = SC_K - 1 - kk
            dsw_ref[kk:kk + 1, :] += csum(dcv_c * _dn(q_e, j)[HB:HB + RB])

        dzx_ref[:, xc + 768:xc + 1024] = dbg.astype(BF16)
        dzx_ref[:, xc + 1024:xc + 1280] = (dq * x_e[HB:HB + RB]).astype(BF16)
        dzx_ref[:, xc + 1280:xc + 1536] = (dq * c_e[HB:HB + RB]).astype(BF16)
        dzx_ref[:, xc + 1536:xc + 1792] = dsg.astype(BF16)

        dbd_ref[...] += d_bd
        dsm_ref[0:1, :] += d_scale
        dsm_ref[1:2, :] += d_cb
        dsm_ref[2:3, :] += d_lng
        dsm_ref[3:4, :] += d_lnb

    a3 = _halo_specs(lp, 512, PV0 // 512)
    cu3 = _halo_specs(lp, 512, CU0 // 512)
    cg3 = _halo_specs(lp, 256, CG0 // 256)
    sbb3 = _halo_specs(lp, 256, SBB0 // 256)
    sbc3 = _halo_specs(lp, 256, SBC0 // 256)
    sbx3 = _halo_specs(lp, 256, SBX0 // 256)
    sg3 = _halo_specs(lp, 256, SG0 // 256)
    du3 = _halo_specs(lp, 1280, 0)
    full = lambda r, c: pl.BlockSpec((r, c), lambda i: (0, 0))
    in_specs = [a3[0], a3[1], a3[2], pl.BlockSpec((RB, 512), lambda i: (i, MG0 // 512)),
                pl.BlockSpec((RB, 512), lambda i: (i, 0)),
                cu3[0], cu3[1], cu3[2], cg3[0], cg3[2], sbb3[0], sbb3[2],
                sbc3[0], sbc3[1], sbc3[2], sbx3[0], sbx3[1], sbx3[2], sg3[0], sg3[2],
                du3[0], du3[2],
                full(256, 256), full(1, 256), full(32, 256), full(1, 256), full(1, 256), full(1, 256),
                full(8, 256)]
    out_specs = [pl.BlockSpec((RB, ZW - PV0), lambda i: (i, 0)), pl.BlockSpec((RB, 512), lambda i: (i, 0)),
                 full(256, 256), full(32, 256), full(8, 256), full(8, 256)]
    out_shape = [jax.ShapeDtypeStruct((lp, ZW - PV0), BF16), jax.ShapeDtypeStruct((lp, 512), F32),
                 jax.ShapeDtypeStruct((256, 256), F32), jax.ShapeDtypeStruct((32, 256), F32),
                 jax.ShapeDtypeStruct((8, 256), F32), jax.ShapeDtypeStruct((8, 256), F32)]
    return pl.pallas_call(
        body, grid=(nb,), in_specs=in_specs, out_specs=out_specs, out_shape=out_shape,
        name="mix_bwd", compiler_params=_cp())(
            z, z, z, z, oat, z, z, z, z, z, z, z, z, z, z, z, z, z, z, z, du, du,
            bd, pscale, cw, cb, lng, lnb, sw)


U_OFF = (0, 256, 768, 1024, 1280)


def _merge_fwd(x, u, z, gb, wout, wo, gpost, lp):
    MRB = RB

    def body(x_ref, u_ref, gl_ref, gb_ref, wout_ref, wo_ref, g_ref, xo_ref, m_ref, o2_ref):
        m = jnp.zeros((MRB, D), F32)
        for b in range(4):
            y = jnp.dot(u_ref[:, U_OFF[b]:U_OFF[b + 1]], wout_ref[U_OFF[b]:U_OFF[b + 1], :],
                        preferred_element_type=F32)
            sl = slice(D * b, D * (b + 1))
            m = m + _sig(gl_ref[:, sl] + gb_ref[:, sl]) * y
        mb = m.astype(BF16)
        m_ref[...] = mb
        o2 = jnp.dot(mb, wo_ref[...], preferred_element_type=F32)
        o2_ref[...] = o2
        r = lax.rsqrt(jnp.mean(o2 * o2, axis=-1, keepdims=True) + EPS)
        xo_ref[...] = x_ref[...] + o2 * r * g_ref[...]

    blk = pl.BlockSpec((MRB, D), lambda i: (i, 0))
    full = lambda r, c: pl.BlockSpec((r, c), lambda i: (0, 0))
    return pl.pallas_call(
        body, grid=(lp // MRB,),
        in_specs=[blk, pl.BlockSpec((MRB, 1280), lambda i: (i, 0)), pl.BlockSpec((MRB, 4096), lambda i: (i, 0)),
                  full(1, 4096), full(1280, D), full(D, D), full(1, D)],
        out_specs=[blk, blk, blk],
        out_shape=[jax.ShapeDtypeStruct((lp, D), F32), jax.ShapeDtypeStruct((lp, D), BF16),
                   jax.ShapeDtypeStruct((lp, D), F32)],
        name="merge_fwd", compiler_params=_cp())(x, u, z, gb, wout, wo, gpost)


def _merge_bwd(dx, o2, u, z, gb, wout, wo, gpost, lp):
    MRB = RB

    def body(dx_ref, o2_ref, u_ref, gl_ref, gb_ref, wout_ref, wo_ref, g_ref,
             do2_ref, dgl_ref, dy_ref, du_ref, dgb_ref, dg_ref):
        i = pl.program_id(0)
        o2 = o2_ref[...]
        dy = dx_ref[...]
        r = lax.rsqrt(jnp.mean(o2 * o2, axis=-1, keepdims=True) + EPS)
        a = dy * g_ref[...]
        do2 = (r * a - o2 * (r * r * r) * jnp.mean(a * o2, axis=-1, keepdims=True)).astype(BF16)
        do2_ref[...] = do2
        dg = jnp.sum(dy * o2 * r, axis=0, keepdims=True)
        dm = lax.dot_general(do2, wo_ref[...], (((1,), (1,)), ((), ())), preferred_element_type=F32)
        for b in range(4):
            rows = slice(U_OFF[b], U_OFF[b + 1])
            y = jnp.dot(u_ref[:, rows], wout_ref[rows, :], preferred_element_type=F32)
            sl = slice(D * b, D * (b + 1))
            gt = _sig(gl_ref[:, sl] + gb_ref[:, sl])
            dgl = dm * y * gt * (1.0 - gt)
            dgl_ref[:, sl] = dgl.astype(BF16)
            part = jnp.sum(dgl, axis=0, keepdims=True)

            @pl.when(i == 0)
            def _(part=part, sl=sl):
                dgb_ref[:, sl] = part

            @pl.when(i > 0)
            def _(part=part, sl=sl):
                dgb_ref[:, sl] += part

            dyb = (dm * gt).astype(BF16)
            dy_ref[:, sl] = dyb
            du_ref[:, rows] = lax.dot_general(dyb, wout_ref[rows, :], (((1,), (1,)), ((), ())),
                                              preferred_element_type=F32)

        @pl.when(i == 0)
        def _():
            dg_ref[...] = dg

        @pl.when(i > 0)
        def _():
            dg_ref[...] += dg

    blk = pl.BlockSpec((MRB, D), lambda i: (i, 0))
    wide = pl.BlockSpec((MRB, 4096), lambda i: (i, 0))
    ub = pl.BlockSpec((MRB, 1280), lambda i: (i, 0))
    full = lambda r, c: pl.BlockSpec((r, c), lambda i: (0, 0))
    once = lambda r, c: pl.BlockSpec((r, c), lambda i: (0, 0), pipeline_mode=pl.Buffered(1))
    return pl.pallas_call(
        body, grid=(lp // MRB,),
        in_specs=[blk, blk, ub, wide, full(1, 4096), once(1280, D), once(D, D), full(1, D)],
        out_specs=[blk, wide, wide, ub, full(1, 4096), full(1, D)],
        out_shape=[jax.ShapeDtypeStruct((lp, D), BF16), jax.ShapeDtypeStruct((lp, 4096), BF16),
                   jax.ShapeDtypeStruct((lp, 4096), BF16), jax.ShapeDtypeStruct((lp, 1280), F32),
                   jax.ShapeDtypeStruct((1, 4096), F32), jax.ShapeDtypeStruct((1, D), F32)],
        name="merge_bwd", compiler_params=_cp())(dx, o2, u, z, gb, wout, wo, gpost)


def _dh(dgl, dzq, dzx, w_in, x, g, dx_in, lp, scatter=None):
    xs, bufs, layer = scatter if scatter else ((), (), None)
    n = len(xs)
    steps = lp // RB
    segs = ((0, CQ0), (CQ0, PV0), (PV0, ZW))

    def body(gl_ref, zq_ref, zx_ref, w_ref, x_ref, g_ref, dxi_ref, *rest):
        dx_ref, dg_ref = rest[2 * n], rest[2 * n + 1]
        i = pl.program_id(0)
        if n:
            s_start, s_finish = _scatter_phases(rest[:n], rest[2 * n + 2:3 * n + 2], *rest[3 * n + 2:], layer)
            pl.when(i == 0)(s_start)
        dh = None
        for a_ref, (lo, hi) in zip((gl_ref, zq_ref, zx_ref), segs):
            part = lax.dot_general(a_ref[...], w_ref[:, lo:hi], (((1,), (1,)), ((), ())),
                                   preferred_element_type=F32)
            dh = part if dh is None else dh + part
        xv = x_ref[...]
        r = lax.rsqrt(jnp.mean(xv * xv, axis=-1, keepdims=True) + EPS)
        a = dh * g_ref[...]
        dx_ref[...] = dxi_ref[...] + r * a - xv * (r * r * r) * jnp.mean(a * xv, axis=-1, keepdims=True)
        part = jnp.sum(dh * xv * r, axis=0, keepdims=True)

        @pl.when(i == 0)
        def _():
            dg_ref[...] = part

        @pl.when(i > 0)
        def _():
            dg_ref[...] += part

        if n:
            pl.when(i == steps - 1)(s_finish)

    anyspec = pl.BlockSpec(memory_space=pl.ANY)
    row = lambda w: pl.BlockSpec((RB, w), lambda i: (i, 0))
    vec = pl.BlockSpec((1, D), lambda i: (0, 0))
    resident = pl.BlockSpec((D, ZW), lambda i: (0, 0), pipeline_mode=pl.Buffered(1))
    return pl.pallas_call(
        body, grid=(steps,),
        in_specs=[row(CQ0), row(PV0 - CQ0), row(ZW - PV0), resident, row(D), vec, row(D)] + [anyspec] * (2 * n),
        out_specs=[row(D), vec] + [anyspec] * n,
        out_shape=[jax.ShapeDtypeStruct((lp, D), F32), jax.ShapeDtypeStruct((1, D), F32)]
        + [jax.ShapeDtypeStruct(b.shape, b.dtype) for b in bufs],
        input_output_aliases={7 + n + a: 2 + a for a in range(n)},
        scratch_shapes=_comm_sems(n) if n else [],
        name="dh_scatter" if n else "dh", compiler_params=_cp())(dgl, dzq, dzx, w_in, x, g, dx_in, *xs, *bufs)


def _loss_head(xf, tgt, n_real, lp):
    def body(x_ref, t_ref, dy_ref, ls_ref):
        i = pl.program_id(0)
        t = i * RB + lax.broadcasted_iota(jnp.int32, (RB, 1), 0)
        real = (t >= N_META) & (t < n_real)
        err = jnp.where(real, x_ref[...] - t_ref[...], 0.0)
        dy_ref[...] = err / D
        part = 0.5 * jnp.sum(jnp.mean(err * err, axis=-1, keepdims=True), axis=0, keepdims=True)
        part = jnp.broadcast_to(part, (8, LANE))

        @pl.when(i == 0)
        def _():
            ls_ref[...] = part

        @pl.when(i > 0)
        def _():
            ls_ref[...] += part

    blk = pl.BlockSpec((RB, D), lambda i: (i, 0))
    return pl.pallas_call(
        body, grid=(lp // RB,), in_specs=[blk, blk],
        out_specs=[blk, pl.BlockSpec((8, LANE), lambda i: (0, 0))],
        out_shape=[jax.ShapeDtypeStruct((lp, D), F32), jax.ShapeDtypeStruct((8, LANE), F32)],
        name="loss_head", compiler_params=_cp())(xf, tgt)


def _peer(d):
    x, y, c = lax.axis_index("x"), lax.axis_index("y"), lax.axis_index("c")
    return (x ^ ((d >> 2) & 1), y ^ ((d >> 1) & 1), c ^ (d & 1))


def _index_of(p):
    return 4 * p[0] + 2 * p[1] + p[2]


def _all_gather(xs, name):
    n = len(xs)

    def body(*refs):
        start, forward, finish = _gather_phases(refs[:n], refs[n:2 * n], *refs[2 * n:])
        start()
        forward()
        finish()

    anyspec = pl.BlockSpec(memory_space=pl.ANY)
    return pl.pallas_call(
        body, in_specs=[anyspec] * n, out_specs=[anyspec] * n,
        out_shape=_gather_shapes(xs), scratch_shapes=_comm_sems(n), name=name)(*xs)


def _gather_shapes(xs):
    return [jax.ShapeDtypeStruct((N_DEV,) + x.shape, x.dtype) for x in xs]


def _comm_sems(n):
    return [pltpu.SemaphoreType.DMA((7 * n,)), pltpu.SemaphoreType.DMA((7 * n,)), pltpu.SemaphoreType.DMA((n,))]


def _gather_phases(x_refs, out_refs, send_sems, recv_sems, local_sems):
    n = len(x_refs)
    chips = [2, 4, 6]

    def copy(a, kk, block, to, src=None):
        slot = out_refs[a].at[_index_of(block)]
        return pltpu.make_async_remote_copy(
            src_ref=slot if src is None else src, dst_ref=slot,
            send_sem=send_sems.at[7 * a + kk], recv_sem=recv_sems.at[7 * a + kk], device_id=to,
            device_id_type=MESH)

    def local(a):
        return pltpu.make_async_copy(x_refs[a], out_refs[a].at[_index_of(_peer(0))], local_sems.at[a])

    def firsts():
        out = []
        for a in range(n):
            out.append(copy(a, 0, _peer(0), _peer(1), src=x_refs[a]))
            out += [copy(a, 1 + j, _peer(0), _peer(d), src=x_refs[a]) for j, d in enumerate(chips)]
        return out

    def passes():
        return [copy(a, 4 + j, _peer(d), _peer(1)) for j, d in enumerate(chips) for a in range(n)]

    def start():
        for a in range(n):
            local(a).start()
        for cp in firsts():
            cp.start()

    def forward():
        for j, d in enumerate(chips):
            for a in range(n):
                copy(a, 1 + j, _peer(d), _peer(0)).wait_recv()
                copy(a, 4 + j, _peer(d), _peer(1)).start()

    def finish():
        for a in range(n):
            copy(a, 0, _peer(1), _peer(0)).wait_recv()
            for j, d in enumerate(chips):
                copy(a, 4 + j, _peer(d | 1), _peer(0)).wait_recv()
        for cp in firsts() + passes():
            cp.wait_send()
        for a in range(n):
            local(a).wait()

    return start, forward, finish


def _scatter_phases(x_refs, out_refs, send_sems, recv_sems, local_sems, layer):
    n = len(x_refs)

    def land(a, dev):
        slot = out_refs[a].at[dev]
        return slot if layer is None else slot.at[layer]

    def local(a):
        my = _index_of(_peer(0))
        return pltpu.make_async_copy(x_refs[a].at[my], land(a, my), local_sems.at[a])

    def copy(a, d):
        my = _index_of(_peer(0))
        return pltpu.make_async_remote_copy(
            src_ref=x_refs[a].at[_index_of(_peer(d))], dst_ref=land(a, my),
            send_sem=send_sems.at[7 * a + d - 1], recv_sem=recv_sems.at[7 * a + d - 1], device_id=_peer(d),
            device_id_type=MESH)

    def arrival(a, d):
        frm = _index_of(_peer(d))
        return pltpu.make_async_remote_copy(
            src_ref=x_refs[a].at[frm], dst_ref=land(a, frm),
            send_sem=send_sems.at[7 * a + d - 1], recv_sem=recv_sems.at[7 * a + d - 1], device_id=_peer(d),
            device_id_type=MESH)

    def start():
        for a in range(n):
            local(a).start()
        for d in range(1, N_DEV):
            for a in range(n):
                copy(a, d).start()

    def finish():
        for d in range(1, N_DEV):
            for a in range(n):
                arrival(a, d).wait_recv()
        for d in range(1, N_DEV):
            for a in range(n):
                copy(a, d).wait_send()
        for a in range(n):
            local(a).wait()

    return start, finish


def _all_to_all(xs, bufs, layer, name):
    n = len(xs)

    def body(*refs):
        start, finish = _scatter_phases(refs[:n], refs[2 * n:3 * n], *refs[3 * n:], layer)
        start()
        finish()

    anyspec = pl.BlockSpec(memory_space=pl.ANY)
    return pl.pallas_call(
        body, in_specs=[anyspec] * (2 * n), out_specs=[anyspec] * n,
        out_shape=[jax.ShapeDtypeStruct(b.shape, b.dtype) for b in bufs],
        input_output_aliases={n + a: a for a in range(n)},
        scratch_shapes=_comm_sems(n), name=name)(*xs, *bufs)


def _adam_math(g, w, m, v):
    c1 = 1.0 - ADAM_B1 ** ADAM_STEP
    c2 = 1.0 - ADAM_B2 ** ADAM_STEP
    mn = ADAM_B1 * m + (1.0 - ADAM_B1) * g
    vn = ADAM_B2 * v + (1.0 - ADAM_B2) * (g * g)
    return -ADAM_LR * ((mn / c1) / (jnp.sqrt(vn / c2) + ADAM_EPS) + ADAM_WD * w), mn, vn


def _reduce_adam(parts, w, m, v, rb, name, row_off=0):
    depth, rows, cols = w.shape
    assert rows % rb == 0 and row_off % rb == 0

    def body(p_ref, w_ref, m_ref, v_ref, g_ref, d_ref, mo_ref, vo_ref):
        g = p_ref[0, 0].astype(F32)
        for j in range(1, N_DEV):
            g = g + p_ref[j, 0].astype(F32)
        g_ref[0] = g
        d_ref[0], mo_ref[0], vo_ref[0] = _adam_math(g, w_ref[0], m_ref[0], v_ref[0])

    blk = pl.BlockSpec((1, rb, cols), lambda l, i: (l, i, 0))
    out = jax.ShapeDtypeStruct(w.shape, F32)
    return pl.pallas_call(
        body, grid=(depth, rows // rb),
        in_specs=[pl.BlockSpec((N_DEV, 1, rb, cols), lambda l, i: (0, l, i + row_off // rb, 0)), blk, blk, blk],
        out_specs=[blk, blk, blk, blk], out_shape=[out, out, out, out],
        name=name, compiler_params=_cp())(parts, w, m, v)


def _reduce_adam_flat(parts, w, m, v, name):
    q_rows = w.shape[0]

    def body(p_ref, w_ref, m_ref, v_ref, g_ref, d_ref, mo_ref, vo_ref):
        g = p_ref[0].astype(F32)
        for j in range(1, N_DEV):
            g = g + p_ref[j].astype(F32)
        g_ref[...] = g
        d_ref[...], mo_ref[...], vo_ref[...] = _adam_math(g, w_ref[...], m_ref[...], v_ref[...])

    blk = pl.BlockSpec((q_rows, LANE), lambda i: (0, 0))
    out = jax.ShapeDtypeStruct((q_rows, LANE), F32)
    return pl.pallas_call(
        body, grid=(1,), in_specs=[pl.BlockSpec((N_DEV, q_rows, LANE), lambda i: (0, 0, 0)), blk, blk, blk],
        out_specs=[blk, blk, blk, blk], out_shape=[out, out, out, out],
        name=name, compiler_params=_cp())(parts, w, m, v)


C128 = (("w_out_pool", 256), ("w_out_mla", 512), ("w_out_conf", 256), ("w_out_sc", 256), ("w_ukv", 128))
C128_ROWS = sum(r for _, r in C128)
TAIL = (("meta_tokens", (N_META, 128)), ("conf_dw_w", (DEPTH, CONF_K, 32)), ("sc_dw_w", (DEPTH, SC_K, 32)))
TAIL_ROWS = sum(int(np.prod(s)) for _, s in TAIL) // LANE
TAIL_PAD = 56
SMALL = (("pre_norm_g", (DEPTH, D)), ("gate_bias", (DEPTH, 4096)), ("pool_w", (DEPTH, 4, 64, 64)),
         ("pool_scale", (DEPTH, 256)), ("q_norm_g", (DEPTH, 256)), ("kv_norm_g", (DEPTH, 128)),
         ("conf_dw_b", (DEPTH, 256)), ("conf_ln_g", (DEPTH, 256)), ("conf_ln_b", (DEPTH, 256)),
         ("post_norm_g", (DEPTH, D)))
SMALL_ROWS = sum(int(np.prod(s)) for _, s in SMALL) // LANE
SMALL_PAD = -(-(SMALL_ROWS + 1) // 8) * 8


def _pack_tail(t):
    parts = [t[n].reshape(-1, LANE) for n, _ in TAIL]
    parts.append(jnp.zeros((TAIL_PAD - TAIL_ROWS, LANE), F32))
    return jnp.concatenate(parts, axis=0)


def _unpack_tail(flat):
    out, off = {}, 0
    for n, s in TAIL:
        rows = int(np.prod(s)) // LANE
        out[n] = flat[off:off + rows].reshape(s)
        off += rows
    return out


def _unpack_tail_full(g):
    out, off = {}, 0
    for n, s in TAIL:
        rows = int(np.prod(s)) // LANE
        blk = jnp.moveaxis(g[:, off:off + rows].reshape((N_DEV,) + s), 0, -2)
        out[n] = blk.reshape(s[:-1] + (N_DEV * s[-1],))
        off += rows
    return out


def _pack_small(t, extra_row):
    parts = [t[n].reshape(-1, LANE) for n, _ in SMALL] + [extra_row]
    parts.append(jnp.zeros((SMALL_PAD - SMALL_ROWS - 1, LANE), F32))
    return jnp.concatenate(parts, axis=0)


def _unpack_small(flat):
    out, off = {}, 0
    for n, s in SMALL:
        rows = int(np.prod(s)) // LANE
        out[n] = flat[off:off + rows].reshape(s)
        off += rows
    return out


def _cols_by_dest(g, width):
    r = g.shape[0]
    return g.reshape(r, N_DEV, width).transpose(1, 0, 2)


def _cols_full(gathered):
    _, r, c = gathered.shape
    return gathered.transpose(1, 0, 2).reshape(r, N_DEV * c)


W_IN_SHARD = 916
PACKED_SEGS = ((3232, 7328), (512, 896), 64, (896, 928), 32, (0, 512), (928, 3232))


def _pack_w_in(g):
    parts = []
    for seg in PACKED_SEGS:
        if isinstance(seg, int):
            parts.append(jnp.zeros((g.shape[1], seg), g.dtype))
            continue
        a, b = seg
        while a < b:
            k = a // W_IN_SHARD
            hi = min(b, W_IN_SHARD * (k + 1))
            parts.append(g[k, :, a - W_IN_SHARD * k:hi - W_IN_SHARD * k])
            a = hi
    return jnp.concatenate(parts, axis=1)


def _w_in_grad_by_dest(gl, mla, mix):
    src = (((0, 512), mix, 0), ((512, 896), mla, 0), ((896, 928), mla, 448), ((928, 3232), mix, 512),
           ((3232, 7328), gl, 0))
    blocks = []
    for k in range(N_DEV):
        lo, hi = W_IN_SHARD * k, W_IN_SHARD * (k + 1)
        parts = []
        for (a, b), arr, off in src:
            s, e = max(a, lo), min(b, hi)
            if s < e:
                parts.append(arr[:, off + s - a:off + e - a])
        blocks.append(jnp.concatenate(parts, axis=1))
    return jnp.stack(blocks)


def _rope_tables(lp):
    inv = 1.0 / (ROPE_THETA ** (jnp.arange(0, QK_ROPE, 2, dtype=F32) / QK_ROPE))
    ang = jnp.arange(lp, dtype=F32)[:, None] * inv[None, :]
    cos, sin = jnp.cos(ang), jnp.sin(ang)
    one = jnp.ones((lp, QK_NOPE), F32)
    zero = jnp.zeros((lp, QK_NOPE), F32)
    z16 = jnp.zeros((lp, 16), F32)
    c = jnp.concatenate([one, cos, cos, jnp.ones((lp, 32), F32)], axis=1)
    s1 = jnp.concatenate([zero, z16, sin, jnp.zeros((lp, 32), F32)], axis=1)
    s2 = jnp.concatenate([zero, -sin, z16, jnp.zeros((lp, 32), F32)], axis=1)
    return c, s1, s2


def kernel(x, meta_tokens, pre_norm_g, w_in, gate_bias, pool_w, pool_scale, w_out_pool, q_norm_g, w_uq, kv_norm_g, w_ukv, w_out_mla, conf_dw_w, conf_dw_b, conf_ln_g, conf_ln_b, w_out_conf, sc_dw_w, w_out_sc, w_o, post_norm_g, loss_target, m_meta_tokens, m_pre_norm_g, m_w_in, m_gate_bias, m_pool_w, m_pool_scale, m_w_out_pool, m_q_norm_g, m_w_uq, m_kv_norm_g, m_w_ukv, m_w_out_mla, m_conf_dw_w, m_conf_dw_b, m_conf_ln_g, m_conf_ln_b, m_w_out_conf, m_sc_dw_w, m_w_out_sc, m_w_o, m_post_norm_g, v_meta_tokens, v_pre_norm_g, v_w_in, v_gate_bias, v_pool_w, v_pool_scale, v_w_out_pool, v_q_norm_g, v_w_uq, v_kv_norm_g, v_w_ukv, v_w_out_mla, v_conf_dw_w, v_conf_dw_b, v_conf_ln_g, v_conf_ln_b, v_w_out_conf, v_sc_dw_w, v_w_out_sc, v_w_o, v_post_norm_g):
    names = ["w_in", "w_uq", "w_o"] + [n for n, _ in C128] + [n for n, _ in TAIL] + [n for n, _ in SMALL]
    loc = locals()
    w = {n: loc[n] for n in names}
    mom = {n: loc["m_" + n] for n in names}
    vel = {n: loc["v_" + n] for n in names}

    seq = x.shape[1]
    n_real = N_META + seq
    lp = -(-n_real // RB) * RB
    tmb = lp // 3
    tabs = _rope_tables(lp)

    c128 = jnp.concatenate([w[n] for n, _ in C128], axis=1)
    def shards_of(i):
        return [w_in[i].astype(BF16), c128[i].astype(BF16), w_uq[i].astype(BF16), w_o[i].astype(BF16)]

    gathered = [_all_gather(shards_of(0), "gather_weights")] + [None] * (DEPTH - 1)
    tail_w = _pack_tail(w)
    tail = _unpack_tail_full(_all_gather([tail_w], "gather_tail")[0])
    eye4 = jnp.eye(4, dtype=F32)
    bd_all = (pool_w[:, :, :, None, :] * eye4[None, :, None, :, None]).reshape(DEPTH, 256, 256).astype(BF16)

    def layer_weights(i):
        g_in, g_c128, g_uq, g_o = gathered[i]
        lw = {}
        lw["w_in"] = _pack_w_in(g_in)
        lw["wc"] = _cols_full(g_c128)
        wuq = _cols_full(g_uq).reshape(Q_RANK, HEADS, 96)
        lw["w_uq"] = jnp.pad(wuq, ((0, 0), (0, 0), (0, 32))).reshape(Q_RANK, HEADS * LANE)
        wukv = lw["wc"][U_OFF[4]:].reshape(KV_RANK, HEADS, 128)
        wk = jnp.pad(wukv[:, :, :QK_NOPE], ((0, 0), (0, 0), (0, 64))).reshape(KV_RANK, HEADS * LANE)
        lw["w_ukv"] = jnp.concatenate([wk, wukv[:, :, QK_NOPE:].reshape(KV_RANK, HEADS * V_DIM)], axis=1)
        lw["w_o"] = g_o.reshape(D, D)
        lw["bd"] = bd_all[i]
        lw["cw"] = jnp.pad(tail["conf_dw_w"][i], ((0, 1), (0, 0)))
        lw["sw"] = jnp.pad(tail["sc_dw_w"][i], ((0, 8 - SC_K), (0, 0)))
        return lw

    meta_full = tail["meta_tokens"]

    pad_rows = lp - n_real
    xr = jnp.concatenate([meta_full, x[0], jnp.zeros((pad_rows, D), F32)], axis=0)
    tgt = jnp.pad(loss_target[0], ((N_META, pad_rows), (0, 0)))
    saved = []
    for i in range(DEPTH):
        lw = layer_weights(i)
        h = _rms_fwd(xr, pre_norm_g[i:i + 1], lp)
        z = _mm(h, lw["w_in"], lp, ZW, D, tm=RB, tn=ZW // 2, tk=D, n_outer=True, name="mm_in")
        qn, kvn, qt, kt, vt = _mla_up(z, q_norm_g[i:i + 1], kv_norm_g[i:i + 1], lw["w_uq"], lw["w_ukv"], tabs, lp)
        res = _attn_fwd(qt, kt, vt, lp, gather=shards_of(i + 1) if i + 1 < DEPTH else ())
        oat, lse = res[0], res[1]
        if i + 1 < DEPTH:
            gathered[i + 1] = res[2:]
        u = _mix_fwd(z, oat, lw["bd"], pool_scale[i:i + 1], lw["cw"], conf_dw_b[i:i + 1], conf_ln_g[i:i + 1],
                     conf_ln_b[i:i + 1], lw["sw"], lp)
        x_new, m_act, o2 = _merge_fwd(xr, u, z, gate_bias[i:i + 1], lw["wc"], lw["w_o"],
                                      post_norm_g[i:i + 1], lp)
        saved.append(dict(lw=lw, x=xr, h=h, z=z, qn=qn, kvn=kvn, qt=qt, kt=kt, vt=vt, oat=oat, lse=lse,
                          u=u, m=m_act, o2=o2))
        xr = x_new

    dx, loss_part = _loss_head(xr, tgt, n_real, lp)

    gsm = {n: [None] * DEPTH for n, _ in SMALL}
    g_cw = [None] * DEPTH
    g_sw = [None] * DEPTH
    recv = [lax.empty((N_DEV, DEPTH) + s, BF16) for s in ((D, 916), (C128_ROWS, 128), (Q_RANK, 96), (128, D))]
    pending = None
    for i in reversed(range(DEPTH)):
        s = saved[i]
        lw = s["lw"]
        do2, dgl, dyb, du, dgb, dgpost = _merge_bwd(dx, s["o2"], s["u"], s["z"], gate_bias[i:i + 1],
                                                    lw["wc"], lw["w_o"], post_norm_g[i:i + 1], lp)
        d_wo = _mm(s["m"], do2, D, D, lp, ta=True, tm=512, tn=D, tk=tmb, out_dtype=BF16, name="mm_dwo")
        d_wout = _dwout(s["u"], dyb, lp)
        dzx, doat, dbd, dcw, dsw, dsm = _mix_bwd(
            s["z"], s["oat"], du, lw["bd"], pool_scale[i:i + 1], lw["cw"], conf_dw_b[i:i + 1],
            conf_ln_g[i:i + 1], conf_ln_b[i:i + 1], lw["sw"], lp)
        res = _attn_bwd(s["qt"], s["kt"], s["vt"], s["oat"], doat, s["lse"], lp,
                        scatter=(pending, recv, i + 1) if pending else None)
        dqt, dkt, dvt = res[:3]
        if pending:
            recv = list(res[3:])
        dzq, dqg, dkvg, d_wuq, d_wukv = _mla_up_bwd(s["z"], q_norm_g[i:i + 1], kv_norm_g[i:i + 1], lw["w_uq"],
                                                    lw["w_ukv"], s["qn"], s["kvn"], dqt, dkt, dvt, tabs, lp)
        d_wuq, d_wukv = d_wuq.astype(BF16), d_wukv.astype(BF16)
        d_win = [_mm(s["h"], seg, D, seg.shape[1], lp, ta=True, tm=D, tn=tn, tk=tmb, out_dtype=BF16,
                     name="mm_dwin%d" % k) for k, (seg, tn) in enumerate(((dgl, 1024), (dzq, 512), (dzx, 1408)))]
        d_wuq_o = d_wuq.reshape(Q_RANK, HEADS, LANE)[:, :, :96].reshape(Q_RANK, HEADS * 96)
        d_wukv_o = jnp.concatenate([d_wukv[:, :1024].reshape(KV_RANK, HEADS, LANE)[:, :, :QK_NOPE],
                                    d_wukv[:, 1024:].reshape(KV_RANK, HEADS, V_DIM)], axis=2).reshape(KV_RANK, 1024)
        pending = [
            _w_in_grad_by_dest(*d_win),
            _cols_by_dest(jnp.concatenate([d_wout, d_wukv_o], axis=0), 128),
            _cols_by_dest(d_wuq_o, 96),
            d_wo.reshape(N_DEV, 128, D)]
        res = _dh(dgl, dzq, dzx, lw["w_in"], s["x"], pre_norm_g[i:i + 1], dx, lp,
                  scatter=(pending, recv, 0) if i == 0 else None)
        dx, dgpre = res[0], res[1]
        if i == 0:
            recv = list(res[2:])

        gsm["pre_norm_g"][i] = dgpre[0]
        gsm["gate_bias"][i] = dgb[0]
        gsm["pool_w"][i] = jnp.stack([dbd[64 * g:64 * (g + 1), 64 * g:64 * (g + 1)] for g in range(4)])
        gsm["pool_scale"][i] = dsm[0]
        gsm["conf_dw_b"][i] = dsm[1]
        gsm["conf_ln_g"][i] = dsm[2]
        gsm["conf_ln_b"][i] = dsm[3]
        gsm["q_norm_g"][i] = dqg[0]
        gsm["kv_norm_g"][i] = dkvg[0]
        gsm["post_norm_g"][i] = dgpost[0]
        g_cw[i] = dcw[:CONF_K]
        g_sw[i] = dsw[:SC_K]

    outs = [dict() for _ in range(4)]

    def put(n, res):
        for t, r in zip(outs, res):
            t[n] = r

    put("w_in", _reduce_adam(recv[0], w["w_in"], mom["w_in"], vel["w_in"], 256, "adam_w_in"))
    off = 0
    for n, rows in C128:
        put(n, _reduce_adam(recv[1], w[n], mom[n], vel[n], 128, "adam_" + n, row_off=off))
        off += rows
    put("w_uq", _reduce_adam(recv[2], w["w_uq"], mom["w_uq"], vel["w_uq"], Q_RANK, "adam_w_uq"))
    put("w_o", _reduce_adam(recv[3], w["w_o"], mom["w_o"], vel["w_o"], 128, "adam_w_o"))

    tail_g = {"meta_tokens": _cols_by_dest(dx[:N_META], 128),
              "conf_dw_w": jnp.moveaxis(jnp.stack(g_cw).reshape(DEPTH, CONF_K, N_DEV, 32), 2, 0),
              "sc_dw_w": jnp.moveaxis(jnp.stack(g_sw).reshape(DEPTH, SC_K, N_DEV, 32), 2, 0)}
    tail_bd = jnp.concatenate([tail_g[n].reshape(N_DEV, -1, LANE) for n, _ in TAIL]
                              + [jnp.zeros((N_DEV, TAIL_PAD - TAIL_ROWS, LANE), F32)], axis=1)
    tail_recv = _all_to_all([tail_bd], [lax.empty((N_DEV, TAIL_PAD, LANE), F32)], None, "scatter_tail")[0]
    tail_res = _reduce_adam_flat(tail_recv, tail_w, _pack_tail(mom), _pack_tail(vel), "adam_tail")

    small_g = {n: jnp.stack(gsm[n]) for n, _ in SMALL}
    loss_row = jnp.concatenate([loss_part[0:1, 0:1], jnp.zeros((1, LANE - 1), F32)], axis=1)
    zrow = jnp.zeros((1, LANE), F32)
    parts = _all_gather([_pack_small(small_g, loss_row)], "gather_small_grads")[0]
    small_res = _reduce_adam_flat(parts, _pack_small(w, zrow), _pack_small(mom, zrow), _pack_small(vel, zrow),
                                  "adam_small")
    loss = small_res[0][SMALL_ROWS, 0]
    for t, tf, sf in zip(outs, tail_res, small_res):
        t.update(_unpack_tail(tf))
        t.update(_unpack_small(sf))
    order = ["meta_tokens", "pre_norm_g", "w_in", "gate_bias", "pool_w", "pool_scale", "w_out_pool", "q_norm_g",
             "w_uq", "kv_norm_g", "w_ukv", "w_out_mla", "conf_dw_w", "conf_dw_b", "conf_ln_g", "conf_ln_b",
             "w_out_conf", "sc_dw_w", "w_out_sc", "w_o", "post_norm_g"]
    grad_x = dx[N_META:n_real][None]
    return (loss, grad_x, *[t[n] for t in outs for n in order])
```

```python
import jax
import jax.numpy as jnp
import numpy as np
from jax import lax
from jax.experimental import pallas as pl
from jax.experimental.pallas import tpu as pltpu

F32 = jnp.float32
BF16 = jnp.bfloat16

D = 1024
N_META = 16
DEPTH = 4
EPS = 1e-6
HEADS = 8
QK_NOPE = 64
QK_ROPE = 32
V_DIM = 64
Q_RANK = 256
KV_RANK = 128
ROPE_THETA = 10000.0
SCALE = (QK_NOPE + QK_ROPE) ** -0.5
CONF_K = 31
SC_K = 3
N_DEV = 8

ADAM_LR = 0.001
ADAM_B1 = 0.9
ADAM_B2 = 0.999
ADAM_EPS = 1e-08
ADAM_WD = 0.01
ADAM_STEP = 10

RB = 384
HB = 32
LANE = 128
VMEM_LIMIT = 56 * 1024 * 1024

GL0, CQ0, CKV0, KR0, PV0, PG0, MG0, CU0, CG0, SBB0, SBC0, SBX0, SG0, ZW = (
    0, 4096, 4352, 4480, 4608, 4864, 5120, 5632, 6144, 6400, 6656, 6912, 7168, 7424)
ZSEG = ((0, 4096), (4096, 512), (4608, 2816))
LOG2E = 1.4426950408889634
LN2 = 0.6931471805599453

MESH = pl.DeviceIdType.MESH


def _cp(**kw):
    return pltpu.CompilerParams(vmem_limit_bytes=VMEM_LIMIT, **kw)


def _sig(x):
    return jax.nn.sigmoid(x)


def _silu(x):
    return x * _sig(x)


def _dsilu(x):
    s = _sig(x)
    return s * (1.0 + x * (1.0 - s))


def _dn(x, k):
    return x if k == 0 else pltpu.roll(x, k, 0)


def _up(x, k):
    return x if k == 0 else pltpu.roll(x, x.shape[0] - k, 0)


def _rope(t, c, s1, s2):
    return t * c + pltpu.roll(t, 16, 1) * s1 + pltpu.roll(t, LANE - 16, 1) * s2


def _rope_t(g, c, s1, s2):
    return g * c + pltpu.roll(g * s1, LANE - 16, 1) + pltpu.roll(g * s2, 16, 1)


def _mm(a, b, m, n, k, *, ta=False, tb=False, out_dtype=F32, tm, tn, tk, name,
        a_moff=0, a_koff=0, b_noff=0, b_koff=0, c=None, n_outer=False):
    assert m % tm == 0 and n % tn == 0 and k % tk == 0, (name, m, n, k, tm, tn, tk)
    nk = k // tk
    dims = (((0,) if ta else (1,), (1,) if tb else (0,)), ((), ()))
    has_c = c is not None

    def body(a_ref, b_ref, *rest):
        c_ref = rest[0] if has_c else None
        o_ref = rest[1] if has_c else rest[0]
        scr = rest[2:] if has_c else rest[1:]
        part = lax.dot_general(a_ref[...].astype(BF16), b_ref[...].astype(BF16), dims,
                               preferred_element_type=F32)

        def finish(total):
            if has_c:
                total = total + c_ref[...]
            o_ref[...] = total.astype(out_dtype)

        if nk == 1:
            finish(part)
        else:
            acc = scr[0]
            kk = pl.program_id(2)

            @pl.when(kk == 0)
            def _():
                acc[...] = part

            @pl.when(kk > 0)
            def _():
                acc[...] += part

            @pl.when(kk == nk - 1)
            def _():
                finish(acc[...])

    def im(f):
        return (lambda g0, g1, q: f(g1, g0, q)) if n_outer else f

    if ta:
        a_spec = pl.BlockSpec((tk, tm), im(lambda i, j, q: (q + a_koff, i + a_moff)))
    else:
        a_spec = pl.BlockSpec((tm, tk), im(lambda i, j, q: (i + a_moff, q + a_koff)))
    if tb:
        b_spec = pl.BlockSpec((tn, tk), im(lambda i, j, q: (j + b_noff, q + b_koff)))
    else:
        b_spec = pl.BlockSpec((tk, tn), im(lambda i, j, q: (q + b_koff, j + b_noff)))
    o_spec = pl.BlockSpec((tm, tn), im(lambda i, j, q: (i, j)))
    grid = (n // tn, m // tm, nk) if n_outer else (m // tm, n // tn, nk)
    return pl.pallas_call(
        body, grid=grid, in_specs=[a_spec, b_spec] + ([o_spec] if has_c else []),
        out_specs=o_spec, out_shape=jax.ShapeDtypeStruct((m, n), out_dtype),
        scratch_shapes=[pltpu.VMEM((tm, tn), F32)] if nk > 1 else [],
        name=name, compiler_params=_cp())(*((a, b, c) if has_c else (a, b)))


def _rms_fwd(x, g, lp):
    def body(x_ref, g_ref, h_ref):
        xv = x_ref[...]
        r = lax.rsqrt(jnp.mean(xv * xv, axis=-1, keepdims=True) + EPS)
        h_ref[...] = (xv * r * g_ref[...]).astype(BF16)

    return pl.pallas_call(
        body, grid=(lp // RB,),
        in_specs=[pl.BlockSpec((RB, D), lambda i: (i, 0)), pl.BlockSpec((1, D), lambda i: (0, 0))],
        out_specs=pl.BlockSpec((RB, D), lambda i: (i, 0)),
        out_shape=jax.ShapeDtypeStruct((lp, D), BF16), name="rms_fwd", compiler_params=_cp())(x, g)


def _dwout(u, dyb, lp):
    tk = lp // 3

    def body(u_ref, dy_ref, o_ref, acc):
        kk = pl.program_id(0)
        for b in range(4):
            rows = slice(U_OFF[b], U_OFF[b + 1])
            part = lax.dot_general(u_ref[:, rows], dy_ref[:, D * b:D * (b + 1)], (((0,), (0,)), ((), ())),
                                   preferred_element_type=F32)

            @pl.when(kk == 0)
            def _(part=part, rows=rows):
                acc[rows, :] = part

            @pl.when(kk > 0)
            def _(part=part, rows=rows):
                acc[rows, :] += part

        @pl.when(kk == 2)
        def _():
            o_ref[...] = acc[...].astype(BF16)

    return pl.pallas_call(
        body, grid=(3,),
        in_specs=[pl.BlockSpec((tk, U_OFF[4]), lambda q: (q, 0)), pl.BlockSpec((tk, 4 * D), lambda q: (q, 0))],
        out_specs=pl.BlockSpec((U_OFF[4], D), lambda q: (0, 0)),
        out_shape=jax.ShapeDtypeStruct((U_OFF[4], D), BF16),
        scratch_shapes=[pltpu.VMEM((U_OFF[4], D), F32)],
        name="dwout", compiler_params=_cp())(u, dyb)


def _mla_up(z, qg, kvg, w_uq, w_ukv, tabs, lp):
    def body(z_ref, qg_ref, kvg_ref, wuq_ref, wukv_ref, c_ref, s1_ref, s2_ref,
             qn_ref, kvn_ref, qo_ref, ko_ref, vo_ref):
        c, s1, s2 = c_ref[...], s1_ref[...], s2_ref[...]
        cq = z_ref[:, 0:256]
        ckv = z_ref[:, 256:384]
        rq = lax.rsqrt(jnp.mean(cq * cq, axis=-1, keepdims=True) + EPS)
        rk = lax.rsqrt(jnp.mean(ckv * ckv, axis=-1, keepdims=True) + EPS)
        qn = (cq * rq * qg_ref[...]).astype(BF16)
        kvn = (ckv * rk * kvg_ref[...]).astype(BF16)
        qn_ref[...] = qn
        kvn_ref[...] = kvn
        kr = _rope(z_ref[:, 384:512], c, s1, s2)
        q_raw = jnp.dot(qn, wuq_ref[...], preferred_element_type=F32)
        kv_raw = jnp.dot(kvn, wukv_ref[...], preferred_element_type=F32)
        for h in range(HEADS):
            sl = slice(LANE * h, LANE * (h + 1))
            qo_ref[:, sl] = (_rope(q_raw[:, sl], c, s1, s2) * (SCALE * LOG2E)).astype(BF16)
            ko_ref[:, sl] = (kv_raw[:, sl] + kr).astype(BF16)
        vo_ref[...] = kv_raw[:, 1024:1536].astype(BF16)

    tab = pl.BlockSpec((RB, LANE), lambda i: (i, 0))
    wide = pl.BlockSpec((RB, 1024), lambda i: (i, 0))
    full = lambda r, c: pl.BlockSpec((r, c), lambda i: (0, 0))
    return pl.pallas_call(
        body, grid=(lp // RB,),
        in_specs=[pl.BlockSpec((RB, 512), lambda i: (i, CQ0 // 512)), full(1, 256), full(1, 128),
                  full(Q_RANK, 1024), full(KV_RANK, 1536), tab, tab, tab],
        out_specs=[pl.BlockSpec((RB, 256), lambda i: (i, 0)), tab, wide, wide,
                   pl.BlockSpec((RB, 512), lambda i: (i, 0))],
        out_shape=[jax.ShapeDtypeStruct((lp, 256), BF16), jax.ShapeDtypeStruct((lp, 128), BF16),
                   jax.ShapeDtypeStruct((lp, 1024), BF16), jax.ShapeDtypeStruct((lp, 1024), BF16),
                   jax.ShapeDtypeStruct((lp, 512), BF16)],
        name="mla_up", compiler_params=_cp())(z, qg, kvg, w_uq, w_ukv, *tabs)


def _mla_up_bwd(z, qg, kvg, w_uq, w_ukv, qn, kvn, dq, dk, dv, tabs, lp):
    def body(z_ref, qg_ref, kvg_ref, wuq_ref, wukv_ref, qn_ref, kvn_ref, dq_ref, dk_ref, dv_ref,
             c_ref, s1_ref, s2_ref, dz_ref, dqg_ref, dkvg_ref, dwuq_ref, dwukv_ref, dqr_ref, dkv_ref):
        i = pl.program_id(0)
        c, s1, s2 = c_ref[...], s1_ref[...], s2_ref[...]
        lane = lax.broadcasted_iota(jnp.int32, (1, LANE), 1)
        ropel = (lane >= QK_NOPE) & (lane < QK_NOPE + QK_ROPE)
        ksum = jnp.zeros((RB, LANE), F32)
        for h in range(HEADS):
            sl = slice(LANE * h, LANE * (h + 1))
            dqr_ref[:, sl] = _rope_t(dq_ref[:, sl].astype(F32) * SCALE, c, s1, s2).astype(BF16)
            dkt = dk_ref[:, sl]
            dkv_ref[:, sl] = dkt
            ksum = ksum + dkt.astype(F32)
        dkv_ref[:, 1024:1536] = dv_ref[...]
        dkr = jnp.where(ropel, _rope_t(jnp.where(ropel, ksum, 0.0), c, s1, s2), 0.0)
        dqn = lax.dot_general(dqr_ref[...], wuq_ref[...], (((1,), (1,)), ((), ())), preferred_element_type=F32)
        dkvn = lax.dot_general(dkv_ref[...], wukv_ref[...], (((1,), (1,)), ((), ())), preferred_element_type=F32)

        def rms_b(xv, g, dy):
            r = lax.rsqrt(jnp.mean(xv * xv, axis=-1, keepdims=True) + EPS)
            a = dy * g
            dx = r * a - xv * (r * r * r) * jnp.mean(a * xv, axis=-1, keepdims=True)
            return dx, jnp.sum(dy * xv * r, axis=0, keepdims=True)

        dcq, pq = rms_b(z_ref[:, 0:256], qg_ref[...], dqn)
        dckv, pk = rms_b(z_ref[:, 256:384], kvg_ref[...], dkvn)
        dz_ref[:, 0:256] = dcq.astype(BF16)
        dz_ref[:, 256:384] = dckv.astype(BF16)
        dz_ref[:, 384:512] = dkr.astype(BF16)

        rows0 = (((0,), (0,)), ((), ()))
        pwq = lax.dot_general(qn_ref[...], dqr_ref[...], rows0, preferred_element_type=F32)
        pwk = lax.dot_general(kvn_ref[...], dkv_ref[...], rows0, preferred_element_type=F32)

        @pl.when(i == 0)
        def _():
            dqg_ref[...] = pq
            dkvg_ref[...] = pk
            dwuq_ref[...] = pwq
            dwukv_ref[...] = pwk

        @pl.when(i > 0)
        def _():
            dqg_ref[...] += pq
            dkvg_ref[...] += pk
            dwuq_ref[...] += pwq
            dwukv_ref[...] += pwk

    tab = pl.BlockSpec((RB, LANE), lambda i: (i, 0))
    wide = pl.BlockSpec((RB, 1024), lambda i: (i, 0))
    half = pl.BlockSpec((RB, 512), lambda i: (i, 0))
    full = lambda r, c: pl.BlockSpec((r, c), lambda i: (0, 0))
    return pl.pallas_call(
        body, grid=(lp // RB,),
        in_specs=[pl.BlockSpec((RB, 512), lambda i: (i, CQ0 // 512)), full(1, 256), full(1, 128),
                  full(Q_RANK, 1024), full(KV_RANK, 1536), pl.BlockSpec((RB, Q_RANK), lambda i: (i, 0)), tab,
                  wide, wide, half, tab, tab, tab],
        out_specs=[half, full(1, 256), full(1, 128), full(Q_RANK, 1024), full(KV_RANK, 1536)],
        out_shape=[jax.ShapeDtypeStruct((lp, 512), BF16), jax.ShapeDtypeStruct((1, 256), F32),
                   jax.ShapeDtypeStruct((1, 128), F32), jax.ShapeDtypeStruct((Q_RANK, 1024), F32),
                   jax.ShapeDtypeStruct((KV_RANK, 1536), F32)],
        scratch_shapes=[pltpu.VMEM((RB, 1024), BF16), pltpu.VMEM((RB, 1536), BF16)],
        name="mla_up_bwd", compiler_params=_cp())(z, qg, kvg, w_uq, w_ukv, qn, kvn, dq, dk, dv, *tabs)


def _head_lanes(e):
    lane = lax.broadcasted_iota(jnp.int32, (1, LANE), 1)
    return lane >= V_DIM if e else lane < V_DIM


ONE_LANE = (V_DIM, 0)


def _attn_fwd(q, k, v, lp, gather=()):
    nq = lp // RB
    n = len(gather)
    steps = HEADS // 2

    def body(q_ref, k_ref, v_ref, *rest):
        o_ref, lse_ref = rest[n], rest[n + 1]
        vm_scr = rest[2 * n + 2]
        if n:
            g_start, g_forward, g_finish = _gather_phases(rest[:n], rest[n + 2:2 * n + 2], *rest[2 * n + 3:])
            pl.when(pl.program_id(0) == 0)(g_start)
            pl.when(pl.program_id(0) == steps - 1)(g_forward)
        lane = lax.broadcasted_iota(jnp.int32, (1, LANE), 1)
        vv = v_ref[...]
        for e in range(2):
            ones = jnp.where(lane == ONE_LANE[e], 1.0, 0.0).astype(BF16)
            vm_scr[e] = jnp.where(_head_lanes(e), vv, jnp.broadcast_to(ones, vv.shape))
        causal = (lax.broadcasted_iota(jnp.int32, (RB, RB), 1) <= lax.broadcasted_iota(jnp.int32, (RB, RB), 0))

        def qblock(i, _):
            rows = pl.ds(pl.multiple_of(i * RB, RB), RB)
            qs = [q_ref[rows, LANE * e:LANE * (e + 1)] for e in range(2)]

            def scores(j):
                cols = pl.ds(pl.multiple_of(j * RB, RB), RB)
                return tuple(lax.dot_general(qs[e], k_ref[cols, LANE * e:LANE * (e + 1)], (((1,), (1,)), ((), ())),
                                             preferred_element_type=F32) for e in range(2))

            def update(j, s, carry, masked):
                cols = pl.ds(pl.multiple_of(j * RB, RB), RB)
                out = []
                for e in range(2):
                    m, acc = carry[2 * e], carry[2 * e + 1]
                    se = jnp.where(causal, s[e], -jnp.inf) if masked else s[e]
                    m_new = jnp.maximum(m, jnp.max(se, axis=-1, keepdims=True))
                    p = jnp.exp2((se - m_new).astype(BF16))
                    acc = jnp.exp2(m - m_new) * acc + jnp.dot(p, vm_scr[e, cols, :], preferred_element_type=F32)
                    out += [m_new, acc]
                return tuple(out)

            def pair(jj, c):
                sa, sb = scores(2 * jj), scores(2 * jj + 1)
                return update(2 * jj + 1, sb, update(2 * jj, sa, c, False), False)

            m0 = jnp.full((RB, 1), -jnp.inf, F32)
            a0 = jnp.zeros((RB, LANE), F32)
            carry = lax.fori_loop(0, i // 2, pair, (m0, a0, m0, a0))

            def last_two(c):
                sa, sb = scores(i - 1), scores(i)
                return update(i, sb, update(i - 1, sa, c, False), True)

            carry = lax.cond(i % 2 == 1, last_two, lambda c: update(i, scores(i), c, True), carry)
            o, lse = [], []
            for e in range(2):
                m, acc = carry[2 * e], carry[2 * e + 1]
                l = acc[:, ONE_LANE[e]:ONE_LANE[e] + 1]
                o.append(acc / l)
                lse.append(jnp.broadcast_to(m + jnp.log2(l), (RB, LANE)))
            o_ref[rows, :] = jnp.where(_head_lanes(0), o[0], o[1])
            lse_ref[rows, :] = jnp.where(_head_lanes(0), lse[0], lse[1])
            return 0

        lax.fori_loop(0, nq, qblock, 0)
        if n:
            pl.when(pl.program_id(0) == steps - 1)(g_finish)

    two = pl.BlockSpec((lp, 2 * LANE), lambda h: (0, h))
    one = pl.BlockSpec((lp, LANE), lambda h: (0, h))
    anyspec = pl.BlockSpec(memory_space=pl.ANY)
    return pl.pallas_call(
        body, grid=(steps,), in_specs=[two, two, one] + [anyspec] * n, out_specs=[one, one] + [anyspec] * n,
        out_shape=[jax.ShapeDtypeStruct((lp, 512), F32), jax.ShapeDtypeStruct((lp, 512), F32)]
        + _gather_shapes(gather),
        scratch_shapes=[pltpu.VMEM((2, lp, LANE), BF16)] + (_comm_sems(n) if n else []),
        name="attn_fwd_gather" if n else "attn_fwd", compiler_params=_cp())(q, k, v, *gather)


def _attn_bwd(q, k, v, o, do, lse, lp, scatter=None):
    nq = lp // RB
    xs, bufs, layer = scatter if scatter else ((), (), None)
    n = len(xs)
    steps = HEADS // 2

    def body(q_ref, k_ref, v_ref, o_ref, do_ref, lse_ref, *rest):
        dq_ref, dk_ref, dv_ref = rest[2 * n:2 * n + 3]
        vm_scr, dom_scr, dl_scr, dq_scr = rest[3 * n + 3:3 * n + 7]
        if n:
            s_start, s_finish = _scatter_phases(rest[:n], rest[2 * n + 3:3 * n + 3], *rest[3 * n + 7:], layer)
            pl.when(pl.program_id(0) == 0)(s_start)
        causal = (lax.broadcasted_iota(jnp.int32, (RB, RB), 1) <= lax.broadcasted_iota(jnp.int32, (RB, RB), 0))
        vv = v_ref[...]
        for e in range(2):
            vm_scr[e] = jnp.where(_head_lanes(e), vv, jnp.zeros_like(vv))

        def prep(i, _):
            rows = pl.ds(pl.multiple_of(i * RB, RB), RB)
            prod = do_ref[rows, :] * o_ref[rows, :]
            dls = []
            for e in range(2):
                hm = _head_lanes(e)
                dom_scr[e, rows, :] = jnp.where(hm, do_ref[rows, :], 0.0).astype(BF16)
                dls.append(jnp.sum(jnp.where(hm, prod, 0.0), axis=-1, keepdims=True))
            dl_scr[rows, :] = jnp.where(_head_lanes(0), dls[0], dls[1])
            dq_scr[rows, :] = jnp.zeros((RB, 2 * LANE), F32)
            return 0

        lax.fori_loop(0, nq, prep, 0)

        def kvblock(j, _):
            cols = pl.ds(pl.multiple_of(j * RB, RB), RB)
            kbs = [k_ref[cols, LANE * e:LANE * (e + 1)] for e in range(2)]

            def products(i):
                rows = pl.ds(pl.multiple_of(i * RB, RB), RB)
                out = []
                for e in range(2):
                    out.append(lax.dot_general(q_ref[rows, LANE * e:LANE * (e + 1)], kbs[e],
                                               (((1,), (1,)), ((), ())), preferred_element_type=F32))
                    out.append(lax.dot_general(dom_scr[e, rows, :], vm_scr[e, cols, :],
                                               (((1,), (1,)), ((), ())), preferred_element_type=F32))
                return tuple(out)

            def update(i, sd, carry, masked):
                rows = pl.ds(pl.multiple_of(i * RB, RB), RB)
                out = []
                for e in range(2):
                    dk, dv = carry[2 * e], carry[2 * e + 1]
                    sl = slice(LANE * e, LANE * (e + 1))
                    col1 = slice(V_DIM * e, V_DIM * e + 1)
                    s, dp = sd[2 * e], sd[2 * e + 1]
                    if masked:
                        s = jnp.where(causal, s, -jnp.inf)
                    p = jnp.exp2((s - lse_ref[rows, col1]).astype(BF16))
                    dv = dv + lax.dot_general(p, dom_scr[e, rows, :], (((0,), (0,)), ((), ())),
                                              preferred_element_type=F32)
                    ds = p * (dp - dl_scr[rows, col1]).astype(BF16)
                    dk = dk + lax.dot_general(ds, q_ref[rows, sl], (((0,), (0,)), ((), ())),
                                              preferred_element_type=F32)
                    dq_scr[rows, sl] += jnp.dot(ds, kbs[e], preferred_element_type=F32)
                    out += [dk, dv]
                return tuple(out)

            def pair(t, c):
                i0 = nq - 2 - 2 * t
                pa, pb = products(i0), products(i0 + 1)
                return update(i0 + 1, pb, update(i0, pa, c, False), False)

            def first_two(c):
                pa, pb = products(j), products(j + 1)
                return update(j + 1, pb, update(j, pa, c, True), False)

            zero = jnp.zeros((RB, LANE), F32)
            below = nq - 1 - j
            carry = lax.fori_loop(0, below // 2, pair, (zero, zero, zero, zero))
            dk0, dv0, dk1, dv1 = lax.cond(below % 2 == 1, first_two,
                                          lambda c: update(j, products(j), c, True), carry)
            dk_ref[cols, 0:LANE] = (dk0 * LN2).astype(BF16)
            dk_ref[cols, LANE:2 * LANE] = (dk1 * LN2).astype(BF16)
            dv_ref[cols, :] = (dv0 + dv1).astype(BF16)
            return 0

        lax.fori_loop(0, nq, kvblock, 0)

        def fin(i, _):
            rows = pl.ds(pl.multiple_of(i * RB, RB), RB)
            dq_ref[rows, :] = dq_scr[rows, :].astype(BF16)
            return 0

        lax.fori_loop(0, nq, fin, 0)
        if n:
            pl.when(pl.program_id(0) == steps - 1)(s_finish)

    two = pl.BlockSpec((lp, 2 * LANE), lambda h: (0, h))
    one = pl.BlockSpec((lp, LANE), lambda h: (0, h))
    anyspec = pl.BlockSpec(memory_space=pl.ANY)
    return pl.pallas_call(
        body, grid=(steps,), in_specs=[two, two, one, one, one, one] + [anyspec] * (2 * n),
        out_specs=[two, two, one] + [anyspec] * n,
        out_shape=[jax.ShapeDtypeStruct((lp, 1024), BF16), jax.ShapeDtypeStruct((lp, 1024), BF16),
                   jax.ShapeDtypeStruct((lp, 512), BF16)] + [jax.ShapeDtypeStruct(b.shape, b.dtype) for b in bufs],
        input_output_aliases={6 + n + a: 3 + a for a in range(n)},
        scratch_shapes=[pltpu.VMEM((2, lp, LANE), BF16), pltpu.VMEM((2, lp, LANE), BF16),
                        pltpu.VMEM((lp, LANE), F32), pltpu.VMEM((lp, 2 * LANE), F32)]
        + (_comm_sems(n) if n else []),
        name="attn_bwd_scatter" if n else "attn_bwd", compiler_params=_cp())(q, k, v, o, do, lse, *xs, *bufs)


def _pool_lane_windows():
    lane = lax.broadcasted_iota(jnp.int32, (1, 256), 1)
    return jnp.where(lane < 64, 2, jnp.where(lane < 128, 4, jnp.where(lane < 192, 8, 16)))


def _by_window(wl, s2, s4, s8, s16):
    return jnp.where(wl == 2, s2, jnp.where(wl == 4, s4, jnp.where(wl == 8, s8, s16)))


def _pool_fwd_rows(pv_ext, t0):
    n = pv_ext.shape[0]
    wl = _pool_lane_windows()
    s2 = pv_ext + _dn(pv_ext, 1)
    s4 = s2 + _dn(s2, 2)
    s8 = s4 + _dn(s4, 4)
    s16 = s8 + _dn(s8, 8)
    t = t0 + lax.broadcasted_iota(jnp.int32, (n, 1), 0)
    cnt = jnp.maximum(jnp.minimum(t + 1, wl), 1).astype(F32)
    return _by_window(wl, s2, s4, s8, s16) / cnt - pv_ext


def _conv_dn(x_ext, w_ref, taps):
    acc = w_ref[taps - 1:taps, :] * x_ext
    for j in range(1, taps):
        acc = acc + w_ref[taps - 1 - j:taps - j, :] * _dn(x_ext, j)
    return acc


def _conv_up(g_ext, w_ref, taps):
    acc = w_ref[taps - 1:taps, :] * g_ext
    for j in range(1, taps):
        acc = acc + w_ref[taps - 1 - j:taps - j, :] * _up(g_ext, j)
    return acc


def _ln_fwd(c, g, b):
    mu = jnp.mean(c, axis=-1, keepdims=True)
    xc = c - mu
    r = lax.rsqrt(jnp.mean(xc * xc, axis=-1, keepdims=True) + EPS)
    xh = xc * r
    return xh * g + b, xh, r


def _halo_specs(lp, width, col):
    per = RB // HB
    last = lp // HB - 1
    cur = pl.BlockSpec((RB, width), lambda i: (i, col))
    prev = pl.BlockSpec((HB, width), lambda i: (jnp.maximum(i * per - 1, 0), col))
    nxt = pl.BlockSpec((HB, width), lambda i: (jnp.minimum((i + 1) * per, last), col))
    return cur, prev, nxt


def _mix_fwd(z, oat, bd, pscale, cw, cb, lng, lnb, sw, lp):
    def body(za, zah, mg, oat_ref, cu, cuh, cg, sbb, sbc, sbch, sbx, sbxh, sg,
             bd_ref, ps_ref, cw_ref, cb_ref, lng_ref, lnb_ref, sw_ref, u_ref):
        i = pl.program_id(0)
        pm = jnp.where(i > 0, 1.0, 0.0).astype(F32)
        pv = jnp.concatenate([zah[:, 0:256] * pm, za[:, 0:256]], axis=0)
        p = _pool_fwd_rows(pv, i * RB - HB)[HB:]
        y = jnp.dot(p.astype(BF16), bd_ref[...], preferred_element_type=F32)
        u_ref[:, 0:256] = (y * ps_ref[...] * _silu(za[:, 256:512])).astype(BF16)
        u_ref[:, 256:768] = (oat_ref[...] * _silu(mg[...])).astype(BF16)
        ce = jnp.concatenate([cuh[...] * pm, cu[...]], axis=0)
        glu = ce[:, 0:256] * _sig(ce[:, 256:512])
        c = _conv_dn(glu, cw_ref, CONF_K)[HB:] + cb_ref[...]
        n, _, _ = _ln_fwd(c, lng_ref[...], lnb_ref[...])
        u_ref[:, 768:1024] = (_silu(n) * _silu(cg[...])).astype(BF16)
        qe = jnp.concatenate([sbch[...] * sbxh[...] * pm, sbc[...] * sbx[...]], axis=0)
        cv = _conv_dn(qe, sw_ref, SC_K)[HB:]
        u_ref[:, 1024:1280] = (sbb[...] * cv * _silu(sg[...])).astype(BF16)

    a_cur, a_prev, _ = _halo_specs(lp, 512, PV0 // 512)
    cu_cur, cu_prev, _ = _halo_specs(lp, 512, CU0 // 512)
    sc_cur, sc_prev, _ = _halo_specs(lp, 256, SBC0 // 256)
    sx_cur, sx_prev, _ = _halo_specs(lp, 256, SBX0 // 256)
    c256 = lambda c0: pl.BlockSpec((RB, 256), lambda i: (i, c0 // 256))
    full = lambda r, c: pl.BlockSpec((r, c), lambda i: (0, 0))
    return pl.pallas_call(
        body, grid=(lp // RB,),
        in_specs=[a_cur, a_prev, pl.BlockSpec((RB, 512), lambda i: (i, MG0 // 512)),
                  pl.BlockSpec((RB, 512), lambda i: (i, 0)),
                  cu_cur, cu_prev, c256(CG0), c256(SBB0), sc_cur, sc_prev, sx_cur, sx_prev, c256(SG0),
                  full(256, 256), full(1, 256), full(32, 256), full(1, 256), full(1, 256), full(1, 256),
                  full(8, 256)],
        out_specs=pl.BlockSpec((RB, 1280), lambda i: (i, 0)),
        out_shape=jax.ShapeDtypeStruct((lp, 1280), BF16),
        name="mix_fwd", compiler_params=_cp())(z, z, z, oat, z, z, z, z, z, z, z, z, z,
                                               bd, pscale, cw, cb, lng, lnb, sw)


def _mix_bwd(z, oat, du, bd, pscale, cw, cb, lng, lnb, sw, lp):
    nb = lp // RB
    ne = RB + 2 * HB
    nf = RB + HB

    def body(za, zah, zan, mg, oat_ref, cu, cuh, cun, cg, cgn, sbb, sbbn, sbc, sbch, sbcn, sbx, sbxh, sbxn,
             sg, sgn, du_ref, dun_ref, bd_ref, ps_ref, cw_ref, cb_ref, lng_ref, lnb_ref, sw_ref,
             dzx_ref, doat_ref, dbd_ref, dcw_ref, dsw_ref, dsm_ref):
        xa, xm, xc = 0, MG0 - PV0, CU0 - PV0
        i = pl.program_id(0)
        pm = jnp.where(i > 0, 1.0, 0.0).astype(F32)
        nm = jnp.where(i < nb - 1, 1.0, 0.0).astype(F32)

        def ext(cur, prev, nxt, sl=slice(None)):
            return jnp.concatenate([prev[:, sl] * pm, cur[:, sl], nxt[:, sl] * nm], axis=0)

        def fwd(cur, nxt, sl=slice(None)):
            return jnp.concatenate([cur[:, sl], nxt[:, sl] * nm], axis=0)

        def csum(x):
            return jnp.sum(x, axis=0, keepdims=True)

        @pl.when(i == 0)
        def _():
            dbd_ref[...] = jnp.zeros((256, 256), F32)
            dcw_ref[...] = jnp.zeros((32, 256), F32)
            dsw_ref[...] = jnp.zeros((8, 256), F32)
            dsm_ref[...] = jnp.zeros((8, 256), F32)

        a_cols, b_cols = slice(0, 256), slice(256, 512)
        pv_e = ext(za, zah, zan, a_cols)
        p = _pool_fwd_rows(pv_e, i * RB - HB)[HB:HB + RB]
        pb = p.astype(BF16)
        y = jnp.dot(pb, bd_ref[...], preferred_element_type=F32)
        pg_f = fwd(za, zan, b_cols)
        dua_f = fwd(du_ref, dun_ref, slice(0, 256))
        dyp_f = dua_f * ps_ref[...] * _silu(pg_f)
        dypb = dyp_f.astype(BF16)
        dp_f = lax.dot_general(dypb, bd_ref[...], (((1,), (1,)), ((), ())), preferred_element_type=F32)
        wl = _pool_lane_windows()
        t = i * RB + lax.broadcasted_iota(jnp.int32, (nf, 1), 0)
        cnt = jnp.minimum(t + 1, wl).astype(F32)
        qf = dp_f / cnt
        f2 = qf + _up(qf, 1)
        f4 = f2 + _up(f2, 2)
        f8 = f4 + _up(f4, 4)
        f16 = f8 + _up(f8, 8)
        dpv = (_by_window(wl, f2, f4, f8, f16) - dp_f)[0:RB]
        dua = du_ref[:, 0:256]
        pg = za[:, b_cols]
        dpg = dua * y * ps_ref[...] * _dsilu(pg)
        dzx_ref[:, xa:xa + 256] = dpv.astype(BF16)
        dzx_ref[:, xa + 256:xa + 512] = dpg.astype(BF16)
        d_scale = csum(dua * y * _silu(pg))
        d_bd = lax.dot_general(pb, dypb[0:RB], (((0,), (0,)), ((), ())), preferred_element_type=F32)

        dub = du_ref[:, 256:768]
        mgv = mg[...]
        dzx_ref[:, xm:xm + 512] = (dub * oat_ref[...] * _dsilu(mgv)).astype(BF16)
        doat_ref[...] = dub * _silu(mgv)

        a_e = ext(cu, cuh, cun, slice(0, 256))
        gt_e = ext(cu, cuh, cun, slice(256, 512))
        sg_e = _sig(gt_e)
        glu_e = a_e * sg_e
        c_f = _conv_dn(glu_e, cw_ref, CONF_K)[HB:] + cb_ref[...]
        n_f, xh_f, r_f = _ln_fwd(c_f, lng_ref[...], lnb_ref[...])
        cg_f = fwd(cg, cgn)
        duc_f = fwd(du_ref, dun_ref, slice(768, 1024))
        sw_f = _silu(n_f)
        dcg = (duc_f * sw_f * _dsilu(cg_f))[0:RB]
        dn_f = duc_f * _silu(cg_f) * _dsilu(n_f)
        a_f = dn_f * lng_ref[...]
        dc_f = r_f * (a_f - jnp.mean(a_f, axis=-1, keepdims=True)
                      - xh_f * jnp.mean(a_f * xh_f, axis=-1, keepdims=True))
        d_lng = csum((dn_f * xh_f)[0:RB])
        d_lnb = csum(dn_f[0:RB])
        d_cb = csum(dc_f[0:RB])
        dglu = _conv_up(dc_f, cw_ref, CONF_K)[0:RB]
        dc_c = dc_f[0:RB]
        for kk in range(CONF_K):
            j = CONF_K - 1 - kk
            dcw_ref[kk:kk + 1, :] += csum(dc_c * _dn(glu_e, j)[HB:HB + RB])

        sgc = sg_e[HB:HB + RB]
        a_c = a_e[HB:HB + RB]
        dzx_ref[:, xc:xc + 256] = (dglu * sgc).astype(BF16)
        dzx_ref[:, xc + 256:xc + 512] = (dglu * a_c * sgc * (1.0 - sgc)).astype(BF16)
        dzx_ref[:, xc + 512:xc + 768] = dcg.astype(BF16)

        c_e = ext(sbc, sbch, sbcn)
        x_e = ext(sbx, sbxh, sbxn)
        q_e = c_e * x_e
        cv_f = _conv_dn(q_e, sw_ref, SC_K)[HB:]
        bg_f = fwd(sbb, sbbn)
        sg_f = fwd(sg, sgn)
        dud_f = fwd(du_ref, dun_ref, slice(1024, 1280))
        ssg_f = _silu(sg_f)
        dcv_f = dud_f * bg_f * ssg_f
        dbg = (dud_f * cv_f * ssg_f)[0:RB]
        dsg = (dud_f * bg_f * cv_f * _dsilu(sg_f))[0:RB]
        dq = _conv_up(dcv_f, sw_ref, SC_K)[0:RB]
        dcv_c = dcv_f[0:RB]
        for kk in range(SC_K):
            j = SC_K - 1 - kk
            dsw_ref[kk:kk + 1, :] += csum(dcv_c * _dn(q_e, j)[HB:HB + RB])

        dzx_ref[:, xc + 768:xc + 1024] = dbg.astype(BF16)
        dzx_ref[:, xc + 1024:xc + 1280] = (dq * x_e[HB:HB + RB]).astype(BF16)
        dzx_ref[:, xc + 1280:xc + 1536] = (dq * c_e[HB:HB + RB]).astype(BF16)
        dzx_ref[:, xc + 1536:xc + 1792] = dsg.astype(BF16)

        dbd_ref[...] += d_bd
        dsm_ref[0:1, :] += d_scale
        dsm_ref[1:2, :] += d_cb
        dsm_ref[2:3, :] += d_lng
        dsm_ref[3:4, :] += d_lnb

    a3 = _halo_specs(lp, 512, PV0 // 512)
    cu3 = _halo_specs(lp, 512, CU0 // 512)
    cg3 = _halo_specs(lp, 256, CG0 // 256)
    sbb3 = _halo_specs(lp, 256, SBB0 // 256)
    sbc3 = _halo_specs(lp, 256, SBC0 // 256)
    sbx3 = _halo_specs(lp, 256, SBX0 // 256)
    sg3 = _halo_specs(lp, 256, SG0 // 256)
    du3 = _halo_specs(lp, 1280, 0)
    full = lambda r, c: pl.BlockSpec((r, c), lambda i: (0, 0))
    in_specs = [a3[0], a3[1], a3[2], pl.BlockSpec((RB, 512), lambda i: (i, MG0 // 512)),
                pl.BlockSpec((RB, 512), lambda i: (i, 0)),
                cu3[0], cu3[1], cu3[2], cg3[0], cg3[2], sbb3[0], sbb3[2],
                sbc3[0], sbc3[1], sbc3[2], sbx3[0], sbx3[1], sbx3[2], sg3[0], sg3[2],
                du3[0], du3[2],
                full(256, 256), full(1, 256), full(32, 256), full(1, 256), full(1, 256), full(1, 256),
                full(8, 256)]
    out_specs = [pl.BlockSpec((RB, ZW - PV0), lambda i: (i, 0)), pl.BlockSpec((RB, 512), lambda i: (i, 0)),
                 full(256, 256), full(32, 256), full(8, 256), full(8, 256)]
    out_shape = [jax.ShapeDtypeStruct((lp, ZW - PV0), BF16), jax.ShapeDtypeStruct((lp, 512), F32),
                 jax.ShapeDtypeStruct((256, 256), F32), jax.ShapeDtypeStruct((32, 256), F32),
                 jax.ShapeDtypeStruct((8, 256), F32), jax.ShapeDtypeStruct((8, 256), F32)]
    return pl.pallas_call(
        body, grid=(nb,), in_specs=in_specs, out_specs=out_specs, out_shape=out_shape,
        name="mix_bwd", compiler_params=_cp())(
            z, z, z, z, oat, z, z, z, z, z, z, z, z, z, z, z, z, z, z, z, du, du,
            bd, pscale, cw, cb, lng, lnb, sw)


U_OFF = (0, 256, 768, 1024, 1280)


def _merge_fwd(x, u, z, gb, wout, wo, gpost, g_next, lp):
    MRB = RB

    def body(x_ref, u_ref, gl_ref, gb_ref, wout_ref, wo_ref, g_ref, gn_ref, xo_ref, m_ref, o2_ref, hn_ref):
        m = jnp.zeros((MRB, D), F32)
        for b in range(4):
            y = jnp.dot(u_ref[:, U_OFF[b]:U_OFF[b + 1]], wout_ref[U_OFF[b]:U_OFF[b + 1], :],
                        preferred_element_type=F32)
            sl = slice(D * b, D * (b + 1))
            m = m + _sig(gl_ref[:, sl] + gb_ref[:, sl]) * y
        mb = m.astype(BF16)
        m_ref[...] = mb
        o2 = jnp.dot(mb, wo_ref[...], preferred_element_type=F32)
        o2_ref[...] = o2
        r = lax.rsqrt(jnp.mean(o2 * o2, axis=-1, keepdims=True) + EPS)
        xn = x_ref[...] + o2 * r * g_ref[...]
        xo_ref[...] = xn
        rn = lax.rsqrt(jnp.mean(xn * xn, axis=-1, keepdims=True) + EPS)
        hn_ref[...] = (xn * rn * gn_ref[...]).astype(BF16)

    blk = pl.BlockSpec((MRB, D), lambda i: (i, 0))
    full = lambda r, c: pl.BlockSpec((r, c), lambda i: (0, 0))
    return pl.pallas_call(
        body, grid=(lp // MRB,),
        in_specs=[blk, pl.BlockSpec((MRB, 1280), lambda i: (i, 0)), pl.BlockSpec((MRB, 4096), lambda i: (i, 0)),
                  full(1, 4096), full(1280, D), full(D, D), full(1, D), full(1, D)],
        out_specs=[blk, blk, blk, blk],
        out_shape=[jax.ShapeDtypeStruct((lp, D), F32), jax.ShapeDtypeStruct((lp, D), BF16),
                   jax.ShapeDtypeStruct((lp, D), F32), jax.ShapeDtypeStruct((lp, D), BF16)],
        name="merge_fwd", compiler_params=_cp())(x, u, z, gb, wout, wo, gpost, g_next)


def _merge_bwd(dx, o2, u, z, gb, wout, wo, gpost, lp):
    MRB = RB

    def body(dx_ref, o2_ref, u_ref, gl_ref, gb_ref, wout_ref, wo_ref, g_ref,
             do2_ref, dgl_ref, dy_ref, du_ref, dgb_ref, dg_ref):
        i = pl.program_id(0)
        o2 = o2_ref[...]
        dy = dx_ref[...]
        r = lax.rsqrt(jnp.mean(o2 * o2, axis=-1, keepdims=True) + EPS)
        a = dy * g_ref[...]
        do2 = (r * a - o2 * (r * r * r) * jnp.mean(a * o2, axis=-1, keepdims=True)).astype(BF16)
        do2_ref[...] = do2
        dg = jnp.sum(dy * o2 * r, axis=0, keepdims=True)
        dm = lax.dot_general(do2, wo_ref[...], (((1,), (1,)), ((), ())), preferred_element_type=F32)
        for b in range(4):
            rows = slice(U_OFF[b], U_OFF[b + 1])
            y = jnp.dot(u_ref[:, rows], wout_ref[rows, :], preferred_element_type=F32)
            sl = slice(D * b, D * (b + 1))
            gt = _sig(gl_ref[:, sl] + gb_ref[:, sl])
            dgl = dm * y * gt * (1.0 - gt)
            dgl_ref[:, sl] = dgl.astype(BF16)
            part = jnp.sum(dgl, axis=0, keepdims=True)

            @pl.when(i == 0)
            def _(part=part, sl=sl):
                dgb_ref[:, sl] = part

            @pl.when(i > 0)
            def _(part=part, sl=sl):
                dgb_ref[:, sl] += part

            dyb = (dm * gt).astype(BF16)
            dy_ref[:, sl] = dyb
            du_ref[:, rows] = lax.dot_general(dyb, wout_ref[rows, :], (((1,), (1,)), ((), ())),
                                              preferred_element_type=F32)

        @pl.when(i == 0)
        def _():
            dg_ref[...] = dg

        @pl.when(i > 0)
        def _():
            dg_ref[...] += dg

    blk = pl.BlockSpec((MRB, D), lambda i: (i, 0))
    wide = pl.BlockSpec((MRB, 4096), lambda i: (i, 0))
    ub = pl.BlockSpec((MRB, 1280), lambda i: (i, 0))
    full = lambda r, c: pl.BlockSpec((r, c), lambda i: (0, 0))
    once = lambda r, c: pl.BlockSpec((r, c), lambda i: (0, 0), pipeline_mode=pl.Buffered(1))
    return pl.pallas_call(
        body, grid=(lp // MRB,),
        in_specs=[blk, blk, ub, wide, full(1, 4096), once(1280, D), once(D, D), full(1, D)],
        out_specs=[blk, wide, wide, ub, full(1, 4096), full(1, D)],
        out_shape=[jax.ShapeDtypeStruct((lp, D), BF16), jax.ShapeDtypeStruct((lp, 4096), BF16),
                   jax.ShapeDtypeStruct((lp, 4096), BF16), jax.ShapeDtypeStruct((lp, 1280), F32),
                   jax.ShapeDtypeStruct((1, 4096), F32), jax.ShapeDtypeStruct((1, D), F32)],
        name="merge_bwd", compiler_params=_cp())(dx, o2, u, z, gb, wout, wo, gpost)


def _dh(dgl, dzq, dzx, w_in, x, g, dx_in, lp, scatter=None):
    xs, bufs, layer = scatter if scatter else ((), (), None)
    n = len(xs)
    steps = lp // RB
    segs = ((0, CQ0), (CQ0, PV0), (PV0, ZW))

    def body(gl_ref, zq_ref, zx_ref, w_ref, x_ref, g_ref, dxi_ref, *rest):
        dx_ref, dg_ref = rest[2 * n], rest[2 * n + 1]
        i = pl.program_id(0)
        if n:
            s_start, s_finish = _scatter_phases(rest[:n], rest[2 * n + 2:3 * n + 2], *rest[3 * n + 2:], layer)
            pl.when(i == 0)(s_start)
        dh = None
        for a_ref, (lo, hi) in zip((gl_ref, zq_ref, zx_ref), segs):
            part = lax.dot_general(a_ref[...], w_ref[:, lo:hi], (((1,), (1,)), ((), ())),
                                   preferred_element_type=F32)
            dh = part if dh is None else dh + part
        xv = x_ref[...]
        r = lax.rsqrt(jnp.mean(xv * xv, axis=-1, keepdims=True) + EPS)
        a = dh * g_ref[...]
        dx_ref[...] = dxi_ref[...] + r * a - xv * (r * r * r) * jnp.mean(a * xv, axis=-1, keepdims=True)
        part = jnp.sum(dh * xv * r, axis=0, keepdims=True)

        @pl.when(i == 0)
        def _():
            dg_ref[...] = part

        @pl.when(i > 0)
        def _():
            dg_ref[...] += part

        if n:
            pl.when(i == steps - 1)(s_finish)

    anyspec = pl.BlockSpec(memory_space=pl.ANY)
    row = lambda w: pl.BlockSpec((RB, w), lambda i: (i, 0))
    vec = pl.BlockSpec((1, D), lambda i: (0, 0))
    resident = pl.BlockSpec((D, ZW), lambda i: (0, 0), pipeline_mode=pl.Buffered(1))
    return pl.pallas_call(
        body, grid=(steps,),
        in_specs=[row(CQ0), row(PV0 - CQ0), row(ZW - PV0), resident, row(D), vec, row(D)] + [anyspec] * (2 * n),
        out_specs=[row(D), vec] + [anyspec] * n,
        out_shape=[jax.ShapeDtypeStruct((lp, D), F32), jax.ShapeDtypeStruct((1, D), F32)]
        + [jax.ShapeDtypeStruct(b.shape, b.dtype) for b in bufs],
        input_output_aliases={7 + n + a: 2 + a for a in range(n)},
        scratch_shapes=_comm_sems(n) if n else [],
        name="dh_scatter" if n else "dh", compiler_params=_cp())(dgl, dzq, dzx, w_in, x, g, dx_in, *xs, *bufs)


def _loss_head(xf, tgt, n_real, lp):
    def body(x_ref, t_ref, dy_ref, ls_ref):
        i = pl.program_id(0)
        t = i * RB + lax.broadcasted_iota(jnp.int32, (RB, 1), 0)
        real = (t >= N_META) & (t < n_real)
        err = jnp.where(real, x_ref[...] - t_ref[...], 0.0)
        dy_ref[...] = err / D
        part = 0.5 * jnp.sum(jnp.mean(err * err, axis=-1, keepdims=True), axis=0, keepdims=True)
        part = jnp.broadcast_to(part, (8, LANE))

        @pl.when(i == 0)
        def _():
            ls_ref[...] = part

        @pl.when(i > 0)
        def _():
            ls_ref[...] += part

    blk = pl.BlockSpec((RB, D), lambda i: (i, 0))
    return pl.pallas_call(
        body, grid=(lp // RB,), in_specs=[blk, blk],
        out_specs=[blk, pl.BlockSpec((8, LANE), lambda i: (0, 0))],
        out_shape=[jax.ShapeDtypeStruct((lp, D), F32), jax.ShapeDtypeStruct((8, LANE), F32)],
        name="loss_head", compiler_params=_cp())(xf, tgt)


def _peer(d):
    x, y, c = lax.axis_index("x"), lax.axis_index("y"), lax.axis_index("c")
    return (x ^ ((d >> 2) & 1), y ^ ((d >> 1) & 1), c ^ (d & 1))


def _index_of(p):
    return 4 * p[0] + 2 * p[1] + p[2]


def _all_gather(xs, name):
    n = len(xs)

    def body(*refs):
        start, forward, finish = _gather_phases(refs[:n], refs[n:2 * n], *refs[2 * n:])
        start()
        forward()
        finish()

    anyspec = pl.BlockSpec(memory_space=pl.ANY)
    return pl.pallas_call(
        body, in_specs=[anyspec] * n, out_specs=[anyspec] * n,
        out_shape=_gather_shapes(xs), scratch_shapes=_comm_sems(n), name=name)(*xs)


def _gather_shapes(xs):
    return [jax.ShapeDtypeStruct((N_DEV,) + x.shape, x.dtype) for x in xs]


def _comm_sems(n):
    return [pltpu.SemaphoreType.DMA((7 * n,)), pltpu.SemaphoreType.DMA((7 * n,)), pltpu.SemaphoreType.DMA((n,))]


def _gather_phases(x_refs, out_refs, send_sems, recv_sems, local_sems):
    n = len(x_refs)
    chips = [2, 4, 6]

    def copy(a, kk, block, to, src=None):
        slot = out_refs[a].at[_index_of(block)]
        return pltpu.make_async_remote_copy(
            src_ref=slot if src is None else src, dst_ref=slot,
            send_sem=send_sems.at[7 * a + kk], recv_sem=recv_sems.at[7 * a + kk], device_id=to,
            device_id_type=MESH)

    def local(a):
        return pltpu.make_async_copy(x_refs[a], out_refs[a].at[_index_of(_peer(0))], local_sems.at[a])

    def firsts():
        out = []
        for a in range(n):
            out.append(copy(a, 0, _peer(0), _peer(1), src=x_refs[a]))
            out += [copy(a, 1 + j, _peer(0), _peer(d), src=x_refs[a]) for j, d in enumerate(chips)]
        return out

    def passes():
        return [copy(a, 4 + j, _peer(d), _peer(1)) for j, d in enumerate(chips) for a in range(n)]

    def start():
        for a in range(n):
            local(a).start()
        for cp in firsts():
            cp.start()

    def forward():
        for j, d in enumerate(chips):
            for a in range(n):
                copy(a, 1 + j, _peer(d), _peer(0)).wait_recv()
                copy(a, 4 + j, _peer(d), _peer(1)).start()

    def finish():
        for a in range(n):
            copy(a, 0, _peer(1), _peer(0)).wait_recv()
            for j, d in enumerate(chips):
                copy(a, 4 + j, _peer(d | 1), _peer(0)).wait_recv()
        for cp in firsts() + passes():
            cp.wait_send()
        for a in range(n):
            local(a).wait()

    return start, forward, finish


def _scatter_phases(x_refs, out_refs, send_sems, recv_sems, local_sems, layer):
    n = len(x_refs)

    def land(a, dev):
        slot = out_refs[a].at[dev]
        return slot if layer is None else slot.at[layer]

    def local(a):
        my = _index_of(_peer(0))
        return pltpu.make_async_copy(x_refs[a].at[my], land(a, my), local_sems.at[a])

    def copy(a, d):
        my = _index_of(_peer(0))
        return pltpu.make_async_remote_copy(
            src_ref=x_refs[a].at[_index_of(_peer(d))], dst_ref=land(a, my),
            send_sem=send_sems.at[7 * a + d - 1], recv_sem=recv_sems.at[7 * a + d - 1], device_id=_peer(d),
            device_id_type=MESH)

    def arrival(a, d):
        frm = _index_of(_peer(d))
        return pltpu.make_async_remote_copy(
            src_ref=x_refs[a].at[frm], dst_ref=land(a, frm),
            send_sem=send_sems.at[7 * a + d - 1], recv_sem=recv_sems.at[7 * a + d - 1], device_id=_peer(d),
            device_id_type=MESH)

    def start():
        for a in range(n):
            local(a).start()
        for d in range(1, N_DEV):
            for a in range(n):
                copy(a, d).start()

    def finish():
        for d in range(1, N_DEV):
            for a in range(n):
                arrival(a, d).wait_recv()
        for d in range(1, N_DEV):
            for a in range(n):
                copy(a, d).wait_send()
        for a in range(n):
            local(a).wait()

    return start, finish


def _all_to_all(xs, bufs, layer, name):
    n = len(xs)

    def body(*refs):
        start, finish = _scatter_phases(refs[:n], refs[2 * n:3 * n], *refs[3 * n:], layer)
        start()
        finish()

    anyspec = pl.BlockSpec(memory_space=pl.ANY)
    return pl.pallas_call(
        body, in_specs=[anyspec] * (2 * n), out_specs=[anyspec] * n,
        out_shape=[jax.ShapeDtypeStruct(b.shape, b.dtype) for b in bufs],
        input_output_aliases={n + a: a for a in range(n)},
        scratch_shapes=_comm_sems(n), name=name)(*xs, *bufs)


def _adam_math(g, w, m, v):
    c1 = 1.0 - ADAM_B1 ** ADAM_STEP
    c2 = 1.0 - ADAM_B2 ** ADAM_STEP
    mn = ADAM_B1 * m + (1.0 - ADAM_B1) * g
    vn = ADAM_B2 * v + (1.0 - ADAM_B2) * (g * g)
    return -ADAM_LR * ((mn / c1) / (jnp.sqrt(vn / c2) + ADAM_EPS) + ADAM_WD * w), mn, vn


def _reduce_adam(parts, w, m, v, rb, name, row_off=0):
    depth, rows, cols = w.shape
    assert rows % rb == 0 and row_off % rb == 0

    def body(p_ref, w_ref, m_ref, v_ref, g_ref, d_ref, mo_ref, vo_ref):
        g = p_ref[0, 0].astype(F32)
        for j in range(1, N_DEV):
            g = g + p_ref[j, 0].astype(F32)
        g_ref[0] = g
        d_ref[0], mo_ref[0], vo_ref[0] = _adam_math(g, w_ref[0], m_ref[0], v_ref[0])

    blk = pl.BlockSpec((1, rb, cols), lambda l, i: (l, i, 0))
    out = jax.ShapeDtypeStruct(w.shape, F32)
    return pl.pallas_call(
        body, grid=(depth, rows // rb),
        in_specs=[pl.BlockSpec((N_DEV, 1, rb, cols), lambda l, i: (0, l, i + row_off // rb, 0)), blk, blk, blk],
        out_specs=[blk, blk, blk, blk], out_shape=[out, out, out, out],
        name=name, compiler_params=_cp())(parts, w, m, v)


def _reduce_adam_flat(parts, w, m, v, name):
    q_rows = w.shape[0]

    def body(p_ref, w_ref, m_ref, v_ref, g_ref, d_ref, mo_ref, vo_ref):
        g = p_ref[0].astype(F32)
        for j in range(1, N_DEV):
            g = g + p_ref[j].astype(F32)
        g_ref[...] = g
        d_ref[...], mo_ref[...], vo_ref[...] = _adam_math(g, w_ref[...], m_ref[...], v_ref[...])

    blk = pl.BlockSpec((q_rows, LANE), lambda i: (0, 0))
    out = jax.ShapeDtypeStruct((q_rows, LANE), F32)
    return pl.pallas_call(
        body, grid=(1,), in_specs=[pl.BlockSpec((N_DEV, q_rows, LANE), lambda i: (0, 0, 0)), blk, blk, blk],
        out_specs=[blk, blk, blk, blk], out_shape=[out, out, out, out],
        name=name, compiler_params=_cp())(parts, w, m, v)


C128 = (("w_out_pool", 256), ("w_out_mla", 512), ("w_out_conf", 256), ("w_out_sc", 256), ("w_ukv", 128))
C128_ROWS = sum(r for _, r in C128)
TAIL = (("meta_tokens", (N_META, 128)), ("conf_dw_w", (DEPTH, CONF_K, 32)), ("sc_dw_w", (DEPTH, SC_K, 32)))
TAIL_ROWS = sum(int(np.prod(s)) for _, s in TAIL) // LANE
TAIL_PAD = 56
SMALL = (("pre_norm_g", (DEPTH, D)), ("gate_bias", (DEPTH, 4096)), ("pool_w", (DEPTH, 4, 64, 64)),
         ("pool_scale", (DEPTH, 256)), ("q_norm_g", (DEPTH, 256)), ("kv_norm_g", (DEPTH, 128)),
         ("conf_dw_b", (DEPTH, 256)), ("conf_ln_g", (DEPTH, 256)), ("conf_ln_b", (DEPTH, 256)),
         ("post_norm_g", (DEPTH, D)))
SMALL_ROWS = sum(int(np.prod(s)) for _, s in SMALL) // LANE
SMALL_PAD = -(-(SMALL_ROWS + 1) // 8) * 8


def _pack_tail(t):
    parts = [t[n].reshape(-1, LANE) for n, _ in TAIL]
    parts.append(jnp.zeros((TAIL_PAD - TAIL_ROWS, LANE), F32))
    return jnp.concatenate(parts, axis=0)


def _unpack_tail(flat):
    out, off = {}, 0
    for n, s in TAIL:
        rows = int(np.prod(s)) // LANE
        out[n] = flat[off:off + rows].reshape(s)
        off += rows
    return out


def _unpack_tail_full(g):
    out, off = {}, 0
    for n, s in TAIL:
        rows = int(np.prod(s)) // LANE
        blk = jnp.moveaxis(g[:, off:off + rows].reshape((N_DEV,) + s), 0, -2)
        out[n] = blk.reshape(s[:-1] + (N_DEV * s[-1],))
        off += rows
    return out


def _pack_small(t, extra_row):
    parts = [t[n].reshape(-1, LANE) for n, _ in SMALL] + [extra_row]
    parts.append(jnp.zeros((SMALL_PAD - SMALL_ROWS - 1, LANE), F32))
    return jnp.concatenate(parts, axis=0)


def _unpack_small(flat):
    out, off = {}, 0
    for n, s in SMALL:
        rows = int(np.prod(s)) // LANE
        out[n] = flat[off:off + rows].reshape(s)
        off += rows
    return out


def _cols_by_dest(g, width):
    r = g.shape[0]
    return g.reshape(r, N_DEV, width).transpose(1, 0, 2)


def _cols_full(gathered):
    _, r, c = gathered.shape
    return gathered.transpose(1, 0, 2).reshape(r, N_DEV * c)


W_IN_SHARD = 916
PACKED_SEGS = ((3232, 7328), (512, 896), 64, (896, 928), 32, (0, 512), (928, 3232))


def _pack_w_in(g):
    parts = []
    for seg in PACKED_SEGS:
        if isinstance(seg, int):
            parts.append(jnp.zeros((g.shape[1], seg), g.dtype))
            continue
        a, b = seg
        while a < b:
            k = a // W_IN_SHARD
            hi = min(b, W_IN_SHARD * (k + 1))
            parts.append(g[k, :, a - W_IN_SHARD * k:hi - W_IN_SHARD * k])
            a = hi
    return jnp.concatenate(parts, axis=1)


def _w_in_grad_by_dest(gl, mla, mix):
    src = (((0, 512), mix, 0), ((512, 896), mla, 0), ((896, 928), mla, 448), ((928, 3232), mix, 512),
           ((3232, 7328), gl, 0))
    blocks = []
    for k in range(N_DEV):
        lo, hi = W_IN_SHARD * k, W_IN_SHARD * (k + 1)
        parts = []
        for (a, b), arr, off in src:
            s, e = max(a, lo), min(b, hi)
            if s < e:
                parts.append(arr[:, off + s - a:off + e - a])
        blocks.append(jnp.concatenate(parts, axis=1))
    return jnp.stack(blocks)


def _rope_tables(lp):
    inv = 1.0 / (ROPE_THETA ** (jnp.arange(0, QK_ROPE, 2, dtype=F32) / QK_ROPE))
    ang = jnp.arange(lp, dtype=F32)[:, None] * inv[None, :]
    cos, sin = jnp.cos(ang), jnp.sin(ang)
    one = jnp.ones((lp, QK_NOPE), F32)
    zero = jnp.zeros((lp, QK_NOPE), F32)
    z16 = jnp.zeros((lp, 16), F32)
    c = jnp.concatenate([one, cos, cos, jnp.ones((lp, 32), F32)], axis=1)
    s1 = jnp.concatenate([zero, z16, sin, jnp.zeros((lp, 32), F32)], axis=1)
    s2 = jnp.concatenate([zero, -sin, z16, jnp.zeros((lp, 32), F32)], axis=1)
    return c, s1, s2


def kernel(x, meta_tokens, pre_norm_g, w_in, gate_bias, pool_w, pool_scale, w_out_pool, q_norm_g, w_uq, kv_norm_g, w_ukv, w_out_mla, conf_dw_w, conf_dw_b, conf_ln_g, conf_ln_b, w_out_conf, sc_dw_w, w_out_sc, w_o, post_norm_g, loss_target, m_meta_tokens, m_pre_norm_g, m_w_in, m_gate_bias, m_pool_w, m_pool_scale, m_w_out_pool, m_q_norm_g, m_w_uq, m_kv_norm_g, m_w_ukv, m_w_out_mla, m_conf_dw_w, m_conf_dw_b, m_conf_ln_g, m_conf_ln_b, m_w_out_conf, m_sc_dw_w, m_w_out_sc, m_w_o, m_post_norm_g, v_meta_tokens, v_pre_norm_g, v_w_in, v_gate_bias, v_pool_w, v_pool_scale, v_w_out_pool, v_q_norm_g, v_w_uq, v_kv_norm_g, v_w_ukv, v_w_out_mla, v_conf_dw_w, v_conf_dw_b, v_conf_ln_g, v_conf_ln_b, v_w_out_conf, v_sc_dw_w, v_w_out_sc, v_w_o, v_post_norm_g):
    names = ["w_in", "w_uq", "w_o"] + [n for n, _ in C128] + [n for n, _ in TAIL] + [n for n, _ in SMALL]
    loc = locals()
    w = {n: loc[n] for n in names}
    mom = {n: loc["m_" + n] for n in names}
    vel = {n: loc["v_" + n] for n in names}

    seq = x.shape[1]
    n_real = N_META + seq
    lp = -(-n_real // RB) * RB
    tmb = lp // 3
    tabs = _rope_tables(lp)

    c128 = jnp.concatenate([w[n] for n, _ in C128], axis=1)
    def shards_of(i):
        return [w_in[i].astype(BF16), c128[i].astype(BF16), w_uq[i].astype(BF16), w_o[i].astype(BF16)]

    gathered = [_all_gather(shards_of(0), "gather_weights")] + [None] * (DEPTH - 1)
    tail_w = _pack_tail(w)
    tail = _unpack_tail_full(_all_gather([tail_w], "gather_tail")[0])
    eye4 = jnp.eye(4, dtype=F32)
    bd_all = (pool_w[:, :, :, None, :] * eye4[None, :, None, :, None]).reshape(DEPTH, 256, 256).astype(BF16)

    def layer_weights(i):
        g_in, g_c128, g_uq, g_o = gathered[i]
        lw = {}
        lw["w_in"] = _pack_w_in(g_in)
        lw["wc"] = _cols_full(g_c128)
        wuq = _cols_full(g_uq).reshape(Q_RANK, HEADS, 96)
        lw["w_uq"] = jnp.pad(wuq, ((0, 0), (0, 0), (0, 32))).reshape(Q_RANK, HEADS * LANE)
        wukv = lw["wc"][U_OFF[4]:].reshape(KV_RANK, HEADS, 128)
        wk = jnp.pad(wukv[:, :, :QK_NOPE], ((0, 0), (0, 0), (0, 64))).reshape(KV_RANK, HEADS * LANE)
        lw["w_ukv"] = jnp.concatenate([wk, wukv[:, :, QK_NOPE:].reshape(KV_RANK, HEADS * V_DIM)], axis=1)
        lw["w_o"] = g_o.reshape(D, D)
        lw["bd"] = bd_all[i]
        lw["cw"] = jnp.pad(tail["conf_dw_w"][i], ((0, 1), (0, 0)))
        lw["sw"] = jnp.pad(tail["sc_dw_w"][i], ((0, 8 - SC_K), (0, 0)))
        return lw

    meta_full = tail["meta_tokens"]

    pad_rows = lp - n_real
    xr = jnp.concatenate([meta_full, x[0], jnp.zeros((pad_rows, D), F32)], axis=0)
    tgt = jnp.pad(loss_target[0], ((N_META, pad_rows), (0, 0)))
    saved = []
    h = _rms_fwd(xr, pre_norm_g[0:1], lp)
    for i in range(DEPTH):
        lw = layer_weights(i)
        z = _mm(h, lw["w_in"], lp, ZW, D, tm=RB, tn=ZW // 2, tk=D, n_outer=True, name="mm_in")
        qn, kvn, qt, kt, vt = _mla_up(z, q_norm_g[i:i + 1], kv_norm_g[i:i + 1], lw["w_uq"], lw["w_ukv"], tabs, lp)
        res = _attn_fwd(qt, kt, vt, lp, gather=shards_of(i + 1) if i + 1 < DEPTH else ())
        oat, lse = res[0], res[1]
        if i + 1 < DEPTH:
            gathered[i + 1] = res[2:]
        u = _mix_fwd(z, oat, lw["bd"], pool_scale[i:i + 1], lw["cw"], conf_dw_b[i:i + 1], conf_ln_g[i:i + 1],
                     conf_ln_b[i:i + 1], lw["sw"], lp)
        nxt = min(i + 1, DEPTH - 1)
        x_new, m_act, o2, h_next = _merge_fwd(xr, u, z, gate_bias[i:i + 1], lw["wc"], lw["w_o"],
                                              post_norm_g[i:i + 1], pre_norm_g[nxt:nxt + 1], lp)
        saved.append(dict(lw=lw, x=xr, h=h, z=z, qn=qn, kvn=kvn, qt=qt, kt=kt, vt=vt, oat=oat, lse=lse,
                          u=u, m=m_act, o2=o2))
        xr, h = x_new, h_next

    dx, loss_part = _loss_head(xr, tgt, n_real, lp)

    gsm = {n: [None] * DEPTH for n, _ in SMALL}
    g_cw = [None] * DEPTH
    g_sw = [None] * DEPTH
    recv = [lax.empty((N_DEV, DEPTH) + s, BF16) for s in ((D, 916), (C128_ROWS, 128), (Q_RANK, 96), (128, D))]
    pending = None
    for i in reversed(range(DEPTH)):
        s = saved[i]
        lw = s["lw"]
        do2, dgl, dyb, du, dgb, dgpost = _merge_bwd(dx, s["o2"], s["u"], s["z"], gate_bias[i:i + 1],
                                                    lw["wc"], lw["w_o"], post_norm_g[i:i + 1], lp)
        d_wo = _mm(s["m"], do2, D, D, lp, ta=True, tm=512, tn=D, tk=tmb, out_dtype=BF16, name="mm_dwo")
        d_wout = _dwout(s["u"], dyb, lp)
        dzx, doat, dbd, dcw, dsw, dsm = _mix_bwd(
            s["z"], s["oat"], du, lw["bd"], pool_scale[i:i + 1], lw["cw"], conf_dw_b[i:i + 1],
            conf_ln_g[i:i + 1], conf_ln_b[i:i + 1], lw["sw"], lp)
        res = _attn_bwd(s["qt"], s["kt"], s["vt"], s["oat"], doat, s["lse"], lp,
                        scatter=(pending, recv, i + 1) if pending else None)
        dqt, dkt, dvt = res[:3]
        if pending:
            recv = list(res[3:])
        dzq, dqg, dkvg, d_wuq, d_wukv = _mla_up_bwd(s["z"], q_norm_g[i:i + 1], kv_norm_g[i:i + 1], lw["w_uq"],
                                                    lw["w_ukv"], s["qn"], s["kvn"], dqt, dkt, dvt, tabs, lp)
        d_wuq, d_wukv = d_wuq.astype(BF16), d_wukv.astype(BF16)
        d_win = [_mm(s["h"], seg, D, seg.shape[1], lp, ta=True, tm=D, tn=tn, tk=tmb, out_dtype=BF16,
                     name="mm_dwin%d" % k) for k, (seg, tn) in enumerate(((dgl, 1024), (dzq, 512), (dzx, 1408)))]
        d_wuq_o = d_wuq.reshape(Q_RANK, HEADS, LANE)[:, :, :96].reshape(Q_RANK, HEADS * 96)
        d_wukv_o = jnp.concatenate([d_wukv[:, :1024].reshape(KV_RANK, HEADS, LANE)[:, :, :QK_NOPE],
                                    d_wukv[:, 1024:].reshape(KV_RANK, HEADS, V_DIM)], axis=2).reshape(KV_RANK, 1024)
        pending = [
            _w_in_grad_by_dest(*d_win),
            _cols_by_dest(jnp.concatenate([d_wout, d_wukv_o], axis=0), 128),
            _cols_by_dest(d_wuq_o, 96),
            d_wo.reshape(N_DEV, 128, D)]
        res = _dh(dgl, dzq, dzx, lw["w_in"], s["x"], pre_norm_g[i:i + 1], dx, lp,
                  scatter=(pending, recv, 0) if i == 0 else None)
        dx, dgpre = res[0], res[1]
        if i == 0:
            recv = list(res[2:])

        gsm["pre_norm_g"][i] = dgpre[0]
        gsm["gate_bias"][i] = dgb[0]
        gsm["pool_w"][i] = jnp.stack([dbd[64 * g:64 * (g + 1), 64 * g:64 * (g + 1)] for g in range(4)])
        gsm["pool_scale"][i] = dsm[0]
        gsm["conf_dw_b"][i] = dsm[1]
        gsm["conf_ln_g"][i] = dsm[2]
        gsm["conf_ln_b"][i] = dsm[3]
        gsm["q_norm_g"][i] = dqg[0]
        gsm["kv_norm_g"][i] = dkvg[0]
        gsm["post_norm_g"][i] = dgpost[0]
        g_cw[i] = dcw[:CONF_K]
        g_sw[i] = dsw[:SC_K]

    outs = [dict() for _ in range(4)]

    def put(n, res):
        for t, r in zip(outs, res):
            t[n] = r

    put("w_in", _reduce_adam(recv[0], w["w_in"], mom["w_in"], vel["w_in"], 256, "adam_w_in"))
    off = 0
    for n, rows in C128:
        put(n, _reduce_adam(recv[1], w[n], mom[n], vel[n], 128, "adam_" + n, row_off=off))
        off += rows
    put("w_uq", _reduce_adam(recv[2], w["w_uq"], mom["w_uq"], vel["w_uq"], Q_RANK, "adam_w_uq"))
    put("w_o", _reduce_adam(recv[3], w["w_o"], mom["w_o"], vel["w_o"], 128, "adam_w_o"))

    tail_g = {"meta_tokens": _cols_by_dest(dx[:N_META], 128),
              "conf_dw_w": jnp.moveaxis(jnp.stack(g_cw).reshape(DEPTH, CONF_K, N_DEV, 32), 2, 0),
              "sc_dw_w": jnp.moveaxis(jnp.stack(g_sw).reshape(DEPTH, SC_K, N_DEV, 32), 2, 0)}
    tail_bd = jnp.concatenate([tail_g[n].reshape(N_DEV, -1, LANE) for n, _ in TAIL]
                              + [jnp.zeros((N_DEV, TAIL_PAD - TAIL_ROWS, LANE), F32)], axis=1)
    tail_recv = _all_to_all([tail_bd], [lax.empty((N_DEV, TAIL_PAD, LANE), F32)], None, "scatter_tail")[0]
    tail_res = _reduce_adam_flat(tail_recv, tail_w, _pack_tail(mom), _pack_tail(vel), "adam_tail")

    small_g = {n: jnp.stack(gsm[n]) for n, _ in SMALL}
    loss_row = jnp.concatenate([loss_part[0:1, 0:1], jnp.zeros((1, LANE - 1), F32)], axis=1)
    zrow = jnp.zeros((1, LANE), F32)
    parts = _all_gather([_pack_small(small_g, loss_row)], "gather_small_grads")[0]
    small_res = _reduce_adam_flat(parts, _pack_small(w, zrow), _pack_small(mom, zrow), _pack_small(vel, zrow),
                                  "adam_small")
    loss = small_res[0][SMALL_ROWS, 0]
    for t, tf, sf in zip(outs, tail_res, small_res):
        t.update(_unpack_tail(tf))
        t.update(_unpack_small(sf))
    order = ["meta_tokens", "pre_norm_g", "w_in", "gate_bias", "pool_w", "pool_scale", "w_out_pool", "q_norm_g",
             "w_uq", "kv_norm_g", "w_ukv", "w_out_mla", "conf_dw_w", "conf_dw_b", "conf_ln_g", "conf_ln_b",
             "w_out_conf", "sc_dw_w", "w_out_sc", "w_o", "post_norm_g"]
    grad_x = dx[N_META:n_real][None]
    return (loss, grad_x, *[t[n] for t in outs for n in order])
```
